```python
import jax, jax.numpy as jnp
from jax import lax
import numpy as np

D_MODEL = 1024
BATCH = 8
SEQ = 8192
DEPTH = 4

N_MIXERS = 3
EPS = 1e-6
CHUNK = 128
A_WIDTH = 2 * D_MODEL
A_GROUPS = 8
A_GROUP_DIM = A_WIDTH // A_GROUPS
HEAD_DIM = 128
B_HEADS = D_MODEL // HEAD_DIM
B_PATTERNS = ((128, 1), (512, 4), (2048, 16))
N_B_GROUPS = len(B_PATTERNS)
B_WIDTH = B_HEADS * HEAD_DIM
B_IN_WIDTH = 3 * N_B_GROUPS * B_WIDTH + B_WIDTH
ROPE_DIM = HEAD_DIM // 4
ROPE_THETA = 500000.0
POOL_SIZES = (2, 4, 8, 16)
N_POOL = len(POOL_SIZES)
C_WIDTH = 2 * D_MODEL
C_GROUP = C_WIDTH // N_POOL
N_A = (DEPTH + 2) // 3
N_B = (DEPTH + 1) // 3
N_C = DEPTH // 3

kernel_name = "hybrid_gmlp_dilated_attn_pool_interleaved"


def rms_norm(x, g):
    xf = x.astype(jnp.float32)
    y = xf * lax.rsqrt(jnp.mean(xf * xf, axis=-1, keepdims=True) + EPS)
    return (y * g.astype(jnp.float32)).astype(x.dtype)


def rotary_tables(seq_len):
    half = ROPE_DIM // 2
    inv_freq = jnp.power(jnp.float32(ROPE_THETA), -jnp.arange(half, dtype=jnp.float32) / half)
    ang = jnp.arange(seq_len, dtype=jnp.float32)[:, None] * inv_freq[None, :]
    return jnp.cos(ang)[None, :, None, :], jnp.sin(ang)[None, :, None, :]


def apply_partial_rotary(x, cos, sin):
    half = ROPE_DIM // 2
    x1 = x[..., :half].astype(jnp.float32)
    x2 = x[..., half:ROPE_DIM].astype(jnp.float32)
    rot = jnp.concatenate([x1 * cos - x2 * sin, x2 * cos + x1 * sin], axis=-1)
    return jnp.concatenate([rot.astype(x.dtype), x[..., ROPE_DIM:]], axis=-1)


def dilated_window_attention(q, k, v, span, dilation):
    bsz, S, H, hd = q.shape
    blk = span
    L = S // dilation
    nb = -(-L // blk)
    Lp = nb * blk

    def to_blocks(t):
        t = t.reshape(bsz, L, dilation, H, hd).transpose(0, 2, 1, 3, 4)
        t = jnp.pad(t, ((0, 0), (0, 0), (0, Lp - L), (0, 0), (0, 0)))
        return t.reshape(bsz, dilation, nb, blk, H, hd)

    def with_prev(t):
        prev = jnp.pad(t, ((0, 0), (0, 0), (1, 0), (0, 0), (0, 0), (0, 0)))[:, :, :-1]
        return jnp.concatenate([prev, t], axis=3)

    qb = to_blocks(q).astype(jnp.float32)
    kk = with_prev(to_blocks(k)).astype(jnp.float32)
    vv = with_prev(to_blocks(v)).astype(jnp.float32)
    scores = jnp.einsum('brnqhd,brnkhd->brnhqk', qb, kk) * (1.0 / np.sqrt(hd)).astype(np.float32)
    qi = jnp.arange(blk)[:, None]
    ki = jnp.arange(2 * blk)[None, :]
    dist = blk + qi - ki
    band = (dist >= 0) & (dist <= span)
    has_prev = (jnp.arange(nb) > 0)[:, None, None] | (ki >= blk)[None]
    mask = band[None] & has_prev
    scores = jnp.where(mask[None, None, :, None], scores, -jnp.inf)
    lse = jax.nn.logsumexp(scores, axis=-1)
    p = jnp.exp(scores - lse[..., None])
    o = jnp.einsum('brnhqk,brnkhd->brnqhd', p, vv)
    o = o.reshape(bsz, dilation, Lp, H, hd)[:, :, :L].transpose(0, 2, 1, 3, 4).reshape(bsz, S, H, hd)
    lse = lse.transpose(0, 1, 2, 4, 3).reshape(bsz, dilation, Lp, H)[:, :, :L]
    lse = lse.transpose(0, 2, 1, 3).reshape(bsz, S, H)
    return o, lse


def mixer_a(h, w_in, v_gain, w_s, b_s, w_out):
    bsz, S, _ = h.shape
    proj = h @ w_in
    u = proj[..., :A_WIDTH]
    v = rms_norm(proj[..., A_WIDTH:2 * A_WIDTH], v_gain)
    z = proj[..., 2 * A_WIDTH:]
    nc = S // CHUNK
    v = v.reshape(bsz, nc, CHUNK, A_GROUPS, A_GROUP_DIM)
    causal = jnp.tril(jnp.ones((CHUNK, CHUNK), dtype=bool))
    ws = jnp.where(causal[None], w_s, jnp.zeros_like(w_s))
    mixed = jnp.einsum('gij,bcjgd->bcigd', ws, v) + b_s.T[None, None, :, :, None]
    mixed = mixed.reshape(bsz, S, A_WIDTH)
    y = u * mixed * jax.nn.silu(z)
    return y @ w_out


def mixer_b(h, w_in, q_gain, k_gain, w_out):
    bsz, S, _ = h.shape
    proj = h @ w_in
    n_qkv = 3 * N_B_GROUPS * B_WIDTH
    qkv = proj[..., :n_qkv].reshape(bsz, S, 3, N_B_GROUPS, B_HEADS, HEAD_DIM)
    z = proj[..., n_qkv:]
    cos, sin = rotary_tables(S)
    outs, lses = [], []
    for g, (window, dilation) in enumerate(B_PATTERNS):
        q = apply_partial_rotary(rms_norm(qkv[:, :, 0, g], q_gain[g]), cos, sin)
        k = apply_partial_rotary(rms_norm(qkv[:, :, 1, g], k_gain[g]), cos, sin)
        o, lse = dilated_window_attention(q, k, qkv[:, :, 2, g], window // dilation, dilation)
        outs.append(o)
        lses.append(lse)
    wgt = jax.nn.softmax(jnp.stack(lses), axis=0)
    o = jnp.einsum('gbsh,gbshd->bshd', wgt, jnp.stack(outs))
    y = o.reshape(bsz, S, B_WIDTH).astype(h.dtype) * jax.nn.silu(z)
    return y @ w_out


def causal_mean(x, window):
    S = x.shape[1]
    c = jnp.cumsum(x.astype(jnp.float32), axis=1)
    c_prev = jnp.pad(c, ((0, 0), (window, 0), (0, 0)))[:, :S]
    cnt = jnp.minimum(jnp.arange(S) + 1, window).astype(jnp.float32)
    return ((c - c_prev) / cnt[None, :, None]).astype(x.dtype)


def mixer_c(h, w_in, w_grp, scale, w_out):
    bsz, S, _ = h.shape
    proj = h @ w_in
    xc = proj[..., :C_WIDTH].reshape(bsz, S, N_POOL, C_GROUP)
    z = proj[..., C_WIDTH:]
    pooled = jnp.stack([causal_mean(xc[:, :, g], w) for g, w in enumerate(POOL_SIZES)], axis=2)
    mixed = jnp.einsum('bsgc,gcd->bsgd', pooled - xc, w_grp).reshape(bsz, S, C_WIDTH) * scale
    y = mixed * jax.nn.silu(z)
    return y @ w_out


def _fwd_setup_inputs(seed: int = 0) -> dict:
    key = jax.random.key(seed)
    ks = jax.random.split(key, 16)
    f32 = jnp.float32

    def nrm(k, shape, fan_in):
        return jax.random.normal(k, shape, f32) * (fan_in ** -0.5)

    def gain(k, shape):
        return 1.0 + 0.1 * jax.random.normal(k, shape, f32)

    return {
        "x": jax.random.normal(ks[0], (BATCH, SEQ, D_MODEL), f32),
        "norm_gain": gain(ks[1], (DEPTH, D_MODEL)),
        "a_w_in": nrm(ks[2], (N_A, D_MODEL, 3 * A_WIDTH), D_MODEL),
        "a_v_gain": gain(ks[3], (N_A, A_WIDTH)),
        "a_w_s": nrm(ks[4], (N_A, A_GROUPS, CHUNK, CHUNK), CHUNK),
        "a_b_s": gain(ks[5], (N_A, A_GROUPS, CHUNK)),
        "a_w_out": nrm(ks[6], (N_A, A_WIDTH, D_MODEL), A_WIDTH),
        "b_w_in": nrm(ks[7], (N_B, D_MODEL, B_IN_WIDTH), D_MODEL),
        "b_q_gain": gain(ks[8], (N_B, N_B_GROUPS, HEAD_DIM)),
        "b_k_gain": gain(ks[9], (N_B, N_B_GROUPS, HEAD_DIM)),
        "b_w_out": nrm(ks[10], (N_B, B_WIDTH, D_MODEL), B_WIDTH),
        "c_w_in": nrm(ks[11], (N_C, D_MODEL, 2 * C_WIDTH), D_MODEL),
        "c_w_grp": nrm(ks[12], (N_C, N_POOL, C_GROUP, C_GROUP), C_GROUP),
        "c_scale": gain(ks[13], (N_C, C_WIDTH)),
        "c_w_out": nrm(ks[14], (N_C, C_WIDTH, D_MODEL), C_WIDTH),
    }


def _fwd_reference(x, norm_gain, a_w_in, a_v_gain, a_w_s, a_b_s, a_w_out,
              b_w_in, b_q_gain, b_k_gain, b_w_out,
              c_w_in, c_w_grp, c_scale, c_w_out):
    for i in range(DEPTH):
        kind, j = i % N_MIXERS, i // N_MIXERS
        h = rms_norm(x, norm_gain[i])
        if kind == 0:
            y = mixer_a(h, a_w_in[j], a_v_gain[j], a_w_s[j], a_b_s[j], a_w_out[j])
        elif kind == 1:
            y = mixer_b(h, b_w_in[j], b_q_gain[j], b_k_gain[j], b_w_out[j])
        else:
            y = mixer_c(h, c_w_in[j], c_w_grp[j], c_scale[j], c_w_out[j])
        x = x + y.astype(x.dtype)
    return x


import jax as _jax
import jax.numpy as _jnp

TWIN_FORMAT = 'train_step'
FWD_PARAMS = ['x', 'norm_gain', 'a_w_in', 'a_v_gain', 'a_w_s', 'a_b_s', 'a_w_out', 'b_w_in', 'b_q_gain', 'b_k_gain', 'b_w_out', 'c_w_in', 'c_w_grp', 'c_scale', 'c_w_out']
TWIN_WEIGHTS = ['norm_gain', 'a_w_in', 'a_v_gain', 'a_w_s', 'a_b_s', 'a_w_out', 'b_w_in', 'b_q_gain', 'b_k_gain', 'b_w_out', 'c_w_in', 'c_w_grp', 'c_scale', 'c_w_out']
TWIN_DIFF_INPUT = 'x'
TWIN_INPUTS = ['x', 'norm_gain', 'a_w_in', 'a_v_gain', 'a_w_s', 'a_b_s', 'a_w_out', 'b_w_in', 'b_q_gain', 'b_k_gain', 'b_w_out', 'c_w_in', 'c_w_grp', 'c_scale', 'c_w_out', 'loss_target', 'm_norm_gain', 'm_a_w_in', 'm_a_v_gain', 'm_a_w_s', 'm_a_b_s', 'm_a_w_out', 'm_b_w_in', 'm_b_q_gain', 'm_b_k_gain', 'm_b_w_out', 'm_c_w_in', 'm_c_w_grp', 'm_c_scale', 'm_c_w_out', 'v_norm_gain', 'v_a_w_in', 'v_a_v_gain', 'v_a_w_s', 'v_a_b_s', 'v_a_w_out', 'v_b_w_in', 'v_b_q_gain', 'v_b_k_gain', 'v_b_w_out', 'v_c_w_in', 'v_c_w_grp', 'v_c_scale', 'v_c_w_out']
TWIN_OUTPUTS = ['loss', 'grad_x', 'grad_norm_gain', 'grad_a_w_in', 'grad_a_v_gain', 'grad_a_w_s', 'grad_a_b_s', 'grad_a_w_out', 'grad_b_w_in', 'grad_b_q_gain', 'grad_b_k_gain', 'grad_b_w_out', 'grad_c_w_in', 'grad_c_w_grp', 'grad_c_scale', 'grad_c_w_out', 'delta_norm_gain', 'delta_a_w_in', 'delta_a_v_gain', 'delta_a_w_s', 'delta_a_b_s', 'delta_a_w_out', 'delta_b_w_in', 'delta_b_q_gain', 'delta_b_k_gain', 'delta_b_w_out', 'delta_c_w_in', 'delta_c_w_grp', 'delta_c_scale', 'delta_c_w_out', 'new_m_norm_gain', 'new_m_a_w_in', 'new_m_a_v_gain', 'new_m_a_w_s', 'new_m_a_b_s', 'new_m_a_w_out', 'new_m_b_w_in', 'new_m_b_q_gain', 'new_m_b_k_gain', 'new_m_b_w_out', 'new_m_c_w_in', 'new_m_c_w_grp', 'new_m_c_scale', 'new_m_c_w_out', 'new_v_norm_gain', 'new_v_a_w_in', 'new_v_a_v_gain', 'new_v_a_w_s', 'new_v_a_b_s', 'new_v_a_w_out', 'new_v_b_w_in', 'new_v_b_q_gain', 'new_v_b_k_gain', 'new_v_b_w_out', 'new_v_c_w_in', 'new_v_c_w_grp', 'new_v_c_scale', 'new_v_c_w_out']
TWIN_LEAF_KINDS = {'loss': 'loss', 'grad_x': 'grad_x', 'grad_norm_gain': 'grad_w', 'grad_a_w_in': 'grad_w', 'grad_a_v_gain': 'grad_w', 'grad_a_w_s': 'grad_w', 'grad_a_b_s': 'grad_w', 'grad_a_w_out': 'grad_w', 'grad_b_w_in': 'grad_w', 'grad_b_q_gain': 'grad_w', 'grad_b_k_gain': 'grad_w', 'grad_b_w_out': 'grad_w', 'grad_c_w_in': 'grad_w', 'grad_c_w_grp': 'grad_w', 'grad_c_scale': 'grad_w', 'grad_c_w_out': 'grad_w', 'delta_norm_gain': 'delta_w', 'delta_a_w_in': 'delta_w', 'delta_a_v_gain': 'delta_w', 'delta_a_w_s': 'delta_w', 'delta_a_b_s': 'delta_w', 'delta_a_w_out': 'delta_w', 'delta_b_w_in': 'delta_w', 'delta_b_q_gain': 'delta_w', 'delta_b_k_gain': 'delta_w', 'delta_b_w_out': 'delta_w', 'delta_c_w_in': 'delta_w', 'delta_c_w_grp': 'delta_w', 'delta_c_scale': 'delta_w', 'delta_c_w_out': 'delta_w', 'new_m_norm_gain': 'new_m', 'new_m_a_w_in': 'new_m', 'new_m_a_v_gain': 'new_m', 'new_m_a_w_s': 'new_m', 'new_m_a_b_s': 'new_m', 'new_m_a_w_out': 'new_m', 'new_m_b_w_in': 'new_m', 'new_m_b_q_gain': 'new_m', 'new_m_b_k_gain': 'new_m', 'new_m_b_w_out': 'new_m', 'new_m_c_w_in': 'new_m', 'new_m_c_w_grp': 'new_m', 'new_m_c_scale': 'new_m', 'new_m_c_w_out': 'new_m', 'new_v_norm_gain': 'new_v', 'new_v_a_w_in': 'new_v', 'new_v_a_v_gain': 'new_v', 'new_v_a_w_s': 'new_v', 'new_v_a_b_s': 'new_v', 'new_v_a_w_out': 'new_v', 'new_v_b_w_in': 'new_v', 'new_v_b_q_gain': 'new_v', 'new_v_b_k_gain': 'new_v', 'new_v_b_w_out': 'new_v', 'new_v_c_w_in': 'new_v', 'new_v_c_w_grp': 'new_v', 'new_v_c_scale': 'new_v', 'new_v_c_w_out': 'new_v'}


def _forward(args):
    return _fwd_reference(*[args[k] for k in FWD_PARAMS])


def _output_shape():
    def fwd():
        inp = _fwd_setup_inputs(0)
        return _fwd_reference(*[inp[k] for k in FWD_PARAMS])
    out = _jax.eval_shape(fwd)
    return out.shape, out.dtype

N_MICROBATCH = 1
ADAM_LR = 0.001
ADAM_B1 = 0.9
ADAM_B2 = 0.999
ADAM_EPS = 1e-08
ADAM_WD = 0.01
ADAM_STEP = 10
PER_EXAMPLE_BATCH_AXIS = {'x': 0, 'loss_target': 0}
SHARED_INPUTS = []
_WEIGHT_DTYPES = {'norm_gain': _jnp.float32, 'a_w_in': _jnp.float32, 'a_v_gain': _jnp.float32, 'a_w_s': _jnp.float32, 'a_b_s': _jnp.float32, 'a_w_out': _jnp.float32, 'b_w_in': _jnp.float32, 'b_q_gain': _jnp.float32, 'b_k_gain': _jnp.float32, 'b_w_out': _jnp.float32, 'c_w_in': _jnp.float32, 'c_w_grp': _jnp.float32, 'c_scale': _jnp.float32, 'c_w_out': _jnp.float32}
MOMENT_SCALE = {'norm_gain': 5.688030e+01, 'a_w_in': 5.344193e-01, 'a_v_gain': 5.570438e+00, 'a_w_s': 1.035065e+00, 'a_b_s': 2.290702e+01, 'a_w_out': 7.883022e-01, 'b_w_in': 6.191145e-02, 'b_q_gain': 3.596690e-01, 'b_k_gain': 3.614040e-01, 'b_w_out': 1.008057e-01, 'c_w_in': 4.216057e-01, 'c_w_grp': 4.864654e-01, 'c_scale': 9.377559e+00, 'c_w_out': 5.664173e-01}


def _to_microbatches(a, axis):
    t = _jnp.moveaxis(a, axis, 0)
    t = t.reshape((N_MICROBATCH, t.shape[0] // N_MICROBATCH) + t.shape[1:])
    return _jnp.moveaxis(t, 1, axis + 1)


def setup_inputs(seed: int = 0) -> dict:
    inp = _fwd_setup_inputs(seed)
    key = _jax.random.fold_in(_jax.random.key(seed), 7919)
    shape, _ = _output_shape()
    out = dict(inp)
    out["loss_target"] = _jax.random.normal(_jax.random.fold_in(key, 0), shape, _jnp.float32)
    for i, name in enumerate(TWIN_WEIGHTS):
        w = inp[name].astype(_jnp.float32)
        if MOMENT_SCALE is None:
            s = _jnp.sqrt(_jnp.mean(_jnp.square(w)) + 1e-30)
        else:
            s = MOMENT_SCALE[name]
        km, kv = _jax.random.split(_jax.random.fold_in(key, i + 1))
        out[name] = w
        out["m_" + name] = s * _jax.random.normal(km, w.shape, _jnp.float32)
        out["v_" + name] = (s * s) * _jax.random.uniform(kv, w.shape, _jnp.float32, 0.5, 1.5)
    if N_MICROBATCH > 1:
        for name, axis in PER_EXAMPLE_BATCH_AXIS.items():
            out[name] = _to_microbatches(out[name], axis)
    return {'x': out['x'], 'norm_gain': out['norm_gain'], 'a_w_in': out['a_w_in'], 'a_v_gain': out['a_v_gain'], 'a_w_s': out['a_w_s'], 'a_b_s': out['a_b_s'], 'a_w_out': out['a_w_out'], 'b_w_in': out['b_w_in'], 'b_q_gain': out['b_q_gain'], 'b_k_gain': out['b_k_gain'], 'b_w_out': out['b_w_out'], 'c_w_in': out['c_w_in'], 'c_w_grp': out['c_w_grp'], 'c_scale': out['c_scale'], 'c_w_out': out['c_w_out'], 'loss_target': out['loss_target'], 'm_norm_gain': out['m_norm_gain'], 'm_a_w_in': out['m_a_w_in'], 'm_a_v_gain': out['m_a_v_gain'], 'm_a_w_s': out['m_a_w_s'], 'm_a_b_s': out['m_a_b_s'], 'm_a_w_out': out['m_a_w_out'], 'm_b_w_in': out['m_b_w_in'], 'm_b_q_gain': out['m_b_q_gain'], 'm_b_k_gain': out['m_b_k_gain'], 'm_b_w_out': out['m_b_w_out'], 'm_c_w_in': out['m_c_w_in'], 'm_c_w_grp': out['m_c_w_grp'], 'm_c_scale': out['m_c_scale'], 'm_c_w_out': out['m_c_w_out'], 'v_norm_gain': out['v_norm_gain'], 'v_a_w_in': out['v_a_w_in'], 'v_a_v_gain': out['v_a_v_gain'], 'v_a_w_s': out['v_a_w_s'], 'v_a_b_s': out['v_a_b_s'], 'v_a_w_out': out['v_a_w_out'], 'v_b_w_in': out['v_b_w_in'], 'v_b_q_gain': out['v_b_q_gain'], 'v_b_k_gain': out['v_b_k_gain'], 'v_b_w_out': out['v_b_w_out'], 'v_c_w_in': out['v_c_w_in'], 'v_c_w_grp': out['v_c_w_grp'], 'v_c_scale': out['v_c_scale'], 'v_c_w_out': out['v_c_w_out']}


def _loss(weights, diff, rest, loss_target):
    with _jax.named_scope("forward"):
        args = {**rest, TWIN_DIFF_INPUT: diff, **{k: w.astype(_WEIGHT_DTYPES[k]) for k, w in weights.items()}}
        y = _forward(args)
    with _jax.named_scope("loss_head"):
        err = _jnp.square(y.astype(_jnp.float32) - loss_target)
        return 0.5 * _jnp.sum(_jnp.mean(err, axis=-1)) if err.ndim else 0.5 * err


def _adamw(w, g, m, v):
    m = ADAM_B1 * m + (1.0 - ADAM_B1) * g
    v = ADAM_B2 * v + (1.0 - ADAM_B2) * _jnp.square(g)
    m_hat = m / (1.0 - ADAM_B1 ** ADAM_STEP)
    v_hat = v / (1.0 - ADAM_B2 ** ADAM_STEP)
    delta = -ADAM_LR * (m_hat / (_jnp.sqrt(v_hat) + ADAM_EPS) + ADAM_WD * w)
    return delta, m, v


def reference(x, norm_gain, a_w_in, a_v_gain, a_w_s, a_b_s, a_w_out, b_w_in, b_q_gain, b_k_gain, b_w_out, c_w_in, c_w_grp, c_scale, c_w_out, loss_target, m_norm_gain, m_a_w_in, m_a_v_gain, m_a_w_s, m_a_b_s, m_a_w_out, m_b_w_in, m_b_q_gain, m_b_k_gain, m_b_w_out, m_c_w_in, m_c_w_grp, m_c_scale, m_c_w_out, v_norm_gain, v_a_w_in, v_a_v_gain, v_a_w_s, v_a_b_s, v_a_w_out, v_b_w_in, v_b_q_gain, v_b_k_gain, v_b_w_out, v_c_w_in, v_c_w_grp, v_c_scale, v_c_w_out):
    given = dict(x=x, norm_gain=norm_gain, a_w_in=a_w_in, a_v_gain=a_v_gain, a_w_s=a_w_s, a_b_s=a_b_s, a_w_out=a_w_out, b_w_in=b_w_in, b_q_gain=b_q_gain, b_k_gain=b_k_gain, b_w_out=b_w_out, c_w_in=c_w_in, c_w_grp=c_w_grp, c_scale=c_scale, c_w_out=c_w_out, loss_target=loss_target, m_norm_gain=m_norm_gain, m_a_w_in=m_a_w_in, m_a_v_gain=m_a_v_gain, m_a_w_s=m_a_w_s, m_a_b_s=m_a_b_s, m_a_w_out=m_a_w_out, m_b_w_in=m_b_w_in, m_b_q_gain=m_b_q_gain, m_b_k_gain=m_b_k_gain, m_b_w_out=m_b_w_out, m_c_w_in=m_c_w_in, m_c_w_grp=m_c_w_grp, m_c_scale=m_c_scale, m_c_w_out=m_c_w_out, v_norm_gain=v_norm_gain, v_a_w_in=v_a_w_in, v_a_v_gain=v_a_v_gain, v_a_w_s=v_a_w_s, v_a_b_s=v_a_b_s, v_a_w_out=v_a_w_out, v_b_w_in=v_b_w_in, v_b_q_gain=v_b_q_gain, v_b_k_gain=v_b_k_gain, v_b_w_out=v_b_w_out, v_c_w_in=v_c_w_in, v_c_w_grp=v_c_w_grp, v_c_scale=v_c_scale, v_c_w_out=v_c_w_out)
    weights = {n: given[n] for n in TWIN_WEIGHTS}
    shared = {n: given[n] for n in SHARED_INPUTS}
    per_example = {n: given[n] for n in ['x']}
    grad_fn = _jax.value_and_grad(_loss, argnums=(0, 1))

    def one_microbatch(ex, loss_target):
        ex = dict(ex)
        diff = ex.pop(TWIN_DIFF_INPUT)
        return grad_fn(weights, diff, {**shared, **ex}, loss_target)

    if N_MICROBATCH == 1:
        loss, (grad_w, grad_x) = one_microbatch(per_example, given["loss_target"])
    else:
        def body(carry, xs):
            loss_sum, grad_sum = carry
            l_k, (gw_k, gx_k) = one_microbatch(xs[0], xs[1])
            with _jax.named_scope("update"):
                return (loss_sum + l_k, _jax.tree.map(_jnp.add, grad_sum, gw_k)), gx_k

        init = (_jnp.zeros((), _jnp.float32), _jax.tree.map(_jnp.zeros_like, weights))
        (loss, grad_w), grad_x = _jax.lax.scan(body, init, (per_example, given["loss_target"]))
    with _jax.named_scope("update"):
        delta_w, new_m, new_v = {}, {}, {}
        for n in TWIN_WEIGHTS:
            delta_w[n], new_m[n], new_v[n] = _adamw(weights[n], grad_w[n], given["m_" + n], given["v_" + n])
    return (loss, grad_x, *[grad_w[n] for n in TWIN_WEIGHTS], *[delta_w[n] for n in TWIN_WEIGHTS],
            *[new_m[n] for n in TWIN_WEIGHTS], *[new_v[n] for n in TWIN_WEIGHTS])
```

```python
import functools

import numpy as np
import jax
import jax.numpy as jnp
from jax import lax
from jax.experimental import pallas as pl
from jax.experimental.pallas import tpu as pltpu

F32 = jnp.float32
MXU = jnp.bfloat16
ACT = jnp.bfloat16
WIRE = jnp.bfloat16

EPS = 1e-6
CHUNK = 128
A_GROUPS = 8
HEAD_DIM = 128
B_HEADS = 8
B_DILATIONS = (1, 4, 16)
ROPE_DIM = 32
ROPE_THETA = 500000.0
POOL_SIZES = (2, 4, 8, 16)
POOL_HALO = 16
N_DEV = 8
NEG = -1e30

ADAM_LR, ADAM_B1, ADAM_B2, ADAM_EPS, ADAM_WD, ADAM_STEP = 0.001, 0.9, 0.999, 1e-08, 0.01, 10

VMEM_LIMIT = 56 * 1024 * 1024
MESH = pl.DeviceIdType.MESH


def _cp(*sem):
    return pltpu.CompilerParams(dimension_semantics=sem, vmem_limit_bytes=VMEM_LIMIT)


def _sigmoid(z):
    return 1.0 / (1.0 + jnp.exp(-z))


def _dot(a, b):
    return jnp.dot(a.astype(MXU), b.astype(MXU), preferred_element_type=F32)


def _dot_nt(a, b):
    return lax.dot_general(a.astype(MXU), b.astype(MXU), (((1,), (1,)), ((), ())), preferred_element_type=F32)


def _dot_tn(a, b):
    return lax.dot_general(a.astype(MXU), b.astype(MXU), (((0,), (0,)), ((), ())), preferred_element_type=F32)


def _chunk_slot(d):
    return (d % 2) * 4 + d // 2


def _norm_proj(x, gain, w_dm, name):
    M, Dm = x.shape
    nd, _, nl = w_dm.shape
    tm = min(M, 1024)

    def body(x_ref, g_ref, w_ref, h_ref, p_ref):
        @pl.when(pl.program_id(1) == 0)
        def _():
            xv = x_ref[...]
            r = lax.rsqrt(jnp.mean(xv * xv, axis=-1, keepdims=True) + EPS)
            h_ref[...] = (xv * r * g_ref[...]).astype(h_ref.dtype)

        p_ref[...] = _dot(h_ref[...], w_ref[...]).astype(p_ref.dtype)

    return pl.pallas_call(
        body, name=name, grid=(M // tm, nd),
        in_specs=[pl.BlockSpec((tm, Dm), lambda i, j: (i, 0)),
                  pl.BlockSpec((1, Dm), lambda i, j: (0, 0)),
                  pl.BlockSpec((None, Dm, nl), lambda i, j: (j, 0, 0))],
        out_specs=[pl.BlockSpec((tm, Dm), lambda i, j: (i, 0)),
                   pl.BlockSpec((tm, nl), lambda i, j: (i, j))],
        out_shape=[jax.ShapeDtypeStruct((M, Dm), ACT), jax.ShapeDtypeStruct((M, nd * nl), ACT)],
        compiler_params=_cp("parallel", "arbitrary"),
    )(x, gain, w_dm)


def _out_proj(x, y, w, name):
    M, Dm = x.shape
    K = y.shape[1]
    tm = min(M, 512)

    def body(x_ref, y_ref, w_ref, o_ref):
        o_ref[...] = x_ref[...] + _dot(y_ref[...], w_ref[...])

    return pl.pallas_call(
        body, name=name, grid=(M // tm,),
        in_specs=[pl.BlockSpec((tm, Dm), lambda i: (i, 0)),
                  pl.BlockSpec((tm, K), lambda i: (i, 0)),
                  pl.BlockSpec((K, Dm), lambda i: (0, 0))],
        out_specs=pl.BlockSpec((tm, Dm), lambda i: (i, 0)),
        out_shape=jax.ShapeDtypeStruct((M, Dm), F32),
        compiler_params=_cp("parallel"),
    )(x, y, w)


def _loss_head(xf, target):
    M, Dm = xf.shape
    tm = min(M, 512)

    def body(x_ref, t_ref, dx_ref, l_ref):
        @pl.when(pl.program_id(0) == 0)
        def _():
            l_ref[...] = jnp.zeros_like(l_ref)

        err = x_ref[...] - t_ref[...]
        dx_ref[...] = err * (1.0 / Dm)
        l_ref[...] += jnp.sum(err * err) * (0.5 / Dm)

    dx, l = pl.pallas_call(
        body, name="loss_head", grid=(M // tm,),
        in_specs=[pl.BlockSpec((tm, Dm), lambda i: (i, 0))] * 2,
        out_specs=[pl.BlockSpec((tm, Dm), lambda i: (i, 0)), pl.BlockSpec((8, 128), lambda i: (0, 0))],
        out_shape=[jax.ShapeDtypeStruct((M, Dm), F32), jax.ShapeDtypeStruct((8, 128), F32)],
        compiler_params=_cp("arbitrary"),
    )(xf, target)
    return l[0, 0], dx


def _dw_in(h, dproj, prev, layer, n_layers, name):
    M, Dm = h.shape
    nl = dproj.shape[1] // N_DEV
    tt = min(M, 512)

    def body(a_ref, b_ref, *rest):
        o_ref = rest[-1]

        @pl.when(pl.program_id(1) == 0)
        def _():
            o_ref[...] = jnp.zeros_like(o_ref)

        o_ref[...] += _dot_tn(a_ref[...], b_ref[...])

    in_specs = [pl.BlockSpec((tt, Dm), lambda j, t: (t, 0)), pl.BlockSpec((tt, nl), lambda j, t: (t, j))]
    args = [h, dproj]
    aliases = {}
    if prev is not None:
        in_specs.append(pl.BlockSpec(memory_space=pl.ANY))
        args.append(prev)
        aliases = {2: 0}
    return pl.pallas_call(
        body, name=name, grid=(N_DEV, M // tt),
        in_specs=in_specs,
        out_specs=pl.BlockSpec((None, None, Dm, nl), lambda j, t: (_chunk_slot(j), layer, 0, 0)),
        out_shape=jax.ShapeDtypeStruct((N_DEV, n_layers, Dm, nl), F32),
        input_output_aliases=aliases,
        compiler_params=_cp("parallel", "arbitrary"),
    )(*args)


def _dw_out(y, dout, prev, layer, n_layers, name):
    M, K = y.shape
    Dm = dout.shape[1]
    kl = K // N_DEV
    tt = min(M, 1024)

    def body(a_ref, b_ref, *rest):
        o_ref = rest[-1]

        @pl.when(pl.program_id(1) == 0)
        def _():
            o_ref[...] = jnp.zeros_like(o_ref)

        o_ref[...] += _dot_tn(a_ref[...], b_ref[...])

    in_specs = [pl.BlockSpec((tt, kl), lambda j, t: (t, j)), pl.BlockSpec((tt, Dm), lambda j, t: (t, 0))]
    args = [y, dout]
    aliases = {}
    if prev is not None:
        in_specs.append(pl.BlockSpec(memory_space=pl.ANY))
        args.append(prev)
        aliases = {2: 0}
    return pl.pallas_call(
        body, name=name, grid=(N_DEV, M // tt),
        in_specs=in_specs,
        out_specs=pl.BlockSpec((None, None, kl, Dm), lambda j, t: (_chunk_slot(j), layer, 0, 0)),
        out_shape=jax.ShapeDtypeStruct((N_DEV, n_layers, kl, Dm), F32),
        input_output_aliases=aliases,
        compiler_params=_cp("parallel", "arbitrary"),
    )(*args)


def _dh_norm_bwd(dproj, w_dm, x, gain, dres, name):
    M, Dm = x.shape
    nd, _, nl = w_dm.shape
    tm = min(M, 1024)

    def body(dp_ref, w_ref, x_ref, g_ref, dr_ref, dx_ref, dg_ref, acc_ref):
        i, j = pl.program_id(0), pl.program_id(1)

        @pl.when(j == 0)
        def _():
            acc_ref[...] = jnp.zeros_like(acc_ref)

        acc_ref[...] += _dot_nt(dp_ref[...], w_ref[...])

        @pl.when(j == nd - 1)
        def _():
            @pl.when(i == 0)
            def _():
                dg_ref[...] = jnp.zeros_like(dg_ref)

            dh = acc_ref[...]
            xv = x_ref[...]
            r = lax.rsqrt(jnp.mean(xv * xv, axis=-1, keepdims=True) + EPS)
            xn = xv * r
            dg_ref[...] += jnp.sum(dh * xn, axis=0, keepdims=True)
            dxn = dh * g_ref[...]
            dx_ref[...] = dr_ref[...] + r * (dxn - xn * jnp.mean(dxn * xn, axis=-1, keepdims=True))

    return pl.pallas_call(
        body, name=name, grid=(M // tm, nd),
        in_specs=[pl.BlockSpec((tm, nl), lambda i, j: (i, j)),
                  pl.BlockSpec((None, Dm, nl), lambda i, j: (j, 0, 0)),
                  pl.BlockSpec((tm, Dm), lambda i, j: (i, 0)),
                  pl.BlockSpec((1, Dm), lambda i, j: (0, 0)),
                  pl.BlockSpec((tm, Dm), lambda i, j: (i, 0))],
        out_specs=[pl.BlockSpec((tm, Dm), lambda i, j: (i, 0)), pl.BlockSpec((1, Dm), lambda i, j: (0, 0))],
        out_shape=[jax.ShapeDtypeStruct((M, Dm), F32), jax.ShapeDtypeStruct((1, Dm), F32)],
        scratch_shapes=[pltpu.VMEM((tm, Dm), F32)],
        compiler_params=_cp("arbitrary", "arbitrary"),
    )(dproj, w_dm, x, gain, dres)


def _tril_mask():
    return lax.broadcasted_iota(jnp.int32, (CHUNK, CHUNK), 0) >= lax.broadcasted_iota(jnp.int32, (CHUNK, CHUNK), 1)


def _a_mid(proj, v_gain, w_s, b_st, name):
    M = proj.shape[0]
    W = proj.shape[1] // 3
    gd = W // A_GROUPS
    tm = min(M, 256)

    def body(p_ref, vg_ref, ws_ref, bs_ref, y_ref):
        pv = p_ref[:, W:2 * W].astype(F32)
        r = lax.rsqrt(jnp.mean(pv * pv, axis=-1, keepdims=True) + EPS)
        v = (pv * r * vg_ref[...]).astype(MXU)
        tri = _tril_mask()
        for g in range(A_GROUPS):
            wg = jnp.where(tri, ws_ref[g], 0.0).astype(MXU)
            bcol = bs_ref[:, g:g + 1]
            for c in range(tm // CHUNK):
                rows, cols = slice(c * CHUNK, (c + 1) * CHUNK), slice(g * gd, (g + 1) * gd)
                mixed = jnp.dot(wg, v[rows, cols], preferred_element_type=F32) + bcol
                u = p_ref[rows, g * gd:(g + 1) * gd].astype(F32)
                z = p_ref[rows, 2 * W + g * gd:2 * W + (g + 1) * gd].astype(F32)
                y_ref[rows, cols] = (u * mixed * (z * _sigmoid(z))).astype(y_ref.dtype)

    return pl.pallas_call(
        body, name=name, grid=(M // tm,),
        in_specs=[pl.BlockSpec((tm, 3 * W), lambda i: (i, 0)),
                  pl.BlockSpec((1, W), lambda i: (0, 0)),
                  pl.BlockSpec((A_GROUPS, CHUNK, CHUNK), lambda i: (0, 0, 0)),
                  pl.BlockSpec((CHUNK, A_GROUPS), lambda i: (0, 0))],
        out_specs=pl.BlockSpec((tm, W), lambda i: (i, 0)),
        out_shape=jax.ShapeDtypeStruct((M, W), ACT),
        compiler_params=_cp("parallel"),
    )(proj, v_gain, w_s, b_st)


def _a_bwd(dout, w_out, proj, v_gain, w_s, b_st, name):
    M = proj.shape[0]
    W = proj.shape[1] // 3
    Dm = dout.shape[1]
    gd = W // A_GROUPS
    tm = min(M, 256)
    nt = M // tm

    def body(do_ref, wo_ref, p_ref, vg_ref, ws_ref, bs_ref, dp_ref, dws_ref, dbs_ref, dvg_ref, dv_s):
        i = pl.program_id(0)

        @pl.when(i == 0)
        def _():
            dws_ref[...] = jnp.zeros_like(dws_ref)
            dbs_ref[...] = jnp.zeros_like(dbs_ref)
            dvg_ref[...] = jnp.zeros_like(dvg_ref)

        dy = _dot_nt(do_ref[...], wo_ref[...])
        pv = p_ref[:, W:2 * W].astype(F32)
        r = lax.rsqrt(jnp.mean(pv * pv, axis=-1, keepdims=True) + EPS)
        pvn = pv * r
        vg = vg_ref[...]
        v = (pvn * vg).astype(MXU)
        tri = _tril_mask()
        for g in range(A_GROUPS):
            wf = jnp.where(tri, ws_ref[g], 0.0)
            wg = wf.astype(MXU)
            wgt = wf.T.astype(MXU)
            bcol = bs_ref[:, g:g + 1]
            for c in range(tm // CHUNK):
                rows, cols = slice(c * CHUNK, (c + 1) * CHUNK), slice(g * gd, (g + 1) * gd)
                vb = v[rows, cols]
                mixed = jnp.dot(wg, vb, preferred_element_type=F32) + bcol
                u = p_ref[rows, g * gd:(g + 1) * gd].astype(F32)
                z = p_ref[rows, 2 * W + g * gd:2 * W + (g + 1) * gd].astype(F32)
                sig = _sigmoid(z)
                sz = z * sig
                dyb = dy[rows, cols]
                dp_ref[rows, g * gd:(g + 1) * gd] = (dyb * mixed * sz).astype(dp_ref.dtype)
                dp_ref[rows, 2 * W + g * gd:2 * W + (g + 1) * gd] = (
                    dyb * u * mixed * (sig * (1.0 + z * (1.0 - sig)))).astype(dp_ref.dtype)
                dmix = dyb * u * sz
                dws_ref[g] += _dot_nt(dmix, vb)
                dbs_ref[:, g:g + 1] += jnp.sum(dmix, axis=1, keepdims=True)
                dv_s[rows, cols] = jnp.dot(wgt, dmix.astype(MXU), preferred_element_type=F32)
        dv = dv_s[...]
        dvg_ref[...] += jnp.sum(dv * pvn, axis=0, keepdims=True)
        dpvn = dv * vg
        dp_ref[:, W:2 * W] = (r * (dpvn - pvn * jnp.mean(dpvn * pvn, axis=-1, keepdims=True))).astype(dp_ref.dtype)

        @pl.when(i == nt - 1)
        def _():
            for g in range(A_GROUPS):
                dws_ref[g] = jnp.where(tri, dws_ref[g], 0.0)

    return pl.pallas_call(
        body, name=name, grid=(nt,),
        in_specs=[pl.BlockSpec((tm, Dm), lambda i: (i, 0)),
                  pl.BlockSpec((W, Dm), lambda i: (0, 0)),
                  pl.BlockSpec((tm, 3 * W), lambda i: (i, 0)),
                  pl.BlockSpec((1, W), lambda i: (0, 0)),
                  pl.BlockSpec((A_GROUPS, CHUNK, CHUNK), lambda i: (0, 0, 0)),
                  pl.BlockSpec((CHUNK, A_GROUPS), lambda i: (0, 0))],
        out_specs=[pl.BlockSpec((tm, 3 * W), lambda i: (i, 0)),
                   pl.BlockSpec((A_GROUPS, CHUNK, CHUNK), lambda i: (0, 0, 0)),
                   pl.BlockSpec((CHUNK, A_GROUPS), lambda i: (0, 0)),
                   pl.BlockSpec((1, W), lambda i: (0, 0))],
        out_shape=[jax.ShapeDtypeStruct((M, 3 * W), ACT),
                   jax.ShapeDtypeStruct((A_GROUPS, CHUNK, CHUNK), F32),
                   jax.ShapeDtypeStruct((CHUNK, A_GROUPS), F32),
                   jax.ShapeDtypeStruct((1, W), F32)],
        scratch_shapes=[pltpu.VMEM((tm, W), F32)],
        compiler_params=_cp("arbitrary"),
    )(dout, w_out, proj, v_gain, w_s, b_st)


def _pool_diff(xg, tail, i, tm, w):
    t = lax.broadcasted_iota(jnp.int32, (tm, tm + POOL_HALO), 0)
    s = lax.broadcasted_iota(jnp.int32, (tm, tm + POOL_HALO), 1)
    off = t - (s - POOL_HALO)
    band = jnp.where((off >= 0) & (off < w), 1.0, 0.0).astype(MXU)
    tail = jnp.where(i > 0, tail, jnp.zeros_like(tail))
    ext = jnp.concatenate([tail, xg], axis=0)
    ssum = jnp.dot(band, ext.astype(MXU), preferred_element_type=F32)
    tglob = i * tm + lax.broadcasted_iota(jnp.int32, (tm, 1), 0)
    cnt = jnp.minimum(tglob + 1, w).astype(F32)
    return ssum / cnt - xg.astype(F32)


def _c_mid(proj, w_grp, scale, name):
    M = proj.shape[0]
    W = proj.shape[1] // 2
    ng = len(POOL_SIZES)
    cg = W // ng
    tm = min(M, 256)
    hb = tm // POOL_HALO

    def body(xc_ref, tail_ref, z_ref, wg_ref, sc_ref, y_ref):
        i = pl.program_id(0)
        for g, w in enumerate(POOL_SIZES):
            cols = slice(g * cg, (g + 1) * cg)
            d = _pool_diff(xc_ref[:, cols], tail_ref[:, cols], i, tm, w)
            mixed = _dot(d, wg_ref[g]) * sc_ref[:, cols]
            z = z_ref[:, cols].astype(F32)
            y_ref[:, cols] = (mixed * (z * _sigmoid(z))).astype(y_ref.dtype)

    return pl.pallas_call(
        body, name=name, grid=(M // tm,),
        in_specs=[pl.BlockSpec((tm, W), lambda i: (i, 0)),
                  pl.BlockSpec((POOL_HALO, W), lambda i: (jnp.maximum(i * hb - 1, 0), 0)),
                  pl.BlockSpec((tm, W), lambda i: (i, 1)),
                  pl.BlockSpec((ng, cg, cg), lambda i: (0, 0, 0)),
                  pl.BlockSpec((1, W), lambda i: (0, 0))],
        out_specs=pl.BlockSpec((tm, W), lambda i: (i, 0)),
        out_shape=jax.ShapeDtypeStruct((M, W), ACT),
        compiler_params=_cp("parallel"),
    )(proj, proj, proj, w_grp, scale)


def _c_bwd1(dout, w_out, proj, w_grp, scale, name):
    M = proj.shape[0]
    W = proj.shape[1] // 2
    Dm = dout.shape[1]
    ng = len(POOL_SIZES)
    cg = W // ng
    rl = cg // N_DEV
    tm = min(M, 256)
    hb = tm // POOL_HALO
    nt = M // tm

    def body(do_ref, wo_ref, xc_ref, tail_ref, z_ref, wg_ref, sc_ref, dd_ref, dz_ref, dwg_ref, dsc_ref, acc_ref):
        i = pl.program_id(0)

        @pl.when(i == 0)
        def _():
            acc_ref[...] = jnp.zeros_like(acc_ref)
            dsc_ref[...] = jnp.zeros_like(dsc_ref)

        dy = _dot_nt(do_ref[...], wo_ref[...])
        for g, w in enumerate(POOL_SIZES):
            cols = slice(g * cg, (g + 1) * cg)
            d = _pool_diff(xc_ref[:, cols], tail_ref[:, cols], i, tm, w)
            mr = _dot(d, wg_ref[g])
            sc = sc_ref[:, cols]
            z = z_ref[:, cols].astype(F32)
            sig = _sigmoid(z)
            dyg = dy[:, cols]
            dmixed = dyg * (z * sig)
            dz_ref[:, cols] = (dyg * (mr * sc) * (sig * (1.0 + z * (1.0 - sig)))).astype(dz_ref.dtype)
            dsc_ref[:, cols] += jnp.sum(dmixed * mr, axis=0, keepdims=True)
            dmr = (dmixed * sc).astype(MXU)
            acc_ref[g] += _dot_tn(d, dmr)
            dd_ref[:, cols] = _dot_nt(dmr, wg_ref[g]).astype(dd_ref.dtype)

        @pl.when(i == nt - 1)
        def _():
            for dev in range(N_DEV):
                for g in range(ng):
                    dwg_ref[_chunk_slot(dev), g] = acc_ref[g, dev * rl:(dev + 1) * rl, :]

    return pl.pallas_call(
        body, name=name, grid=(nt,),
        in_specs=[pl.BlockSpec((tm, Dm), lambda i: (i, 0)),
                  pl.BlockSpec((W, Dm), lambda i: (0, 0)),
                  pl.BlockSpec((tm, W), lambda i: (i, 0)),
                  pl.BlockSpec((POOL_HALO, W), lambda i: (jnp.maximum(i * hb - 1, 0), 0)),
                  pl.BlockSpec((tm, W), lambda i: (i, 1)),
                  pl.BlockSpec((ng, cg, cg), lambda i: (0, 0, 0)),
                  pl.BlockSpec((1, W), lambda i: (0, 0))],
        out_specs=[pl.BlockSpec((tm, W), lambda i: (i, 0)),
                   pl.BlockSpec((tm, W), lambda i: (i, 0)),
                   pl.BlockSpec((N_DEV, ng, rl, cg), lambda i: (0, 0, 0, 0)),
                   pl.BlockSpec((1, W), lambda i: (0, 0))],
        out_shape=[jax.ShapeDtypeStruct((M, W), ACT), jax.ShapeDtypeStruct((M, W), ACT),
                   jax.ShapeDtypeStruct((N_DEV, ng, rl, cg), F32), jax.ShapeDtypeStruct((1, W), F32)],
        scratch_shapes=[pltpu.VMEM((ng, cg, cg), F32)],
        compiler_params=_cp("arbitrary"),
    )(dout, w_out, proj, proj, proj, w_grp, scale)


def _c_bwd2(dd, dz, name):
    M, W = dd.shape
    ng = len(POOL_SIZES)
    cg = W // ng
    tm = min(M, 256)
    hb = tm // POOL_HALO
    nt = M // tm

    def body(dd_ref, head_ref, dz_ref, dp_ref):
        i = pl.program_id(0)
        s = lax.broadcasted_iota(jnp.int32, (tm, tm + POOL_HALO), 0)
        t = lax.broadcasted_iota(jnp.int32, (tm, tm + POOL_HALO), 1)
        off = t - s
        tglob = i * tm + lax.broadcasted_iota(jnp.int32, (tm + POOL_HALO, 1), 0)
        for g, w in enumerate(POOL_SIZES):
            cols = slice(g * cg, (g + 1) * cg)
            ddg = dd_ref[:, cols].astype(F32)
            head = head_ref[:, cols].astype(F32)
            head = jnp.where(i < nt - 1, head, jnp.zeros_like(head))
            cnt = jnp.minimum(tglob + 1, w).astype(F32)
            ext = (jnp.concatenate([ddg, head], axis=0) / cnt).astype(MXU)
            band = jnp.where((off >= 0) & (off < w), 1.0, 0.0).astype(MXU)
            dp_ref[:, cols] = (jnp.dot(band, ext, preferred_element_type=F32) - ddg).astype(dp_ref.dtype)
        dp_ref[:, W:] = dz_ref[...]

    return pl.pallas_call(
        body, name=name, grid=(nt,),
        in_specs=[pl.BlockSpec((tm, W), lambda i: (i, 0)),
                  pl.BlockSpec((POOL_HALO, W), lambda i: (jnp.minimum((i + 1) * hb, M // POOL_HALO - 1), 0)),
                  pl.BlockSpec((tm, W), lambda i: (i, 0))],
        out_specs=pl.BlockSpec((tm, 2 * W), lambda i: (i, 0)),
        out_shape=jax.ShapeDtypeStruct((M, 2 * W), ACT),
        compiler_params=_cp("parallel"),
    )(dd, dd, dz)


def _rope_tables(S):
    half = ROPE_DIM // 2
    inv_freq = jnp.power(jnp.float32(ROPE_THETA), -jnp.arange(half, dtype=F32) / half)
    ang = jnp.arange(S, dtype=F32)[:, None] * inv_freq[None, :]
    cos, sin = jnp.cos(ang), jnp.sin(ang)
    rest = HEAD_DIM - ROPE_DIM
    cf = jnp.concatenate([cos, cos, jnp.ones((S, rest), F32)], axis=1)
    sa = jnp.concatenate([-sin, jnp.zeros((S, HEAD_DIM - half), F32)], axis=1)
    sb = jnp.concatenate([jnp.zeros((S, half), F32), sin, jnp.zeros((S, rest), F32)], axis=1)
    return cf, sa, sb


def _b_qk_fwd(proj, tables, gains, name):
    M = proj.shape[0]
    nsl = 2 * len(B_DILATIONS) * B_HEADS
    Wqk = nsl * HEAD_DIM
    tm = min(M, 256)
    half = ROPE_DIM // 2

    def body(p_ref, cf_ref, sa_ref, sb_ref, g_ref, o_ref):
        cf, sa, sb = cf_ref[...], sa_ref[...], sb_ref[...]
        for j in range(nsl):
            cols = slice(j * HEAD_DIM, (j + 1) * HEAD_DIM)
            xv = p_ref[:, cols].astype(F32)
            r = lax.rsqrt(jnp.mean(xv * xv, axis=-1, keepdims=True) + EPS)
            xn = xv * r * g_ref[j // B_HEADS:j // B_HEADS + 1, :]
            rot = xn * cf + pltpu.roll(xn, HEAD_DIM - half, 1) * sa + pltpu.roll(xn, half, 1) * sb
            o_ref[:, cols] = rot.astype(o_ref.dtype)

    tspec = pl.BlockSpec((tm, HEAD_DIM), lambda i: (i, 0))
    return pl.pallas_call(
        body, name=name, grid=(M // tm,),
        in_specs=[pl.BlockSpec((tm, Wqk), lambda i: (i, 0)), tspec, tspec, tspec,
                  pl.BlockSpec((8, HEAD_DIM), lambda i: (0, 0))],
        out_specs=pl.BlockSpec((tm, Wqk), lambda i: (i, 0)),
        out_shape=jax.ShapeDtypeStruct((M, Wqk), ACT),
        compiler_params=_cp("parallel"),
    )(proj, *tables, gains)


def _b_qk_bwd(dqs, dks, proj, tables, gains, dproj, name):
    M = proj.shape[0]
    ngr = len(B_DILATIONS)
    nsl = 2 * ngr * B_HEADS
    Wqk = nsl * HEAD_DIM
    Wg = B_HEADS * HEAD_DIM
    tm = min(M, 256)
    half = ROPE_DIM // 2

    def body(*refs):
        d_refs = refs[:2 * ngr]
        p_ref, cf_ref, sa_ref, sb_ref, g_ref = refs[2 * ngr:2 * ngr + 5]
        dp_ref, dg_ref = refs[-2], refs[-1]

        @pl.when(pl.program_id(0) == 0)
        def _():
            dg_ref[...] = jnp.zeros_like(dg_ref)

        cf, sa, sb = cf_ref[...], sa_ref[...], sb_ref[...]
        for j in range(nsl):
            t, hh = j // B_HEADS, j % B_HEADS
            cols = slice(j * HEAD_DIM, (j + 1) * HEAD_DIM)
            dy = d_refs[t][:, hh * HEAD_DIM:(hh + 1) * HEAD_DIM].astype(F32)
            dxn = dy * cf + pltpu.roll(dy * sa, half, 1) + pltpu.roll(dy * sb, HEAD_DIM - half, 1)
            xv = p_ref[:, cols].astype(F32)
            r = lax.rsqrt(jnp.mean(xv * xv, axis=-1, keepdims=True) + EPS)
            xh = xv * r
            dg_ref[t:t + 1, :] += jnp.sum(dxn * xh, axis=0, keepdims=True)
            dxh = dxn * g_ref[t:t + 1, :]
            dp_ref[:, cols] = (r * (dxh - xh * jnp.mean(dxh * xh, axis=-1, keepdims=True))).astype(dp_ref.dtype)

    tspec = pl.BlockSpec((tm, HEAD_DIM), lambda i: (i, 0))
    dspec = pl.BlockSpec((tm, Wg), lambda i: (i, 0))
    n_in = 2 * ngr + 6
    return pl.pallas_call(
        body, name=name, grid=(M // tm,),
        in_specs=[dspec] * (2 * ngr) + [pl.BlockSpec((tm, Wqk), lambda i: (i, 0)), tspec, tspec, tspec,
                                        pl.BlockSpec((8, HEAD_DIM), lambda i: (0, 0)),
                                        pl.BlockSpec(memory_space=pl.ANY)],
        out_specs=[pl.BlockSpec((tm, Wqk), lambda i: (i, 0)), pl.BlockSpec((8, HEAD_DIM), lambda i: (0, 0))],
        out_shape=[jax.ShapeDtypeStruct(dproj.shape, dproj.dtype), jax.ShapeDtypeStruct((8, HEAD_DIM), F32)],
        input_output_aliases={n_in - 1: 0},
        compiler_params=_cp("arbitrary"),
    )(*dqs, *dks, proj, *tables, gains, dproj)


def _attn_tile(D):
    return max(HEAD_DIM * D, 512)


def _strided(start, size, D):
    return pl.ds(start, size) if D == 1 else pl.ds(start, size, stride=D)


def _attn_mask(base):
    qi = lax.broadcasted_iota(jnp.int32, (CHUNK, 2 * CHUNK), 0)
    ki = lax.broadcasted_iota(jnp.int32, (CHUNK, 2 * CHUNK), 1)
    return (ki >= qi) & (ki <= qi + CHUNK) & (ki >= CHUNK - base)


def _b_attn_fwd(qk, proj, g, name):
    M = qk.shape[0]
    D = B_DILATIONS[g]
    ngr = len(B_DILATIONS)
    T = _attn_tile(D)
    P = HEAD_DIM * D
    nsb = T // P
    Wg = B_HEADS * HEAD_DIM
    scale = np.float32(1.0 / np.sqrt(HEAD_DIM))

    def body(q_ref, k_ref, v_ref, o_ref, l_ref, qs, ks, vs, os_, ls):
        n = pl.program_id(1)

        @pl.when(n == 0)
        def _():
            ks[0:P, :] = jnp.zeros((P, HEAD_DIM), F32)
            vs[0:P, :] = jnp.zeros((P, HEAD_DIM), F32)

        qs[...] = q_ref[...].astype(F32)
        ks[P:P + T, :] = k_ref[...].astype(F32)
        vs[P:P + T, :] = v_ref[...].astype(F32)

        def unit(u, carry):
            b, r = u // D, u % D
            start = b * P + r
            q = qs[_strided(start, CHUNK, D), :]
            k = ks[_strided(start, 2 * CHUNK, D), :]
            v = vs[_strided(start, 2 * CHUNK, D), :]
            s = _dot_nt(q, k) * scale
            s = jnp.where(_attn_mask(n * (T // D) + b * CHUNK), s, NEG)
            m = jnp.max(s, axis=-1, keepdims=True)
            p = jnp.exp(s - m)
            l = jnp.sum(p, axis=-1, keepdims=True)
            o = _dot(p, v) / l
            os_[_strided(start, CHUNK, D), :] = o
            ls[_strided(start, CHUNK, D), :] = jnp.broadcast_to(m + jnp.log(l), (CHUNK, HEAD_DIM))
            return carry

        lax.fori_loop(0, nsb * D, unit, 0)
        o_ref[...] = os_[...].astype(o_ref.dtype)
        l_ref[...] = ls[...]
        ks[0:P, :] = ks[T:T + P, :]
        vs[0:P, :] = vs[T:T + P, :]

    blk = (T, HEAD_DIM)
    return pl.pallas_call(
        body, name=name, grid=(B_HEADS, M // T),
        in_specs=[pl.BlockSpec(blk, lambda h, n: (n, g * B_HEADS + h)),
                  pl.BlockSpec(blk, lambda h, n: (n, (ngr + g) * B_HEADS + h)),
                  pl.BlockSpec(blk, lambda h, n: (n, (2 * ngr + g) * B_HEADS + h))],
        out_specs=[pl.BlockSpec(blk, lambda h, n: (n, h)), pl.BlockSpec(blk, lambda h, n: (n, h))],
        out_shape=[jax.ShapeDtypeStruct((M, Wg), ACT), jax.ShapeDtypeStruct((M, Wg), F32)],
        scratch_shapes=[pltpu.VMEM((T, HEAD_DIM), F32), pltpu.VMEM((P + T, HEAD_DIM), F32),
                        pltpu.VMEM((P + T, HEAD_DIM), F32), pltpu.VMEM((T, HEAD_DIM), F32),
                        pltpu.VMEM((T, HEAD_DIM), F32)],
        compiler_params=_cp("parallel", "arbitrary"),
    )(qk, qk, proj)


def _b_combine(os_, ls, proj, name):
    M, Wg = os_[0].shape
    ngr = len(B_DILATIONS)
    tm = min(M, 512)

    def body(*refs):
        o_refs, l_refs, z_ref = refs[:ngr], refs[ngr:2 * ngr], refs[2 * ngr]
        y_ref, o_ref, lse_ref = refs[2 * ngr + 1:]
        ls_ = [r[...] for r in l_refs]
        m = functools.reduce(jnp.maximum, ls_)
        es = [jnp.exp(l - m) for l in ls_]
        tot = functools.reduce(lambda a, b: a + b, es)
        o = functools.reduce(lambda a, b: a + b, [e * r[...].astype(F32) for e, r in zip(es, o_refs)]) / tot
        z = z_ref[...].astype(F32)
        y_ref[...] = (o.astype(F32) * (z * _sigmoid(z))).astype(y_ref.dtype)
        o_ref[...] = o.astype(o_ref.dtype)
        lse_ref[...] = m + jnp.log(tot)

    spec = pl.BlockSpec((tm, Wg), lambda i: (i, 0))
    return pl.pallas_call(
        body, name=name, grid=(M // tm,),
        in_specs=[spec] * (2 * ngr) + [pl.BlockSpec((tm, Wg), lambda i: (i, 3 * ngr))],
        out_specs=[spec] * 3,
        out_shape=[jax.ShapeDtypeStruct((M, Wg), ACT), jax.ShapeDtypeStruct((M, Wg), ACT),
                   jax.ShapeDtypeStruct((M, Wg), F32)],
        compiler_params=_cp("parallel"),
    )(*os_, *ls, proj)


def _b_bwd_pre(dout, w_out, o, proj, name):
    M, Wg = o.shape
    Dm = dout.shape[1]
    ngr = len(B_DILATIONS)
    tm = min(M, 512)

    def body(do_ref, wo_ref, o_ref, z_ref, dov_ref, dl_ref, dp_ref):
        dy = _dot_nt(do_ref[...], wo_ref[...])
        z = z_ref[...].astype(F32)
        sig = _sigmoid(z)
        ov = o_ref[...].astype(F32)
        dp_ref[...] = (dy * ov * (sig * (1.0 + z * (1.0 - sig)))).astype(dp_ref.dtype)
        dov = dy * (z * sig)
        dov_ref[...] = dov.astype(dov_ref.dtype)
        prod = dov * ov
        for h in range(B_HEADS):
            cols = slice(h * HEAD_DIM, (h + 1) * HEAD_DIM)
            dl_ref[:, cols] = jnp.broadcast_to(jnp.sum(prod[:, cols], axis=-1, keepdims=True), (tm, HEAD_DIM))

    spec = pl.BlockSpec((tm, Wg), lambda i: (i, 0))
    zspec = pl.BlockSpec((tm, Wg), lambda i: (i, 3 * ngr))
    return pl.pallas_call(
        body, name=name, grid=(M // tm,),
        in_specs=[pl.BlockSpec((tm, Dm), lambda i: (i, 0)), pl.BlockSpec((Wg, Dm), lambda i: (0, 0)), spec, zspec],
        out_specs=[spec, spec, zspec],
        out_shape=[jax.ShapeDtypeStruct((M, Wg), ACT), jax.ShapeDtypeStruct((M, Wg), F32),
                   jax.ShapeDtypeStruct(proj.shape, ACT)],
        compiler_params=_cp("parallel"),
    )(dout, w_out, o, proj)


def _b_attn_bwd(qk, proj, dov, lse, delta, dproj, g, name):
    M = qk.shape[0]
    D = B_DILATIONS[g]
    ngr = len(B_DILATIONS)
    T = _attn_tile(D)
    P = HEAD_DIM * D
    nsb = T // P
    nt = M // T
    Wg = B_HEADS * HEAD_DIM
    scale = np.float32(1.0 / np.sqrt(HEAD_DIM))
    shift = T - P

    def body(q_ref, k_ref, v_ref, do_ref, l_ref, dl_ref, dp_any, dq_ref, dk_ref, dv_ref,
             qs, dos, lss, dls, ks, vs, dqs, dks, dvs):
        n = pl.program_id(1)

        @pl.when(n == 0)
        def _():
            ks[0:P, :] = jnp.zeros((P, HEAD_DIM), F32)
            vs[0:P, :] = jnp.zeros((P, HEAD_DIM), F32)
            dks[0:T, :] = jnp.zeros((T, HEAD_DIM), F32)
            dvs[0:T, :] = jnp.zeros((T, HEAD_DIM), F32)

        dks[T:2 * T, :] = jnp.zeros((T, HEAD_DIM), F32)
        dvs[T:2 * T, :] = jnp.zeros((T, HEAD_DIM), F32)

        @pl.when(n < nt)
        def _():
            qs[...] = q_ref[...].astype(F32)
            dos[...] = do_ref[...].astype(F32)
            lss[...] = l_ref[...]
            dls[...] = dl_ref[...]
            ks[P:P + T, :] = k_ref[...].astype(F32)
            vs[P:P + T, :] = v_ref[...].astype(F32)

            def unit(u, carry):
                b, r = u // D, u % D
                start = b * P + r
                qsl = _strided(start, CHUNK, D)
                ksl = _strided(start, 2 * CHUNK, D)
                dsl = _strided(start + shift, 2 * CHUNK, D)
                q = qs[qsl, :]
                do = dos[qsl, :]
                k = ks[ksl, :]
                v = vs[ksl, :]
                s = _dot_nt(q, k) * scale
                mask = _attn_mask(n * (T // D) + b * CHUNK)
                p = jnp.where(mask, jnp.exp(s - lss[qsl, :][:, :1]), 0.0)
                dvs[dsl, :] += _dot_tn(p, do)
                dp = _dot_nt(do, v)
                ds = (p * (dp - dls[qsl, :][:, :1]) * scale).astype(MXU)
                dqs[qsl, :] = _dot(ds, k)
                dks[dsl, :] += _dot_tn(ds, q)
                return carry

            lax.fori_loop(0, nsb * D, unit, 0)

        dq_ref[...] = dqs[...].astype(dq_ref.dtype)
        dk_ref[...] = dks[0:T, :].astype(dk_ref.dtype)
        dv_ref[...] = dvs[0:T, :].astype(dv_ref.dtype)
        dks[0:T, :] = dks[T:2 * T, :]
        dvs[0:T, :] = dvs[T:2 * T, :]
        ks[0:P, :] = ks[T:T + P, :]
        vs[0:P, :] = vs[T:T + P, :]

    blk = (T, HEAD_DIM)
    cur = lambda n: jnp.minimum(n, nt - 1)
    prv = lambda n: jnp.maximum(n - 1, 0)
    return pl.pallas_call(
        body, name=name, grid=(B_HEADS, nt + 1),
        in_specs=[pl.BlockSpec(blk, lambda h, n: (cur(n), g * B_HEADS + h)),
                  pl.BlockSpec(blk, lambda h, n: (cur(n), (ngr + g) * B_HEADS + h)),
                  pl.BlockSpec(blk, lambda h, n: (cur(n), (2 * ngr + g) * B_HEADS + h)),
                  pl.BlockSpec(blk, lambda h, n: (cur(n), h)),
                  pl.BlockSpec(blk, lambda h, n: (cur(n), h)),
                  pl.BlockSpec(blk, lambda h, n: (cur(n), h)),
                  pl.BlockSpec(memory_space=pl.ANY)],
        out_specs=[pl.BlockSpec(blk, lambda h, n: (cur(n), h)),
                   pl.BlockSpec(blk, lambda h, n: (prv(n), h)),
                   pl.BlockSpec(blk, lambda h, n: (prv(n), (2 * ngr + g) * B_HEADS + h))],
        out_shape=[jax.ShapeDtypeStruct((M, Wg), ACT), jax.ShapeDtypeStruct((M, Wg), ACT),
                   jax.ShapeDtypeStruct(dproj.shape, dproj.dtype)],
        scratch_shapes=[pltpu.VMEM((T, HEAD_DIM), F32)] * 4
        + [pltpu.VMEM((P + T, HEAD_DIM), F32)] * 2
        + [pltpu.VMEM((T, HEAD_DIM), F32)]
        + [pltpu.VMEM((2 * T, HEAD_DIM), F32)] * 2,
        input_output_aliases={6: 2},
        compiler_params=_cp("parallel", "arbitrary"),
    )(qk, qk, proj, dov, lse, delta, dproj)


def _coords():
    return lax.axis_index("x"), lax.axis_index("y"), lax.axis_index("c")


def _gather_blocks(x_ref, out_ref, send_sems, recv_sems, local_sem):
    x, y, c = _coords()
    me, sibling = (x, y, c), (x, y, 1 - c)
    chips = [(1 - x, y), (x, 1 - y), (1 - x, 1 - y)]

    def slot(px, py, pc):
        return out_ref.at[4 * px + 2 * py + pc]

    def copy(k, block, to, src=None):
        return pltpu.make_async_remote_copy(
            src_ref=slot(*block) if src is None else src, dst_ref=slot(*block),
            send_sem=send_sems.at[k], recv_sem=recv_sems.at[k], device_id=to, device_id_type=MESH)

    mine_cp = pltpu.make_async_copy(x_ref, slot(*me), local_sem)
    mine_cp.start()
    first = [copy(0, me, sibling, src=x_ref)]
    first += [copy(1 + j, me, (*chip, c), src=x_ref) for j, chip in enumerate(chips)]
    for cp in first:
        cp.start()
    passed = [copy(4 + j, (*chip, c), sibling) for j, chip in enumerate(chips)]
    for j, chip in enumerate(chips):
        copy(1 + j, (*chip, c), me).wait_recv()
        passed[j].start()
    copy(0, sibling, me).wait_recv()
    for j, chip in enumerate(chips):
        copy(4 + j, (*chip, 1 - c), me).wait_recv()
    for cp in first + passed:
        cp.wait_send()
    mine_cp.wait()


def _all_gather_hbm(mine):
    R, C = mine.shape

    def body(x_ref, out_ref, send_sems, recv_sems, local_sem):
        _gather_blocks(x_ref, out_ref, send_sems, recv_sems, local_sem)

    return pl.pallas_call(
        body, name="ag_weights",
        in_specs=[pl.BlockSpec(memory_space=pl.ANY)],
        out_specs=pl.BlockSpec(memory_space=pl.ANY),
        out_shape=jax.ShapeDtypeStruct((N_DEV, R, C), mine.dtype),
        scratch_shapes=[pltpu.SemaphoreType.DMA((7,)), pltpu.SemaphoreType.DMA((7,)), pltpu.SemaphoreType.DMA],
    )(mine)


def _all_reduce_small(part):
    R, C = part.shape

    def body(x_ref, tot_ref, gath, send_sems, recv_sems, local_sem):
        _gather_blocks(x_ref, gath, send_sems, recv_sems, local_sem)
        acc = gath[0]
        for d in range(1, N_DEV):
            acc = acc + gath[d]
        tot_ref[...] = acc

    return pl.pallas_call(
        body, name="ar_small",
        in_specs=[pl.BlockSpec(memory_space=pltpu.VMEM)],
        out_specs=pl.BlockSpec(memory_space=pltpu.VMEM),
        out_shape=jax.ShapeDtypeStruct((R, C), F32),
        scratch_shapes=[pltpu.VMEM((N_DEV, R, C), F32),
                        pltpu.SemaphoreType.DMA((7,)), pltpu.SemaphoreType.DMA((7,)), pltpu.SemaphoreType.DMA],
        compiler_params=pltpu.CompilerParams(vmem_limit_bytes=VMEM_LIMIT),
    )(part)


def _rs_sibling(grads):
    n = len(grads)

    def body(*refs):
        g_refs, r_refs = refs[:n], refs[n:2 * n]
        send_sems, recv_sems = refs[2 * n], refs[2 * n + 1]
        x, y, c = _coords()
        cps = []
        for a in range(n):
            cp = pltpu.make_async_remote_copy(
                src_ref=g_refs[a].at[pl.ds(4 * (1 - c), 4)], dst_ref=r_refs[a],
                send_sem=send_sems.at[a], recv_sem=recv_sems.at[a], device_id=(x, y, 1 - c), device_id_type=MESH)
            cp.start()
            cps.append(cp)
        for cp in cps:
            cp.wait()

    return pl.pallas_call(
        body, name="rs_sibling",
        in_specs=[pl.BlockSpec(memory_space=pl.ANY)] * n,
        out_specs=[pl.BlockSpec(memory_space=pl.ANY)] * n,
        out_shape=[jax.ShapeDtypeStruct((4,) + g.shape[1:], g.dtype) for g in grads],
        scratch_shapes=[pltpu.SemaphoreType.DMA((n,)), pltpu.SemaphoreType.DMA((n,))],
    )(*grads)


def _rs_chips(parts):
    n = len(parts)

    def body(*refs):
        s_refs, r_refs = refs[:n], refs[n:2 * n]
        send_sems, recv_sems = refs[2 * n], refs[2 * n + 1]
        x, y, c = _coords()
        peers = [(x, 1 - y, c), (1 - x, y, c), (1 - x, 1 - y, c)]
        cps = []
        for a in range(n):
            for k, peer in enumerate(peers):
                cp = pltpu.make_async_remote_copy(
                    src_ref=s_refs[a].at[k], dst_ref=r_refs[a].at[k],
                    send_sem=send_sems.at[3 * a + k], recv_sem=recv_sems.at[3 * a + k],
                    device_id=peer, device_id_type=MESH)
                cp.start()
                cps.append(cp)
        for cp in cps:
            cp.wait()

    return pl.pallas_call(
        body, name="rs_chips",
        in_specs=[pl.BlockSpec(memory_space=pl.ANY)] * n,
        out_specs=[pl.BlockSpec(memory_space=pl.ANY)] * n,
        out_shape=[jax.ShapeDtypeStruct(p.shape, p.dtype) for p in parts],
        scratch_shapes=[pltpu.SemaphoreType.DMA((3 * n,)), pltpu.SemaphoreType.DMA((3 * n,))],
    )(*parts)


def _row_tile(rows, cols):
    tr = min(rows, 1 << int(np.log2((1 << 18) // cols)))
    assert rows % tr == 0
    return tr


def _chip_partials(coords, g, r1, name):
    _, rows, C = g.shape
    tr = _row_tile(rows, C)

    def body(co_ref, g_ref, r_ref, o_ref):
        o_ref[...] = (g_ref[...] + r_ref[...]).astype(o_ref.dtype)

    def chip(k, co):
        return jnp.bitwise_xor(2 * co[0] + co[1], k + 1)

    return pl.pallas_call(
        body, name=name,
        grid_spec=pltpu.PrefetchScalarGridSpec(
            num_scalar_prefetch=1, grid=(3, rows // tr),
            in_specs=[pl.BlockSpec((None, tr, C), lambda k, t, co: (4 * co[2] + chip(k, co), t, 0)),
                      pl.BlockSpec((None, tr, C), lambda k, t, co: (chip(k, co), t, 0))],
            out_specs=pl.BlockSpec((None, tr, C), lambda k, t, co: (k, t, 0))),
        out_shape=jax.ShapeDtypeStruct((3, rows, C), WIRE),
        compiler_params=_cp("parallel", "parallel"),
    )(coords, g, r1)


def _adam_math(w, g, m, v):
    m = ADAM_B1 * m + (1.0 - ADAM_B1) * g
    v = ADAM_B2 * v + (1.0 - ADAM_B2) * (g * g)
    m_hat = m / (1.0 - ADAM_B1 ** ADAM_STEP)
    v_hat = v / (1.0 - ADAM_B2 ** ADAM_STEP)
    delta = -ADAM_LR * (m_hat / (jnp.sqrt(v_hat) + ADAM_EPS) + ADAM_WD * w)
    return delta, m, v


def _adamw_sharded(coords, w, m, v, g, r1, r2, name):
    rows, C = w.shape
    tr = _row_tile(rows, C)

    def body(co_ref, w_ref, m_ref, v_ref, g_ref, r1_ref, r2_ref, go_ref, d_ref, mo_ref, vo_ref):
        grad = g_ref[...] + r1_ref[...]
        for k in range(3):
            grad = grad + r2_ref[k].astype(F32)
        go_ref[...] = grad
        d_ref[...], mo_ref[...], vo_ref[...] = _adam_math(w_ref[...], grad, m_ref[...], v_ref[...])

    spec = pl.BlockSpec((tr, C), lambda t, co: (t, 0))
    return pl.pallas_call(
        body, name=name,
        grid_spec=pltpu.PrefetchScalarGridSpec(
            num_scalar_prefetch=1, grid=(rows // tr,),
            in_specs=[spec, spec, spec,
                      pl.BlockSpec((None, tr, C), lambda t, co: (4 * co[2] + 2 * co[0] + co[1], t, 0)),
                      pl.BlockSpec((None, tr, C), lambda t, co: (2 * co[0] + co[1], t, 0)),
                      pl.BlockSpec((3, tr, C), lambda t, co: (0, t, 0))],
            out_specs=[spec] * 4),
        out_shape=[jax.ShapeDtypeStruct((rows, C), F32)] * 4,
        compiler_params=_cp("parallel"),
    )(coords, w, m, v, g, r1, r2)


def _adamw_small(w, g, m, v, name):
    def body(w_ref, g_ref, m_ref, v_ref, d_ref, mo_ref, vo_ref):
        d_ref[...], mo_ref[...], vo_ref[...] = _adam_math(w_ref[...], g_ref[...], m_ref[...], v_ref[...])

    return pl.pallas_call(
        body, name=name, out_shape=[jax.ShapeDtypeStruct(w.shape, F32)] * 3,
        in_specs=[pl.BlockSpec(memory_space=pltpu.VMEM)] * 4,
        out_specs=[pl.BlockSpec(memory_space=pltpu.VMEM)] * 3,
    )(w, g, m, v)


def _wire_rows(a, cols):
    if a.dtype != WIRE:
        a = lax.bitcast_convert_type(a, WIRE)
    flat = a.reshape(-1)
    pad = (-flat.shape[0]) % cols
    if pad:
        flat = jnp.concatenate([flat, jnp.zeros((pad,), WIRE)])
    return flat.reshape(-1, cols)


def _local_step(xs, tgt, W, norm_gain, a_w_s, a_b_s, b_gains):
    S = xs.shape[0]
    tables = _rope_tables(S)
    ng = lambda i: norm_gain[i:i + 1]
    ngr = len(B_DILATIONS)
    bst = [a_b_s[l].T for l in range(2)]

    h0, p0 = _norm_proj(xs, ng(0), W["a_in"][0], "l0_proj")
    y0 = _a_mid(p0, W["a_vg"][0:1], a_w_s[0], bst[0], "l0_mid")
    x1 = _out_proj(xs, y0, W["a_out"][0], "l0_out")
    h1, p1 = _norm_proj(x1, ng(1), W["b_in"], "l1_proj")
    qk = _b_qk_fwd(p1, tables, b_gains, "l1_qk")
    ogs, lgs = zip(*[_b_attn_fwd(qk, p1, g, f"l1_attn{g}") for g in range(ngr)])
    y1, o1, lse = _b_combine(ogs, lgs, p1, "l1_comb")
    x2 = _out_proj(x1, y1, W["b_out"], "l1_out")
    h2, p2 = _norm_proj(x2, ng(2), W["c_in"], "l2_proj")
    y2 = _c_mid(p2, W["c_grp"], W["c_sc"], "l2_mid")
    x3 = _out_proj(x2, y2, W["c_out"], "l2_out")
    h3, p3 = _norm_proj(x3, ng(3), W["a_in"][1], "l3_proj")
    y3 = _a_mid(p3, W["a_vg"][1:2], a_w_s[1], bst[1], "l3_mid")
    x4 = _out_proj(x3, y3, W["a_out"][1], "l3_out")
    loss, dx4 = _loss_head(x4, tgt)

    dp3, dws1, dbs1, dvg1 = _a_bwd(dx4, W["a_out"][1], p3, W["a_vg"][1:2], a_w_s[1], bst[1], "l3_bwd")
    ga_out = _dw_out(y3, dx4, None, 1, 2, "l3_dwout")
    ga_in = _dw_in(h3, dp3, None, 1, 2, "l3_dwin")
    dx3, dg3 = _dh_norm_bwd(dp3, W["a_in"][1], x3, ng(3), dx4, "l3_dh")

    dd, dz, gc_grp, dsc = _c_bwd1(dx3, W["c_out"], p2, W["c_grp"], W["c_sc"], "l2_bwd1")
    dp2 = _c_bwd2(dd, dz, "l2_bwd2")
    gc_out = _dw_out(y2, dx3, None, 0, 1, "l2_dwout")
    gc_in = _dw_in(h2, dp2, None, 0, 1, "l2_dwin")
    dx2, dg2 = _dh_norm_bwd(dp2, W["c_in"], x2, ng(2), dx3, "l2_dh")

    dov, delta, dp1 = _b_bwd_pre(dx2, W["b_out"], o1, p1, "l1_bwdpre")
    dqs, dks = [], []
    for g in range(ngr):
        dq, dk, dp1 = _b_attn_bwd(qk, p1, dov, lse, delta, dp1, g, f"l1_attnbwd{g}")
        dqs.append(dq)
        dks.append(dk)
    dp1, dgains = _b_qk_bwd(dqs, dks, p1, tables, b_gains, dp1, "l1_qkbwd")
    gb_out = _dw_out(y1, dx2, None, 0, 1, "l1_dwout")
    gb_in = _dw_in(h1, dp1, None, 0, 1, "l1_dwin")
    dx1, dg1 = _dh_norm_bwd(dp1, W["b_in"], x1, ng(1), dx2, "l1_dh")

    dp0, dws0, dbs0, dvg0 = _a_bwd(dx1, W["a_out"][0], p0, W["a_vg"][0:1], a_w_s[0], bst[0], "l0_bwd")
    ga_out = _dw_out(y0, dx1, ga_out, 0, 2, "l0_dwout")
    ga_in = _dw_in(h0, dp0, ga_in, 0, 2, "l0_dwin")
    dx0, dg0 = _dh_norm_bwd(dp0, W["a_in"][0], xs, ng(0), dx1, "l0_dh")

    big = dict(a_in=ga_in, a_out=ga_out, b_in=gb_in, b_out=gb_out, c_in=gc_in, c_grp=gc_grp, c_out=gc_out)
    small = dict(norm=jnp.concatenate([dg0, dg1, dg2, dg3], axis=0), a_ws=jnp.stack([dws0, dws1]),
                 a_bs=jnp.stack([dbs0.T, dbs1.T]), b_gains=dgains, a_vg=jnp.concatenate([dvg0, dvg1], axis=0), c_sc=dsc)
    return loss, dx0, big, small


def kernel(x, norm_gain, a_w_in, a_v_gain, a_w_s, a_b_s, a_w_out, b_w_in, b_q_gain, b_k_gain, b_w_out, c_w_in, c_w_grp, c_scale, c_w_out, loss_target, m_norm_gain, m_a_w_in, m_a_v_gain, m_a_w_s, m_a_b_s, m_a_w_out, m_b_w_in, m_b_q_gain, m_b_k_gain, m_b_w_out, m_c_w_in, m_c_w_grp, m_c_scale, m_c_w_out, v_norm_gain, v_a_w_in, v_a_v_gain, v_a_w_s, v_a_b_s, v_a_w_out, v_b_w_in, v_b_q_gain, v_b_k_gain, v_b_w_out, v_c_w_in, v_c_w_grp, v_c_scale, v_c_w_out):
    cx, cy, cc = _coords()
    coords = jnp.stack([cx, cy, cc]).astype(jnp.int32)
    dev = 4 * cx + 2 * cy + cc
    Dm = x.shape[2]

    big_w = [a_w_in, a_w_out, b_w_in, b_w_out, c_w_in, c_w_grp, c_w_out]
    pieces = [_wire_rows(w.astype(WIRE), Dm) for w in big_w] + [_wire_rows(a_v_gain, Dm), _wire_rows(c_scale, Dm)]
    offs = np.cumsum([0] + [p.shape[0] for p in pieces])
    pad = (-int(offs[-1])) % 16
    packed = jnp.concatenate(pieces + ([jnp.zeros((pad, Dm), WIRE)] if pad else []), axis=0)
    gath = _all_gather_hbm(packed)

    def piece(i):
        return gath[:, int(offs[i]):int(offs[i + 1]), :]

    def small_f32(i, shape):
        flat = piece(i).reshape(N_DEV, -1)
        n = int(np.prod(shape))
        if WIRE == F32:
            return flat[:, :n].reshape((N_DEV,) + shape)
        return lax.bitcast_convert_type(flat[:, :2 * n].reshape((N_DEV,) + shape + (2,)), F32)

    na, _, nla = a_w_in.shape
    kla = a_w_out.shape[1]
    ngp, rlc, cgc = c_w_grp.shape[1:]
    a_in_g = piece(0).reshape(N_DEV, na, Dm, nla)
    a_out_g = piece(1).reshape(N_DEV, na, kla, Dm)
    a_vg_g = small_f32(7, a_v_gain.shape)
    c_sc_g = small_f32(8, c_scale.shape)
    W = dict(
        a_in=[a_in_g[:, l] for l in range(na)],
        a_out=[a_out_g[:, l].reshape(N_DEV * kla, Dm) for l in range(na)],
        b_in=piece(2).reshape(N_DEV, Dm, b_w_in.shape[2]),
        b_out=piece(3).reshape(-1, Dm),
        c_in=piece(4).reshape(N_DEV, Dm, c_w_in.shape[2]),
        c_grp=piece(5).reshape(N_DEV, ngp, rlc, cgc).transpose(1, 0, 2, 3).reshape(ngp, N_DEV * rlc, cgc),
        c_out=piece(6).reshape(-1, Dm),
        a_vg=a_vg_g.transpose(1, 0, 2).reshape(na, -1),
        c_sc=c_sc_g.transpose(1, 0, 2).reshape(1, -1),
    )
    b_gains = jnp.concatenate([b_q_gain[0], b_k_gain[0], jnp.zeros((2, HEAD_DIM), F32)], axis=0)

    loss_local, dx0, big, small = _local_step(x[0], loss_target[0], W, norm_gain, a_w_s, a_b_s, b_gains)
    loss = lax.psum(loss_local, ("x", "y", "c"))

    order = ["norm", "a_ws", "a_bs", "b_gains", "a_vg", "c_sc"]
    rows = [small[k].reshape(-1, 128) for k in order]
    roff = np.cumsum([0] + [r.shape[0] for r in rows])
    tot = _all_reduce_small(jnp.concatenate(rows, axis=0))
    sm = {k: tot[int(roff[i]):int(roff[i + 1])].reshape(small[k].shape) for i, k in enumerate(order)}
    vl = a_v_gain.shape[1]
    g_small = dict(
        norm_gain=sm["norm"], a_w_s=sm["a_ws"], a_b_s=sm["a_bs"],
        b_q_gain=sm["b_gains"][None, 0:3], b_k_gain=sm["b_gains"][None, 3:6],
        a_v_gain=lax.dynamic_slice_in_dim(sm["a_vg"], dev * vl, vl, axis=1),
        c_scale=lax.dynamic_slice_in_dim(sm["c_sc"], dev * vl, vl, axis=1),
    )

    names = ["a_in", "a_out", "b_in", "b_out", "c_in", "c_grp", "c_out"]
    pnames = ["a_w_in", "a_w_out", "b_w_in", "b_w_out", "c_w_in", "c_w_grp", "c_w_out"]
    g3 = [big[k].reshape(N_DEV, -1, big[k].shape[-1]) for k in names]
    r1 = _rs_sibling(g3)
    parts = [_chip_partials(coords, g, r, f"rs_add_{k}") for k, g, r in zip(names, g3, r1)]
    r2 = _rs_chips(parts)

    params = dict(a_w_in=a_w_in, a_w_out=a_w_out, b_w_in=b_w_in, b_w_out=b_w_out, c_w_in=c_w_in, c_w_grp=c_w_grp, c_w_out=c_w_out,
                  norm_gain=norm_gain, a_v_gain=a_v_gain, a_w_s=a_w_s, a_b_s=a_b_s, b_q_gain=b_q_gain, b_k_gain=b_k_gain, c_scale=c_scale)
    moms = dict(a_w_in=(m_a_w_in, v_a_w_in), a_w_out=(m_a_w_out, v_a_w_out), b_w_in=(m_b_w_in, v_b_w_in), b_w_out=(m_b_w_out, v_b_w_out),
                c_w_in=(m_c_w_in, v_c_w_in), c_w_grp=(m_c_w_grp, v_c_w_grp), c_w_out=(m_c_w_out, v_c_w_out),
                norm_gain=(m_norm_gain, v_norm_gain), a_v_gain=(m_a_v_gain, v_a_v_gain), a_w_s=(m_a_w_s, v_a_w_s),
                a_b_s=(m_a_b_s, v_a_b_s), b_q_gain=(m_b_q_gain, v_b_q_gain), b_k_gain=(m_b_k_gain, v_b_k_gain),
                c_scale=(m_c_scale, v_c_scale))
    grad, delta, new_m, new_v = {}, {}, {}, {}
    for i, (k, pname) in enumerate(zip(names, pnames)):
        w = params[pname]
        C = w.shape[-1]
        outs = _adamw_sharded(coords, w.reshape(-1, C), moms[pname][0].reshape(-1, C), moms[pname][1].reshape(-1, C),
                              g3[i], r1[i], r2[i], f"adamw_{k}")
        grad[pname], delta[pname], new_m[pname], new_v[pname] = [o.reshape(w.shape) for o in outs]
    for pname, g in g_small.items():
        w = params[pname]
        C = w.shape[-1]
        outs = _adamw_small(w.reshape(-1, C), g.reshape(-1, C), moms[pname][0].reshape(-1, C), moms[pname][1].reshape(-1, C),
                            f"adamw_{pname}")
        grad[pname] = g.reshape(w.shape)
        delta[pname], new_m[pname], new_v[pname] = [o.reshape(w.shape) for o in outs]

    wnames = ["norm_gain", "a_w_in", "a_v_gain", "a_w_s", "a_b_s", "a_w_out", "b_w_in", "b_q_gain", "b_k_gain", "b_w_out",
              "c_w_in", "c_w_grp", "c_scale", "c_w_out"]
    return (loss, dx0[None], *[grad[n] for n in wnames], *[delta[n] for n in wnames],
            *[new_m[n] for n in wnames], *[new_v[n] for n in wnames])
```

```python
import functools

import numpy as np
import jax
import jax.numpy as jnp
from jax import lax
from jax.experimental import pallas as pl
from jax.experimental.pallas import tpu as pltpu

F32 = jnp.float32
MXU = jnp.bfloat16
ACT = jnp.bfloat16
WIRE = jnp.bfloat16

EPS = 1e-6
CHUNK = 128
A_GROUPS = 8
HEAD_DIM = 128
B_HEADS = 8
B_DILATIONS = (1, 4, 16)
ROPE_DIM = 32
ROPE_THETA = 500000.0
POOL_SIZES = (2, 4, 8, 16)
POOL_HALO = 16
N_DEV = 8
NEG = -1e30

ADAM_LR, ADAM_B1, ADAM_B2, ADAM_EPS, ADAM_WD, ADAM_STEP = 0.001, 0.9, 0.999, 1e-08, 0.01, 10

VMEM_LIMIT = 56 * 1024 * 1024
MESH = pl.DeviceIdType.MESH


def _cp(*sem):
    return pltpu.CompilerParams(dimension_semantics=sem, vmem_limit_bytes=VMEM_LIMIT)


def _sigmoid(z):
    return 1.0 / (1.0 + jnp.exp(-z))


def _dot(a, b):
    return jnp.dot(a.astype(MXU), b.astype(MXU), preferred_element_type=F32)


def _dot_nt(a, b):
    return lax.dot_general(a.astype(MXU), b.astype(MXU), (((1,), (1,)), ((), ())), preferred_element_type=F32)


def _dot_tn(a, b):
    return lax.dot_general(a.astype(MXU), b.astype(MXU), (((0,), (0,)), ((), ())), preferred_element_type=F32)


def _chunk_slot(d):
    return (d % 2) * 4 + d // 2


def _norm_proj(x, gain, w_dm, name):
    M, Dm = x.shape
    nd, _, nl = w_dm.shape
    tm = min(M, 1024)

    def body(x_ref, g_ref, w_ref, h_ref, p_ref):
        @pl.when(pl.program_id(1) == 0)
        def _():
            xv = x_ref[...]
            r = lax.rsqrt(jnp.mean(xv * xv, axis=-1, keepdims=True) + EPS)
            h_ref[...] = (xv * r * g_ref[...]).astype(h_ref.dtype)

        p_ref[...] = _dot(h_ref[...], w_ref[...]).astype(p_ref.dtype)

    return pl.pallas_call(
        body, name=name, grid=(M // tm, nd),
        in_specs=[pl.BlockSpec((tm, Dm), lambda i, j: (i, 0)),
                  pl.BlockSpec((1, Dm), lambda i, j: (0, 0)),
                  pl.BlockSpec((None, Dm, nl), lambda i, j: (j, 0, 0))],
        out_specs=[pl.BlockSpec((tm, Dm), lambda i, j: (i, 0)),
                   pl.BlockSpec((tm, nl), lambda i, j: (i, j))],
        out_shape=[jax.ShapeDtypeStruct((M, Dm), ACT), jax.ShapeDtypeStruct((M, nd * nl), ACT)],
        compiler_params=_cp("parallel", "arbitrary"),
    )(x, gain, w_dm)


def _out_proj(x, y, w, name):
    M, Dm = x.shape
    K = y.shape[1]
    tm = min(M, 512)

    def body(x_ref, y_ref, w_ref, o_ref):
        o_ref[...] = x_ref[...] + _dot(y_ref[...], w_ref[...])

    return pl.pallas_call(
        body, name=name, grid=(M // tm,),
        in_specs=[pl.BlockSpec((tm, Dm), lambda i: (i, 0)),
                  pl.BlockSpec((tm, K), lambda i: (i, 0)),
                  pl.BlockSpec((K, Dm), lambda i: (0, 0))],
        out_specs=pl.BlockSpec((tm, Dm), lambda i: (i, 0)),
        out_shape=jax.ShapeDtypeStruct((M, Dm), F32),
        compiler_params=_cp("parallel"),
    )(x, y, w)


def _loss_head(xf, target):
    M, Dm = xf.shape
    tm = min(M, 512)

    def body(x_ref, t_ref, dx_ref, l_ref):
        @pl.when(pl.program_id(0) == 0)
        def _():
            l_ref[...] = jnp.zeros_like(l_ref)

        err = x_ref[...] - t_ref[...]
        dx_ref[...] = err * (1.0 / Dm)
        l_ref[...] += jnp.sum(err * err) * (0.5 / Dm)

    dx, l = pl.pallas_call(
        body, name="loss_head", grid=(M // tm,),
        in_specs=[pl.BlockSpec((tm, Dm), lambda i: (i, 0))] * 2,
        out_specs=[pl.BlockSpec((tm, Dm), lambda i: (i, 0)), pl.BlockSpec((8, 128), lambda i: (0, 0))],
        out_shape=[jax.ShapeDtypeStruct((M, Dm), F32), jax.ShapeDtypeStruct((8, 128), F32)],
        compiler_params=_cp("arbitrary"),
    )(xf, target)
    return l[0, 0], dx


def _dw_in(h, dproj, prev, layer, n_layers, name):
    M, Dm = h.shape
    nl = dproj.shape[1] // N_DEV
    tt = min(M, 512)

    def body(a_ref, b_ref, *rest):
        o_ref = rest[-1]

        @pl.when(pl.program_id(1) == 0)
        def _():
            o_ref[...] = jnp.zeros_like(o_ref)

        o_ref[...] += _dot_tn(a_ref[...], b_ref[...])

    in_specs = [pl.BlockSpec((tt, Dm), lambda j, t: (t, 0)), pl.BlockSpec((tt, nl), lambda j, t: (t, j))]
    args = [h, dproj]
    aliases = {}
    if prev is not None:
        in_specs.append(pl.BlockSpec(memory_space=pl.ANY))
        args.append(prev)
        aliases = {2: 0}
    return pl.pallas_call(
        body, name=name, grid=(N_DEV, M // tt),
        in_specs=in_specs,
        out_specs=pl.BlockSpec((None, None, Dm, nl), lambda j, t: (_chunk_slot(j), layer, 0, 0)),
        out_shape=jax.ShapeDtypeStruct((N_DEV, n_layers, Dm, nl), F32),
        input_output_aliases=aliases,
        compiler_params=_cp("parallel", "arbitrary"),
    )(*args)


def _dw_out(y, dout, prev, layer, n_layers, name):
    M, K = y.shape
    Dm = dout.shape[1]
    kl = K // N_DEV
    tt = min(M, 1024)

    def body(a_ref, b_ref, *rest):
        o_ref = rest[-1]

        @pl.when(pl.program_id(1) == 0)
        def _():
            o_ref[...] = jnp.zeros_like(o_ref)

        o_ref[...] += _dot_tn(a_ref[...], b_ref[...])

    in_specs = [pl.BlockSpec((tt, kl), lambda j, t: (t, j)), pl.BlockSpec((tt, Dm), lambda j, t: (t, 0))]
    args = [y, dout]
    aliases = {}
    if prev is not None:
        in_specs.append(pl.BlockSpec(memory_space=pl.ANY))
        args.append(prev)
        aliases = {2: 0}
    return pl.pallas_call(
        body, name=name, grid=(N_DEV, M // tt),
        in_specs=in_specs,
        out_specs=pl.BlockSpec((None, None, kl, Dm), lambda j, t: (_chunk_slot(j), layer, 0, 0)),
        out_shape=jax.ShapeDtypeStruct((N_DEV, n_layers, kl, Dm), F32),
        input_output_aliases=aliases,
        compiler_params=_cp("parallel", "arbitrary"),
    )(*args)


def _dh_norm_bwd(dproj, w_dm, x, gain, dres, name):
    M, Dm = x.shape
    nd, _, nl = w_dm.shape
    tm = min(M, 1024)

    def body(dp_ref, w_ref, x_ref, g_ref, dr_ref, dx_ref, dg_ref, acc_ref):
        i, j = pl.program_id(0), pl.program_id(1)

        @pl.when(j == 0)
        def _():
            acc_ref[...] = jnp.zeros_like(acc_ref)

        acc_ref[...] += _dot_nt(dp_ref[...], w_ref[...])

        @pl.when(j == nd - 1)
        def _():
            @pl.when(i == 0)
            def _():
                dg_ref[...] = jnp.zeros_like(dg_ref)

            dh = acc_ref[...]
            xv = x_ref[...]
            r = lax.rsqrt(jnp.mean(xv * xv, axis=-1, keepdims=True) + EPS)
            xn = xv * r
            dg_ref[...] += jnp.sum(dh * xn, axis=0, keepdims=True)
            dxn = dh * g_ref[...]
            dx_ref[...] = dr_ref[...] + r * (dxn - xn * jnp.mean(dxn * xn, axis=-1, keepdims=True))

    return pl.pallas_call(
        body, name=name, grid=(M // tm, nd),
        in_specs=[pl.BlockSpec((tm, nl), lambda i, j: (i, j)),
                  pl.BlockSpec((None, Dm, nl), lambda i, j: (j, 0, 0)),
                  pl.BlockSpec((tm, Dm), lambda i, j: (i, 0)),
                  pl.BlockSpec((1, Dm), lambda i, j: (0, 0)),
                  pl.BlockSpec((tm, Dm), lambda i, j: (i, 0))],
        out_specs=[pl.BlockSpec((tm, Dm), lambda i, j: (i, 0)), pl.BlockSpec((1, Dm), lambda i, j: (0, 0))],
        out_shape=[jax.ShapeDtypeStruct((M, Dm), F32), jax.ShapeDtypeStruct((1, Dm), F32)],
        scratch_shapes=[pltpu.VMEM((tm, Dm), F32)],
        compiler_params=_cp("arbitrary", "arbitrary"),
    )(dproj, w_dm, x, gain, dres)


def _tril_mask():
    return lax.broadcasted_iota(jnp.int32, (CHUNK, CHUNK), 0) >= lax.broadcasted_iota(jnp.int32, (CHUNK, CHUNK), 1)


def _a_mid(proj, v_gain, w_s, b_st, name):
    M = proj.shape[0]
    W = proj.shape[1] // 3
    gd = W // A_GROUPS
    tm = min(M, 256)

    def body(p_ref, vg_ref, ws_ref, bs_ref, y_ref):
        pv = p_ref[:, W:2 * W].astype(F32)
        r = lax.rsqrt(jnp.mean(pv * pv, axis=-1, keepdims=True) + EPS)
        v = (pv * r * vg_ref[...]).astype(MXU)
        tri = _tril_mask()
        for g in range(A_GROUPS):
            wg = jnp.where(tri, ws_ref[g], 0.0).astype(MXU)
            bcol = bs_ref[:, g:g + 1]
            for c in range(tm // CHUNK):
                rows, cols = slice(c * CHUNK, (c + 1) * CHUNK), slice(g * gd, (g + 1) * gd)
                mixed = jnp.dot(wg, v[rows, cols], preferred_element_type=F32) + bcol
                u = p_ref[rows, g * gd:(g + 1) * gd].astype(F32)
                z = p_ref[rows, 2 * W + g * gd:2 * W + (g + 1) * gd].astype(F32)
                y_ref[rows, cols] = (u * mixed * (z * _sigmoid(z))).astype(y_ref.dtype)

    return pl.pallas_call(
        body, name=name, grid=(M // tm,),
        in_specs=[pl.BlockSpec((tm, 3 * W), lambda i: (i, 0)),
                  pl.BlockSpec((1, W), lambda i: (0, 0)),
                  pl.BlockSpec((A_GROUPS, CHUNK, CHUNK), lambda i: (0, 0, 0)),
                  pl.BlockSpec((CHUNK, A_GROUPS), lambda i: (0, 0))],
        out_specs=pl.BlockSpec((tm, W), lambda i: (i, 0)),
        out_shape=jax.ShapeDtypeStruct((M, W), ACT),
        compiler_params=_cp("parallel"),
    )(proj, v_gain, w_s, b_st)


def _a_bwd(dout, w_out, proj, v_gain, w_s, b_st, name):
    M = proj.shape[0]
    W = proj.shape[1] // 3
    Dm = dout.shape[1]
    gd = W // A_GROUPS
    tm = min(M, 256)
    nt = M // tm

    def body(do_ref, wo_ref, p_ref, vg_ref, ws_ref, bs_ref, dp_ref, dws_ref, dbs_ref, dvg_ref, dv_s):
        i = pl.program_id(0)

        @pl.when(i == 0)
        def _():
            dws_ref[...] = jnp.zeros_like(dws_ref)
            dbs_ref[...] = jnp.zeros_like(dbs_ref)
            dvg_ref[...] = jnp.zeros_like(dvg_ref)

        dy = _dot_nt(do_ref[...], wo_ref[...])
        pv = p_ref[:, W:2 * W].astype(F32)
        r = lax.rsqrt(jnp.mean(pv * pv, axis=-1, keepdims=True) + EPS)
        pvn = pv * r
        vg = vg_ref[...]
        v = (pvn * vg).astype(MXU)
        tri = _tril_mask()
        for g in range(A_GROUPS):
            wf = jnp.where(tri, ws_ref[g], 0.0)
            wg = wf.astype(MXU)
            wgt = wf.T.astype(MXU)
            bcol = bs_ref[:, g:g + 1]
            for c in range(tm // CHUNK):
                rows, cols = slice(c * CHUNK, (c + 1) * CHUNK), slice(g * gd, (g + 1) * gd)
                vb = v[rows, cols]
                mixed = jnp.dot(wg, vb, preferred_element_type=F32) + bcol
                u = p_ref[rows, g * gd:(g + 1) * gd].astype(F32)
                z = p_ref[rows, 2 * W + g * gd:2 * W + (g + 1) * gd].astype(F32)
                sig = _sigmoid(z)
                sz = z * sig
                dyb = dy[rows, cols]
                dp_ref[rows, g * gd:(g + 1) * gd] = (dyb * mixed * sz).astype(dp_ref.dtype)
                dp_ref[rows, 2 * W + g * gd:2 * W + (g + 1) * gd] = (
                    dyb * u * mixed * (sig * (1.0 + z * (1.0 - sig)))).astype(dp_ref.dtype)
                dmix = dyb * u * sz
                dws_ref[g] += _dot_nt(dmix, vb)
                dbs_ref[:, g:g + 1] += jnp.sum(dmix, axis=1, keepdims=True)
                dv_s[rows, cols] = jnp.dot(wgt, dmix.astype(MXU), preferred_element_type=F32)
        dv = dv_s[...]
        dvg_ref[...] += jnp.sum(dv * pvn, axis=0, keepdims=True)
        dpvn = dv * vg
        dp_ref[:, W:2 * W] = (r * (dpvn - pvn * jnp.mean(dpvn * pvn, axis=-1, keepdims=True))).astype(dp_ref.dtype)

        @pl.when(i == nt - 1)
        def _():
            for g in range(A_GROUPS):
                dws_ref[g] = jnp.where(tri, dws_ref[g], 0.0)

    return pl.pallas_call(
        body, name=name, grid=(nt,),
        in_specs=[pl.BlockSpec((tm, Dm), lambda i: (i, 0)),
                  pl.BlockSpec((W, Dm), lambda i: (0, 0)),
                  pl.BlockSpec((tm, 3 * W), lambda i: (i, 0)),
                  pl.BlockSpec((1, W), lambda i: (0, 0)),
                  pl.BlockSpec((A_GROUPS, CHUNK, CHUNK), lambda i: (0, 0, 0)),
                  pl.BlockSpec((CHUNK, A_GROUPS), lambda i: (0, 0))],
        out_specs=[pl.BlockSpec((tm, 3 * W), lambda i: (i, 0)),
                   pl.BlockSpec((A_GROUPS, CHUNK, CHUNK), lambda i: (0, 0, 0)),
                   pl.BlockSpec((CHUNK, A_GROUPS), lambda i: (0, 0)),
                   pl.BlockSpec((1, W), lambda i: (0, 0))],
        out_shape=[jax.ShapeDtypeStruct((M, 3 * W), ACT),
                   jax.ShapeDtypeStruct((A_GROUPS, CHUNK, CHUNK), F32),
                   jax.ShapeDtypeStruct((CHUNK, A_GROUPS), F32),
                   jax.ShapeDtypeStruct((1, W), F32)],
        scratch_shapes=[pltpu.VMEM((tm, W), F32)],
        compiler_params=_cp("arbitrary"),
    )(dout, w_out, proj, v_gain, w_s, b_st)


def _pool_diff(xg, tail, i, tm, w):
    t = lax.broadcasted_iota(jnp.int32, (tm, tm + POOL_HALO), 0)
    s = lax.broadcasted_iota(jnp.int32, (tm, tm + POOL_HALO), 1)
    off = t - (s - POOL_HALO)
    band = jnp.where((off >= 0) & (off < w), 1.0, 0.0).astype(MXU)
    tail = jnp.where(i > 0, tail, jnp.zeros_like(tail))
    ext = jnp.concatenate([tail, xg], axis=0)
    ssum = jnp.dot(band, ext.astype(MXU), preferred_element_type=F32)
    tglob = i * tm + lax.broadcasted_iota(jnp.int32, (tm, 1), 0)
    cnt = jnp.minimum(tglob + 1, w).astype(F32)
    return ssum / cnt - xg.astype(F32)


def _c_mid(proj, w_grp, scale, name):
    M = proj.shape[0]
    W = proj.shape[1] // 2
    ng = len(POOL_SIZES)
    cg = W // ng
    tm = min(M, 256)
    hb = tm // POOL_HALO

    def body(xc_ref, tail_ref, z_ref, wg_ref, sc_ref, y_ref):
        i = pl.program_id(0)
        for g, w in enumerate(POOL_SIZES):
            cols = slice(g * cg, (g + 1) * cg)
            d = _pool_diff(xc_ref[:, cols], tail_ref[:, cols], i, tm, w)
            mixed = _dot(d, wg_ref[g]) * sc_ref[:, cols]
            z = z_ref[:, cols].astype(F32)
            y_ref[:, cols] = (mixed * (z * _sigmoid(z))).astype(y_ref.dtype)

    return pl.pallas_call(
        body, name=name, grid=(M // tm,),
        in_specs=[pl.BlockSpec((tm, W), lambda i: (i, 0)),
                  pl.BlockSpec((POOL_HALO, W), lambda i: (jnp.maximum(i * hb - 1, 0), 0)),
                  pl.BlockSpec((tm, W), lambda i: (i, 1)),
                  pl.BlockSpec((ng, cg, cg), lambda i: (0, 0, 0)),
                  pl.BlockSpec((1, W), lambda i: (0, 0))],
        out_specs=pl.BlockSpec((tm, W), lambda i: (i, 0)),
        out_shape=jax.ShapeDtypeStruct((M, W), ACT),
        compiler_params=_cp("parallel"),
    )(proj, proj, proj, w_grp, scale)


def _c_bwd1(dout, w_out, proj, w_grp, scale, name):
    M = proj.shape[0]
    W = proj.shape[1] // 2
    Dm = dout.shape[1]
    ng = len(POOL_SIZES)
    cg = W // ng
    rl = cg // N_DEV
    tm = min(M, 256)
    hb = tm // POOL_HALO
    nt = M // tm

    def body(do_ref, wo_ref, xc_ref, tail_ref, z_ref, wg_ref, sc_ref, dd_ref, dz_ref, dwg_ref, dsc_ref, acc_ref):
        i = pl.program_id(0)

        @pl.when(i == 0)
        def _():
            acc_ref[...] = jnp.zeros_like(acc_ref)
            dsc_ref[...] = jnp.zeros_like(dsc_ref)

        dy = _dot_nt(do_ref[...], wo_ref[...])
        for g, w in enumerate(POOL_SIZES):
            cols = slice(g * cg, (g + 1) * cg)
            d = _pool_diff(xc_ref[:, cols], tail_ref[:, cols], i, tm, w)
            mr = _dot(d, wg_ref[g])
            sc = sc_ref[:, cols]
            z = z_ref[:, cols].astype(F32)
            sig = _sigmoid(z)
            dyg = dy[:, cols]
            dmixed = dyg * (z * sig)
            dz_ref[:, cols] = (dyg * (mr * sc) * (sig * (1.0 + z * (1.0 - sig)))).astype(dz_ref.dtype)
            dsc_ref[:, cols] += jnp.sum(dmixed * mr, axis=0, keepdims=True)
            dmr = (dmixed * sc).astype(MXU)
            acc_ref[g] += _dot_tn(d, dmr)
            dd_ref[:, cols] = _dot_nt(dmr, wg_ref[g]).astype(dd_ref.dtype)

        @pl.when(i == nt - 1)
        def _():
            for dev in range(N_DEV):
                for g in range(ng):
                    dwg_ref[_chunk_slot(dev), g] = acc_ref[g, dev * rl:(dev + 1) * rl, :]

    return pl.pallas_call(
        body, name=name, grid=(nt,),
        in_specs=[pl.BlockSpec((tm, Dm), lambda i: (i, 0)),
                  pl.BlockSpec((W, Dm), lambda i: (0, 0)),
                  pl.BlockSpec((tm, W), lambda i: (i, 0)),
                  pl.BlockSpec((POOL_HALO, W), lambda i: (jnp.maximum(i * hb - 1, 0), 0)),
                  pl.BlockSpec((tm, W), lambda i: (i, 1)),
                  pl.BlockSpec((ng, cg, cg), lambda i: (0, 0, 0)),
                  pl.BlockSpec((1, W), lambda i: (0, 0))],
        out_specs=[pl.BlockSpec((tm, W), lambda i: (i, 0)),
                   pl.BlockSpec((tm, W), lambda i: (i, 0)),
                   pl.BlockSpec((N_DEV, ng, rl, cg), lambda i: (0, 0, 0, 0)),
                   pl.BlockSpec((1, W), lambda i: (0, 0))],
        out_shape=[jax.ShapeDtypeStruct((M, W), ACT), jax.ShapeDtypeStruct((M, W), ACT),
                   jax.ShapeDtypeStruct((N_DEV, ng, rl, cg), F32), jax.ShapeDtypeStruct((1, W), F32)],
        scratch_shapes=[pltpu.VMEM((ng, cg, cg), F32)],
        compiler_params=_cp("arbitrary"),
    )(dout, w_out, proj, proj, proj, w_grp, scale)


def _c_bwd2(dd, dz, name):
    M, W = dd.shape
    ng = len(POOL_SIZES)
    cg = W // ng
    tm = min(M, 256)
    hb = tm // POOL_HALO
    nt = M // tm

    def body(dd_ref, head_ref, dz_ref, dp_ref):
        i = pl.program_id(0)
        s = lax.broadcasted_iota(jnp.int32, (tm, tm + POOL_HALO), 0)
        t = lax.broadcasted_iota(jnp.int32, (tm, tm + POOL_HALO), 1)
        off = t - s
        tglob = i * tm + lax.broadcasted_iota(jnp.int32, (tm + POOL_HALO, 1), 0)
        for g, w in enumerate(POOL_SIZES):
            cols = slice(g * cg, (g + 1) * cg)
            ddg = dd_ref[:, cols].astype(F32)
            head = head_ref[:, cols].astype(F32)
            head = jnp.where(i < nt - 1, head, jnp.zeros_like(head))
            cnt = jnp.minimum(tglob + 1, w).astype(F32)
            ext = (jnp.concatenate([ddg, head], axis=0) / cnt).astype(MXU)
            band = jnp.where((off >= 0) & (off < w), 1.0, 0.0).astype(MXU)
            dp_ref[:, cols] = (jnp.dot(band, ext, preferred_element_type=F32) - ddg).astype(dp_ref.dtype)
        dp_ref[:, W:] = dz_ref[...]

    return pl.pallas_call(
        body, name=name, grid=(nt,),
        in_specs=[pl.BlockSpec((tm, W), lambda i: (i, 0)),
                  pl.BlockSpec((POOL_HALO, W), lambda i: (jnp.minimum((i + 1) * hb, M // POOL_HALO - 1), 0)),
                  pl.BlockSpec((tm, W), lambda i: (i, 0))],
        out_specs=pl.BlockSpec((tm, 2 * W), lambda i: (i, 0)),
        out_shape=jax.ShapeDtypeStruct((M, 2 * W), ACT),
        compiler_params=_cp("parallel"),
    )(dd, dd, dz)


def _rope_tables(S):
    half = ROPE_DIM // 2
    inv_freq = jnp.power(jnp.float32(ROPE_THETA), -jnp.arange(half, dtype=F32) / half)
    ang = jnp.arange(S, dtype=F32)[:, None] * inv_freq[None, :]
    cos, sin = jnp.cos(ang), jnp.sin(ang)
    rest = HEAD_DIM - ROPE_DIM
    cf = jnp.concatenate([cos, cos, jnp.ones((S, rest), F32)], axis=1)
    sf = jnp.concatenate([-sin, sin, jnp.zeros((S, rest), F32)], axis=1)
    return cf, sf


def _swap_matrix():
    half = ROPE_DIM // 2
    a = lax.broadcasted_iota(jnp.int32, (HEAD_DIM, HEAD_DIM), 0)
    e = lax.broadcasted_iota(jnp.int32, (HEAD_DIM, HEAD_DIM), 1)
    hit = ((e < half) & (a == e + half)) | ((e >= half) & (e < 2 * half) & (a == e - half))
    return jnp.where(hit, 1.0, 0.0).astype(MXU)


def _b_qk_fwd(proj, tables, gains, name):
    M = proj.shape[0]
    nsl = 2 * len(B_DILATIONS) * B_HEADS
    Wqk = nsl * HEAD_DIM
    tm = min(M, 256)

    def body(p_ref, cf_ref, sf_ref, g_ref, o_ref):
        cf, sf = cf_ref[...], sf_ref[...]
        swap = _swap_matrix()
        for j in range(nsl):
            cols = slice(j * HEAD_DIM, (j + 1) * HEAD_DIM)
            xv = p_ref[:, cols].astype(F32)
            r = lax.rsqrt(jnp.mean(xv * xv, axis=-1, keepdims=True) + EPS)
            xg = xv * g_ref[j // B_HEADS:j // B_HEADS + 1, :]
            hi = xg.astype(MXU)
            lo = (xg - hi.astype(F32)).astype(MXU)
            sw = jnp.dot(hi, swap, preferred_element_type=F32) + jnp.dot(lo, swap, preferred_element_type=F32)
            o_ref[:, cols] = (r * (xg * cf + sw * sf)).astype(o_ref.dtype)

    tspec = pl.BlockSpec((tm, HEAD_DIM), lambda i: (i, 0))
    return pl.pallas_call(
        body, name=name, grid=(M // tm,),
        in_specs=[pl.BlockSpec((tm, Wqk), lambda i: (i, 0)), tspec, tspec,
                  pl.BlockSpec((8, HEAD_DIM), lambda i: (0, 0))],
        out_specs=pl.BlockSpec((tm, Wqk), lambda i: (i, 0)),
        out_shape=jax.ShapeDtypeStruct((M, Wqk), ACT),
        compiler_params=_cp("parallel"),
    )(proj, *tables, gains)


def _b_qk_bwd(dqs, dks, proj, tables, gains, dproj, name):
    M = proj.shape[0]
    ngr = len(B_DILATIONS)
    nsl = 2 * ngr * B_HEADS
    Wqk = nsl * HEAD_DIM
    Wg = B_HEADS * HEAD_DIM
    tm = min(M, 256)

    def body(*refs):
        d_refs = refs[:2 * ngr]
        p_ref, cf_ref, sf_ref, g_ref = refs[2 * ngr:2 * ngr + 4]
        dp_ref, dg_ref = refs[-2], refs[-1]

        @pl.when(pl.program_id(0) == 0)
        def _():
            dg_ref[...] = jnp.zeros_like(dg_ref)

        cf, sf = cf_ref[...], sf_ref[...]
        swap = _swap_matrix()
        for j in range(nsl):
            t, hh = j // B_HEADS, j % B_HEADS
            cols = slice(j * HEAD_DIM, (j + 1) * HEAD_DIM)
            dy = d_refs[t][:, hh * HEAD_DIM:(hh + 1) * HEAD_DIM].astype(F32)
            dxn = dy * cf + jnp.dot((dy * sf).astype(MXU), swap, preferred_element_type=F32)
            xv = p_ref[:, cols].astype(F32)
            r = lax.rsqrt(jnp.mean(xv * xv, axis=-1, keepdims=True) + EPS)
            xh = xv * r
            dg_ref[t:t + 1, :] += jnp.sum(dxn * xh, axis=0, keepdims=True)
            dxh = dxn * g_ref[t:t + 1, :]
            dp_ref[:, cols] = (r * (dxh - xh * jnp.mean(dxh * xh, axis=-1, keepdims=True))).astype(dp_ref.dtype)

    tspec = pl.BlockSpec((tm, HEAD_DIM), lambda i: (i, 0))
    dspec = pl.BlockSpec((tm, Wg), lambda i: (i, 0))
    n_in = 2 * ngr + 5
    return pl.pallas_call(
        body, name=name, grid=(M // tm,),
        in_specs=[dspec] * (2 * ngr) + [pl.BlockSpec((tm, Wqk), lambda i: (i, 0)), tspec, tspec,
                                        pl.BlockSpec((8, HEAD_DIM), lambda i: (0, 0)),
                                        pl.BlockSpec(memory_space=pl.ANY)],
        out_specs=[pl.BlockSpec((tm, Wqk), lambda i: (i, 0)), pl.BlockSpec((8, HEAD_DIM), lambda i: (0, 0))],
        out_shape=[jax.ShapeDtypeStruct(dproj.shape, dproj.dtype), jax.ShapeDtypeStruct((8, HEAD_DIM), F32)],
        input_output_aliases={n_in - 1: 0},
        compiler_params=_cp("arbitrary"),
    )(*dqs, *dks, proj, *tables, gains, dproj)


def _attn_tile(D, M):
    return max(HEAD_DIM * D, min(M, 2048))


def _strided(start, size, D):
    return pl.ds(start, size) if D == 1 else pl.ds(start, size, stride=D)


def _attn_mask(base):
    qi = lax.broadcasted_iota(jnp.int32, (CHUNK, 2 * CHUNK), 0)
    ki = lax.broadcasted_iota(jnp.int32, (CHUNK, 2 * CHUNK), 1)
    return (ki >= qi) & (ki <= qi + CHUNK) & (ki >= CHUNK - base)


def _b_attn_fwd(qk, proj, g, name):
    M = qk.shape[0]
    D = B_DILATIONS[g]
    ngr = len(B_DILATIONS)
    T = _attn_tile(D, M)
    P = HEAD_DIM * D
    nsb = T // P
    Wg = B_HEADS * HEAD_DIM
    scale = np.float32(1.0 / np.sqrt(HEAD_DIM))

    def body(q_ref, k_ref, v_ref, o_ref, l_ref, qs, ks, vs, os_, ls):
        n = pl.program_id(1)

        @pl.when(n == 0)
        def _():
            ks[0:P, :] = jnp.zeros((P, HEAD_DIM), F32)
            vs[0:P, :] = jnp.zeros((P, HEAD_DIM), F32)

        qs[...] = q_ref[...].astype(F32)
        ks[P:P + T, :] = k_ref[...].astype(F32)
        vs[P:P + T, :] = v_ref[...].astype(F32)

        for b in range(nsb):
            mask = _attn_mask(n * (T // D) + b * CHUNK)
            for r in range(D):
                start = b * P + r
                q = qs[_strided(start, CHUNK, D), :]
                k = ks[_strided(start, 2 * CHUNK, D), :]
                v = vs[_strided(start, 2 * CHUNK, D), :]
                s = jnp.where(mask, _dot_nt(q, k) * scale, NEG)
                m = jnp.max(s, axis=-1, keepdims=True)
                p = jnp.exp(s - m)
                l = jnp.sum(p, axis=-1, keepdims=True)
                o = _dot(p, v) / l
                os_[_strided(start, CHUNK, D), :] = o
                ls[_strided(start, CHUNK, D), :] = jnp.broadcast_to(m + jnp.log(l), (CHUNK, HEAD_DIM))

        o_ref[...] = os_[...].astype(o_ref.dtype)
        l_ref[...] = ls[...]
        ks[0:P, :] = ks[T:T + P, :]
        vs[0:P, :] = vs[T:T + P, :]

    blk = (T, HEAD_DIM)
    return pl.pallas_call(
        body, name=name, grid=(B_HEADS, M // T),
        in_specs=[pl.BlockSpec(blk, lambda h, n: (n, g * B_HEADS + h)),
                  pl.BlockSpec(blk, lambda h, n: (n, (ngr + g) * B_HEADS + h)),
                  pl.BlockSpec(blk, lambda h, n: (n, (2 * ngr + g) * B_HEADS + h))],
        out_specs=[pl.BlockSpec(blk, lambda h, n: (n, h)), pl.BlockSpec(blk, lambda h, n: (n, h))],
        out_shape=[jax.ShapeDtypeStruct((M, Wg), ACT), jax.ShapeDtypeStruct((M, Wg), F32)],
        scratch_shapes=[pltpu.VMEM((T, HEAD_DIM), F32), pltpu.VMEM((P + T, HEAD_DIM), F32),
                        pltpu.VMEM((P + T, HEAD_DIM), F32), pltpu.VMEM((T, HEAD_DIM), F32),
                        pltpu.VMEM((T, HEAD_DIM), F32)],
        compiler_params=_cp("parallel", "arbitrary"),
    )(qk, qk, proj)


def _b_combine(os_, ls, proj, name):
    M, Wg = os_[0].shape
    ngr = len(B_DILATIONS)
    tm = min(M, 512)

    def body(*refs):
        o_refs, l_refs, z_ref = refs[:ngr], refs[ngr:2 * ngr], refs[2 * ngr]
        y_ref, o_ref, lse_ref = refs[2 * ngr + 1:]
        ls_ = [r[...] for r in l_refs]
        m = functools.reduce(jnp.maximum, ls_)
        es = [jnp.exp(l - m) for l in ls_]
        tot = functools.reduce(lambda a, b: a + b, es)
        o = functools.reduce(lambda a, b: a + b, [e * r[...].astype(F32) for e, r in zip(es, o_refs)]) / tot
        z = z_ref[...].astype(F32)
        y_ref[...] = (o.astype(F32) * (z * _sigmoid(z))).astype(y_ref.dtype)
        o_ref[...] = o.astype(o_ref.dtype)
        lse_ref[...] = m + jnp.log(tot)

    spec = pl.BlockSpec((tm, Wg), lambda i: (i, 0))
    return pl.pallas_call(
        body, name=name, grid=(M // tm,),
        in_specs=[spec] * (2 * ngr) + [pl.BlockSpec((tm, Wg), lambda i: (i, 3 * ngr))],
        out_specs=[spec] * 3,
        out_shape=[jax.ShapeDtypeStruct((M, Wg), ACT), jax.ShapeDtypeStruct((M, Wg), ACT),
                   jax.ShapeDtypeStruct((M, Wg), F32)],
        compiler_params=_cp("parallel"),
    )(*os_, *ls, proj)


def _b_bwd_pre(dout, w_out, o, proj, name):
    M, Wg = o.shape
    Dm = dout.shape[1]
    ngr = len(B_DILATIONS)
    tm = min(M, 512)

    def body(do_ref, wo_ref, o_ref, z_ref, dov_ref, dl_ref, dp_ref):
        dy = _dot_nt(do_ref[...], wo_ref[...])
        z = z_ref[...].astype(F32)
        sig = _sigmoid(z)
        ov = o_ref[...].astype(F32)
        dp_ref[...] = (dy * ov * (sig * (1.0 + z * (1.0 - sig)))).astype(dp_ref.dtype)
        dov = dy * (z * sig)
        dov_ref[...] = dov.astype(dov_ref.dtype)
        prod = dov * ov
        for h in range(B_HEADS):
            cols = slice(h * HEAD_DIM, (h + 1) * HEAD_DIM)
            dl_ref[:, cols] = jnp.broadcast_to(jnp.sum(prod[:, cols], axis=-1, keepdims=True), (tm, HEAD_DIM))

    spec = pl.BlockSpec((tm, Wg), lambda i: (i, 0))
    zspec = pl.BlockSpec((tm, Wg), lambda i: (i, 3 * ngr))
    return pl.pallas_call(
        body, name=name, grid=(M // tm,),
        in_specs=[pl.BlockSpec((tm, Dm), lambda i: (i, 0)), pl.BlockSpec((Wg, Dm), lambda i: (0, 0)), spec, zspec],
        out_specs=[spec, spec, zspec],
        out_shape=[jax.ShapeDtypeStruct((M, Wg), ACT), jax.ShapeDtypeStruct((M, Wg), F32),
                   jax.ShapeDtypeStruct(proj.shape, ACT)],
        compiler_params=_cp("parallel"),
    )(dout, w_out, o, proj)


def _b_attn_bwd(qk, proj, dov, lse, delta, dproj, g, name):
    M = qk.shape[0]
    D = B_DILATIONS[g]
    ngr = len(B_DILATIONS)
    T = _attn_tile(D, M)
    P = HEAD_DIM * D
    nsb = T // P
    nt = M // T
    Wg = B_HEADS * HEAD_DIM
    scale = np.float32(1.0 / np.sqrt(HEAD_DIM))
    shift = T - P

    def body(q_ref, k_ref, v_ref, do_ref, l_ref, dl_ref, dp_any, dq_ref, dk_ref, dv_ref,
             qs, dos, lss, dls, ks, vs, dqs, dks, dvs):
        n = pl.program_id(1)

        @pl.when(n == 0)
        def _():
            ks[0:P, :] = jnp.zeros((P, HEAD_DIM), F32)
            vs[0:P, :] = jnp.zeros((P, HEAD_DIM), F32)
            dks[0:T, :] = jnp.zeros((T, HEAD_DIM), F32)
            dvs[0:T, :] = jnp.zeros((T, HEAD_DIM), F32)

        dks[T:2 * T, :] = jnp.zeros((T, HEAD_DIM), F32)
        dvs[T:2 * T, :] = jnp.zeros((T, HEAD_DIM), F32)

        @pl.when(n < nt)
        def _():
            qs[...] = q_ref[...].astype(F32)
            dos[...] = do_ref[...].astype(F32)
            lss[...] = l_ref[...]
            dls[...] = dl_ref[...]
            ks[P:P + T, :] = k_ref[...].astype(F32)
            vs[P:P + T, :] = v_ref[...].astype(F32)

            for b in range(nsb):
                mask = _attn_mask(n * (T // D) + b * CHUNK)
                for r in range(D):
                    start = b * P + r
                    qsl = _strided(start, CHUNK, D)
                    ksl = _strided(start, 2 * CHUNK, D)
                    dsl = _strided(start + shift, 2 * CHUNK, D)
                    q = qs[qsl, :]
                    do = dos[qsl, :]
                    k = ks[ksl, :]
                    v = vs[ksl, :]
                    s = _dot_nt(q, k) * scale
                    p = jnp.where(mask, jnp.exp(s - lss[qsl, :][:, :1]), 0.0)
                    dvs[dsl, :] += _dot_tn(p, do)
                    dp = _dot_nt(do, v)
                    ds = (p * (dp - dls[qsl, :][:, :1]) * scale).astype(MXU)
                    dqs[qsl, :] = _dot(ds, k)
                    dks[dsl, :] += _dot_tn(ds, q)

        dq_ref[...] = dqs[...].astype(dq_ref.dtype)
        dk_ref[...] = dks[0:T, :].astype(dk_ref.dtype)
        dv_ref[...] = dvs[0:T, :].astype(dv_ref.dtype)
        dks[0:T, :] = dks[T:2 * T, :]
        dvs[0:T, :] = dvs[T:2 * T, :]
        ks[0:P, :] = ks[T:T + P, :]
        vs[0:P, :] = vs[T:T + P, :]

    blk = (T, HEAD_DIM)
    cur = lambda n: jnp.minimum(n, nt - 1)
    prv = lambda n: jnp.maximum(n - 1, 0)
    return pl.pallas_call(
        body, name=name, grid=(B_HEADS, nt + 1),
        in_specs=[pl.BlockSpec(blk, lambda h, n: (cur(n), g * B_HEADS + h)),
                  pl.BlockSpec(blk, lambda h, n: (cur(n), (ngr + g) * B_HEADS + h)),
                  pl.BlockSpec(blk, lambda h, n: (cur(n), (2 * ngr + g) * B_HEADS + h)),
                  pl.BlockSpec(blk, lambda h, n: (cur(n), h)),
                  pl.BlockSpec(blk, lambda h, n: (cur(n), h)),
                  pl.BlockSpec(blk, lambda h, n: (cur(n), h)),
                  pl.BlockSpec(memory_space=pl.ANY)],
        out_specs=[pl.BlockSpec(blk, lambda h, n: (cur(n), h)),
                   pl.BlockSpec(blk, lambda h, n: (prv(n), h)),
                   pl.BlockSpec(blk, lambda h, n: (prv(n), (2 * ngr + g) * B_HEADS + h))],
        out_shape=[jax.ShapeDtypeStruct((M, Wg), ACT), jax.ShapeDtypeStruct((M, Wg), ACT),
                   jax.ShapeDtypeStruct(dproj.shape, dproj.dtype)],
        scratch_shapes=[pltpu.VMEM((T, HEAD_DIM), F32)] * 4
        + [pltpu.VMEM((P + T, HEAD_DIM), F32)] * 2
        + [pltpu.VMEM((T, HEAD_DIM), F32)]
        + [pltpu.VMEM((2 * T, HEAD_DIM), F32)] * 2,
        input_output_aliases={6: 2},
        compiler_params=_cp("parallel", "arbitrary"),
    )(qk, qk, proj, dov, lse, delta, dproj)


def _coords():
    return lax.axis_index("x"), lax.axis_index("y"), lax.axis_index("c")


def _gather_blocks(x_ref, out_ref, send_sems, recv_sems, local_sem):
    x, y, c = _coords()
    me, sibling = (x, y, c), (x, y, 1 - c)
    chips = [(1 - x, y), (x, 1 - y), (1 - x, 1 - y)]

    def slot(px, py, pc):
        return out_ref.at[4 * px + 2 * py + pc]

    def copy(k, block, to, src=None):
        return pltpu.make_async_remote_copy(
            src_ref=slot(*block) if src is None else src, dst_ref=slot(*block),
            send_sem=send_sems.at[k], recv_sem=recv_sems.at[k], device_id=to, device_id_type=MESH)

    mine_cp = pltpu.make_async_copy(x_ref, slot(*me), local_sem)
    mine_cp.start()
    first = [copy(0, me, sibling, src=x_ref)]
    first += [copy(1 + j, me, (*chip, c), src=x_ref) for j, chip in enumerate(chips)]
    for cp in first:
        cp.start()
    passed = [copy(4 + j, (*chip, c), sibling) for j, chip in enumerate(chips)]
    for j, chip in enumerate(chips):
        copy(1 + j, (*chip, c), me).wait_recv()
        passed[j].start()
    copy(0, sibling, me).wait_recv()
    for j, chip in enumerate(chips):
        copy(4 + j, (*chip, 1 - c), me).wait_recv()
    for cp in first + passed:
        cp.wait_send()
    mine_cp.wait()


def _all_gather_hbm(mine):
    R, C = mine.shape

    def body(x_ref, out_ref, send_sems, recv_sems, local_sem):
        _gather_blocks(x_ref, out_ref, send_sems, recv_sems, local_sem)

    return pl.pallas_call(
        body, name="ag_weights",
        in_specs=[pl.BlockSpec(memory_space=pl.ANY)],
        out_specs=pl.BlockSpec(memory_space=pl.ANY),
        out_shape=jax.ShapeDtypeStruct((N_DEV, R, C), mine.dtype),
        scratch_shapes=[pltpu.SemaphoreType.DMA((7,)), pltpu.SemaphoreType.DMA((7,)), pltpu.SemaphoreType.DMA],
    )(mine)


def _all_reduce_small(part):
    R, C = part.shape

    def body(x_ref, tot_ref, gath, send_sems, recv_sems, local_sem):
        _gather_blocks(x_ref, gath, send_sems, recv_sems, local_sem)
        acc = gath[0]
        for d in range(1, N_DEV):
            acc = acc + gath[d]
        tot_ref[...] = acc

    return pl.pallas_call(
        body, name="ar_small",
        in_specs=[pl.BlockSpec(memory_space=pltpu.VMEM)],
        out_specs=pl.BlockSpec(memory_space=pltpu.VMEM),
        out_shape=jax.ShapeDtypeStruct((R, C), F32),
        scratch_shapes=[pltpu.VMEM((N_DEV, R, C), F32),
                        pltpu.SemaphoreType.DMA((7,)), pltpu.SemaphoreType.DMA((7,)), pltpu.SemaphoreType.DMA],
        compiler_params=pltpu.CompilerParams(vmem_limit_bytes=VMEM_LIMIT),
    )(part)


def _rs_sibling(grads):
    n = len(grads)

    def body(*refs):
        g_refs, r_refs = refs[:n], refs[n:2 * n]
        send_sems, recv_sems = refs[2 * n], refs[2 * n + 1]
        x, y, c = _coords()
        cps = []
        for a in range(n):
            cp = pltpu.make_async_remote_copy(
                src_ref=g_refs[a].at[pl.ds(4 * (1 - c), 4)], dst_ref=r_refs[a],
                send_sem=send_sems.at[a], recv_sem=recv_sems.at[a], device_id=(x, y, 1 - c), device_id_type=MESH)
            cp.start()
            cps.append(cp)
        for cp in cps:
            cp.wait()

    return pl.pallas_call(
        body, name="rs_sibling",
        in_specs=[pl.BlockSpec(memory_space=pl.ANY)] * n,
        out_specs=[pl.BlockSpec(memory_space=pl.ANY)] * n,
        out_shape=[jax.ShapeDtypeStruct((4,) + g.shape[1:], g.dtype) for g in grads],
        scratch_shapes=[pltpu.SemaphoreType.DMA((n,)), pltpu.SemaphoreType.DMA((n,))],
    )(*grads)


def _rs_chips(parts):
    n = len(parts)

    def body(*refs):
        s_refs, r_refs = refs[:n], refs[n:2 * n]
        send_sems, recv_sems = refs[2 * n], refs[2 * n + 1]
        x, y, c = _coords()
        peers = [(x, 1 - y, c), (1 - x, y, c), (1 - x, 1 - y, c)]
        cps = []
        for a in range(n):
            for k, peer in enumerate(peers):
                cp = pltpu.make_async_remote_copy(
                    src_ref=s_refs[a].at[k], dst_ref=r_refs[a].at[k],
                    send_sem=send_sems.at[3 * a + k], recv_sem=recv_sems.at[3 * a + k],
                    device_id=peer, device_id_type=MESH)
                cp.start()
                cps.append(cp)
        for cp in cps:
            cp.wait()

    return pl.pallas_call(
        body, name="rs_chips",
        in_specs=[pl.BlockSpec(memory_space=pl.ANY)] * n,
        out_specs=[pl.BlockSpec(memory_space=pl.ANY)] * n,
        out_shape=[jax.ShapeDtypeStruct(p.shape, p.dtype) for p in parts],
        scratch_shapes=[pltpu.SemaphoreType.DMA((3 * n,)), pltpu.SemaphoreType.DMA((3 * n,))],
    )(*parts)


def _row_tile(rows, cols):
    tr = min(rows, 1 << int(np.log2((1 << 18) // cols)))
    assert rows % tr == 0
    return tr


def _chip_partials(coords, g, r1, name):
    _, rows, C = g.shape
    tr = _row_tile(rows, C)

    def body(co_ref, g_ref, r_ref, o_ref):
        o_ref[...] = (g_ref[...] + r_ref[...]).astype(o_ref.dtype)

    def chip(k, co):
        return jnp.bitwise_xor(2 * co[0] + co[1], k + 1)

    return pl.pallas_call(
        body, name=name,
        grid_spec=pltpu.PrefetchScalarGridSpec(
            num_scalar_prefetch=1, grid=(3, rows // tr),
            in_specs=[pl.BlockSpec((None, tr, C), lambda k, t, co: (4 * co[2] + chip(k, co), t, 0)),
                      pl.BlockSpec((None, tr, C), lambda k, t, co: (chip(k, co), t, 0))],
            out_specs=pl.BlockSpec((None, tr, C), lambda k, t, co: (k, t, 0))),
        out_shape=jax.ShapeDtypeStruct((3, rows, C), WIRE),
        compiler_params=_cp("parallel", "parallel"),
    )(coords, g, r1)


def _adam_math(w, g, m, v):
    m = ADAM_B1 * m + (1.0 - ADAM_B1) * g
    v = ADAM_B2 * v + (1.0 - ADAM_B2) * (g * g)
    m_hat = m / (1.0 - ADAM_B1 ** ADAM_STEP)
    v_hat = v / (1.0 - ADAM_B2 ** ADAM_STEP)
    delta = -ADAM_LR * (m_hat / (jnp.sqrt(v_hat) + ADAM_EPS) + ADAM_WD * w)
    return delta, m, v


def _adamw_sharded(coords, w, m, v, g, r1, r2, name):
    rows, C = w.shape
    tr = _row_tile(rows, C)

    def body(co_ref, w_ref, m_ref, v_ref, g_ref, r1_ref, r2_ref, go_ref, d_ref, mo_ref, vo_ref):
        grad = g_ref[...] + r1_ref[...]
        for k in range(3):
            grad = grad + r2_ref[k].astype(F32)
        go_ref[...] = grad
        d_ref[...], mo_ref[...], vo_ref[...] = _adam_math(w_ref[...], grad, m_ref[...], v_ref[...])

    spec = pl.BlockSpec((tr, C), lambda t, co: (t, 0))
    return pl.pallas_call(
        body, name=name,
        grid_spec=pltpu.PrefetchScalarGridSpec(
            num_scalar_prefetch=1, grid=(rows // tr,),
            in_specs=[spec, spec, spec,
                      pl.BlockSpec((None, tr, C), lambda t, co: (4 * co[2] + 2 * co[0] + co[1], t, 0)),
                      pl.BlockSpec((None, tr, C), lambda t, co: (2 * co[0] + co[1], t, 0)),
                      pl.BlockSpec((3, tr, C), lambda t, co: (0, t, 0))],
            out_specs=[spec] * 4),
        out_shape=[jax.ShapeDtypeStruct((rows, C), F32)] * 4,
        compiler_params=_cp("parallel"),
    )(coords, w, m, v, g, r1, r2)


def _adamw_small(w, g, m, v, name):
    def body(w_ref, g_ref, m_ref, v_ref, d_ref, mo_ref, vo_ref):
        d_ref[...], mo_ref[...], vo_ref[...] = _adam_math(w_ref[...], g_ref[...], m_ref[...], v_ref[...])

    return pl.pallas_call(
        body, name=name, out_shape=[jax.ShapeDtypeStruct(w.shape, F32)] * 3,
        in_specs=[pl.BlockSpec(memory_space=pltpu.VMEM)] * 4,
        out_specs=[pl.BlockSpec(memory_space=pltpu.VMEM)] * 3,
    )(w, g, m, v)


def _wire_rows(a, cols):
    if a.dtype != WIRE:
        a = lax.bitcast_convert_type(a, WIRE)
    flat = a.reshape(-1)
    pad = (-flat.shape[0]) % (16 * cols)
    if pad:
        flat = jnp.concatenate([flat, jnp.zeros((pad,), WIRE)])
    return flat.reshape(-1, cols)


def _local_step(xs, tgt, W, norm_gain, a_w_s, a_b_s, b_gains):
    S = xs.shape[0]
    tables = _rope_tables(S)
    ng = lambda i: norm_gain[i:i + 1]
    ngr = len(B_DILATIONS)
    bst = [a_b_s[l].T for l in range(2)]

    h0, p0 = _norm_proj(xs, ng(0), W["a_in"][0], "l0_proj")
    y0 = _a_mid(p0, W["a_vg"][0:1], a_w_s[0], bst[0], "l0_mid")
    x1 = _out_proj(xs, y0, W["a_out"][0], "l0_out")
    h1, p1 = _norm_proj(x1, ng(1), W["b_in"], "l1_proj")
    qk = _b_qk_fwd(p1, tables, b_gains, "l1_qk")
    ogs, lgs = zip(*[_b_attn_fwd(qk, p1, g, f"l1_attn{g}") for g in range(ngr)])
    y1, o1, lse = _b_combine(ogs, lgs, p1, "l1_comb")
    x2 = _out_proj(x1, y1, W["b_out"], "l1_out")
    h2, p2 = _norm_proj(x2, ng(2), W["c_in"], "l2_proj")
    y2 = _c_mid(p2, W["c_grp"], W["c_sc"], "l2_mid")
    x3 = _out_proj(x2, y2, W["c_out"], "l2_out")
    h3, p3 = _norm_proj(x3, ng(3), W["a_in"][1], "l3_proj")
    y3 = _a_mid(p3, W["a_vg"][1:2], a_w_s[1], bst[1], "l3_mid")
    x4 = _out_proj(x3, y3, W["a_out"][1], "l3_out")
    loss, dx4 = _loss_head(x4, tgt)

    dp3, dws1, dbs1, dvg1 = _a_bwd(dx4, W["a_out"][1], p3, W["a_vg"][1:2], a_w_s[1], bst[1], "l3_bwd")
    ga_out = _dw_out(y3, dx4, None, 1, 2, "l3_dwout")
    ga_in = _dw_in(h3, dp3, None, 1, 2, "l3_dwin")
    dx3, dg3 = _dh_norm_bwd(dp3, W["a_in"][1], x3, ng(3), dx4, "l3_dh")

    dd, dz, gc_grp, dsc = _c_bwd1(dx3, W["c_out"], p2, W["c_grp"], W["c_sc"], "l2_bwd1")
    dp2 = _c_bwd2(dd, dz, "l2_bwd2")
    gc_out = _dw_out(y2, dx3, None, 0, 1, "l2_dwout")
    gc_in = _dw_in(h2, dp2, None, 0, 1, "l2_dwin")
    dx2, dg2 = _dh_norm_bwd(dp2, W["c_in"], x2, ng(2), dx3, "l2_dh")

    dov, delta, dp1 = _b_bwd_pre(dx2, W["b_out"], o1, p1, "l1_bwdpre")
    dqs, dks = [], []
    for g in range(ngr):
        dq, dk, dp1 = _b_attn_bwd(qk, p1, dov, lse, delta, dp1, g, f"l1_attnbwd{g}")
        dqs.append(dq)
        dks.append(dk)
    dp1, dgains = _b_qk_bwd(dqs, dks, p1, tables, b_gains, dp1, "l1_qkbwd")
    gb_out = _dw_out(y1, dx2, None, 0, 1, "l1_dwout")
    gb_in = _dw_in(h1, dp1, None, 0, 1, "l1_dwin")
    dx1, dg1 = _dh_norm_bwd(dp1, W["b_in"], x1, ng(1), dx2, "l1_dh")

    dp0, dws0, dbs0, dvg0 = _a_bwd(dx1, W["a_out"][0], p0, W["a_vg"][0:1], a_w_s[0], bst[0], "l0_bwd")
    ga_out = _dw_out(y0, dx1, ga_out, 0, 2, "l0_dwout")
    ga_in = _dw_in(h0, dp0, ga_in, 0, 2, "l0_dwin")
    dx0, dg0 = _dh_norm_bwd(dp0, W["a_in"][0], xs, ng(0), dx1, "l0_dh")

    big = dict(a_in=ga_in, a_out=ga_out, b_in=gb_in, b_out=gb_out, c_in=gc_in, c_grp=gc_grp, c_out=gc_out)
    small = dict(norm=jnp.concatenate([dg0, dg1, dg2, dg3], axis=0), a_ws=jnp.stack([dws0, dws1]),
                 a_bs=jnp.stack([dbs0.T, dbs1.T]), b_gains=dgains, a_vg=jnp.concatenate([dvg0, dvg1], axis=0), c_sc=dsc)
    return loss, dx0, big, small


def kernel(x, norm_gain, a_w_in, a_v_gain, a_w_s, a_b_s, a_w_out, b_w_in, b_q_gain, b_k_gain, b_w_out, c_w_in, c_w_grp, c_scale, c_w_out, loss_target, m_norm_gain, m_a_w_in, m_a_v_gain, m_a_w_s, m_a_b_s, m_a_w_out, m_b_w_in, m_b_q_gain, m_b_k_gain, m_b_w_out, m_c_w_in, m_c_w_grp, m_c_scale, m_c_w_out, v_norm_gain, v_a_w_in, v_a_v_gain, v_a_w_s, v_a_b_s, v_a_w_out, v_b_w_in, v_b_q_gain, v_b_k_gain, v_b_w_out, v_c_w_in, v_c_w_grp, v_c_scale, v_c_w_out):
    cx, cy, cc = _coords()
    coords = jnp.stack([cx, cy, cc]).astype(jnp.int32)
    dev = 4 * cx + 2 * cy + cc
    Dm = x.shape[2]

    big_w = [a_w_in, a_w_out, b_w_in, b_w_out, c_w_in, c_w_grp, c_w_out]
    pieces = [_wire_rows(w.astype(WIRE), Dm) for w in big_w] + [_wire_rows(a_v_gain, Dm), _wire_rows(c_scale, Dm)]
    offs = np.cumsum([0] + [p.shape[0] for p in pieces])
    pad = (-int(offs[-1])) % 16
    packed = jnp.concatenate(pieces + ([jnp.zeros((pad, Dm), WIRE)] if pad else []), axis=0)
    gath = _all_gather_hbm(packed)

    def piece(i):
        return gath[:, int(offs[i]):int(offs[i + 1]), :]

    def small_f32(i, shape):
        flat = piece(i).reshape(N_DEV, -1)
        n = int(np.prod(shape))
        if WIRE == F32:
            return flat[:, :n].reshape((N_DEV,) + shape)
        return lax.bitcast_convert_type(flat[:, :2 * n].reshape((N_DEV,) + shape + (2,)), F32)

    na, _, nla = a_w_in.shape
    kla = a_w_out.shape[1]
    ngp, rlc, cgc = c_w_grp.shape[1:]
    a_in_g = piece(0).reshape(N_DEV, na, Dm, nla)
    a_out_g = piece(1).reshape(N_DEV, na, kla, Dm)
    a_vg_g = small_f32(7, a_v_gain.shape)
    c_sc_g = small_f32(8, c_scale.shape)
    W = dict(
        a_in=[a_in_g[:, l] for l in range(na)],
        a_out=[a_out_g[:, l].reshape(N_DEV * kla, Dm) for l in range(na)],
        b_in=piece(2).reshape(N_DEV, Dm, b_w_in.shape[2]),
        b_out=piece(3).reshape(-1, Dm),
        c_in=piece(4).reshape(N_DEV, Dm, c_w_in.shape[2]),
        c_grp=piece(5).reshape(N_DEV, ngp, rlc, cgc).transpose(1, 0, 2, 3).reshape(ngp, N_DEV * rlc, cgc),
        c_out=piece(6).reshape(-1, Dm),
        a_vg=a_vg_g.transpose(1, 0, 2).reshape(na, -1),
        c_sc=c_sc_g.transpose(1, 0, 2).reshape(1, -1),
    )
    b_gains = jnp.concatenate([b_q_gain[0], b_k_gain[0], jnp.zeros((2, HEAD_DIM), F32)], axis=0)

    loss_local, dx0, big, small = _local_step(x[0], loss_target[0], W, norm_gain, a_w_s, a_b_s, b_gains)
    loss = lax.psum(loss_local, ("x", "y", "c"))

    order = ["norm", "a_ws", "a_bs", "b_gains", "a_vg", "c_sc"]
    rows = [small[k].reshape(-1, 128) for k in order]
    roff = np.cumsum([0] + [r.shape[0] for r in rows])
    tot = _all_reduce_small(jnp.concatenate(rows, axis=0))
    sm = {k: tot[int(roff[i]):int(roff[i + 1])].reshape(small[k].shape) for i, k in enumerate(order)}
    vl = a_v_gain.shape[1]
    g_small = dict(
        norm_gain=sm["norm"], a_w_s=sm["a_ws"], a_b_s=sm["a_bs"],
        b_q_gain=sm["b_gains"][None, 0:3], b_k_gain=sm["b_gains"][None, 3:6],
        a_v_gain=lax.dynamic_slice_in_dim(sm["a_vg"], dev * vl, vl, axis=1),
        c_scale=lax.dynamic_slice_in_dim(sm["c_sc"], dev * vl, vl, axis=1),
    )

    names = ["a_in", "a_out", "b_in", "b_out", "c_in", "c_grp", "c_out"]
    pnames = ["a_w_in", "a_w_out", "b_w_in", "b_w_out", "c_w_in", "c_w_grp", "c_w_out"]
    g3 = [big[k].reshape(N_DEV, -1, big[k].shape[-1]) for k in names]
    r1 = _rs_sibling(g3)
    parts = [_chip_partials(coords, g, r, f"rs_add_{k}") for k, g, r in zip(names, g3, r1)]
    r2 = _rs_chips(parts)

    params = dict(a_w_in=a_w_in, a_w_out=a_w_out, b_w_in=b_w_in, b_w_out=b_w_out, c_w_in=c_w_in, c_w_grp=c_w_grp, c_w_out=c_w_out,
                  norm_gain=norm_gain, a_v_gain=a_v_gain, a_w_s=a_w_s, a_b_s=a_b_s, b_q_gain=b_q_gain, b_k_gain=b_k_gain, c_scale=c_scale)
    moms = dict(a_w_in=(m_a_w_in, v_a_w_in), a_w_out=(m_a_w_out, v_a_w_out), b_w_in=(m_b_w_in, v_b_w_in), b_w_out=(m_b_w_out, v_b_w_out),
                c_w_in=(m_c_w_in, v_c_w_in), c_w_grp=(m_c_w_grp, v_c_w_grp), c_w_out=(m_c_w_out, v_c_w_out),
                norm_gain=(m_norm_gain, v_norm_gain), a_v_gain=(m_a_v_gain, v_a_v_gain), a_w_s=(m_a_w_s, v_a_w_s),
                a_b_s=(m_a_b_s, v_a_b_s), b_q_gain=(m_b_q_gain, v_b_q_gain), b_k_gain=(m_b_k_gain, v_b_k_gain),
                c_scale=(m_c_scale, v_c_scale))
    grad, delta, new_m, new_v = {}, {}, {}, {}
    for i, (k, pname) in enumerate(zip(names, pnames)):
        w = params[pname]
        C = w.shape[-1]
        outs = _adamw_sharded(coords, w.reshape(-1, C), moms[pname][0].reshape(-1, C), moms[pname][1].reshape(-1, C),
                              g3[i], r1[i], r2[i], f"adamw_{k}")
        grad[pname], delta[pname], new_m[pname], new_v[pname] = [o.reshape(w.shape) for o in outs]
    for pname, g in g_small.items():
        w = params[pname]
        C = w.shape[-1]
        outs = _adamw_small(w.reshape(-1, C), g.reshape(-1, C), moms[pname][0].reshape(-1, C), moms[pname][1].reshape(-1, C),
                            f"adamw_{pname}")
        grad[pname] = g.reshape(w.shape)
        delta[pname], new_m[pname], new_v[pname] = [o.reshape(w.shape) for o in outs]

    wnames = ["norm_gain", "a_w_in", "a_v_gain", "a_w_s", "a_b_s", "a_w_out", "b_w_in", "b_q_gain", "b_k_gain", "b_w_out",
              "c_w_in", "c_w_grp", "c_scale", "c_w_out"]
    return (loss, dx0[None], *[grad[n] for n in wnames], *[delta[n] for n in wnames],
            *[new_m[n] for n in wnames], *[new_v[n] for n in wnames])
```

```python
import functools

import numpy as np
import jax
import jax.numpy as jnp
from jax import lax
from jax.experimental import pallas as pl
from jax.experimental.pallas import tpu as pltpu

F32 = jnp.float32
MXU = jnp.bfloat16
ACT = jnp.bfloat16
WIRE = jnp.bfloat16

EPS = 1e-6
CHUNK = 128
A_GROUPS = 8
HEAD_DIM = 128
B_HEADS = 8
B_DILATIONS = (1, 4, 16)
ROPE_DIM = 32
ROPE_THETA = 500000.0
POOL_SIZES = (2, 4, 8, 16)
POOL_HALO = 16
N_DEV = 8
NEG = -1e30

ADAM_LR, ADAM_B1, ADAM_B2, ADAM_EPS, ADAM_WD, ADAM_STEP = 0.001, 0.9, 0.999, 1e-08, 0.01, 10

VMEM_LIMIT = 56 * 1024 * 1024
MESH = pl.DeviceIdType.MESH


def _cp(*sem):
    return pltpu.CompilerParams(dimension_semantics=sem, vmem_limit_bytes=VMEM_LIMIT)


def _sigmoid(z):
    return 1.0 / (1.0 + jnp.exp(-z))


def _dot(a, b):
    return jnp.dot(a.astype(MXU), b.astype(MXU), preferred_element_type=F32)


def _dot_nt(a, b):
    return lax.dot_general(a.astype(MXU), b.astype(MXU), (((1,), (1,)), ((), ())), preferred_element_type=F32)


def _dot_tn(a, b):
    return lax.dot_general(a.astype(MXU), b.astype(MXU), (((0,), (0,)), ((), ())), preferred_element_type=F32)


def _chunk_slot(d):
    return (d % 2) * 4 + d // 2


class _Comm:
    def __init__(self, inputs, out_shapes, n_remote, make, aliases=None):
        self.inputs = list(inputs)
        self.out_shapes = list(out_shapes)
        self.n_remote = n_remote
        self.make = make
        self.aliases = dict(aliases or {})

    def sems(self):
        return [pltpu.SemaphoreType.DMA((self.n_remote,)), pltpu.SemaphoreType.DMA((self.n_remote,)),
                pltpu.SemaphoreType.DMA((1,))]


_HBM = pl.BlockSpec(memory_space=pl.ANY)


def _launch(body, *, name, grid, in_specs, out_specs, out_shape, args, sem, scratch=(), aliases=None, comm=None):
    in_specs, out_specs, out_shape, scratch = list(in_specs), list(out_specs), list(out_shape), list(scratch)
    aliases = dict(aliases or {})
    if comm is None:
        return pl.pallas_call(body, name=name, grid=grid, in_specs=in_specs, out_specs=out_specs, out_shape=out_shape,
                              scratch_shapes=scratch, input_output_aliases=aliases, compiler_params=_cp(*sem))(*args)
    n_in, n_out, n_sc = len(in_specs), len(out_specs), len(scratch)
    nci, nco = len(comm.inputs), len(comm.out_shapes)

    def hosted(*refs):
        b_in, c_in = refs[:n_in], refs[n_in:n_in + nci]
        o0 = n_in + nci
        b_out, c_out = refs[o0:o0 + n_out], refs[o0 + n_out:o0 + n_out + nco]
        s0 = o0 + n_out + nco
        b_sc, sems = refs[s0:s0 + n_sc], refs[s0 + n_sc:]
        ids = [pl.program_id(a) for a in range(len(grid))]
        first = functools.reduce(jnp.logical_and, [i == 0 for i in ids])
        last = functools.reduce(jnp.logical_and, [i == g - 1 for i, g in zip(ids, grid)])

        @pl.when(first)
        def _():
            for cp in comm.make(c_in, c_out, *sems):
                cp.start()

        body(*b_in, *b_out, *b_sc)

        @pl.when(last)
        def _():
            for cp in comm.make(c_in, c_out, *sems):
                cp.wait()

    for ci, co in comm.aliases.items():
        aliases[n_in + ci] = n_out + co
    return pl.pallas_call(
        hosted, name=name, grid=grid, in_specs=in_specs + [_HBM] * nci, out_specs=out_specs + [_HBM] * nco,
        out_shape=out_shape + comm.out_shapes, scratch_shapes=scratch + comm.sems(),
        input_output_aliases=aliases, compiler_params=_cp(*["arbitrary"] * len(grid)))(*args, *comm.inputs)


def _run_comm(comm, name):
    nci, nco = len(comm.inputs), len(comm.out_shapes)

    def body(*refs):
        cps = comm.make(refs[:nci], refs[nci:nci + nco], *refs[nci + nco:])
        for cp in cps:
            cp.start()
        for cp in cps:
            cp.wait()

    return pl.pallas_call(
        body, name=name, in_specs=[_HBM] * nci, out_specs=[_HBM] * nco, out_shape=comm.out_shapes,
        scratch_shapes=comm.sems(), input_output_aliases=dict(comm.aliases))(*comm.inputs)


def _norm_proj(x, gain, w_dm, name, comm=None):
    M, Dm = x.shape
    nd, _, nl = w_dm.shape
    tm = min(M, 1024)

    def body(x_ref, g_ref, w_ref, h_ref, p_ref):
        @pl.when(pl.program_id(1) == 0)
        def _():
            xv = x_ref[...]
            r = lax.rsqrt(jnp.mean(xv * xv, axis=-1, keepdims=True) + EPS)
            h_ref[...] = (xv * r * g_ref[...]).astype(h_ref.dtype)

        p_ref[...] = _dot(h_ref[...], w_ref[...]).astype(p_ref.dtype)

    return _launch(
        body, name=name, grid=(M // tm, nd),
        in_specs=[pl.BlockSpec((tm, Dm), lambda i, j: (i, 0)),
                  pl.BlockSpec((1, Dm), lambda i, j: (0, 0)),
                  pl.BlockSpec((None, Dm, nl), lambda i, j: (j, 0, 0))],
        out_specs=[pl.BlockSpec((tm, Dm), lambda i, j: (i, 0)),
                   pl.BlockSpec((tm, nl), lambda i, j: (i, j))],
        out_shape=[jax.ShapeDtypeStruct((M, Dm), ACT), jax.ShapeDtypeStruct((M, nd * nl), ACT)],
        args=(x, gain, w_dm), sem=("parallel", "arbitrary"), comm=comm)


def _out_proj(x, y, w, name):
    M, Dm = x.shape
    K = y.shape[1]
    tm = min(M, 512)

    def body(x_ref, y_ref, w_ref, o_ref):
        o_ref[...] = x_ref[...] + _dot(y_ref[...], w_ref[...])

    return pl.pallas_call(
        body, name=name, grid=(M // tm,),
        in_specs=[pl.BlockSpec((tm, Dm), lambda i: (i, 0)),
                  pl.BlockSpec((tm, K), lambda i: (i, 0)),
                  pl.BlockSpec((K, Dm), lambda i: (0, 0))],
        out_specs=pl.BlockSpec((tm, Dm), lambda i: (i, 0)),
        out_shape=jax.ShapeDtypeStruct((M, Dm), F32),
        compiler_params=_cp("parallel"),
    )(x, y, w)


def _loss_head(xf, target):
    M, Dm = xf.shape
    tm = min(M, 512)

    def body(x_ref, t_ref, dx_ref, l_ref):
        @pl.when(pl.program_id(0) == 0)
        def _():
            l_ref[...] = jnp.zeros_like(l_ref)

        err = x_ref[...] - t_ref[...]
        dx_ref[...] = err * (1.0 / Dm)
        l_ref[...] += jnp.sum(err * err) * (0.5 / Dm)

    dx, l = pl.pallas_call(
        body, name="loss_head", grid=(M // tm,),
        in_specs=[pl.BlockSpec((tm, Dm), lambda i: (i, 0))] * 2,
        out_specs=[pl.BlockSpec((tm, Dm), lambda i: (i, 0)), pl.BlockSpec((8, 128), lambda i: (0, 0))],
        out_shape=[jax.ShapeDtypeStruct((M, Dm), F32), jax.ShapeDtypeStruct((8, 128), F32)],
        compiler_params=_cp("arbitrary"),
    )(xf, target)
    return l[0, 0], dx


def _dw_in(h, dproj, name):
    M, Dm = h.shape
    nl = dproj.shape[1] // N_DEV
    tt = min(M, 512)

    def body(a_ref, b_ref, o_ref):
        @pl.when(pl.program_id(1) == 0)
        def _():
            o_ref[...] = jnp.zeros_like(o_ref)

        o_ref[...] += _dot_tn(a_ref[...], b_ref[...])

    return pl.pallas_call(
        body, name=name, grid=(N_DEV, M // tt),
        in_specs=[pl.BlockSpec((tt, Dm), lambda j, t: (t, 0)), pl.BlockSpec((tt, nl), lambda j, t: (t, j))],
        out_specs=pl.BlockSpec((None, Dm, nl), lambda j, t: (_chunk_slot(j), 0, 0)),
        out_shape=jax.ShapeDtypeStruct((N_DEV, Dm, nl), F32),
        compiler_params=_cp("parallel", "arbitrary"),
    )(h, dproj)


def _dw_out(y, dout, name):
    M, K = y.shape
    Dm = dout.shape[1]
    kl = K // N_DEV
    tt = min(M, 1024)

    def body(a_ref, b_ref, o_ref):
        @pl.when(pl.program_id(1) == 0)
        def _():
            o_ref[...] = jnp.zeros_like(o_ref)

        o_ref[...] += _dot_tn(a_ref[...], b_ref[...])

    return pl.pallas_call(
        body, name=name, grid=(N_DEV, M // tt),
        in_specs=[pl.BlockSpec((tt, kl), lambda j, t: (t, j)), pl.BlockSpec((tt, Dm), lambda j, t: (t, 0))],
        out_specs=pl.BlockSpec((None, kl, Dm), lambda j, t: (_chunk_slot(j), 0, 0)),
        out_shape=jax.ShapeDtypeStruct((N_DEV, kl, Dm), F32),
        compiler_params=_cp("parallel", "arbitrary"),
    )(y, dout)


def _dh_norm_bwd(dproj, w_dm, x, gain, dres, name, comm=None):
    M, Dm = x.shape
    nd, _, nl = w_dm.shape
    tm = min(M, 1024)

    def body(dp_ref, w_ref, x_ref, g_ref, dr_ref, dx_ref, dg_ref, acc_ref):
        i, j = pl.program_id(0), pl.program_id(1)

        @pl.when(j == 0)
        def _():
            acc_ref[...] = jnp.zeros_like(acc_ref)

        acc_ref[...] += _dot_nt(dp_ref[...], w_ref[...])

        @pl.when(j == nd - 1)
        def _():
            @pl.when(i == 0)
            def _():
                dg_ref[...] = jnp.zeros_like(dg_ref)

            dh = acc_ref[...]
            xv = x_ref[...]
            r = lax.rsqrt(jnp.mean(xv * xv, axis=-1, keepdims=True) + EPS)
            xn = xv * r
            dg_ref[...] += jnp.sum(dh * xn, axis=0, keepdims=True)
            dxn = dh * g_ref[...]
            dx_ref[...] = dr_ref[...] + r * (dxn - xn * jnp.mean(dxn * xn, axis=-1, keepdims=True))

    return _launch(
        body, name=name, grid=(M // tm, nd),
        in_specs=[pl.BlockSpec((tm, nl), lambda i, j: (i, j)),
                  pl.BlockSpec((None, Dm, nl), lambda i, j: (j, 0, 0)),
                  pl.BlockSpec((tm, Dm), lambda i, j: (i, 0)),
                  pl.BlockSpec((1, Dm), lambda i, j: (0, 0)),
                  pl.BlockSpec((tm, Dm), lambda i, j: (i, 0))],
        out_specs=[pl.BlockSpec((tm, Dm), lambda i, j: (i, 0)), pl.BlockSpec((1, Dm), lambda i, j: (0, 0))],
        out_shape=[jax.ShapeDtypeStruct((M, Dm), F32), jax.ShapeDtypeStruct((1, Dm), F32)],
        scratch=[pltpu.VMEM((tm, Dm), F32)],
        args=(dproj, w_dm, x, gain, dres), sem=("arbitrary", "arbitrary"), comm=comm)


def _tril_mask():
    return lax.broadcasted_iota(jnp.int32, (CHUNK, CHUNK), 0) >= lax.broadcasted_iota(jnp.int32, (CHUNK, CHUNK), 1)


def _a_mid(proj, v_gain, w_s, b_st, name, comm=None):
    M = proj.shape[0]
    W = proj.shape[1] // 3
    gd = W // A_GROUPS
    tm = min(M, 256)

    def body(p_ref, vg_ref, ws_ref, bs_ref, y_ref):
        pv = p_ref[:, W:2 * W].astype(F32)
        r = lax.rsqrt(jnp.mean(pv * pv, axis=-1, keepdims=True) + EPS)
        v = (pv * r * vg_ref[...]).astype(MXU)
        tri = _tril_mask()
        for g in range(A_GROUPS):
            wg = jnp.where(tri, ws_ref[g], 0.0).astype(MXU)
            bcol = bs_ref[:, g:g + 1]
            for c in range(tm // CHUNK):
                rows, cols = slice(c * CHUNK, (c + 1) * CHUNK), slice(g * gd, (g + 1) * gd)
                mixed = jnp.dot(wg, v[rows, cols], preferred_element_type=F32) + bcol
                u = p_ref[rows, g * gd:(g + 1) * gd].astype(F32)
                z = p_ref[rows, 2 * W + g * gd:2 * W + (g + 1) * gd].astype(F32)
                y_ref[rows, cols] = (u * mixed * (z * _sigmoid(z))).astype(y_ref.dtype)

    return _launch(
        body, name=name, grid=(M // tm,),
        in_specs=[pl.BlockSpec((tm, 3 * W), lambda i: (i, 0)),
                  pl.BlockSpec((1, W), lambda i: (0, 0)),
                  pl.BlockSpec((A_GROUPS, CHUNK, CHUNK), lambda i: (0, 0, 0)),
                  pl.BlockSpec((CHUNK, A_GROUPS), lambda i: (0, 0))],
        out_specs=[pl.BlockSpec((tm, W), lambda i: (i, 0))],
        out_shape=[jax.ShapeDtypeStruct((M, W), ACT)],
        args=(proj, v_gain, w_s, b_st), sem=("parallel",), comm=comm)


def _a_bwd(dout, w_out, proj, v_gain, w_s, b_st, name, comm=None):
    M = proj.shape[0]
    W = proj.shape[1] // 3
    Dm = dout.shape[1]
    gd = W // A_GROUPS
    tm = min(M, 256)
    nt = M // tm

    def body(do_ref, wo_ref, p_ref, vg_ref, ws_ref, bs_ref, dp_ref, dws_ref, dbs_ref, dvg_ref, dv_s):
        i = pl.program_id(0)

        @pl.when(i == 0)
        def _():
            dws_ref[...] = jnp.zeros_like(dws_ref)
            dbs_ref[...] = jnp.zeros_like(dbs_ref)
            dvg_ref[...] = jnp.zeros_like(dvg_ref)

        dy = _dot_nt(do_ref[...], wo_ref[...])
        pv = p_ref[:, W:2 * W].astype(F32)
        r = lax.rsqrt(jnp.mean(pv * pv, axis=-1, keepdims=True) + EPS)
        pvn = pv * r
        vg = vg_ref[...]
        v = (pvn * vg).astype(MXU)
        tri = _tril_mask()
        for g in range(A_GROUPS):
            wf = jnp.where(tri, ws_ref[g], 0.0)
            wg = wf.astype(MXU)
            wgt = wf.T.astype(MXU)
            bcol = bs_ref[:, g:g + 1]
            for c in range(tm // CHUNK):
                rows, cols = slice(c * CHUNK, (c + 1) * CHUNK), slice(g * gd, (g + 1) * gd)
                vb = v[rows, cols]
                mixed = jnp.dot(wg, vb, preferred_element_type=F32) + bcol
                u = p_ref[rows, g * gd:(g + 1) * gd].astype(F32)
                z = p_ref[rows, 2 * W + g * gd:2 * W + (g + 1) * gd].astype(F32)
                sig = _sigmoid(z)
                sz = z * sig
                dyb = dy[rows, cols]
                dp_ref[rows, g * gd:(g + 1) * gd] = (dyb * mixed * sz).astype(dp_ref.dtype)
                dp_ref[rows, 2 * W + g * gd:2 * W + (g + 1) * gd] = (
                    dyb * u * mixed * (sig * (1.0 + z * (1.0 - sig)))).astype(dp_ref.dtype)
                dmix = dyb * u * sz
                dws_ref[g] += _dot_nt(dmix, vb)
                dbs_ref[:, g:g + 1] += jnp.sum(dmix, axis=1, keepdims=True)
                dv_s[rows, cols] = jnp.dot(wgt, dmix.astype(MXU), preferred_element_type=F32)
        dv = dv_s[...]
        dvg_ref[...] += jnp.sum(dv * pvn, axis=0, keepdims=True)
        dpvn = dv * vg
        dp_ref[:, W:2 * W] = (r * (dpvn - pvn * jnp.mean(dpvn * pvn, axis=-1, keepdims=True))).astype(dp_ref.dtype)

        @pl.when(i == nt - 1)
        def _():
            for g in range(A_GROUPS):
                dws_ref[g] = jnp.where(tri, dws_ref[g], 0.0)

    return _launch(
        body, name=name, grid=(nt,),
        in_specs=[pl.BlockSpec((tm, Dm), lambda i: (i, 0)),
                  pl.BlockSpec((W, Dm), lambda i: (0, 0)),
                  pl.BlockSpec((tm, 3 * W), lambda i: (i, 0)),
                  pl.BlockSpec((1, W), lambda i: (0, 0)),
                  pl.BlockSpec((A_GROUPS, CHUNK, CHUNK), lambda i: (0, 0, 0)),
                  pl.BlockSpec((CHUNK, A_GROUPS), lambda i: (0, 0))],
        out_specs=[pl.BlockSpec((tm, 3 * W), lambda i: (i, 0)),
                   pl.BlockSpec((A_GROUPS, CHUNK, CHUNK), lambda i: (0, 0, 0)),
                   pl.BlockSpec((CHUNK, A_GROUPS), lambda i: (0, 0)),
                   pl.BlockSpec((1, W), lambda i: (0, 0))],
        out_shape=[jax.ShapeDtypeStruct((M, 3 * W), ACT),
                   jax.ShapeDtypeStruct((A_GROUPS, CHUNK, CHUNK), F32),
                   jax.ShapeDtypeStruct((CHUNK, A_GROUPS), F32),
                   jax.ShapeDtypeStruct((1, W), F32)],
        scratch=[pltpu.VMEM((tm, W), F32)],
        args=(dout, w_out, proj, v_gain, w_s, b_st), sem=("arbitrary",), comm=comm)


def _pool_diff(xg, tail, i, tm, w):
    t = lax.broadcasted_iota(jnp.int32, (tm, tm + POOL_HALO), 0)
    s = lax.broadcasted_iota(jnp.int32, (tm, tm + POOL_HALO), 1)
    off = t - (s - POOL_HALO)
    band = jnp.where((off >= 0) & (off < w), 1.0, 0.0).astype(MXU)
    tail = jnp.where(i > 0, tail, jnp.zeros_like(tail))
    ext = jnp.concatenate([tail, xg], axis=0)
    ssum = jnp.dot(band, ext.astype(MXU), preferred_element_type=F32)
    tglob = i * tm + lax.broadcasted_iota(jnp.int32, (tm, 1), 0)
    cnt = jnp.minimum(tglob + 1, w).astype(F32)
    return ssum / cnt - xg.astype(F32)


def _c_mid(proj, w_grp, scale, name):
    M = proj.shape[0]
    W = proj.shape[1] // 2
    ng = len(POOL_SIZES)
    cg = W // ng
    tm = min(M, 256)
    hb = tm // POOL_HALO

    def body(xc_ref, tail_ref, z_ref, wg_ref, sc_ref, y_ref):
        i = pl.program_id(0)
        for g, w in enumerate(POOL_SIZES):
            cols = slice(g * cg, (g + 1) * cg)
            d = _pool_diff(xc_ref[:, cols], tail_ref[:, cols], i, tm, w)
            mixed = _dot(d, wg_ref[g]) * sc_ref[:, cols]
            z = z_ref[:, cols].astype(F32)
            y_ref[:, cols] = (mixed * (z * _sigmoid(z))).astype(y_ref.dtype)

    return pl.pallas_call(
        body, name=name, grid=(M // tm,),
        in_specs=[pl.BlockSpec((tm, W), lambda i: (i, 0)),
                  pl.BlockSpec((POOL_HALO, W), lambda i: (jnp.maximum(i * hb - 1, 0), 0)),
                  pl.BlockSpec((tm, W), lambda i: (i, 1)),
                  pl.BlockSpec((ng, cg, cg), lambda i: (0, 0, 0)),
                  pl.BlockSpec((1, W), lambda i: (0, 0))],
        out_specs=pl.BlockSpec((tm, W), lambda i: (i, 0)),
        out_shape=jax.ShapeDtypeStruct((M, W), ACT),
        compiler_params=_cp("parallel"),
    )(proj, proj, proj, w_grp, scale)


def _c_bwd1(dout, w_out, proj, w_grp, scale, name, comm=None):
    M = proj.shape[0]
    W = proj.shape[1] // 2
    Dm = dout.shape[1]
    ng = len(POOL_SIZES)
    cg = W // ng
    rl = cg // N_DEV
    tm = min(M, 256)
    hb = tm // POOL_HALO
    nt = M // tm

    def body(do_ref, wo_ref, xc_ref, tail_ref, z_ref, wg_ref, sc_ref, dd_ref, dz_ref, dwg_ref, dsc_ref, acc_ref):
        i = pl.program_id(0)

        @pl.when(i == 0)
        def _():
            acc_ref[...] = jnp.zeros_like(acc_ref)
            dsc_ref[...] = jnp.zeros_like(dsc_ref)

        dy = _dot_nt(do_ref[...], wo_ref[...])
        for g, w in enumerate(POOL_SIZES):
            cols = slice(g * cg, (g + 1) * cg)
            d = _pool_diff(xc_ref[:, cols], tail_ref[:, cols], i, tm, w)
            mr = _dot(d, wg_ref[g])
            sc = sc_ref[:, cols]
            z = z_ref[:, cols].astype(F32)
            sig = _sigmoid(z)
            dyg = dy[:, cols]
            dmixed = dyg * (z * sig)
            dz_ref[:, cols] = (dyg * (mr * sc) * (sig * (1.0 + z * (1.0 - sig)))).astype(dz_ref.dtype)
            dsc_ref[:, cols] += jnp.sum(dmixed * mr, axis=0, keepdims=True)
            dmr = (dmixed * sc).astype(MXU)
            acc_ref[g] += _dot_tn(d, dmr)
            dd_ref[:, cols] = _dot_nt(dmr, wg_ref[g]).astype(dd_ref.dtype)

        @pl.when(i == nt - 1)
        def _():
            for dev in range(N_DEV):
                for g in range(ng):
                    dwg_ref[_chunk_slot(dev), g] = acc_ref[g, dev * rl:(dev + 1) * rl, :]

    return _launch(
        body, name=name, grid=(nt,),
        in_specs=[pl.BlockSpec((tm, Dm), lambda i: (i, 0)),
                  pl.BlockSpec((W, Dm), lambda i: (0, 0)),
                  pl.BlockSpec((tm, W), lambda i: (i, 0)),
                  pl.BlockSpec((POOL_HALO, W), lambda i: (jnp.maximum(i * hb - 1, 0), 0)),
                  pl.BlockSpec((tm, W), lambda i: (i, 1)),
                  pl.BlockSpec((ng, cg, cg), lambda i: (0, 0, 0)),
                  pl.BlockSpec((1, W), lambda i: (0, 0))],
        out_specs=[pl.BlockSpec((tm, W), lambda i: (i, 0)),
                   pl.BlockSpec((tm, W), lambda i: (i, 0)),
                   pl.BlockSpec((N_DEV, ng, rl, cg), lambda i: (0, 0, 0, 0)),
                   pl.BlockSpec((1, W), lambda i: (0, 0))],
        out_shape=[jax.ShapeDtypeStruct((M, W), ACT), jax.ShapeDtypeStruct((M, W), ACT),
                   jax.ShapeDtypeStruct((N_DEV, ng, rl, cg), F32), jax.ShapeDtypeStruct((1, W), F32)],
        scratch=[pltpu.VMEM((ng, cg, cg), F32)],
        args=(dout, w_out, proj, proj, proj, w_grp, scale), sem=("arbitrary",), comm=comm)


def _c_bwd2(dd, dz, name):
    M, W = dd.shape
    ng = len(POOL_SIZES)
    cg = W // ng
    tm = min(M, 256)
    hb = tm // POOL_HALO
    nt = M // tm

    def body(dd_ref, head_ref, dz_ref, dp_ref):
        i = pl.program_id(0)
        s = lax.broadcasted_iota(jnp.int32, (tm, tm + POOL_HALO), 0)
        t = lax.broadcasted_iota(jnp.int32, (tm, tm + POOL_HALO), 1)
        off = t - s
        tglob = i * tm + lax.broadcasted_iota(jnp.int32, (tm + POOL_HALO, 1), 0)
        for g, w in enumerate(POOL_SIZES):
            cols = slice(g * cg, (g + 1) * cg)
            ddg = dd_ref[:, cols].astype(F32)
            head = head_ref[:, cols].astype(F32)
            head = jnp.where(i < nt - 1, head, jnp.zeros_like(head))
            cnt = jnp.minimum(tglob + 1, w).astype(F32)
            ext = (jnp.concatenate([ddg, head], axis=0) / cnt).astype(MXU)
            band = jnp.where((off >= 0) & (off < w), 1.0, 0.0).astype(MXU)
            dp_ref[:, cols] = (jnp.dot(band, ext, preferred_element_type=F32) - ddg).astype(dp_ref.dtype)
        dp_ref[:, W:] = dz_ref[...]

    return pl.pallas_call(
        body, name=name, grid=(nt,),
        in_specs=[pl.BlockSpec((tm, W), lambda i: (i, 0)),
                  pl.BlockSpec((POOL_HALO, W), lambda i: (jnp.minimum((i + 1) * hb, M // POOL_HALO - 1), 0)),
                  pl.BlockSpec((tm, W), lambda i: (i, 0))],
        out_specs=pl.BlockSpec((tm, 2 * W), lambda i: (i, 0)),
        out_shape=jax.ShapeDtypeStruct((M, 2 * W), ACT),
        compiler_params=_cp("parallel"),
    )(dd, dd, dz)


def _rope_tables(S):
    half = ROPE_DIM // 2
    inv_freq = jnp.power(jnp.float32(ROPE_THETA), -jnp.arange(half, dtype=F32) / half)
    ang = jnp.arange(S, dtype=F32)[:, None] * inv_freq[None, :]
    cos, sin = jnp.cos(ang), jnp.sin(ang)
    rest = HEAD_DIM - ROPE_DIM
    cf = jnp.concatenate([cos, cos, jnp.ones((S, rest), F32)], axis=1)
    sf = jnp.concatenate([-sin, sin, jnp.zeros((S, rest), F32)], axis=1)
    return cf, sf


def _swap_matrix():
    half = ROPE_DIM // 2
    a = lax.broadcasted_iota(jnp.int32, (HEAD_DIM, HEAD_DIM), 0)
    e = lax.broadcasted_iota(jnp.int32, (HEAD_DIM, HEAD_DIM), 1)
    hit = ((e < half) & (a == e + half)) | ((e >= half) & (e < 2 * half) & (a == e - half))
    return jnp.where(hit, 1.0, 0.0).astype(MXU)


def _b_qk_fwd(proj, tables, gains, name, comm=None):
    M = proj.shape[0]
    nsl = 2 * len(B_DILATIONS) * B_HEADS
    Wqk = nsl * HEAD_DIM
    tm = min(M, 256)

    def body(p_ref, cf_ref, sf_ref, g_ref, o_ref):
        cf, sf = cf_ref[...], sf_ref[...]
        swap = _swap_matrix()
        for j in range(nsl):
            cols = slice(j * HEAD_DIM, (j + 1) * HEAD_DIM)
            xv = p_ref[:, cols].astype(F32)
            r = lax.rsqrt(jnp.mean(xv * xv, axis=-1, keepdims=True) + EPS)
            xg = xv * g_ref[j // B_HEADS:j // B_HEADS + 1, :]
            hi = xg.astype(MXU)
            lo = (xg - hi.astype(F32)).astype(MXU)
            sw = jnp.dot(hi, swap, preferred_element_type=F32) + jnp.dot(lo, swap, preferred_element_type=F32)
            o_ref[:, cols] = (r * (xg * cf + sw * sf)).astype(o_ref.dtype)

    tspec = pl.BlockSpec((tm, HEAD_DIM), lambda i: (i, 0))
    return _launch(
        body, name=name, grid=(M // tm,),
        in_specs=[pl.BlockSpec((tm, Wqk), lambda i: (i, 0)), tspec, tspec,
                  pl.BlockSpec((8, HEAD_DIM), lambda i: (0, 0))],
        out_specs=[pl.BlockSpec((tm, Wqk), lambda i: (i, 0))],
        out_shape=[jax.ShapeDtypeStruct((M, Wqk), ACT)],
        args=(proj, *tables, gains), sem=("parallel",), comm=comm)


def _b_qk_bwd(dqs, dks, proj, tables, gains, dproj, name):
    M = proj.shape[0]
    ngr = len(B_DILATIONS)
    nsl = 2 * ngr * B_HEADS
    Wqk = nsl * HEAD_DIM
    Wg = B_HEADS * HEAD_DIM
    tm = min(M, 256)

    def body(*refs):
        d_refs = refs[:2 * ngr]
        p_ref, cf_ref, sf_ref, g_ref = refs[2 * ngr:2 * ngr + 4]
        dp_ref, dg_ref = refs[-2], refs[-1]

        @pl.when(pl.program_id(0) == 0)
        def _():
            dg_ref[...] = jnp.zeros_like(dg_ref)

        cf, sf = cf_ref[...], sf_ref[...]
        swap = _swap_matrix()
        for j in range(nsl):
            t, hh = j // B_HEADS, j % B_HEADS
            cols = slice(j * HEAD_DIM, (j + 1) * HEAD_DIM)
            dy = d_refs[t][:, hh * HEAD_DIM:(hh + 1) * HEAD_DIM].astype(F32)
            dxn = dy * cf + jnp.dot((dy * sf).astype(MXU), swap, preferred_element_type=F32)
            xv = p_ref[:, cols].astype(F32)
            r = lax.rsqrt(jnp.mean(xv * xv, axis=-1, keepdims=True) + EPS)
            xh = xv * r
            dg_ref[t:t + 1, :] += jnp.sum(dxn * xh, axis=0, keepdims=True)
            dxh = dxn * g_ref[t:t + 1, :]
            dp_ref[:, cols] = (r * (dxh - xh * jnp.mean(dxh * xh, axis=-1, keepdims=True))).astype(dp_ref.dtype)

    tspec = pl.BlockSpec((tm, HEAD_DIM), lambda i: (i, 0))
    dspec = pl.BlockSpec((tm, Wg), lambda i: (i, 0))
    n_in = 2 * ngr + 5
    return pl.pallas_call(
        body, name=name, grid=(M // tm,),
        in_specs=[dspec] * (2 * ngr) + [pl.BlockSpec((tm, Wqk), lambda i: (i, 0)), tspec, tspec,
                                        pl.BlockSpec((8, HEAD_DIM), lambda i: (0, 0)),
                                        pl.BlockSpec(memory_space=pl.ANY)],
        out_specs=[pl.BlockSpec((tm, Wqk), lambda i: (i, 0)), pl.BlockSpec((8, HEAD_DIM), lambda i: (0, 0))],
        out_shape=[jax.ShapeDtypeStruct(dproj.shape, dproj.dtype), jax.ShapeDtypeStruct((8, HEAD_DIM), F32)],
        input_output_aliases={n_in - 1: 0},
        compiler_params=_cp("arbitrary"),
    )(*dqs, *dks, proj, *tables, gains, dproj)


def _attn_tile(D, M):
    return max(HEAD_DIM * D, min(M, 2048))


def _strided(start, size, D):
    return pl.ds(start, size) if D == 1 else pl.ds(start, size, stride=D)


def _attn_mask(base):
    qi = lax.broadcasted_iota(jnp.int32, (CHUNK, 2 * CHUNK), 0)
    ki = lax.broadcasted_iota(jnp.int32, (CHUNK, 2 * CHUNK), 1)
    return (ki >= qi) & (ki <= qi + CHUNK) & (ki >= CHUNK - base)


def _b_attn_fwd(qk, proj, g, name):
    M = qk.shape[0]
    D = B_DILATIONS[g]
    ngr = len(B_DILATIONS)
    T = _attn_tile(D, M)
    P = HEAD_DIM * D
    nsb = T // P
    Wg = B_HEADS * HEAD_DIM
    scale = np.float32(1.0 / np.sqrt(HEAD_DIM))

    def body(q_ref, k_ref, v_ref, o_ref, l_ref, qs, ks, vs, os_, ls):
        n = pl.program_id(1)

        @pl.when(n == 0)
        def _():
            ks[0:P, :] = jnp.zeros((P, HEAD_DIM), F32)
            vs[0:P, :] = jnp.zeros((P, HEAD_DIM), F32)

        qs[...] = q_ref[...].astype(F32)
        ks[P:P + T, :] = k_ref[...].astype(F32)
        vs[P:P + T, :] = v_ref[...].astype(F32)

        for b in range(nsb):
            mask = _attn_mask(n * (T // D) + b * CHUNK)
            for r in range(D):
                start = b * P + r
                q = qs[_strided(start, CHUNK, D), :]
                k = ks[_strided(start, 2 * CHUNK, D), :]
                v = vs[_strided(start, 2 * CHUNK, D), :]
                s = jnp.where(mask, _dot_nt(q, k) * scale, NEG)
                m = jnp.max(s, axis=-1, keepdims=True)
                p = jnp.exp(s - m)
                l = jnp.sum(p, axis=-1, keepdims=True)
                o = _dot(p, v) / l
                os_[_strided(start, CHUNK, D), :] = o
                ls[_strided(start, CHUNK, D), :] = jnp.broadcast_to(m + jnp.log(l), (CHUNK, HEAD_DIM))

        o_ref[...] = os_[...].astype(o_ref.dtype)
        l_ref[...] = ls[...]
        ks[0:P, :] = ks[T:T + P, :]
        vs[0:P, :] = vs[T:T + P, :]

    blk = (T, HEAD_DIM)
    return pl.pallas_call(
        body, name=name, grid=(B_HEADS, M // T),
        in_specs=[pl.BlockSpec(blk, lambda h, n: (n, g * B_HEADS + h)),
                  pl.BlockSpec(blk, lambda h, n: (n, (ngr + g) * B_HEADS + h)),
                  pl.BlockSpec(blk, lambda h, n: (n, (2 * ngr + g) * B_HEADS + h))],
        out_specs=[pl.BlockSpec(blk, lambda h, n: (n, h)), pl.BlockSpec(blk, lambda h, n: (n, h))],
        out_shape=[jax.ShapeDtypeStruct((M, Wg), ACT), jax.ShapeDtypeStruct((M, Wg), F32)],
        scratch_shapes=[pltpu.VMEM((T, HEAD_DIM), F32), pltpu.VMEM((P + T, HEAD_DIM), F32),
                        pltpu.VMEM((P + T, HEAD_DIM), F32), pltpu.VMEM((T, HEAD_DIM), F32),
                        pltpu.VMEM((T, HEAD_DIM), F32)],
        compiler_params=_cp("parallel", "arbitrary"),
    )(qk, qk, proj)


def _b_combine(os_, ls, proj, name):
    M, Wg = os_[0].shape
    ngr = len(B_DILATIONS)
    tm = min(M, 512)

    def body(*refs):
        o_refs, l_refs, z_ref = refs[:ngr], refs[ngr:2 * ngr], refs[2 * ngr]
        y_ref, o_ref, lse_ref = refs[2 * ngr + 1:]
        ls_ = [r[...] for r in l_refs]
        m = functools.reduce(jnp.maximum, ls_)
        es = [jnp.exp(l - m) for l in ls_]
        tot = functools.reduce(lambda a, b: a + b, es)
        o = functools.reduce(lambda a, b: a + b, [e * r[...].astype(F32) for e, r in zip(es, o_refs)]) / tot
        z = z_ref[...].astype(F32)
        y_ref[...] = (o.astype(F32) * (z * _sigmoid(z))).astype(y_ref.dtype)
        o_ref[...] = o.astype(o_ref.dtype)
        lse_ref[...] = m + jnp.log(tot)

    spec = pl.BlockSpec((tm, Wg), lambda i: (i, 0))
    return pl.pallas_call(
        body, name=name, grid=(M // tm,),
        in_specs=[spec] * (2 * ngr) + [pl.BlockSpec((tm, Wg), lambda i: (i, 3 * ngr))],
        out_specs=[spec] * 3,
        out_shape=[jax.ShapeDtypeStruct((M, Wg), ACT), jax.ShapeDtypeStruct((M, Wg), ACT),
                   jax.ShapeDtypeStruct((M, Wg), F32)],
        compiler_params=_cp("parallel"),
    )(*os_, *ls, proj)


def _b_bwd_pre(dout, w_out, o, proj, name):
    M, Wg = o.shape
    Dm = dout.shape[1]
    ngr = len(B_DILATIONS)
    tm = min(M, 512)

    def body(do_ref, wo_ref, o_ref, z_ref, dov_ref, dl_ref, dp_ref):
        dy = _dot_nt(do_ref[...], wo_ref[...])
        z = z_ref[...].astype(F32)
        sig = _sigmoid(z)
        ov = o_ref[...].astype(F32)
        dp_ref[...] = (dy * ov * (sig * (1.0 + z * (1.0 - sig)))).astype(dp_ref.dtype)
        dov = dy * (z * sig)
        dov_ref[...] = dov.astype(dov_ref.dtype)
        prod = dov * ov
        for h in range(B_HEADS):
            cols = slice(h * HEAD_DIM, (h + 1) * HEAD_DIM)
            dl_ref[:, cols] = jnp.broadcast_to(jnp.sum(prod[:, cols], axis=-1, keepdims=True), (tm, HEAD_DIM))

    spec = pl.BlockSpec((tm, Wg), lambda i: (i, 0))
    zspec = pl.BlockSpec((tm, Wg), lambda i: (i, 3 * ngr))
    return pl.pallas_call(
        body, name=name, grid=(M // tm,),
        in_specs=[pl.BlockSpec((tm, Dm), lambda i: (i, 0)), pl.BlockSpec((Wg, Dm), lambda i: (0, 0)), spec, zspec],
        out_specs=[spec, spec, zspec],
        out_shape=[jax.ShapeDtypeStruct((M, Wg), ACT), jax.ShapeDtypeStruct((M, Wg), F32),
                   jax.ShapeDtypeStruct(proj.shape, ACT)],
        compiler_params=_cp("parallel"),
    )(dout, w_out, o, proj)


def _b_attn_bwd(qk, proj, dov, lse, delta, dproj, g, name, comm=None):
    M = qk.shape[0]
    D = B_DILATIONS[g]
    ngr = len(B_DILATIONS)
    T = _attn_tile(D, M)
    P = HEAD_DIM * D
    nsb = T // P
    nt = M // T
    Wg = B_HEADS * HEAD_DIM
    scale = np.float32(1.0 / np.sqrt(HEAD_DIM))
    shift = T - P

    def body(q_ref, k_ref, v_ref, do_ref, l_ref, dl_ref, dp_any, dq_ref, dk_ref, dv_ref,
             qs, dos, lss, dls, ks, vs, dqs, dks, dvs):
        n = pl.program_id(1)

        @pl.when(n == 0)
        def _():
            ks[0:P, :] = jnp.zeros((P, HEAD_DIM), F32)
            vs[0:P, :] = jnp.zeros((P, HEAD_DIM), F32)
            dks[0:T, :] = jnp.zeros((T, HEAD_DIM), F32)
            dvs[0:T, :] = jnp.zeros((T, HEAD_DIM), F32)

        dks[T:2 * T, :] = jnp.zeros((T, HEAD_DIM), F32)
        dvs[T:2 * T, :] = jnp.zeros((T, HEAD_DIM), F32)

        @pl.when(n < nt)
        def _():
            qs[...] = q_ref[...].astype(F32)
            dos[...] = do_ref[...].astype(F32)
            lss[...] = l_ref[...]
            dls[...] = dl_ref[...]
            ks[P:P + T, :] = k_ref[...].astype(F32)
            vs[P:P + T, :] = v_ref[...].astype(F32)

            for b in range(nsb):
                mask = _attn_mask(n * (T // D) + b * CHUNK)
                for r in range(D):
                    start = b * P + r
                    qsl = _strided(start, CHUNK, D)
                    ksl = _strided(start, 2 * CHUNK, D)
                    dsl = _strided(start + shift, 2 * CHUNK, D)
                    q = qs[qsl, :]
                    do = dos[qsl, :]
                    k = ks[ksl, :]
                    v = vs[ksl, :]
                    s = _dot_nt(q, k) * scale
                    p = jnp.where(mask, jnp.exp(s - lss[qsl, :][:, :1]), 0.0)
                    dvs[dsl, :] += _dot_tn(p, do)
                    dp = _dot_nt(do, v)
                    ds = (p * (dp - dls[qsl, :][:, :1]) * scale).astype(MXU)
                    dqs[qsl, :] = _dot(ds, k)
                    dks[dsl, :] += _dot_tn(ds, q)

        dq_ref[...] = dqs[...].astype(dq_ref.dtype)
        dk_ref[...] = dks[0:T, :].astype(dk_ref.dtype)
        dv_ref[...] = dvs[0:T, :].astype(dv_ref.dtype)
        dks[0:T, :] = dks[T:2 * T, :]
        dvs[0:T, :] = dvs[T:2 * T, :]
        ks[0:P, :] = ks[T:T + P, :]
        vs[0:P, :] = vs[T:T + P, :]

    blk = (T, HEAD_DIM)
    cur = lambda n: jnp.minimum(n, nt - 1)
    prv = lambda n: jnp.maximum(n - 1, 0)
    return _launch(
        body, name=name, grid=(B_HEADS, nt + 1),
        in_specs=[pl.BlockSpec(blk, lambda h, n: (cur(n), g * B_HEADS + h)),
                  pl.BlockSpec(blk, lambda h, n: (cur(n), (ngr + g) * B_HEADS + h)),
                  pl.BlockSpec(blk, lambda h, n: (cur(n), (2 * ngr + g) * B_HEADS + h)),
                  pl.BlockSpec(blk, lambda h, n: (cur(n), h)),
                  pl.BlockSpec(blk, lambda h, n: (cur(n), h)),
                  pl.BlockSpec(blk, lambda h, n: (cur(n), h)),
                  pl.BlockSpec(memory_space=pl.ANY)],
        out_specs=[pl.BlockSpec(blk, lambda h, n: (cur(n), h)),
                   pl.BlockSpec(blk, lambda h, n: (prv(n), h)),
                   pl.BlockSpec(blk, lambda h, n: (prv(n), (2 * ngr + g) * B_HEADS + h))],
        out_shape=[jax.ShapeDtypeStruct((M, Wg), ACT), jax.ShapeDtypeStruct((M, Wg), ACT),
                   jax.ShapeDtypeStruct(dproj.shape, dproj.dtype)],
        scratch=[pltpu.VMEM((T, HEAD_DIM), F32)] * 4
        + [pltpu.VMEM((P + T, HEAD_DIM), F32)] * 2
        + [pltpu.VMEM((T, HEAD_DIM), F32)]
        + [pltpu.VMEM((2 * T, HEAD_DIM), F32)] * 2,
        aliases={6: 2},
        args=(qk, qk, proj, dov, lse, delta, dproj), sem=("parallel", "arbitrary"), comm=comm)


def _coords():
    return lax.axis_index("x"), lax.axis_index("y"), lax.axis_index("c")


def _gather_blocks(x_ref, out_ref, send_sems, recv_sems, local_sem):
    x, y, c = _coords()
    me, sibling = (x, y, c), (x, y, 1 - c)
    chips = [(1 - x, y), (x, 1 - y), (1 - x, 1 - y)]

    def slot(px, py, pc):
        return out_ref.at[4 * px + 2 * py + pc]

    def copy(k, block, to, src=None):
        return pltpu.make_async_remote_copy(
            src_ref=slot(*block) if src is None else src, dst_ref=slot(*block),
            send_sem=send_sems.at[k], recv_sem=recv_sems.at[k], device_id=to, device_id_type=MESH)

    mine_cp = pltpu.make_async_copy(x_ref, slot(*me), local_sem)
    mine_cp.start()
    first = [copy(0, me, sibling, src=x_ref)]
    first += [copy(1 + j, me, (*chip, c), src=x_ref) for j, chip in enumerate(chips)]
    for cp in first:
        cp.start()
    passed = [copy(4 + j, (*chip, c), sibling) for j, chip in enumerate(chips)]
    for j, chip in enumerate(chips):
        copy(1 + j, (*chip, c), me).wait_recv()
        passed[j].start()
    copy(0, sibling, me).wait_recv()
    for j, chip in enumerate(chips):
        copy(4 + j, (*chip, 1 - c), me).wait_recv()
    for cp in first + passed:
        cp.wait_send()
    mine_cp.wait()


def _all_gather_hbm(mine):
    R, C = mine.shape

    def body(x_ref, out_ref, send_sems, recv_sems, local_sem):
        _gather_blocks(x_ref, out_ref, send_sems, recv_sems, local_sem)

    return pl.pallas_call(
        body, name="ag_weights",
        in_specs=[pl.BlockSpec(memory_space=pl.ANY)],
        out_specs=pl.BlockSpec(memory_space=pl.ANY),
        out_shape=jax.ShapeDtypeStruct((N_DEV, R, C), mine.dtype),
        scratch_shapes=[pltpu.SemaphoreType.DMA((7,)), pltpu.SemaphoreType.DMA((7,)), pltpu.SemaphoreType.DMA],
    )(mine)


def _all_reduce_small(part):
    R, C = part.shape

    def body(x_ref, tot_ref, gath, send_sems, recv_sems, local_sem):
        _gather_blocks(x_ref, gath, send_sems, recv_sems, local_sem)
        acc = gath[0]
        for d in range(1, N_DEV):
            acc = acc + gath[d]
        tot_ref[...] = acc

    return pl.pallas_call(
        body, name="ar_small",
        in_specs=[pl.BlockSpec(memory_space=pltpu.VMEM)],
        out_specs=pl.BlockSpec(memory_space=pltpu.VMEM),
        out_shape=jax.ShapeDtypeStruct((R, C), F32),
        scratch_shapes=[pltpu.VMEM((N_DEV, R, C), F32),
                        pltpu.SemaphoreType.DMA((7,)), pltpu.SemaphoreType.DMA((7,)), pltpu.SemaphoreType.DMA],
        compiler_params=pltpu.CompilerParams(vmem_limit_bytes=VMEM_LIMIT),
    )(part)


def _remote(src, dst, send_sems, recv_sems, k, peer):
    return pltpu.make_async_remote_copy(src_ref=src, dst_ref=dst, send_sem=send_sems.at[k], recv_sem=recv_sems.at[k],
                                        device_id=peer, device_id_type=MESH)


def _ag_send(mine):
    R, C = mine.shape

    def make(c_in, c_out, send_sems, recv_sems, local_sem):
        x, y, c = _coords()
        src, dst = c_in[0], c_out[0].at[4 * x + 2 * y + c]
        peers = [(x, y, 1 - c), (1 - x, y, c), (x, 1 - y, c), (1 - x, 1 - y, c)]
        return [pltpu.make_async_copy(src, dst, local_sem.at[0])] + [
            _remote(src, dst, send_sems, recv_sems, k, peer) for k, peer in enumerate(peers)]

    return _Comm([mine], [jax.ShapeDtypeStruct((N_DEV, R, C), mine.dtype)], 4, make)


def _ag_forward(gath):
    def make(c_in, c_out, send_sems, recv_sems, local_sem):
        x, y, c = _coords()
        buf = c_out[0]
        chips = [(1 - x, y), (x, 1 - y), (1 - x, 1 - y)]
        return [_remote(buf.at[4 * px + 2 * py + c], buf.at[4 * px + 2 * py + c], send_sems, recv_sems, j, (x, y, 1 - c))
                for j, (px, py) in enumerate(chips)]

    return _Comm([gath], [jax.ShapeDtypeStruct(gath.shape, gath.dtype)], 3, make, aliases={0: 0})


def _rs_sibling(grads):
    n = len(grads)

    def make(c_in, c_out, send_sems, recv_sems, local_sem):
        x, y, c = _coords()
        return [_remote(c_in[a].at[pl.ds(4 * (1 - c), 4)], c_out[a], send_sems, recv_sems, a, (x, y, 1 - c))
                for a in range(n)]

    return _Comm(grads, [jax.ShapeDtypeStruct((4,) + g.shape[1:], g.dtype) for g in grads], n, make)


def _rs_chips(parts):
    n = len(parts)

    def make(c_in, c_out, send_sems, recv_sems, local_sem):
        x, y, c = _coords()
        peers = [(x, 1 - y, c), (1 - x, y, c), (1 - x, 1 - y, c)]
        return [_remote(c_in[a].at[k], c_out[a].at[k], send_sems, recv_sems, 3 * a + k, peer)
                for a in range(n) for k, peer in enumerate(peers)]

    return _Comm(parts, [jax.ShapeDtypeStruct(p.shape, p.dtype) for p in parts], 3 * n, make)


def _row_tile(rows, cols):
    tr = min(rows, 1 << int(np.log2((1 << 18) // cols)))
    assert rows % tr == 0
    return tr


def _chip_partials(coords, g, r1, name):
    _, rows, C = g.shape
    tr = _row_tile(rows, C)

    def body(co_ref, g_ref, r_ref, o_ref):
        o_ref[...] = (g_ref[...] + r_ref[...]).astype(o_ref.dtype)

    def chip(k, co):
        return jnp.bitwise_xor(2 * co[0] + co[1], k + 1)

    return pl.pallas_call(
        body, name=name,
        grid_spec=pltpu.PrefetchScalarGridSpec(
            num_scalar_prefetch=1, grid=(3, rows // tr),
            in_specs=[pl.BlockSpec((None, tr, C), lambda k, t, co: (4 * co[2] + chip(k, co), t, 0)),
                      pl.BlockSpec((None, tr, C), lambda k, t, co: (chip(k, co), t, 0))],
            out_specs=pl.BlockSpec((None, tr, C), lambda k, t, co: (k, t, 0))),
        out_shape=jax.ShapeDtypeStruct((3, rows, C), WIRE),
        compiler_params=_cp("parallel", "parallel"),
    )(coords, g, r1)


def _adam_math(w, g, m, v):
    m = ADAM_B1 * m + (1.0 - ADAM_B1) * g
    v = ADAM_B2 * v + (1.0 - ADAM_B2) * (g * g)
    m_hat = m / (1.0 - ADAM_B1 ** ADAM_STEP)
    v_hat = v / (1.0 - ADAM_B2 ** ADAM_STEP)
    delta = -ADAM_LR * (m_hat / (jnp.sqrt(v_hat) + ADAM_EPS) + ADAM_WD * w)
    return delta, m, v


def _adamw_sharded(coords, w, m, v, g, r1, r2, name):
    rows, C = w.shape
    tr = _row_tile(rows, C)

    def body(co_ref, w_ref, m_ref, v_ref, g_ref, r1_ref, r2_ref, go_ref, d_ref, mo_ref, vo_ref):
        grad = g_ref[...] + r1_ref[...]
        for k in range(3):
            grad = grad + r2_ref[k].astype(F32)
        go_ref[...] = grad
        d_ref[...], mo_ref[...], vo_ref[...] = _adam_math(w_ref[...], grad, m_ref[...], v_ref[...])

    spec = pl.BlockSpec((tr, C), lambda t, co: (t, 0))
    return pl.pallas_call(
        body, name=name,
        grid_spec=pltpu.PrefetchScalarGridSpec(
            num_scalar_prefetch=1, grid=(rows // tr,),
            in_specs=[spec, spec, spec,
                      pl.BlockSpec((None, tr, C), lambda t, co: (4 * co[2] + 2 * co[0] + co[1], t, 0)),
                      pl.BlockSpec((None, tr, C), lambda t, co: (2 * co[0] + co[1], t, 0)),
                      pl.BlockSpec((3, tr, C), lambda t, co: (0, t, 0))],
            out_specs=[spec] * 4),
        out_shape=[jax.ShapeDtypeStruct((rows, C), F32)] * 4,
        compiler_params=_cp("parallel"),
    )(coords, w, m, v, g, r1, r2)


def _adamw_small(w, g, m, v, name):
    def body(w_ref, g_ref, m_ref, v_ref, d_ref, mo_ref, vo_ref):
        d_ref[...], mo_ref[...], vo_ref[...] = _adam_math(w_ref[...], g_ref[...], m_ref[...], v_ref[...])

    return pl.pallas_call(
        body, name=name, out_shape=[jax.ShapeDtypeStruct(w.shape, F32)] * 3,
        in_specs=[pl.BlockSpec(memory_space=pltpu.VMEM)] * 4,
        out_specs=[pl.BlockSpec(memory_space=pltpu.VMEM)] * 3,
    )(w, g, m, v)


def _wire_rows(a, cols):
    if a.dtype != WIRE:
        a = lax.bitcast_convert_type(a, WIRE)
    flat = a.reshape(-1)
    pad = (-flat.shape[0]) % (16 * cols)
    if pad:
        flat = jnp.concatenate([flat, jnp.zeros((pad,), WIRE)])
    return flat.reshape(-1, cols)


class _Pack:
    def __init__(self, pieces, cols):
        rows = [_wire_rows(p, cols) for p in pieces]
        self.offs = np.cumsum([0] + [r.shape[0] for r in rows])
        self.packed = jnp.concatenate(rows, axis=0)

    def piece(self, gath, i):
        return gath[:, int(self.offs[i]):int(self.offs[i + 1]), :]

    def piece_f32(self, gath, i, shape):
        flat = self.piece(gath, i).reshape(N_DEV, -1)
        n = int(np.prod(shape))
        if WIRE == F32:
            return flat[:, :n].reshape((N_DEV,) + shape)
        return lax.bitcast_convert_type(flat[:, :2 * n].reshape((N_DEV,) + shape + (2,)), F32)


def _reduce_scatter_adds(coords, grads, r1s, tag):
    return [_chip_partials(coords, g, r, f"rs_add_{tag}{i}") for i, (g, r) in enumerate(zip(grads, r1s))]


def kernel(x, norm_gain, a_w_in, a_v_gain, a_w_s, a_b_s, a_w_out, b_w_in, b_q_gain, b_k_gain, b_w_out, c_w_in, c_w_grp, c_scale, c_w_out, loss_target, m_norm_gain, m_a_w_in, m_a_v_gain, m_a_w_s, m_a_b_s, m_a_w_out, m_b_w_in, m_b_q_gain, m_b_k_gain, m_b_w_out, m_c_w_in, m_c_w_grp, m_c_scale, m_c_w_out, v_norm_gain, v_a_w_in, v_a_v_gain, v_a_w_s, v_a_b_s, v_a_w_out, v_b_w_in, v_b_q_gain, v_b_k_gain, v_b_w_out, v_c_w_in, v_c_w_grp, v_c_scale, v_c_w_out):
    cx, cy, cc = _coords()
    coords = jnp.stack([cx, cy, cc]).astype(jnp.int32)
    dev = 4 * cx + 2 * cy + cc
    Dm = x.shape[2]

    xs, tgt = x[0], loss_target[0]
    tables = _rope_tables(xs.shape[0])
    ng = lambda i: norm_gain[i:i + 1]
    ngr = len(B_DILATIONS)
    bst = [a_b_s[l].T for l in range(2)]
    b_gains = jnp.concatenate([b_q_gain[0], b_k_gain[0], jnp.zeros((2, HEAD_DIM), F32)], axis=0)
    nla, nlb, nlc = a_w_in.shape[2], b_w_in.shape[2], c_w_in.shape[2]
    ngp, rlc, cgc = c_w_grp.shape[1:]
    wire = lambda w: w.astype(WIRE)

    pk0 = _Pack([wire(a_w_in[0]), wire(a_w_out[0]), a_v_gain, c_scale], Dm)
    pk1 = _Pack([wire(b_w_in[0]), wire(b_w_out[0])], Dm)
    pk2 = _Pack([wire(c_w_in[0]), wire(c_w_grp[0]), wire(c_w_out[0]), wire(a_w_in[1]), wire(a_w_out[1])], Dm)

    g0 = _all_gather_hbm(pk0.packed)
    wa_in0 = pk0.piece(g0, 0).reshape(N_DEV, Dm, nla)
    wa_out0 = pk0.piece(g0, 1).reshape(-1, Dm)
    a_vg = pk0.piece_f32(g0, 2, a_v_gain.shape).transpose(1, 0, 2).reshape(a_v_gain.shape[0], -1)
    c_sc = pk0.piece_f32(g0, 3, c_scale.shape).transpose(1, 0, 2).reshape(1, -1)

    h0, p0, g1 = _norm_proj(xs, ng(0), wa_in0, "l0_proj", comm=_ag_send(pk1.packed))
    y0, g1 = _a_mid(p0, a_vg[0:1], a_w_s[0], bst[0], "l0_mid", comm=_ag_forward(g1))
    x1 = _out_proj(xs, y0, wa_out0, "l0_out")
    wb_in = pk1.piece(g1, 0).reshape(N_DEV, Dm, nlb)
    wb_out = pk1.piece(g1, 1).reshape(-1, Dm)

    h1, p1, g2 = _norm_proj(x1, ng(1), wb_in, "l1_proj", comm=_ag_send(pk2.packed))
    qk, g2 = _b_qk_fwd(p1, tables, b_gains, "l1_qk", comm=_ag_forward(g2))
    ogs, lgs = zip(*[_b_attn_fwd(qk, p1, g, f"l1_attn{g}") for g in range(ngr)])
    y1, o1, lse = _b_combine(ogs, lgs, p1, "l1_comb")
    x2 = _out_proj(x1, y1, wb_out, "l1_out")
    wc_in = pk2.piece(g2, 0).reshape(N_DEV, Dm, nlc)
    wc_grp = pk2.piece(g2, 1).reshape(N_DEV, ngp, rlc, cgc).transpose(1, 0, 2, 3).reshape(ngp, N_DEV * rlc, cgc)
    wc_out = pk2.piece(g2, 2).reshape(-1, Dm)
    wa_in1 = pk2.piece(g2, 3).reshape(N_DEV, Dm, nla)
    wa_out1 = pk2.piece(g2, 4).reshape(-1, Dm)

    h2, p2 = _norm_proj(x2, ng(2), wc_in, "l2_proj")
    y2 = _c_mid(p2, wc_grp, c_sc, "l2_mid")
    x3 = _out_proj(x2, y2, wc_out, "l2_out")
    h3, p3 = _norm_proj(x3, ng(3), wa_in1, "l3_proj")
    y3, = _a_mid(p3, a_vg[1:2], a_w_s[1], bst[1], "l3_mid")
    x4 = _out_proj(x3, y3, wa_out1, "l3_out")
    loss_local, dx4 = _loss_head(x4, tgt)
    loss = lax.psum(loss_local, ("x", "y", "c"))

    flat3 = lambda g: g.reshape(N_DEV, -1, g.shape[-1])
    dp3, dws1, dbs1, dvg1 = _a_bwd(dx4, wa_out1, p3, a_vg[1:2], a_w_s[1], bst[1], "l3_bwd")
    grads3 = [_dw_in(h3, dp3, "l3_dwin"), _dw_out(y3, dx4, "l3_dwout")]
    dx3, dg3, *r1_3 = _dh_norm_bwd(dp3, wa_in1, x3, ng(3), dx4, "l3_dh", comm=_rs_sibling(grads3))
    parts3 = _reduce_scatter_adds(coords, grads3, r1_3, "l3_")

    dd, dz, gc_grp, dsc, *r2_3 = _c_bwd1(dx3, wc_out, p2, wc_grp, c_sc, "l2_bwd1", comm=_rs_chips(parts3))
    dp2 = _c_bwd2(dd, dz, "l2_bwd2")
    grads2 = [_dw_in(h2, dp2, "l2_dwin"), _dw_out(y2, dx3, "l2_dwout"), flat3(gc_grp)]
    dx2, dg2, *r1_2 = _dh_norm_bwd(dp2, wc_in, x2, ng(2), dx3, "l2_dh", comm=_rs_sibling(grads2))
    parts2 = _reduce_scatter_adds(coords, grads2, r1_2, "l2_")

    dov, delta, dp1 = _b_bwd_pre(dx2, wb_out, o1, p1, "l1_bwdpre")
    dqs, dks, r2_2 = [], [], None
    for g in range(ngr):
        dq, dk, dp1, *rest = _b_attn_bwd(qk, p1, dov, lse, delta, dp1, g, f"l1_attnbwd{g}",
                                         comm=_rs_chips(parts2) if g == 0 else None)
        if g == 0:
            r2_2 = rest
        dqs.append(dq)
        dks.append(dk)
    dp1, dgains = _b_qk_bwd(dqs, dks, p1, tables, b_gains, dp1, "l1_qkbwd")
    grads1 = [_dw_in(h1, dp1, "l1_dwin"), _dw_out(y1, dx2, "l1_dwout")]
    dx1, dg1, *r1_1 = _dh_norm_bwd(dp1, wb_in, x1, ng(1), dx2, "l1_dh", comm=_rs_sibling(grads1))
    parts1 = _reduce_scatter_adds(coords, grads1, r1_1, "l1_")

    dp0, dws0, dbs0, dvg0, *r2_1 = _a_bwd(dx1, wa_out0, p0, a_vg[0:1], a_w_s[0], bst[0], "l0_bwd", comm=_rs_chips(parts1))
    grads0 = [_dw_in(h0, dp0, "l0_dwin"), _dw_out(y0, dx1, "l0_dwout")]
    r1_0 = _run_comm(_rs_sibling(grads0), "l0_rs_sibling")
    parts0 = _reduce_scatter_adds(coords, grads0, r1_0, "l0_")
    dx0, dg0, *r2_0 = _dh_norm_bwd(dp0, wa_in0, xs, ng(0), dx1, "l0_dh", comm=_rs_chips(parts0))

    small = dict(norm=jnp.concatenate([dg0, dg1, dg2, dg3], axis=0), a_ws=jnp.stack([dws0, dws1]),
                 a_bs=jnp.stack([dbs0.T, dbs1.T]), b_gains=dgains, a_vg=jnp.concatenate([dvg0, dvg1], axis=0), c_sc=dsc)

    order = ["norm", "a_ws", "a_bs", "b_gains", "a_vg", "c_sc"]
    rows = [small[k].reshape(-1, 128) for k in order]
    roff = np.cumsum([0] + [r.shape[0] for r in rows])
    tot = _all_reduce_small(jnp.concatenate(rows, axis=0))
    sm = {k: tot[int(roff[i]):int(roff[i + 1])].reshape(small[k].shape) for i, k in enumerate(order)}
    vl = a_v_gain.shape[1]
    g_small = dict(
        norm_gain=sm["norm"], a_w_s=sm["a_ws"], a_b_s=sm["a_bs"],
        b_q_gain=sm["b_gains"][None, 0:3], b_k_gain=sm["b_gains"][None, 3:6],
        a_v_gain=lax.dynamic_slice_in_dim(sm["a_vg"], dev * vl, vl, axis=1),
        c_scale=lax.dynamic_slice_in_dim(sm["c_sc"], dev * vl, vl, axis=1),
    )

    shares = dict(
        a_w_in=[(grads0[0], r1_0[0], r2_0[0]), (grads3[0], r1_3[0], r2_3[0])],
        a_w_out=[(grads0[1], r1_0[1], r2_0[1]), (grads3[1], r1_3[1], r2_3[1])],
        b_w_in=[(grads1[0], r1_1[0], r2_1[0])], b_w_out=[(grads1[1], r1_1[1], r2_1[1])],
        c_w_in=[(grads2[0], r1_2[0], r2_2[0])], c_w_out=[(grads2[1], r1_2[1], r2_2[1])],
        c_w_grp=[(grads2[2], r1_2[2], r2_2[2])])

    params = dict(a_w_in=a_w_in, a_w_out=a_w_out, b_w_in=b_w_in, b_w_out=b_w_out, c_w_in=c_w_in, c_w_grp=c_w_grp, c_w_out=c_w_out,
                  norm_gain=norm_gain, a_v_gain=a_v_gain, a_w_s=a_w_s, a_b_s=a_b_s, b_q_gain=b_q_gain, b_k_gain=b_k_gain, c_scale=c_scale)
    moms = dict(a_w_in=(m_a_w_in, v_a_w_in), a_w_out=(m_a_w_out, v_a_w_out), b_w_in=(m_b_w_in, v_b_w_in), b_w_out=(m_b_w_out, v_b_w_out),
                c_w_in=(m_c_w_in, v_c_w_in), c_w_grp=(m_c_w_grp, v_c_w_grp), c_w_out=(m_c_w_out, v_c_w_out),
                norm_gain=(m_norm_gain, v_norm_gain), a_v_gain=(m_a_v_gain, v_a_v_gain), a_w_s=(m_a_w_s, v_a_w_s),
                a_b_s=(m_a_b_s, v_a_b_s), b_q_gain=(m_b_q_gain, v_b_q_gain), b_k_gain=(m_b_k_gain, v_b_k_gain),
                c_scale=(m_c_scale, v_c_scale))
    grad, delta, new_m, new_v = {}, {}, {}, {}
    for pname, layers in shares.items():
        w, (m, v) = params[pname], moms[pname]
        C = w.shape[-1]
        per_layer = [_adamw_sharded(coords, w[l].reshape(-1, C), m[l].reshape(-1, C), v[l].reshape(-1, C), g, r1, r2,
                                    f"adamw_{pname}{l}") for l, (g, r1, r2) in enumerate(layers)]
        grad[pname], delta[pname], new_m[pname], new_v[pname] = [
            jnp.stack([o.reshape(w.shape[1:]) for o in outs]) for outs in zip(*per_layer)]
    for pname, g in g_small.items():
        w = params[pname]
        C = w.shape[-1]
        outs = _adamw_small(w.reshape(-1, C), g.reshape(-1, C), moms[pname][0].reshape(-1, C), moms[pname][1].reshape(-1, C),
                            f"adamw_{pname}")
        grad[pname] = g.reshape(w.shape)
        delta[pname], new_m[pname], new_v[pname] = [o.reshape(w.shape) for o in outs]

    wnames = ["norm_gain", "a_w_in", "a_v_gain", "a_w_s", "a_b_s", "a_w_out", "b_w_in", "b_q_gain", "b_k_gain", "b_w_out",
              "c_w_in", "c_w_grp", "c_scale", "c_w_out"]
    return (loss, dx0[None], *[grad[n] for n in wnames], *[delta[n] for n in wnames],
            *[new_m[n] for n in wnames], *[new_v[n] for n in wnames])
```

```python
import functools

import numpy as np
import jax
import jax.numpy as jnp
from jax import lax
from jax.experimental import pallas as pl
from jax.experimental.pallas import tpu as pltpu

F32 = jnp.float32
MXU = jnp.bfloat16
ACT = jnp.bfloat16
WIRE = jnp.bfloat16

EPS = 1e-6
CHUNK = 128
A_GROUPS = 8
HEAD_DIM = 128
B_HEADS = 8
B_DILATIONS = (1, 4, 16)
ROPE_DIM = 32
ROPE_THETA = 500000.0
POOL_SIZES = (2, 4, 8, 16)
POOL_HALO = 16
N_DEV = 8
NEG = -1e30

ADAM_LR, ADAM_B1, ADAM_B2, ADAM_EPS, ADAM_WD, ADAM_STEP = 0.001, 0.9, 0.999, 1e-08, 0.01, 10

VMEM_LIMIT = 56 * 1024 * 1024
MESH = pl.DeviceIdType.MESH


def _cp(*sem):
    return pltpu.CompilerParams(dimension_semantics=sem, vmem_limit_bytes=VMEM_LIMIT)


def _sigmoid(z):
    return 1.0 / (1.0 + jnp.exp(-z))


def _dot(a, b):
    return jnp.dot(a.astype(MXU), b.astype(MXU), preferred_element_type=F32)


def _dot_nt(a, b):
    return lax.dot_general(a.astype(MXU), b.astype(MXU), (((1,), (1,)), ((), ())), preferred_element_type=F32)


def _dot_tn(a, b):
    return lax.dot_general(a.astype(MXU), b.astype(MXU), (((0,), (0,)), ((), ())), preferred_element_type=F32)


def _chunk_slot(d):
    return (d % 2) * 4 + d // 2


class _Comm:
    def __init__(self, inputs, out_shapes, n_remote, make, aliases=None, n_local=1):
        self.inputs = list(inputs)
        self.out_shapes = list(out_shapes)
        self.n_remote = n_remote
        self.n_local = n_local
        self.make = make
        self.aliases = dict(aliases or {})

    def sems(self):
        return [pltpu.SemaphoreType.DMA((self.n_remote,)), pltpu.SemaphoreType.DMA((self.n_remote,)),
                pltpu.SemaphoreType.DMA((self.n_local,))]


_HBM = pl.BlockSpec(memory_space=pl.ANY)


def _launch(body, *, name, grid, in_specs, out_specs, out_shape, args, sem, scratch=(), aliases=None, comm=None):
    in_specs, out_specs, out_shape, scratch = list(in_specs), list(out_specs), list(out_shape), list(scratch)
    aliases = dict(aliases or {})
    if comm is None:
        return pl.pallas_call(body, name=name, grid=grid, in_specs=in_specs, out_specs=out_specs, out_shape=out_shape,
                              scratch_shapes=scratch, input_output_aliases=aliases, compiler_params=_cp(*sem))(*args)
    n_in, n_out, n_sc = len(in_specs), len(out_specs), len(scratch)
    nci, nco = len(comm.inputs), len(comm.out_shapes)

    def hosted(*refs):
        b_in, c_in = refs[:n_in], refs[n_in:n_in + nci]
        o0 = n_in + nci
        b_out, c_out = refs[o0:o0 + n_out], refs[o0 + n_out:o0 + n_out + nco]
        s0 = o0 + n_out + nco
        b_sc, sems = refs[s0:s0 + n_sc], refs[s0 + n_sc:]
        ids = [pl.program_id(a) for a in range(len(grid))]
        first = functools.reduce(jnp.logical_and, [i == 0 for i in ids])
        last = functools.reduce(jnp.logical_and, [i == g - 1 for i, g in zip(ids, grid)])

        @pl.when(first)
        def _():
            for cp in comm.make(c_in, c_out, *sems):
                cp.start()

        body(*b_in, *b_out, *b_sc)

        @pl.when(last)
        def _():
            for cp in comm.make(c_in, c_out, *sems):
                cp.wait()

    for ci, co in comm.aliases.items():
        aliases[n_in + ci] = n_out + co
    return pl.pallas_call(
        hosted, name=name, grid=grid, in_specs=in_specs + [_HBM] * nci, out_specs=out_specs + [_HBM] * nco,
        out_shape=out_shape + comm.out_shapes, scratch_shapes=scratch + comm.sems(),
        input_output_aliases=aliases, compiler_params=_cp(*["arbitrary"] * len(grid)))(*args, *comm.inputs)


def _run_comm(comm, name):
    nci, nco = len(comm.inputs), len(comm.out_shapes)

    def body(*refs):
        cps = comm.make(refs[:nci], refs[nci:nci + nco], *refs[nci + nco:])
        for cp in cps:
            cp.start()
        for cp in cps:
            cp.wait()

    return pl.pallas_call(
        body, name=name, in_specs=[_HBM] * nci, out_specs=[_HBM] * nco, out_shape=comm.out_shapes,
        scratch_shapes=comm.sems(), input_output_aliases=dict(comm.aliases))(*comm.inputs)


def _norm_proj(x, gain, w_dm, name, comm=None):
    M, Dm = x.shape
    nd, _, nl = w_dm.shape
    tm = min(M, 1024)

    def body(x_ref, g_ref, w_ref, h_ref, p_ref):
        @pl.when(pl.program_id(1) == 0)
        def _():
            xv = x_ref[...]
            r = lax.rsqrt(jnp.mean(xv * xv, axis=-1, keepdims=True) + EPS)
            h_ref[...] = (xv * r * g_ref[...]).astype(h_ref.dtype)

        p_ref[...] = _dot(h_ref[...], w_ref[...]).astype(p_ref.dtype)

    return _launch(
        body, name=name, grid=(M // tm, nd),
        in_specs=[pl.BlockSpec((tm, Dm), lambda i, j: (i, 0)),
                  pl.BlockSpec((1, Dm), lambda i, j: (0, 0)),
                  pl.BlockSpec((None, Dm, nl), lambda i, j: (j, 0, 0))],
        out_specs=[pl.BlockSpec((tm, Dm), lambda i, j: (i, 0)),
                   pl.BlockSpec((tm, nl), lambda i, j: (i, j))],
        out_shape=[jax.ShapeDtypeStruct((M, Dm), ACT), jax.ShapeDtypeStruct((M, nd * nl), ACT)],
        args=(x, gain, w_dm), sem=("parallel", "arbitrary"), comm=comm)


def _out_proj(x, y, w, name):
    M, Dm = x.shape
    K = y.shape[1]
    tm = min(M, 512)

    def body(x_ref, y_ref, w_ref, o_ref):
        o_ref[...] = x_ref[...] + _dot(y_ref[...], w_ref[...])

    return pl.pallas_call(
        body, name=name, grid=(M // tm,),
        in_specs=[pl.BlockSpec((tm, Dm), lambda i: (i, 0)),
                  pl.BlockSpec((tm, K), lambda i: (i, 0)),
                  pl.BlockSpec((K, Dm), lambda i: (0, 0))],
        out_specs=pl.BlockSpec((tm, Dm), lambda i: (i, 0)),
        out_shape=jax.ShapeDtypeStruct((M, Dm), F32),
        compiler_params=_cp("parallel"),
    )(x, y, w)


def _loss_head(xf, target):
    M, Dm = xf.shape
    tm = min(M, 512)

    def body(x_ref, t_ref, dx_ref, dxa_ref, l_ref):
        @pl.when(pl.program_id(0) == 0)
        def _():
            l_ref[...] = jnp.zeros_like(l_ref)

        err = x_ref[...] - t_ref[...]
        dx = err * (1.0 / Dm)
        dx_ref[...] = dx
        dxa_ref[...] = dx.astype(dxa_ref.dtype)
        l_ref[...] += jnp.sum(err * err) * (0.5 / Dm)

    spec = pl.BlockSpec((tm, Dm), lambda i: (i, 0))
    dx, dxa, l = pl.pallas_call(
        body, name="loss_head", grid=(M // tm,),
        in_specs=[spec] * 2,
        out_specs=[spec, spec, pl.BlockSpec((8, 128), lambda i: (0, 0))],
        out_shape=[jax.ShapeDtypeStruct((M, Dm), F32), jax.ShapeDtypeStruct((M, Dm), ACT),
                   jax.ShapeDtypeStruct((8, 128), F32)],
        compiler_params=_cp("arbitrary"),
    )(xf, target)
    return l[0, 0], dx, dxa


def _dw_in(h, dproj, name):
    M, Dm = h.shape
    nl = dproj.shape[1] // N_DEV
    tt = min(M, 1024)

    def body(a_ref, b_ref, o_ref):
        @pl.when(pl.program_id(1) == 0)
        def _():
            o_ref[...] = jnp.zeros_like(o_ref)

        o_ref[...] += _dot_tn(a_ref[...], b_ref[...])

    return pl.pallas_call(
        body, name=name, grid=(N_DEV, M // tt),
        in_specs=[pl.BlockSpec((tt, Dm), lambda j, t: (t, 0)), pl.BlockSpec((tt, nl), lambda j, t: (t, j))],
        out_specs=pl.BlockSpec((None, Dm, nl), lambda j, t: (_chunk_slot(j), 0, 0)),
        out_shape=jax.ShapeDtypeStruct((N_DEV, Dm, nl), F32),
        compiler_params=_cp("parallel", "arbitrary"),
    )(h, dproj)


def _dw_out(y, dout, name):
    M, K = y.shape
    Dm = dout.shape[1]
    kl = K // N_DEV
    tt = min(M, 512)

    def body(a_ref, b_ref, o_ref):
        @pl.when(pl.program_id(0) == 0)
        def _():
            o_ref[...] = jnp.zeros_like(o_ref)

        b = b_ref[...]
        for j in range(N_DEV):
            o_ref[_chunk_slot(j)] += _dot_tn(a_ref[:, j * kl:(j + 1) * kl], b)

    return pl.pallas_call(
        body, name=name, grid=(M // tt,),
        in_specs=[pl.BlockSpec((tt, K), lambda t: (t, 0)), pl.BlockSpec((tt, Dm), lambda t: (t, 0))],
        out_specs=pl.BlockSpec((N_DEV, kl, Dm), lambda t: (0, 0, 0)),
        out_shape=jax.ShapeDtypeStruct((N_DEV, kl, Dm), F32),
        compiler_params=_cp("arbitrary"),
    )(y, dout)


def _dh_norm_bwd(dproj, w_dm, x, gain, dres, name, comm=None):
    M, Dm = x.shape
    nd, _, nl = w_dm.shape
    tm = min(M, 1024)

    def body(dp_ref, w_ref, x_ref, g_ref, dr_ref, dx_ref, dxa_ref, dg_ref, acc_ref):
        i, j = pl.program_id(0), pl.program_id(1)

        @pl.when(j == 0)
        def _():
            acc_ref[...] = jnp.zeros_like(acc_ref)

        acc_ref[...] += _dot_nt(dp_ref[...], w_ref[...])

        @pl.when(j == nd - 1)
        def _():
            @pl.when(i == 0)
            def _():
                dg_ref[...] = jnp.zeros_like(dg_ref)

            dh = acc_ref[...]
            xv = x_ref[...]
            r = lax.rsqrt(jnp.mean(xv * xv, axis=-1, keepdims=True) + EPS)
            xn = xv * r
            dg_ref[...] += jnp.sum(dh * xn, axis=0, keepdims=True)
            dxn = dh * g_ref[...]
            dx = dr_ref[...] + r * (dxn - xn * jnp.mean(dxn * xn, axis=-1, keepdims=True))
            dx_ref[...] = dx
            dxa_ref[...] = dx.astype(dxa_ref.dtype)

    row = pl.BlockSpec((tm, Dm), lambda i, j: (i, 0))
    return _launch(
        body, name=name, grid=(M // tm, nd),
        in_specs=[pl.BlockSpec((tm, nl), lambda i, j: (i, j)),
                  pl.BlockSpec((None, Dm, nl), lambda i, j: (j, 0, 0)),
                  row, pl.BlockSpec((1, Dm), lambda i, j: (0, 0)), row],
        out_specs=[row, row, pl.BlockSpec((1, Dm), lambda i, j: (0, 0))],
        out_shape=[jax.ShapeDtypeStruct((M, Dm), F32), jax.ShapeDtypeStruct((M, Dm), ACT),
                   jax.ShapeDtypeStruct((1, Dm), F32)],
        scratch=[pltpu.VMEM((tm, Dm), F32)],
        args=(dproj, w_dm, x, gain, dres), sem=("arbitrary", "arbitrary"), comm=comm)


def _tril_mask():
    return lax.broadcasted_iota(jnp.int32, (CHUNK, CHUNK), 0) >= lax.broadcasted_iota(jnp.int32, (CHUNK, CHUNK), 1)


def _a_mid(proj, v_gain, w_s, b_st, name, comm=None):
    M = proj.shape[0]
    W = proj.shape[1] // 3
    gd = W // A_GROUPS
    tm = min(M, 256)

    def body(p_ref, vg_ref, ws_ref, bs_ref, y_ref):
        pv = p_ref[:, W:2 * W].astype(F32)
        r = lax.rsqrt(jnp.mean(pv * pv, axis=-1, keepdims=True) + EPS)
        v = (pv * r * vg_ref[...]).astype(MXU)
        tri = _tril_mask()
        for g in range(A_GROUPS):
            wg = jnp.where(tri, ws_ref[g], 0.0).astype(MXU)
            bcol = bs_ref[:, g:g + 1]
            for c in range(tm // CHUNK):
                rows, cols = slice(c * CHUNK, (c + 1) * CHUNK), slice(g * gd, (g + 1) * gd)
                mixed = jnp.dot(wg, v[rows, cols], preferred_element_type=F32) + bcol
                u = p_ref[rows, g * gd:(g + 1) * gd].astype(F32)
                z = p_ref[rows, 2 * W + g * gd:2 * W + (g + 1) * gd].astype(F32)
                y_ref[rows, cols] = (u * mixed * (z * _sigmoid(z))).astype(y_ref.dtype)

    return _launch(
        body, name=name, grid=(M // tm,),
        in_specs=[pl.BlockSpec((tm, 3 * W), lambda i: (i, 0)),
                  pl.BlockSpec((1, W), lambda i: (0, 0)),
                  pl.BlockSpec((A_GROUPS, CHUNK, CHUNK), lambda i: (0, 0, 0)),
                  pl.BlockSpec((CHUNK, A_GROUPS), lambda i: (0, 0))],
        out_specs=[pl.BlockSpec((tm, W), lambda i: (i, 0))],
        out_shape=[jax.ShapeDtypeStruct((M, W), ACT)],
        args=(proj, v_gain, w_s, b_st), sem=("parallel",), comm=comm)


def _a_bwd(dout, w_out, proj, v_gain, w_s, b_st, name, comm=None):
    M = proj.shape[0]
    W = proj.shape[1] // 3
    Dm = dout.shape[1]
    gd = W // A_GROUPS
    tm = min(M, 256)
    nt = M // tm

    def body(do_ref, wo_ref, p_ref, vg_ref, ws_ref, bs_ref, dp_ref, dws_ref, dbs_ref, dvg_ref, dv_s):
        i = pl.program_id(0)

        @pl.when(i == 0)
        def _():
            dws_ref[...] = jnp.zeros_like(dws_ref)
            dbs_ref[...] = jnp.zeros_like(dbs_ref)
            dvg_ref[...] = jnp.zeros_like(dvg_ref)

        dy = _dot_nt(do_ref[...], wo_ref[...])
        pv = p_ref[:, W:2 * W].astype(F32)
        r = lax.rsqrt(jnp.mean(pv * pv, axis=-1, keepdims=True) + EPS)
        pvn = pv * r
        vg = vg_ref[...]
        v = (pvn * vg).astype(MXU)
        tri = _tril_mask()
        for g in range(A_GROUPS):
            wf = jnp.where(tri, ws_ref[g], 0.0)
            wg = wf.astype(MXU)
            wgt = wf.T.astype(MXU)
            bcol = bs_ref[:, g:g + 1]
            for c in range(tm // CHUNK):
                rows, cols = slice(c * CHUNK, (c + 1) * CHUNK), slice(g * gd, (g + 1) * gd)
                vb = v[rows, cols]
                mixed = jnp.dot(wg, vb, preferred_element_type=F32) + bcol
                u = p_ref[rows, g * gd:(g + 1) * gd].astype(F32)
                z = p_ref[rows, 2 * W + g * gd:2 * W + (g + 1) * gd].astype(F32)
                sig = _sigmoid(z)
                sz = z * sig
                dyb = dy[rows, cols]
                dp_ref[rows, g * gd:(g + 1) * gd] = (dyb * mixed * sz).astype(dp_ref.dtype)
                dp_ref[rows, 2 * W + g * gd:2 * W + (g + 1) * gd] = (
                    dyb * u * mixed * (sig * (1.0 + z * (1.0 - sig)))).astype(dp_ref.dtype)
                dmix = dyb * u * sz
                dws_ref[g] += _dot_nt(dmix, vb)
                dbs_ref[:, g:g + 1] += jnp.sum(dmix, axis=1, keepdims=True)
                dv_s[rows, cols] = jnp.dot(wgt, dmix.astype(MXU), preferred_element_type=F32)
        dv = dv_s[...]
        dvg_ref[...] += jnp.sum(dv * pvn, axis=0, keepdims=True)
        dpvn = dv * vg
        dp_ref[:, W:2 * W] = (r * (dpvn - pvn * jnp.mean(dpvn * pvn, axis=-1, keepdims=True))).astype(dp_ref.dtype)

        @pl.when(i == nt - 1)
        def _():
            for g in range(A_GROUPS):
                dws_ref[g] = jnp.where(tri, dws_ref[g], 0.0)

    return _launch(
        body, name=name, grid=(nt,),
        in_specs=[pl.BlockSpec((tm, Dm), lambda i: (i, 0)),
                  pl.BlockSpec((W, Dm), lambda i: (0, 0)),
                  pl.BlockSpec((tm, 3 * W), lambda i: (i, 0)),
                  pl.BlockSpec((1, W), lambda i: (0, 0)),
                  pl.BlockSpec((A_GROUPS, CHUNK, CHUNK), lambda i: (0, 0, 0)),
                  pl.BlockSpec((CHUNK, A_GROUPS), lambda i: (0, 0))],
        out_specs=[pl.BlockSpec((tm, 3 * W), lambda i: (i, 0)),
                   pl.BlockSpec((A_GROUPS, CHUNK, CHUNK), lambda i: (0, 0, 0)),
                   pl.BlockSpec((CHUNK, A_GROUPS), lambda i: (0, 0)),
                   pl.BlockSpec((1, W), lambda i: (0, 0))],
        out_shape=[jax.ShapeDtypeStruct((M, 3 * W), ACT),
                   jax.ShapeDtypeStruct((A_GROUPS, CHUNK, CHUNK), F32),
                   jax.ShapeDtypeStruct((CHUNK, A_GROUPS), F32),
                   jax.ShapeDtypeStruct((1, W), F32)],
        scratch=[pltpu.VMEM((tm, W), F32)],
        args=(dout, w_out, proj, v_gain, w_s, b_st), sem=("arbitrary",), comm=comm)


def _pool_diff(xg, tail, i, tm, w):
    t = lax.broadcasted_iota(jnp.int32, (tm, tm + POOL_HALO), 0)
    s = lax.broadcasted_iota(jnp.int32, (tm, tm + POOL_HALO), 1)
    off = t - (s - POOL_HALO)
    band = jnp.where((off >= 0) & (off < w), 1.0, 0.0).astype(MXU)
    tail = jnp.where(i > 0, tail, jnp.zeros_like(tail))
    ext = jnp.concatenate([tail, xg], axis=0)
    ssum = jnp.dot(band, ext.astype(MXU), preferred_element_type=F32)
    tglob = i * tm + lax.broadcasted_iota(jnp.int32, (tm, 1), 0)
    cnt = jnp.minimum(tglob + 1, w).astype(F32)
    return ssum / cnt - xg.astype(F32)


def _c_mid(proj, w_grp, scale, name):
    M = proj.shape[0]
    W = proj.shape[1] // 2
    ng = len(POOL_SIZES)
    cg = W // ng
    tm = min(M, 256)
    hb = tm // POOL_HALO

    def body(xc_ref, tail_ref, z_ref, wg_ref, sc_ref, y_ref):
        i = pl.program_id(0)
        for g, w in enumerate(POOL_SIZES):
            cols = slice(g * cg, (g + 1) * cg)
            d = _pool_diff(xc_ref[:, cols], tail_ref[:, cols], i, tm, w)
            mixed = _dot(d, wg_ref[g]) * sc_ref[:, cols]
            z = z_ref[:, cols].astype(F32)
            y_ref[:, cols] = (mixed * (z * _sigmoid(z))).astype(y_ref.dtype)

    return pl.pallas_call(
        body, name=name, grid=(M // tm,),
        in_specs=[pl.BlockSpec((tm, W), lambda i: (i, 0)),
                  pl.BlockSpec((POOL_HALO, W), lambda i: (jnp.maximum(i * hb - 1, 0), 0)),
                  pl.BlockSpec((tm, W), lambda i: (i, 1)),
                  pl.BlockSpec((ng, cg, cg), lambda i: (0, 0, 0)),
                  pl.BlockSpec((1, W), lambda i: (0, 0))],
        out_specs=pl.BlockSpec((tm, W), lambda i: (i, 0)),
        out_shape=jax.ShapeDtypeStruct((M, W), ACT),
        compiler_params=_cp("parallel"),
    )(proj, proj, proj, w_grp, scale)


def _c_bwd1(dout, w_out, proj, w_grp, scale, name, comm=None):
    M = proj.shape[0]
    W = proj.shape[1] // 2
    Dm = dout.shape[1]
    ng = len(POOL_SIZES)
    cg = W // ng
    rl = cg // N_DEV
    tm = min(M, 256)
    hb = tm // POOL_HALO
    nt = M // tm

    def body(do_ref, wo_ref, xc_ref, tail_ref, z_ref, wg_ref, sc_ref, dd_ref, dz_ref, dwg_ref, dsc_ref, acc_ref):
        i = pl.program_id(0)

        @pl.when(i == 0)
        def _():
            acc_ref[...] = jnp.zeros_like(acc_ref)
            dsc_ref[...] = jnp.zeros_like(dsc_ref)

        dy = _dot_nt(do_ref[...], wo_ref[...])
        for g, w in enumerate(POOL_SIZES):
            cols = slice(g * cg, (g + 1) * cg)
            d = _pool_diff(xc_ref[:, cols], tail_ref[:, cols], i, tm, w)
            mr = _dot(d, wg_ref[g])
            sc = sc_ref[:, cols]
            z = z_ref[:, cols].astype(F32)
            sig = _sigmoid(z)
            dyg = dy[:, cols]
            dmixed = dyg * (z * sig)
            dz_ref[:, cols] = (dyg * (mr * sc) * (sig * (1.0 + z * (1.0 - sig)))).astype(dz_ref.dtype)
            dsc_ref[:, cols] += jnp.sum(dmixed * mr, axis=0, keepdims=True)
            dmr = (dmixed * sc).astype(MXU)
            acc_ref[g] += _dot_tn(d, dmr)
            dd_ref[:, cols] = _dot_nt(dmr, wg_ref[g]).astype(dd_ref.dtype)

        @pl.when(i == nt - 1)
        def _():
            for dev in range(N_DEV):
                for g in range(ng):
                    dwg_ref[_chunk_slot(dev), g] = acc_ref[g, dev * rl:(dev + 1) * rl, :]

    return _launch(
        body, name=name, grid=(nt,),
        in_specs=[pl.BlockSpec((tm, Dm), lambda i: (i, 0)),
                  pl.BlockSpec((W, Dm), lambda i: (0, 0)),
                  pl.BlockSpec((tm, W), lambda i: (i, 0)),
                  pl.BlockSpec((POOL_HALO, W), lambda i: (jnp.maximum(i * hb - 1, 0), 0)),
                  pl.BlockSpec((tm, W), lambda i: (i, 1)),
                  pl.BlockSpec((ng, cg, cg), lambda i: (0, 0, 0)),
                  pl.BlockSpec((1, W), lambda i: (0, 0))],
        out_specs=[pl.BlockSpec((tm, W), lambda i: (i, 0)),
                   pl.BlockSpec((tm, W), lambda i: (i, 0)),
                   pl.BlockSpec((N_DEV, ng, rl, cg), lambda i: (0, 0, 0, 0)),
                   pl.BlockSpec((1, W), lambda i: (0, 0))],
        out_shape=[jax.ShapeDtypeStruct((M, W), ACT), jax.ShapeDtypeStruct((M, W), ACT),
                   jax.ShapeDtypeStruct((N_DEV, ng, rl, cg), F32), jax.ShapeDtypeStruct((1, W), F32)],
        scratch=[pltpu.VMEM((ng, cg, cg), F32)],
        args=(dout, w_out, proj, proj, proj, w_grp, scale), sem=("arbitrary",), comm=comm)


def _c_bwd2(dd, dz, name):
    M, W = dd.shape
    ng = len(POOL_SIZES)
    cg = W // ng
    tm = min(M, 256)
    hb = tm // POOL_HALO
    nt = M // tm

    def body(dd_ref, head_ref, dz_ref, dp_ref):
        i = pl.program_id(0)
        s = lax.broadcasted_iota(jnp.int32, (tm, tm + POOL_HALO), 0)
        t = lax.broadcasted_iota(jnp.int32, (tm, tm + POOL_HALO), 1)
        off = t - s
        tglob = i * tm + lax.broadcasted_iota(jnp.int32, (tm + POOL_HALO, 1), 0)
        for g, w in enumerate(POOL_SIZES):
            cols = slice(g * cg, (g + 1) * cg)
            ddg = dd_ref[:, cols].astype(F32)
            head = head_ref[:, cols].astype(F32)
            head = jnp.where(i < nt - 1, head, jnp.zeros_like(head))
            cnt = jnp.minimum(tglob + 1, w).astype(F32)
            ext = (jnp.concatenate([ddg, head], axis=0) / cnt).astype(MXU)
            band = jnp.where((off >= 0) & (off < w), 1.0, 0.0).astype(MXU)
            dp_ref[:, cols] = (jnp.dot(band, ext, preferred_element_type=F32) - ddg).astype(dp_ref.dtype)
        dp_ref[:, W:] = dz_ref[...]

    return pl.pallas_call(
        body, name=name, grid=(nt,),
        in_specs=[pl.BlockSpec((tm, W), lambda i: (i, 0)),
                  pl.BlockSpec((POOL_HALO, W), lambda i: (jnp.minimum((i + 1) * hb, M // POOL_HALO - 1), 0)),
                  pl.BlockSpec((tm, W), lambda i: (i, 0))],
        out_specs=pl.BlockSpec((tm, 2 * W), lambda i: (i, 0)),
        out_shape=jax.ShapeDtypeStruct((M, 2 * W), ACT),
        compiler_params=_cp("parallel"),
    )(dd, dd, dz)


def _rope_tables(S):
    half = ROPE_DIM // 2
    inv_freq = jnp.power(jnp.float32(ROPE_THETA), -jnp.arange(half, dtype=F32) / half)
    ang = jnp.arange(S, dtype=F32)[:, None] * inv_freq[None, :]
    cos, sin = jnp.cos(ang), jnp.sin(ang)
    rest = HEAD_DIM - ROPE_DIM
    cf = jnp.concatenate([cos, cos, jnp.ones((S, rest), F32)], axis=1)
    sf = jnp.concatenate([-sin, sin, jnp.zeros((S, rest), F32)], axis=1)
    return cf, sf


def _swap_matrix():
    half = ROPE_DIM // 2
    a = lax.broadcasted_iota(jnp.int32, (HEAD_DIM, HEAD_DIM), 0)
    e = lax.broadcasted_iota(jnp.int32, (HEAD_DIM, HEAD_DIM), 1)
    hit = ((e < half) & (a == e + half)) | ((e >= half) & (e < 2 * half) & (a == e - half))
    return jnp.where(hit, 1.0, 0.0).astype(MXU)


def _b_qk_fwd(proj, tables, gains, name, comm=None):
    M = proj.shape[0]
    nsl = 2 * len(B_DILATIONS) * B_HEADS
    Wqk = nsl * HEAD_DIM
    tm = min(M, 256)

    def body(p_ref, cf_ref, sf_ref, g_ref, o_ref):
        cf, sf = cf_ref[...], sf_ref[...]
        swap = _swap_matrix()
        for j in range(nsl):
            cols = slice(j * HEAD_DIM, (j + 1) * HEAD_DIM)
            xv = p_ref[:, cols].astype(F32)
            r = lax.rsqrt(jnp.mean(xv * xv, axis=-1, keepdims=True) + EPS)
            xg = xv * g_ref[j // B_HEADS:j // B_HEADS + 1, :]
            hi = xg.astype(MXU)
            lo = (xg - hi.astype(F32)).astype(MXU)
            sw = jnp.dot(hi, swap, preferred_element_type=F32) + jnp.dot(lo, swap, preferred_element_type=F32)
            o_ref[:, cols] = (r * (xg * cf + sw * sf)).astype(o_ref.dtype)

    tspec = pl.BlockSpec((tm, HEAD_DIM), lambda i: (i, 0))
    return _launch(
        body, name=name, grid=(M // tm,),
        in_specs=[pl.BlockSpec((tm, Wqk), lambda i: (i, 0)), tspec, tspec,
                  pl.BlockSpec((8, HEAD_DIM), lambda i: (0, 0))],
        out_specs=[pl.BlockSpec((tm, Wqk), lambda i: (i, 0))],
        out_shape=[jax.ShapeDtypeStruct((M, Wqk), ACT)],
        args=(proj, *tables, gains), sem=("parallel",), comm=comm)


def _b_qk_bwd(dqs, dks, proj, tables, gains, dproj, name):
    M = proj.shape[0]
    ngr = len(B_DILATIONS)
    nsl = 2 * ngr * B_HEADS
    Wqk = nsl * HEAD_DIM
    Wg = B_HEADS * HEAD_DIM
    tm = min(M, 256)

    def body(*refs):
        d_refs = refs[:2 * ngr]
        p_ref, cf_ref, sf_ref, g_ref = refs[2 * ngr:2 * ngr + 4]
        dp_ref, dg_ref = refs[-2], refs[-1]

        @pl.when(pl.program_id(0) == 0)
        def _():
            dg_ref[...] = jnp.zeros_like(dg_ref)

        cf, sf = cf_ref[...], sf_ref[...]
        swap = _swap_matrix()
        for j in range(nsl):
            t, hh = j // B_HEADS, j % B_HEADS
            cols = slice(j * HEAD_DIM, (j + 1) * HEAD_DIM)
            dy = d_refs[t][:, hh * HEAD_DIM:(hh + 1) * HEAD_DIM].astype(F32)
            dxn = dy * cf + jnp.dot((dy * sf).astype(MXU), swap, preferred_element_type=F32)
            xv = p_ref[:, cols].astype(F32)
            r = lax.rsqrt(jnp.mean(xv * xv, axis=-1, keepdims=True) + EPS)
            xh = xv * r
            dg_ref[t:t + 1, :] += jnp.sum(dxn * xh, axis=0, keepdims=True)
            dxh = dxn * g_ref[t:t + 1, :]
            dp_ref[:, cols] = (r * (dxh - xh * jnp.mean(dxh * xh, axis=-1, keepdims=True))).astype(dp_ref.dtype)

    tspec = pl.BlockSpec((tm, HEAD_DIM), lambda i: (i, 0))
    dspec = pl.BlockSpec((tm, Wg), lambda i: (i, 0))
    n_in = 2 * ngr + 5
    return pl.pallas_call(
        body, name=name, grid=(M // tm,),
        in_specs=[dspec] * (2 * ngr) + [pl.BlockSpec((tm, Wqk), lambda i: (i, 0)), tspec, tspec,
                                        pl.BlockSpec((8, HEAD_DIM), lambda i: (0, 0)),
                                        pl.BlockSpec(memory_space=pl.ANY)],
        out_specs=[pl.BlockSpec((tm, Wqk), lambda i: (i, 0)), pl.BlockSpec((8, HEAD_DIM), lambda i: (0, 0))],
        out_shape=[jax.ShapeDtypeStruct(dproj.shape, dproj.dtype), jax.ShapeDtypeStruct((8, HEAD_DIM), F32)],
        input_output_aliases={n_in - 1: 0},
        compiler_params=_cp("arbitrary"),
    )(*dqs, *dks, proj, *tables, gains, dproj)


def _attn_tile(D, M):
    return max(HEAD_DIM * D, min(M, 2048))


def _strided(start, size, D):
    return pl.ds(start, size) if D == 1 else pl.ds(start, size, stride=D)


def _attn_mask(base):
    qi = lax.broadcasted_iota(jnp.int32, (CHUNK, 2 * CHUNK), 0)
    ki = lax.broadcasted_iota(jnp.int32, (CHUNK, 2 * CHUNK), 1)
    return (ki >= qi) & (ki <= qi + CHUNK) & (ki >= CHUNK - base)


def _b_attn_fwd(qk, proj, g, name):
    M = qk.shape[0]
    D = B_DILATIONS[g]
    ngr = len(B_DILATIONS)
    T = _attn_tile(D, M)
    P = HEAD_DIM * D
    nsb = T // P
    Wg = B_HEADS * HEAD_DIM
    scale = np.float32(1.0 / np.sqrt(HEAD_DIM))

    def body(q_ref, k_ref, v_ref, o_ref, l_ref, qs, ks, vs, os_, ls):
        n = pl.program_id(1)

        @pl.when(n == 0)
        def _():
            ks[0:P, :] = jnp.zeros((P, HEAD_DIM), F32)
            vs[0:P, :] = jnp.zeros((P, HEAD_DIM), F32)

        qs[...] = q_ref[...].astype(F32)
        ks[P:P + T, :] = k_ref[...].astype(F32)
        vs[P:P + T, :] = v_ref[...].astype(F32)

        for b in range(nsb):
            mask = _attn_mask(n * (T // D) + b * CHUNK)
            for r in range(D):
                start = b * P + r
                q = qs[_strided(start, CHUNK, D), :]
                k = ks[_strided(start, 2 * CHUNK, D), :]
                v = vs[_strided(start, 2 * CHUNK, D), :]
                s = jnp.where(mask, _dot_nt(q, k) * scale, NEG)
                m = jnp.max(s, axis=-1, keepdims=True)
                p = jnp.exp(s - m)
                l = jnp.sum(p, axis=-1, keepdims=True)
                o = _dot(p, v) / l
                os_[_strided(start, CHUNK, D), :] = o
                ls[_strided(start, CHUNK, D), :] = jnp.broadcast_to(m + jnp.log(l), (CHUNK, HEAD_DIM))

        o_ref[...] = os_[...].astype(o_ref.dtype)
        l_ref[...] = ls[...]
        ks[0:P, :] = ks[T:T + P, :]
        vs[0:P, :] = vs[T:T + P, :]

    blk = (T, HEAD_DIM)
    return pl.pallas_call(
        body, name=name, grid=(B_HEADS, M // T),
        in_specs=[pl.BlockSpec(blk, lambda h, n: (n, g * B_HEADS + h)),
                  pl.BlockSpec(blk, lambda h, n: (n, (ngr + g) * B_HEADS + h)),
                  pl.BlockSpec(blk, lambda h, n: (n, (2 * ngr + g) * B_HEADS + h))],
        out_specs=[pl.BlockSpec(blk, lambda h, n: (n, h)), pl.BlockSpec(blk, lambda h, n: (n, h))],
        out_shape=[jax.ShapeDtypeStruct((M, Wg), ACT), jax.ShapeDtypeStruct((M, Wg), F32)],
        scratch_shapes=[pltpu.VMEM((T, HEAD_DIM), F32), pltpu.VMEM((P + T, HEAD_DIM), F32),
                        pltpu.VMEM((P + T, HEAD_DIM), F32), pltpu.VMEM((T, HEAD_DIM), F32),
                        pltpu.VMEM((T, HEAD_DIM), F32)],
        compiler_params=_cp("parallel", "arbitrary"),
    )(qk, qk, proj)


def _b_combine(os_, ls, proj, name):
    M, Wg = os_[0].shape
    ngr = len(B_DILATIONS)
    tm = min(M, 512)

    def body(*refs):
        o_refs, l_refs, z_ref = refs[:ngr], refs[ngr:2 * ngr], refs[2 * ngr]
        y_ref, o_ref, lse_ref = refs[2 * ngr + 1:]
        ls_ = [r[...] for r in l_refs]
        m = functools.reduce(jnp.maximum, ls_)
        es = [jnp.exp(l - m) for l in ls_]
        tot = functools.reduce(lambda a, b: a + b, es)
        o = functools.reduce(lambda a, b: a + b, [e * r[...].astype(F32) for e, r in zip(es, o_refs)]) / tot
        z = z_ref[...].astype(F32)
        y_ref[...] = (o.astype(F32) * (z * _sigmoid(z))).astype(y_ref.dtype)
        o_ref[...] = o.astype(o_ref.dtype)
        lse_ref[...] = m + jnp.log(tot)

    spec = pl.BlockSpec((tm, Wg), lambda i: (i, 0))
    return pl.pallas_call(
        body, name=name, grid=(M // tm,),
        in_specs=[spec] * (2 * ngr) + [pl.BlockSpec((tm, Wg), lambda i: (i, 3 * ngr))],
        out_specs=[spec] * 3,
        out_shape=[jax.ShapeDtypeStruct((M, Wg), ACT), jax.ShapeDtypeStruct((M, Wg), ACT),
                   jax.ShapeDtypeStruct((M, Wg), F32)],
        compiler_params=_cp("parallel"),
    )(*os_, *ls, proj)


def _b_bwd_pre(dout, w_out, o, proj, name):
    M, Wg = o.shape
    Dm = dout.shape[1]
    ngr = len(B_DILATIONS)
    tm = min(M, 512)

    def body(do_ref, wo_ref, o_ref, z_ref, dov_ref, dl_ref, dp_ref):
        dy = _dot_nt(do_ref[...], wo_ref[...])
        z = z_ref[...].astype(F32)
        sig = _sigmoid(z)
        ov = o_ref[...].astype(F32)
        dp_ref[...] = (dy * ov * (sig * (1.0 + z * (1.0 - sig)))).astype(dp_ref.dtype)
        dov = dy * (z * sig)
        dov_ref[...] = dov.astype(dov_ref.dtype)
        prod = dov * ov
        for h in range(B_HEADS):
            cols = slice(h * HEAD_DIM, (h + 1) * HEAD_DIM)
            dl_ref[:, cols] = jnp.broadcast_to(jnp.sum(prod[:, cols], axis=-1, keepdims=True), (tm, HEAD_DIM))

    spec = pl.BlockSpec((tm, Wg), lambda i: (i, 0))
    zspec = pl.BlockSpec((tm, Wg), lambda i: (i, 3 * ngr))
    return pl.pallas_call(
        body, name=name, grid=(M // tm,),
        in_specs=[pl.BlockSpec((tm, Dm), lambda i: (i, 0)), pl.BlockSpec((Wg, Dm), lambda i: (0, 0)), spec, zspec],
        out_specs=[spec, spec, zspec],
        out_shape=[jax.ShapeDtypeStruct((M, Wg), ACT), jax.ShapeDtypeStruct((M, Wg), F32),
                   jax.ShapeDtypeStruct(proj.shape, ACT)],
        compiler_params=_cp("parallel"),
    )(dout, w_out, o, proj)


def _b_attn_bwd(qk, proj, dov, lse, delta, dproj, g, name, comm=None):
    M = qk.shape[0]
    D = B_DILATIONS[g]
    ngr = len(B_DILATIONS)
    T = _attn_tile(D, M)
    P = HEAD_DIM * D
    nsb = T // P
    nt = M // T
    Wg = B_HEADS * HEAD_DIM
    scale = np.float32(1.0 / np.sqrt(HEAD_DIM))
    shift = T - P

    def body(q_ref, k_ref, v_ref, do_ref, l_ref, dl_ref, dp_any, dq_ref, dk_ref, dv_ref,
             qs, dos, lss, dls, ks, vs, dqs, dks, dvs):
        n = pl.program_id(1)

        @pl.when(n == 0)
        def _():
            ks[0:P, :] = jnp.zeros((P, HEAD_DIM), F32)
            vs[0:P, :] = jnp.zeros((P, HEAD_DIM), F32)
            dks[0:T, :] = jnp.zeros((T, HEAD_DIM), F32)
            dvs[0:T, :] = jnp.zeros((T, HEAD_DIM), F32)

        dks[T:2 * T, :] = jnp.zeros((T, HEAD_DIM), F32)
        dvs[T:2 * T, :] = jnp.zeros((T, HEAD_DIM), F32)

        @pl.when(n < nt)
        def _():
            qs[...] = q_ref[...].astype(F32)
            dos[...] = do_ref[...].astype(F32)
            lss[...] = l_ref[...]
            dls[...] = dl_ref[...]
            ks[P:P + T, :] = k_ref[...].astype(F32)
            vs[P:P + T, :] = v_ref[...].astype(F32)

            for b in range(nsb):
                mask = _attn_mask(n * (T // D) + b * CHUNK)
                for r in range(D):
                    start = b * P + r
                    qsl = _strided(start, CHUNK, D)
                    ksl = _strided(start, 2 * CHUNK, D)
                    dsl = _strided(start + shift, 2 * CHUNK, D)
                    q = qs[qsl, :]
                    do = dos[qsl, :]
                    k = ks[ksl, :]
                    v = vs[ksl, :]
                    s = _dot_nt(q, k) * scale
                    p = jnp.where(mask, jnp.exp(s - lss[qsl, :][:, :1]), 0.0)
                    dvs[dsl, :] += _dot_tn(p, do)
                    dp = _dot_nt(do, v)
                    ds = (p * (dp - dls[qsl, :][:, :1]) * scale).astype(MXU)
                    dqs[qsl, :] = _dot(ds, k)
                    dks[dsl, :] += _dot_tn(ds, q)

        dq_ref[...] = dqs[...].astype(dq_ref.dtype)
        dk_ref[...] = dks[0:T, :].astype(dk_ref.dtype)
        dv_ref[...] = dvs[0:T, :].astype(dv_ref.dtype)
        dks[0:T, :] = dks[T:2 * T, :]
        dvs[0:T, :] = dvs[T:2 * T, :]
        ks[0:P, :] = ks[T:T + P, :]
        vs[0:P, :] = vs[T:T + P, :]

    blk = (T, HEAD_DIM)
    cur = lambda n: jnp.minimum(n, nt - 1)
    prv = lambda n: jnp.maximum(n - 1, 0)
    return _launch(
        body, name=name, grid=(B_HEADS, nt + 1),
        in_specs=[pl.BlockSpec(blk, lambda h, n: (cur(n), g * B_HEADS + h)),
                  pl.BlockSpec(blk, lambda h, n: (cur(n), (ngr + g) * B_HEADS + h)),
                  pl.BlockSpec(blk, lambda h, n: (cur(n), (2 * ngr + g) * B_HEADS + h)),
                  pl.BlockSpec(blk, lambda h, n: (cur(n), h)),
                  pl.BlockSpec(blk, lambda h, n: (cur(n), h)),
                  pl.BlockSpec(blk, lambda h, n: (cur(n), h)),
                  pl.BlockSpec(memory_space=pl.ANY)],
        out_specs=[pl.BlockSpec(blk, lambda h, n: (cur(n), h)),
                   pl.BlockSpec(blk, lambda h, n: (prv(n), h)),
                   pl.BlockSpec(blk, lambda h, n: (prv(n), (2 * ngr + g) * B_HEADS + h))],
        out_shape=[jax.ShapeDtypeStruct((M, Wg), ACT), jax.ShapeDtypeStruct((M, Wg), ACT),
                   jax.ShapeDtypeStruct(dproj.shape, dproj.dtype)],
        scratch=[pltpu.VMEM((T, HEAD_DIM), F32)] * 4
        + [pltpu.VMEM((P + T, HEAD_DIM), F32)] * 2
        + [pltpu.VMEM((T, HEAD_DIM), F32)]
        + [pltpu.VMEM((2 * T, HEAD_DIM), F32)] * 2,
        aliases={6: 2},
        args=(qk, qk, proj, dov, lse, delta, dproj), sem=("parallel", "arbitrary"), comm=comm)


def _coords():
    return lax.axis_index("x"), lax.axis_index("y"), lax.axis_index("c")


def _gather_blocks(x_refs, out_refs, send_sems, recv_sems, local_sems):
    x, y, c = _coords()
    me, sibling = (x, y, c), (x, y, 1 - c)
    chips = [(1 - x, y), (x, 1 - y), (1 - x, 1 - y)]
    arrays = range(len(x_refs))

    def slot(a, px, py, pc):
        return out_refs[a].at[4 * px + 2 * py + pc]

    def copy(a, k, block, to, src=None):
        return _remote(slot(a, *block) if src is None else src, slot(a, *block), send_sems, recv_sems, 7 * a + k, to)

    mine = [pltpu.make_async_copy(x_refs[a], slot(a, *me), local_sems.at[a]) for a in arrays]
    first = [copy(a, 0, me, sibling, src=x_refs[a]) for a in arrays]
    first += [copy(a, 1 + j, me, (*chip, c), src=x_refs[a]) for j, chip in enumerate(chips) for a in arrays]
    for cp in mine + first:
        cp.start()
    passed = []
    for j, chip in enumerate(chips):
        for a in arrays:
            copy(a, 1 + j, (*chip, c), me).wait_recv()
            passed.append(copy(a, 4 + j, (*chip, c), sibling))
            passed[-1].start()
    for a in arrays:
        copy(a, 0, sibling, me).wait_recv()
        for j, chip in enumerate(chips):
            copy(a, 4 + j, (*chip, 1 - c), me).wait_recv()
    for cp in first + passed:
        cp.wait_send()
    for cp in mine:
        cp.wait()


def _all_gather_hbm(arrays, name):
    n = len(arrays)

    def body(*refs):
        _gather_blocks(refs[:n], refs[n:2 * n], *refs[2 * n:])

    return pl.pallas_call(
        body, name=name, in_specs=[_HBM] * n, out_specs=[_HBM] * n,
        out_shape=[jax.ShapeDtypeStruct((N_DEV,) + a.shape, a.dtype) for a in arrays],
        scratch_shapes=[pltpu.SemaphoreType.DMA((7 * n,)), pltpu.SemaphoreType.DMA((7 * n,)),
                        pltpu.SemaphoreType.DMA((n,))],
    )(*arrays)


def _all_reduce_small(part):
    R, C = part.shape

    def body(x_ref, tot_ref, gath, send_sems, recv_sems, local_sems):
        _gather_blocks([x_ref], [gath], send_sems, recv_sems, local_sems)
        acc = gath[0]
        for d in range(1, N_DEV):
            acc = acc + gath[d]
        tot_ref[...] = acc

    return pl.pallas_call(
        body, name="ar_small",
        in_specs=[pl.BlockSpec(memory_space=pltpu.VMEM)],
        out_specs=pl.BlockSpec(memory_space=pltpu.VMEM),
        out_shape=jax.ShapeDtypeStruct((R, C), F32),
        scratch_shapes=[pltpu.VMEM((N_DEV, R, C), F32),
                        pltpu.SemaphoreType.DMA((7,)), pltpu.SemaphoreType.DMA((7,)), pltpu.SemaphoreType.DMA((1,))],
        compiler_params=pltpu.CompilerParams(vmem_limit_bytes=VMEM_LIMIT),
    )(part)


def _remote(src, dst, send_sems, recv_sems, k, peer):
    return pltpu.make_async_remote_copy(src_ref=src, dst_ref=dst, send_sem=send_sems.at[k], recv_sem=recv_sems.at[k],
                                        device_id=peer, device_id_type=MESH)


def _ag_send(arrays):
    n = len(arrays)

    def make(c_in, c_out, send_sems, recv_sems, local_sems):
        x, y, c = _coords()
        peers = [(x, y, 1 - c), (1 - x, y, c), (x, 1 - y, c), (1 - x, 1 - y, c)]
        cps = []
        for a in range(n):
            src, dst = c_in[a], c_out[a].at[4 * x + 2 * y + c]
            cps.append(pltpu.make_async_copy(src, dst, local_sems.at[a]))
            cps += [_remote(src, dst, send_sems, recv_sems, 4 * a + k, peer) for k, peer in enumerate(peers)]
        return cps

    return _Comm(arrays, [jax.ShapeDtypeStruct((N_DEV,) + a.shape, a.dtype) for a in arrays], 4 * n, make, n_local=n)


def _ag_forward(gaths):
    n = len(gaths)

    def make(c_in, c_out, send_sems, recv_sems, local_sems):
        x, y, c = _coords()
        chips = [(1 - x, y), (x, 1 - y), (1 - x, 1 - y)]
        cps = []
        for a in range(n):
            buf = c_out[a]
            cps += [_remote(buf.at[4 * px + 2 * py + c], buf.at[4 * px + 2 * py + c], send_sems, recv_sems, 3 * a + j,
                            (x, y, 1 - c)) for j, (px, py) in enumerate(chips)]
        return cps

    return _Comm(gaths, [jax.ShapeDtypeStruct(g.shape, g.dtype) for g in gaths], 3 * n, make,
                 aliases={a: a for a in range(n)})


def _rs_sibling(grads):
    n = len(grads)

    def make(c_in, c_out, send_sems, recv_sems, local_sem):
        x, y, c = _coords()
        return [_remote(c_in[a].at[pl.ds(4 * (1 - c), 4)], c_out[a], send_sems, recv_sems, a, (x, y, 1 - c))
                for a in range(n)]

    return _Comm(grads, [jax.ShapeDtypeStruct((4,) + g.shape[1:], g.dtype) for g in grads], n, make)


def _rs_chips(parts):
    n = len(parts)

    def make(c_in, c_out, send_sems, recv_sems, local_sem):
        x, y, c = _coords()
        peers = [(x, 1 - y, c), (1 - x, y, c), (1 - x, 1 - y, c)]
        return [_remote(c_in[a].at[k], c_out[a].at[k], send_sems, recv_sems, 3 * a + k, peer)
                for a in range(n) for k, peer in enumerate(peers)]

    return _Comm(parts, [jax.ShapeDtypeStruct(p.shape, p.dtype) for p in parts], 3 * n, make)


def _row_tile(rows, cols):
    tr = min(rows, 1 << int(np.log2((1 << 18) // cols)))
    assert rows % tr == 0
    return tr


def _chip_partials(coords, g, r1, name):
    _, rows, C = g.shape
    tr = _row_tile(rows, C)

    def body(co_ref, g_ref, r_ref, o_ref):
        o_ref[...] = (g_ref[...] + r_ref[...]).astype(o_ref.dtype)

    def chip(k, co):
        return jnp.bitwise_xor(2 * co[0] + co[1], k + 1)

    return pl.pallas_call(
        body, name=name,
        grid_spec=pltpu.PrefetchScalarGridSpec(
            num_scalar_prefetch=1, grid=(3, rows // tr),
            in_specs=[pl.BlockSpec((None, tr, C), lambda k, t, co: (4 * co[2] + chip(k, co), t, 0)),
                      pl.BlockSpec((None, tr, C), lambda k, t, co: (chip(k, co), t, 0))],
            out_specs=pl.BlockSpec((None, tr, C), lambda k, t, co: (k, t, 0))),
        out_shape=jax.ShapeDtypeStruct((3, rows, C), WIRE),
        compiler_params=_cp("parallel", "parallel"),
    )(coords, g, r1)


def _adam_math(w, g, m, v):
    m = ADAM_B1 * m + (1.0 - ADAM_B1) * g
    v = ADAM_B2 * v + (1.0 - ADAM_B2) * (g * g)
    m_hat = m / (1.0 - ADAM_B1 ** ADAM_STEP)
    v_hat = v / (1.0 - ADAM_B2 ** ADAM_STEP)
    delta = -ADAM_LR * (m_hat / (jnp.sqrt(v_hat) + ADAM_EPS) + ADAM_WD * w)
    return delta, m, v


def _adamw_sharded(coords, w, m, v, g, r1, r2, name):
    rows, C = w.shape
    tr = _row_tile(rows, C)

    def body(co_ref, w_ref, m_ref, v_ref, g_ref, r1_ref, r2_ref, go_ref, d_ref, mo_ref, vo_ref):
        grad = g_ref[...] + r1_ref[...]
        for k in range(3):
            grad = grad + r2_ref[k].astype(F32)
        go_ref[...] = grad
        d_ref[...], mo_ref[...], vo_ref[...] = _adam_math(w_ref[...], grad, m_ref[...], v_ref[...])

    spec = pl.BlockSpec((tr, C), lambda t, co: (t, 0))
    return pl.pallas_call(
        body, name=name,
        grid_spec=pltpu.PrefetchScalarGridSpec(
            num_scalar_prefetch=1, grid=(rows // tr,),
            in_specs=[spec, spec, spec,
                      pl.BlockSpec((None, tr, C), lambda t, co: (4 * co[2] + 2 * co[0] + co[1], t, 0)),
                      pl.BlockSpec((None, tr, C), lambda t, co: (2 * co[0] + co[1], t, 0)),
                      pl.BlockSpec((3, tr, C), lambda t, co: (0, t, 0))],
            out_specs=[spec] * 4),
        out_shape=[jax.ShapeDtypeStruct((rows, C), F32)] * 4,
        compiler_params=_cp("parallel"),
    )(coords, w, m, v, g, r1, r2)


def _adamw_small(w, g, m, v, name):
    def body(w_ref, g_ref, m_ref, v_ref, d_ref, mo_ref, vo_ref):
        d_ref[...], mo_ref[...], vo_ref[...] = _adam_math(w_ref[...], g_ref[...], m_ref[...], v_ref[...])

    return pl.pallas_call(
        body, name=name, out_shape=[jax.ShapeDtypeStruct(w.shape, F32)] * 3,
        in_specs=[pl.BlockSpec(memory_space=pltpu.VMEM)] * 4,
        out_specs=[pl.BlockSpec(memory_space=pltpu.VMEM)] * 3,
    )(w, g, m, v)


def _reduce_scatter_adds(coords, grads, r1s, tag):
    return [_chip_partials(coords, g, r, f"rs_add_{tag}{i}") for i, (g, r) in enumerate(zip(grads, r1s))]


def kernel(x, norm_gain, a_w_in, a_v_gain, a_w_s, a_b_s, a_w_out, b_w_in, b_q_gain, b_k_gain, b_w_out, c_w_in, c_w_grp, c_scale, c_w_out, loss_target, m_norm_gain, m_a_w_in, m_a_v_gain, m_a_w_s, m_a_b_s, m_a_w_out, m_b_w_in, m_b_q_gain, m_b_k_gain, m_b_w_out, m_c_w_in, m_c_w_grp, m_c_scale, m_c_w_out, v_norm_gain, v_a_w_in, v_a_v_gain, v_a_w_s, v_a_b_s, v_a_w_out, v_b_w_in, v_b_q_gain, v_b_k_gain, v_b_w_out, v_c_w_in, v_c_w_grp, v_c_scale, v_c_w_out):
    cx, cy, cc = _coords()
    coords = jnp.stack([cx, cy, cc]).astype(jnp.int32)
    dev = 4 * cx + 2 * cy + cc
    Dm = x.shape[2]

    xs, tgt = x[0], loss_target[0]
    tables = _rope_tables(xs.shape[0])
    ng = lambda i: norm_gain[i:i + 1]
    ngr = len(B_DILATIONS)
    bst = [a_b_s[l].T for l in range(2)]
    b_gains = jnp.concatenate([b_q_gain[0], b_k_gain[0], jnp.zeros((2, HEAD_DIM), F32)], axis=0)
    nla, nlb, nlc = a_w_in.shape[2], b_w_in.shape[2], c_w_in.shape[2]
    ngp, rlc, cgc = c_w_grp.shape[1:]
    wire = lambda w: w.astype(WIRE)

    nvg, nsc = a_v_gain.size, c_scale.size
    vec = jnp.concatenate([a_v_gain.reshape(-1), c_scale.reshape(-1), jnp.zeros((1024 - nvg - nsc,), F32)]).reshape(8, 128)
    wa_in0, wa_out0, vecs = _all_gather_hbm([wire(a_w_in[0]), wire(a_w_out[0]), vec], "ag_layer0")
    wa_out0 = wa_out0.reshape(-1, Dm)
    vecs = vecs.reshape(N_DEV, -1)
    a_vg = vecs[:, :nvg].reshape((N_DEV,) + a_v_gain.shape).transpose(1, 0, 2).reshape(a_v_gain.shape[0], -1)
    c_sc = vecs[:, nvg:nvg + nsc].reshape(1, -1)

    h0, p0, *g1 = _norm_proj(xs, ng(0), wa_in0, "l0_proj", comm=_ag_send([wire(b_w_in[0]), wire(b_w_out[0])]))
    y0, wb_in, wb_out = _a_mid(p0, a_vg[0:1], a_w_s[0], bst[0], "l0_mid", comm=_ag_forward(g1))
    x1 = _out_proj(xs, y0, wa_out0, "l0_out")
    wb_out = wb_out.reshape(-1, Dm)

    later = [wire(c_w_in[0]), wire(c_w_grp[0]), wire(c_w_out[0]), wire(a_w_in[1]), wire(a_w_out[1])]
    h1, p1, *g2 = _norm_proj(x1, ng(1), wb_in, "l1_proj", comm=_ag_send(later))
    qk, wc_in, wc_grp, wc_out, wa_in1, wa_out1 = _b_qk_fwd(p1, tables, b_gains, "l1_qk", comm=_ag_forward(g2))
    ogs, lgs = zip(*[_b_attn_fwd(qk, p1, g, f"l1_attn{g}") for g in range(ngr)])
    y1, o1, lse = _b_combine(ogs, lgs, p1, "l1_comb")
    x2 = _out_proj(x1, y1, wb_out, "l1_out")
    wc_grp = wc_grp.transpose(1, 0, 2, 3).reshape(ngp, N_DEV * rlc, cgc)
    wc_out = wc_out.reshape(-1, Dm)
    wa_out1 = wa_out1.reshape(-1, Dm)

    h2, p2 = _norm_proj(x2, ng(2), wc_in, "l2_proj")
    y2 = _c_mid(p2, wc_grp, c_sc, "l2_mid")
    x3 = _out_proj(x2, y2, wc_out, "l2_out")
    h3, p3 = _norm_proj(x3, ng(3), wa_in1, "l3_proj")
    y3, = _a_mid(p3, a_vg[1:2], a_w_s[1], bst[1], "l3_mid")
    x4 = _out_proj(x3, y3, wa_out1, "l3_out")
    loss_local, dx4, dx4a = _loss_head(x4, tgt)
    loss = lax.psum(loss_local, ("x", "y", "c"))

    flat3 = lambda g: g.reshape(N_DEV, -1, g.shape[-1])
    dp3, dws1, dbs1, dvg1 = _a_bwd(dx4a, wa_out1, p3, a_vg[1:2], a_w_s[1], bst[1], "l3_bwd")
    grads3 = [_dw_in(h3, dp3, "l3_dwin"), _dw_out(y3, dx4a, "l3_dwout")]
    dx3, dx3a, dg3, *r1_3 = _dh_norm_bwd(dp3, wa_in1, x3, ng(3), dx4, "l3_dh", comm=_rs_sibling(grads3))
    parts3 = _reduce_scatter_adds(coords, grads3, r1_3, "l3_")

    dd, dz, gc_grp, dsc, *r2_3 = _c_bwd1(dx3a, wc_out, p2, wc_grp, c_sc, "l2_bwd1", comm=_rs_chips(parts3))
    dp2 = _c_bwd2(dd, dz, "l2_bwd2")
    grads2 = [_dw_in(h2, dp2, "l2_dwin"), _dw_out(y2, dx3a, "l2_dwout"), flat3(gc_grp)]
    dx2, dx2a, dg2, *r1_2 = _dh_norm_bwd(dp2, wc_in, x2, ng(2), dx3, "l2_dh", comm=_rs_sibling(grads2))
    parts2 = _reduce_scatter_adds(coords, grads2, r1_2, "l2_")

    dov, delta, dp1 = _b_bwd_pre(dx2a, wb_out, o1, p1, "l1_bwdpre")
    dqs, dks, r2_2 = [], [], None
    for g in range(ngr):
        dq, dk, dp1, *rest = _b_attn_bwd(qk, p1, dov, lse, delta, dp1, g, f"l1_attnbwd{g}",
                                         comm=_rs_chips(parts2) if g == 0 else None)
        if g == 0:
            r2_2 = rest
        dqs.append(dq)
        dks.append(dk)
    dp1, dgains = _b_qk_bwd(dqs, dks, p1, tables, b_gains, dp1, "l1_qkbwd")
    grads1 = [_dw_in(h1, dp1, "l1_dwin"), _dw_out(y1, dx2a, "l1_dwout")]
    dx1, dx1a, dg1, *r1_1 = _dh_norm_bwd(dp1, wb_in, x1, ng(1), dx2, "l1_dh", comm=_rs_sibling(grads1))
    parts1 = _reduce_scatter_adds(coords, grads1, r1_1, "l1_")

    dp0, dws0, dbs0, dvg0, *r2_1 = _a_bwd(dx1a, wa_out0, p0, a_vg[0:1], a_w_s[0], bst[0], "l0_bwd", comm=_rs_chips(parts1))
    grads0 = [_dw_in(h0, dp0, "l0_dwin"), _dw_out(y0, dx1a, "l0_dwout")]
    r1_0 = _run_comm(_rs_sibling(grads0), "l0_rs_sibling")
    parts0 = _reduce_scatter_adds(coords, grads0, r1_0, "l0_")
    dx0, _, dg0, *r2_0 = _dh_norm_bwd(dp0, wa_in0, xs, ng(0), dx1, "l0_dh", comm=_rs_chips(parts0))

    small = dict(norm=jnp.concatenate([dg0, dg1, dg2, dg3], axis=0), a_ws=jnp.stack([dws0, dws1]),
                 a_bs=jnp.stack([dbs0.T, dbs1.T]), b_gains=dgains, a_vg=jnp.concatenate([dvg0, dvg1], axis=0), c_sc=dsc)

    order = ["norm", "a_ws", "a_bs", "b_gains", "a_vg", "c_sc"]
    rows = [small[k].reshape(-1, 128) for k in order]
    roff = np.cumsum([0] + [r.shape[0] for r in rows])
    tot = _all_reduce_small(jnp.concatenate(rows, axis=0))
    sm = {k: tot[int(roff[i]):int(roff[i + 1])].reshape(small[k].shape) for i, k in enumerate(order)}
    vl = a_v_gain.shape[1]
    g_small = dict(
        norm_gain=sm["norm"], a_w_s=sm["a_ws"], a_b_s=sm["a_bs"],
        b_q_gain=sm["b_gains"][None, 0:3], b_k_gain=sm["b_gains"][None, 3:6],
        a_v_gain=lax.dynamic_slice_in_dim(sm["a_vg"], dev * vl, vl, axis=1),
        c_scale=lax.dynamic_slice_in_dim(sm["c_sc"], dev * vl, vl, axis=1),
    )

    shares = dict(
        a_w_in=[(grads0[0], r1_0[0], r2_0[0]), (grads3[0], r1_3[0], r2_3[0])],
        a_w_out=[(grads0[1], r1_0[1], r2_0[1]), (grads3[1], r1_3[1], r2_3[1])],
        b_w_in=[(grads1[0], r1_1[0], r2_1[0])], b_w_out=[(grads1[1], r1_1[1], r2_1[1])],
        c_w_in=[(grads2[0], r1_2[0], r2_2[0])], c_w_out=[(grads2[1], r1_2[1], r2_2[1])],
        c_w_grp=[(grads2[2], r1_2[2], r2_2[2])])

    params = dict(a_w_in=a_w_in, a_w_out=a_w_out, b_w_in=b_w_in, b_w_out=b_w_out, c_w_in=c_w_in, c_w_grp=c_w_grp, c_w_out=c_w_out,
                  norm_gain=norm_gain, a_v_gain=a_v_gain, a_w_s=a_w_s, a_b_s=a_b_s, b_q_gain=b_q_gain, b_k_gain=b_k_gain, c_scale=c_scale)
    moms = dict(a_w_in=(m_a_w_in, v_a_w_in), a_w_out=(m_a_w_out, v_a_w_out), b_w_in=(m_b_w_in, v_b_w_in), b_w_out=(m_b_w_out, v_b_w_out),
                c_w_in=(m_c_w_in, v_c_w_in), c_w_grp=(m_c_w_grp, v_c_w_grp), c_w_out=(m_c_w_out, v_c_w_out),
                norm_gain=(m_norm_gain, v_norm_gain), a_v_gain=(m_a_v_gain, v_a_v_gain), a_w_s=(m_a_w_s, v_a_w_s),
                a_b_s=(m_a_b_s, v_a_b_s), b_q_gain=(m_b_q_gain, v_b_q_gain), b_k_gain=(m_b_k_gain, v_b_k_gain),
                c_scale=(m_c_scale, v_c_scale))
    grad, delta, new_m, new_v = {}, {}, {}, {}
    for pname, layers in shares.items():
        w, (m, v) = params[pname], moms[pname]
        C = w.shape[-1]
        per_layer = [_adamw_sharded(coords, w[l].reshape(-1, C), m[l].reshape(-1, C), v[l].reshape(-1, C), g, r1, r2,
                                    f"adamw_{pname}{l}") for l, (g, r1, r2) in enumerate(layers)]
        grad[pname], delta[pname], new_m[pname], new_v[pname] = [
            jnp.stack([o.reshape(w.shape[1:]) for o in outs]) for outs in zip(*per_layer)]
    for pname, g in g_small.items():
        w = params[pname]
        C = w.shape[-1]
        outs = _adamw_small(w.reshape(-1, C), g.reshape(-1, C), moms[pname][0].reshape(-1, C), moms[pname][1].reshape(-1, C),
                            f"adamw_{pname}")
        grad[pname] = g.reshape(w.shape)
        delta[pname], new_m[pname], new_v[pname] = [o.reshape(w.shape) for o in outs]

    wnames = ["norm_gain", "a_w_in", "a_v_gain", "a_w_s", "a_b_s", "a_w_out", "b_w_in", "b_q_gain", "b_k_gain", "b_w_out",
              "c_w_in", "c_w_grp", "c_scale", "c_w_out"]
    return (loss, dx0[None], *[grad[n] for n in wnames], *[delta[n] for n in wnames],
            *[new_m[n] for n in wnames], *[new_v[n] for n in wnames])
```

```python
import functools

import numpy as np
import jax
import jax.numpy as jnp
from jax import lax
from jax.experimental import pallas as pl
from jax.experimental.pallas import tpu as pltpu

F32 = jnp.float32
MXU = jnp.bfloat16
ACT = jnp.bfloat16
WIRE = jnp.bfloat16

EPS = 1e-6
CHUNK = 128
A_GROUPS = 8
HEAD_DIM = 128
B_HEADS = 8
B_DILATIONS = (1, 4, 16)
ROPE_DIM = 32
ROPE_THETA = 500000.0
POOL_SIZES = (2, 4, 8, 16)
POOL_HALO = 16
N_DEV = 8
NEG = -1e30

ADAM_LR, ADAM_B1, ADAM_B2, ADAM_EPS, ADAM_WD, ADAM_STEP = 0.001, 0.9, 0.999, 1e-08, 0.01, 10

VMEM_LIMIT = 56 * 1024 * 1024
MESH = pl.DeviceIdType.MESH


def _cp(*sem):
    return pltpu.CompilerParams(dimension_semantics=sem, vmem_limit_bytes=VMEM_LIMIT)


def _sigmoid(z):
    return 1.0 / (1.0 + jnp.exp(-z))


def _dot(a, b):
    return jnp.dot(a.astype(MXU), b.astype(MXU), preferred_element_type=F32)


def _dot_nt(a, b):
    return lax.dot_general(a.astype(MXU), b.astype(MXU), (((1,), (1,)), ((), ())), preferred_element_type=F32)


def _dot_tn(a, b):
    return lax.dot_general(a.astype(MXU), b.astype(MXU), (((0,), (0,)), ((), ())), preferred_element_type=F32)


def _chunk_slot(d):
    return (d % 2) * 4 + d // 2


class _Comm:
    def __init__(self, inputs, out_shapes, n_remote, make, aliases=None, n_local=1):
        self.inputs = list(inputs)
        self.out_shapes = list(out_shapes)
        self.n_remote = n_remote
        self.n_local = n_local
        self.make = make
        self.aliases = dict(aliases or {})

    def sems(self):
        return [pltpu.SemaphoreType.DMA((self.n_remote,)), pltpu.SemaphoreType.DMA((self.n_remote,)),
                pltpu.SemaphoreType.DMA((self.n_local,))]


_HBM = pl.BlockSpec(memory_space=pl.ANY)


def _launch(body, *, name, grid, in_specs, out_specs, out_shape, args, sem, scratch=(), aliases=None, comm=None):
    in_specs, out_specs, out_shape, scratch = list(in_specs), list(out_specs), list(out_shape), list(scratch)
    aliases = dict(aliases or {})
    if comm is None:
        return pl.pallas_call(body, name=name, grid=grid, in_specs=in_specs, out_specs=out_specs, out_shape=out_shape,
                              scratch_shapes=scratch, input_output_aliases=aliases, compiler_params=_cp(*sem))(*args)
    n_in, n_out, n_sc = len(in_specs), len(out_specs), len(scratch)
    nci, nco = len(comm.inputs), len(comm.out_shapes)

    def hosted(*refs):
        b_in, c_in = refs[:n_in], refs[n_in:n_in + nci]
        o0 = n_in + nci
        b_out, c_out = refs[o0:o0 + n_out], refs[o0 + n_out:o0 + n_out + nco]
        s0 = o0 + n_out + nco
        b_sc, sems = refs[s0:s0 + n_sc], refs[s0 + n_sc:]
        ids = [pl.program_id(a) for a in range(len(grid))]
        first = functools.reduce(jnp.logical_and, [i == 0 for i in ids])
        last = functools.reduce(jnp.logical_and, [i == g - 1 for i, g in zip(ids, grid)])

        @pl.when(first)
        def _():
            for cp in comm.make(c_in, c_out, *sems):
                cp.start()

        body(*b_in, *b_out, *b_sc)

        @pl.when(last)
        def _():
            for cp in comm.make(c_in, c_out, *sems):
                cp.wait()

    for ci, co in comm.aliases.items():
        aliases[n_in + ci] = n_out + co
    return pl.pallas_call(
        hosted, name=name, grid=grid, in_specs=in_specs + [_HBM] * nci, out_specs=out_specs + [_HBM] * nco,
        out_shape=out_shape + comm.out_shapes, scratch_shapes=scratch + comm.sems(),
        input_output_aliases=aliases, compiler_params=_cp(*["arbitrary"] * len(grid)))(*args, *comm.inputs)


def _run_comm(comm, name):
    nci, nco = len(comm.inputs), len(comm.out_shapes)

    def body(*refs):
        cps = comm.make(refs[:nci], refs[nci:nci + nco], *refs[nci + nco:])
        for cp in cps:
            cp.start()
        for cp in cps:
            cp.wait()

    return pl.pallas_call(
        body, name=name, in_specs=[_HBM] * nci, out_specs=[_HBM] * nco, out_shape=comm.out_shapes,
        scratch_shapes=comm.sems(), input_output_aliases=dict(comm.aliases))(*comm.inputs)


def _norm_proj(x, gain, w_dm, name, comm=None):
    M, Dm = x.shape
    nd, _, nl = w_dm.shape
    tm = min(M, 1024)

    def body(x_ref, g_ref, w_ref, h_ref, p_ref):
        @pl.when(pl.program_id(1) == 0)
        def _():
            xv = x_ref[...]
            r = lax.rsqrt(jnp.mean(xv * xv, axis=-1, keepdims=True) + EPS)
            h_ref[...] = (xv * r * g_ref[...]).astype(h_ref.dtype)

        p_ref[...] = _dot(h_ref[...], w_ref[...]).astype(p_ref.dtype)

    return _launch(
        body, name=name, grid=(M // tm, nd),
        in_specs=[pl.BlockSpec((tm, Dm), lambda i, j: (i, 0)),
                  pl.BlockSpec((1, Dm), lambda i, j: (0, 0)),
                  pl.BlockSpec((None, Dm, nl), lambda i, j: (j, 0, 0))],
        out_specs=[pl.BlockSpec((tm, Dm), lambda i, j: (i, 0)),
                   pl.BlockSpec((tm, nl), lambda i, j: (i, j))],
        out_shape=[jax.ShapeDtypeStruct((M, Dm), ACT), jax.ShapeDtypeStruct((M, nd * nl), ACT)],
        args=(x, gain, w_dm), sem=("parallel", "arbitrary"), comm=comm)


def _out_proj(x, y, w, name):
    M, Dm = x.shape
    K = y.shape[1]
    tm = min(M, 512)

    def body(x_ref, y_ref, w_ref, o_ref):
        o_ref[...] = x_ref[...] + _dot(y_ref[...], w_ref[...])

    return pl.pallas_call(
        body, name=name, grid=(M // tm,),
        in_specs=[pl.BlockSpec((tm, Dm), lambda i: (i, 0)),
                  pl.BlockSpec((tm, K), lambda i: (i, 0)),
                  pl.BlockSpec((K, Dm), lambda i: (0, 0))],
        out_specs=pl.BlockSpec((tm, Dm), lambda i: (i, 0)),
        out_shape=jax.ShapeDtypeStruct((M, Dm), F32),
        compiler_params=_cp("parallel"),
    )(x, y, w)


def _loss_head(xf, target):
    M, Dm = xf.shape
    tm = min(M, 512)

    def body(x_ref, t_ref, dx_ref, dxa_ref, l_ref):
        @pl.when(pl.program_id(0) == 0)
        def _():
            l_ref[...] = jnp.zeros_like(l_ref)

        err = x_ref[...] - t_ref[...]
        dx = err * (1.0 / Dm)
        dx_ref[...] = dx
        dxa_ref[...] = dx.astype(dxa_ref.dtype)
        l_ref[...] += jnp.sum(err * err) * (0.5 / Dm)

    spec = pl.BlockSpec((tm, Dm), lambda i: (i, 0))
    dx, dxa, l = pl.pallas_call(
        body, name="loss_head", grid=(M // tm,),
        in_specs=[spec] * 2,
        out_specs=[spec, spec, pl.BlockSpec((8, 128), lambda i: (0, 0))],
        out_shape=[jax.ShapeDtypeStruct((M, Dm), F32), jax.ShapeDtypeStruct((M, Dm), ACT),
                   jax.ShapeDtypeStruct((8, 128), F32)],
        compiler_params=_cp("arbitrary"),
    )(xf, target)
    return l[0, 0], dx, dxa


def _dw_in(h, dproj, name):
    M, Dm = h.shape
    nl = dproj.shape[1] // N_DEV
    tt = min(M, 1024)

    def body(a_ref, b_ref, o_ref):
        @pl.when(pl.program_id(1) == 0)
        def _():
            o_ref[...] = jnp.zeros_like(o_ref)

        o_ref[...] += _dot_tn(a_ref[...], b_ref[...])

    return pl.pallas_call(
        body, name=name, grid=(N_DEV, M // tt),
        in_specs=[pl.BlockSpec((tt, Dm), lambda j, t: (t, 0)), pl.BlockSpec((tt, nl), lambda j, t: (t, j))],
        out_specs=pl.BlockSpec((None, Dm, nl), lambda j, t: (_chunk_slot(j), 0, 0)),
        out_shape=jax.ShapeDtypeStruct((N_DEV, Dm, nl), F32),
        compiler_params=_cp("parallel", "arbitrary"),
    )(h, dproj)


def _dw_out(y, dout, name):
    M, K = y.shape
    Dm = dout.shape[1]
    kl = K // N_DEV
    tt = min(M, 512)

    def body(a_ref, b_ref, o_ref):
        @pl.when(pl.program_id(0) == 0)
        def _():
            o_ref[...] = jnp.zeros_like(o_ref)

        b = b_ref[...]
        for j in range(N_DEV):
            o_ref[_chunk_slot(j)] += _dot_tn(a_ref[:, j * kl:(j + 1) * kl], b)

    return pl.pallas_call(
        body, name=name, grid=(M // tt,),
        in_specs=[pl.BlockSpec((tt, K), lambda t: (t, 0)), pl.BlockSpec((tt, Dm), lambda t: (t, 0))],
        out_specs=pl.BlockSpec((N_DEV, kl, Dm), lambda t: (0, 0, 0)),
        out_shape=jax.ShapeDtypeStruct((N_DEV, kl, Dm), F32),
        compiler_params=_cp("arbitrary"),
    )(y, dout)


def _dh_norm_bwd(dproj, w_dm, x, gain, dres, name, comm=None):
    M, Dm = x.shape
    nd, _, nl = w_dm.shape
    tm = min(M, 1024)

    def body(dp_ref, w_ref, x_ref, g_ref, dr_ref, dx_ref, dxa_ref, dg_ref, acc_ref):
        i, j = pl.program_id(0), pl.program_id(1)

        @pl.when(j == 0)
        def _():
            acc_ref[...] = jnp.zeros_like(acc_ref)

        acc_ref[...] += _dot_nt(dp_ref[...], w_ref[...])

        @pl.when(j == nd - 1)
        def _():
            @pl.when(i == 0)
            def _():
                dg_ref[...] = jnp.zeros_like(dg_ref)

            dh = acc_ref[...]
            xv = x_ref[...]
            r = lax.rsqrt(jnp.mean(xv * xv, axis=-1, keepdims=True) + EPS)
            xn = xv * r
            dg_ref[...] += jnp.sum(dh * xn, axis=0, keepdims=True)
            dxn = dh * g_ref[...]
            dx = dr_ref[...] + r * (dxn - xn * jnp.mean(dxn * xn, axis=-1, keepdims=True))
            dx_ref[...] = dx
            dxa_ref[...] = dx.astype(dxa_ref.dtype)

    row = pl.BlockSpec((tm, Dm), lambda i, j: (i, 0))
    return _launch(
        body, name=name, grid=(M // tm, nd),
        in_specs=[pl.BlockSpec((tm, nl), lambda i, j: (i, j)),
                  pl.BlockSpec((None, Dm, nl), lambda i, j: (j, 0, 0)),
                  row, pl.BlockSpec((1, Dm), lambda i, j: (0, 0)), row],
        out_specs=[row, row, pl.BlockSpec((1, Dm), lambda i, j: (0, 0))],
        out_shape=[jax.ShapeDtypeStruct((M, Dm), F32), jax.ShapeDtypeStruct((M, Dm), ACT),
                   jax.ShapeDtypeStruct((1, Dm), F32)],
        scratch=[pltpu.VMEM((tm, Dm), F32)],
        args=(dproj, w_dm, x, gain, dres), sem=("arbitrary", "arbitrary"), comm=comm)


def _tril_mask():
    return lax.broadcasted_iota(jnp.int32, (CHUNK, CHUNK), 0) >= lax.broadcasted_iota(jnp.int32, (CHUNK, CHUNK), 1)


def _a_mid(proj, v_gain, w_s, b_st, name, comm=None):
    M = proj.shape[0]
    W = proj.shape[1] // 3
    gd = W // A_GROUPS
    tm = min(M, 256)

    def body(p_ref, vg_ref, ws_ref, bs_ref, y_ref):
        pv = p_ref[:, W:2 * W].astype(F32)
        r = lax.rsqrt(jnp.mean(pv * pv, axis=-1, keepdims=True) + EPS)
        v = (pv * r * vg_ref[...]).astype(MXU)
        tri = _tril_mask()
        for g in range(A_GROUPS):
            wg = jnp.where(tri, ws_ref[g], 0.0).astype(MXU)
            bcol = bs_ref[:, g:g + 1]
            for c in range(tm // CHUNK):
                rows, cols = slice(c * CHUNK, (c + 1) * CHUNK), slice(g * gd, (g + 1) * gd)
                mixed = jnp.dot(wg, v[rows, cols], preferred_element_type=F32) + bcol
                u = p_ref[rows, g * gd:(g + 1) * gd].astype(F32)
                z = p_ref[rows, 2 * W + g * gd:2 * W + (g + 1) * gd].astype(F32)
                y_ref[rows, cols] = (u * mixed * (z * _sigmoid(z))).astype(y_ref.dtype)

    return _launch(
        body, name=name, grid=(M // tm,),
        in_specs=[pl.BlockSpec((tm, 3 * W), lambda i: (i, 0)),
                  pl.BlockSpec((1, W), lambda i: (0, 0)),
                  pl.BlockSpec((A_GROUPS, CHUNK, CHUNK), lambda i: (0, 0, 0)),
                  pl.BlockSpec((CHUNK, A_GROUPS), lambda i: (0, 0))],
        out_specs=[pl.BlockSpec((tm, W), lambda i: (i, 0))],
        out_shape=[jax.ShapeDtypeStruct((M, W), ACT)],
        args=(proj, v_gain, w_s, b_st), sem=("parallel",), comm=comm)


def _a_bwd(dout, w_out, proj, v_gain, w_s, b_st, name, comm=None):
    M = proj.shape[0]
    W = proj.shape[1] // 3
    Dm = dout.shape[1]
    gd = W // A_GROUPS
    tm = min(M, 256)
    nt = M // tm

    def body(do_ref, wo_ref, p_ref, vg_ref, ws_ref, bs_ref, dp_ref, dws_ref, dbs_ref, dvg_ref, dv_s):
        i = pl.program_id(0)

        @pl.when(i == 0)
        def _():
            dws_ref[...] = jnp.zeros_like(dws_ref)
            dbs_ref[...] = jnp.zeros_like(dbs_ref)
            dvg_ref[...] = jnp.zeros_like(dvg_ref)

        dy = _dot_nt(do_ref[...], wo_ref[...])
        pv = p_ref[:, W:2 * W].astype(F32)
        r = lax.rsqrt(jnp.mean(pv * pv, axis=-1, keepdims=True) + EPS)
        pvn = pv * r
        vg = vg_ref[...]
        v = (pvn * vg).astype(MXU)
        tri = _tril_mask()
        for g in range(A_GROUPS):
            wf = jnp.where(tri, ws_ref[g], 0.0)
            wg = wf.astype(MXU)
            wgt = wf.T.astype(MXU)
            bcol = bs_ref[:, g:g + 1]
            for c in range(tm // CHUNK):
                rows, cols = slice(c * CHUNK, (c + 1) * CHUNK), slice(g * gd, (g + 1) * gd)
                vb = v[rows, cols]
                mixed = jnp.dot(wg, vb, preferred_element_type=F32) + bcol
                u = p_ref[rows, g * gd:(g + 1) * gd].astype(F32)
                z = p_ref[rows, 2 * W + g * gd:2 * W + (g + 1) * gd].astype(F32)
                sig = _sigmoid(z)
                sz = z * sig
                dyb = dy[rows, cols]
                dp_ref[rows, g * gd:(g + 1) * gd] = (dyb * mixed * sz).astype(dp_ref.dtype)
                dp_ref[rows, 2 * W + g * gd:2 * W + (g + 1) * gd] = (
                    dyb * u * mixed * (sig * (1.0 + z * (1.0 - sig)))).astype(dp_ref.dtype)
                dmix = dyb * u * sz
                dws_ref[g] += _dot_nt(dmix, vb)
                dbs_ref[:, g:g + 1] += jnp.sum(dmix, axis=1, keepdims=True)
                dv_s[rows, cols] = jnp.dot(wgt, dmix.astype(MXU), preferred_element_type=F32)
        dv = dv_s[...]
        dvg_ref[...] += jnp.sum(dv * pvn, axis=0, keepdims=True)
        dpvn = dv * vg
        dp_ref[:, W:2 * W] = (r * (dpvn - pvn * jnp.mean(dpvn * pvn, axis=-1, keepdims=True))).astype(dp_ref.dtype)

        @pl.when(i == nt - 1)
        def _():
            for g in range(A_GROUPS):
                dws_ref[g] = jnp.where(tri, dws_ref[g], 0.0)

    return _launch(
        body, name=name, grid=(nt,),
        in_specs=[pl.BlockSpec((tm, Dm), lambda i: (i, 0)),
                  pl.BlockSpec((W, Dm), lambda i: (0, 0)),
                  pl.BlockSpec((tm, 3 * W), lambda i: (i, 0)),
                  pl.BlockSpec((1, W), lambda i: (0, 0)),
                  pl.BlockSpec((A_GROUPS, CHUNK, CHUNK), lambda i: (0, 0, 0)),
                  pl.BlockSpec((CHUNK, A_GROUPS), lambda i: (0, 0))],
        out_specs=[pl.BlockSpec((tm, 3 * W), lambda i: (i, 0)),
                   pl.BlockSpec((A_GROUPS, CHUNK, CHUNK), lambda i: (0, 0, 0)),
                   pl.BlockSpec((CHUNK, A_GROUPS), lambda i: (0, 0)),
                   pl.BlockSpec((1, W), lambda i: (0, 0))],
        out_shape=[jax.ShapeDtypeStruct((M, 3 * W), ACT),
                   jax.ShapeDtypeStruct((A_GROUPS, CHUNK, CHUNK), F32),
                   jax.ShapeDtypeStruct((CHUNK, A_GROUPS), F32),
                   jax.ShapeDtypeStruct((1, W), F32)],
        scratch=[pltpu.VMEM((tm, W), F32)],
        args=(dout, w_out, proj, v_gain, w_s, b_st), sem=("arbitrary",), comm=comm)


def _pool_diff(xg, tail, i, tm, w):
    t = lax.broadcasted_iota(jnp.int32, (tm, tm + POOL_HALO), 0)
    s = lax.broadcasted_iota(jnp.int32, (tm, tm + POOL_HALO), 1)
    off = t - (s - POOL_HALO)
    band = jnp.where((off >= 0) & (off < w), 1.0, 0.0).astype(MXU)
    tail = jnp.where(i > 0, tail, jnp.zeros_like(tail))
    ext = jnp.concatenate([tail, xg], axis=0)
    ssum = jnp.dot(band, ext.astype(MXU), preferred_element_type=F32)
    tglob = i * tm + lax.broadcasted_iota(jnp.int32, (tm, 1), 0)
    cnt = jnp.minimum(tglob + 1, w).astype(F32)
    return ssum / cnt - xg.astype(F32)


def _c_mid(proj, w_grp, scale, name):
    M = proj.shape[0]
    W = proj.shape[1] // 2
    ng = len(POOL_SIZES)
    cg = W // ng
    tm = min(M, 256)
    hb = tm // POOL_HALO

    def body(xc_ref, tail_ref, z_ref, wg_ref, sc_ref, y_ref):
        i = pl.program_id(0)
        for g, w in enumerate(POOL_SIZES):
            cols = slice(g * cg, (g + 1) * cg)
            d = _pool_diff(xc_ref[:, cols], tail_ref[:, cols], i, tm, w)
            mixed = _dot(d, wg_ref[g]) * sc_ref[:, cols]
            z = z_ref[:, cols].astype(F32)
            y_ref[:, cols] = (mixed * (z * _sigmoid(z))).astype(y_ref.dtype)

    return pl.pallas_call(
        body, name=name, grid=(M // tm,),
        in_specs=[pl.BlockSpec((tm, W), lambda i: (i, 0)),
                  pl.BlockSpec((POOL_HALO, W), lambda i: (jnp.maximum(i * hb - 1, 0), 0)),
                  pl.BlockSpec((tm, W), lambda i: (i, 1)),
                  pl.BlockSpec((ng, cg, cg), lambda i: (0, 0, 0)),
                  pl.BlockSpec((1, W), lambda i: (0, 0))],
        out_specs=pl.BlockSpec((tm, W), lambda i: (i, 0)),
        out_shape=jax.ShapeDtypeStruct((M, W), ACT),
        compiler_params=_cp("parallel"),
    )(proj, proj, proj, w_grp, scale)


def _c_bwd1(dout, w_out, proj, w_grp, scale, name, comm=None):
    M = proj.shape[0]
    W = proj.shape[1] // 2
    Dm = dout.shape[1]
    ng = len(POOL_SIZES)
    cg = W // ng
    rl = cg // N_DEV
    tm = min(M, 256)
    hb = tm // POOL_HALO
    nt = M // tm

    def body(do_ref, wo_ref, xc_ref, tail_ref, z_ref, wg_ref, sc_ref, dd_ref, dz_ref, dwg_ref, dsc_ref, acc_ref):
        i = pl.program_id(0)

        @pl.when(i == 0)
        def _():
            acc_ref[...] = jnp.zeros_like(acc_ref)
            dsc_ref[...] = jnp.zeros_like(dsc_ref)

        dy = _dot_nt(do_ref[...], wo_ref[...])
        for g, w in enumerate(POOL_SIZES):
            cols = slice(g * cg, (g + 1) * cg)
            d = _pool_diff(xc_ref[:, cols], tail_ref[:, cols], i, tm, w)
            mr = _dot(d, wg_ref[g])
            sc = sc_ref[:, cols]
            z = z_ref[:, cols].astype(F32)
            sig = _sigmoid(z)
            dyg = dy[:, cols]
            dmixed = dyg * (z * sig)
            dz_ref[:, cols] = (dyg * (mr * sc) * (sig * (1.0 + z * (1.0 - sig)))).astype(dz_ref.dtype)
            dsc_ref[:, cols] += jnp.sum(dmixed * mr, axis=0, keepdims=True)
            dmr = (dmixed * sc).astype(MXU)
            acc_ref[g] += _dot_tn(d, dmr)
            dd_ref[:, cols] = _dot_nt(dmr, wg_ref[g]).astype(dd_ref.dtype)

        @pl.when(i == nt - 1)
        def _():
            for dev in range(N_DEV):
                for g in range(ng):
                    dwg_ref[_chunk_slot(dev), g] = acc_ref[g, dev * rl:(dev + 1) * rl, :]

    return _launch(
        body, name=name, grid=(nt,),
        in_specs=[pl.BlockSpec((tm, Dm), lambda i: (i, 0)),
                  pl.BlockSpec((W, Dm), lambda i: (0, 0)),
                  pl.BlockSpec((tm, W), lambda i: (i, 0)),
                  pl.BlockSpec((POOL_HALO, W), lambda i: (jnp.maximum(i * hb - 1, 0), 0)),
                  pl.BlockSpec((tm, W), lambda i: (i, 1)),
                  pl.BlockSpec((ng, cg, cg), lambda i: (0, 0, 0)),
                  pl.BlockSpec((1, W), lambda i: (0, 0))],
        out_specs=[pl.BlockSpec((tm, W), lambda i: (i, 0)),
                   pl.BlockSpec((tm, W), lambda i: (i, 0)),
                   pl.BlockSpec((N_DEV, ng, rl, cg), lambda i: (0, 0, 0, 0)),
                   pl.BlockSpec((1, W), lambda i: (0, 0))],
        out_shape=[jax.ShapeDtypeStruct((M, W), ACT), jax.ShapeDtypeStruct((M, W), ACT),
                   jax.ShapeDtypeStruct((N_DEV, ng, rl, cg), F32), jax.ShapeDtypeStruct((1, W), F32)],
        scratch=[pltpu.VMEM((ng, cg, cg), F32)],
        args=(dout, w_out, proj, proj, proj, w_grp, scale), sem=("arbitrary",), comm=comm)


def _c_bwd2(dd, dz, name):
    M, W = dd.shape
    ng = len(POOL_SIZES)
    cg = W // ng
    tm = min(M, 256)
    hb = tm // POOL_HALO
    nt = M // tm

    def body(dd_ref, head_ref, dz_ref, dp_ref):
        i = pl.program_id(0)
        s = lax.broadcasted_iota(jnp.int32, (tm, tm + POOL_HALO), 0)
        t = lax.broadcasted_iota(jnp.int32, (tm, tm + POOL_HALO), 1)
        off = t - s
        tglob = i * tm + lax.broadcasted_iota(jnp.int32, (tm + POOL_HALO, 1), 0)
        for g, w in enumerate(POOL_SIZES):
            cols = slice(g * cg, (g + 1) * cg)
            ddg = dd_ref[:, cols].astype(F32)
            head = head_ref[:, cols].astype(F32)
            head = jnp.where(i < nt - 1, head, jnp.zeros_like(head))
            cnt = jnp.minimum(tglob + 1, w).astype(F32)
            ext = (jnp.concatenate([ddg, head], axis=0) / cnt).astype(MXU)
            band = jnp.where((off >= 0) & (off < w), 1.0, 0.0).astype(MXU)
            dp_ref[:, cols] = (jnp.dot(band, ext, preferred_element_type=F32) - ddg).astype(dp_ref.dtype)
        dp_ref[:, W:] = dz_ref[...]

    return pl.pallas_call(
        body, name=name, grid=(nt,),
        in_specs=[pl.BlockSpec((tm, W), lambda i: (i, 0)),
                  pl.BlockSpec((POOL_HALO, W), lambda i: (jnp.minimum((i + 1) * hb, M // POOL_HALO - 1), 0)),
                  pl.BlockSpec((tm, W), lambda i: (i, 0))],
        out_specs=pl.BlockSpec((tm, 2 * W), lambda i: (i, 0)),
        out_shape=jax.ShapeDtypeStruct((M, 2 * W), ACT),
        compiler_params=_cp("parallel"),
    )(dd, dd, dz)


def _rope_tables(S):
    half = ROPE_DIM // 2
    inv_freq = jnp.power(jnp.float32(ROPE_THETA), -jnp.arange(half, dtype=F32) / half)
    ang = jnp.arange(S, dtype=F32)[:, None] * inv_freq[None, :]
    cos, sin = jnp.cos(ang), jnp.sin(ang)
    rest = HEAD_DIM - ROPE_DIM
    cf = jnp.concatenate([cos, cos, jnp.ones((S, rest), F32)], axis=1)
    sf = jnp.concatenate([-sin, sin, jnp.zeros((S, rest), F32)], axis=1)
    return cf, sf


def _swap_matrix():
    half = ROPE_DIM // 2
    a = lax.broadcasted_iota(jnp.int32, (HEAD_DIM, HEAD_DIM), 0)
    e = lax.broadcasted_iota(jnp.int32, (HEAD_DIM, HEAD_DIM), 1)
    hit = ((e < half) & (a == e + half)) | ((e >= half) & (e < 2 * half) & (a == e - half))
    return jnp.where(hit, 1.0, 0.0).astype(MXU)


def _b_qk_fwd(proj, tables, gains, name, comm=None):
    M = proj.shape[0]
    nsl = 2 * len(B_DILATIONS) * B_HEADS
    Wqk = nsl * HEAD_DIM
    tm = min(M, 256)

    def body(p_ref, cf_ref, sf_ref, g_ref, o_ref):
        cf, sf = cf_ref[...], sf_ref[...]
        swap = _swap_matrix()
        for j in range(nsl):
            cols = slice(j * HEAD_DIM, (j + 1) * HEAD_DIM)
            xv = p_ref[:, cols].astype(F32)
            r = lax.rsqrt(jnp.mean(xv * xv, axis=-1, keepdims=True) + EPS)
            xg = xv * g_ref[j // B_HEADS:j // B_HEADS + 1, :]
            hi = xg.astype(MXU)
            lo = (xg - hi.astype(F32)).astype(MXU)
            sw = jnp.dot(hi, swap, preferred_element_type=F32) + jnp.dot(lo, swap, preferred_element_type=F32)
            o_ref[:, cols] = (r * (xg * cf + sw * sf)).astype(o_ref.dtype)

    tspec = pl.BlockSpec((tm, HEAD_DIM), lambda i: (i, 0))
    return _launch(
        body, name=name, grid=(M // tm,),
        in_specs=[pl.BlockSpec((tm, Wqk), lambda i: (i, 0)), tspec, tspec,
                  pl.BlockSpec((8, HEAD_DIM), lambda i: (0, 0))],
        out_specs=[pl.BlockSpec((tm, Wqk), lambda i: (i, 0))],
        out_shape=[jax.ShapeDtypeStruct((M, Wqk), ACT)],
        args=(proj, *tables, gains), sem=("parallel",), comm=comm)


def _b_qk_bwd(dqs, dks, proj, tables, gains, dproj, name):
    M = proj.shape[0]
    ngr = len(B_DILATIONS)
    nsl = 2 * ngr * B_HEADS
    Wqk = nsl * HEAD_DIM
    Wg = B_HEADS * HEAD_DIM
    tm = min(M, 256)

    def body(*refs):
        d_refs = refs[:2 * ngr]
        p_ref, cf_ref, sf_ref, g_ref = refs[2 * ngr:2 * ngr + 4]
        dp_ref, dg_ref = refs[-2], refs[-1]

        @pl.when(pl.program_id(0) == 0)
        def _():
            dg_ref[...] = jnp.zeros_like(dg_ref)

        cf, sf = cf_ref[...], sf_ref[...]
        swap = _swap_matrix()
        for j in range(nsl):
            t, hh = j // B_HEADS, j % B_HEADS
            cols = slice(j * HEAD_DIM, (j + 1) * HEAD_DIM)
            dy = d_refs[t][:, hh * HEAD_DIM:(hh + 1) * HEAD_DIM].astype(F32)
            dxn = dy * cf + jnp.dot((dy * sf).astype(MXU), swap, preferred_element_type=F32)
            xv = p_ref[:, cols].astype(F32)
            r = lax.rsqrt(jnp.mean(xv * xv, axis=-1, keepdims=True) + EPS)
            xh = xv * r
            dg_ref[t:t + 1, :] += jnp.sum(dxn * xh, axis=0, keepdims=True)
            dxh = dxn * g_ref[t:t + 1, :]
            dp_ref[:, cols] = (r * (dxh - xh * jnp.mean(dxh * xh, axis=-1, keepdims=True))).astype(dp_ref.dtype)

    tspec = pl.BlockSpec((tm, HEAD_DIM), lambda i: (i, 0))
    dspec = pl.BlockSpec((tm, Wg), lambda i: (i, 0))
    n_in = 2 * ngr + 5
    return pl.pallas_call(
        body, name=name, grid=(M // tm,),
        in_specs=[dspec] * (2 * ngr) + [pl.BlockSpec((tm, Wqk), lambda i: (i, 0)), tspec, tspec,
                                        pl.BlockSpec((8, HEAD_DIM), lambda i: (0, 0)),
                                        pl.BlockSpec(memory_space=pl.ANY)],
        out_specs=[pl.BlockSpec((tm, Wqk), lambda i: (i, 0)), pl.BlockSpec((8, HEAD_DIM), lambda i: (0, 0))],
        out_shape=[jax.ShapeDtypeStruct(dproj.shape, dproj.dtype), jax.ShapeDtypeStruct((8, HEAD_DIM), F32)],
        input_output_aliases={n_in - 1: 0},
        compiler_params=_cp("arbitrary"),
    )(*dqs, *dks, proj, *tables, gains, dproj)


def _attn_tile(D, M):
    return max(HEAD_DIM * D, min(M, 2048))


class _TokenRows:
    GROUP = 16

    def __init__(self, D):
        self.D = D
        self.pitch = 24 if D == 16 else self.GROUP

    def rows(self, ntok):
        return ntok // self.GROUP * self.pitch

    def every_dth(self, tok0, n):
        start = tok0 // self.GROUP * self.pitch + tok0 % self.GROUP
        stride = self.D * self.pitch // self.GROUP
        return pl.ds(start, n) if stride == 1 else pl.ds(start, n, stride=stride)

    def put(self, dst, tok0, src_ref, ntok):
        if self.pitch == self.GROUP:
            dst[tok0:tok0 + ntok, :] = src_ref[...].astype(F32)
            return

        def group(i, carry):
            row = pl.multiple_of((tok0 // self.GROUP + i) * self.pitch, 8)
            dst[pl.ds(row, self.GROUP), :] = src_ref[pl.ds(pl.multiple_of(i * self.GROUP, self.GROUP), self.GROUP), :].astype(F32)
            return carry

        lax.fori_loop(0, ntok // self.GROUP, group, 0, unroll=8)

    def get(self, dst_ref, src, ntok):
        if self.pitch == self.GROUP:
            dst_ref[...] = src[0:ntok, :].astype(dst_ref.dtype)
            return

        def group(i, carry):
            row = pl.multiple_of(i * self.pitch, 8)
            dst_ref[pl.ds(pl.multiple_of(i * self.GROUP, self.GROUP), self.GROUP), :] = src[pl.ds(row, self.GROUP), :].astype(dst_ref.dtype)
            return carry

        lax.fori_loop(0, ntok // self.GROUP, group, 0, unroll=8)


def _attn_mask(base):
    qi = lax.broadcasted_iota(jnp.int32, (CHUNK, 2 * CHUNK), 0)
    ki = lax.broadcasted_iota(jnp.int32, (CHUNK, 2 * CHUNK), 1)
    return (ki >= qi) & (ki <= qi + CHUNK) & (ki >= CHUNK - base)


def _b_attn_fwd(qk, proj, g, name):
    M = qk.shape[0]
    D = B_DILATIONS[g]
    ngr = len(B_DILATIONS)
    T = _attn_tile(D, M)
    P = HEAD_DIM * D
    nsb = T // P
    Wg = B_HEADS * HEAD_DIM
    scale = np.float32(1.0 / np.sqrt(HEAD_DIM))

    lay = _TokenRows(D)
    RP, RT = lay.rows(P), lay.rows(T)

    def body(q_ref, k_ref, v_ref, o_ref, l_ref, qs, ks, vs, os_, ls):
        n = pl.program_id(1)

        @pl.when(n == 0)
        def _():
            ks[0:RP, :] = jnp.zeros((RP, HEAD_DIM), F32)
            vs[0:RP, :] = jnp.zeros((RP, HEAD_DIM), F32)

        lay.put(qs, 0, q_ref, T)
        lay.put(ks, P, k_ref, T)
        lay.put(vs, P, v_ref, T)

        for b in range(nsb):
            mask = _attn_mask(n * (T // D) + b * CHUNK)
            for r in range(D):
                start = b * P + r
                q = qs[lay.every_dth(start, CHUNK), :]
                k = ks[lay.every_dth(start, 2 * CHUNK), :]
                v = vs[lay.every_dth(start, 2 * CHUNK), :]
                s = jnp.where(mask, _dot_nt(q, k) * scale, NEG)
                m = jnp.max(s, axis=-1, keepdims=True)
                p = jnp.exp(s - m)
                l = jnp.sum(p, axis=-1, keepdims=True)
                o = _dot(p, v) / l
                os_[lay.every_dth(start, CHUNK), :] = o
                ls[lay.every_dth(start, CHUNK), :] = jnp.broadcast_to(m + jnp.log(l), (CHUNK, HEAD_DIM))

        lay.get(o_ref, os_, T)
        lay.get(l_ref, ls, T)
        ks[0:RP, :] = ks[RT:RT + RP, :]
        vs[0:RP, :] = vs[RT:RT + RP, :]

    blk = (T, HEAD_DIM)
    return pl.pallas_call(
        body, name=name, grid=(B_HEADS, M // T),
        in_specs=[pl.BlockSpec(blk, lambda h, n: (n, g * B_HEADS + h)),
                  pl.BlockSpec(blk, lambda h, n: (n, (ngr + g) * B_HEADS + h)),
                  pl.BlockSpec(blk, lambda h, n: (n, (2 * ngr + g) * B_HEADS + h))],
        out_specs=[pl.BlockSpec(blk, lambda h, n: (n, h)), pl.BlockSpec(blk, lambda h, n: (n, h))],
        out_shape=[jax.ShapeDtypeStruct((M, Wg), ACT), jax.ShapeDtypeStruct((M, Wg), F32)],
        scratch_shapes=[pltpu.VMEM((RT, HEAD_DIM), F32), pltpu.VMEM((RP + RT, HEAD_DIM), F32),
                        pltpu.VMEM((RP + RT, HEAD_DIM), F32), pltpu.VMEM((RT, HEAD_DIM), F32),
                        pltpu.VMEM((RT, HEAD_DIM), F32)],
        compiler_params=_cp("parallel", "arbitrary"),
    )(qk, qk, proj)


def _b_combine(os_, ls, proj, name):
    M, Wg = os_[0].shape
    ngr = len(B_DILATIONS)
    tm = min(M, 512)

    def body(*refs):
        o_refs, l_refs, z_ref = refs[:ngr], refs[ngr:2 * ngr], refs[2 * ngr]
        y_ref, o_ref, lse_ref = refs[2 * ngr + 1:]
        ls_ = [r[...] for r in l_refs]
        m = functools.reduce(jnp.maximum, ls_)
        es = [jnp.exp(l - m) for l in ls_]
        tot = functools.reduce(lambda a, b: a + b, es)
        o = functools.reduce(lambda a, b: a + b, [e * r[...].astype(F32) for e, r in zip(es, o_refs)]) / tot
        z = z_ref[...].astype(F32)
        y_ref[...] = (o.astype(F32) * (z * _sigmoid(z))).astype(y_ref.dtype)
        o_ref[...] = o.astype(o_ref.dtype)
        lse_ref[...] = m + jnp.log(tot)

    spec = pl.BlockSpec((tm, Wg), lambda i: (i, 0))
    return pl.pallas_call(
        body, name=name, grid=(M // tm,),
        in_specs=[spec] * (2 * ngr) + [pl.BlockSpec((tm, Wg), lambda i: (i, 3 * ngr))],
        out_specs=[spec] * 3,
        out_shape=[jax.ShapeDtypeStruct((M, Wg), ACT), jax.ShapeDtypeStruct((M, Wg), ACT),
                   jax.ShapeDtypeStruct((M, Wg), F32)],
        compiler_params=_cp("parallel"),
    )(*os_, *ls, proj)


def _b_bwd_pre(dout, w_out, o, proj, name):
    M, Wg = o.shape
    Dm = dout.shape[1]
    ngr = len(B_DILATIONS)
    tm = min(M, 512)

    def body(do_ref, wo_ref, o_ref, z_ref, dov_ref, dl_ref, dp_ref):
        dy = _dot_nt(do_ref[...], wo_ref[...])
        z = z_ref[...].astype(F32)
        sig = _sigmoid(z)
        ov = o_ref[...].astype(F32)
        dp_ref[...] = (dy * ov * (sig * (1.0 + z * (1.0 - sig)))).astype(dp_ref.dtype)
        dov = dy * (z * sig)
        dov_ref[...] = dov.astype(dov_ref.dtype)
        prod = dov * ov
        for h in range(B_HEADS):
            cols = slice(h * HEAD_DIM, (h + 1) * HEAD_DIM)
            dl_ref[:, cols] = jnp.broadcast_to(jnp.sum(prod[:, cols], axis=-1, keepdims=True), (tm, HEAD_DIM))

    spec = pl.BlockSpec((tm, Wg), lambda i: (i, 0))
    zspec = pl.BlockSpec((tm, Wg), lambda i: (i, 3 * ngr))
    return pl.pallas_call(
        body, name=name, grid=(M // tm,),
        in_specs=[pl.BlockSpec((tm, Dm), lambda i: (i, 0)), pl.BlockSpec((Wg, Dm), lambda i: (0, 0)), spec, zspec],
        out_specs=[spec, spec, zspec],
        out_shape=[jax.ShapeDtypeStruct((M, Wg), ACT), jax.ShapeDtypeStruct((M, Wg), F32),
                   jax.ShapeDtypeStruct(proj.shape, ACT)],
        compiler_params=_cp("parallel"),
    )(dout, w_out, o, proj)


def _b_attn_bwd(qk, proj, dov, lse, delta, dproj, g, name, comm=None):
    M = qk.shape[0]
    D = B_DILATIONS[g]
    ngr = len(B_DILATIONS)
    T = _attn_tile(D, M)
    P = HEAD_DIM * D
    nsb = T // P
    nt = M // T
    Wg = B_HEADS * HEAD_DIM
    scale = np.float32(1.0 / np.sqrt(HEAD_DIM))
    shift = T - P
    lay = _TokenRows(D)
    RP, RT = lay.rows(P), lay.rows(T)

    def body(q_ref, k_ref, v_ref, do_ref, l_ref, dl_ref, dp_any, dq_ref, dk_ref, dv_ref,
             qs, dos, lss, dls, ks, vs, dqs, dks, dvs):
        n = pl.program_id(1)

        @pl.when(n == 0)
        def _():
            ks[0:RP, :] = jnp.zeros((RP, HEAD_DIM), F32)
            vs[0:RP, :] = jnp.zeros((RP, HEAD_DIM), F32)
            dks[0:RT, :] = jnp.zeros((RT, HEAD_DIM), F32)
            dvs[0:RT, :] = jnp.zeros((RT, HEAD_DIM), F32)

        dks[RT:2 * RT, :] = jnp.zeros((RT, HEAD_DIM), F32)
        dvs[RT:2 * RT, :] = jnp.zeros((RT, HEAD_DIM), F32)

        @pl.when(n < nt)
        def _():
            lay.put(qs, 0, q_ref, T)
            lay.put(dos, 0, do_ref, T)
            lay.put(lss, 0, l_ref, T)
            lay.put(dls, 0, dl_ref, T)
            lay.put(ks, P, k_ref, T)
            lay.put(vs, P, v_ref, T)

            for b in range(nsb):
                mask = _attn_mask(n * (T // D) + b * CHUNK)
                for r in range(D):
                    start = b * P + r
                    qsl = lay.every_dth(start, CHUNK)
                    ksl = lay.every_dth(start, 2 * CHUNK)
                    dsl = lay.every_dth(start + shift, 2 * CHUNK)
                    q = qs[qsl, :]
                    do = dos[qsl, :]
                    k = ks[ksl, :]
                    v = vs[ksl, :]
                    s = _dot_nt(q, k) * scale
                    p = jnp.where(mask, jnp.exp(s - lss[qsl, :][:, :1]), 0.0)
                    dvs[dsl, :] += _dot_tn(p, do)
                    dp = _dot_nt(do, v)
                    ds = (p * (dp - dls[qsl, :][:, :1]) * scale).astype(MXU)
                    dqs[qsl, :] = _dot(ds, k)
                    dks[dsl, :] += _dot_tn(ds, q)

        lay.get(dq_ref, dqs, T)
        lay.get(dk_ref, dks, T)
        lay.get(dv_ref, dvs, T)
        dks[0:RT, :] = dks[RT:2 * RT, :]
        dvs[0:RT, :] = dvs[RT:2 * RT, :]
        ks[0:RP, :] = ks[RT:RT + RP, :]
        vs[0:RP, :] = vs[RT:RT + RP, :]

    blk = (T, HEAD_DIM)
    cur = lambda n: jnp.minimum(n, nt - 1)
    prv = lambda n: jnp.maximum(n - 1, 0)
    return _launch(
        body, name=name, grid=(B_HEADS, nt + 1),
        in_specs=[pl.BlockSpec(blk, lambda h, n: (cur(n), g * B_HEADS + h)),
                  pl.BlockSpec(blk, lambda h, n: (cur(n), (ngr + g) * B_HEADS + h)),
                  pl.BlockSpec(blk, lambda h, n: (cur(n), (2 * ngr + g) * B_HEADS + h)),
                  pl.BlockSpec(blk, lambda h, n: (cur(n), h)),
                  pl.BlockSpec(blk, lambda h, n: (cur(n), h)),
                  pl.BlockSpec(blk, lambda h, n: (cur(n), h)),
                  pl.BlockSpec(memory_space=pl.ANY)],
        out_specs=[pl.BlockSpec(blk, lambda h, n: (cur(n), h)),
                   pl.BlockSpec(blk, lambda h, n: (prv(n), h)),
                   pl.BlockSpec(blk, lambda h, n: (prv(n), (2 * ngr + g) * B_HEADS + h))],
        out_shape=[jax.ShapeDtypeStruct((M, Wg), ACT), jax.ShapeDtypeStruct((M, Wg), ACT),
                   jax.ShapeDtypeStruct(dproj.shape, dproj.dtype)],
        scratch=[pltpu.VMEM((RT, HEAD_DIM), F32)] * 4
        + [pltpu.VMEM((RP + RT, HEAD_DIM), F32)] * 2
        + [pltpu.VMEM((RT, HEAD_DIM), F32)]
        + [pltpu.VMEM((2 * RT, HEAD_DIM), F32)] * 2,
        aliases={6: 2},
        args=(qk, qk, proj, dov, lse, delta, dproj), sem=("parallel", "arbitrary"), comm=comm)


def _coords():
    return lax.axis_index("x"), lax.axis_index("y"), lax.axis_index("c")


def _gather_blocks(x_refs, out_refs, send_sems, recv_sems, local_sems):
    x, y, c = _coords()
    me, sibling = (x, y, c), (x, y, 1 - c)
    chips = [(1 - x, y), (x, 1 - y), (1 - x, 1 - y)]
    arrays = range(len(x_refs))

    def slot(a, px, py, pc):
        return out_refs[a].at[4 * px + 2 * py + pc]

    def copy(a, k, block, to, src=None):
        return _remote(slot(a, *block) if src is None else src, slot(a, *block), send_sems, recv_sems, 7 * a + k, to)

    mine = [pltpu.make_async_copy(x_refs[a], slot(a, *me), local_sems.at[a]) for a in arrays]
    first = [copy(a, 0, me, sibling, src=x_refs[a]) for a in arrays]
    first += [copy(a, 1 + j, me, (*chip, c), src=x_refs[a]) for j, chip in enumerate(chips) for a in arrays]
    for cp in mine + first:
        cp.start()
    passed = []
    for j, chip in enumerate(chips):
        for a in arrays:
            copy(a, 1 + j, (*chip, c), me).wait_recv()
            passed.append(copy(a, 4 + j, (*chip, c), sibling))
            passed[-1].start()
    for a in arrays:
        copy(a, 0, sibling, me).wait_recv()
        for j, chip in enumerate(chips):
            copy(a, 4 + j, (*chip, 1 - c), me).wait_recv()
    for cp in first + passed:
        cp.wait_send()
    for cp in mine:
        cp.wait()


def _all_gather_hbm(arrays, name):
    n = len(arrays)

    def body(*refs):
        _gather_blocks(refs[:n], refs[n:2 * n], *refs[2 * n:])

    return pl.pallas_call(
        body, name=name, in_specs=[_HBM] * n, out_specs=[_HBM] * n,
        out_shape=[jax.ShapeDtypeStruct((N_DEV,) + a.shape, a.dtype) for a in arrays],
        scratch_shapes=[pltpu.SemaphoreType.DMA((7 * n,)), pltpu.SemaphoreType.DMA((7 * n,)),
                        pltpu.SemaphoreType.DMA((n,))],
    )(*arrays)


def _all_reduce_small(part):
    R, C = part.shape

    def body(x_ref, tot_ref, gath, send_sems, recv_sems, local_sems):
        _gather_blocks([x_ref], [gath], send_sems, recv_sems, local_sems)
        acc = gath[0]
        for d in range(1, N_DEV):
            acc = acc + gath[d]
        tot_ref[...] = acc

    return pl.pallas_call(
        body, name="ar_small",
        in_specs=[pl.BlockSpec(memory_space=pltpu.VMEM)],
        out_specs=pl.BlockSpec(memory_space=pltpu.VMEM),
        out_shape=jax.ShapeDtypeStruct((R, C), F32),
        scratch_shapes=[pltpu.VMEM((N_DEV, R, C), F32),
                        pltpu.SemaphoreType.DMA((7,)), pltpu.SemaphoreType.DMA((7,)), pltpu.SemaphoreType.DMA((1,))],
        compiler_params=pltpu.CompilerParams(vmem_limit_bytes=VMEM_LIMIT),
    )(part)


def _remote(src, dst, send_sems, recv_sems, k, peer):
    return pltpu.make_async_remote_copy(src_ref=src, dst_ref=dst, send_sem=send_sems.at[k], recv_sem=recv_sems.at[k],
                                        device_id=peer, device_id_type=MESH)


def _ag_send(arrays):
    n = len(arrays)

    def make(c_in, c_out, send_sems, recv_sems, local_sems):
        x, y, c = _coords()
        peers = [(x, y, 1 - c), (1 - x, y, c), (x, 1 - y, c), (1 - x, 1 - y, c)]
        cps = []
        for a in range(n):
            src, dst = c_in[a], c_out[a].at[4 * x + 2 * y + c]
            cps.append(pltpu.make_async_copy(src, dst, local_sems.at[a]))
            cps += [_remote(src, dst, send_sems, recv_sems, 4 * a + k, peer) for k, peer in enumerate(peers)]
        return cps

    return _Comm(arrays, [jax.ShapeDtypeStruct((N_DEV,) + a.shape, a.dtype) for a in arrays], 4 * n, make, n_local=n)


def _ag_forward(gaths):
    n = len(gaths)

    def make(c_in, c_out, send_sems, recv_sems, local_sems):
        x, y, c = _coords()
        chips = [(1 - x, y), (x, 1 - y), (1 - x, 1 - y)]
        cps = []
        for a in range(n):
            buf = c_out[a]
            cps += [_remote(buf.at[4 * px + 2 * py + c], buf.at[4 * px + 2 * py + c], send_sems, recv_sems, 3 * a + j,
                            (x, y, 1 - c)) for j, (px, py) in enumerate(chips)]
        return cps

    return _Comm(gaths, [jax.ShapeDtypeStruct(g.shape, g.dtype) for g in gaths], 3 * n, make,
                 aliases={a: a for a in range(n)})


def _rs_sibling(grads):
    n = len(grads)

    def make(c_in, c_out, send_sems, recv_sems, local_sem):
        x, y, c = _coords()
        return [_remote(c_in[a].at[pl.ds(4 * (1 - c), 4)], c_out[a], send_sems, recv_sems, a, (x, y, 1 - c))
                for a in range(n)]

    return _Comm(grads, [jax.ShapeDtypeStruct((4,) + g.shape[1:], g.dtype) for g in grads], n, make)


def _rs_chips(parts):
    n = len(parts)

    def make(c_in, c_out, send_sems, recv_sems, local_sem):
        x, y, c = _coords()
        peers = [(x, 1 - y, c), (1 - x, y, c), (1 - x, 1 - y, c)]
        return [_remote(c_in[a].at[k], c_out[a].at[k], send_sems, recv_sems, 3 * a + k, peer)
                for a in range(n) for k, peer in enumerate(peers)]

    return _Comm(parts, [jax.ShapeDtypeStruct(p.shape, p.dtype) for p in parts], 3 * n, make)


def _row_tile(rows, cols):
    tr = min(rows, 1 << int(np.log2((1 << 18) // cols)))
    assert rows % tr == 0
    return tr


def _chip_partials(coords, g, r1, name):
    _, rows, C = g.shape
    tr = _row_tile(rows, C)

    def body(co_ref, g_ref, r_ref, o_ref):
        o_ref[...] = (g_ref[...] + r_ref[...]).astype(o_ref.dtype)

    def chip(k, co):
        return jnp.bitwise_xor(2 * co[0] + co[1], k + 1)

    return pl.pallas_call(
        body, name=name,
        grid_spec=pltpu.PrefetchScalarGridSpec(
            num_scalar_prefetch=1, grid=(3, rows // tr),
            in_specs=[pl.BlockSpec((None, tr, C), lambda k, t, co: (4 * co[2] + chip(k, co), t, 0)),
                      pl.BlockSpec((None, tr, C), lambda k, t, co: (chip(k, co), t, 0))],
            out_specs=pl.BlockSpec((None, tr, C), lambda k, t, co: (k, t, 0))),
        out_shape=jax.ShapeDtypeStruct((3, rows, C), WIRE),
        compiler_params=_cp("parallel", "parallel"),
    )(coords, g, r1)


def _adam_math(w, g, m, v):
    m = ADAM_B1 * m + (1.0 - ADAM_B1) * g
    v = ADAM_B2 * v + (1.0 - ADAM_B2) * (g * g)
    m_hat = m / (1.0 - ADAM_B1 ** ADAM_STEP)
    v_hat = v / (1.0 - ADAM_B2 ** ADAM_STEP)
    delta = -ADAM_LR * (m_hat / (jnp.sqrt(v_hat) + ADAM_EPS) + ADAM_WD * w)
    return delta, m, v


def _adamw_sharded(coords, w, m, v, g, r1, r2, name):
    rows, C = w.shape
    tr = _row_tile(rows, C)

    def body(co_ref, w_ref, m_ref, v_ref, g_ref, r1_ref, r2_ref, go_ref, d_ref, mo_ref, vo_ref):
        grad = g_ref[...] + r1_ref[...]
        for k in range(3):
            grad = grad + r2_ref[k].astype(F32)
        go_ref[...] = grad
        d_ref[...], mo_ref[...], vo_ref[...] = _adam_math(w_ref[...], grad, m_ref[...], v_ref[...])

    spec = pl.BlockSpec((tr, C), lambda t, co: (t, 0))
    return pl.pallas_call(
        body, name=name,
        grid_spec=pltpu.PrefetchScalarGridSpec(
            num_scalar_prefetch=1, grid=(rows // tr,),
            in_specs=[spec, spec, spec,
                      pl.BlockSpec((None, tr, C), lambda t, co: (4 * co[2] + 2 * co[0] + co[1], t, 0)),
                      pl.BlockSpec((None, tr, C), lambda t, co: (2 * co[0] + co[1], t, 0)),
                      pl.BlockSpec((3, tr, C), lambda t, co: (0, t, 0))],
            out_specs=[spec] * 4),
        out_shape=[jax.ShapeDtypeStruct((rows, C), F32)] * 4,
        compiler_params=_cp("parallel"),
    )(coords, w, m, v, g, r1, r2)


def _adamw_small(w, g, m, v, name):
    def body(w_ref, g_ref, m_ref, v_ref, d_ref, mo_ref, vo_ref):
        d_ref[...], mo_ref[...], vo_ref[...] = _adam_math(w_ref[...], g_ref[...], m_ref[...], v_ref[...])

    return pl.pallas_call(
        body, name=name, out_shape=[jax.ShapeDtypeStruct(w.shape, F32)] * 3,
        in_specs=[pl.BlockSpec(memory_space=pltpu.VMEM)] * 4,
        out_specs=[pl.BlockSpec(memory_space=pltpu.VMEM)] * 3,
    )(w, g, m, v)


def _reduce_scatter_adds(coords, grads, r1s, tag):
    return [_chip_partials(coords, g, r, f"rs_add_{tag}{i}") for i, (g, r) in enumerate(zip(grads, r1s))]


def kernel(x, norm_gain, a_w_in, a_v_gain, a_w_s, a_b_s, a_w_out, b_w_in, b_q_gain, b_k_gain, b_w_out, c_w_in, c_w_grp, c_scale, c_w_out, loss_target, m_norm_gain, m_a_w_in, m_a_v_gain, m_a_w_s, m_a_b_s, m_a_w_out, m_b_w_in, m_b_q_gain, m_b_k_gain, m_b_w_out, m_c_w_in, m_c_w_grp, m_c_scale, m_c_w_out, v_norm_gain, v_a_w_in, v_a_v_gain, v_a_w_s, v_a_b_s, v_a_w_out, v_b_w_in, v_b_q_gain, v_b_k_gain, v_b_w_out, v_c_w_in, v_c_w_grp, v_c_scale, v_c_w_out):
    cx, cy, cc = _coords()
    coords = jnp.stack([cx, cy, cc]).astype(jnp.int32)
    dev = 4 * cx + 2 * cy + cc
    Dm = x.shape[2]

    xs, tgt = x[0], loss_target[0]
    tables = _rope_tables(xs.shape[0])
    ng = lambda i: norm_gain[i:i + 1]
    ngr = len(B_DILATIONS)
    bst = [a_b_s[l].T for l in range(2)]
    b_gains = jnp.concatenate([b_q_gain[0], b_k_gain[0], jnp.zeros((2, HEAD_DIM), F32)], axis=0)
    nla, nlb, nlc = a_w_in.shape[2], b_w_in.shape[2], c_w_in.shape[2]
    ngp, rlc, cgc = c_w_grp.shape[1:]
    wire = lambda w: w.astype(WIRE)

    nvg, nsc = a_v_gain.size, c_scale.size
    vec = jnp.concatenate([a_v_gain.reshape(-1), c_scale.reshape(-1), jnp.zeros((1024 - nvg - nsc,), F32)]).reshape(8, 128)
    wa_in0, wa_out0, vecs = _all_gather_hbm([wire(a_w_in[0]), wire(a_w_out[0]), vec], "ag_layer0")
    wa_out0 = wa_out0.reshape(-1, Dm)
    vecs = vecs.reshape(N_DEV, -1)
    a_vg = vecs[:, :nvg].reshape((N_DEV,) + a_v_gain.shape).transpose(1, 0, 2).reshape(a_v_gain.shape[0], -1)
    c_sc = vecs[:, nvg:nvg + nsc].reshape(1, -1)

    h0, p0, *g1 = _norm_proj(xs, ng(0), wa_in0, "l0_proj", comm=_ag_send([wire(b_w_in[0]), wire(b_w_out[0])]))
    y0, wb_in, wb_out = _a_mid(p0, a_vg[0:1], a_w_s[0], bst[0], "l0_mid", comm=_ag_forward(g1))
    x1 = _out_proj(xs, y0, wa_out0, "l0_out")
    wb_out = wb_out.reshape(-1, Dm)

    later = [wire(c_w_in[0]), wire(c_w_grp[0]), wire(c_w_out[0]), wire(a_w_in[1]), wire(a_w_out[1])]
    h1, p1, *g2 = _norm_proj(x1, ng(1), wb_in, "l1_proj", comm=_ag_send(later))
    qk, wc_in, wc_grp, wc_out, wa_in1, wa_out1 = _b_qk_fwd(p1, tables, b_gains, "l1_qk", comm=_ag_forward(g2))
    ogs, lgs = zip(*[_b_attn_fwd(qk, p1, g, f"l1_attn{g}") for g in range(ngr)])
    y1, o1, lse = _b_combine(ogs, lgs, p1, "l1_comb")
    x2 = _out_proj(x1, y1, wb_out, "l1_out")
    wc_grp = wc_grp.transpose(1, 0, 2, 3).reshape(ngp, N_DEV * rlc, cgc)
    wc_out = wc_out.reshape(-1, Dm)
    wa_out1 = wa_out1.reshape(-1, Dm)

    h2, p2 = _norm_proj(x2, ng(2), wc_in, "l2_proj")
    y2 = _c_mid(p2, wc_grp, c_sc, "l2_mid")
    x3 = _out_proj(x2, y2, wc_out, "l2_out")
    h3, p3 = _norm_proj(x3, ng(3), wa_in1, "l3_proj")
    y3, = _a_mid(p3, a_vg[1:2], a_w_s[1], bst[1], "l3_mid")
    x4 = _out_proj(x3, y3, wa_out1, "l3_out")
    loss_local, dx4, dx4a = _loss_head(x4, tgt)
    loss = lax.psum(loss_local, ("x", "y", "c"))

    flat3 = lambda g: g.reshape(N_DEV, -1, g.shape[-1])
    dp3, dws1, dbs1, dvg1 = _a_bwd(dx4a, wa_out1, p3, a_vg[1:2], a_w_s[1], bst[1], "l3_bwd")
    grads3 = [_dw_in(h3, dp3, "l3_dwin"), _dw_out(y3, dx4a, "l3_dwout")]
    dx3, dx3a, dg3, *r1_3 = _dh_norm_bwd(dp3, wa_in1, x3, ng(3), dx4, "l3_dh", comm=_rs_sibling(grads3))
    parts3 = _reduce_scatter_adds(coords, grads3, r1_3, "l3_")

    dd, dz, gc_grp, dsc, *r2_3 = _c_bwd1(dx3a, wc_out, p2, wc_grp, c_sc, "l2_bwd1", comm=_rs_chips(parts3))
    dp2 = _c_bwd2(dd, dz, "l2_bwd2")
    grads2 = [_dw_in(h2, dp2, "l2_dwin"), _dw_out(y2, dx3a, "l2_dwout"), flat3(gc_grp)]
    dx2, dx2a, dg2, *r1_2 = _dh_norm_bwd(dp2, wc_in, x2, ng(2), dx3, "l2_dh", comm=_rs_sibling(grads2))
    parts2 = _reduce_scatter_adds(coords, grads2, r1_2, "l2_")

    dov, delta, dp1 = _b_bwd_pre(dx2a, wb_out, o1, p1, "l1_bwdpre")
    dqs, dks, r2_2 = [], [], None
    for g in range(ngr):
        dq, dk, dp1, *rest = _b_attn_bwd(qk, p1, dov, lse, delta, dp1, g, f"l1_attnbwd{g}",
                                         comm=_rs_chips(parts2) if g == 0 else None)
        if g == 0:
            r2_2 = rest
        dqs.append(dq)
        dks.append(dk)
    dp1, dgains = _b_qk_bwd(dqs, dks, p1, tables, b_gains, dp1, "l1_qkbwd")
    grads1 = [_dw_in(h1, dp1, "l1_dwin"), _dw_out(y1, dx2a, "l1_dwout")]
    dx1, dx1a, dg1, *r1_1 = _dh_norm_bwd(dp1, wb_in, x1, ng(1), dx2, "l1_dh", comm=_rs_sibling(grads1))
    parts1 = _reduce_scatter_adds(coords, grads1, r1_1, "l1_")

    dp0, dws0, dbs0, dvg0, *r2_1 = _a_bwd(dx1a, wa_out0, p0, a_vg[0:1], a_w_s[0], bst[0], "l0_bwd", comm=_rs_chips(parts1))
    grads0 = [_dw_in(h0, dp0, "l0_dwin"), _dw_out(y0, dx1a, "l0_dwout")]
    r1_0 = _run_comm(_rs_sibling(grads0), "l0_rs_sibling")
    parts0 = _reduce_scatter_adds(coords, grads0, r1_0, "l0_")
    dx0, _, dg0, *r2_0 = _dh_norm_bwd(dp0, wa_in0, xs, ng(0), dx1, "l0_dh", comm=_rs_chips(parts0))

    small = dict(norm=jnp.concatenate([dg0, dg1, dg2, dg3], axis=0), a_ws=jnp.stack([dws0, dws1]),
                 a_bs=jnp.stack([dbs0.T, dbs1.T]), b_gains=dgains, a_vg=jnp.concatenate([dvg0, dvg1], axis=0), c_sc=dsc)

    order = ["norm", "a_ws", "a_bs", "b_gains", "a_vg", "c_sc"]
    rows = [small[k].reshape(-1, 128) for k in order]
    roff = np.cumsum([0] + [r.shape[0] for r in rows])
    tot = _all_reduce_small(jnp.concatenate(rows, axis=0))
    sm = {k: tot[int(roff[i]):int(roff[i + 1])].reshape(small[k].shape) for i, k in enumerate(order)}
    vl = a_v_gain.shape[1]
    g_small = dict(
        norm_gain=sm["norm"], a_w_s=sm["a_ws"], a_b_s=sm["a_bs"],
        b_q_gain=sm["b_gains"][None, 0:3], b_k_gain=sm["b_gains"][None, 3:6],
        a_v_gain=lax.dynamic_slice_in_dim(sm["a_vg"], dev * vl, vl, axis=1),
        c_scale=lax.dynamic_slice_in_dim(sm["c_sc"], dev * vl, vl, axis=1),
    )

    shares = dict(
        a_w_in=[(grads0[0], r1_0[0], r2_0[0]), (grads3[0], r1_3[0], r2_3[0])],
        a_w_out=[(grads0[1], r1_0[1], r2_0[1]), (grads3[1], r1_3[1], r2_3[1])],
        b_w_in=[(grads1[0], r1_1[0], r2_1[0])], b_w_out=[(grads1[1], r1_1[1], r2_1[1])],
        c_w_in=[(grads2[0], r1_2[0], r2_2[0])], c_w_out=[(grads2[1], r1_2[1], r2_2[1])],
        c_w_grp=[(grads2[2], r1_2[2], r2_2[2])])

    params = dict(a_w_in=a_w_in, a_w_out=a_w_out, b_w_in=b_w_in, b_w_out=b_w_out, c_w_in=c_w_in, c_w_grp=c_w_grp, c_w_out=c_w_out,
                  norm_gain=norm_gain, a_v_gain=a_v_gain, a_w_s=a_w_s, a_b_s=a_b_s, b_q_gain=b_q_gain, b_k_gain=b_k_gain, c_scale=c_scale)
    moms = dict(a_w_in=(m_a_w_in, v_a_w_in), a_w_out=(m_a_w_out, v_a_w_out), b_w_in=(m_b_w_in, v_b_w_in), b_w_out=(m_b_w_out, v_b_w_out),
                c_w_in=(m_c_w_in, v_c_w_in), c_w_grp=(m_c_w_grp, v_c_w_grp), c_w_out=(m_c_w_out, v_c_w_out),
                norm_gain=(m_norm_gain, v_norm_gain), a_v_gain=(m_a_v_gain, v_a_v_gain), a_w_s=(m_a_w_s, v_a_w_s),
                a_b_s=(m_a_b_s, v_a_b_s), b_q_gain=(m_b_q_gain, v_b_q_gain), b_k_gain=(m_b_k_gain, v_b_k_gain),
                c_scale=(m_c_scale, v_c_scale))
    grad, delta, new_m, new_v = {}, {}, {}, {}
    for pname, layers in shares.items():
        w, (m, v) = params[pname], moms[pname]
        C = w.shape[-1]
        per_layer = [_adamw_sharded(coords, w[l].reshape(-1, C), m[l].reshape(-1, C), v[l].reshape(-1, C), g, r1, r2,
                                    f"adamw_{pname}{l}") for l, (g, r1, r2) in enumerate(layers)]
        grad[pname], delta[pname], new_m[pname], new_v[pname] = [
            jnp.stack([o.reshape(w.shape[1:]) for o in outs]) for outs in zip(*per_layer)]
    for pname, g in g_small.items():
        w = params[pname]
        C = w.shape[-1]
        outs = _adamw_small(w.reshape(-1, C), g.reshape(-1, C), moms[pname][0].reshape(-1, C), moms[pname][1].reshape(-1, C),
                            f"adamw_{pname}")
        grad[pname] = g.reshape(w.shape)
        delta[pname], new_m[pname], new_v[pname] = [o.reshape(w.shape) for o in outs]

    wnames = ["norm_gain", "a_w_in", "a_v_gain", "a_w_s", "a_b_s", "a_w_out", "b_w_in", "b_q_gain", "b_k_gain", "b_w_out",
              "c_w_in", "c_w_grp", "c_scale", "c_w_out"]
    return (loss, dx0[None], *[grad[n] for n in wnames], *[delta[n] for n in wnames],
            *[new_m[n] for n in wnames], *[new_v[n] for n in wnames])
```

```python
import functools

import numpy as np
import jax
import jax.numpy as jnp
from jax import lax
from jax.experimental import pallas as pl
from jax.experimental.pallas import tpu as pltpu

F32 = jnp.float32
MXU = jnp.bfloat16
ACT = jnp.bfloat16
WIRE = jnp.bfloat16

EPS = 1e-6
CHUNK = 128
A_GROUPS = 8
HEAD_DIM = 128
B_HEADS = 8
B_DILATIONS = (1, 4, 16)
ROPE_DIM = 32
ROPE_THETA = 500000.0
POOL_SIZES = (2, 4, 8, 16)
POOL_HALO = 16
N_DEV = 8
NEG = -1e30

ADAM_LR, ADAM_B1, ADAM_B2, ADAM_EPS, ADAM_WD, ADAM_STEP = 0.001, 0.9, 0.999, 1e-08, 0.01, 10

VMEM_LIMIT = 56 * 1024 * 1024
MESH = pl.DeviceIdType.MESH


def _cp(*sem):
    return pltpu.CompilerParams(dimension_semantics=sem, vmem_limit_bytes=VMEM_LIMIT)


def _sigmoid(z):
    return 1.0 / (1.0 + jnp.exp(-z))


def _dot(a, b):
    return jnp.dot(a.astype(MXU), b.astype(MXU), preferred_element_type=F32)


def _dot_nt(a, b):
    return lax.dot_general(a.astype(MXU), b.astype(MXU), (((1,), (1,)), ((), ())), preferred_element_type=F32)


def _dot_tn(a, b):
    return lax.dot_general(a.astype(MXU), b.astype(MXU), (((0,), (0,)), ((), ())), preferred_element_type=F32)


def _chunk_slot(d):
    return (d % 2) * 4 + d // 2


class _Comm:
    def __init__(self, inputs, out_shapes, n_remote, make, aliases=None, n_local=1):
        self.inputs = list(inputs)
        self.out_shapes = list(out_shapes)
        self.n_remote = n_remote
        self.n_local = n_local
        self.make = make
        self.aliases = dict(aliases or {})

    def sems(self):
        return [pltpu.SemaphoreType.DMA((self.n_remote,)), pltpu.SemaphoreType.DMA((self.n_remote,)),
                pltpu.SemaphoreType.DMA((self.n_local,))]


_HBM = pl.BlockSpec(memory_space=pl.ANY)


def _launch(body, *, name, grid, in_specs, out_specs, out_shape, args, sem, scratch=(), aliases=None, comm=None):
    in_specs, out_specs, out_shape, scratch = list(in_specs), list(out_specs), list(out_shape), list(scratch)
    aliases = dict(aliases or {})
    if comm is None:
        return pl.pallas_call(body, name=name, grid=grid, in_specs=in_specs, out_specs=out_specs, out_shape=out_shape,
                              scratch_shapes=scratch, input_output_aliases=aliases, compiler_params=_cp(*sem))(*args)
    n_in, n_out, n_sc = len(in_specs), len(out_specs), len(scratch)
    nci, nco = len(comm.inputs), len(comm.out_shapes)

    def hosted(*refs):
        b_in, c_in = refs[:n_in], refs[n_in:n_in + nci]
        o0 = n_in + nci
        b_out, c_out = refs[o0:o0 + n_out], refs[o0 + n_out:o0 + n_out + nco]
        s0 = o0 + n_out + nco
        b_sc, sems = refs[s0:s0 + n_sc], refs[s0 + n_sc:]
        ids = [pl.program_id(a) for a in range(len(grid))]
        first = functools.reduce(jnp.logical_and, [i == 0 for i in ids])
        last = functools.reduce(jnp.logical_and, [i == g - 1 for i, g in zip(ids, grid)])

        @pl.when(first)
        def _():
            for cp in comm.make(c_in, c_out, *sems):
                cp.start()

        body(*b_in, *b_out, *b_sc)

        @pl.when(last)
        def _():
            for cp in comm.make(c_in, c_out, *sems):
                cp.wait()

    for ci, co in comm.aliases.items():
        aliases[n_in + ci] = n_out + co
    return pl.pallas_call(
        hosted, name=name, grid=grid, in_specs=in_specs + [_HBM] * nci, out_specs=out_specs + [_HBM] * nco,
        out_shape=out_shape + comm.out_shapes, scratch_shapes=scratch + comm.sems(),
        input_output_aliases=aliases, compiler_params=_cp(*["arbitrary"] * len(grid)))(*args, *comm.inputs)


def _run_comm(comm, name):
    nci, nco = len(comm.inputs), len(comm.out_shapes)

    def body(*refs):
        cps = comm.make(refs[:nci], refs[nci:nci + nco], *refs[nci + nco:])
        for cp in cps:
            cp.start()
        for cp in cps:
            cp.wait()

    return pl.pallas_call(
        body, name=name, in_specs=[_HBM] * nci, out_specs=[_HBM] * nco, out_shape=comm.out_shapes,
        scratch_shapes=comm.sems(), input_output_aliases=dict(comm.aliases))(*comm.inputs)


def _norm_proj(x, gain, w_dm, name, comm=None):
    M, Dm = x.shape
    nd, _, nl = w_dm.shape
    tm = min(M, 1024)

    def body(x_ref, g_ref, w_ref, h_ref, p_ref):
        @pl.when(pl.program_id(1) == 0)
        def _():
            xv = x_ref[...]
            r = lax.rsqrt(jnp.mean(xv * xv, axis=-1, keepdims=True) + EPS)
            h_ref[...] = (xv * r * g_ref[...]).astype(h_ref.dtype)

        p_ref[...] = _dot(h_ref[...], w_ref[...]).astype(p_ref.dtype)

    return _launch(
        body, name=name, grid=(M // tm, nd),
        in_specs=[pl.BlockSpec((tm, Dm), lambda i, j: (i, 0)),
                  pl.BlockSpec((1, Dm), lambda i, j: (0, 0)),
                  pl.BlockSpec((None, Dm, nl), lambda i, j: (j, 0, 0))],
        out_specs=[pl.BlockSpec((tm, Dm), lambda i, j: (i, 0)),
                   pl.BlockSpec((tm, nl), lambda i, j: (i, j))],
        out_shape=[jax.ShapeDtypeStruct((M, Dm), ACT), jax.ShapeDtypeStruct((M, nd * nl), ACT)],
        args=(x, gain, w_dm), sem=("parallel", "arbitrary"), comm=comm)


def _out_proj(x, y, w, name):
    M, Dm = x.shape
    K = y.shape[1]
    tm = min(M, 512)

    def body(x_ref, y_ref, w_ref, o_ref):
        o_ref[...] = x_ref[...] + _dot(y_ref[...], w_ref[...])

    return pl.pallas_call(
        body, name=name, grid=(M // tm,),
        in_specs=[pl.BlockSpec((tm, Dm), lambda i: (i, 0)),
                  pl.BlockSpec((tm, K), lambda i: (i, 0)),
                  pl.BlockSpec((K, Dm), lambda i: (0, 0))],
        out_specs=pl.BlockSpec((tm, Dm), lambda i: (i, 0)),
        out_shape=jax.ShapeDtypeStruct((M, Dm), F32),
        compiler_params=_cp("parallel"),
    )(x, y, w)


def _loss_head(xf, target):
    M, Dm = xf.shape
    tm = min(M, 512)

    def body(x_ref, t_ref, dx_ref, dxa_ref, l_ref):
        @pl.when(pl.program_id(0) == 0)
        def _():
            l_ref[...] = jnp.zeros_like(l_ref)

        err = x_ref[...] - t_ref[...]
        dx = err * (1.0 / Dm)
        dx_ref[...] = dx
        dxa_ref[...] = dx.astype(dxa_ref.dtype)
        l_ref[...] += jnp.sum(err * err) * (0.5 / Dm)

    spec = pl.BlockSpec((tm, Dm), lambda i: (i, 0))
    dx, dxa, l = pl.pallas_call(
        body, name="loss_head", grid=(M // tm,),
        in_specs=[spec] * 2,
        out_specs=[spec, spec, pl.BlockSpec((8, 128), lambda i: (0, 0))],
        out_shape=[jax.ShapeDtypeStruct((M, Dm), F32), jax.ShapeDtypeStruct((M, Dm), ACT),
                   jax.ShapeDtypeStruct((8, 128), F32)],
        compiler_params=_cp("arbitrary"),
    )(xf, target)
    return l[0, 0], dx, dxa


def _dw_in(h, dproj, name):
    M, Dm = h.shape
    nl = dproj.shape[1] // N_DEV
    tt = min(M, 1024)

    def body(a_ref, b_ref, o_ref):
        @pl.when(pl.program_id(1) == 0)
        def _():
            o_ref[...] = jnp.zeros_like(o_ref)

        o_ref[...] += _dot_tn(a_ref[...], b_ref[...])

    return pl.pallas_call(
        body, name=name, grid=(N_DEV, M // tt),
        in_specs=[pl.BlockSpec((tt, Dm), lambda j, t: (t, 0)), pl.BlockSpec((tt, nl), lambda j, t: (t, j))],
        out_specs=pl.BlockSpec((None, Dm, nl), lambda j, t: (_chunk_slot(j), 0, 0)),
        out_shape=jax.ShapeDtypeStruct((N_DEV, Dm, nl), F32),
        compiler_params=_cp("parallel", "arbitrary"),
    )(h, dproj)


def _dw_out(y, dout, name):
    M, K = y.shape
    Dm = dout.shape[1]
    kl = K // N_DEV
    tt = min(M, 512)

    def body(a_ref, b_ref, o_ref):
        @pl.when(pl.program_id(0) == 0)
        def _():
            o_ref[...] = jnp.zeros_like(o_ref)

        b = b_ref[...]
        for j in range(N_DEV):
            o_ref[_chunk_slot(j)] += _dot_tn(a_ref[:, j * kl:(j + 1) * kl], b)

    return pl.pallas_call(
        body, name=name, grid=(M // tt,),
        in_specs=[pl.BlockSpec((tt, K), lambda t: (t, 0)), pl.BlockSpec((tt, Dm), lambda t: (t, 0))],
        out_specs=pl.BlockSpec((N_DEV, kl, Dm), lambda t: (0, 0, 0)),
        out_shape=jax.ShapeDtypeStruct((N_DEV, kl, Dm), F32),
        compiler_params=_cp("arbitrary"),
    )(y, dout)


def _dh_norm_bwd(dproj, w_dm, x, gain, dres, name, comm=None):
    M, Dm = x.shape
    nd, _, nl = w_dm.shape
    tm = min(M, 1024)

    def body(dp_ref, w_ref, x_ref, g_ref, dr_ref, dx_ref, dxa_ref, dg_ref, acc_ref):
        i, j = pl.program_id(0), pl.program_id(1)

        @pl.when(j == 0)
        def _():
            acc_ref[...] = jnp.zeros_like(acc_ref)

        acc_ref[...] += _dot_nt(dp_ref[...], w_ref[...])

        @pl.when(j == nd - 1)
        def _():
            @pl.when(i == 0)
            def _():
                dg_ref[...] = jnp.zeros_like(dg_ref)

            dh = acc_ref[...]
            xv = x_ref[...]
            r = lax.rsqrt(jnp.mean(xv * xv, axis=-1, keepdims=True) + EPS)
            xn = xv * r
            dg_ref[...] += jnp.sum(dh * xn, axis=0, keepdims=True)
            dxn = dh * g_ref[...]
            dx = dr_ref[...] + r * (dxn - xn * jnp.mean(dxn * xn, axis=-1, keepdims=True))
            dx_ref[...] = dx
            dxa_ref[...] = dx.astype(dxa_ref.dtype)

    row = pl.BlockSpec((tm, Dm), lambda i, j: (i, 0))
    return _launch(
        body, name=name, grid=(M // tm, nd),
        in_specs=[pl.BlockSpec((tm, nl), lambda i, j: (i, j)),
                  pl.BlockSpec((None, Dm, nl), lambda i, j: (j, 0, 0)),
                  row, pl.BlockSpec((1, Dm), lambda i, j: (0, 0)), row],
        out_specs=[row, row, pl.BlockSpec((1, Dm), lambda i, j: (0, 0))],
        out_shape=[jax.ShapeDtypeStruct((M, Dm), F32), jax.ShapeDtypeStruct((M, Dm), ACT),
                   jax.ShapeDtypeStruct((1, Dm), F32)],
        scratch=[pltpu.VMEM((tm, Dm), F32)],
        args=(dproj, w_dm, x, gain, dres), sem=("arbitrary", "arbitrary"), comm=comm)


def _tril_mask():
    return lax.broadcasted_iota(jnp.int32, (CHUNK, CHUNK), 0) >= lax.broadcasted_iota(jnp.int32, (CHUNK, CHUNK), 1)


def _a_mid(proj, v_gain, w_s, b_st, name, comm=None):
    M = proj.shape[0]
    W = proj.shape[1] // 3
    gd = W // A_GROUPS
    tm = min(M, 256)

    def body(p_ref, vg_ref, ws_ref, bs_ref, y_ref):
        pv = p_ref[:, W:2 * W].astype(F32)
        r = lax.rsqrt(jnp.mean(pv * pv, axis=-1, keepdims=True) + EPS)
        v = (pv * r * vg_ref[...]).astype(MXU)
        tri = _tril_mask()
        for g in range(A_GROUPS):
            wg = jnp.where(tri, ws_ref[g], 0.0).astype(MXU)
            bcol = bs_ref[:, g:g + 1]
            for c in range(tm // CHUNK):
                rows, cols = slice(c * CHUNK, (c + 1) * CHUNK), slice(g * gd, (g + 1) * gd)
                mixed = jnp.dot(wg, v[rows, cols], preferred_element_type=F32) + bcol
                u = p_ref[rows, g * gd:(g + 1) * gd].astype(F32)
                z = p_ref[rows, 2 * W + g * gd:2 * W + (g + 1) * gd].astype(F32)
                y_ref[rows, cols] = (u * mixed * (z * _sigmoid(z))).astype(y_ref.dtype)

    return _launch(
        body, name=name, grid=(M // tm,),
        in_specs=[pl.BlockSpec((tm, 3 * W), lambda i: (i, 0)),
                  pl.BlockSpec((1, W), lambda i: (0, 0)),
                  pl.BlockSpec((A_GROUPS, CHUNK, CHUNK), lambda i: (0, 0, 0)),
                  pl.BlockSpec((CHUNK, A_GROUPS), lambda i: (0, 0))],
        out_specs=[pl.BlockSpec((tm, W), lambda i: (i, 0))],
        out_shape=[jax.ShapeDtypeStruct((M, W), ACT)],
        args=(proj, v_gain, w_s, b_st), sem=("parallel",), comm=comm)


def _a_bwd(dout, w_out, proj, v_gain, w_s, b_st, name, comm=None):
    M = proj.shape[0]
    W = proj.shape[1] // 3
    Dm = dout.shape[1]
    gd = W // A_GROUPS
    tm = min(M, 256)
    nt = M // tm

    def body(do_ref, wo_ref, p_ref, vg_ref, ws_ref, bs_ref, dp_ref, dws_ref, dbs_ref, dvg_ref, dv_s):
        i = pl.program_id(0)

        @pl.when(i == 0)
        def _():
            dws_ref[...] = jnp.zeros_like(dws_ref)
            dbs_ref[...] = jnp.zeros_like(dbs_ref)
            dvg_ref[...] = jnp.zeros_like(dvg_ref)

        dy = _dot_nt(do_ref[...], wo_ref[...])
        pv = p_ref[:, W:2 * W].astype(F32)
        r = lax.rsqrt(jnp.mean(pv * pv, axis=-1, keepdims=True) + EPS)
        pvn = pv * r
        vg = vg_ref[...]
        v = (pvn * vg).astype(MXU)
        tri = _tril_mask()
        for g in range(A_GROUPS):
            wf = jnp.where(tri, ws_ref[g], 0.0)
            wg = wf.astype(MXU)
            wgt = wf.T.astype(MXU)
            bcol = bs_ref[:, g:g + 1]
            for c in range(tm // CHUNK):
                rows, cols = slice(c * CHUNK, (c + 1) * CHUNK), slice(g * gd, (g + 1) * gd)
                vb = v[rows, cols]
                mixed = jnp.dot(wg, vb, preferred_element_type=F32) + bcol
                u = p_ref[rows, g * gd:(g + 1) * gd].astype(F32)
                z = p_ref[rows, 2 * W + g * gd:2 * W + (g + 1) * gd].astype(F32)
                sig = _sigmoid(z)
                sz = z * sig
                dyb = dy[rows, cols]
                dp_ref[rows, g * gd:(g + 1) * gd] = (dyb * mixed * sz).astype(dp_ref.dtype)
                dp_ref[rows, 2 * W + g * gd:2 * W + (g + 1) * gd] = (
                    dyb * u * mixed * (sig * (1.0 + z * (1.0 - sig)))).astype(dp_ref.dtype)
                dmix = dyb * u * sz
                dws_ref[g] += _dot_nt(dmix, vb)
                dbs_ref[:, g:g + 1] += jnp.sum(dmix, axis=1, keepdims=True)
                dv_s[rows, cols] = jnp.dot(wgt, dmix.astype(MXU), preferred_element_type=F32)
        dv = dv_s[...]
        dvg_ref[...] += jnp.sum(dv * pvn, axis=0, keepdims=True)
        dpvn = dv * vg
        dp_ref[:, W:2 * W] = (r * (dpvn - pvn * jnp.mean(dpvn * pvn, axis=-1, keepdims=True))).astype(dp_ref.dtype)

        @pl.when(i == nt - 1)
        def _():
            for g in range(A_GROUPS):
                dws_ref[g] = jnp.where(tri, dws_ref[g], 0.0)

    return _launch(
        body, name=name, grid=(nt,),
        in_specs=[pl.BlockSpec((tm, Dm), lambda i: (i, 0)),
                  pl.BlockSpec((W, Dm), lambda i: (0, 0)),
                  pl.BlockSpec((tm, 3 * W), lambda i: (i, 0)),
                  pl.BlockSpec((1, W), lambda i: (0, 0)),
                  pl.BlockSpec((A_GROUPS, CHUNK, CHUNK), lambda i: (0, 0, 0)),
                  pl.BlockSpec((CHUNK, A_GROUPS), lambda i: (0, 0))],
        out_specs=[pl.BlockSpec((tm, 3 * W), lambda i: (i, 0)),
                   pl.BlockSpec((A_GROUPS, CHUNK, CHUNK), lambda i: (0, 0, 0)),
                   pl.BlockSpec((CHUNK, A_GROUPS), lambda i: (0, 0)),
                   pl.BlockSpec((1, W), lambda i: (0, 0))],
        out_shape=[jax.ShapeDtypeStruct((M, 3 * W), ACT),
                   jax.ShapeDtypeStruct((A_GROUPS, CHUNK, CHUNK), F32),
                   jax.ShapeDtypeStruct((CHUNK, A_GROUPS), F32),
                   jax.ShapeDtypeStruct((1, W), F32)],
        scratch=[pltpu.VMEM((tm, W), F32)],
        args=(dout, w_out, proj, v_gain, w_s, b_st), sem=("arbitrary",), comm=comm)


def _pool_diff(xg, tail, i, tm, w):
    t = lax.broadcasted_iota(jnp.int32, (tm, tm + POOL_HALO), 0)
    s = lax.broadcasted_iota(jnp.int32, (tm, tm + POOL_HALO), 1)
    off = t - (s - POOL_HALO)
    band = jnp.where((off >= 0) & (off < w), 1.0, 0.0).astype(MXU)
    tail = jnp.where(i > 0, tail, jnp.zeros_like(tail))
    ext = jnp.concatenate([tail, xg], axis=0)
    ssum = jnp.dot(band, ext.astype(MXU), preferred_element_type=F32)
    tglob = i * tm + lax.broadcasted_iota(jnp.int32, (tm, 1), 0)
    cnt = jnp.minimum(tglob + 1, w).astype(F32)
    return ssum / cnt - xg.astype(F32)


def _c_mid(proj, w_grp, scale, name):
    M = proj.shape[0]
    W = proj.shape[1] // 2
    ng = len(POOL_SIZES)
    cg = W // ng
    tm = min(M, 256)
    hb = tm // POOL_HALO

    def body(xc_ref, tail_ref, z_ref, wg_ref, sc_ref, y_ref):
        i = pl.program_id(0)
        for g, w in enumerate(POOL_SIZES):
            cols = slice(g * cg, (g + 1) * cg)
            d = _pool_diff(xc_ref[:, cols], tail_ref[:, cols], i, tm, w)
            mixed = _dot(d, wg_ref[g]) * sc_ref[:, cols]
            z = z_ref[:, cols].astype(F32)
            y_ref[:, cols] = (mixed * (z * _sigmoid(z))).astype(y_ref.dtype)

    return pl.pallas_call(
        body, name=name, grid=(M // tm,),
        in_specs=[pl.BlockSpec((tm, W), lambda i: (i, 0)),
                  pl.BlockSpec((POOL_HALO, W), lambda i: (jnp.maximum(i * hb - 1, 0), 0)),
                  pl.BlockSpec((tm, W), lambda i: (i, 1)),
                  pl.BlockSpec((ng, cg, cg), lambda i: (0, 0, 0)),
                  pl.BlockSpec((1, W), lambda i: (0, 0))],
        out_specs=pl.BlockSpec((tm, W), lambda i: (i, 0)),
        out_shape=jax.ShapeDtypeStruct((M, W), ACT),
        compiler_params=_cp("parallel"),
    )(proj, proj, proj, w_grp, scale)


def _c_bwd1(dout, w_out, proj, w_grp, scale, name, comm=None):
    M = proj.shape[0]
    W = proj.shape[1] // 2
    Dm = dout.shape[1]
    ng = len(POOL_SIZES)
    cg = W // ng
    rl = cg // N_DEV
    tm = min(M, 256)
    hb = tm // POOL_HALO
    nt = M // tm

    def body(do_ref, wo_ref, xc_ref, tail_ref, z_ref, wg_ref, sc_ref, dd_ref, dz_ref, dwg_ref, dsc_ref, acc_ref):
        i = pl.program_id(0)

        @pl.when(i == 0)
        def _():
            acc_ref[...] = jnp.zeros_like(acc_ref)
            dsc_ref[...] = jnp.zeros_like(dsc_ref)

        dy = _dot_nt(do_ref[...], wo_ref[...])
        for g, w in enumerate(POOL_SIZES):
            cols = slice(g * cg, (g + 1) * cg)
            d = _pool_diff(xc_ref[:, cols], tail_ref[:, cols], i, tm, w)
            mr = _dot(d, wg_ref[g])
            sc = sc_ref[:, cols]
            z = z_ref[:, cols].astype(F32)
            sig = _sigmoid(z)
            dyg = dy[:, cols]
            dmixed = dyg * (z * sig)
            dz_ref[:, cols] = (dyg * (mr * sc) * (sig * (1.0 + z * (1.0 - sig)))).astype(dz_ref.dtype)
            dsc_ref[:, cols] += jnp.sum(dmixed * mr, axis=0, keepdims=True)
            dmr = (dmixed * sc).astype(MXU)
            acc_ref[g] += _dot_tn(d, dmr)
            dd_ref[:, cols] = _dot_nt(dmr, wg_ref[g]).astype(dd_ref.dtype)

        @pl.when(i == nt - 1)
        def _():
            for dev in range(N_DEV):
                for g in range(ng):
                    dwg_ref[_chunk_slot(dev), g] = acc_ref[g, dev * rl:(dev + 1) * rl, :]

    return _launch(
        body, name=name, grid=(nt,),
        in_specs=[pl.BlockSpec((tm, Dm), lambda i: (i, 0)),
                  pl.BlockSpec((W, Dm), lambda i: (0, 0)),
                  pl.BlockSpec((tm, W), lambda i: (i, 0)),
                  pl.BlockSpec((POOL_HALO, W), lambda i: (jnp.maximum(i * hb - 1, 0), 0)),
                  pl.BlockSpec((tm, W), lambda i: (i, 1)),
                  pl.BlockSpec((ng, cg, cg), lambda i: (0, 0, 0)),
                  pl.BlockSpec((1, W), lambda i: (0, 0))],
        out_specs=[pl.BlockSpec((tm, W), lambda i: (i, 0)),
                   pl.BlockSpec((tm, W), lambda i: (i, 0)),
                   pl.BlockSpec((N_DEV, ng, rl, cg), lambda i: (0, 0, 0, 0)),
                   pl.BlockSpec((1, W), lambda i: (0, 0))],
        out_shape=[jax.ShapeDtypeStruct((M, W), ACT), jax.ShapeDtypeStruct((M, W), ACT),
                   jax.ShapeDtypeStruct((N_DEV, ng, rl, cg), F32), jax.ShapeDtypeStruct((1, W), F32)],
        scratch=[pltpu.VMEM((ng, cg, cg), F32)],
        args=(dout, w_out, proj, proj, proj, w_grp, scale), sem=("arbitrary",), comm=comm)


def _c_bwd2(dd, dz, name):
    M, W = dd.shape
    ng = len(POOL_SIZES)
    cg = W // ng
    tm = min(M, 256)
    hb = tm // POOL_HALO
    nt = M // tm

    def body(dd_ref, head_ref, dz_ref, dp_ref):
        i = pl.program_id(0)
        s = lax.broadcasted_iota(jnp.int32, (tm, tm + POOL_HALO), 0)
        t = lax.broadcasted_iota(jnp.int32, (tm, tm + POOL_HALO), 1)
        off = t - s
        tglob = i * tm + lax.broadcasted_iota(jnp.int32, (tm + POOL_HALO, 1), 0)
        for g, w in enumerate(POOL_SIZES):
            cols = slice(g * cg, (g + 1) * cg)
            ddg = dd_ref[:, cols].astype(F32)
            head = head_ref[:, cols].astype(F32)
            head = jnp.where(i < nt - 1, head, jnp.zeros_like(head))
            cnt = jnp.minimum(tglob + 1, w).astype(F32)
            ext = (jnp.concatenate([ddg, head], axis=0) / cnt).astype(MXU)
            band = jnp.where((off >= 0) & (off < w), 1.0, 0.0).astype(MXU)
            dp_ref[:, cols] = (jnp.dot(band, ext, preferred_element_type=F32) - ddg).astype(dp_ref.dtype)
        dp_ref[:, W:] = dz_ref[...]

    return pl.pallas_call(
        body, name=name, grid=(nt,),
        in_specs=[pl.BlockSpec((tm, W), lambda i: (i, 0)),
                  pl.BlockSpec((POOL_HALO, W), lambda i: (jnp.minimum((i + 1) * hb, M // POOL_HALO - 1), 0)),
                  pl.BlockSpec((tm, W), lambda i: (i, 0))],
        out_specs=pl.BlockSpec((tm, 2 * W), lambda i: (i, 0)),
        out_shape=jax.ShapeDtypeStruct((M, 2 * W), ACT),
        compiler_params=_cp("parallel"),
    )(dd, dd, dz)


def _rope_tables(S):
    half = ROPE_DIM // 2
    inv_freq = jnp.power(jnp.float32(ROPE_THETA), -jnp.arange(half, dtype=F32) / half)
    ang = jnp.arange(S, dtype=F32)[:, None] * inv_freq[None, :]
    cos, sin = jnp.cos(ang), jnp.sin(ang)
    rest = HEAD_DIM - ROPE_DIM
    cf = jnp.concatenate([cos, cos, jnp.ones((S, rest), F32)], axis=1)
    sf = jnp.concatenate([-sin, sin, jnp.zeros((S, rest), F32)], axis=1)
    return cf, sf


def _swap_matrix():
    half = ROPE_DIM // 2
    a = lax.broadcasted_iota(jnp.int32, (HEAD_DIM, HEAD_DIM), 0)
    e = lax.broadcasted_iota(jnp.int32, (HEAD_DIM, HEAD_DIM), 1)
    hit = ((e < half) & (a == e + half)) | ((e >= half) & (e < 2 * half) & (a == e - half))
    return jnp.where(hit, 1.0, 0.0).astype(MXU)


def _b_qk_fwd(proj, tables, gains, name, comm=None):
    M = proj.shape[0]
    nsl = 2 * len(B_DILATIONS) * B_HEADS
    Wqk = nsl * HEAD_DIM
    tm = min(M, 256)

    def body(p_ref, cf_ref, sf_ref, g_ref, o_ref):
        cf, sf = cf_ref[...], sf_ref[...]
        swap = _swap_matrix()
        for j in range(nsl):
            cols = slice(j * HEAD_DIM, (j + 1) * HEAD_DIM)
            xv = p_ref[:, cols].astype(F32)
            r = lax.rsqrt(jnp.mean(xv * xv, axis=-1, keepdims=True) + EPS)
            xg = xv * g_ref[j // B_HEADS:j // B_HEADS + 1, :]
            hi = xg.astype(MXU)
            lo = (xg - hi.astype(F32)).astype(MXU)
            sw = jnp.dot(hi, swap, preferred_element_type=F32) + jnp.dot(lo, swap, preferred_element_type=F32)
            o_ref[:, cols] = (r * (xg * cf + sw * sf)).astype(o_ref.dtype)

    tspec = pl.BlockSpec((tm, HEAD_DIM), lambda i: (i, 0))
    return _launch(
        body, name=name, grid=(M // tm,),
        in_specs=[pl.BlockSpec((tm, Wqk), lambda i: (i, 0)), tspec, tspec,
                  pl.BlockSpec((8, HEAD_DIM), lambda i: (0, 0))],
        out_specs=[pl.BlockSpec((tm, Wqk), lambda i: (i, 0))],
        out_shape=[jax.ShapeDtypeStruct((M, Wqk), ACT)],
        args=(proj, *tables, gains), sem=("parallel",), comm=comm)


def _b_qk_bwd(dqs, dks, proj, tables, gains, dproj, name):
    M = proj.shape[0]
    ngr = len(B_DILATIONS)
    nsl = 2 * ngr * B_HEADS
    Wqk = nsl * HEAD_DIM
    Wg = B_HEADS * HEAD_DIM
    tm = min(M, 256)

    def body(*refs):
        d_refs = refs[:2 * ngr]
        p_ref, cf_ref, sf_ref, g_ref = refs[2 * ngr:2 * ngr + 4]
        dp_ref, dg_ref = refs[-2], refs[-1]

        @pl.when(pl.program_id(0) == 0)
        def _():
            dg_ref[...] = jnp.zeros_like(dg_ref)

        cf, sf = cf_ref[...], sf_ref[...]
        swap = _swap_matrix()
        for j in range(nsl):
            t, hh = j // B_HEADS, j % B_HEADS
            cols = slice(j * HEAD_DIM, (j + 1) * HEAD_DIM)
            dy = d_refs[t][:, hh * HEAD_DIM:(hh + 1) * HEAD_DIM].astype(F32)
            dxn = dy * cf + jnp.dot((dy * sf).astype(MXU), swap, preferred_element_type=F32)
            xv = p_ref[:, cols].astype(F32)
            r = lax.rsqrt(jnp.mean(xv * xv, axis=-1, keepdims=True) + EPS)
            xh = xv * r
            dg_ref[t:t + 1, :] += jnp.sum(dxn * xh, axis=0, keepdims=True)
            dxh = dxn * g_ref[t:t + 1, :]
            dp_ref[:, cols] = (r * (dxh - xh * jnp.mean(dxh * xh, axis=-1, keepdims=True))).astype(dp_ref.dtype)

    tspec = pl.BlockSpec((tm, HEAD_DIM), lambda i: (i, 0))
    dspec = pl.BlockSpec((tm, Wg), lambda i: (i, 0))
    n_in = 2 * ngr + 5
    return pl.pallas_call(
        body, name=name, grid=(M // tm,),
        in_specs=[dspec] * (2 * ngr) + [pl.BlockSpec((tm, Wqk), lambda i: (i, 0)), tspec, tspec,
                                        pl.BlockSpec((8, HEAD_DIM), lambda i: (0, 0)),
                                        pl.BlockSpec(memory_space=pl.ANY)],
        out_specs=[pl.BlockSpec((tm, Wqk), lambda i: (i, 0)), pl.BlockSpec((8, HEAD_DIM), lambda i: (0, 0))],
        out_shape=[jax.ShapeDtypeStruct(dproj.shape, dproj.dtype), jax.ShapeDtypeStruct((8, HEAD_DIM), F32)],
        input_output_aliases={n_in - 1: 0},
        compiler_params=_cp("arbitrary"),
    )(*dqs, *dks, proj, *tables, gains, dproj)


def _attn_tile(D, M):
    return max(HEAD_DIM * D, min(M, 2048))


class _TokenRows:
    GROUP = 16

    def __init__(self, D):
        self.D = D
        self.pitch = 24 if D == 16 else self.GROUP

    def rows(self, ntok):
        return ntok // self.GROUP * self.pitch

    def every_dth(self, tok0, n):
        start = tok0 // self.GROUP * self.pitch + tok0 % self.GROUP
        stride = self.D * self.pitch // self.GROUP
        return pl.ds(start, n) if stride == 1 else pl.ds(start, n, stride=stride)

    def put(self, dst, tok0, src_ref, ntok):
        if self.pitch == self.GROUP:
            dst[tok0:tok0 + ntok, :] = src_ref[...].astype(F32)
            return

        def group(i, carry):
            row = pl.multiple_of((tok0 // self.GROUP + i) * self.pitch, 8)
            dst[pl.ds(row, self.GROUP), :] = src_ref[pl.ds(pl.multiple_of(i * self.GROUP, self.GROUP), self.GROUP), :].astype(F32)
            return carry

        lax.fori_loop(0, ntok // self.GROUP, group, 0, unroll=8)

    def get(self, dst_ref, src, ntok):
        if self.pitch == self.GROUP:
            dst_ref[...] = src[0:ntok, :].astype(dst_ref.dtype)
            return

        def group(i, carry):
            row = pl.multiple_of(i * self.pitch, 8)
            dst_ref[pl.ds(pl.multiple_of(i * self.GROUP, self.GROUP), self.GROUP), :] = src[pl.ds(row, self.GROUP), :].astype(dst_ref.dtype)
            return carry

        lax.fori_loop(0, ntok // self.GROUP, group, 0, unroll=8)


def _attn_mask(base):
    qi = lax.broadcasted_iota(jnp.int32, (CHUNK, 2 * CHUNK), 0)
    ki = lax.broadcasted_iota(jnp.int32, (CHUNK, 2 * CHUNK), 1)
    return (ki >= qi) & (ki <= qi + CHUNK) & (ki >= CHUNK - base)


def _b_attn_fwd(qk, proj, g, name):
    M = qk.shape[0]
    D = B_DILATIONS[g]
    ngr = len(B_DILATIONS)
    T = _attn_tile(D, M)
    P = HEAD_DIM * D
    nsb = T // P
    Wg = B_HEADS * HEAD_DIM
    scale = np.float32(1.0 / np.sqrt(HEAD_DIM))

    lay = _TokenRows(D)
    RP, RT = lay.rows(P), lay.rows(T)

    def body(q_ref, k_ref, v_ref, o_ref, l_ref, qs, ks, vs, os_):
        n = pl.program_id(1)

        @pl.when(n == 0)
        def _():
            ks[0:RP, :] = jnp.zeros((RP, HEAD_DIM), F32)
            vs[0:RP, :] = jnp.zeros((RP, HEAD_DIM), F32)

        lay.put(qs, 0, q_ref, T)
        lay.put(ks, P, k_ref, T)
        lay.put(vs, P, v_ref, T)

        for b in range(nsb):
            mask = _attn_mask(n * (T // D) + b * CHUNK)
            for r in range(D):
                start = b * P + r
                q = qs[lay.every_dth(start, CHUNK), :]
                k = ks[lay.every_dth(start, 2 * CHUNK), :]
                v = vs[lay.every_dth(start, 2 * CHUNK), :]
                s = jnp.where(mask, _dot_nt(q, k) * scale, NEG)
                m = jnp.max(s, axis=-1, keepdims=True)
                p = jnp.exp(s - m)
                l = jnp.sum(p, axis=-1, keepdims=True)
                o = _dot(p, v) / l
                os_[lay.every_dth(start, CHUNK), :] = o
                l_ref[:, b * D + r:b * D + r + 1] = m + jnp.log(l)

        lay.get(o_ref, os_, T)
        ks[0:RP, :] = ks[RT:RT + RP, :]
        vs[0:RP, :] = vs[RT:RT + RP, :]

    blk = (T, HEAD_DIM)
    U = nsb * D
    return pl.pallas_call(
        body, name=name, grid=(B_HEADS, M // T),
        in_specs=[pl.BlockSpec(blk, lambda h, n: (n, g * B_HEADS + h)),
                  pl.BlockSpec(blk, lambda h, n: (n, (ngr + g) * B_HEADS + h)),
                  pl.BlockSpec(blk, lambda h, n: (n, (2 * ngr + g) * B_HEADS + h))],
        out_specs=[pl.BlockSpec(blk, lambda h, n: (n, h)), pl.BlockSpec((None, CHUNK, U), lambda h, n: (h, n, 0))],
        out_shape=[jax.ShapeDtypeStruct((M, Wg), ACT), jax.ShapeDtypeStruct((B_HEADS, (M // T) * CHUNK, U), F32)],
        scratch_shapes=[pltpu.VMEM((RT, HEAD_DIM), F32), pltpu.VMEM((RP + RT, HEAD_DIM), F32),
                        pltpu.VMEM((RP + RT, HEAD_DIM), F32), pltpu.VMEM((RT, HEAD_DIM), F32)],
        compiler_params=_cp("parallel", "arbitrary"),
    )(qk, qk, proj)


def _units_to_tokens(a, D, T):
    H = a.shape[0]
    nsb = T // (HEAD_DIM * D)
    return a.reshape(H, -1, CHUNK, nsb, D).transpose(1, 3, 2, 4, 0).reshape(-1, H)


def _tokens_to_units(a, D, T):
    M, H = a.shape
    nsb = T // (HEAD_DIM * D)
    return a.reshape(M // T, nsb, CHUNK, D, H).transpose(4, 0, 2, 1, 3).reshape(H, (M // T) * CHUNK, nsb * D)


def _b_combine(os_, ls, proj, name):
    M, Wg = os_[0].shape
    ngr = len(B_DILATIONS)
    tm = min(M, 512)

    def body(*refs):
        o_refs, l_refs, z_ref = refs[:ngr], refs[ngr:2 * ngr], refs[2 * ngr]
        y_ref, o_ref, lse_ref = refs[2 * ngr + 1:]
        for h in range(B_HEADS):
            cols = slice(h * HEAD_DIM, (h + 1) * HEAD_DIM)
            ls_ = [r[:, h:h + 1] for r in l_refs]
            m = functools.reduce(jnp.maximum, ls_)
            es = [jnp.exp(l - m) for l in ls_]
            tot = functools.reduce(lambda a, b: a + b, es)
            o = functools.reduce(lambda a, b: a + b, [(e / tot) * r[:, cols].astype(F32) for e, r in zip(es, o_refs)])
            z = z_ref[:, cols].astype(F32)
            y_ref[:, cols] = (o * (z * _sigmoid(z))).astype(y_ref.dtype)
            o_ref[:, cols] = o.astype(o_ref.dtype)
            lse_ref[:, h:h + 1] = m + jnp.log(tot)

    spec = pl.BlockSpec((tm, Wg), lambda i: (i, 0))
    hspec = pl.BlockSpec((tm, B_HEADS), lambda i: (i, 0))
    return pl.pallas_call(
        body, name=name, grid=(M // tm,),
        in_specs=[spec] * ngr + [hspec] * ngr + [pl.BlockSpec((tm, Wg), lambda i: (i, 3 * ngr))],
        out_specs=[spec, spec, hspec],
        out_shape=[jax.ShapeDtypeStruct((M, Wg), ACT), jax.ShapeDtypeStruct((M, Wg), ACT),
                   jax.ShapeDtypeStruct((M, B_HEADS), F32)],
        compiler_params=_cp("parallel"),
    )(*os_, *ls, proj)


def _b_bwd_pre(dout, w_out, o, proj, name):
    M, Wg = o.shape
    Dm = dout.shape[1]
    ngr = len(B_DILATIONS)
    tm = min(M, 512)

    def body(do_ref, wo_ref, o_ref, z_ref, dov_ref, dl_ref, dp_ref):
        dy = _dot_nt(do_ref[...], wo_ref[...])
        z = z_ref[...].astype(F32)
        sig = _sigmoid(z)
        ov = o_ref[...].astype(F32)
        dp_ref[...] = (dy * ov * (sig * (1.0 + z * (1.0 - sig)))).astype(dp_ref.dtype)
        dov = dy * (z * sig)
        dov_ref[...] = dov.astype(dov_ref.dtype)
        prod = dov * ov
        for h in range(B_HEADS):
            dl_ref[:, h:h + 1] = jnp.sum(prod[:, h * HEAD_DIM:(h + 1) * HEAD_DIM], axis=-1, keepdims=True)

    spec = pl.BlockSpec((tm, Wg), lambda i: (i, 0))
    zspec = pl.BlockSpec((tm, Wg), lambda i: (i, 3 * ngr))
    return pl.pallas_call(
        body, name=name, grid=(M // tm,),
        in_specs=[pl.BlockSpec((tm, Dm), lambda i: (i, 0)), pl.BlockSpec((Wg, Dm), lambda i: (0, 0)), spec, zspec],
        out_specs=[spec, pl.BlockSpec((tm, B_HEADS), lambda i: (i, 0)), zspec],
        out_shape=[jax.ShapeDtypeStruct((M, Wg), ACT), jax.ShapeDtypeStruct((M, B_HEADS), F32),
                   jax.ShapeDtypeStruct(proj.shape, ACT)],
        compiler_params=_cp("parallel"),
    )(dout, w_out, o, proj)


def _b_attn_bwd(qk, proj, dov, lse, delta, dproj, g, name, comm=None):
    M = qk.shape[0]
    D = B_DILATIONS[g]
    ngr = len(B_DILATIONS)
    T = _attn_tile(D, M)
    P = HEAD_DIM * D
    nsb = T // P
    nt = M // T
    Wg = B_HEADS * HEAD_DIM
    scale = np.float32(1.0 / np.sqrt(HEAD_DIM))
    shift = T - P
    lay = _TokenRows(D)
    RP, RT = lay.rows(P), lay.rows(T)

    def body(q_ref, k_ref, v_ref, do_ref, l_ref, dl_ref, dp_any, dq_ref, dk_ref, dv_ref,
             qs, dos, ks, vs, dqs, dks, dvs):
        n = pl.program_id(1)

        @pl.when(n == 0)
        def _():
            ks[0:RP, :] = jnp.zeros((RP, HEAD_DIM), F32)
            vs[0:RP, :] = jnp.zeros((RP, HEAD_DIM), F32)
            dks[0:RT, :] = jnp.zeros((RT, HEAD_DIM), F32)
            dvs[0:RT, :] = jnp.zeros((RT, HEAD_DIM), F32)

        dks[RT:2 * RT, :] = jnp.zeros((RT, HEAD_DIM), F32)
        dvs[RT:2 * RT, :] = jnp.zeros((RT, HEAD_DIM), F32)

        @pl.when(n < nt)
        def _():
            lay.put(qs, 0, q_ref, T)
            lay.put(dos, 0, do_ref, T)
            lay.put(ks, P, k_ref, T)
            lay.put(vs, P, v_ref, T)

            for b in range(nsb):
                mask = _attn_mask(n * (T // D) + b * CHUNK)
                for r in range(D):
                    start = b * P + r
                    qsl = lay.every_dth(start, CHUNK)
                    ksl = lay.every_dth(start, 2 * CHUNK)
                    dsl = lay.every_dth(start + shift, 2 * CHUNK)
                    q = qs[qsl, :]
                    do = dos[qsl, :]
                    k = ks[ksl, :]
                    v = vs[ksl, :]
                    s = _dot_nt(q, k) * scale
                    u = b * D + r
                    p = jnp.where(mask, jnp.exp(s - l_ref[:, u:u + 1]), 0.0)
                    dvs[dsl, :] += _dot_tn(p, do)
                    dp = _dot_nt(do, v)
                    ds = (p * (dp - dl_ref[:, u:u + 1]) * scale).astype(MXU)
                    dqs[qsl, :] = _dot(ds, k)
                    dks[dsl, :] += _dot_tn(ds, q)

        lay.get(dq_ref, dqs, T)
        lay.get(dk_ref, dks, T)
        lay.get(dv_ref, dvs, T)
        dks[0:RT, :] = dks[RT:2 * RT, :]
        dvs[0:RT, :] = dvs[RT:2 * RT, :]
        ks[0:RP, :] = ks[RT:RT + RP, :]
        vs[0:RP, :] = vs[RT:RT + RP, :]

    blk = (T, HEAD_DIM)
    cur = lambda n: jnp.minimum(n, nt - 1)
    prv = lambda n: jnp.maximum(n - 1, 0)
    return _launch(
        body, name=name, grid=(B_HEADS, nt + 1),
        in_specs=[pl.BlockSpec(blk, lambda h, n: (cur(n), g * B_HEADS + h)),
                  pl.BlockSpec(blk, lambda h, n: (cur(n), (ngr + g) * B_HEADS + h)),
                  pl.BlockSpec(blk, lambda h, n: (cur(n), (2 * ngr + g) * B_HEADS + h)),
                  pl.BlockSpec(blk, lambda h, n: (cur(n), h)),
                  pl.BlockSpec((None, CHUNK, nsb * D), lambda h, n: (h, cur(n), 0)),
                  pl.BlockSpec((None, CHUNK, nsb * D), lambda h, n: (h, cur(n), 0)),
                  pl.BlockSpec(memory_space=pl.ANY)],
        out_specs=[pl.BlockSpec(blk, lambda h, n: (cur(n), h)),
                   pl.BlockSpec(blk, lambda h, n: (prv(n), h)),
                   pl.BlockSpec(blk, lambda h, n: (prv(n), (2 * ngr + g) * B_HEADS + h))],
        out_shape=[jax.ShapeDtypeStruct((M, Wg), ACT), jax.ShapeDtypeStruct((M, Wg), ACT),
                   jax.ShapeDtypeStruct(dproj.shape, dproj.dtype)],
        scratch=[pltpu.VMEM((RT, HEAD_DIM), F32)] * 2
        + [pltpu.VMEM((RP + RT, HEAD_DIM), F32)] * 2
        + [pltpu.VMEM((RT, HEAD_DIM), F32)]
        + [pltpu.VMEM((2 * RT, HEAD_DIM), F32)] * 2,
        aliases={6: 2},
        args=(qk, qk, proj, dov, lse, delta, dproj), sem=("parallel", "arbitrary"), comm=comm)


def _coords():
    return lax.axis_index("x"), lax.axis_index("y"), lax.axis_index("c")


def _gather_blocks(x_refs, out_refs, send_sems, recv_sems, local_sems):
    x, y, c = _coords()
    me, sibling = (x, y, c), (x, y, 1 - c)
    chips = [(1 - x, y), (x, 1 - y), (1 - x, 1 - y)]
    arrays = range(len(x_refs))

    def slot(a, px, py, pc):
        return out_refs[a].at[4 * px + 2 * py + pc]

    def copy(a, k, block, to, src=None):
        return _remote(slot(a, *block) if src is None else src, slot(a, *block), send_sems, recv_sems, 7 * a + k, to)

    mine = [pltpu.make_async_copy(x_refs[a], slot(a, *me), local_sems.at[a]) for a in arrays]
    first = [copy(a, 0, me, sibling, src=x_refs[a]) for a in arrays]
    first += [copy(a, 1 + j, me, (*chip, c), src=x_refs[a]) for j, chip in enumerate(chips) for a in arrays]
    for cp in mine + first:
        cp.start()
    passed = []
    for j, chip in enumerate(chips):
        for a in arrays:
            copy(a, 1 + j, (*chip, c), me).wait_recv()
            passed.append(copy(a, 4 + j, (*chip, c), sibling))
            passed[-1].start()
    for a in arrays:
        copy(a, 0, sibling, me).wait_recv()
        for j, chip in enumerate(chips):
            copy(a, 4 + j, (*chip, 1 - c), me).wait_recv()
    for cp in first + passed:
        cp.wait_send()
    for cp in mine:
        cp.wait()


def _all_gather_hbm(arrays, name):
    n = len(arrays)

    def body(*refs):
        _gather_blocks(refs[:n], refs[n:2 * n], *refs[2 * n:])

    return pl.pallas_call(
        body, name=name, in_specs=[_HBM] * n, out_specs=[_HBM] * n,
        out_shape=[jax.ShapeDtypeStruct((N_DEV,) + a.shape, a.dtype) for a in arrays],
        scratch_shapes=[pltpu.SemaphoreType.DMA((7 * n,)), pltpu.SemaphoreType.DMA((7 * n,)),
                        pltpu.SemaphoreType.DMA((n,))],
    )(*arrays)


def _all_reduce_small(part):
    R, C = part.shape

    def body(x_ref, tot_ref, gath, send_sems, recv_sems, local_sems):
        _gather_blocks([x_ref], [gath], send_sems, recv_sems, local_sems)
        acc = gath[0]
        for d in range(1, N_DEV):
            acc = acc + gath[d]
        tot_ref[...] = acc

    return pl.pallas_call(
        body, name="ar_small",
        in_specs=[pl.BlockSpec(memory_space=pltpu.VMEM)],
        out_specs=pl.BlockSpec(memory_space=pltpu.VMEM),
        out_shape=jax.ShapeDtypeStruct((R, C), F32),
        scratch_shapes=[pltpu.VMEM((N_DEV, R, C), F32),
                        pltpu.SemaphoreType.DMA((7,)), pltpu.SemaphoreType.DMA((7,)), pltpu.SemaphoreType.DMA((1,))],
        compiler_params=pltpu.CompilerParams(vmem_limit_bytes=VMEM_LIMIT),
    )(part)


def _remote(src, dst, send_sems, recv_sems, k, peer):
    return pltpu.make_async_remote_copy(src_ref=src, dst_ref=dst, send_sem=send_sems.at[k], recv_sem=recv_sems.at[k],
                                        device_id=peer, device_id_type=MESH)


def _ag_send(arrays):
    n = len(arrays)

    def make(c_in, c_out, send_sems, recv_sems, local_sems):
        x, y, c = _coords()
        peers = [(x, y, 1 - c), (1 - x, y, c), (x, 1 - y, c), (1 - x, 1 - y, c)]
        cps = []
        for a in range(n):
            src, dst = c_in[a], c_out[a].at[4 * x + 2 * y + c]
            cps.append(pltpu.make_async_copy(src, dst, local_sems.at[a]))
            cps += [_remote(src, dst, send_sems, recv_sems, 4 * a + k, peer) for k, peer in enumerate(peers)]
        return cps

    return _Comm(arrays, [jax.ShapeDtypeStruct((N_DEV,) + a.shape, a.dtype) for a in arrays], 4 * n, make, n_local=n)


def _ag_forward(gaths):
    n = len(gaths)

    def make(c_in, c_out, send_sems, recv_sems, local_sems):
        x, y, c = _coords()
        chips = [(1 - x, y), (x, 1 - y), (1 - x, 1 - y)]
        cps = []
        for a in range(n):
            buf = c_out[a]
            cps += [_remote(buf.at[4 * px + 2 * py + c], buf.at[4 * px + 2 * py + c], send_sems, recv_sems, 3 * a + j,
                            (x, y, 1 - c)) for j, (px, py) in enumerate(chips)]
        return cps

    return _Comm(gaths, [jax.ShapeDtypeStruct(g.shape, g.dtype) for g in gaths], 3 * n, make,
                 aliases={a: a for a in range(n)})


def _rs_sibling(grads):
    n = len(grads)

    def make(c_in, c_out, send_sems, recv_sems, local_sem):
        x, y, c = _coords()
        return [_remote(c_in[a].at[pl.ds(4 * (1 - c), 4)], c_out[a], send_sems, recv_sems, a, (x, y, 1 - c))
                for a in range(n)]

    return _Comm(grads, [jax.ShapeDtypeStruct((4,) + g.shape[1:], g.dtype) for g in grads], n, make)


def _rs_chips(parts):
    n = len(parts)

    def make(c_in, c_out, send_sems, recv_sems, local_sem):
        x, y, c = _coords()
        peers = [(x, 1 - y, c), (1 - x, y, c), (1 - x, 1 - y, c)]
        return [_remote(c_in[a].at[k], c_out[a].at[k], send_sems, recv_sems, 3 * a + k, peer)
                for a in range(n) for k, peer in enumerate(peers)]

    return _Comm(parts, [jax.ShapeDtypeStruct(p.shape, p.dtype) for p in parts], 3 * n, make)


def _row_tile(rows, cols):
    tr = min(rows, 1 << int(np.log2((1 << 18) // cols)))
    assert rows % tr == 0
    return tr


def _chip_partials(coords, g, r1, name):
    _, rows, C = g.shape
    tr = _row_tile(rows, C)

    def body(co_ref, g_ref, r_ref, o_ref):
        o_ref[...] = (g_ref[...] + r_ref[...]).astype(o_ref.dtype)

    def chip(k, co):
        return jnp.bitwise_xor(2 * co[0] + co[1], k + 1)

    return pl.pallas_call(
        body, name=name,
        grid_spec=pltpu.PrefetchScalarGridSpec(
            num_scalar_prefetch=1, grid=(3, rows // tr),
            in_specs=[pl.BlockSpec((None, tr, C), lambda k, t, co: (4 * co[2] + chip(k, co), t, 0)),
                      pl.BlockSpec((None, tr, C), lambda k, t, co: (chip(k, co), t, 0))],
            out_specs=pl.BlockSpec((None, tr, C), lambda k, t, co: (k, t, 0))),
        out_shape=jax.ShapeDtypeStruct((3, rows, C), WIRE),
        compiler_params=_cp("parallel", "parallel"),
    )(coords, g, r1)


def _adam_math(w, g, m, v):
    m = ADAM_B1 * m + (1.0 - ADAM_B1) * g
    v = ADAM_B2 * v + (1.0 - ADAM_B2) * (g * g)
    m_hat = m / (1.0 - ADAM_B1 ** ADAM_STEP)
    v_hat = v / (1.0 - ADAM_B2 ** ADAM_STEP)
    delta = -ADAM_LR * (m_hat / (jnp.sqrt(v_hat) + ADAM_EPS) + ADAM_WD * w)
    return delta, m, v


def _adamw_sharded(coords, w, m, v, g, r1, r2, name):
    rows, C = w.shape
    tr = _row_tile(rows, C)

    def body(co_ref, w_ref, m_ref, v_ref, g_ref, r1_ref, r2_ref, go_ref, d_ref, mo_ref, vo_ref):
        grad = g_ref[...] + r1_ref[...]
        for k in range(3):
            grad = grad + r2_ref[k].astype(F32)
        go_ref[...] = grad
        d_ref[...], mo_ref[...], vo_ref[...] = _adam_math(w_ref[...], grad, m_ref[...], v_ref[...])

    spec = pl.BlockSpec((tr, C), lambda t, co: (t, 0))
    return pl.pallas_call(
        body, name=name,
        grid_spec=pltpu.PrefetchScalarGridSpec(
            num_scalar_prefetch=1, grid=(rows // tr,),
            in_specs=[spec, spec, spec,
                      pl.BlockSpec((None, tr, C), lambda t, co: (4 * co[2] + 2 * co[0] + co[1], t, 0)),
                      pl.BlockSpec((None, tr, C), lambda t, co: (2 * co[0] + co[1], t, 0)),
                      pl.BlockSpec((3, tr, C), lambda t, co: (0, t, 0))],
            out_specs=[spec] * 4),
        out_shape=[jax.ShapeDtypeStruct((rows, C), F32)] * 4,
        compiler_params=_cp("parallel"),
    )(coords, w, m, v, g, r1, r2)


def _adamw_small(w, g, m, v, name):
    def body(w_ref, g_ref, m_ref, v_ref, d_ref, mo_ref, vo_ref):
        d_ref[...], mo_ref[...], vo_ref[...] = _adam_math(w_ref[...], g_ref[...], m_ref[...], v_ref[...])

    return pl.pallas_call(
        body, name=name, out_shape=[jax.ShapeDtypeStruct(w.shape, F32)] * 3,
        in_specs=[pl.BlockSpec(memory_space=pltpu.VMEM)] * 4,
        out_specs=[pl.BlockSpec(memory_space=pltpu.VMEM)] * 3,
    )(w, g, m, v)


def _reduce_scatter_adds(coords, grads, r1s, tag):
    return [_chip_partials(coords, g, r, f"rs_add_{tag}{i}") for i, (g, r) in enumerate(zip(grads, r1s))]


def kernel(x, norm_gain, a_w_in, a_v_gain, a_w_s, a_b_s, a_w_out, b_w_in, b_q_gain, b_k_gain, b_w_out, c_w_in, c_w_grp, c_scale, c_w_out, loss_target, m_norm_gain, m_a_w_in, m_a_v_gain, m_a_w_s, m_a_b_s, m_a_w_out, m_b_w_in, m_b_q_gain, m_b_k_gain, m_b_w_out, m_c_w_in, m_c_w_grp, m_c_scale, m_c_w_out, v_norm_gain, v_a_w_in, v_a_v_gain, v_a_w_s, v_a_b_s, v_a_w_out, v_b_w_in, v_b_q_gain, v_b_k_gain, v_b_w_out, v_c_w_in, v_c_w_grp, v_c_scale, v_c_w_out):
    cx, cy, cc = _coords()
    coords = jnp.stack([cx, cy, cc]).astype(jnp.int32)
    dev = 4 * cx + 2 * cy + cc
    Dm = x.shape[2]

    xs, tgt = x[0], loss_target[0]
    tables = _rope_tables(xs.shape[0])
    ng = lambda i: norm_gain[i:i + 1]
    ngr = len(B_DILATIONS)
    bst = [a_b_s[l].T for l in range(2)]
    b_gains = jnp.concatenate([b_q_gain[0], b_k_gain[0], jnp.zeros((2, HEAD_DIM), F32)], axis=0)
    nla, nlb, nlc = a_w_in.shape[2], b_w_in.shape[2], c_w_in.shape[2]
    ngp, rlc, cgc = c_w_grp.shape[1:]
    wire = lambda w: w.astype(WIRE)

    nvg, nsc = a_v_gain.size, c_scale.size
    vec = jnp.concatenate([a_v_gain.reshape(-1), c_scale.reshape(-1), jnp.zeros((1024 - nvg - nsc,), F32)]).reshape(8, 128)
    wa_in0, wa_out0, vecs = _all_gather_hbm([wire(a_w_in[0]), wire(a_w_out[0]), vec], "ag_layer0")
    wa_out0 = wa_out0.reshape(-1, Dm)
    vecs = vecs.reshape(N_DEV, -1)
    a_vg = vecs[:, :nvg].reshape((N_DEV,) + a_v_gain.shape).transpose(1, 0, 2).reshape(a_v_gain.shape[0], -1)
    c_sc = vecs[:, nvg:nvg + nsc].reshape(1, -1)

    h0, p0, *g1 = _norm_proj(xs, ng(0), wa_in0, "l0_proj", comm=_ag_send([wire(b_w_in[0]), wire(b_w_out[0])]))
    y0, wb_in, wb_out = _a_mid(p0, a_vg[0:1], a_w_s[0], bst[0], "l0_mid", comm=_ag_forward(g1))
    x1 = _out_proj(xs, y0, wa_out0, "l0_out")
    wb_out = wb_out.reshape(-1, Dm)

    later = [wire(c_w_in[0]), wire(c_w_grp[0]), wire(c_w_out[0]), wire(a_w_in[1]), wire(a_w_out[1])]
    h1, p1, *g2 = _norm_proj(x1, ng(1), wb_in, "l1_proj", comm=_ag_send(later))
    qk, wc_in, wc_grp, wc_out, wa_in1, wa_out1 = _b_qk_fwd(p1, tables, b_gains, "l1_qk", comm=_ag_forward(g2))
    ogs, lgs = zip(*[_b_attn_fwd(qk, p1, g, f"l1_attn{g}") for g in range(ngr)])
    tiles = [_attn_tile(D, xs.shape[0]) for D in B_DILATIONS]
    lgs = [_units_to_tokens(l, D, T) for l, D, T in zip(lgs, B_DILATIONS, tiles)]
    y1, o1, lse = _b_combine(ogs, lgs, p1, "l1_comb")
    x2 = _out_proj(x1, y1, wb_out, "l1_out")
    wc_grp = wc_grp.transpose(1, 0, 2, 3).reshape(ngp, N_DEV * rlc, cgc)
    wc_out = wc_out.reshape(-1, Dm)
    wa_out1 = wa_out1.reshape(-1, Dm)

    h2, p2 = _norm_proj(x2, ng(2), wc_in, "l2_proj")
    y2 = _c_mid(p2, wc_grp, c_sc, "l2_mid")
    x3 = _out_proj(x2, y2, wc_out, "l2_out")
    h3, p3 = _norm_proj(x3, ng(3), wa_in1, "l3_proj")
    y3, = _a_mid(p3, a_vg[1:2], a_w_s[1], bst[1], "l3_mid")
    x4 = _out_proj(x3, y3, wa_out1, "l3_out")
    loss_local, dx4, dx4a = _loss_head(x4, tgt)
    loss = lax.psum(loss_local, ("x", "y", "c"))

    flat3 = lambda g: g.reshape(N_DEV, -1, g.shape[-1])
    dp3, dws1, dbs1, dvg1 = _a_bwd(dx4a, wa_out1, p3, a_vg[1:2], a_w_s[1], bst[1], "l3_bwd")
    grads3 = [_dw_in(h3, dp3, "l3_dwin"), _dw_out(y3, dx4a, "l3_dwout")]
    dx3, dx3a, dg3, *r1_3 = _dh_norm_bwd(dp3, wa_in1, x3, ng(3), dx4, "l3_dh", comm=_rs_sibling(grads3))
    parts3 = _reduce_scatter_adds(coords, grads3, r1_3, "l3_")

    dd, dz, gc_grp, dsc, *r2_3 = _c_bwd1(dx3a, wc_out, p2, wc_grp, c_sc, "l2_bwd1", comm=_rs_chips(parts3))
    dp2 = _c_bwd2(dd, dz, "l2_bwd2")
    grads2 = [_dw_in(h2, dp2, "l2_dwin"), _dw_out(y2, dx3a, "l2_dwout"), flat3(gc_grp)]
    dx2, dx2a, dg2, *r1_2 = _dh_norm_bwd(dp2, wc_in, x2, ng(2), dx3, "l2_dh", comm=_rs_sibling(grads2))
    parts2 = _reduce_scatter_adds(coords, grads2, r1_2, "l2_")

    dov, delta, dp1 = _b_bwd_pre(dx2a, wb_out, o1, p1, "l1_bwdpre")
    dqs, dks, r2_2 = [], [], None
    for g in range(ngr):
        lse_u, delta_u = [_tokens_to_units(a, B_DILATIONS[g], tiles[g]) for a in (lse, delta)]
        dq, dk, dp1, *rest = _b_attn_bwd(qk, p1, dov, lse_u, delta_u, dp1, g, f"l1_attnbwd{g}",
                                         comm=_rs_chips(parts2) if g == 0 else None)
        if g == 0:
            r2_2 = rest
        dqs.append(dq)
        dks.append(dk)
    dp1, dgains = _b_qk_bwd(dqs, dks, p1, tables, b_gains, dp1, "l1_qkbwd")
    grads1 = [_dw_in(h1, dp1, "l1_dwin"), _dw_out(y1, dx2a, "l1_dwout")]
    dx1, dx1a, dg1, *r1_1 = _dh_norm_bwd(dp1, wb_in, x1, ng(1), dx2, "l1_dh", comm=_rs_sibling(grads1))
    parts1 = _reduce_scatter_adds(coords, grads1, r1_1, "l1_")

    dp0, dws0, dbs0, dvg0, *r2_1 = _a_bwd(dx1a, wa_out0, p0, a_vg[0:1], a_w_s[0], bst[0], "l0_bwd", comm=_rs_chips(parts1))
    grads0 = [_dw_in(h0, dp0, "l0_dwin"), _dw_out(y0, dx1a, "l0_dwout")]
    r1_0 = _run_comm(_rs_sibling(grads0), "l0_rs_sibling")
    parts0 = _reduce_scatter_adds(coords, grads0, r1_0, "l0_")
    dx0, _, dg0, *r2_0 = _dh_norm_bwd(dp0, wa_in0, xs, ng(0), dx1, "l0_dh", comm=_rs_chips(parts0))

    small = dict(norm=jnp.concatenate([dg0, dg1, dg2, dg3], axis=0), a_ws=jnp.stack([dws0, dws1]),
                 a_bs=jnp.stack([dbs0.T, dbs1.T]), b_gains=dgains, a_vg=jnp.concatenate([dvg0, dvg1], axis=0), c_sc=dsc)

    order = ["norm", "a_ws", "a_bs", "b_gains", "a_vg", "c_sc"]
    rows = [small[k].reshape(-1, 128) for k in order]
    roff = np.cumsum([0] + [r.shape[0] for r in rows])
    tot = _all_reduce_small(jnp.concatenate(rows, axis=0))
    sm = {k: tot[int(roff[i]):int(roff[i + 1])].reshape(small[k].shape) for i, k in enumerate(order)}
    vl = a_v_gain.shape[1]
    g_small = dict(
        norm_gain=sm["norm"], a_w_s=sm["a_ws"], a_b_s=sm["a_bs"],
        b_q_gain=sm["b_gains"][None, 0:3], b_k_gain=sm["b_gains"][None, 3:6],
        a_v_gain=lax.dynamic_slice_in_dim(sm["a_vg"], dev * vl, vl, axis=1),
        c_scale=lax.dynamic_slice_in_dim(sm["c_sc"], dev * vl, vl, axis=1),
    )

    shares = dict(
        a_w_in=[(grads0[0], r1_0[0], r2_0[0]), (grads3[0], r1_3[0], r2_3[0])],
        a_w_out=[(grads0[1], r1_0[1], r2_0[1]), (grads3[1], r1_3[1], r2_3[1])],
        b_w_in=[(grads1[0], r1_1[0], r2_1[0])], b_w_out=[(grads1[1], r1_1[1], r2_1[1])],
        c_w_in=[(grads2[0], r1_2[0], r2_2[0])], c_w_out=[(grads2[1], r1_2[1], r2_2[1])],
        c_w_grp=[(grads2[2], r1_2[2], r2_2[2])])

    params = dict(a_w_in=a_w_in, a_w_out=a_w_out, b_w_in=b_w_in, b_w_out=b_w_out, c_w_in=c_w_in, c_w_grp=c_w_grp, c_w_out=c_w_out,
                  norm_gain=norm_gain, a_v_gain=a_v_gain, a_w_s=a_w_s, a_b_s=a_b_s, b_q_gain=b_q_gain, b_k_gain=b_k_gain, c_scale=c_scale)
    moms = dict(a_w_in=(m_a_w_in, v_a_w_in), a_w_out=(m_a_w_out, v_a_w_out), b_w_in=(m_b_w_in, v_b_w_in), b_w_out=(m_b_w_out, v_b_w_out),
                c_w_in=(m_c_w_in, v_c_w_in), c_w_grp=(m_c_w_grp, v_c_w_grp), c_w_out=(m_c_w_out, v_c_w_out),
                norm_gain=(m_norm_gain, v_norm_gain), a_v_gain=(m_a_v_gain, v_a_v_gain), a_w_s=(m_a_w_s, v_a_w_s),
                a_b_s=(m_a_b_s, v_a_b_s), b_q_gain=(m_b_q_gain, v_b_q_gain), b_k_gain=(m_b_k_gain, v_b_k_gain),
                c_scale=(m_c_scale, v_c_scale))
    grad, delta, new_m, new_v = {}, {}, {}, {}
    for pname, layers in shares.items():
        w, (m, v) = params[pname], moms[pname]
        C = w.shape[-1]
        per_layer = [_adamw_sharded(coords, w[l].reshape(-1, C), m[l].reshape(-1, C), v[l].reshape(-1, C), g, r1, r2,
                                    f"adamw_{pname}{l}") for l, (g, r1, r2) in enumerate(layers)]
        grad[pname], delta[pname], new_m[pname], new_v[pname] = [
            jnp.stack([o.reshape(w.shape[1:]) for o in outs]) for outs in zip(*per_layer)]
    for pname, g in g_small.items():
        w = params[pname]
        C = w.shape[-1]
        outs = _adamw_small(w.reshape(-1, C), g.reshape(-1, C), moms[pname][0].reshape(-1, C), moms[pname][1].reshape(-1, C),
                            f"adamw_{pname}")
        grad[pname] = g.reshape(w.shape)
        delta[pname], new_m[pname], new_v[pname] = [o.reshape(w.shape) for o in outs]

    wnames = ["norm_gain", "a_w_in", "a_v_gain", "a_w_s", "a_b_s", "a_w_out", "b_w_in", "b_q_gain", "b_k_gain", "b_w_out",
              "c_w_in", "c_w_grp", "c_scale", "c_w_out"]
    return (loss, dx0[None], *[grad[n] for n in wnames], *[delta[n] for n in wnames],
            *[new_m[n] for n in wnames], *[new_v[n] for n in wnames])
```

```python
import functools

import numpy as np
import jax
import jax.numpy as jnp
from jax import lax
from jax.experimental import pallas as pl
from jax.experimental.pallas import tpu as pltpu

F32 = jnp.float32
MXU = jnp.bfloat16
ACT = jnp.bfloat16
WIRE = jnp.bfloat16

EPS = 1e-6
CHUNK = 128
A_GROUPS = 8
HEAD_DIM = 128
B_HEADS = 8
B_DILATIONS = (1, 4, 16)
ROPE_DIM = 32
ROPE_THETA = 500000.0
POOL_SIZES = (2, 4, 8, 16)
POOL_HALO = 16
N_DEV = 8
NEG = -1e30

ADAM_LR, ADAM_B1, ADAM_B2, ADAM_EPS, ADAM_WD, ADAM_STEP = 0.001, 0.9, 0.999, 1e-08, 0.01, 10

VMEM_LIMIT = 56 * 1024 * 1024
MESH = pl.DeviceIdType.MESH


def _cp(*sem):
    return pltpu.CompilerParams(dimension_semantics=sem, vmem_limit_bytes=VMEM_LIMIT)


def _sigmoid(z):
    return 1.0 / (1.0 + jnp.exp(-z))


def _dot(a, b):
    return jnp.dot(a.astype(MXU), b.astype(MXU), preferred_element_type=F32)


def _dot_nt(a, b):
    return lax.dot_general(a.astype(MXU), b.astype(MXU), (((1,), (1,)), ((), ())), preferred_element_type=F32)


def _dot_tn(a, b):
    return lax.dot_general(a.astype(MXU), b.astype(MXU), (((0,), (0,)), ((), ())), preferred_element_type=F32)


def _chunk_slot(d):
    return (d % 2) * 4 + d // 2


class _Comm:
    def __init__(self, inputs, out_shapes, n_remote, make, aliases=None, n_local=1):
        self.inputs = list(inputs)
        self.out_shapes = list(out_shapes)
        self.n_remote = n_remote
        self.n_local = n_local
        self.make = make
        self.aliases = dict(aliases or {})

    def sems(self):
        return [pltpu.SemaphoreType.DMA((self.n_remote,)), pltpu.SemaphoreType.DMA((self.n_remote,)),
                pltpu.SemaphoreType.DMA((self.n_local,))]


_HBM = pl.BlockSpec(memory_space=pl.ANY)


def _launch(body, *, name, grid, in_specs, out_specs, out_shape, args, sem, scratch=(), aliases=None, comm=None):
    in_specs, out_specs, out_shape, scratch = list(in_specs), list(out_specs), list(out_shape), list(scratch)
    aliases = dict(aliases or {})
    if comm is None:
        return pl.pallas_call(body, name=name, grid=grid, in_specs=in_specs, out_specs=out_specs, out_shape=out_shape,
                              scratch_shapes=scratch, input_output_aliases=aliases, compiler_params=_cp(*sem))(*args)
    n_in, n_out, n_sc = len(in_specs), len(out_specs), len(scratch)
    nci, nco = len(comm.inputs), len(comm.out_shapes)

    def hosted(*refs):
        b_in, c_in = refs[:n_in], refs[n_in:n_in + nci]
        o0 = n_in + nci
        b_out, c_out = refs[o0:o0 + n_out], refs[o0 + n_out:o0 + n_out + nco]
        s0 = o0 + n_out + nco
        b_sc, sems = refs[s0:s0 + n_sc], refs[s0 + n_sc:]
        ids = [pl.program_id(a) for a in range(len(grid))]
        first = functools.reduce(jnp.logical_and, [i == 0 for i in ids])
        last = functools.reduce(jnp.logical_and, [i == g - 1 for i, g in zip(ids, grid)])

        @pl.when(first)
        def _():
            for cp in comm.make(c_in, c_out, *sems):
                cp.start()

        body(*b_in, *b_out, *b_sc)

        @pl.when(last)
        def _():
            for cp in comm.make(c_in, c_out, *sems):
                cp.wait()

    for ci, co in comm.aliases.items():
        aliases[n_in + ci] = n_out + co
    return pl.pallas_call(
        hosted, name=name, grid=grid, in_specs=in_specs + [_HBM] * nci, out_specs=out_specs + [_HBM] * nco,
        out_shape=out_shape + comm.out_shapes, scratch_shapes=scratch + comm.sems(),
        input_output_aliases=aliases, compiler_params=_cp(*["arbitrary"] * len(grid)))(*args, *comm.inputs)


def _run_comm(comm, name):
    nci, nco = len(comm.inputs), len(comm.out_shapes)

    def body(*refs):
        cps = comm.make(refs[:nci], refs[nci:nci + nco], *refs[nci + nco:])
        for cp in cps:
            cp.start()
        for cp in cps:
            cp.wait()

    return pl.pallas_call(
        body, name=name, in_specs=[_HBM] * nci, out_specs=[_HBM] * nco, out_shape=comm.out_shapes,
        scratch_shapes=comm.sems(), input_output_aliases=dict(comm.aliases))(*comm.inputs)


def _norm_proj(x, gain, w_dm, name, comm=None):
    M, Dm = x.shape
    nd, _, nl = w_dm.shape
    tm = min(M, 2048)

    def body(x_ref, g_ref, w_ref, h_ref, p_ref):
        @pl.when(pl.program_id(1) == 0)
        def _():
            xv = x_ref[...]
            r = lax.rsqrt(jnp.mean(xv * xv, axis=-1, keepdims=True) + EPS)
            h_ref[...] = (xv * r * g_ref[...]).astype(h_ref.dtype)

        p_ref[...] = _dot(h_ref[...], w_ref[...]).astype(p_ref.dtype)

    return _launch(
        body, name=name, grid=(M // tm, nd),
        in_specs=[pl.BlockSpec((tm, Dm), lambda i, j: (i, 0)),
                  pl.BlockSpec((1, Dm), lambda i, j: (0, 0)),
                  pl.BlockSpec((None, Dm, nl), lambda i, j: (j, 0, 0))],
        out_specs=[pl.BlockSpec((tm, Dm), lambda i, j: (i, 0)),
                   pl.BlockSpec((tm, nl), lambda i, j: (i, j))],
        out_shape=[jax.ShapeDtypeStruct((M, Dm), ACT), jax.ShapeDtypeStruct((M, nd * nl), ACT)],
        args=(x, gain, w_dm), sem=("parallel", "arbitrary"), comm=comm)


def _out_proj(x, y, w, name):
    M, Dm = x.shape
    K = y.shape[1]
    tm = min(M, 512)

    def body(x_ref, y_ref, w_ref, o_ref):
        o_ref[...] = x_ref[...] + _dot(y_ref[...], w_ref[...])

    return pl.pallas_call(
        body, name=name, grid=(M // tm,),
        in_specs=[pl.BlockSpec((tm, Dm), lambda i: (i, 0)),
                  pl.BlockSpec((tm, K), lambda i: (i, 0)),
                  pl.BlockSpec((K, Dm), lambda i: (0, 0))],
        out_specs=pl.BlockSpec((tm, Dm), lambda i: (i, 0)),
        out_shape=jax.ShapeDtypeStruct((M, Dm), F32),
        compiler_params=_cp("parallel"),
    )(x, y, w)


def _loss_head(xf, target):
    M, Dm = xf.shape
    tm = min(M, 512)

    def body(x_ref, t_ref, dx_ref, dxa_ref, l_ref):
        @pl.when(pl.program_id(0) == 0)
        def _():
            l_ref[...] = jnp.zeros_like(l_ref)

        err = x_ref[...] - t_ref[...]
        dx = err * (1.0 / Dm)
        dx_ref[...] = dx
        dxa_ref[...] = dx.astype(dxa_ref.dtype)
        l_ref[...] += jnp.sum(err * err) * (0.5 / Dm)

    spec = pl.BlockSpec((tm, Dm), lambda i: (i, 0))
    dx, dxa, l = pl.pallas_call(
        body, name="loss_head", grid=(M // tm,),
        in_specs=[spec] * 2,
        out_specs=[spec, spec, pl.BlockSpec((8, 128), lambda i: (0, 0))],
        out_shape=[jax.ShapeDtypeStruct((M, Dm), F32), jax.ShapeDtypeStruct((M, Dm), ACT),
                   jax.ShapeDtypeStruct((8, 128), F32)],
        compiler_params=_cp("arbitrary"),
    )(xf, target)
    return l[0, 0], dx, dxa


def _dw_in(h, dproj, name):
    M, Dm = h.shape
    nl = dproj.shape[1] // N_DEV
    tt = min(M, 2048)

    def body(a_ref, b_ref, o_ref):
        @pl.when(pl.program_id(1) == 0)
        def _():
            o_ref[...] = jnp.zeros_like(o_ref)

        o_ref[...] += _dot_tn(a_ref[...], b_ref[...])

    return pl.pallas_call(
        body, name=name, grid=(N_DEV, M // tt),
        in_specs=[pl.BlockSpec((tt, Dm), lambda j, t: (t, 0)), pl.BlockSpec((tt, nl), lambda j, t: (t, j))],
        out_specs=pl.BlockSpec((None, Dm, nl), lambda j, t: (_chunk_slot(j), 0, 0)),
        out_shape=jax.ShapeDtypeStruct((N_DEV, Dm, nl), F32),
        compiler_params=_cp("parallel", "arbitrary"),
    )(h, dproj)


def _dw_out(y, dout, name):
    M, K = y.shape
    Dm = dout.shape[1]
    kl = K // N_DEV
    tt = min(M, 512)

    def body(a_ref, b_ref, o_ref):
        @pl.when(pl.program_id(0) == 0)
        def _():
            o_ref[...] = jnp.zeros_like(o_ref)

        b = b_ref[...]
        for j in range(N_DEV):
            o_ref[_chunk_slot(j)] += _dot_tn(a_ref[:, j * kl:(j + 1) * kl], b)

    return pl.pallas_call(
        body, name=name, grid=(M // tt,),
        in_specs=[pl.BlockSpec((tt, K), lambda t: (t, 0)), pl.BlockSpec((tt, Dm), lambda t: (t, 0))],
        out_specs=pl.BlockSpec((N_DEV, kl, Dm), lambda t: (0, 0, 0)),
        out_shape=jax.ShapeDtypeStruct((N_DEV, kl, Dm), F32),
        compiler_params=_cp("arbitrary"),
    )(y, dout)


def _dh_norm_bwd(dproj, w_dm, x, gain, dres, name, comm=None):
    M, Dm = x.shape
    nd, _, nl = w_dm.shape
    tm = min(M, 1024)

    def body(dp_ref, w_ref, x_ref, g_ref, dr_ref, dx_ref, dxa_ref, dg_ref, acc_ref):
        i, j = pl.program_id(0), pl.program_id(1)

        @pl.when(j == 0)
        def _():
            acc_ref[...] = jnp.zeros_like(acc_ref)

        acc_ref[...] += _dot_nt(dp_ref[...], w_ref[...])

        @pl.when(j == nd - 1)
        def _():
            @pl.when(i == 0)
            def _():
                dg_ref[...] = jnp.zeros_like(dg_ref)

            dh = acc_ref[...]
            xv = x_ref[...]
            r = lax.rsqrt(jnp.mean(xv * xv, axis=-1, keepdims=True) + EPS)
            xn = xv * r
            dg_ref[...] += jnp.sum(dh * xn, axis=0, keepdims=True)
            dxn = dh * g_ref[...]
            dx = dr_ref[...] + r * (dxn - xn * jnp.mean(dxn * xn, axis=-1, keepdims=True))
            dx_ref[...] = dx
            dxa_ref[...] = dx.astype(dxa_ref.dtype)

    row = pl.BlockSpec((tm, Dm), lambda i, j: (i, 0))
    return _launch(
        body, name=name, grid=(M // tm, nd),
        in_specs=[pl.BlockSpec((tm, nl), lambda i, j: (i, j)),
                  pl.BlockSpec((None, Dm, nl), lambda i, j: (j, 0, 0)),
                  row, pl.BlockSpec((1, Dm), lambda i, j: (0, 0)), row],
        out_specs=[row, row, pl.BlockSpec((1, Dm), lambda i, j: (0, 0))],
        out_shape=[jax.ShapeDtypeStruct((M, Dm), F32), jax.ShapeDtypeStruct((M, Dm), ACT),
                   jax.ShapeDtypeStruct((1, Dm), F32)],
        scratch=[pltpu.VMEM((tm, Dm), F32)],
        args=(dproj, w_dm, x, gain, dres), sem=("arbitrary", "arbitrary"), comm=comm)


def _tril_mask():
    return lax.broadcasted_iota(jnp.int32, (CHUNK, CHUNK), 0) >= lax.broadcasted_iota(jnp.int32, (CHUNK, CHUNK), 1)


def _a_mid(proj, v_gain, w_s, b_st, name, comm=None):
    M = proj.shape[0]
    W = proj.shape[1] // 3
    gd = W // A_GROUPS
    tm = min(M, 256)

    def body(p_ref, vg_ref, ws_ref, bs_ref, y_ref):
        pv = p_ref[:, W:2 * W].astype(F32)
        r = lax.rsqrt(jnp.mean(pv * pv, axis=-1, keepdims=True) + EPS)
        v = (pv * r * vg_ref[...]).astype(MXU)
        tri = _tril_mask()
        for g in range(A_GROUPS):
            wg = jnp.where(tri, ws_ref[g], 0.0).astype(MXU)
            bcol = bs_ref[:, g:g + 1]
            for c in range(tm // CHUNK):
                rows, cols = slice(c * CHUNK, (c + 1) * CHUNK), slice(g * gd, (g + 1) * gd)
                mixed = jnp.dot(wg, v[rows, cols], preferred_element_type=F32) + bcol
                u = p_ref[rows, g * gd:(g + 1) * gd].astype(F32)
                z = p_ref[rows, 2 * W + g * gd:2 * W + (g + 1) * gd].astype(F32)
                y_ref[rows, cols] = (u * mixed * (z * _sigmoid(z))).astype(y_ref.dtype)

    return _launch(
        body, name=name, grid=(M // tm,),
        in_specs=[pl.BlockSpec((tm, 3 * W), lambda i: (i, 0)),
                  pl.BlockSpec((1, W), lambda i: (0, 0)),
                  pl.BlockSpec((A_GROUPS, CHUNK, CHUNK), lambda i: (0, 0, 0)),
                  pl.BlockSpec((CHUNK, A_GROUPS), lambda i: (0, 0))],
        out_specs=[pl.BlockSpec((tm, W), lambda i: (i, 0))],
        out_shape=[jax.ShapeDtypeStruct((M, W), ACT)],
        args=(proj, v_gain, w_s, b_st), sem=("parallel",), comm=comm)


def _a_bwd(dout, w_out, proj, v_gain, w_s, b_st, name, comm=None):
    M = proj.shape[0]
    W = proj.shape[1] // 3
    Dm = dout.shape[1]
    gd = W // A_GROUPS
    tm = min(M, 256)
    nt = M // tm

    def body(do_ref, wo_ref, p_ref, vg_ref, ws_ref, bs_ref, dp_ref, dws_ref, dbs_ref, dvg_ref, dv_s):
        i = pl.program_id(0)

        @pl.when(i == 0)
        def _():
            dws_ref[...] = jnp.zeros_like(dws_ref)
            dbs_ref[...] = jnp.zeros_like(dbs_ref)
            dvg_ref[...] = jnp.zeros_like(dvg_ref)

        dy = _dot_nt(do_ref[...], wo_ref[...])
        pv = p_ref[:, W:2 * W].astype(F32)
        r = lax.rsqrt(jnp.mean(pv * pv, axis=-1, keepdims=True) + EPS)
        pvn = pv * r
        vg = vg_ref[...]
        v = (pvn * vg).astype(MXU)
        tri = _tril_mask()
        for g in range(A_GROUPS):
            wf = jnp.where(tri, ws_ref[g], 0.0)
            wg = wf.astype(MXU)
            wgt = wf.T.astype(MXU)
            bcol = bs_ref[:, g:g + 1]
            for c in range(tm // CHUNK):
                rows, cols = slice(c * CHUNK, (c + 1) * CHUNK), slice(g * gd, (g + 1) * gd)
                vb = v[rows, cols]
                mixed = jnp.dot(wg, vb, preferred_element_type=F32) + bcol
                u = p_ref[rows, g * gd:(g + 1) * gd].astype(F32)
                z = p_ref[rows, 2 * W + g * gd:2 * W + (g + 1) * gd].astype(F32)
                sig = _sigmoid(z)
                sz = z * sig
                dyb = dy[rows, cols]
                dp_ref[rows, g * gd:(g + 1) * gd] = (dyb * mixed * sz).astype(dp_ref.dtype)
                dp_ref[rows, 2 * W + g * gd:2 * W + (g + 1) * gd] = (
                    dyb * u * mixed * (sig * (1.0 + z * (1.0 - sig)))).astype(dp_ref.dtype)
                dmix = dyb * u * sz
                dws_ref[g] += _dot_nt(dmix, vb)
                dbs_ref[:, g:g + 1] += jnp.sum(dmix, axis=1, keepdims=True)
                dv_s[rows, cols] = jnp.dot(wgt, dmix.astype(MXU), preferred_element_type=F32)
        dv = dv_s[...]
        dvg_ref[...] += jnp.sum(dv * pvn, axis=0, keepdims=True)
        dpvn = dv * vg
        dp_ref[:, W:2 * W] = (r * (dpvn - pvn * jnp.mean(dpvn * pvn, axis=-1, keepdims=True))).astype(dp_ref.dtype)

        @pl.when(i == nt - 1)
        def _():
            for g in range(A_GROUPS):
                dws_ref[g] = jnp.where(tri, dws_ref[g], 0.0)

    return _launch(
        body, name=name, grid=(nt,),
        in_specs=[pl.BlockSpec((tm, Dm), lambda i: (i, 0)),
                  pl.BlockSpec((W, Dm), lambda i: (0, 0)),
                  pl.BlockSpec((tm, 3 * W), lambda i: (i, 0)),
                  pl.BlockSpec((1, W), lambda i: (0, 0)),
                  pl.BlockSpec((A_GROUPS, CHUNK, CHUNK), lambda i: (0, 0, 0)),
                  pl.BlockSpec((CHUNK, A_GROUPS), lambda i: (0, 0))],
        out_specs=[pl.BlockSpec((tm, 3 * W), lambda i: (i, 0)),
                   pl.BlockSpec((A_GROUPS, CHUNK, CHUNK), lambda i: (0, 0, 0)),
                   pl.BlockSpec((CHUNK, A_GROUPS), lambda i: (0, 0)),
                   pl.BlockSpec((1, W), lambda i: (0, 0))],
        out_shape=[jax.ShapeDtypeStruct((M, 3 * W), ACT),
                   jax.ShapeDtypeStruct((A_GROUPS, CHUNK, CHUNK), F32),
                   jax.ShapeDtypeStruct((CHUNK, A_GROUPS), F32),
                   jax.ShapeDtypeStruct((1, W), F32)],
        scratch=[pltpu.VMEM((tm, W), F32)],
        args=(dout, w_out, proj, v_gain, w_s, b_st), sem=("arbitrary",), comm=comm)


def _pool_diff(xg, tail, i, tm, w):
    t = lax.broadcasted_iota(jnp.int32, (tm, tm + POOL_HALO), 0)
    s = lax.broadcasted_iota(jnp.int32, (tm, tm + POOL_HALO), 1)
    off = t - (s - POOL_HALO)
    band = jnp.where((off >= 0) & (off < w), 1.0, 0.0).astype(MXU)
    tail = jnp.where(i > 0, tail, jnp.zeros_like(tail))
    ext = jnp.concatenate([tail, xg], axis=0)
    ssum = jnp.dot(band, ext.astype(MXU), preferred_element_type=F32)
    tglob = i * tm + lax.broadcasted_iota(jnp.int32, (tm, 1), 0)
    cnt = jnp.minimum(tglob + 1, w).astype(F32)
    return ssum / cnt - xg.astype(F32)


def _c_mid(proj, w_grp, scale, name):
    M = proj.shape[0]
    W = proj.shape[1] // 2
    ng = len(POOL_SIZES)
    cg = W // ng
    tm = min(M, 256)
    hb = tm // POOL_HALO

    def body(xc_ref, tail_ref, z_ref, wg_ref, sc_ref, y_ref):
        i = pl.program_id(0)
        for g, w in enumerate(POOL_SIZES):
            cols = slice(g * cg, (g + 1) * cg)
            d = _pool_diff(xc_ref[:, cols], tail_ref[:, cols], i, tm, w)
            mixed = _dot(d, wg_ref[g]) * sc_ref[:, cols]
            z = z_ref[:, cols].astype(F32)
            y_ref[:, cols] = (mixed * (z * _sigmoid(z))).astype(y_ref.dtype)

    return pl.pallas_call(
        body, name=name, grid=(M // tm,),
        in_specs=[pl.BlockSpec((tm, W), lambda i: (i, 0)),
                  pl.BlockSpec((POOL_HALO, W), lambda i: (jnp.maximum(i * hb - 1, 0), 0)),
                  pl.BlockSpec((tm, W), lambda i: (i, 1)),
                  pl.BlockSpec((ng, cg, cg), lambda i: (0, 0, 0)),
                  pl.BlockSpec((1, W), lambda i: (0, 0))],
        out_specs=pl.BlockSpec((tm, W), lambda i: (i, 0)),
        out_shape=jax.ShapeDtypeStruct((M, W), ACT),
        compiler_params=_cp("parallel"),
    )(proj, proj, proj, w_grp, scale)


def _c_bwd1(dout, w_out, proj, w_grp, scale, name, comm=None):
    M = proj.shape[0]
    W = proj.shape[1] // 2
    Dm = dout.shape[1]
    ng = len(POOL_SIZES)
    cg = W // ng
    rl = cg // N_DEV
    tm = min(M, 256)
    hb = tm // POOL_HALO
    nt = M // tm

    def body(do_ref, wo_ref, xc_ref, tail_ref, z_ref, wg_ref, sc_ref, dd_ref, dz_ref, dwg_ref, dsc_ref, acc_ref):
        i = pl.program_id(0)

        @pl.when(i == 0)
        def _():
            acc_ref[...] = jnp.zeros_like(acc_ref)
            dsc_ref[...] = jnp.zeros_like(dsc_ref)

        dy = _dot_nt(do_ref[...], wo_ref[...])
        for g, w in enumerate(POOL_SIZES):
            cols = slice(g * cg, (g + 1) * cg)
            d = _pool_diff(xc_ref[:, cols], tail_ref[:, cols], i, tm, w)
            mr = _dot(d, wg_ref[g])
            sc = sc_ref[:, cols]
            z = z_ref[:, cols].astype(F32)
            sig = _sigmoid(z)
            dyg = dy[:, cols]
            dmixed = dyg * (z * sig)
            dz_ref[:, cols] = (dyg * (mr * sc) * (sig * (1.0 + z * (1.0 - sig)))).astype(dz_ref.dtype)
            dsc_ref[:, cols] += jnp.sum(dmixed * mr, axis=0, keepdims=True)
            dmr = (dmixed * sc).astype(MXU)
            acc_ref[g] += _dot_tn(d, dmr)
            dd_ref[:, cols] = _dot_nt(dmr, wg_ref[g]).astype(dd_ref.dtype)

        @pl.when(i == nt - 1)
        def _():
            for dev in range(N_DEV):
                for g in range(ng):
                    dwg_ref[_chunk_slot(dev), g] = acc_ref[g, dev * rl:(dev + 1) * rl, :]

    return _launch(
        body, name=name, grid=(nt,),
        in_specs=[pl.BlockSpec((tm, Dm), lambda i: (i, 0)),
                  pl.BlockSpec((W, Dm), lambda i: (0, 0)),
                  pl.BlockSpec((tm, W), lambda i: (i, 0)),
                  pl.BlockSpec((POOL_HALO, W), lambda i: (jnp.maximum(i * hb - 1, 0), 0)),
                  pl.BlockSpec((tm, W), lambda i: (i, 1)),
                  pl.BlockSpec((ng, cg, cg), lambda i: (0, 0, 0)),
                  pl.BlockSpec((1, W), lambda i: (0, 0))],
        out_specs=[pl.BlockSpec((tm, W), lambda i: (i, 0)),
                   pl.BlockSpec((tm, W), lambda i: (i, 0)),
                   pl.BlockSpec((N_DEV, ng, rl, cg), lambda i: (0, 0, 0, 0)),
                   pl.BlockSpec((1, W), lambda i: (0, 0))],
        out_shape=[jax.ShapeDtypeStruct((M, W), ACT), jax.ShapeDtypeStruct((M, W), ACT),
                   jax.ShapeDtypeStruct((N_DEV, ng, rl, cg), F32), jax.ShapeDtypeStruct((1, W), F32)],
        scratch=[pltpu.VMEM((ng, cg, cg), F32)],
        args=(dout, w_out, proj, proj, proj, w_grp, scale), sem=("arbitrary",), comm=comm)


def _c_bwd2(dd, dz, name):
    M, W = dd.shape
    ng = len(POOL_SIZES)
    cg = W // ng
    tm = min(M, 256)
    hb = tm // POOL_HALO
    nt = M // tm

    def body(dd_ref, head_ref, dz_ref, dp_ref):
        i = pl.program_id(0)
        s = lax.broadcasted_iota(jnp.int32, (tm, tm + POOL_HALO), 0)
        t = lax.broadcasted_iota(jnp.int32, (tm, tm + POOL_HALO), 1)
        off = t - s
        tglob = i * tm + lax.broadcasted_iota(jnp.int32, (tm + POOL_HALO, 1), 0)
        for g, w in enumerate(POOL_SIZES):
            cols = slice(g * cg, (g + 1) * cg)
            ddg = dd_ref[:, cols].astype(F32)
            head = head_ref[:, cols].astype(F32)
            head = jnp.where(i < nt - 1, head, jnp.zeros_like(head))
            cnt = jnp.minimum(tglob + 1, w).astype(F32)
            ext = (jnp.concatenate([ddg, head], axis=0) / cnt).astype(MXU)
            band = jnp.where((off >= 0) & (off < w), 1.0, 0.0).astype(MXU)
            dp_ref[:, cols] = (jnp.dot(band, ext, preferred_element_type=F32) - ddg).astype(dp_ref.dtype)
        dp_ref[:, W:] = dz_ref[...]

    return pl.pallas_call(
        body, name=name, grid=(nt,),
        in_specs=[pl.BlockSpec((tm, W), lambda i: (i, 0)),
                  pl.BlockSpec((POOL_HALO, W), lambda i: (jnp.minimum((i + 1) * hb, M // POOL_HALO - 1), 0)),
                  pl.BlockSpec((tm, W), lambda i: (i, 0))],
        out_specs=pl.BlockSpec((tm, 2 * W), lambda i: (i, 0)),
        out_shape=jax.ShapeDtypeStruct((M, 2 * W), ACT),
        compiler_params=_cp("parallel"),
    )(dd, dd, dz)


def _rope_tables(S):
    half = ROPE_DIM // 2
    inv_freq = jnp.power(jnp.float32(ROPE_THETA), -jnp.arange(half, dtype=F32) / half)
    ang = jnp.arange(S, dtype=F32)[:, None] * inv_freq[None, :]
    cos, sin = jnp.cos(ang), jnp.sin(ang)
    rest = HEAD_DIM - ROPE_DIM
    cf = jnp.concatenate([cos, cos, jnp.ones((S, rest), F32)], axis=1)
    sf = jnp.concatenate([-sin, sin, jnp.zeros((S, rest), F32)], axis=1)
    return cf, sf


def _swap_matrix():
    half = ROPE_DIM // 2
    a = lax.broadcasted_iota(jnp.int32, (HEAD_DIM, HEAD_DIM), 0)
    e = lax.broadcasted_iota(jnp.int32, (HEAD_DIM, HEAD_DIM), 1)
    hit = ((e < half) & (a == e + half)) | ((e >= half) & (e < 2 * half) & (a == e - half))
    return jnp.where(hit, 1.0, 0.0).astype(MXU)


def _b_qk_fwd(proj, tables, gains, name, comm=None):
    M = proj.shape[0]
    nsl = 2 * len(B_DILATIONS) * B_HEADS
    Wqk = nsl * HEAD_DIM
    tm = min(M, 256)

    def body(p_ref, cf_ref, sf_ref, g_ref, o_ref):
        cf, sf = cf_ref[...], sf_ref[...]
        swap = _swap_matrix()
        for j in range(nsl):
            cols = slice(j * HEAD_DIM, (j + 1) * HEAD_DIM)
            xv = p_ref[:, cols].astype(F32)
            r = lax.rsqrt(jnp.mean(xv * xv, axis=-1, keepdims=True) + EPS)
            xg = xv * g_ref[j // B_HEADS:j // B_HEADS + 1, :]
            hi = xg.astype(MXU)
            lo = (xg - hi.astype(F32)).astype(MXU)
            sw = jnp.dot(hi, swap, preferred_element_type=F32) + jnp.dot(lo, swap, preferred_element_type=F32)
            o_ref[:, cols] = (r * (xg * cf + sw * sf)).astype(o_ref.dtype)

    tspec = pl.BlockSpec((tm, HEAD_DIM), lambda i: (i, 0))
    return _launch(
        body, name=name, grid=(M // tm,),
        in_specs=[pl.BlockSpec((tm, Wqk), lambda i: (i, 0)), tspec, tspec,
                  pl.BlockSpec((8, HEAD_DIM), lambda i: (0, 0))],
        out_specs=[pl.BlockSpec((tm, Wqk), lambda i: (i, 0))],
        out_shape=[jax.ShapeDtypeStruct((M, Wqk), ACT)],
        args=(proj, *tables, gains), sem=("parallel",), comm=comm)


def _b_qk_bwd(dqs, dks, proj, tables, gains, dproj, name):
    M = proj.shape[0]
    ngr = len(B_DILATIONS)
    nsl = 2 * ngr * B_HEADS
    Wqk = nsl * HEAD_DIM
    Wg = B_HEADS * HEAD_DIM
    tm = min(M, 256)

    def body(*refs):
        d_refs = refs[:2 * ngr]
        p_ref, cf_ref, sf_ref, g_ref = refs[2 * ngr:2 * ngr + 4]
        dp_ref, dg_ref = refs[-2], refs[-1]

        @pl.when(pl.program_id(0) == 0)
        def _():
            dg_ref[...] = jnp.zeros_like(dg_ref)

        cf, sf = cf_ref[...], sf_ref[...]
        swap = _swap_matrix()
        for j in range(nsl):
            t, hh = j // B_HEADS, j % B_HEADS
            cols = slice(j * HEAD_DIM, (j + 1) * HEAD_DIM)
            dy = d_refs[t][:, hh * HEAD_DIM:(hh + 1) * HEAD_DIM].astype(F32)
            dxn = dy * cf + jnp.dot((dy * sf).astype(MXU), swap, preferred_element_type=F32)
            xv = p_ref[:, cols].astype(F32)
            r = lax.rsqrt(jnp.mean(xv * xv, axis=-1, keepdims=True) + EPS)
            xh = xv * r
            dg_ref[t:t + 1, :] += jnp.sum(dxn * xh, axis=0, keepdims=True)
            dxh = dxn * g_ref[t:t + 1, :]
            dp_ref[:, cols] = (r * (dxh - xh * jnp.mean(dxh * xh, axis=-1, keepdims=True))).astype(dp_ref.dtype)

    tspec = pl.BlockSpec((tm, HEAD_DIM), lambda i: (i, 0))
    dspec = pl.BlockSpec((tm, Wg), lambda i: (i, 0))
    n_in = 2 * ngr + 5
    return pl.pallas_call(
        body, name=name, grid=(M // tm,),
        in_specs=[dspec] * (2 * ngr) + [pl.BlockSpec((tm, Wqk), lambda i: (i, 0)), tspec, tspec,
                                        pl.BlockSpec((8, HEAD_DIM), lambda i: (0, 0)),
                                        pl.BlockSpec(memory_space=pl.ANY)],
        out_specs=[pl.BlockSpec((tm, Wqk), lambda i: (i, 0)), pl.BlockSpec((8, HEAD_DIM), lambda i: (0, 0))],
        out_shape=[jax.ShapeDtypeStruct(dproj.shape, dproj.dtype), jax.ShapeDtypeStruct((8, HEAD_DIM), F32)],
        input_output_aliases={n_in - 1: 0},
        compiler_params=_cp("arbitrary"),
    )(*dqs, *dks, proj, *tables, gains, dproj)


def _attn_tile(D, M):
    return max(HEAD_DIM * D, min(M, 2048))


class _TokenRows:
    GROUP = 16

    def __init__(self, D):
        self.D = D
        self.pitch = 24 if D == 16 else self.GROUP

    def rows(self, ntok):
        return ntok // self.GROUP * self.pitch

    def every_dth(self, tok0, n):
        start = tok0 // self.GROUP * self.pitch + tok0 % self.GROUP
        stride = self.D * self.pitch // self.GROUP
        return pl.ds(start, n) if stride == 1 else pl.ds(start, n, stride=stride)

    def put(self, dst, tok0, src_ref, ntok):
        if self.pitch == self.GROUP:
            dst[tok0:tok0 + ntok, :] = src_ref[...].astype(F32)
            return

        def group(i, carry):
            row = pl.multiple_of((tok0 // self.GROUP + i) * self.pitch, 8)
            dst[pl.ds(row, self.GROUP), :] = src_ref[pl.ds(pl.multiple_of(i * self.GROUP, self.GROUP), self.GROUP), :].astype(F32)
            return carry

        lax.fori_loop(0, ntok // self.GROUP, group, 0, unroll=8)

    def get(self, dst_ref, src, ntok):
        if self.pitch == self.GROUP:
            dst_ref[...] = src[0:ntok, :].astype(dst_ref.dtype)
            return

        def group(i, carry):
            row = pl.multiple_of(i * self.pitch, 8)
            dst_ref[pl.ds(pl.multiple_of(i * self.GROUP, self.GROUP), self.GROUP), :] = src[pl.ds(row, self.GROUP), :].astype(dst_ref.dtype)
            return carry

        lax.fori_loop(0, ntok // self.GROUP, group, 0, unroll=8)


def _attn_mask(base):
    qi = lax.broadcasted_iota(jnp.int32, (CHUNK, 2 * CHUNK), 0)
    ki = lax.broadcasted_iota(jnp.int32, (CHUNK, 2 * CHUNK), 1)
    return (ki >= qi) & (ki <= qi + CHUNK) & (ki >= CHUNK - base)


def _b_attn_fwd(qk, proj, g, name):
    M = qk.shape[0]
    D = B_DILATIONS[g]
    ngr = len(B_DILATIONS)
    T = _attn_tile(D, M)
    P = HEAD_DIM * D
    nsb = T // P
    Wg = B_HEADS * HEAD_DIM
    scale = np.float32(1.0 / np.sqrt(HEAD_DIM))

    lay = _TokenRows(D)
    RP, RT = lay.rows(P), lay.rows(T)

    def body(q_ref, k_ref, v_ref, o_ref, l_ref, qs, ks, vs, os_):
        n = pl.program_id(1)

        @pl.when(n == 0)
        def _():
            ks[0:RP, :] = jnp.zeros((RP, HEAD_DIM), F32)
            vs[0:RP, :] = jnp.zeros((RP, HEAD_DIM), F32)

        lay.put(qs, 0, q_ref, T)
        lay.put(ks, P, k_ref, T)
        lay.put(vs, P, v_ref, T)

        for b in range(nsb):
            mask = _attn_mask(n * (T // D) + b * CHUNK)
            for r in range(D):
                start = b * P + r
                q = qs[lay.every_dth(start, CHUNK), :]
                k = ks[lay.every_dth(start, 2 * CHUNK), :]
                v = vs[lay.every_dth(start, 2 * CHUNK), :]
                s = jnp.where(mask, _dot_nt(q, k) * scale, NEG)
                m = jnp.max(s, axis=-1, keepdims=True)
                p = jnp.exp(s - m)
                l = jnp.sum(p, axis=-1, keepdims=True)
                o = _dot(p, v) / l
                os_[lay.every_dth(start, CHUNK), :] = o
                l_ref[:, b * D + r:b * D + r + 1] = m + jnp.log(l)

        lay.get(o_ref, os_, T)
        ks[0:RP, :] = ks[RT:RT + RP, :]
        vs[0:RP, :] = vs[RT:RT + RP, :]

    blk = (T, HEAD_DIM)
    U = nsb * D
    return pl.pallas_call(
        body, name=name, grid=(B_HEADS, M // T),
        in_specs=[pl.BlockSpec(blk, lambda h, n: (n, g * B_HEADS + h)),
                  pl.BlockSpec(blk, lambda h, n: (n, (ngr + g) * B_HEADS + h)),
                  pl.BlockSpec(blk, lambda h, n: (n, (2 * ngr + g) * B_HEADS + h))],
        out_specs=[pl.BlockSpec(blk, lambda h, n: (n, h)), pl.BlockSpec((None, CHUNK, U), lambda h, n: (h, n, 0))],
        out_shape=[jax.ShapeDtypeStruct((M, Wg), ACT), jax.ShapeDtypeStruct((B_HEADS, (M // T) * CHUNK, U), F32)],
        scratch_shapes=[pltpu.VMEM((RT, HEAD_DIM), F32), pltpu.VMEM((RP + RT, HEAD_DIM), F32),
                        pltpu.VMEM((RP + RT, HEAD_DIM), F32), pltpu.VMEM((RT, HEAD_DIM), F32)],
        compiler_params=_cp("parallel", "arbitrary"),
    )(qk, qk, proj)


def _units_to_tokens(a, D, T):
    H = a.shape[0]
    nsb = T // (HEAD_DIM * D)
    return a.reshape(H, -1, CHUNK, nsb, D).transpose(1, 3, 2, 4, 0).reshape(-1, H)


def _tokens_to_units(a, D, T):
    M, H = a.shape
    nsb = T // (HEAD_DIM * D)
    return a.reshape(M // T, nsb, CHUNK, D, H).transpose(4, 0, 2, 1, 3).reshape(H, (M // T) * CHUNK, nsb * D)


def _b_combine(os_, ls, proj, name):
    M, Wg = os_[0].shape
    ngr = len(B_DILATIONS)
    tm = min(M, 512)

    def body(*refs):
        o_refs, l_refs, z_ref = refs[:ngr], refs[ngr:2 * ngr], refs[2 * ngr]
        y_ref, o_ref, lse_ref = refs[2 * ngr + 1:]
        for h in range(B_HEADS):
            cols = slice(h * HEAD_DIM, (h + 1) * HEAD_DIM)
            ls_ = [r[:, h:h + 1] for r in l_refs]
            m = functools.reduce(jnp.maximum, ls_)
            es = [jnp.exp(l - m) for l in ls_]
            tot = functools.reduce(lambda a, b: a + b, es)
            o = functools.reduce(lambda a, b: a + b, [(e / tot) * r[:, cols].astype(F32) for e, r in zip(es, o_refs)])
            z = z_ref[:, cols].astype(F32)
            y_ref[:, cols] = (o * (z * _sigmoid(z))).astype(y_ref.dtype)
            o_ref[:, cols] = o.astype(o_ref.dtype)
            lse_ref[:, h:h + 1] = m + jnp.log(tot)

    spec = pl.BlockSpec((tm, Wg), lambda i: (i, 0))
    hspec = pl.BlockSpec((tm, B_HEADS), lambda i: (i, 0))
    return pl.pallas_call(
        body, name=name, grid=(M // tm,),
        in_specs=[spec] * ngr + [hspec] * ngr + [pl.BlockSpec((tm, Wg), lambda i: (i, 3 * ngr))],
        out_specs=[spec, spec, hspec],
        out_shape=[jax.ShapeDtypeStruct((M, Wg), ACT), jax.ShapeDtypeStruct((M, Wg), ACT),
                   jax.ShapeDtypeStruct((M, B_HEADS), F32)],
        compiler_params=_cp("parallel"),
    )(*os_, *ls, proj)


def _b_bwd_pre(dout, w_out, o, proj, name):
    M, Wg = o.shape
    Dm = dout.shape[1]
    ngr = len(B_DILATIONS)
    tm = min(M, 512)

    def body(do_ref, wo_ref, o_ref, z_ref, dov_ref, dl_ref, dp_ref):
        dy = _dot_nt(do_ref[...], wo_ref[...])
        z = z_ref[...].astype(F32)
        sig = _sigmoid(z)
        ov = o_ref[...].astype(F32)
        dp_ref[...] = (dy * ov * (sig * (1.0 + z * (1.0 - sig)))).astype(dp_ref.dtype)
        dov = dy * (z * sig)
        dov_ref[...] = dov.astype(dov_ref.dtype)
        prod = dov * ov
        for h in range(B_HEADS):
            dl_ref[:, h:h + 1] = jnp.sum(prod[:, h * HEAD_DIM:(h + 1) * HEAD_DIM], axis=-1, keepdims=True)

    spec = pl.BlockSpec((tm, Wg), lambda i: (i, 0))
    zspec = pl.BlockSpec((tm, Wg), lambda i: (i, 3 * ngr))
    return pl.pallas_call(
        body, name=name, grid=(M // tm,),
        in_specs=[pl.BlockSpec((tm, Dm), lambda i: (i, 0)), pl.BlockSpec((Wg, Dm), lambda i: (0, 0)), spec, zspec],
        out_specs=[spec, pl.BlockSpec((tm, B_HEADS), lambda i: (i, 0)), zspec],
        out_shape=[jax.ShapeDtypeStruct((M, Wg), ACT), jax.ShapeDtypeStruct((M, B_HEADS), F32),
                   jax.ShapeDtypeStruct(proj.shape, ACT)],
        compiler_params=_cp("parallel"),
    )(dout, w_out, o, proj)


def _b_attn_bwd(qk, proj, dov, lse, delta, dproj, g, name, comm=None):
    M = qk.shape[0]
    D = B_DILATIONS[g]
    ngr = len(B_DILATIONS)
    T = _attn_tile(D, M)
    P = HEAD_DIM * D
    nsb = T // P
    nt = M // T
    Wg = B_HEADS * HEAD_DIM
    scale = np.float32(1.0 / np.sqrt(HEAD_DIM))
    shift = T - P
    lay = _TokenRows(D)
    RP, RT = lay.rows(P), lay.rows(T)

    def body(q_ref, k_ref, v_ref, do_ref, l_ref, dl_ref, dp_any, dq_ref, dk_ref, dv_ref,
             qs, dos, ks, vs, dqs, dks, dvs):
        n = pl.program_id(1)

        @pl.when(n == 0)
        def _():
            ks[0:RP, :] = jnp.zeros((RP, HEAD_DIM), F32)
            vs[0:RP, :] = jnp.zeros((RP, HEAD_DIM), F32)
            dks[...] = jnp.zeros((2 * RT, HEAD_DIM), F32)
            dvs[...] = jnp.zeros((2 * RT, HEAD_DIM), F32)

        @pl.when(n < nt)
        def _():
            lay.put(qs, 0, q_ref, T)
            lay.put(dos, 0, do_ref, T)
            lay.put(ks, P, k_ref, T)
            lay.put(vs, P, v_ref, T)

            masks = [_attn_mask(n * (T // D) + b * CHUNK) for b in range(nsb)]
            for r in range(D):
                carry_dv = carry_dk = None
                for b in range(nsb):
                    start = b * P + r
                    qsl = lay.every_dth(start, CHUNK)
                    ksl = lay.every_dth(start, 2 * CHUNK)
                    lo = lay.every_dth(start + shift, CHUNK)
                    q = qs[qsl, :]
                    do = dos[qsl, :]
                    k = ks[ksl, :]
                    v = vs[ksl, :]
                    s = _dot_nt(q, k) * scale
                    u = b * D + r
                    p = jnp.where(masks[b], jnp.exp(s - l_ref[:, u:u + 1]), 0.0)
                    dv = _dot_tn(p, do)
                    dp = _dot_nt(do, v)
                    ds = (p * (dp - dl_ref[:, u:u + 1]) * scale).astype(MXU)
                    dqs[qsl, :] = _dot(ds, k)
                    dk = _dot_tn(ds, q)
                    if b == 0:
                        dvs[lo, :] += dv[:CHUNK]
                        dks[lo, :] += dk[:CHUNK]
                    else:
                        dvs[lo, :] = carry_dv + dv[:CHUNK]
                        dks[lo, :] = carry_dk + dk[:CHUNK]
                    carry_dv, carry_dk = dv[CHUNK:], dk[CHUNK:]
                hi = lay.every_dth((nsb - 1) * P + r + shift + P, CHUNK)
                dvs[hi, :] = carry_dv
                dks[hi, :] = carry_dk

        lay.get(dq_ref, dqs, T)
        lay.get(dk_ref, dks, T)
        lay.get(dv_ref, dvs, T)
        dks[0:RT, :] = dks[RT:2 * RT, :]
        dvs[0:RT, :] = dvs[RT:2 * RT, :]
        ks[0:RP, :] = ks[RT:RT + RP, :]
        vs[0:RP, :] = vs[RT:RT + RP, :]

    blk = (T, HEAD_DIM)
    cur = lambda n: jnp.minimum(n, nt - 1)
    prv = lambda n: jnp.maximum(n - 1, 0)
    return _launch(
        body, name=name, grid=(B_HEADS, nt + 1),
        in_specs=[pl.BlockSpec(blk, lambda h, n: (cur(n), g * B_HEADS + h)),
                  pl.BlockSpec(blk, lambda h, n: (cur(n), (ngr + g) * B_HEADS + h)),
                  pl.BlockSpec(blk, lambda h, n: (cur(n), (2 * ngr + g) * B_HEADS + h)),
                  pl.BlockSpec(blk, lambda h, n: (cur(n), h)),
                  pl.BlockSpec((None, CHUNK, nsb * D), lambda h, n: (h, cur(n), 0)),
                  pl.BlockSpec((None, CHUNK, nsb * D), lambda h, n: (h, cur(n), 0)),
                  pl.BlockSpec(memory_space=pl.ANY)],
        out_specs=[pl.BlockSpec(blk, lambda h, n: (cur(n), h)),
                   pl.BlockSpec(blk, lambda h, n: (prv(n), h)),
                   pl.BlockSpec(blk, lambda h, n: (prv(n), (2 * ngr + g) * B_HEADS + h))],
        out_shape=[jax.ShapeDtypeStruct((M, Wg), ACT), jax.ShapeDtypeStruct((M, Wg), ACT),
                   jax.ShapeDtypeStruct(dproj.shape, dproj.dtype)],
        scratch=[pltpu.VMEM((RT, HEAD_DIM), F32)] * 2
        + [pltpu.VMEM((RP + RT, HEAD_DIM), F32)] * 2
        + [pltpu.VMEM((RT, HEAD_DIM), F32)]
        + [pltpu.VMEM((2 * RT, HEAD_DIM), F32)] * 2,
        aliases={6: 2},
        args=(qk, qk, proj, dov, lse, delta, dproj), sem=("parallel", "arbitrary"), comm=comm)


def _coords():
    return lax.axis_index("x"), lax.axis_index("y"), lax.axis_index("c")


def _gather_blocks(x_refs, out_refs, send_sems, recv_sems, local_sems):
    x, y, c = _coords()
    me, sibling = (x, y, c), (x, y, 1 - c)
    chips = [(1 - x, y), (x, 1 - y), (1 - x, 1 - y)]
    arrays = range(len(x_refs))

    def slot(a, px, py, pc):
        return out_refs[a].at[4 * px + 2 * py + pc]

    def copy(a, k, block, to, src=None):
        return _remote(slot(a, *block) if src is None else src, slot(a, *block), send_sems, recv_sems, 7 * a + k, to)

    mine = [pltpu.make_async_copy(x_refs[a], slot(a, *me), local_sems.at[a]) for a in arrays]
    first = [copy(a, 0, me, sibling, src=x_refs[a]) for a in arrays]
    first += [copy(a, 1 + j, me, (*chip, c), src=x_refs[a]) for j, chip in enumerate(chips) for a in arrays]
    for cp in mine + first:
        cp.start()
    passed = []
    for j, chip in enumerate(chips):
        for a in arrays:
            copy(a, 1 + j, (*chip, c), me).wait_recv()
            passed.append(copy(a, 4 + j, (*chip, c), sibling))
            passed[-1].start()
    for a in arrays:
        copy(a, 0, sibling, me).wait_recv()
        for j, chip in enumerate(chips):
            copy(a, 4 + j, (*chip, 1 - c), me).wait_recv()
    for cp in first + passed:
        cp.wait_send()
    for cp in mine:
        cp.wait()


def _all_gather_hbm(arrays, name):
    n = len(arrays)

    def body(*refs):
        _gather_blocks(refs[:n], refs[n:2 * n], *refs[2 * n:])

    return pl.pallas_call(
        body, name=name, in_specs=[_HBM] * n, out_specs=[_HBM] * n,
        out_shape=[jax.ShapeDtypeStruct((N_DEV,) + a.shape, a.dtype) for a in arrays],
        scratch_shapes=[pltpu.SemaphoreType.DMA((7 * n,)), pltpu.SemaphoreType.DMA((7 * n,)),
                        pltpu.SemaphoreType.DMA((n,))],
    )(*arrays)


def _all_reduce_small(part):
    R, C = part.shape

    def body(x_ref, tot_ref, gath, send_sems, recv_sems, local_sems):
        _gather_blocks([x_ref], [gath], send_sems, recv_sems, local_sems)
        acc = gath[0]
        for d in range(1, N_DEV):
            acc = acc + gath[d]
        tot_ref[...] = acc

    return pl.pallas_call(
        body, name="ar_small",
        in_specs=[pl.BlockSpec(memory_space=pltpu.VMEM)],
        out_specs=pl.BlockSpec(memory_space=pltpu.VMEM),
        out_shape=jax.ShapeDtypeStruct((R, C), F32),
        scratch_shapes=[pltpu.VMEM((N_DEV, R, C), F32),
                        pltpu.SemaphoreType.DMA((7,)), pltpu.SemaphoreType.DMA((7,)), pltpu.SemaphoreType.DMA((1,))],
        compiler_params=pltpu.CompilerParams(vmem_limit_bytes=VMEM_LIMIT),
    )(part)


def _remote(src, dst, send_sems, recv_sems, k, peer):
    return pltpu.make_async_remote_copy(src_ref=src, dst_ref=dst, send_sem=send_sems.at[k], recv_sem=recv_sems.at[k],
                                        device_id=peer, device_id_type=MESH)


def _ag_send(arrays):
    n = len(arrays)

    def make(c_in, c_out, send_sems, recv_sems, local_sems):
        x, y, c = _coords()
        peers = [(x, y, 1 - c), (1 - x, y, c), (x, 1 - y, c), (1 - x, 1 - y, c)]
        cps = []
        for a in range(n):
            src, dst = c_in[a], c_out[a].at[4 * x + 2 * y + c]
            cps.append(pltpu.make_async_copy(src, dst, local_sems.at[a]))
            cps += [_remote(src, dst, send_sems, recv_sems, 4 * a + k, peer) for k, peer in enumerate(peers)]
        return cps

    return _Comm(arrays, [jax.ShapeDtypeStruct((N_DEV,) + a.shape, a.dtype) for a in arrays], 4 * n, make, n_local=n)


def _ag_forward(gaths):
    n = len(gaths)

    def make(c_in, c_out, send_sems, recv_sems, local_sems):
        x, y, c = _coords()
        chips = [(1 - x, y), (x, 1 - y), (1 - x, 1 - y)]
        cps = []
        for a in range(n):
            buf = c_out[a]
            cps += [_remote(buf.at[4 * px + 2 * py + c], buf.at[4 * px + 2 * py + c], send_sems, recv_sems, 3 * a + j,
                            (x, y, 1 - c)) for j, (px, py) in enumerate(chips)]
        return cps

    return _Comm(gaths, [jax.ShapeDtypeStruct(g.shape, g.dtype) for g in gaths], 3 * n, make,
                 aliases={a: a for a in range(n)})


def _rs_sibling(grads):
    n = len(grads)

    def make(c_in, c_out, send_sems, recv_sems, local_sem):
        x, y, c = _coords()
        return [_remote(c_in[a].at[pl.ds(4 * (1 - c), 4)], c_out[a], send_sems, recv_sems, a, (x, y, 1 - c))
                for a in range(n)]

    return _Comm(grads, [jax.ShapeDtypeStruct((4,) + g.shape[1:], g.dtype) for g in grads], n, make)


def _rs_chips(parts):
    n = len(parts)

    def make(c_in, c_out, send_sems, recv_sems, local_sem):
        x, y, c = _coords()
        peers = [(x, 1 - y, c), (1 - x, y, c), (1 - x, 1 - y, c)]
        return [_remote(c_in[a].at[k], c_out[a].at[k], send_sems, recv_sems, 3 * a + k, peer)
                for a in range(n) for k, peer in enumerate(peers)]

    return _Comm(parts, [jax.ShapeDtypeStruct(p.shape, p.dtype) for p in parts], 3 * n, make)


def _row_tile(rows, cols):
    tr = min(rows, 1 << int(np.log2((1 << 18) // cols)))
    assert rows % tr == 0
    return tr


def _chip_partials(coords, g, r1, name):
    _, rows, C = g.shape
    tr = _row_tile(rows, C)

    def body(co_ref, g_ref, r_ref, o_ref):
        o_ref[...] = (g_ref[...] + r_ref[...]).astype(o_ref.dtype)

    def chip(k, co):
        return jnp.bitwise_xor(2 * co[0] + co[1], k + 1)

    return pl.pallas_call(
        body, name=name,
        grid_spec=pltpu.PrefetchScalarGridSpec(
            num_scalar_prefetch=1, grid=(3, rows // tr),
            in_specs=[pl.BlockSpec((None, tr, C), lambda k, t, co: (4 * co[2] + chip(k, co), t, 0)),
                      pl.BlockSpec((None, tr, C), lambda k, t, co: (chip(k, co), t, 0))],
            out_specs=pl.BlockSpec((None, tr, C), lambda k, t, co: (k, t, 0))),
        out_shape=jax.ShapeDtypeStruct((3, rows, C), WIRE),
        compiler_params=_cp("parallel", "parallel"),
    )(coords, g, r1)


def _adam_math(w, g, m, v):
    m = ADAM_B1 * m + (1.0 - ADAM_B1) * g
    v = ADAM_B2 * v + (1.0 - ADAM_B2) * (g * g)
    m_hat = m / (1.0 - ADAM_B1 ** ADAM_STEP)
    v_hat = v / (1.0 - ADAM_B2 ** ADAM_STEP)
    delta = -ADAM_LR * (m_hat / (jnp.sqrt(v_hat) + ADAM_EPS) + ADAM_WD * w)
    return delta, m, v


def _adamw_sharded(coords, w, m, v, g, r1, r2, name):
    rows, C = w.shape
    tr = _row_tile(rows, C)

    def body(co_ref, w_ref, m_ref, v_ref, g_ref, r1_ref, r2_ref, go_ref, d_ref, mo_ref, vo_ref):
        grad = g_ref[...] + r1_ref[...]
        for k in range(3):
            grad = grad + r2_ref[k].astype(F32)
        go_ref[...] = grad
        d_ref[...], mo_ref[...], vo_ref[...] = _adam_math(w_ref[...], grad, m_ref[...], v_ref[...])

    spec = pl.BlockSpec((tr, C), lambda t, co: (t, 0))
    return pl.pallas_call(
        body, name=name,
        grid_spec=pltpu.PrefetchScalarGridSpec(
            num_scalar_prefetch=1, grid=(rows // tr,),
            in_specs=[spec, spec, spec,
                      pl.BlockSpec((None, tr, C), lambda t, co: (4 * co[2] + 2 * co[0] + co[1], t, 0)),
                      pl.BlockSpec((None, tr, C), lambda t, co: (2 * co[0] + co[1], t, 0)),
                      pl.BlockSpec((3, tr, C), lambda t, co: (0, t, 0))],
            out_specs=[spec] * 4),
        out_shape=[jax.ShapeDtypeStruct((rows, C), F32)] * 4,
        compiler_params=_cp("parallel"),
    )(coords, w, m, v, g, r1, r2)


def _adamw_small(w, g, m, v, name):
    def body(w_ref, g_ref, m_ref, v_ref, d_ref, mo_ref, vo_ref):
        d_ref[...], mo_ref[...], vo_ref[...] = _adam_math(w_ref[...], g_ref[...], m_ref[...], v_ref[...])

    return pl.pallas_call(
        body, name=name, out_shape=[jax.ShapeDtypeStruct(w.shape, F32)] * 3,
        in_specs=[pl.BlockSpec(memory_space=pltpu.VMEM)] * 4,
        out_specs=[pl.BlockSpec(memory_space=pltpu.VMEM)] * 3,
    )(w, g, m, v)


def _reduce_scatter_adds(coords, grads, r1s, tag):
    return [_chip_partials(coords, g, r, f"rs_add_{tag}{i}") for i, (g, r) in enumerate(zip(grads, r1s))]


def kernel(x, norm_gain, a_w_in, a_v_gain, a_w_s, a_b_s, a_w_out, b_w_in, b_q_gain, b_k_gain, b_w_out, c_w_in, c_w_grp, c_scale, c_w_out, loss_target, m_norm_gain, m_a_w_in, m_a_v_gain, m_a_w_s, m_a_b_s, m_a_w_out, m_b_w_in, m_b_q_gain, m_b_k_gain, m_b_w_out, m_c_w_in, m_c_w_grp, m_c_scale, m_c_w_out, v_norm_gain, v_a_w_in, v_a_v_gain, v_a_w_s, v_a_b_s, v_a_w_out, v_b_w_in, v_b_q_gain, v_b_k_gain, v_b_w_out, v_c_w_in, v_c_w_grp, v_c_scale, v_c_w_out):
    cx, cy, cc = _coords()
    coords = jnp.stack([cx, cy, cc]).astype(jnp.int32)
    dev = 4 * cx + 2 * cy + cc
    Dm = x.shape[2]

    xs, tgt = x[0], loss_target[0]
    tables = _rope_tables(xs.shape[0])
    ng = lambda i: norm_gain[i:i + 1]
    ngr = len(B_DILATIONS)
    bst = [a_b_s[l].T for l in range(2)]
    b_gains = jnp.concatenate([b_q_gain[0], b_k_gain[0], jnp.zeros((2, HEAD_DIM), F32)], axis=0)
    nla, nlb, nlc = a_w_in.shape[2], b_w_in.shape[2], c_w_in.shape[2]
    ngp, rlc, cgc = c_w_grp.shape[1:]
    wire = lambda w: w.astype(WIRE)

    nvg, nsc = a_v_gain.size, c_scale.size
    vec = jnp.concatenate([a_v_gain.reshape(-1), c_scale.reshape(-1), jnp.zeros((1024 - nvg - nsc,), F32)]).reshape(8, 128)
    wa_in0, wa_out0, vecs = _all_gather_hbm([wire(a_w_in[0]), wire(a_w_out[0]), vec], "ag_layer0")
    wa_out0 = wa_out0.reshape(-1, Dm)
    vecs = vecs.reshape(N_DEV, -1)
    a_vg = vecs[:, :nvg].reshape((N_DEV,) + a_v_gain.shape).transpose(1, 0, 2).reshape(a_v_gain.shape[0], -1)
    c_sc = vecs[:, nvg:nvg + nsc].reshape(1, -1)

    h0, p0, *g1 = _norm_proj(xs, ng(0), wa_in0, "l0_proj", comm=_ag_send([wire(b_w_in[0]), wire(b_w_out[0])]))
    y0, wb_in, wb_out = _a_mid(p0, a_vg[0:1], a_w_s[0], bst[0], "l0_mid", comm=_ag_forward(g1))
    x1 = _out_proj(xs, y0, wa_out0, "l0_out")
    wb_out = wb_out.reshape(-1, Dm)

    later = [wire(c_w_in[0]), wire(c_w_grp[0]), wire(c_w_out[0]), wire(a_w_in[1]), wire(a_w_out[1])]
    h1, p1, *g2 = _norm_proj(x1, ng(1), wb_in, "l1_proj", comm=_ag_send(later))
    qk, wc_in, wc_grp, wc_out, wa_in1, wa_out1 = _b_qk_fwd(p1, tables, b_gains, "l1_qk", comm=_ag_forward(g2))
    ogs, lgs = zip(*[_b_attn_fwd(qk, p1, g, f"l1_attn{g}") for g in range(ngr)])
    tiles = [_attn_tile(D, xs.shape[0]) for D in B_DILATIONS]
    lgs = [_units_to_tokens(l, D, T) for l, D, T in zip(lgs, B_DILATIONS, tiles)]
    y1, o1, lse = _b_combine(ogs, lgs, p1, "l1_comb")
    x2 = _out_proj(x1, y1, wb_out, "l1_out")
    wc_grp = wc_grp.transpose(1, 0, 2, 3).reshape(ngp, N_DEV * rlc, cgc)
    wc_out = wc_out.reshape(-1, Dm)
    wa_out1 = wa_out1.reshape(-1, Dm)

    h2, p2 = _norm_proj(x2, ng(2), wc_in, "l2_proj")
    y2 = _c_mid(p2, wc_grp, c_sc, "l2_mid")
    x3 = _out_proj(x2, y2, wc_out, "l2_out")
    h3, p3 = _norm_proj(x3, ng(3), wa_in1, "l3_proj")
    y3, = _a_mid(p3, a_vg[1:2], a_w_s[1], bst[1], "l3_mid")
    x4 = _out_proj(x3, y3, wa_out1, "l3_out")
    loss_local, dx4, dx4a = _loss_head(x4, tgt)
    loss = lax.psum(loss_local, ("x", "y", "c"))

    flat3 = lambda g: g.reshape(N_DEV, -1, g.shape[-1])
    dp3, dws1, dbs1, dvg1 = _a_bwd(dx4a, wa_out1, p3, a_vg[1:2], a_w_s[1], bst[1], "l3_bwd")
    grads3 = [_dw_in(h3, dp3, "l3_dwin"), _dw_out(y3, dx4a, "l3_dwout")]
    dx3, dx3a, dg3, *r1_3 = _dh_norm_bwd(dp3, wa_in1, x3, ng(3), dx4, "l3_dh", comm=_rs_sibling(grads3))
    parts3 = _reduce_scatter_adds(coords, grads3, r1_3, "l3_")

    dd, dz, gc_grp, dsc, *r2_3 = _c_bwd1(dx3a, wc_out, p2, wc_grp, c_sc, "l2_bwd1", comm=_rs_chips(parts3))
    dp2 = _c_bwd2(dd, dz, "l2_bwd2")
    grads2 = [_dw_in(h2, dp2, "l2_dwin"), _dw_out(y2, dx3a, "l2_dwout"), flat3(gc_grp)]
    dx2, dx2a, dg2, *r1_2 = _dh_norm_bwd(dp2, wc_in, x2, ng(2), dx3, "l2_dh", comm=_rs_sibling(grads2))
    parts2 = _reduce_scatter_adds(coords, grads2, r1_2, "l2_")

    dov, delta, dp1 = _b_bwd_pre(dx2a, wb_out, o1, p1, "l1_bwdpre")
    dqs, dks, r2_2 = [], [], None
    for g in range(ngr):
        lse_u, delta_u = [_tokens_to_units(a, B_DILATIONS[g], tiles[g]) for a in (lse, delta)]
        dq, dk, dp1, *rest = _b_attn_bwd(qk, p1, dov, lse_u, delta_u, dp1, g, f"l1_attnbwd{g}",
                                         comm=_rs_chips(parts2) if g == 0 else None)
        if g == 0:
            r2_2 = rest
        dqs.append(dq)
        dks.append(dk)
    dp1, dgains = _b_qk_bwd(dqs, dks, p1, tables, b_gains, dp1, "l1_qkbwd")
    grads1 = [_dw_in(h1, dp1, "l1_dwin"), _dw_out(y1, dx2a, "l1_dwout")]
    dx1, dx1a, dg1, *r1_1 = _dh_norm_bwd(dp1, wb_in, x1, ng(1), dx2, "l1_dh", comm=_rs_sibling(grads1))
    parts1 = _reduce_scatter_adds(coords, grads1, r1_1, "l1_")

    dp0, dws0, dbs0, dvg0, *r2_1 = _a_bwd(dx1a, wa_out0, p0, a_vg[0:1], a_w_s[0], bst[0], "l0_bwd", comm=_rs_chips(parts1))
    grads0 = [_dw_in(h0, dp0, "l0_dwin"), _dw_out(y0, dx1a, "l0_dwout")]
    r1_0 = _run_comm(_rs_sibling(grads0), "l0_rs_sibling")
    parts0 = _reduce_scatter_adds(coords, grads0, r1_0, "l0_")
    dx0, _, dg0, *r2_0 = _dh_norm_bwd(dp0, wa_in0, xs, ng(0), dx1, "l0_dh", comm=_rs_chips(parts0))

    small = dict(norm=jnp.concatenate([dg0, dg1, dg2, dg3], axis=0), a_ws=jnp.stack([dws0, dws1]),
                 a_bs=jnp.stack([dbs0.T, dbs1.T]), b_gains=dgains, a_vg=jnp.concatenate([dvg0, dvg1], axis=0), c_sc=dsc)

    order = ["norm", "a_ws", "a_bs", "b_gains", "a_vg", "c_sc"]
    rows = [small[k].reshape(-1, 128) for k in order]
    roff = np.cumsum([0] + [r.shape[0] for r in rows])
    tot = _all_reduce_small(jnp.concatenate(rows, axis=0))
    sm = {k: tot[int(roff[i]):int(roff[i + 1])].reshape(small[k].shape) for i, k in enumerate(order)}
    vl = a_v_gain.shape[1]
    g_small = dict(
        norm_gain=sm["norm"], a_w_s=sm["a_ws"], a_b_s=sm["a_bs"],
        b_q_gain=sm["b_gains"][None, 0:3], b_k_gain=sm["b_gains"][None, 3:6],
        a_v_gain=lax.dynamic_slice_in_dim(sm["a_vg"], dev * vl, vl, axis=1),
        c_scale=lax.dynamic_slice_in_dim(sm["c_sc"], dev * vl, vl, axis=1),
    )

    shares = dict(
        a_w_in=[(grads0[0], r1_0[0], r2_0[0]), (grads3[0], r1_3[0], r2_3[0])],
        a_w_out=[(grads0[1], r1_0[1], r2_0[1]), (grads3[1], r1_3[1], r2_3[1])],
        b_w_in=[(grads1[0], r1_1[0], r2_1[0])], b_w_out=[(grads1[1], r1_1[1], r2_1[1])],
        c_w_in=[(grads2[0], r1_2[0], r2_2[0])], c_w_out=[(grads2[1], r1_2[1], r2_2[1])],
        c_w_grp=[(grads2[2], r1_2[2], r2_2[2])])

    params = dict(a_w_in=a_w_in, a_w_out=a_w_out, b_w_in=b_w_in, b_w_out=b_w_out, c_w_in=c_w_in, c_w_grp=c_w_grp, c_w_out=c_w_out,
                  norm_gain=norm_gain, a_v_gain=a_v_gain, a_w_s=a_w_s, a_b_s=a_b_s, b_q_gain=b_q_gain, b_k_gain=b_k_gain, c_scale=c_scale)
    moms = dict(a_w_in=(m_a_w_in, v_a_w_in), a_w_out=(m_a_w_out, v_a_w_out), b_w_in=(m_b_w_in, v_b_w_in), b_w_out=(m_b_w_out, v_b_w_out),
                c_w_in=(m_c_w_in, v_c_w_in), c_w_grp=(m_c_w_grp, v_c_w_grp), c_w_out=(m_c_w_out, v_c_w_out),
                norm_gain=(m_norm_gain, v_norm_gain), a_v_gain=(m_a_v_gain, v_a_v_gain), a_w_s=(m_a_w_s, v_a_w_s),
                a_b_s=(m_a_b_s, v_a_b_s), b_q_gain=(m_b_q_gain, v_b_q_gain), b_k_gain=(m_b_k_gain, v_b_k_gain),
                c_scale=(m_c_scale, v_c_scale))
    grad, delta, new_m, new_v = {}, {}, {}, {}
    for pname, layers in shares.items():
        w, (m, v) = params[pname], moms[pname]
        C = w.shape[-1]
        per_layer = [_adamw_sharded(coords, w[l].reshape(-1, C), m[l].reshape(-1, C), v[l].reshape(-1, C), g, r1, r2,
                                    f"adamw_{pname}{l}") for l, (g, r1, r2) in enumerate(layers)]
        grad[pname], delta[pname], new_m[pname], new_v[pname] = [
            jnp.stack([o.reshape(w.shape[1:]) for o in outs]) for outs in zip(*per_layer)]
    for pname, g in g_small.items():
        w = params[pname]
        C = w.shape[-1]
        outs = _adamw_small(w.reshape(-1, C), g.reshape(-1, C), moms[pname][0].reshape(-1, C), moms[pname][1].reshape(-1, C),
                            f"adamw_{pname}")
        grad[pname] = g.reshape(w.shape)
        delta[pname], new_m[pname], new_v[pname] = [o.reshape(w.shape) for o in outs]

    wnames = ["norm_gain", "a_w_in", "a_v_gain", "a_w_s", "a_b_s", "a_w_out", "b_w_in", "b_q_gain", "b_k_gain", "b_w_out",
              "c_w_in", "c_w_grp", "c_scale", "c_w_out"]
    return (loss, dx0[None], *[grad[n] for n in wnames], *[delta[n] for n in wnames],
            *[new_m[n] for n in wnames], *[new_v[n] for n in wnames])
```

```python
import functools

import numpy as np
import jax
import jax.numpy as jnp
from jax import lax
from jax.experimental import pallas as pl
from jax.experimental.pallas import tpu as pltpu

F32 = jnp.float32
MXU = jnp.bfloat16
ACT = jnp.bfloat16
WIRE = jnp.bfloat16

EPS = 1e-6
CHUNK = 128
A_GROUPS = 8
HEAD_DIM = 128
B_HEADS = 8
B_DILATIONS = (1, 4, 16)
ROPE_DIM = 32
ROPE_THETA = 500000.0
POOL_SIZES = (2, 4, 8, 16)
POOL_HALO = 16
N_DEV = 8
NEG = -1e30

ADAM_LR, ADAM_B1, ADAM_B2, ADAM_EPS, ADAM_WD, ADAM_STEP = 0.001, 0.9, 0.999, 1e-08, 0.01, 10

VMEM_LIMIT = 56 * 1024 * 1024
MESH = pl.DeviceIdType.MESH


def _cp(*sem):
    return pltpu.CompilerParams(dimension_semantics=sem, vmem_limit_bytes=VMEM_LIMIT)


def _sigmoid(z):
    return 1.0 / (1.0 + jnp.exp(-z))


def _dot(a, b):
    return jnp.dot(a.astype(MXU), b.astype(MXU), preferred_element_type=F32)


def _dot_nt(a, b):
    return lax.dot_general(a.astype(MXU), b.astype(MXU), (((1,), (1,)), ((), ())), preferred_element_type=F32)


def _dot_tn(a, b):
    return lax.dot_general(a.astype(MXU), b.astype(MXU), (((0,), (0,)), ((), ())), preferred_element_type=F32)


def _chunk_slot(d):
    return (d % 2) * 4 + d // 2


class _Comm:
    def __init__(self, inputs, out_shapes, n_remote, make, aliases=None, n_local=1):
        self.inputs = list(inputs)
        self.out_shapes = list(out_shapes)
        self.n_remote = n_remote
        self.n_local = n_local
        self.make = make
        self.aliases = dict(aliases or {})

    def sems(self):
        return [pltpu.SemaphoreType.DMA((self.n_remote,)), pltpu.SemaphoreType.DMA((self.n_remote,)),
                pltpu.SemaphoreType.DMA((self.n_local,))]


_HBM = pl.BlockSpec(memory_space=pl.ANY)


def _launch(body, *, name, grid, in_specs, out_specs, out_shape, args, sem, scratch=(), aliases=None, comm=None):
    in_specs, out_specs, out_shape, scratch = list(in_specs), list(out_specs), list(out_shape), list(scratch)
    aliases = dict(aliases or {})
    if comm is None:
        return pl.pallas_call(body, name=name, grid=grid, in_specs=in_specs, out_specs=out_specs, out_shape=out_shape,
                              scratch_shapes=scratch, input_output_aliases=aliases, compiler_params=_cp(*sem))(*args)
    n_in, n_out, n_sc = len(in_specs), len(out_specs), len(scratch)
    nci, nco = len(comm.inputs), len(comm.out_shapes)

    def hosted(*refs):
        b_in, c_in = refs[:n_in], refs[n_in:n_in + nci]
        o0 = n_in + nci
        b_out, c_out = refs[o0:o0 + n_out], refs[o0 + n_out:o0 + n_out + nco]
        s0 = o0 + n_out + nco
        b_sc, sems = refs[s0:s0 + n_sc], refs[s0 + n_sc:]
        ids = [pl.program_id(a) for a in range(len(grid))]
        first = functools.reduce(jnp.logical_and, [i == 0 for i in ids])
        last = functools.reduce(jnp.logical_and, [i == g - 1 for i, g in zip(ids, grid)])

        @pl.when(first)
        def _():
            for cp in comm.make(c_in, c_out, *sems):
                cp.start()

        body(*b_in, *b_out, *b_sc)

        @pl.when(last)
        def _():
            for cp in comm.make(c_in, c_out, *sems):
                cp.wait()

    for ci, co in comm.aliases.items():
        aliases[n_in + ci] = n_out + co
    return pl.pallas_call(
        hosted, name=name, grid=grid, in_specs=in_specs + [_HBM] * nci, out_specs=out_specs + [_HBM] * nco,
        out_shape=out_shape + comm.out_shapes, scratch_shapes=scratch + comm.sems(),
        input_output_aliases=aliases, compiler_params=_cp(*["arbitrary"] * len(grid)))(*args, *comm.inputs)


def _run_comm(comm, name):
    nci, nco = len(comm.inputs), len(comm.out_shapes)

    def body(*refs):
        cps = comm.make(refs[:nci], refs[nci:nci + nco], *refs[nci + nco:])
        for cp in cps:
            cp.start()
        for cp in cps:
            cp.wait()

    return pl.pallas_call(
        body, name=name, in_specs=[_HBM] * nci, out_specs=[_HBM] * nco, out_shape=comm.out_shapes,
        scratch_shapes=comm.sems(), input_output_aliases=dict(comm.aliases))(*comm.inputs)


def _norm_proj(x, gain, w_dm, name, comm=None):
    M, Dm = x.shape
    nd, _, nl = w_dm.shape
    tm = min(M, 2048)

    def body(x_ref, g_ref, w_ref, h_ref, p_ref):
        @pl.when(pl.program_id(1) == 0)
        def _():
            xv = x_ref[...]
            r = lax.rsqrt(jnp.mean(xv * xv, axis=-1, keepdims=True) + EPS)
            h_ref[...] = (xv * r * g_ref[...]).astype(h_ref.dtype)

        p_ref[...] = _dot(h_ref[...], w_ref[...]).astype(p_ref.dtype)

    return _launch(
        body, name=name, grid=(M // tm, nd),
        in_specs=[pl.BlockSpec((tm, Dm), lambda i, j: (i, 0)),
                  pl.BlockSpec((1, Dm), lambda i, j: (0, 0)),
                  pl.BlockSpec((None, Dm, nl), lambda i, j: (j, 0, 0))],
        out_specs=[pl.BlockSpec((tm, Dm), lambda i, j: (i, 0)),
                   pl.BlockSpec((tm, nl), lambda i, j: (i, j))],
        out_shape=[jax.ShapeDtypeStruct((M, Dm), ACT), jax.ShapeDtypeStruct((M, nd * nl), ACT)],
        args=(x, gain, w_dm), sem=("parallel", "arbitrary"), comm=comm)


def _out_proj(x, y, w, name):
    M, Dm = x.shape
    K = y.shape[1]
    tm = min(M, 1024)

    def body(x_ref, y_ref, w_ref, o_ref):
        o_ref[...] = x_ref[...] + _dot(y_ref[...], w_ref[...])

    return pl.pallas_call(
        body, name=name, grid=(M // tm,),
        in_specs=[pl.BlockSpec((tm, Dm), lambda i: (i, 0)),
                  pl.BlockSpec((tm, K), lambda i: (i, 0)),
                  pl.BlockSpec((K, Dm), lambda i: (0, 0))],
        out_specs=pl.BlockSpec((tm, Dm), lambda i: (i, 0)),
        out_shape=jax.ShapeDtypeStruct((M, Dm), F32),
        compiler_params=_cp("parallel"),
    )(x, y, w)


def _out_proj_loss(x, y, w, target, name):
    M, Dm = x.shape
    K = y.shape[1]
    tm = min(M, 512)

    def body(x_ref, y_ref, w_ref, t_ref, dx_ref, dxa_ref, l_ref):
        @pl.when(pl.program_id(0) == 0)
        def _():
            l_ref[...] = jnp.zeros_like(l_ref)

        err = x_ref[...] + _dot(y_ref[...], w_ref[...]) - t_ref[...]
        dx = err * (1.0 / Dm)
        dx_ref[...] = dx
        dxa_ref[...] = dx.astype(dxa_ref.dtype)
        l_ref[...] += jnp.sum(err * err) * (0.5 / Dm)

    spec = pl.BlockSpec((tm, Dm), lambda i: (i, 0))
    dx, dxa, l = pl.pallas_call(
        body, name=name, grid=(M // tm,),
        in_specs=[spec, pl.BlockSpec((tm, K), lambda i: (i, 0)), pl.BlockSpec((K, Dm), lambda i: (0, 0)), spec],
        out_specs=[spec, spec, pl.BlockSpec((8, 128), lambda i: (0, 0))],
        out_shape=[jax.ShapeDtypeStruct((M, Dm), F32), jax.ShapeDtypeStruct((M, Dm), ACT),
                   jax.ShapeDtypeStruct((8, 128), F32)],
        compiler_params=_cp("arbitrary"),
    )(x, y, w, target)
    return l[0, 0], dx, dxa


def _dw_in(h, dproj, name):
    M, Dm = h.shape
    nl = dproj.shape[1] // N_DEV
    tt = min(M, 2048)

    def body(a_ref, b_ref, o_ref):
        @pl.when(pl.program_id(1) == 0)
        def _():
            o_ref[...] = jnp.zeros_like(o_ref)

        o_ref[...] += _dot_tn(a_ref[...], b_ref[...])

    return pl.pallas_call(
        body, name=name, grid=(N_DEV, M // tt),
        in_specs=[pl.BlockSpec((tt, Dm), lambda j, t: (t, 0)), pl.BlockSpec((tt, nl), lambda j, t: (t, j))],
        out_specs=pl.BlockSpec((None, Dm, nl), lambda j, t: (_chunk_slot(j), 0, 0)),
        out_shape=jax.ShapeDtypeStruct((N_DEV, Dm, nl), F32),
        compiler_params=_cp("parallel", "arbitrary"),
    )(h, dproj)


def _dw_out(y, dout, name):
    M, K = y.shape
    Dm = dout.shape[1]
    kl = K // N_DEV
    tt = min(M, 512)

    def body(a_ref, b_ref, o_ref):
        @pl.when(pl.program_id(0) == 0)
        def _():
            o_ref[...] = jnp.zeros_like(o_ref)

        b = b_ref[...]
        for j in range(N_DEV):
            o_ref[_chunk_slot(j)] += _dot_tn(a_ref[:, j * kl:(j + 1) * kl], b)

    return pl.pallas_call(
        body, name=name, grid=(M // tt,),
        in_specs=[pl.BlockSpec((tt, K), lambda t: (t, 0)), pl.BlockSpec((tt, Dm), lambda t: (t, 0))],
        out_specs=pl.BlockSpec((N_DEV, kl, Dm), lambda t: (0, 0, 0)),
        out_shape=jax.ShapeDtypeStruct((N_DEV, kl, Dm), F32),
        compiler_params=_cp("arbitrary"),
    )(y, dout)


def _dh_norm_bwd(dproj, w_dm, x, gain, dres, name, comm=None):
    M, Dm = x.shape
    nd, _, nl = w_dm.shape
    tm = min(M, 1024)
    rows_bytes = tm * Dm * (4 + 2 * 4 + 2 * 4 + 2 * 4 + 2 * 2)
    block_bytes = 2 * (tm * nl + Dm * nl) * 2
    pair = 2 if rows_bytes + 2 * block_bytes <= VMEM_LIMIT - 8 * 1024 * 1024 else 1
    nj = nd // pair

    def body(dp_ref, w_ref, x_ref, g_ref, dr_ref, dx_ref, dxa_ref, dg_ref, acc_ref):
        i, j = pl.program_id(0), pl.program_id(1)

        @pl.when(j == 0)
        def _():
            acc_ref[...] = jnp.zeros_like(acc_ref)

        acc_ref[...] += functools.reduce(
            lambda a, b: a + b, [_dot_nt(dp_ref[:, d * nl:(d + 1) * nl], w_ref[d]) for d in range(pair)])

        @pl.when(j == nj - 1)
        def _():
            @pl.when(i == 0)
            def _():
                dg_ref[...] = jnp.zeros_like(dg_ref)

            dh = acc_ref[...]
            xv = x_ref[...]
            r = lax.rsqrt(jnp.mean(xv * xv, axis=-1, keepdims=True) + EPS)
            xn = xv * r
            dg_ref[...] += jnp.sum(dh * xn, axis=0, keepdims=True)
            dxn = dh * g_ref[...]
            dx = dr_ref[...] + r * (dxn - xn * jnp.mean(dxn * xn, axis=-1, keepdims=True))
            dx_ref[...] = dx
            dxa_ref[...] = dx.astype(dxa_ref.dtype)

    row = pl.BlockSpec((tm, Dm), lambda i, j: (i, 0))
    return _launch(
        body, name=name, grid=(M // tm, nj),
        in_specs=[pl.BlockSpec((tm, pair * nl), lambda i, j: (i, j)),
                  pl.BlockSpec((pair, Dm, nl), lambda i, j: (j, 0, 0)),
                  row, pl.BlockSpec((1, Dm), lambda i, j: (0, 0)), row],
        out_specs=[row, row, pl.BlockSpec((1, Dm), lambda i, j: (0, 0))],
        out_shape=[jax.ShapeDtypeStruct((M, Dm), F32), jax.ShapeDtypeStruct((M, Dm), ACT),
                   jax.ShapeDtypeStruct((1, Dm), F32)],
        scratch=[pltpu.VMEM((tm, Dm), F32)],
        args=(dproj, w_dm, x, gain, dres), sem=("arbitrary", "arbitrary"), comm=comm)


def _tril_mask():
    return lax.broadcasted_iota(jnp.int32, (CHUNK, CHUNK), 0) >= lax.broadcasted_iota(jnp.int32, (CHUNK, CHUNK), 1)


def _a_mid(proj, v_gain, w_s, b_st, name, comm=None):
    M = proj.shape[0]
    W = proj.shape[1] // 3
    gd = W // A_GROUPS
    tm = min(M, 256)

    def body(p_ref, vg_ref, ws_ref, bs_ref, y_ref):
        pv = p_ref[:, W:2 * W].astype(F32)
        r = lax.rsqrt(jnp.mean(pv * pv, axis=-1, keepdims=True) + EPS)
        v = (pv * r * vg_ref[...]).astype(MXU)
        tri = _tril_mask()
        for g in range(A_GROUPS):
            wg = jnp.where(tri, ws_ref[g], 0.0).astype(MXU)
            bcol = bs_ref[:, g:g + 1]
            for c in range(tm // CHUNK):
                rows, cols = slice(c * CHUNK, (c + 1) * CHUNK), slice(g * gd, (g + 1) * gd)
                mixed = jnp.dot(wg, v[rows, cols], preferred_element_type=F32) + bcol
                u = p_ref[rows, g * gd:(g + 1) * gd].astype(F32)
                z = p_ref[rows, 2 * W + g * gd:2 * W + (g + 1) * gd].astype(F32)
                y_ref[rows, cols] = (u * mixed * (z * _sigmoid(z))).astype(y_ref.dtype)

    return _launch(
        body, name=name, grid=(M // tm,),
        in_specs=[pl.BlockSpec((tm, 3 * W), lambda i: (i, 0)),
                  pl.BlockSpec((1, W), lambda i: (0, 0)),
                  pl.BlockSpec((A_GROUPS, CHUNK, CHUNK), lambda i: (0, 0, 0)),
                  pl.BlockSpec((CHUNK, A_GROUPS), lambda i: (0, 0))],
        out_specs=[pl.BlockSpec((tm, W), lambda i: (i, 0))],
        out_shape=[jax.ShapeDtypeStruct((M, W), ACT)],
        args=(proj, v_gain, w_s, b_st), sem=("parallel",), comm=comm)


def _a_bwd(dout, w_out, proj, v_gain, w_s, b_st, name, comm=None):
    M = proj.shape[0]
    W = proj.shape[1] // 3
    Dm = dout.shape[1]
    gd = W // A_GROUPS
    tm = min(M, 256)
    nt = M // tm

    def body(do_ref, wo_ref, p_ref, vg_ref, ws_ref, bs_ref, dp_ref, dws_ref, dbs_ref, dvg_ref, dv_s):
        i = pl.program_id(0)

        @pl.when(i == 0)
        def _():
            dws_ref[...] = jnp.zeros_like(dws_ref)
            dbs_ref[...] = jnp.zeros_like(dbs_ref)
            dvg_ref[...] = jnp.zeros_like(dvg_ref)

        dy = _dot_nt(do_ref[...], wo_ref[...])
        pv = p_ref[:, W:2 * W].astype(F32)
        r = lax.rsqrt(jnp.mean(pv * pv, axis=-1, keepdims=True) + EPS)
        pvn = pv * r
        vg = vg_ref[...]
        v = (pvn * vg).astype(MXU)
        tri = _tril_mask()
        for g in range(A_GROUPS):
            wf = jnp.where(tri, ws_ref[g], 0.0)
            wg = wf.astype(MXU)
            wgt = wf.T.astype(MXU)
            bcol = bs_ref[:, g:g + 1]
            for c in range(tm // CHUNK):
                rows, cols = slice(c * CHUNK, (c + 1) * CHUNK), slice(g * gd, (g + 1) * gd)
                vb = v[rows, cols]
                mixed = jnp.dot(wg, vb, preferred_element_type=F32) + bcol
                u = p_ref[rows, g * gd:(g + 1) * gd].astype(F32)
                z = p_ref[rows, 2 * W + g * gd:2 * W + (g + 1) * gd].astype(F32)
                sig = _sigmoid(z)
                sz = z * sig
                dyb = dy[rows, cols]
                dp_ref[rows, g * gd:(g + 1) * gd] = (dyb * mixed * sz).astype(dp_ref.dtype)
                dp_ref[rows, 2 * W + g * gd:2 * W + (g + 1) * gd] = (
                    dyb * u * mixed * (sig * (1.0 + z * (1.0 - sig)))).astype(dp_ref.dtype)
                dmix = dyb * u * sz
                dws_ref[g] += _dot_nt(dmix, vb)
                dbs_ref[:, g:g + 1] += jnp.sum(dmix, axis=1, keepdims=True)
                dv_s[rows, cols] = jnp.dot(wgt, dmix.astype(MXU), preferred_element_type=F32)
        dv = dv_s[...]
        dvg_ref[...] += jnp.sum(dv * pvn, axis=0, keepdims=True)
        dpvn = dv * vg
        dp_ref[:, W:2 * W] = (r * (dpvn - pvn * jnp.mean(dpvn * pvn, axis=-1, keepdims=True))).astype(dp_ref.dtype)

        @pl.when(i == nt - 1)
        def _():
            for g in range(A_GROUPS):
                dws_ref[g] = jnp.where(tri, dws_ref[g], 0.0)

    return _launch(
        body, name=name, grid=(nt,),
        in_specs=[pl.BlockSpec((tm, Dm), lambda i: (i, 0)),
                  pl.BlockSpec((W, Dm), lambda i: (0, 0)),
                  pl.BlockSpec((tm, 3 * W), lambda i: (i, 0)),
                  pl.BlockSpec((1, W), lambda i: (0, 0)),
                  pl.BlockSpec((A_GROUPS, CHUNK, CHUNK), lambda i: (0, 0, 0)),
                  pl.BlockSpec((CHUNK, A_GROUPS), lambda i: (0, 0))],
        out_specs=[pl.BlockSpec((tm, 3 * W), lambda i: (i, 0)),
                   pl.BlockSpec((A_GROUPS, CHUNK, CHUNK), lambda i: (0, 0, 0)),
                   pl.BlockSpec((CHUNK, A_GROUPS), lambda i: (0, 0)),
                   pl.BlockSpec((1, W), lambda i: (0, 0))],
        out_shape=[jax.ShapeDtypeStruct((M, 3 * W), ACT),
                   jax.ShapeDtypeStruct((A_GROUPS, CHUNK, CHUNK), F32),
                   jax.ShapeDtypeStruct((CHUNK, A_GROUPS), F32),
                   jax.ShapeDtypeStruct((1, W), F32)],
        scratch=[pltpu.VMEM((tm, W), F32)],
        args=(dout, w_out, proj, v_gain, w_s, b_st), sem=("arbitrary",), comm=comm)


def _pool_diff(xg, tail, i, tm, w):
    t = lax.broadcasted_iota(jnp.int32, (tm, tm + POOL_HALO), 0)
    s = lax.broadcasted_iota(jnp.int32, (tm, tm + POOL_HALO), 1)
    off = t - (s - POOL_HALO)
    band = jnp.where((off >= 0) & (off < w), 1.0, 0.0).astype(MXU)
    tail = jnp.where(i > 0, tail, jnp.zeros_like(tail))
    ext = jnp.concatenate([tail, xg], axis=0)
    ssum = jnp.dot(band, ext.astype(MXU), preferred_element_type=F32)
    tglob = i * tm + lax.broadcasted_iota(jnp.int32, (tm, 1), 0)
    cnt = jnp.minimum(tglob + 1, w).astype(F32)
    return ssum / cnt - xg.astype(F32)


def _c_mid(proj, w_grp, scale, name):
    M = proj.shape[0]
    W = proj.shape[1] // 2
    ng = len(POOL_SIZES)
    cg = W // ng
    tm = min(M, 256)
    hb = tm // POOL_HALO

    def body(xc_ref, tail_ref, z_ref, wg_ref, sc_ref, y_ref):
        i = pl.program_id(0)
        for g, w in enumerate(POOL_SIZES):
            cols = slice(g * cg, (g + 1) * cg)
            d = _pool_diff(xc_ref[:, cols], tail_ref[:, cols], i, tm, w)
            mixed = _dot(d, wg_ref[g]) * sc_ref[:, cols]
            z = z_ref[:, cols].astype(F32)
            y_ref[:, cols] = (mixed * (z * _sigmoid(z))).astype(y_ref.dtype)

    return pl.pallas_call(
        body, name=name, grid=(M // tm,),
        in_specs=[pl.BlockSpec((tm, W), lambda i: (i, 0)),
                  pl.BlockSpec((POOL_HALO, W), lambda i: (jnp.maximum(i * hb - 1, 0), 0)),
                  pl.BlockSpec((tm, W), lambda i: (i, 1)),
                  pl.BlockSpec((ng, cg, cg), lambda i: (0, 0, 0)),
                  pl.BlockSpec((1, W), lambda i: (0, 0))],
        out_specs=pl.BlockSpec((tm, W), lambda i: (i, 0)),
        out_shape=jax.ShapeDtypeStruct((M, W), ACT),
        compiler_params=_cp("parallel"),
    )(proj, proj, proj, w_grp, scale)


def _c_bwd1(dout, w_out, proj, w_grp, scale, name, comm=None):
    M = proj.shape[0]
    W = proj.shape[1] // 2
    Dm = dout.shape[1]
    ng = len(POOL_SIZES)
    cg = W // ng
    rl = cg // N_DEV
    tm = min(M, 256)
    hb = tm // POOL_HALO
    nt = M // tm

    def body(do_ref, wo_ref, xc_ref, tail_ref, z_ref, wg_ref, sc_ref, dd_ref, dz_ref, dwg_ref, dsc_ref, acc_ref):
        i = pl.program_id(0)

        @pl.when(i == 0)
        def _():
            acc_ref[...] = jnp.zeros_like(acc_ref)
            dsc_ref[...] = jnp.zeros_like(dsc_ref)

        dy = _dot_nt(do_ref[...], wo_ref[...])
        for g, w in enumerate(POOL_SIZES):
            cols = slice(g * cg, (g + 1) * cg)
            d = _pool_diff(xc_ref[:, cols], tail_ref[:, cols], i, tm, w)
            mr = _dot(d, wg_ref[g])
            sc = sc_ref[:, cols]
            z = z_ref[:, cols].astype(F32)
            sig = _sigmoid(z)
            dyg = dy[:, cols]
            dmixed = dyg * (z * sig)
            dz_ref[:, cols] = (dyg * (mr * sc) * (sig * (1.0 + z * (1.0 - sig)))).astype(dz_ref.dtype)
            dsc_ref[:, cols] += jnp.sum(dmixed * mr, axis=0, keepdims=True)
            dmr = (dmixed * sc).astype(MXU)
            acc_ref[g] += _dot_tn(d, dmr)
            dd_ref[:, cols] = _dot_nt(dmr, wg_ref[g]).astype(dd_ref.dtype)

        @pl.when(i == nt - 1)
        def _():
            for dev in range(N_DEV):
                for g in range(ng):
                    dwg_ref[_chunk_slot(dev), g] = acc_ref[g, dev * rl:(dev + 1) * rl, :]

    return _launch(
        body, name=name, grid=(nt,),
        in_specs=[pl.BlockSpec((tm, Dm), lambda i: (i, 0)),
                  pl.BlockSpec((W, Dm), lambda i: (0, 0)),
                  pl.BlockSpec((tm, W), lambda i: (i, 0)),
                  pl.BlockSpec((POOL_HALO, W), lambda i: (jnp.maximum(i * hb - 1, 0), 0)),
                  pl.BlockSpec((tm, W), lambda i: (i, 1)),
                  pl.BlockSpec((ng, cg, cg), lambda i: (0, 0, 0)),
                  pl.BlockSpec((1, W), lambda i: (0, 0))],
        out_specs=[pl.BlockSpec((tm, W), lambda i: (i, 0)),
                   pl.BlockSpec((tm, W), lambda i: (i, 0)),
                   pl.BlockSpec((N_DEV, ng, rl, cg), lambda i: (0, 0, 0, 0)),
                   pl.BlockSpec((1, W), lambda i: (0, 0))],
        out_shape=[jax.ShapeDtypeStruct((M, W), ACT), jax.ShapeDtypeStruct((M, W), ACT),
                   jax.ShapeDtypeStruct((N_DEV, ng, rl, cg), F32), jax.ShapeDtypeStruct((1, W), F32)],
        scratch=[pltpu.VMEM((ng, cg, cg), F32)],
        args=(dout, w_out, proj, proj, proj, w_grp, scale), sem=("arbitrary",), comm=comm)


def _c_bwd2(dd, dz, name):
    M, W = dd.shape
    ng = len(POOL_SIZES)
    cg = W // ng
    tm = min(M, 256)
    hb = tm // POOL_HALO
    nt = M // tm

    def body(dd_ref, head_ref, dz_ref, dp_ref):
        i = pl.program_id(0)
        s = lax.broadcasted_iota(jnp.int32, (tm, tm + POOL_HALO), 0)
        t = lax.broadcasted_iota(jnp.int32, (tm, tm + POOL_HALO), 1)
        off = t - s
        tglob = i * tm + lax.broadcasted_iota(jnp.int32, (tm + POOL_HALO, 1), 0)
        for g, w in enumerate(POOL_SIZES):
            cols = slice(g * cg, (g + 1) * cg)
            ddg = dd_ref[:, cols].astype(F32)
            head = head_ref[:, cols].astype(F32)
            head = jnp.where(i < nt - 1, head, jnp.zeros_like(head))
            cnt = jnp.minimum(tglob + 1, w).astype(F32)
            ext = (jnp.concatenate([ddg, head], axis=0) / cnt).astype(MXU)
            band = jnp.where((off >= 0) & (off < w), 1.0, 0.0).astype(MXU)
            dp_ref[:, cols] = (jnp.dot(band, ext, preferred_element_type=F32) - ddg).astype(dp_ref.dtype)
        dp_ref[:, W:] = dz_ref[...]

    return pl.pallas_call(
        body, name=name, grid=(nt,),
        in_specs=[pl.BlockSpec((tm, W), lambda i: (i, 0)),
                  pl.BlockSpec((POOL_HALO, W), lambda i: (jnp.minimum((i + 1) * hb, M // POOL_HALO - 1), 0)),
                  pl.BlockSpec((tm, W), lambda i: (i, 0))],
        out_specs=pl.BlockSpec((tm, 2 * W), lambda i: (i, 0)),
        out_shape=jax.ShapeDtypeStruct((M, 2 * W), ACT),
        compiler_params=_cp("parallel"),
    )(dd, dd, dz)


def _rope_tables(S):
    half = ROPE_DIM // 2
    inv_freq = jnp.power(jnp.float32(ROPE_THETA), -jnp.arange(half, dtype=F32) / half)
    ang = jnp.arange(S, dtype=F32)[:, None] * inv_freq[None, :]
    cos, sin = jnp.cos(ang), jnp.sin(ang)
    rest = HEAD_DIM - ROPE_DIM
    cf = jnp.concatenate([cos, cos, jnp.ones((S, rest), F32)], axis=1)
    sf = jnp.concatenate([-sin, sin, jnp.zeros((S, rest), F32)], axis=1)
    return cf, sf


def _swap_matrix():
    half = ROPE_DIM // 2
    a = lax.broadcasted_iota(jnp.int32, (HEAD_DIM, HEAD_DIM), 0)
    e = lax.broadcasted_iota(jnp.int32, (HEAD_DIM, HEAD_DIM), 1)
    hit = ((e < half) & (a == e + half)) | ((e >= half) & (e < 2 * half) & (a == e - half))
    return jnp.where(hit, 1.0, 0.0).astype(MXU)


def _b_qk_fwd(proj, tables, gains, name, comm=None):
    M = proj.shape[0]
    nsl = 2 * len(B_DILATIONS) * B_HEADS
    Wqk = nsl * HEAD_DIM
    tm = min(M, 256)

    def body(p_ref, cf_ref, sf_ref, g_ref, o_ref):
        cf, sf = cf_ref[...], sf_ref[...]
        swap = _swap_matrix()
        for j in range(nsl):
            cols = slice(j * HEAD_DIM, (j + 1) * HEAD_DIM)
            xv = p_ref[:, cols].astype(F32)
            r = lax.rsqrt(jnp.mean(xv * xv, axis=-1, keepdims=True) + EPS)
            xg = xv * g_ref[j // B_HEADS:j // B_HEADS + 1, :]
            hi = xg.astype(MXU)
            lo = (xg - hi.astype(F32)).astype(MXU)
            sw = jnp.dot(hi, swap, preferred_element_type=F32) + jnp.dot(lo, swap, preferred_element_type=F32)
            o_ref[:, cols] = (r * (xg * cf + sw * sf)).astype(o_ref.dtype)

    tspec = pl.BlockSpec((tm, HEAD_DIM), lambda i: (i, 0))
    return _launch(
        body, name=name, grid=(M // tm,),
        in_specs=[pl.BlockSpec((tm, Wqk), lambda i: (i, 0)), tspec, tspec,
                  pl.BlockSpec((8, HEAD_DIM), lambda i: (0, 0))],
        out_specs=[pl.BlockSpec((tm, Wqk), lambda i: (i, 0))],
        out_shape=[jax.ShapeDtypeStruct((M, Wqk), ACT)],
        args=(proj, *tables, gains), sem=("parallel",), comm=comm)


def _b_qk_bwd(dqs, dks, proj, tables, gains, dproj, name):
    M = proj.shape[0]
    ngr = len(B_DILATIONS)
    nsl = 2 * ngr * B_HEADS
    Wqk = nsl * HEAD_DIM
    Wg = B_HEADS * HEAD_DIM
    tm = min(M, 256)

    def body(*refs):
        d_refs = refs[:2 * ngr]
        p_ref, cf_ref, sf_ref, g_ref = refs[2 * ngr:2 * ngr + 4]
        dp_ref, dg_ref = refs[-2], refs[-1]

        @pl.when(pl.program_id(0) == 0)
        def _():
            dg_ref[...] = jnp.zeros_like(dg_ref)

        cf, sf = cf_ref[...], sf_ref[...]
        swap = _swap_matrix()
        for j in range(nsl):
            t, hh = j // B_HEADS, j % B_HEADS
            cols = slice(j * HEAD_DIM, (j + 1) * HEAD_DIM)
            dy = d_refs[t][:, hh * HEAD_DIM:(hh + 1) * HEAD_DIM].astype(F32)
            dxn = dy * cf + jnp.dot((dy * sf).astype(MXU), swap, preferred_element_type=F32)
            xv = p_ref[:, cols].astype(F32)
            r = lax.rsqrt(jnp.mean(xv * xv, axis=-1, keepdims=True) + EPS)
            xh = xv * r
            dg_ref[t:t + 1, :] += jnp.sum(dxn * xh, axis=0, keepdims=True)
            dxh = dxn * g_ref[t:t + 1, :]
            dp_ref[:, cols] = (r * (dxh - xh * jnp.mean(dxh * xh, axis=-1, keepdims=True))).astype(dp_ref.dtype)

    tspec = pl.BlockSpec((tm, HEAD_DIM), lambda i: (i, 0))
    dspec = pl.BlockSpec((tm, Wg), lambda i: (i, 0))
    n_in = 2 * ngr + 5
    return pl.pallas_call(
        body, name=name, grid=(M // tm,),
        in_specs=[dspec] * (2 * ngr) + [pl.BlockSpec((tm, Wqk), lambda i: (i, 0)), tspec, tspec,
                                        pl.BlockSpec((8, HEAD_DIM), lambda i: (0, 0)),
                                        pl.BlockSpec(memory_space=pl.ANY)],
        out_specs=[pl.BlockSpec((tm, Wqk), lambda i: (i, 0)), pl.BlockSpec((8, HEAD_DIM), lambda i: (0, 0))],
        out_shape=[jax.ShapeDtypeStruct(dproj.shape, dproj.dtype), jax.ShapeDtypeStruct((8, HEAD_DIM), F32)],
        input_output_aliases={n_in - 1: 0},
        compiler_params=_cp("arbitrary"),
    )(*dqs, *dks, proj, *tables, gains, dproj)


def _attn_tile(D, M):
    return max(HEAD_DIM * D, min(M, 2048))


class _TokenRows:
    GROUP = 16

    def __init__(self, D):
        self.D = D
        self.pitch = 24 if D == 16 else self.GROUP

    def rows(self, ntok):
        return ntok // self.GROUP * self.pitch

    def every_dth(self, tok0, n):
        start = tok0 // self.GROUP * self.pitch + tok0 % self.GROUP
        stride = self.D * self.pitch // self.GROUP
        return pl.ds(start, n) if stride == 1 else pl.ds(start, n, stride=stride)

    def put(self, dst, tok0, src_ref, ntok):
        if self.pitch == self.GROUP:
            dst[tok0:tok0 + ntok, :] = src_ref[...].astype(F32)
            return

        def group(i, carry):
            row = pl.multiple_of((tok0 // self.GROUP + i) * self.pitch, 8)
            dst[pl.ds(row, self.GROUP), :] = src_ref[pl.ds(pl.multiple_of(i * self.GROUP, self.GROUP), self.GROUP), :].astype(F32)
            return carry

        lax.fori_loop(0, ntok // self.GROUP, group, 0, unroll=8)

    def get(self, dst_ref, src, ntok):
        if self.pitch == self.GROUP:
            dst_ref[...] = src[0:ntok, :].astype(dst_ref.dtype)
            return

        def group(i, carry):
            row = pl.multiple_of(i * self.pitch, 8)
            dst_ref[pl.ds(pl.multiple_of(i * self.GROUP, self.GROUP), self.GROUP), :] = src[pl.ds(row, self.GROUP), :].astype(dst_ref.dtype)
            return carry

        lax.fori_loop(0, ntok // self.GROUP, group, 0, unroll=8)


def _attn_mask(base):
    qi = lax.broadcasted_iota(jnp.int32, (CHUNK, 2 * CHUNK), 0)
    ki = lax.broadcasted_iota(jnp.int32, (CHUNK, 2 * CHUNK), 1)
    return (ki >= qi) & (ki <= qi + CHUNK) & (ki >= CHUNK - base)


def _b_attn_fwd(qk, proj, g, name):
    M = qk.shape[0]
    D = B_DILATIONS[g]
    ngr = len(B_DILATIONS)
    T = _attn_tile(D, M)
    P = HEAD_DIM * D
    nsb = T // P
    Wg = B_HEADS * HEAD_DIM
    scale = np.float32(1.0 / np.sqrt(HEAD_DIM))

    lay = _TokenRows(D)
    RP, RT = lay.rows(P), lay.rows(T)

    def body(q_ref, k_ref, v_ref, o_ref, l_ref, qs, ks, vs, os_):
        n = pl.program_id(1)

        @pl.when(n == 0)
        def _():
            ks[0:RP, :] = jnp.zeros((RP, HEAD_DIM), F32)
            vs[0:RP, :] = jnp.zeros((RP, HEAD_DIM), F32)

        lay.put(qs, 0, q_ref, T)
        lay.put(ks, P, k_ref, T)
        lay.put(vs, P, v_ref, T)

        for b in range(nsb):
            mask = _attn_mask(n * (T // D) + b * CHUNK)
            for r in range(D):
                start = b * P + r
                q = qs[lay.every_dth(start, CHUNK), :]
                k = ks[lay.every_dth(start, 2 * CHUNK), :]
                v = vs[lay.every_dth(start, 2 * CHUNK), :]
                s = jnp.where(mask, _dot_nt(q, k) * scale, NEG)
                m = jnp.max(s, axis=-1, keepdims=True)
                p = jnp.exp(s - m)
                l = jnp.sum(p, axis=-1, keepdims=True)
                o = _dot(p, v) / l
                os_[lay.every_dth(start, CHUNK), :] = o
                l_ref[:, b * D + r:b * D + r + 1] = m + jnp.log(l)

        lay.get(o_ref, os_, T)
        ks[0:RP, :] = ks[RT:RT + RP, :]
        vs[0:RP, :] = vs[RT:RT + RP, :]

    blk = (T, HEAD_DIM)
    U = nsb * D
    return pl.pallas_call(
        body, name=name, grid=(B_HEADS, M // T),
        in_specs=[pl.BlockSpec(blk, lambda h, n: (n, g * B_HEADS + h)),
                  pl.BlockSpec(blk, lambda h, n: (n, (ngr + g) * B_HEADS + h)),
                  pl.BlockSpec(blk, lambda h, n: (n, (2 * ngr + g) * B_HEADS + h))],
        out_specs=[pl.BlockSpec(blk, lambda h, n: (n, h)), pl.BlockSpec((None, CHUNK, U), lambda h, n: (h, n, 0))],
        out_shape=[jax.ShapeDtypeStruct((M, Wg), ACT), jax.ShapeDtypeStruct((B_HEADS, (M // T) * CHUNK, U), F32)],
        scratch_shapes=[pltpu.VMEM((RT, HEAD_DIM), F32), pltpu.VMEM((RP + RT, HEAD_DIM), F32),
                        pltpu.VMEM((RP + RT, HEAD_DIM), F32), pltpu.VMEM((RT, HEAD_DIM), F32)],
        compiler_params=_cp("parallel", "arbitrary"),
    )(qk, qk, proj)


def _units_to_tokens(a, D, T):
    H = a.shape[0]
    nsb = T // (HEAD_DIM * D)
    return a.reshape(H, -1, CHUNK, nsb, D).transpose(1, 3, 2, 4, 0).reshape(-1, H)


def _tokens_to_units(a, D, T):
    M, H = a.shape
    nsb = T // (HEAD_DIM * D)
    return a.reshape(M // T, nsb, CHUNK, D, H).transpose(4, 0, 2, 1, 3).reshape(H, (M // T) * CHUNK, nsb * D)


def _b_combine(os_, ls, proj, name):
    M, Wg = os_[0].shape
    ngr = len(B_DILATIONS)
    tm = min(M, 512)

    def body(*refs):
        o_refs, l_refs, z_ref = refs[:ngr], refs[ngr:2 * ngr], refs[2 * ngr]
        y_ref, o_ref, lse_ref = refs[2 * ngr + 1:]
        for h in range(B_HEADS):
            cols = slice(h * HEAD_DIM, (h + 1) * HEAD_DIM)
            ls_ = [r[:, h:h + 1] for r in l_refs]
            m = functools.reduce(jnp.maximum, ls_)
            es = [jnp.exp(l - m) for l in ls_]
            tot = functools.reduce(lambda a, b: a + b, es)
            o = functools.reduce(lambda a, b: a + b, [(e / tot) * r[:, cols].astype(F32) for e, r in zip(es, o_refs)])
            z = z_ref[:, cols].astype(F32)
            y_ref[:, cols] = (o * (z * _sigmoid(z))).astype(y_ref.dtype)
            o_ref[:, cols] = o.astype(o_ref.dtype)
            lse_ref[:, h:h + 1] = m + jnp.log(tot)

    spec = pl.BlockSpec((tm, Wg), lambda i: (i, 0))
    hspec = pl.BlockSpec((tm, B_HEADS), lambda i: (i, 0))
    return pl.pallas_call(
        body, name=name, grid=(M // tm,),
        in_specs=[spec] * ngr + [hspec] * ngr + [pl.BlockSpec((tm, Wg), lambda i: (i, 3 * ngr))],
        out_specs=[spec, spec, hspec],
        out_shape=[jax.ShapeDtypeStruct((M, Wg), ACT), jax.ShapeDtypeStruct((M, Wg), ACT),
                   jax.ShapeDtypeStruct((M, B_HEADS), F32)],
        compiler_params=_cp("parallel"),
    )(*os_, *ls, proj)


def _b_bwd_pre(dout, w_out, o, proj, name):
    M, Wg = o.shape
    Dm = dout.shape[1]
    ngr = len(B_DILATIONS)
    tm = min(M, 512)

    def body(do_ref, wo_ref, o_ref, z_ref, dov_ref, dl_ref, dp_ref):
        dy = _dot_nt(do_ref[...], wo_ref[...])
        z = z_ref[...].astype(F32)
        sig = _sigmoid(z)
        ov = o_ref[...].astype(F32)
        dp_ref[...] = (dy * ov * (sig * (1.0 + z * (1.0 - sig)))).astype(dp_ref.dtype)
        dov = dy * (z * sig)
        dov_ref[...] = dov.astype(dov_ref.dtype)
        prod = dov * ov
        for h in range(B_HEADS):
            dl_ref[:, h:h + 1] = jnp.sum(prod[:, h * HEAD_DIM:(h + 1) * HEAD_DIM], axis=-1, keepdims=True)

    spec = pl.BlockSpec((tm, Wg), lambda i: (i, 0))
    zspec = pl.BlockSpec((tm, Wg), lambda i: (i, 3 * ngr))
    return pl.pallas_call(
        body, name=name, grid=(M // tm,),
        in_specs=[pl.BlockSpec((tm, Dm), lambda i: (i, 0)), pl.BlockSpec((Wg, Dm), lambda i: (0, 0)), spec, zspec],
        out_specs=[spec, pl.BlockSpec((tm, B_HEADS), lambda i: (i, 0)), zspec],
        out_shape=[jax.ShapeDtypeStruct((M, Wg), ACT), jax.ShapeDtypeStruct((M, B_HEADS), F32),
                   jax.ShapeDtypeStruct(proj.shape, ACT)],
        compiler_params=_cp("parallel"),
    )(dout, w_out, o, proj)


def _b_attn_bwd(qk, proj, dov, lse, delta, dproj, g, name, comm=None):
    M = qk.shape[0]
    D = B_DILATIONS[g]
    ngr = len(B_DILATIONS)
    T = _attn_tile(D, M)
    P = HEAD_DIM * D
    nsb = T // P
    nt = M // T
    Wg = B_HEADS * HEAD_DIM
    scale = np.float32(1.0 / np.sqrt(HEAD_DIM))
    shift = T - P
    lay = _TokenRows(D)
    RP, RT = lay.rows(P), lay.rows(T)

    def body(q_ref, k_ref, v_ref, do_ref, l_ref, dl_ref, dp_any, dq_ref, dk_ref, dv_ref,
             qs, dos, ks, vs, dqs, dks, dvs):
        n = pl.program_id(1)

        @pl.when(n == 0)
        def _():
            ks[0:RP, :] = jnp.zeros((RP, HEAD_DIM), F32)
            vs[0:RP, :] = jnp.zeros((RP, HEAD_DIM), F32)
            dks[...] = jnp.zeros((2 * RT, HEAD_DIM), F32)
            dvs[...] = jnp.zeros((2 * RT, HEAD_DIM), F32)

        @pl.when(n < nt)
        def _():
            lay.put(qs, 0, q_ref, T)
            lay.put(dos, 0, do_ref, T)
            lay.put(ks, P, k_ref, T)
            lay.put(vs, P, v_ref, T)

            masks = [_attn_mask(n * (T // D) + b * CHUNK) for b in range(nsb)]
            for r in range(D):
                carry_dv = carry_dk = None
                for b in range(nsb):
                    start = b * P + r
                    qsl = lay.every_dth(start, CHUNK)
                    ksl = lay.every_dth(start, 2 * CHUNK)
                    lo = lay.every_dth(start + shift, CHUNK)
                    q = qs[qsl, :]
                    do = dos[qsl, :]
                    k = ks[ksl, :]
                    v = vs[ksl, :]
                    s = _dot_nt(q, k) * scale
                    u = b * D + r
                    p = jnp.where(masks[b], jnp.exp(s - l_ref[:, u:u + 1]), 0.0)
                    dv = _dot_tn(p, do)
                    dp = _dot_nt(do, v)
                    ds = (p * (dp - dl_ref[:, u:u + 1]) * scale).astype(MXU)
                    dqs[qsl, :] = _dot(ds, k)
                    dk = _dot_tn(ds, q)
                    if b == 0:
                        dvs[lo, :] += dv[:CHUNK]
                        dks[lo, :] += dk[:CHUNK]
                    else:
                        dvs[lo, :] = carry_dv + dv[:CHUNK]
                        dks[lo, :] = carry_dk + dk[:CHUNK]
                    carry_dv, carry_dk = dv[CHUNK:], dk[CHUNK:]
                hi = lay.every_dth((nsb - 1) * P + r + shift + P, CHUNK)
                dvs[hi, :] = carry_dv
                dks[hi, :] = carry_dk

        lay.get(dq_ref, dqs, T)
        lay.get(dk_ref, dks, T)
        lay.get(dv_ref, dvs, T)
        dks[0:RT, :] = dks[RT:2 * RT, :]
        dvs[0:RT, :] = dvs[RT:2 * RT, :]
        ks[0:RP, :] = ks[RT:RT + RP, :]
        vs[0:RP, :] = vs[RT:RT + RP, :]

    blk = (T, HEAD_DIM)
    cur = lambda n: jnp.minimum(n, nt - 1)
    prv = lambda n: jnp.maximum(n - 1, 0)
    return _launch(
        body, name=name, grid=(B_HEADS, nt + 1),
        in_specs=[pl.BlockSpec(blk, lambda h, n: (cur(n), g * B_HEADS + h)),
                  pl.BlockSpec(blk, lambda h, n: (cur(n), (ngr + g) * B_HEADS + h)),
                  pl.BlockSpec(blk, lambda h, n: (cur(n), (2 * ngr + g) * B_HEADS + h)),
                  pl.BlockSpec(blk, lambda h, n: (cur(n), h)),
                  pl.BlockSpec((None, CHUNK, nsb * D), lambda h, n: (h, cur(n), 0)),
                  pl.BlockSpec((None, CHUNK, nsb * D), lambda h, n: (h, cur(n), 0)),
                  pl.BlockSpec(memory_space=pl.ANY)],
        out_specs=[pl.BlockSpec(blk, lambda h, n: (cur(n), h)),
                   pl.BlockSpec(blk, lambda h, n: (prv(n), h)),
                   pl.BlockSpec(blk, lambda h, n: (prv(n), (2 * ngr + g) * B_HEADS + h))],
        out_shape=[jax.ShapeDtypeStruct((M, Wg), ACT), jax.ShapeDtypeStruct((M, Wg), ACT),
                   jax.ShapeDtypeStruct(dproj.shape, dproj.dtype)],
        scratch=[pltpu.VMEM((RT, HEAD_DIM), F32)] * 2
        + [pltpu.VMEM((RP + RT, HEAD_DIM), F32)] * 2
        + [pltpu.VMEM((RT, HEAD_DIM), F32)]
        + [pltpu.VMEM((2 * RT, HEAD_DIM), F32)] * 2,
        aliases={6: 2},
        args=(qk, qk, proj, dov, lse, delta, dproj), sem=("parallel", "arbitrary"), comm=comm)


def _coords():
    return lax.axis_index("x"), lax.axis_index("y"), lax.axis_index("c")


def _gather_blocks(x_refs, out_refs, send_sems, recv_sems, local_sems):
    x, y, c = _coords()
    me, sibling = (x, y, c), (x, y, 1 - c)
    chips = [(1 - x, y), (x, 1 - y), (1 - x, 1 - y)]
    arrays = range(len(x_refs))

    def slot(a, px, py, pc):
        return out_refs[a].at[4 * px + 2 * py + pc]

    def copy(a, k, block, to, src=None):
        return _remote(slot(a, *block) if src is None else src, slot(a, *block), send_sems, recv_sems, 7 * a + k, to)

    mine = [pltpu.make_async_copy(x_refs[a], slot(a, *me), local_sems.at[a]) for a in arrays]
    first = [copy(a, 0, me, sibling, src=x_refs[a]) for a in arrays]
    first += [copy(a, 1 + j, me, (*chip, c), src=x_refs[a]) for j, chip in enumerate(chips) for a in arrays]
    for cp in mine + first:
        cp.start()
    passed = []
    for j, chip in enumerate(chips):
        for a in arrays:
            copy(a, 1 + j, (*chip, c), me).wait_recv()
            passed.append(copy(a, 4 + j, (*chip, c), sibling))
            passed[-1].start()
    for a in arrays:
        copy(a, 0, sibling, me).wait_recv()
        for j, chip in enumerate(chips):
            copy(a, 4 + j, (*chip, 1 - c), me).wait_recv()
    for cp in first + passed:
        cp.wait_send()
    for cp in mine:
        cp.wait()


def _all_gather_hbm(arrays, name):
    n = len(arrays)

    def body(*refs):
        _gather_blocks(refs[:n], refs[n:2 * n], *refs[2 * n:])

    return pl.pallas_call(
        body, name=name, in_specs=[_HBM] * n, out_specs=[_HBM] * n,
        out_shape=[jax.ShapeDtypeStruct((N_DEV,) + a.shape, a.dtype) for a in arrays],
        scratch_shapes=[pltpu.SemaphoreType.DMA((7 * n,)), pltpu.SemaphoreType.DMA((7 * n,)),
                        pltpu.SemaphoreType.DMA((n,))],
    )(*arrays)


def _all_reduce_small(part):
    R, C = part.shape

    def body(x_ref, tot_ref, gath, send_sems, recv_sems, local_sems):
        _gather_blocks([x_ref], [gath], send_sems, recv_sems, local_sems)
        acc = gath[0]
        for d in range(1, N_DEV):
            acc = acc + gath[d]
        tot_ref[...] = acc

    return pl.pallas_call(
        body, name="ar_small",
        in_specs=[pl.BlockSpec(memory_space=pltpu.VMEM)],
        out_specs=pl.BlockSpec(memory_space=pltpu.VMEM),
        out_shape=jax.ShapeDtypeStruct((R, C), F32),
        scratch_shapes=[pltpu.VMEM((N_DEV, R, C), F32),
                        pltpu.SemaphoreType.DMA((7,)), pltpu.SemaphoreType.DMA((7,)), pltpu.SemaphoreType.DMA((1,))],
        compiler_params=pltpu.CompilerParams(vmem_limit_bytes=VMEM_LIMIT),
    )(part)


def _remote(src, dst, send_sems, recv_sems, k, peer):
    return pltpu.make_async_remote_copy(src_ref=src, dst_ref=dst, send_sem=send_sems.at[k], recv_sem=recv_sems.at[k],
                                        device_id=peer, device_id_type=MESH)


def _ag_send(arrays):
    n = len(arrays)

    def make(c_in, c_out, send_sems, recv_sems, local_sems):
        x, y, c = _coords()
        peers = [(x, y, 1 - c), (1 - x, y, c), (x, 1 - y, c), (1 - x, 1 - y, c)]
        cps = []
        for a in range(n):
            src, dst = c_in[a], c_out[a].at[4 * x + 2 * y + c]
            cps.append(pltpu.make_async_copy(src, dst, local_sems.at[a]))
            cps += [_remote(src, dst, send_sems, recv_sems, 4 * a + k, peer) for k, peer in enumerate(peers)]
        return cps

    return _Comm(arrays, [jax.ShapeDtypeStruct((N_DEV,) + a.shape, a.dtype) for a in arrays], 4 * n, make, n_local=n)


def _ag_forward(gaths):
    n = len(gaths)

    def make(c_in, c_out, send_sems, recv_sems, local_sems):
        x, y, c = _coords()
        chips = [(1 - x, y), (x, 1 - y), (1 - x, 1 - y)]
        cps = []
        for a in range(n):
            buf = c_out[a]
            cps += [_remote(buf.at[4 * px + 2 * py + c], buf.at[4 * px + 2 * py + c], send_sems, recv_sems, 3 * a + j,
                            (x, y, 1 - c)) for j, (px, py) in enumerate(chips)]
        return cps

    return _Comm(gaths, [jax.ShapeDtypeStruct(g.shape, g.dtype) for g in gaths], 3 * n, make,
                 aliases={a: a for a in range(n)})


def _rs_sibling(grads):
    n = len(grads)

    def make(c_in, c_out, send_sems, recv_sems, local_sem):
        x, y, c = _coords()
        return [_remote(c_in[a].at[pl.ds(4 * (1 - c), 4)], c_out[a], send_sems, recv_sems, a, (x, y, 1 - c))
                for a in range(n)]

    return _Comm(grads, [jax.ShapeDtypeStruct((4,) + g.shape[1:], g.dtype) for g in grads], n, make)


def _rs_chips(parts):
    n = len(parts)

    def make(c_in, c_out, send_sems, recv_sems, local_sem):
        x, y, c = _coords()
        peers = [(x, 1 - y, c), (1 - x, y, c), (1 - x, 1 - y, c)]
        return [_remote(c_in[a].at[k], c_out[a].at[k], send_sems, recv_sems, 3 * a + k, peer)
                for a in range(n) for k, peer in enumerate(peers)]

    return _Comm(parts, [jax.ShapeDtypeStruct(p.shape, p.dtype) for p in parts], 3 * n, make)


def _row_tile(rows, cols):
    tr = min(rows, 1 << int(np.log2((1 << 18) // cols)))
    assert rows % tr == 0
    return tr


def _chip_partials(coords, g, r1, name):
    _, rows, C = g.shape
    tr = _row_tile(rows, C)

    def body(co_ref, g_ref, r_ref, o_ref):
        o_ref[...] = (g_ref[...] + r_ref[...]).astype(o_ref.dtype)

    def chip(k, co):
        return jnp.bitwise_xor(2 * co[0] + co[1], k + 1)

    return pl.pallas_call(
        body, name=name,
        grid_spec=pltpu.PrefetchScalarGridSpec(
            num_scalar_prefetch=1, grid=(3, rows // tr),
            in_specs=[pl.BlockSpec((None, tr, C), lambda k, t, co: (4 * co[2] + chip(k, co), t, 0)),
                      pl.BlockSpec((None, tr, C), lambda k, t, co: (chip(k, co), t, 0))],
            out_specs=pl.BlockSpec((None, tr, C), lambda k, t, co: (k, t, 0))),
        out_shape=jax.ShapeDtypeStruct((3, rows, C), WIRE),
        compiler_params=_cp("parallel", "parallel"),
    )(coords, g, r1)


def _adam_math(w, g, m, v):
    m = ADAM_B1 * m + (1.0 - ADAM_B1) * g
    v = ADAM_B2 * v + (1.0 - ADAM_B2) * (g * g)
    m_hat = m / (1.0 - ADAM_B1 ** ADAM_STEP)
    v_hat = v / (1.0 - ADAM_B2 ** ADAM_STEP)
    delta = -ADAM_LR * (m_hat / (jnp.sqrt(v_hat) + ADAM_EPS) + ADAM_WD * w)
    return delta, m, v


def _adamw_sharded(coords, w, m, v, g, r1, r2, name):
    rows, C = w.shape
    tr = _row_tile(rows, C)

    def body(co_ref, w_ref, m_ref, v_ref, g_ref, r1_ref, r2_ref, go_ref, d_ref, mo_ref, vo_ref):
        grad = g_ref[...] + r1_ref[...]
        for k in range(3):
            grad = grad + r2_ref[k].astype(F32)
        go_ref[...] = grad
        d_ref[...], mo_ref[...], vo_ref[...] = _adam_math(w_ref[...], grad, m_ref[...], v_ref[...])

    spec = pl.BlockSpec((tr, C), lambda t, co: (t, 0))
    return pl.pallas_call(
        body, name=name,
        grid_spec=pltpu.PrefetchScalarGridSpec(
            num_scalar_prefetch=1, grid=(rows // tr,),
            in_specs=[spec, spec, spec,
                      pl.BlockSpec((None, tr, C), lambda t, co: (4 * co[2] + 2 * co[0] + co[1], t, 0)),
                      pl.BlockSpec((None, tr, C), lambda t, co: (2 * co[0] + co[1], t, 0)),
                      pl.BlockSpec((3, tr, C), lambda t, co: (0, t, 0))],
            out_specs=[spec] * 4),
        out_shape=[jax.ShapeDtypeStruct((rows, C), F32)] * 4,
        compiler_params=_cp("parallel"),
    )(coords, w, m, v, g, r1, r2)


def _adamw_small(w, g, m, v, name):
    def body(w_ref, g_ref, m_ref, v_ref, d_ref, mo_ref, vo_ref):
        d_ref[...], mo_ref[...], vo_ref[...] = _adam_math(w_ref[...], g_ref[...], m_ref[...], v_ref[...])

    return pl.pallas_call(
        body, name=name, out_shape=[jax.ShapeDtypeStruct(w.shape, F32)] * 3,
        in_specs=[pl.BlockSpec(memory_space=pltpu.VMEM)] * 4,
        out_specs=[pl.BlockSpec(memory_space=pltpu.VMEM)] * 3,
    )(w, g, m, v)


def _reduce_scatter_adds(coords, grads, r1s, tag):
    return [_chip_partials(coords, g, r, f"rs_add_{tag}{i}") for i, (g, r) in enumerate(zip(grads, r1s))]


def kernel(x, norm_gain, a_w_in, a_v_gain, a_w_s, a_b_s, a_w_out, b_w_in, b_q_gain, b_k_gain, b_w_out, c_w_in, c_w_grp, c_scale, c_w_out, loss_target, m_norm_gain, m_a_w_in, m_a_v_gain, m_a_w_s, m_a_b_s, m_a_w_out, m_b_w_in, m_b_q_gain, m_b_k_gain, m_b_w_out, m_c_w_in, m_c_w_grp, m_c_scale, m_c_w_out, v_norm_gain, v_a_w_in, v_a_v_gain, v_a_w_s, v_a_b_s, v_a_w_out, v_b_w_in, v_b_q_gain, v_b_k_gain, v_b_w_out, v_c_w_in, v_c_w_grp, v_c_scale, v_c_w_out):
    cx, cy, cc = _coords()
    coords = jnp.stack([cx, cy, cc]).astype(jnp.int32)
    dev = 4 * cx + 2 * cy + cc
    Dm = x.shape[2]

    xs, tgt = x[0], loss_target[0]
    tables = _rope_tables(xs.shape[0])
    ng = lambda i: norm_gain[i:i + 1]
    ngr = len(B_DILATIONS)
    bst = [a_b_s[l].T for l in range(2)]
    b_gains = jnp.concatenate([b_q_gain[0], b_k_gain[0], jnp.zeros((2, HEAD_DIM), F32)], axis=0)
    nla, nlb, nlc = a_w_in.shape[2], b_w_in.shape[2], c_w_in.shape[2]
    ngp, rlc, cgc = c_w_grp.shape[1:]
    wire = lambda w: w.astype(WIRE)

    nvg, nsc = a_v_gain.size, c_scale.size
    vec = jnp.concatenate([a_v_gain.reshape(-1), c_scale.reshape(-1), jnp.zeros((1024 - nvg - nsc,), F32)]).reshape(8, 128)
    wa_in0, wa_out0, vecs = _all_gather_hbm([wire(a_w_in[0]), wire(a_w_out[0]), vec], "ag_layer0")
    wa_out0 = wa_out0.reshape(-1, Dm)
    vecs = vecs.reshape(N_DEV, -1)
    a_vg = vecs[:, :nvg].reshape((N_DEV,) + a_v_gain.shape).transpose(1, 0, 2).reshape(a_v_gain.shape[0], -1)
    c_sc = vecs[:, nvg:nvg + nsc].reshape(1, -1)

    h0, p0, *g1 = _norm_proj(xs, ng(0), wa_in0, "l0_proj", comm=_ag_send([wire(b_w_in[0]), wire(b_w_out[0])]))
    y0, wb_in, wb_out = _a_mid(p0, a_vg[0:1], a_w_s[0], bst[0], "l0_mid", comm=_ag_forward(g1))
    x1 = _out_proj(xs, y0, wa_out0, "l0_out")
    wb_out = wb_out.reshape(-1, Dm)

    later = [wire(c_w_in[0]), wire(c_w_grp[0]), wire(c_w_out[0]), wire(a_w_in[1]), wire(a_w_out[1])]
    h1, p1, *g2 = _norm_proj(x1, ng(1), wb_in, "l1_proj", comm=_ag_send(later))
    qk, wc_in, wc_grp, wc_out, wa_in1, wa_out1 = _b_qk_fwd(p1, tables, b_gains, "l1_qk", comm=_ag_forward(g2))
    ogs, lgs = zip(*[_b_attn_fwd(qk, p1, g, f"l1_attn{g}") for g in range(ngr)])
    tiles = [_attn_tile(D, xs.shape[0]) for D in B_DILATIONS]
    lgs = [_units_to_tokens(l, D, T) for l, D, T in zip(lgs, B_DILATIONS, tiles)]
    y1, o1, lse = _b_combine(ogs, lgs, p1, "l1_comb")
    x2 = _out_proj(x1, y1, wb_out, "l1_out")
    wc_grp = wc_grp.transpose(1, 0, 2, 3).reshape(ngp, N_DEV * rlc, cgc)
    wc_out = wc_out.reshape(-1, Dm)
    wa_out1 = wa_out1.reshape(-1, Dm)

    h2, p2 = _norm_proj(x2, ng(2), wc_in, "l2_proj")
    y2 = _c_mid(p2, wc_grp, c_sc, "l2_mid")
    x3 = _out_proj(x2, y2, wc_out, "l2_out")
    h3, p3 = _norm_proj(x3, ng(3), wa_in1, "l3_proj")
    y3, = _a_mid(p3, a_vg[1:2], a_w_s[1], bst[1], "l3_mid")
    loss_local, dx4, dx4a = _out_proj_loss(x3, y3, wa_out1, tgt, "l3_out_loss")
    loss = lax.psum(loss_local, ("x", "y", "c"))

    flat3 = lambda g: g.reshape(N_DEV, -1, g.shape[-1])
    dp3, dws1, dbs1, dvg1 = _a_bwd(dx4a, wa_out1, p3, a_vg[1:2], a_w_s[1], bst[1], "l3_bwd")
    grads3 = [_dw_in(h3, dp3, "l3_dwin"), _dw_out(y3, dx4a, "l3_dwout")]
    dx3, dx3a, dg3, *r1_3 = _dh_norm_bwd(dp3, wa_in1, x3, ng(3), dx4, "l3_dh", comm=_rs_sibling(grads3))
    parts3 = _reduce_scatter_adds(coords, grads3, r1_3, "l3_")

    dd, dz, gc_grp, dsc, *r2_3 = _c_bwd1(dx3a, wc_out, p2, wc_grp, c_sc, "l2_bwd1", comm=_rs_chips(parts3))
    dp2 = _c_bwd2(dd, dz, "l2_bwd2")
    grads2 = [_dw_in(h2, dp2, "l2_dwin"), _dw_out(y2, dx3a, "l2_dwout"), flat3(gc_grp)]
    dx2, dx2a, dg2, *r1_2 = _dh_norm_bwd(dp2, wc_in, x2, ng(2), dx3, "l2_dh", comm=_rs_sibling(grads2))
    parts2 = _reduce_scatter_adds(coords, grads2, r1_2, "l2_")

    dov, delta, dp1 = _b_bwd_pre(dx2a, wb_out, o1, p1, "l1_bwdpre")
    dqs, dks, r2_2 = [], [], None
    for g in range(ngr):
        lse_u, delta_u = [_tokens_to_units(a, B_DILATIONS[g], tiles[g]) for a in (lse, delta)]
        dq, dk, dp1, *rest = _b_attn_bwd(qk, p1, dov, lse_u, delta_u, dp1, g, f"l1_attnbwd{g}",
                                         comm=_rs_chips(parts2) if g == 0 else None)
        if g == 0:
            r2_2 = rest
        dqs.append(dq)
        dks.append(dk)
    dp1, dgains = _b_qk_bwd(dqs, dks, p1, tables, b_gains, dp1, "l1_qkbwd")
    grads1 = [_dw_in(h1, dp1, "l1_dwin"), _dw_out(y1, dx2a, "l1_dwout")]
    dx1, dx1a, dg1, *r1_1 = _dh_norm_bwd(dp1, wb_in, x1, ng(1), dx2, "l1_dh", comm=_rs_sibling(grads1))
    parts1 = _reduce_scatter_adds(coords, grads1, r1_1, "l1_")

    dp0, dws0, dbs0, dvg0, *r2_1 = _a_bwd(dx1a, wa_out0, p0, a_vg[0:1], a_w_s[0], bst[0], "l0_bwd", comm=_rs_chips(parts1))
    grads0 = [_dw_in(h0, dp0, "l0_dwin"), _dw_out(y0, dx1a, "l0_dwout")]
    r1_0 = _run_comm(_rs_sibling(grads0), "l0_rs_sibling")
    parts0 = _reduce_scatter_adds(coords, grads0, r1_0, "l0_")
    dx0, _, dg0, *r2_0 = _dh_norm_bwd(dp0, wa_in0, xs, ng(0), dx1, "l0_dh", comm=_rs_chips(parts0))

    small = dict(norm=jnp.concatenate([dg0, dg1, dg2, dg3], axis=0), a_ws=jnp.stack([dws0, dws1]),
                 a_bs=jnp.stack([dbs0.T, dbs1.T]), b_gains=dgains, a_vg=jnp.concatenate([dvg0, dvg1], axis=0), c_sc=dsc)

    order = ["norm", "a_ws", "a_bs", "b_gains", "a_vg", "c_sc"]
    rows = [small[k].reshape(-1, 128) for k in order]
    roff = np.cumsum([0] + [r.shape[0] for r in rows])
    tot = _all_reduce_small(jnp.concatenate(rows, axis=0))
    sm = {k: tot[int(roff[i]):int(roff[i + 1])].reshape(small[k].shape) for i, k in enumerate(order)}
    vl = a_v_gain.shape[1]
    g_small = dict(
        norm_gain=sm["norm"], a_w_s=sm["a_ws"], a_b_s=sm["a_bs"],
        b_q_gain=sm["b_gains"][None, 0:3], b_k_gain=sm["b_gains"][None, 3:6],
        a_v_gain=lax.dynamic_slice_in_dim(sm["a_vg"], dev * vl, vl, axis=1),
        c_scale=lax.dynamic_slice_in_dim(sm["c_sc"], dev * vl, vl, axis=1),
    )

    shares = dict(
        a_w_in=[(grads0[0], r1_0[0], r2_0[0]), (grads3[0], r1_3[0], r2_3[0])],
        a_w_out=[(grads0[1], r1_0[1], r2_0[1]), (grads3[1], r1_3[1], r2_3[1])],
        b_w_in=[(grads1[0], r1_1[0], r2_1[0])], b_w_out=[(grads1[1], r1_1[1], r2_1[1])],
        c_w_in=[(grads2[0], r1_2[0], r2_2[0])], c_w_out=[(grads2[1], r1_2[1], r2_2[1])],
        c_w_grp=[(grads2[2], r1_2[2], r2_2[2])])

    params = dict(a_w_in=a_w_in, a_w_out=a_w_out, b_w_in=b_w_in, b_w_out=b_w_out, c_w_in=c_w_in, c_w_grp=c_w_grp, c_w_out=c_w_out,
                  norm_gain=norm_gain, a_v_gain=a_v_gain, a_w_s=a_w_s, a_b_s=a_b_s, b_q_gain=b_q_gain, b_k_gain=b_k_gain, c_scale=c_scale)
    moms = dict(a_w_in=(m_a_w_in, v_a_w_in), a_w_out=(m_a_w_out, v_a_w_out), b_w_in=(m_b_w_in, v_b_w_in), b_w_out=(m_b_w_out, v_b_w_out),
                c_w_in=(m_c_w_in, v_c_w_in), c_w_grp=(m_c_w_grp, v_c_w_grp), c_w_out=(m_c_w_out, v_c_w_out),
                norm_gain=(m_norm_gain, v_norm_gain), a_v_gain=(m_a_v_gain, v_a_v_gain), a_w_s=(m_a_w_s, v_a_w_s),
                a_b_s=(m_a_b_s, v_a_b_s), b_q_gain=(m_b_q_gain, v_b_q_gain), b_k_gain=(m_b_k_gain, v_b_k_gain),
                c_scale=(m_c_scale, v_c_scale))
    grad, delta, new_m, new_v = {}, {}, {}, {}
    for pname, layers in shares.items():
        w, (m, v) = params[pname], moms[pname]
        C = w.shape[-1]
        per_layer = [_adamw_sharded(coords, w[l].reshape(-1, C), m[l].reshape(-1, C), v[l].reshape(-1, C), g, r1, r2,
                                    f"adamw_{pname}{l}") for l, (g, r1, r2) in enumerate(layers)]
        grad[pname], delta[pname], new_m[pname], new_v[pname] = [
            jnp.stack([o.reshape(w.shape[1:]) for o in outs]) for outs in zip(*per_layer)]
    for pname, g in g_small.items():
        w = params[pname]
        C = w.shape[-1]
        outs = _adamw_small(w.reshape(-1, C), g.reshape(-1, C), moms[pname][0].reshape(-1, C), moms[pname][1].reshape(-1, C),
                            f"adamw_{pname}")
        grad[pname] = g.reshape(w.shape)
        delta[pname], new_m[pname], new_v[pname] = [o.reshape(w.shape) for o in outs]

    wnames = ["norm_gain", "a_w_in", "a_v_gain", "a_w_s", "a_b_s", "a_w_out", "b_w_in", "b_q_gain", "b_k_gain", "b_w_out",
              "c_w_in", "c_w_grp", "c_scale", "c_w_out"]
    return (loss, dx0[None], *[grad[n] for n in wnames], *[delta[n] for n in wnames],
            *[new_m[n] for n in wnames], *[new_v[n] for n in wnames])
```

```python
import functools

import numpy as np
import jax
import jax.numpy as jnp
from jax import lax
from jax.experimental import pallas as pl
from jax.experimental.pallas import tpu as pltpu

F32 = jnp.float32
MXU = jnp.bfloat16
ACT = jnp.bfloat16
WIRE = jnp.bfloat16

EPS = 1e-6
CHUNK = 128
A_GROUPS = 8
HEAD_DIM = 128
B_HEADS = 8
B_DILATIONS = (1, 4, 16)
ROPE_DIM = 32
ROPE_THETA = 500000.0
POOL_SIZES = (2, 4, 8, 16)
POOL_HALO = 16
N_DEV = 8
NEG = -1e30

ADAM_LR, ADAM_B1, ADAM_B2, ADAM_EPS, ADAM_WD, ADAM_STEP = 0.001, 0.9, 0.999, 1e-08, 0.01, 10

VMEM_LIMIT = 56 * 1024 * 1024
MESH = pl.DeviceIdType.MESH


def _cp(*sem):
    return pltpu.CompilerParams(dimension_semantics=sem, vmem_limit_bytes=VMEM_LIMIT)


def _sigmoid(z):
    return 0.5 * jnp.tanh(0.5 * z) + 0.5


def _dot(a, b):
    return jnp.dot(a.astype(MXU), b.astype(MXU), preferred_element_type=F32)


def _dot_nt(a, b):
    return lax.dot_general(a.astype(MXU), b.astype(MXU), (((1,), (1,)), ((), ())), preferred_element_type=F32)


def _dot_tn(a, b):
    return lax.dot_general(a.astype(MXU), b.astype(MXU), (((0,), (0,)), ((), ())), preferred_element_type=F32)


def _chunk_slot(d):
    return (d % 2) * 4 + d // 2


class _Comm:
    def __init__(self, inputs, out_shapes, n_remote, make, aliases=None, n_local=1):
        self.inputs = list(inputs)
        self.out_shapes = list(out_shapes)
        self.n_remote = n_remote
        self.n_local = n_local
        self.make = make
        self.aliases = dict(aliases or {})

    def sems(self):
        return [pltpu.SemaphoreType.DMA((self.n_remote,)), pltpu.SemaphoreType.DMA((self.n_remote,)),
                pltpu.SemaphoreType.DMA((self.n_local,))]


_HBM = pl.BlockSpec(memory_space=pl.ANY)


def _launch(body, *, name, grid, in_specs, out_specs, out_shape, args, sem, scratch=(), aliases=None, comm=None):
    in_specs, out_specs, out_shape, scratch = list(in_specs), list(out_specs), list(out_shape), list(scratch)
    aliases = dict(aliases or {})
    if comm is None:
        return pl.pallas_call(body, name=name, grid=grid, in_specs=in_specs, out_specs=out_specs, out_shape=out_shape,
                              scratch_shapes=scratch, input_output_aliases=aliases, compiler_params=_cp(*sem))(*args)
    n_in, n_out, n_sc = len(in_specs), len(out_specs), len(scratch)
    nci, nco = len(comm.inputs), len(comm.out_shapes)

    def hosted(*refs):
        b_in, c_in = refs[:n_in], refs[n_in:n_in + nci]
        o0 = n_in + nci
        b_out, c_out = refs[o0:o0 + n_out], refs[o0 + n_out:o0 + n_out + nco]
        s0 = o0 + n_out + nco
        b_sc, sems = refs[s0:s0 + n_sc], refs[s0 + n_sc:]
        ids = [pl.program_id(a) for a in range(len(grid))]
        first = functools.reduce(jnp.logical_and, [i == 0 for i in ids])
        last = functools.reduce(jnp.logical_and, [i == g - 1 for i, g in zip(ids, grid)])

        @pl.when(first)
        def _():
            for cp in comm.make(c_in, c_out, *sems):
                cp.start()

        body(*b_in, *b_out, *b_sc)

        @pl.when(last)
        def _():
            for cp in comm.make(c_in, c_out, *sems):
                cp.wait()

    for ci, co in comm.aliases.items():
        aliases[n_in + ci] = n_out + co
    return pl.pallas_call(
        hosted, name=name, grid=grid, in_specs=in_specs + [_HBM] * nci, out_specs=out_specs + [_HBM] * nco,
        out_shape=out_shape + comm.out_shapes, scratch_shapes=scratch + comm.sems(),
        input_output_aliases=aliases, compiler_params=_cp(*["arbitrary"] * len(grid)))(*args, *comm.inputs)


def _run_comm(comm, name):
    nci, nco = len(comm.inputs), len(comm.out_shapes)

    def body(*refs):
        cps = comm.make(refs[:nci], refs[nci:nci + nco], *refs[nci + nco:])
        for cp in cps:
            cp.start()
        for cp in cps:
            cp.wait()

    return pl.pallas_call(
        body, name=name, in_specs=[_HBM] * nci, out_specs=[_HBM] * nco, out_shape=comm.out_shapes,
        scratch_shapes=comm.sems(), input_output_aliases=dict(comm.aliases))(*comm.inputs)


def _norm_proj(x, gain, w_dm, name, comm=None):
    M, Dm = x.shape
    nd, _, nl = w_dm.shape
    tm = min(M, 2048)

    def body(x_ref, g_ref, w_ref, h_ref, p_ref):
        @pl.when(pl.program_id(1) == 0)
        def _():
            xv = x_ref[...]
            r = lax.rsqrt(jnp.mean(xv * xv, axis=-1, keepdims=True) + EPS)
            h_ref[...] = (xv * r * g_ref[...]).astype(h_ref.dtype)

        p_ref[...] = _dot(h_ref[...], w_ref[...]).astype(p_ref.dtype)

    return _launch(
        body, name=name, grid=(M // tm, nd),
        in_specs=[pl.BlockSpec((tm, Dm), lambda i, j: (i, 0)),
                  pl.BlockSpec((1, Dm), lambda i, j: (0, 0)),
                  pl.BlockSpec((None, Dm, nl), lambda i, j: (j, 0, 0))],
        out_specs=[pl.BlockSpec((tm, Dm), lambda i, j: (i, 0)),
                   pl.BlockSpec((tm, nl), lambda i, j: (i, j))],
        out_shape=[jax.ShapeDtypeStruct((M, Dm), ACT), jax.ShapeDtypeStruct((M, nd * nl), ACT)],
        args=(x, gain, w_dm), sem=("parallel", "arbitrary"), comm=comm)


def _out_proj(x, y, w, name):
    M, Dm = x.shape
    K = y.shape[1]
    tm = min(M, 1024)

    def body(x_ref, y_ref, w_ref, o_ref):
        o_ref[...] = x_ref[...] + _dot(y_ref[...], w_ref[...])

    return pl.pallas_call(
        body, name=name, grid=(M // tm,),
        in_specs=[pl.BlockSpec((tm, Dm), lambda i: (i, 0)),
                  pl.BlockSpec((tm, K), lambda i: (i, 0)),
                  pl.BlockSpec((K, Dm), lambda i: (0, 0))],
        out_specs=pl.BlockSpec((tm, Dm), lambda i: (i, 0)),
        out_shape=jax.ShapeDtypeStruct((M, Dm), F32),
        compiler_params=_cp("parallel"),
    )(x, y, w)


def _out_proj_loss(x, y, w, target, name):
    M, Dm = x.shape
    K = y.shape[1]
    tm = min(M, 512)

    def body(x_ref, y_ref, w_ref, t_ref, dx_ref, dxa_ref, l_ref):
        @pl.when(pl.program_id(0) == 0)
        def _():
            l_ref[...] = jnp.zeros_like(l_ref)

        err = x_ref[...] + _dot(y_ref[...], w_ref[...]) - t_ref[...]
        dx = err * (1.0 / Dm)
        dx_ref[...] = dx
        dxa_ref[...] = dx.astype(dxa_ref.dtype)
        l_ref[...] += jnp.sum(err * err) * (0.5 / Dm)

    spec = pl.BlockSpec((tm, Dm), lambda i: (i, 0))
    dx, dxa, l = pl.pallas_call(
        body, name=name, grid=(M // tm,),
        in_specs=[spec, pl.BlockSpec((tm, K), lambda i: (i, 0)), pl.BlockSpec((K, Dm), lambda i: (0, 0)), spec],
        out_specs=[spec, spec, pl.BlockSpec((8, 128), lambda i: (0, 0))],
        out_shape=[jax.ShapeDtypeStruct((M, Dm), F32), jax.ShapeDtypeStruct((M, Dm), ACT),
                   jax.ShapeDtypeStruct((8, 128), F32)],
        compiler_params=_cp("arbitrary"),
    )(x, y, w, target)
    return l[0, 0], dx, dxa


def _dw_in(h, dproj, name):
    M, Dm = h.shape
    nl = dproj.shape[1] // N_DEV
    tt = min(M, 2048)

    def body(a_ref, b_ref, o_ref):
        @pl.when(pl.program_id(1) == 0)
        def _():
            o_ref[...] = jnp.zeros_like(o_ref)

        o_ref[...] += _dot_tn(a_ref[...], b_ref[...])

    return pl.pallas_call(
        body, name=name, grid=(N_DEV, M // tt),
        in_specs=[pl.BlockSpec((tt, Dm), lambda j, t: (t, 0)), pl.BlockSpec((tt, nl), lambda j, t: (t, j))],
        out_specs=pl.BlockSpec((None, Dm, nl), lambda j, t: (_chunk_slot(j), 0, 0)),
        out_shape=jax.ShapeDtypeStruct((N_DEV, Dm, nl), F32),
        compiler_params=_cp("parallel", "arbitrary"),
    )(h, dproj)


def _dw_out(y, dout, name):
    M, K = y.shape
    Dm = dout.shape[1]
    kl = K // N_DEV
    tt = min(M, 512)

    def body(a_ref, b_ref, o_ref):
        @pl.when(pl.program_id(0) == 0)
        def _():
            o_ref[...] = jnp.zeros_like(o_ref)

        b = b_ref[...]
        for j in range(N_DEV):
            o_ref[_chunk_slot(j)] += _dot_tn(a_ref[:, j * kl:(j + 1) * kl], b)

    return pl.pallas_call(
        body, name=name, grid=(M // tt,),
        in_specs=[pl.BlockSpec((tt, K), lambda t: (t, 0)), pl.BlockSpec((tt, Dm), lambda t: (t, 0))],
        out_specs=pl.BlockSpec((N_DEV, kl, Dm), lambda t: (0, 0, 0)),
        out_shape=jax.ShapeDtypeStruct((N_DEV, kl, Dm), F32),
        compiler_params=_cp("arbitrary"),
    )(y, dout)


def _dh_norm_bwd(dproj, w_dm, x, gain, dres, name, comm=None):
    M, Dm = x.shape
    nd, _, nl = w_dm.shape
    tm = min(M, 1024)
    rows_bytes = tm * Dm * (4 + 2 * 4 + 2 * 4 + 2 * 4 + 2 * 2)
    block_bytes = 2 * (tm * nl + Dm * nl) * 2
    pair = 2 if rows_bytes + 2 * block_bytes <= VMEM_LIMIT - 8 * 1024 * 1024 else 1
    nj = nd // pair

    def body(dp_ref, w_ref, x_ref, g_ref, dr_ref, dx_ref, dxa_ref, dg_ref, acc_ref):
        i, j = pl.program_id(0), pl.program_id(1)

        @pl.when(j == 0)
        def _():
            acc_ref[...] = jnp.zeros_like(acc_ref)

        acc_ref[...] += functools.reduce(
            lambda a, b: a + b, [_dot_nt(dp_ref[:, d * nl:(d + 1) * nl], w_ref[d]) for d in range(pair)])

        @pl.when(j == nj - 1)
        def _():
            @pl.when(i == 0)
            def _():
                dg_ref[...] = jnp.zeros_like(dg_ref)

            dh = acc_ref[...]
            xv = x_ref[...]
            r = lax.rsqrt(jnp.mean(xv * xv, axis=-1, keepdims=True) + EPS)
            xn = xv * r
            dg_ref[...] += jnp.sum(dh * xn, axis=0, keepdims=True)
            dxn = dh * g_ref[...]
            dx = dr_ref[...] + r * (dxn - xn * jnp.mean(dxn * xn, axis=-1, keepdims=True))
            dx_ref[...] = dx
            dxa_ref[...] = dx.astype(dxa_ref.dtype)

    row = pl.BlockSpec((tm, Dm), lambda i, j: (i, 0))
    return _launch(
        body, name=name, grid=(M // tm, nj),
        in_specs=[pl.BlockSpec((tm, pair * nl), lambda i, j: (i, j)),
                  pl.BlockSpec((pair, Dm, nl), lambda i, j: (j, 0, 0)),
                  row, pl.BlockSpec((1, Dm), lambda i, j: (0, 0)), row],
        out_specs=[row, row, pl.BlockSpec((1, Dm), lambda i, j: (0, 0))],
        out_shape=[jax.ShapeDtypeStruct((M, Dm), F32), jax.ShapeDtypeStruct((M, Dm), ACT),
                   jax.ShapeDtypeStruct((1, Dm), F32)],
        scratch=[pltpu.VMEM((tm, Dm), F32)],
        args=(dproj, w_dm, x, gain, dres), sem=("arbitrary", "arbitrary"), comm=comm)


def _tril_mask():
    return lax.broadcasted_iota(jnp.int32, (CHUNK, CHUNK), 0) >= lax.broadcasted_iota(jnp.int32, (CHUNK, CHUNK), 1)


def _a_mid(proj, v_gain, w_s, b_st, name, comm=None):
    M = proj.shape[0]
    W = proj.shape[1] // 3
    gd = W // A_GROUPS
    tm = min(M, 256)

    def body(p_ref, vg_ref, ws_ref, bs_ref, y_ref):
        pv = p_ref[:, W:2 * W].astype(F32)
        r = lax.rsqrt(jnp.mean(pv * pv, axis=-1, keepdims=True) + EPS)
        v = (pv * r * vg_ref[...]).astype(MXU)
        tri = _tril_mask()
        for g in range(A_GROUPS):
            wg = jnp.where(tri, ws_ref[g], 0.0).astype(MXU)
            bcol = bs_ref[:, g:g + 1]
            for c in range(tm // CHUNK):
                rows, cols = slice(c * CHUNK, (c + 1) * CHUNK), slice(g * gd, (g + 1) * gd)
                mixed = jnp.dot(wg, v[rows, cols], preferred_element_type=F32) + bcol
                u = p_ref[rows, g * gd:(g + 1) * gd].astype(F32)
                z = p_ref[rows, 2 * W + g * gd:2 * W + (g + 1) * gd].astype(F32)
                y_ref[rows, cols] = (u * mixed * (z * _sigmoid(z))).astype(y_ref.dtype)

    return _launch(
        body, name=name, grid=(M // tm,),
        in_specs=[pl.BlockSpec((tm, 3 * W), lambda i: (i, 0)),
                  pl.BlockSpec((1, W), lambda i: (0, 0)),
                  pl.BlockSpec((A_GROUPS, CHUNK, CHUNK), lambda i: (0, 0, 0)),
                  pl.BlockSpec((CHUNK, A_GROUPS), lambda i: (0, 0))],
        out_specs=[pl.BlockSpec((tm, W), lambda i: (i, 0))],
        out_shape=[jax.ShapeDtypeStruct((M, W), ACT)],
        args=(proj, v_gain, w_s, b_st), sem=("parallel",), comm=comm)


def _a_bwd(dout, w_out, proj, v_gain, w_s, b_st, name, comm=None):
    M = proj.shape[0]
    W = proj.shape[1] // 3
    Dm = dout.shape[1]
    gd = W // A_GROUPS
    tm = min(M, 256)
    nt = M // tm

    def body(do_ref, wo_ref, p_ref, vg_ref, ws_ref, bs_ref, dp_ref, dws_ref, dbs_ref, dvg_ref, dv_s):
        i = pl.program_id(0)

        @pl.when(i == 0)
        def _():
            dws_ref[...] = jnp.zeros_like(dws_ref)
            dbs_ref[...] = jnp.zeros_like(dbs_ref)
            dvg_ref[...] = jnp.zeros_like(dvg_ref)

        dy = _dot_nt(do_ref[...], wo_ref[...])
        pv = p_ref[:, W:2 * W].astype(F32)
        r = lax.rsqrt(jnp.mean(pv * pv, axis=-1, keepdims=True) + EPS)
        pvn = pv * r
        vg = vg_ref[...]
        v = (pvn * vg).astype(MXU)
        tri = _tril_mask()
        for g in range(A_GROUPS):
            wf = jnp.where(tri, ws_ref[g], 0.0)
            wg = wf.astype(MXU)
            wgt = wf.T.astype(MXU)
            bcol = bs_ref[:, g:g + 1]
            for c in range(tm // CHUNK):
                rows, cols = slice(c * CHUNK, (c + 1) * CHUNK), slice(g * gd, (g + 1) * gd)
                vb = v[rows, cols]
                mixed = jnp.dot(wg, vb, preferred_element_type=F32) + bcol
                u = p_ref[rows, g * gd:(g + 1) * gd].astype(F32)
                z = p_ref[rows, 2 * W + g * gd:2 * W + (g + 1) * gd].astype(F32)
                sig = _sigmoid(z)
                sz = z * sig
                dyb = dy[rows, cols]
                dp_ref[rows, g * gd:(g + 1) * gd] = (dyb * mixed * sz).astype(dp_ref.dtype)
                dp_ref[rows, 2 * W + g * gd:2 * W + (g + 1) * gd] = (
                    dyb * u * mixed * (sig * (1.0 + z * (1.0 - sig)))).astype(dp_ref.dtype)
                dmix = dyb * u * sz
                dws_ref[g] += _dot_nt(dmix, vb)
                dbs_ref[:, g:g + 1] += jnp.sum(dmix, axis=1, keepdims=True)
                dv_s[rows, cols] = jnp.dot(wgt, dmix.astype(MXU), preferred_element_type=F32)
        dv = dv_s[...]
        dvg_ref[...] += jnp.sum(dv * pvn, axis=0, keepdims=True)
        dpvn = dv * vg
        dp_ref[:, W:2 * W] = (r * (dpvn - pvn * jnp.mean(dpvn * pvn, axis=-1, keepdims=True))).astype(dp_ref.dtype)

        @pl.when(i == nt - 1)
        def _():
            for g in range(A_GROUPS):
                dws_ref[g] = jnp.where(tri, dws_ref[g], 0.0)

    return _launch(
        body, name=name, grid=(nt,),
        in_specs=[pl.BlockSpec((tm, Dm), lambda i: (i, 0)),
                  pl.BlockSpec((W, Dm), lambda i: (0, 0)),
                  pl.BlockSpec((tm, 3 * W), lambda i: (i, 0)),
                  pl.BlockSpec((1, W), lambda i: (0, 0)),
                  pl.BlockSpec((A_GROUPS, CHUNK, CHUNK), lambda i: (0, 0, 0)),
                  pl.BlockSpec((CHUNK, A_GROUPS), lambda i: (0, 0))],
        out_specs=[pl.BlockSpec((tm, 3 * W), lambda i: (i, 0)),
                   pl.BlockSpec((A_GROUPS, CHUNK, CHUNK), lambda i: (0, 0, 0)),
                   pl.BlockSpec((CHUNK, A_GROUPS), lambda i: (0, 0)),
                   pl.BlockSpec((1, W), lambda i: (0, 0))],
        out_shape=[jax.ShapeDtypeStruct((M, 3 * W), ACT),
                   jax.ShapeDtypeStruct((A_GROUPS, CHUNK, CHUNK), F32),
                   jax.ShapeDtypeStruct((CHUNK, A_GROUPS), F32),
                   jax.ShapeDtypeStruct((1, W), F32)],
        scratch=[pltpu.VMEM((tm, W), F32)],
        args=(dout, w_out, proj, v_gain, w_s, b_st), sem=("arbitrary",), comm=comm)


def _pool_diff(xg, tail, i, tm, w):
    t = lax.broadcasted_iota(jnp.int32, (tm, tm + POOL_HALO), 0)
    s = lax.broadcasted_iota(jnp.int32, (tm, tm + POOL_HALO), 1)
    off = t - (s - POOL_HALO)
    band = jnp.where((off >= 0) & (off < w), 1.0, 0.0).astype(MXU)
    tail = jnp.where(i > 0, tail, jnp.zeros_like(tail))
    ext = jnp.concatenate([tail, xg], axis=0)
    ssum = jnp.dot(band, ext.astype(MXU), preferred_element_type=F32)
    tglob = i * tm + lax.broadcasted_iota(jnp.int32, (tm, 1), 0)
    cnt = jnp.minimum(tglob + 1, w).astype(F32)
    return ssum / cnt - xg.astype(F32)


def _c_mid(proj, w_grp, scale, name):
    M = proj.shape[0]
    W = proj.shape[1] // 2
    ng = len(POOL_SIZES)
    cg = W // ng
    tm = min(M, 256)
    hb = tm // POOL_HALO

    def body(xc_ref, tail_ref, z_ref, wg_ref, sc_ref, y_ref):
        i = pl.program_id(0)
        for g, w in enumerate(POOL_SIZES):
            cols = slice(g * cg, (g + 1) * cg)
            d = _pool_diff(xc_ref[:, cols], tail_ref[:, cols], i, tm, w)
            mixed = _dot(d, wg_ref[g]) * sc_ref[:, cols]
            z = z_ref[:, cols].astype(F32)
            y_ref[:, cols] = (mixed * (z * _sigmoid(z))).astype(y_ref.dtype)

    return pl.pallas_call(
        body, name=name, grid=(M // tm,),
        in_specs=[pl.BlockSpec((tm, W), lambda i: (i, 0)),
                  pl.BlockSpec((POOL_HALO, W), lambda i: (jnp.maximum(i * hb - 1, 0), 0)),
                  pl.BlockSpec((tm, W), lambda i: (i, 1)),
                  pl.BlockSpec((ng, cg, cg), lambda i: (0, 0, 0)),
                  pl.BlockSpec((1, W), lambda i: (0, 0))],
        out_specs=pl.BlockSpec((tm, W), lambda i: (i, 0)),
        out_shape=jax.ShapeDtypeStruct((M, W), ACT),
        compiler_params=_cp("parallel"),
    )(proj, proj, proj, w_grp, scale)


def _c_bwd1(dout, w_out, proj, w_grp, scale, name, comm=None):
    M = proj.shape[0]
    W = proj.shape[1] // 2
    Dm = dout.shape[1]
    ng = len(POOL_SIZES)
    cg = W // ng
    rl = cg // N_DEV
    tm = min(M, 256)
    hb = tm // POOL_HALO
    nt = M // tm

    def body(do_ref, wo_ref, xc_ref, tail_ref, z_ref, wg_ref, sc_ref, dd_ref, dz_ref, dwg_ref, dsc_ref, acc_ref):
        i = pl.program_id(0)

        @pl.when(i == 0)
        def _():
            acc_ref[...] = jnp.zeros_like(acc_ref)
            dsc_ref[...] = jnp.zeros_like(dsc_ref)

        dy = _dot_nt(do_ref[...], wo_ref[...])
        for g, w in enumerate(POOL_SIZES):
            cols = slice(g * cg, (g + 1) * cg)
            d = _pool_diff(xc_ref[:, cols], tail_ref[:, cols], i, tm, w)
            mr = _dot(d, wg_ref[g])
            sc = sc_ref[:, cols]
            z = z_ref[:, cols].astype(F32)
            sig = _sigmoid(z)
            dyg = dy[:, cols]
            dmixed = dyg * (z * sig)
            dz_ref[:, cols] = (dyg * (mr * sc) * (sig * (1.0 + z * (1.0 - sig)))).astype(dz_ref.dtype)
            dsc_ref[:, cols] += jnp.sum(dmixed * mr, axis=0, keepdims=True)
            dmr = (dmixed * sc).astype(MXU)
            acc_ref[g] += _dot_tn(d, dmr)
            dd_ref[:, cols] = _dot_nt(dmr, wg_ref[g]).astype(dd_ref.dtype)

        @pl.when(i == nt - 1)
        def _():
            for dev in range(N_DEV):
                for g in range(ng):
                    dwg_ref[_chunk_slot(dev), g] = acc_ref[g, dev * rl:(dev + 1) * rl, :]

    return _launch(
        body, name=name, grid=(nt,),
        in_specs=[pl.BlockSpec((tm, Dm), lambda i: (i, 0)),
                  pl.BlockSpec((W, Dm), lambda i: (0, 0)),
                  pl.BlockSpec((tm, W), lambda i: (i, 0)),
                  pl.BlockSpec((POOL_HALO, W), lambda i: (jnp.maximum(i * hb - 1, 0), 0)),
                  pl.BlockSpec((tm, W), lambda i: (i, 1)),
                  pl.BlockSpec((ng, cg, cg), lambda i: (0, 0, 0)),
                  pl.BlockSpec((1, W), lambda i: (0, 0))],
        out_specs=[pl.BlockSpec((tm, W), lambda i: (i, 0)),
                   pl.BlockSpec((tm, W), lambda i: (i, 0)),
                   pl.BlockSpec((N_DEV, ng, rl, cg), lambda i: (0, 0, 0, 0)),
                   pl.BlockSpec((1, W), lambda i: (0, 0))],
        out_shape=[jax.ShapeDtypeStruct((M, W), ACT), jax.ShapeDtypeStruct((M, W), ACT),
                   jax.ShapeDtypeStruct((N_DEV, ng, rl, cg), F32), jax.ShapeDtypeStruct((1, W), F32)],
        scratch=[pltpu.VMEM((ng, cg, cg), F32)],
        args=(dout, w_out, proj, proj, proj, w_grp, scale), sem=("arbitrary",), comm=comm)


def _c_bwd2(dd, dz, name):
    M, W = dd.shape
    ng = len(POOL_SIZES)
    cg = W // ng
    tm = min(M, 256)
    hb = tm // POOL_HALO
    nt = M // tm

    def body(dd_ref, head_ref, dz_ref, dp_ref):
        i = pl.program_id(0)
        s = lax.broadcasted_iota(jnp.int32, (tm, tm + POOL_HALO), 0)
        t = lax.broadcasted_iota(jnp.int32, (tm, tm + POOL_HALO), 1)
        off = t - s
        tglob = i * tm + lax.broadcasted_iota(jnp.int32, (tm + POOL_HALO, 1), 0)
        for g, w in enumerate(POOL_SIZES):
            cols = slice(g * cg, (g + 1) * cg)
            ddg = dd_ref[:, cols].astype(F32)
            head = head_ref[:, cols].astype(F32)
            head = jnp.where(i < nt - 1, head, jnp.zeros_like(head))
            cnt = jnp.minimum(tglob + 1, w).astype(F32)
            ext = (jnp.concatenate([ddg, head], axis=0) / cnt).astype(MXU)
            band = jnp.where((off >= 0) & (off < w), 1.0, 0.0).astype(MXU)
            dp_ref[:, cols] = (jnp.dot(band, ext, preferred_element_type=F32) - ddg).astype(dp_ref.dtype)
        dp_ref[:, W:] = dz_ref[...]

    return pl.pallas_call(
        body, name=name, grid=(nt,),
        in_specs=[pl.BlockSpec((tm, W), lambda i: (i, 0)),
                  pl.BlockSpec((POOL_HALO, W), lambda i: (jnp.minimum((i + 1) * hb, M // POOL_HALO - 1), 0)),
                  pl.BlockSpec((tm, W), lambda i: (i, 0))],
        out_specs=pl.BlockSpec((tm, 2 * W), lambda i: (i, 0)),
        out_shape=jax.ShapeDtypeStruct((M, 2 * W), ACT),
        compiler_params=_cp("parallel"),
    )(dd, dd, dz)


def _rope_tables(S):
    half = ROPE_DIM // 2
    inv_freq = jnp.power(jnp.float32(ROPE_THETA), -jnp.arange(half, dtype=F32) / half)
    ang = jnp.arange(S, dtype=F32)[:, None] * inv_freq[None, :]
    cos, sin = jnp.cos(ang), jnp.sin(ang)
    rest = HEAD_DIM - ROPE_DIM
    cf = jnp.concatenate([cos, cos, jnp.ones((S, rest), F32)], axis=1)
    sf = jnp.concatenate([-sin, sin, jnp.zeros((S, rest), F32)], axis=1)
    return cf, sf


def _swap_matrix():
    half = ROPE_DIM // 2
    a = lax.broadcasted_iota(jnp.int32, (HEAD_DIM, HEAD_DIM), 0)
    e = lax.broadcasted_iota(jnp.int32, (HEAD_DIM, HEAD_DIM), 1)
    hit = ((e < half) & (a == e + half)) | ((e >= half) & (e < 2 * half) & (a == e - half))
    return jnp.where(hit, 1.0, 0.0).astype(MXU)


def _b_qk_fwd(proj, tables, gains, name, comm=None):
    M = proj.shape[0]
    nsl = 2 * len(B_DILATIONS) * B_HEADS
    Wqk = nsl * HEAD_DIM
    tm = min(M, 256)

    def body(p_ref, cf_ref, sf_ref, g_ref, o_ref):
        cf, sf = cf_ref[...], sf_ref[...]
        swap = _swap_matrix()
        for j in range(nsl):
            cols = slice(j * HEAD_DIM, (j + 1) * HEAD_DIM)
            xv = p_ref[:, cols].astype(F32)
            r = lax.rsqrt(jnp.mean(xv * xv, axis=-1, keepdims=True) + EPS)
            xg = xv * g_ref[j // B_HEADS:j // B_HEADS + 1, :]
            hi = xg.astype(MXU)
            lo = (xg - hi.astype(F32)).astype(MXU)
            sw = jnp.dot(hi, swap, preferred_element_type=F32) + jnp.dot(lo, swap, preferred_element_type=F32)
            o_ref[:, cols] = (r * (xg * cf + sw * sf)).astype(o_ref.dtype)

    tspec = pl.BlockSpec((tm, HEAD_DIM), lambda i: (i, 0))
    return _launch(
        body, name=name, grid=(M // tm,),
        in_specs=[pl.BlockSpec((tm, Wqk), lambda i: (i, 0)), tspec, tspec,
                  pl.BlockSpec((8, HEAD_DIM), lambda i: (0, 0))],
        out_specs=[pl.BlockSpec((tm, Wqk), lambda i: (i, 0))],
        out_shape=[jax.ShapeDtypeStruct((M, Wqk), ACT)],
        args=(proj, *tables, gains), sem=("parallel",), comm=comm)


def _b_qk_bwd(dqs, dks, proj, tables, gains, dproj, name):
    M = proj.shape[0]
    ngr = len(B_DILATIONS)
    nsl = 2 * ngr * B_HEADS
    Wqk = nsl * HEAD_DIM
    Wg = B_HEADS * HEAD_DIM
    tm = min(M, 256)

    def body(*refs):
        d_refs = refs[:2 * ngr]
        p_ref, cf_ref, sf_ref, g_ref = refs[2 * ngr:2 * ngr + 4]
        dp_ref, dg_ref = refs[-2], refs[-1]

        @pl.when(pl.program_id(0) == 0)
        def _():
            dg_ref[...] = jnp.zeros_like(dg_ref)

        cf, sf = cf_ref[...], sf_ref[...]
        swap = _swap_matrix()
        for j in range(nsl):
            t, hh = j // B_HEADS, j % B_HEADS
            cols = slice(j * HEAD_DIM, (j + 1) * HEAD_DIM)
            dy = d_refs[t][:, hh * HEAD_DIM:(hh + 1) * HEAD_DIM].astype(F32)
            dxn = dy * cf + jnp.dot((dy * sf).astype(MXU), swap, preferred_element_type=F32)
            xv = p_ref[:, cols].astype(F32)
            r = lax.rsqrt(jnp.mean(xv * xv, axis=-1, keepdims=True) + EPS)
            xh = xv * r
            dg_ref[t:t + 1, :] += jnp.sum(dxn * xh, axis=0, keepdims=True)
            dxh = dxn * g_ref[t:t + 1, :]
            dp_ref[:, cols] = (r * (dxh - xh * jnp.mean(dxh * xh, axis=-1, keepdims=True))).astype(dp_ref.dtype)

    tspec = pl.BlockSpec((tm, HEAD_DIM), lambda i: (i, 0))
    dspec = pl.BlockSpec((tm, Wg), lambda i: (i, 0))
    n_in = 2 * ngr + 5
    return pl.pallas_call(
        body, name=name, grid=(M // tm,),
        in_specs=[dspec] * (2 * ngr) + [pl.BlockSpec((tm, Wqk), lambda i: (i, 0)), tspec, tspec,
                                        pl.BlockSpec((8, HEAD_DIM), lambda i: (0, 0)),
                                        pl.BlockSpec(memory_space=pl.ANY)],
        out_specs=[pl.BlockSpec((tm, Wqk), lambda i: (i, 0)), pl.BlockSpec((8, HEAD_DIM), lambda i: (0, 0))],
        out_shape=[jax.ShapeDtypeStruct(dproj.shape, dproj.dtype), jax.ShapeDtypeStruct((8, HEAD_DIM), F32)],
        input_output_aliases={n_in - 1: 0},
        compiler_params=_cp("arbitrary"),
    )(*dqs, *dks, proj, *tables, gains, dproj)


def _attn_tile(D, M):
    return max(HEAD_DIM * D, min(M, 2048))


class _TokenRows:
    GROUP = 16

    def __init__(self, D):
        self.D = D
        self.pitch = 24 if D == 16 else self.GROUP

    def rows(self, ntok):
        return ntok // self.GROUP * self.pitch

    def every_dth(self, tok0, n):
        start = tok0 // self.GROUP * self.pitch + tok0 % self.GROUP
        stride = self.D * self.pitch // self.GROUP
        return pl.ds(start, n) if stride == 1 else pl.ds(start, n, stride=stride)

    def put(self, dst, tok0, src_ref, ntok):
        if self.pitch == self.GROUP:
            dst[tok0:tok0 + ntok, :] = src_ref[...].astype(F32)
            return

        def group(i, carry):
            row = pl.multiple_of((tok0 // self.GROUP + i) * self.pitch, 8)
            dst[pl.ds(row, self.GROUP), :] = src_ref[pl.ds(pl.multiple_of(i * self.GROUP, self.GROUP), self.GROUP), :].astype(F32)
            return carry

        lax.fori_loop(0, ntok // self.GROUP, group, 0, unroll=8)

    def get(self, dst_ref, src, ntok):
        if self.pitch == self.GROUP:
            dst_ref[...] = src[0:ntok, :].astype(dst_ref.dtype)
            return

        def group(i, carry):
            row = pl.multiple_of(i * self.pitch, 8)
            dst_ref[pl.ds(pl.multiple_of(i * self.GROUP, self.GROUP), self.GROUP), :] = src[pl.ds(row, self.GROUP), :].astype(dst_ref.dtype)
            return carry

        lax.fori_loop(0, ntok // self.GROUP, group, 0, unroll=8)


def _attn_mask(base):
    qi = lax.broadcasted_iota(jnp.int32, (CHUNK, 2 * CHUNK), 0)
    ki = lax.broadcasted_iota(jnp.int32, (CHUNK, 2 * CHUNK), 1)
    return (ki >= qi) & (ki <= qi + CHUNK) & (ki >= CHUNK - base)


def _b_attn_fwd(qk, proj, g, name):
    M = qk.shape[0]
    D = B_DILATIONS[g]
    ngr = len(B_DILATIONS)
    T = _attn_tile(D, M)
    P = HEAD_DIM * D
    nsb = T // P
    Wg = B_HEADS * HEAD_DIM
    scale = np.float32(1.0 / np.sqrt(HEAD_DIM))

    lay = _TokenRows(D)
    RP, RT = lay.rows(P), lay.rows(T)

    def body(q_ref, k_ref, v_ref, o_ref, l_ref, qs, ks, vs, os_):
        n = pl.program_id(1)

        @pl.when(n == 0)
        def _():
            ks[0:RP, :] = jnp.zeros((RP, HEAD_DIM), F32)
            vs[0:RP, :] = jnp.zeros((RP, HEAD_DIM), F32)

        lay.put(qs, 0, q_ref, T)
        lay.put(ks, P, k_ref, T)
        lay.put(vs, P, v_ref, T)

        for b in range(nsb):
            mask = _attn_mask(n * (T // D) + b * CHUNK)
            for r in range(D):
                start = b * P + r
                q = qs[lay.every_dth(start, CHUNK), :]
                k = ks[lay.every_dth(start, 2 * CHUNK), :]
                v = vs[lay.every_dth(start, 2 * CHUNK), :]
                s = jnp.where(mask, _dot_nt(q, k) * scale, NEG)
                m = jnp.max(s, axis=-1, keepdims=True)
                p = jnp.exp(s - m)
                l = jnp.sum(p, axis=-1, keepdims=True)
                o = _dot(p, v) / l
                os_[lay.every_dth(start, CHUNK), :] = o
                l_ref[:, b * D + r:b * D + r + 1] = m + jnp.log(l)

        lay.get(o_ref, os_, T)
        ks[0:RP, :] = ks[RT:RT + RP, :]
        vs[0:RP, :] = vs[RT:RT + RP, :]

    blk = (T, HEAD_DIM)
    U = nsb * D
    return pl.pallas_call(
        body, name=name, grid=(B_HEADS, M // T),
        in_specs=[pl.BlockSpec(blk, lambda h, n: (n, g * B_HEADS + h)),
                  pl.BlockSpec(blk, lambda h, n: (n, (ngr + g) * B_HEADS + h)),
                  pl.BlockSpec(blk, lambda h, n: (n, (2 * ngr + g) * B_HEADS + h))],
        out_specs=[pl.BlockSpec(blk, lambda h, n: (n, h)), pl.BlockSpec((None, CHUNK, U), lambda h, n: (h, n, 0))],
        out_shape=[jax.ShapeDtypeStruct((M, Wg), ACT), jax.ShapeDtypeStruct((B_HEADS, (M // T) * CHUNK, U), F32)],
        scratch_shapes=[pltpu.VMEM((RT, HEAD_DIM), F32), pltpu.VMEM((RP + RT, HEAD_DIM), F32),
                        pltpu.VMEM((RP + RT, HEAD_DIM), F32), pltpu.VMEM((RT, HEAD_DIM), F32)],
        compiler_params=_cp("parallel", "arbitrary"),
    )(qk, qk, proj)


def _units_to_tokens(a, D, T):
    H = a.shape[0]
    nsb = T // (HEAD_DIM * D)
    return a.reshape(H, -1, CHUNK, nsb, D).transpose(1, 3, 2, 4, 0).reshape(-1, H)


def _tokens_to_units(a, D, T):
    M, H = a.shape
    nsb = T // (HEAD_DIM * D)
    return a.reshape(M // T, nsb, CHUNK, D, H).transpose(4, 0, 2, 1, 3).reshape(H, (M // T) * CHUNK, nsb * D)


def _b_combine(os_, ls, proj, name):
    M, Wg = os_[0].shape
    ngr = len(B_DILATIONS)
    tm = min(M, 512)

    def body(*refs):
        o_refs, l_refs, z_ref = refs[:ngr], refs[ngr:2 * ngr], refs[2 * ngr]
        y_ref, o_ref, lse_ref = refs[2 * ngr + 1:]
        for h in range(B_HEADS):
            cols = slice(h * HEAD_DIM, (h + 1) * HEAD_DIM)
            ls_ = [r[:, h:h + 1] for r in l_refs]
            m = functools.reduce(jnp.maximum, ls_)
            es = [jnp.exp(l - m) for l in ls_]
            tot = functools.reduce(lambda a, b: a + b, es)
            o = functools.reduce(lambda a, b: a + b, [(e / tot) * r[:, cols].astype(F32) for e, r in zip(es, o_refs)])
            z = z_ref[:, cols].astype(F32)
            y_ref[:, cols] = (o * (z * _sigmoid(z))).astype(y_ref.dtype)
            o_ref[:, cols] = o.astype(o_ref.dtype)
            lse_ref[:, h:h + 1] = m + jnp.log(tot)

    spec = pl.BlockSpec((tm, Wg), lambda i: (i, 0))
    hspec = pl.BlockSpec((tm, B_HEADS), lambda i: (i, 0))
    return pl.pallas_call(
        body, name=name, grid=(M // tm,),
        in_specs=[spec] * ngr + [hspec] * ngr + [pl.BlockSpec((tm, Wg), lambda i: (i, 3 * ngr))],
        out_specs=[spec, spec, hspec],
        out_shape=[jax.ShapeDtypeStruct((M, Wg), ACT), jax.ShapeDtypeStruct((M, Wg), ACT),
                   jax.ShapeDtypeStruct((M, B_HEADS), F32)],
        compiler_params=_cp("parallel"),
    )(*os_, *ls, proj)


def _b_bwd_pre(dout, w_out, o, proj, name):
    M, Wg = o.shape
    Dm = dout.shape[1]
    ngr = len(B_DILATIONS)
    tm = min(M, 512)

    def body(do_ref, wo_ref, o_ref, z_ref, dov_ref, dl_ref, dp_ref):
        dy = _dot_nt(do_ref[...], wo_ref[...])
        z = z_ref[...].astype(F32)
        sig = _sigmoid(z)
        ov = o_ref[...].astype(F32)
        dp_ref[...] = (dy * ov * (sig * (1.0 + z * (1.0 - sig)))).astype(dp_ref.dtype)
        dov = dy * (z * sig)
        dov_ref[...] = dov.astype(dov_ref.dtype)
        prod = dov * ov
        for h in range(B_HEADS):
            dl_ref[:, h:h + 1] = jnp.sum(prod[:, h * HEAD_DIM:(h + 1) * HEAD_DIM], axis=-1, keepdims=True)

    spec = pl.BlockSpec((tm, Wg), lambda i: (i, 0))
    zspec = pl.BlockSpec((tm, Wg), lambda i: (i, 3 * ngr))
    return pl.pallas_call(
        body, name=name, grid=(M // tm,),
        in_specs=[pl.BlockSpec((tm, Dm), lambda i: (i, 0)), pl.BlockSpec((Wg, Dm), lambda i: (0, 0)), spec, zspec],
        out_specs=[spec, pl.BlockSpec((tm, B_HEADS), lambda i: (i, 0)), zspec],
        out_shape=[jax.ShapeDtypeStruct((M, Wg), ACT), jax.ShapeDtypeStruct((M, B_HEADS), F32),
                   jax.ShapeDtypeStruct(proj.shape, ACT)],
        compiler_params=_cp("parallel"),
    )(dout, w_out, o, proj)


def _b_attn_bwd(qk, proj, dov, lse, delta, dproj, g, name, comm=None):
    M = qk.shape[0]
    D = B_DILATIONS[g]
    ngr = len(B_DILATIONS)
    T = _attn_tile(D, M)
    P = HEAD_DIM * D
    nsb = T // P
    nt = M // T
    Wg = B_HEADS * HEAD_DIM
    scale = np.float32(1.0 / np.sqrt(HEAD_DIM))
    shift = T - P
    lay = _TokenRows(D)
    RP, RT = lay.rows(P), lay.rows(T)

    def body(q_ref, k_ref, v_ref, do_ref, l_ref, dl_ref, dp_any, dq_ref, dk_ref, dv_ref,
             qs, dos, ks, vs, dqs, dks, dvs):
        n = pl.program_id(1)

        @pl.when(n == 0)
        def _():
            ks[0:RP, :] = jnp.zeros((RP, HEAD_DIM), F32)
            vs[0:RP, :] = jnp.zeros((RP, HEAD_DIM), F32)
            dks[...] = jnp.zeros((2 * RT, HEAD_DIM), F32)
            dvs[...] = jnp.zeros((2 * RT, HEAD_DIM), F32)

        @pl.when(n < nt)
        def _():
            lay.put(qs, 0, q_ref, T)
            lay.put(dos, 0, do_ref, T)
            lay.put(ks, P, k_ref, T)
            lay.put(vs, P, v_ref, T)

            masks = [_attn_mask(n * (T // D) + b * CHUNK) for b in range(nsb)]
            for r in range(D):
                carry_dv = carry_dk = None
                for b in range(nsb):
                    start = b * P + r
                    qsl = lay.every_dth(start, CHUNK)
                    ksl = lay.every_dth(start, 2 * CHUNK)
                    lo = lay.every_dth(start + shift, CHUNK)
                    q = qs[qsl, :]
                    do = dos[qsl, :]
                    k = ks[ksl, :]
                    v = vs[ksl, :]
                    s = _dot_nt(q, k) * scale
                    u = b * D + r
                    p = jnp.where(masks[b], jnp.exp(s - l_ref[:, u:u + 1]), 0.0)
                    dv = _dot_tn(p, do)
                    dp = _dot_nt(do, v)
                    ds = (p * (dp - dl_ref[:, u:u + 1]) * scale).astype(MXU)
                    dqs[qsl, :] = _dot(ds, k)
                    dk = _dot_tn(ds, q)
                    if b == 0:
                        dvs[lo, :] += dv[:CHUNK]
                        dks[lo, :] += dk[:CHUNK]
                    else:
                        dvs[lo, :] = carry_dv + dv[:CHUNK]
                        dks[lo, :] = carry_dk + dk[:CHUNK]
                    carry_dv, carry_dk = dv[CHUNK:], dk[CHUNK:]
                hi = lay.every_dth((nsb - 1) * P + r + shift + P, CHUNK)
                dvs[hi, :] = carry_dv
                dks[hi, :] = carry_dk

        lay.get(dq_ref, dqs, T)
        lay.get(dk_ref, dks, T)
        lay.get(dv_ref, dvs, T)
        dks[0:RT, :] = dks[RT:2 * RT, :]
        dvs[0:RT, :] = dvs[RT:2 * RT, :]
        ks[0:RP, :] = ks[RT:RT + RP, :]
        vs[0:RP, :] = vs[RT:RT + RP, :]

    blk = (T, HEAD_DIM)
    cur = lambda n: jnp.minimum(n, nt - 1)
    prv = lambda n: jnp.maximum(n - 1, 0)
    return _launch(
        body, name=name, grid=(B_HEADS, nt + 1),
        in_specs=[pl.BlockSpec(blk, lambda h, n: (cur(n), g * B_HEADS + h)),
                  pl.BlockSpec(blk, lambda h, n: (cur(n), (ngr + g) * B_HEADS + h)),
                  pl.BlockSpec(blk, lambda h, n: (cur(n), (2 * ngr + g) * B_HEADS + h)),
                  pl.BlockSpec(blk, lambda h, n: (cur(n), h)),
                  pl.BlockSpec((None, CHUNK, nsb * D), lambda h, n: (h, cur(n), 0)),
                  pl.BlockSpec((None, CHUNK, nsb * D), lambda h, n: (h, cur(n), 0)),
                  pl.BlockSpec(memory_space=pl.ANY)],
        out_specs=[pl.BlockSpec(blk, lambda h, n: (cur(n), h)),
                   pl.BlockSpec(blk, lambda h, n: (prv(n), h)),
                   pl.BlockSpec(blk, lambda h, n: (prv(n), (2 * ngr + g) * B_HEADS + h))],
        out_shape=[jax.ShapeDtypeStruct((M, Wg), ACT), jax.ShapeDtypeStruct((M, Wg), ACT),
                   jax.ShapeDtypeStruct(dproj.shape, dproj.dtype)],
        scratch=[pltpu.VMEM((RT, HEAD_DIM), F32)] * 2
        + [pltpu.VMEM((RP + RT, HEAD_DIM), F32)] * 2
        + [pltpu.VMEM((RT, HEAD_DIM), F32)]
        + [pltpu.VMEM((2 * RT, HEAD_DIM), F32)] * 2,
        aliases={6: 2},
        args=(qk, qk, proj, dov, lse, delta, dproj), sem=("parallel", "arbitrary"), comm=comm)


def _coords():
    return lax.axis_index("x"), lax.axis_index("y"), lax.axis_index("c")


def _gather_blocks(x_refs, out_refs, send_sems, recv_sems, local_sems):
    x, y, c = _coords()
    me, sibling = (x, y, c), (x, y, 1 - c)
    chips = [(1 - x, y), (x, 1 - y), (1 - x, 1 - y)]
    arrays = range(len(x_refs))

    def slot(a, px, py, pc):
        return out_refs[a].at[4 * px + 2 * py + pc]

    def copy(a, k, block, to, src=None):
        return _remote(slot(a, *block) if src is None else src, slot(a, *block), send_sems, recv_sems, 7 * a + k, to)

    mine = [pltpu.make_async_copy(x_refs[a], slot(a, *me), local_sems.at[a]) for a in arrays]
    first = [copy(a, 0, me, sibling, src=x_refs[a]) for a in arrays]
    first += [copy(a, 1 + j, me, (*chip, c), src=x_refs[a]) for j, chip in enumerate(chips) for a in arrays]
    for cp in mine + first:
        cp.start()
    passed = []
    for j, chip in enumerate(chips):
        for a in arrays:
            copy(a, 1 + j, (*chip, c), me).wait_recv()
            passed.append(copy(a, 4 + j, (*chip, c), sibling))
            passed[-1].start()
    for a in arrays:
        copy(a, 0, sibling, me).wait_recv()
        for j, chip in enumerate(chips):
            copy(a, 4 + j, (*chip, 1 - c), me).wait_recv()
    for cp in first + passed:
        cp.wait_send()
    for cp in mine:
        cp.wait()


def _all_gather_hbm(arrays, name):
    n = len(arrays)

    def body(*refs):
        _gather_blocks(refs[:n], refs[n:2 * n], *refs[2 * n:])

    return pl.pallas_call(
        body, name=name, in_specs=[_HBM] * n, out_specs=[_HBM] * n,
        out_shape=[jax.ShapeDtypeStruct((N_DEV,) + a.shape, a.dtype) for a in arrays],
        scratch_shapes=[pltpu.SemaphoreType.DMA((7 * n,)), pltpu.SemaphoreType.DMA((7 * n,)),
                        pltpu.SemaphoreType.DMA((n,))],
    )(*arrays)


def _all_reduce_small(part):
    R, C = part.shape

    def body(x_ref, tot_ref, gath, send_sems, recv_sems, local_sems):
        _gather_blocks([x_ref], [gath], send_sems, recv_sems, local_sems)
        acc = gath[0]
        for d in range(1, N_DEV):
            acc = acc + gath[d]
        tot_ref[...] = acc

    return pl.pallas_call(
        body, name="ar_small",
        in_specs=[pl.BlockSpec(memory_space=pltpu.VMEM)],
        out_specs=pl.BlockSpec(memory_space=pltpu.VMEM),
        out_shape=jax.ShapeDtypeStruct((R, C), F32),
        scratch_shapes=[pltpu.VMEM((N_DEV, R, C), F32),
                        pltpu.SemaphoreType.DMA((7,)), pltpu.SemaphoreType.DMA((7,)), pltpu.SemaphoreType.DMA((1,))],
        compiler_params=pltpu.CompilerParams(vmem_limit_bytes=VMEM_LIMIT),
    )(part)


def _remote(src, dst, send_sems, recv_sems, k, peer):
    return pltpu.make_async_remote_copy(src_ref=src, dst_ref=dst, send_sem=send_sems.at[k], recv_sem=recv_sems.at[k],
                                        device_id=peer, device_id_type=MESH)


def _ag_send(arrays):
    n = len(arrays)

    def make(c_in, c_out, send_sems, recv_sems, local_sems):
        x, y, c = _coords()
        peers = [(x, y, 1 - c), (1 - x, y, c), (x, 1 - y, c), (1 - x, 1 - y, c)]
        cps = []
        for a in range(n):
            src, dst = c_in[a], c_out[a].at[4 * x + 2 * y + c]
            cps.append(pltpu.make_async_copy(src, dst, local_sems.at[a]))
            cps += [_remote(src, dst, send_sems, recv_sems, 4 * a + k, peer) for k, peer in enumerate(peers)]
        return cps

    return _Comm(arrays, [jax.ShapeDtypeStruct((N_DEV,) + a.shape, a.dtype) for a in arrays], 4 * n, make, n_local=n)


def _ag_forward(gaths):
    n = len(gaths)

    def make(c_in, c_out, send_sems, recv_sems, local_sems):
        x, y, c = _coords()
        chips = [(1 - x, y), (x, 1 - y), (1 - x, 1 - y)]
        cps = []
        for a in range(n):
            buf = c_out[a]
            cps += [_remote(buf.at[4 * px + 2 * py + c], buf.at[4 * px + 2 * py + c], send_sems, recv_sems, 3 * a + j,
                            (x, y, 1 - c)) for j, (px, py) in enumerate(chips)]
        return cps

    return _Comm(gaths, [jax.ShapeDtypeStruct(g.shape, g.dtype) for g in gaths], 3 * n, make,
                 aliases={a: a for a in range(n)})


def _rs_sibling(grads):
    n = len(grads)

    def make(c_in, c_out, send_sems, recv_sems, local_sem):
        x, y, c = _coords()
        return [_remote(c_in[a].at[pl.ds(4 * (1 - c), 4)], c_out[a], send_sems, recv_sems, a, (x, y, 1 - c))
                for a in range(n)]

    return _Comm(grads, [jax.ShapeDtypeStruct((4,) + g.shape[1:], g.dtype) for g in grads], n, make)


def _rs_chips(parts):
    n = len(parts)

    def make(c_in, c_out, send_sems, recv_sems, local_sem):
        x, y, c = _coords()
        peers = [(x, 1 - y, c), (1 - x, y, c), (1 - x, 1 - y, c)]
        return [_remote(c_in[a].at[k], c_out[a].at[k], send_sems, recv_sems, 3 * a + k, peer)
                for a in range(n) for k, peer in enumerate(peers)]

    return _Comm(parts, [jax.ShapeDtypeStruct(p.shape, p.dtype) for p in parts], 3 * n, make)


def _row_tile(rows, cols):
    tr = min(rows, 1 << int(np.log2((1 << 18) // cols)))
    assert rows % tr == 0
    return tr


def _chip_partials(coords, g, r1, name):
    _, rows, C = g.shape
    tr = _row_tile(rows, C)

    def body(co_ref, g_ref, r_ref, o_ref):
        o_ref[...] = (g_ref[...] + r_ref[...]).astype(o_ref.dtype)

    def chip(k, co):
        return jnp.bitwise_xor(2 * co[0] + co[1], k + 1)

    return pl.pallas_call(
        body, name=name,
        grid_spec=pltpu.PrefetchScalarGridSpec(
            num_scalar_prefetch=1, grid=(3, rows // tr),
            in_specs=[pl.BlockSpec((None, tr, C), lambda k, t, co: (4 * co[2] + chip(k, co), t, 0)),
                      pl.BlockSpec((None, tr, C), lambda k, t, co: (chip(k, co), t, 0))],
            out_specs=pl.BlockSpec((None, tr, C), lambda k, t, co: (k, t, 0))),
        out_shape=jax.ShapeDtypeStruct((3, rows, C), WIRE),
        compiler_params=_cp("parallel", "parallel"),
    )(coords, g, r1)


def _adam_math(w, g, m, v):
    m = ADAM_B1 * m + (1.0 - ADAM_B1) * g
    v = ADAM_B2 * v + (1.0 - ADAM_B2) * (g * g)
    m_hat = m / (1.0 - ADAM_B1 ** ADAM_STEP)
    v_hat = v / (1.0 - ADAM_B2 ** ADAM_STEP)
    delta = -ADAM_LR * (m_hat / (jnp.sqrt(v_hat) + ADAM_EPS) + ADAM_WD * w)
    return delta, m, v


def _adamw_sharded(coords, w, m, v, g, r1, r2, name):
    rows, C = w.shape
    tr = _row_tile(rows, C)

    def body(co_ref, w_ref, m_ref, v_ref, g_ref, r1_ref, r2_ref, go_ref, d_ref, mo_ref, vo_ref):
        grad = g_ref[...] + r1_ref[...]
        for k in range(3):
            grad = grad + r2_ref[k].astype(F32)
        go_ref[...] = grad
        d_ref[...], mo_ref[...], vo_ref[...] = _adam_math(w_ref[...], grad, m_ref[...], v_ref[...])

    spec = pl.BlockSpec((tr, C), lambda t, co: (t, 0))
    return pl.pallas_call(
        body, name=name,
        grid_spec=pltpu.PrefetchScalarGridSpec(
            num_scalar_prefetch=1, grid=(rows // tr,),
            in_specs=[spec, spec, spec,
                      pl.BlockSpec((None, tr, C), lambda t, co: (4 * co[2] + 2 * co[0] + co[1], t, 0)),
                      pl.BlockSpec((None, tr, C), lambda t, co: (2 * co[0] + co[1], t, 0)),
                      pl.BlockSpec((3, tr, C), lambda t, co: (0, t, 0))],
            out_specs=[spec] * 4),
        out_shape=[jax.ShapeDtypeStruct((rows, C), F32)] * 4,
        compiler_params=_cp("parallel"),
    )(coords, w, m, v, g, r1, r2)


def _adamw_small(w, g, m, v, name):
    def body(w_ref, g_ref, m_ref, v_ref, d_ref, mo_ref, vo_ref):
        d_ref[...], mo_ref[...], vo_ref[...] = _adam_math(w_ref[...], g_ref[...], m_ref[...], v_ref[...])

    return pl.pallas_call(
        body, name=name, out_shape=[jax.ShapeDtypeStruct(w.shape, F32)] * 3,
        in_specs=[pl.BlockSpec(memory_space=pltpu.VMEM)] * 4,
        out_specs=[pl.BlockSpec(memory_space=pltpu.VMEM)] * 3,
    )(w, g, m, v)


def _reduce_scatter_adds(coords, grads, r1s, tag):
    return [_chip_partials(coords, g, r, f"rs_add_{tag}{i}") for i, (g, r) in enumerate(zip(grads, r1s))]


def kernel(x, norm_gain, a_w_in, a_v_gain, a_w_s, a_b_s, a_w_out, b_w_in, b_q_gain, b_k_gain, b_w_out, c_w_in, c_w_grp, c_scale, c_w_out, loss_target, m_norm_gain, m_a_w_in, m_a_v_gain, m_a_w_s, m_a_b_s, m_a_w_out, m_b_w_in, m_b_q_gain, m_b_k_gain, m_b_w_out, m_c_w_in, m_c_w_grp, m_c_scale, m_c_w_out, v_norm_gain, v_a_w_in, v_a_v_gain, v_a_w_s, v_a_b_s, v_a_w_out, v_b_w_in, v_b_q_gain, v_b_k_gain, v_b_w_out, v_c_w_in, v_c_w_grp, v_c_scale, v_c_w_out):
    cx, cy, cc = _coords()
    coords = jnp.stack([cx, cy, cc]).astype(jnp.int32)
    dev = 4 * cx + 2 * cy + cc
    Dm = x.shape[2]

    xs, tgt = x[0], loss_target[0]
    tables = _rope_tables(xs.shape[0])
    ng = lambda i: norm_gain[i:i + 1]
    ngr = len(B_DILATIONS)
    bst = [a_b_s[l].T for l in range(2)]
    b_gains = jnp.concatenate([b_q_gain[0], b_k_gain[0], jnp.zeros((2, HEAD_DIM), F32)], axis=0)
    nla, nlb, nlc = a_w_in.shape[2], b_w_in.shape[2], c_w_in.shape[2]
    ngp, rlc, cgc = c_w_grp.shape[1:]
    wire = lambda w: w.astype(WIRE)

    nvg, nsc = a_v_gain.size, c_scale.size
    vec = jnp.concatenate([a_v_gain.reshape(-1), c_scale.reshape(-1), jnp.zeros((1024 - nvg - nsc,), F32)]).reshape(8, 128)
    wa_in0, vecs = _all_gather_hbm([wire(a_w_in[0]), vec], "ag_layer0")
    vecs = vecs.reshape(N_DEV, -1)
    a_vg = vecs[:, :nvg].reshape((N_DEV,) + a_v_gain.shape).transpose(1, 0, 2).reshape(a_v_gain.shape[0], -1)
    c_sc = vecs[:, nvg:nvg + nsc].reshape(1, -1)

    soon = [wire(a_w_out[0]), wire(b_w_in[0]), wire(b_w_out[0])]
    h0, p0, *g1 = _norm_proj(xs, ng(0), wa_in0, "l0_proj", comm=_ag_send(soon))
    y0, wa_out0, wb_in, wb_out = _a_mid(p0, a_vg[0:1], a_w_s[0], bst[0], "l0_mid", comm=_ag_forward(g1))
    x1 = _out_proj(xs, y0, wa_out0.reshape(-1, Dm), "l0_out")
    wa_out0 = wa_out0.reshape(-1, Dm)
    wb_out = wb_out.reshape(-1, Dm)

    later = [wire(c_w_in[0]), wire(c_w_grp[0]), wire(c_w_out[0]), wire(a_w_in[1]), wire(a_w_out[1])]
    h1, p1, *g2 = _norm_proj(x1, ng(1), wb_in, "l1_proj", comm=_ag_send(later))
    qk, wc_in, wc_grp, wc_out, wa_in1, wa_out1 = _b_qk_fwd(p1, tables, b_gains, "l1_qk", comm=_ag_forward(g2))
    ogs, lgs = zip(*[_b_attn_fwd(qk, p1, g, f"l1_attn{g}") for g in range(ngr)])
    tiles = [_attn_tile(D, xs.shape[0]) for D in B_DILATIONS]
    lgs = [_units_to_tokens(l, D, T) for l, D, T in zip(lgs, B_DILATIONS, tiles)]
    y1, o1, lse = _b_combine(ogs, lgs, p1, "l1_comb")
    x2 = _out_proj(x1, y1, wb_out, "l1_out")
    wc_grp = wc_grp.transpose(1, 0, 2, 3).reshape(ngp, N_DEV * rlc, cgc)
    wc_out = wc_out.reshape(-1, Dm)
    wa_out1 = wa_out1.reshape(-1, Dm)

    h2, p2 = _norm_proj(x2, ng(2), wc_in, "l2_proj")
    y2 = _c_mid(p2, wc_grp, c_sc, "l2_mid")
    x3 = _out_proj(x2, y2, wc_out, "l2_out")
    h3, p3 = _norm_proj(x3, ng(3), wa_in1, "l3_proj")
    y3, = _a_mid(p3, a_vg[1:2], a_w_s[1], bst[1], "l3_mid")
    loss_local, dx4, dx4a = _out_proj_loss(x3, y3, wa_out1, tgt, "l3_out_loss")
    loss = lax.psum(loss_local, ("x", "y", "c"))

    flat3 = lambda g: g.reshape(N_DEV, -1, g.shape[-1])
    dp3, dws1, dbs1, dvg1 = _a_bwd(dx4a, wa_out1, p3, a_vg[1:2], a_w_s[1], bst[1], "l3_bwd")
    grads3 = [_dw_in(h3, dp3, "l3_dwin"), _dw_out(y3, dx4a, "l3_dwout")]
    dx3, dx3a, dg3, *r1_3 = _dh_norm_bwd(dp3, wa_in1, x3, ng(3), dx4, "l3_dh", comm=_rs_sibling(grads3))
    parts3 = _reduce_scatter_adds(coords, grads3, r1_3, "l3_")

    dd, dz, gc_grp, dsc, *r2_3 = _c_bwd1(dx3a, wc_out, p2, wc_grp, c_sc, "l2_bwd1", comm=_rs_chips(parts3))
    dp2 = _c_bwd2(dd, dz, "l2_bwd2")
    grads2 = [_dw_in(h2, dp2, "l2_dwin"), _dw_out(y2, dx3a, "l2_dwout"), flat3(gc_grp)]
    dx2, dx2a, dg2, *r1_2 = _dh_norm_bwd(dp2, wc_in, x2, ng(2), dx3, "l2_dh", comm=_rs_sibling(grads2))
    parts2 = _reduce_scatter_adds(coords, grads2, r1_2, "l2_")

    dov, delta, dp1 = _b_bwd_pre(dx2a, wb_out, o1, p1, "l1_bwdpre")
    dqs, dks, r2_2 = [], [], None
    for g in range(ngr):
        lse_u, delta_u = [_tokens_to_units(a, B_DILATIONS[g], tiles[g]) for a in (lse, delta)]
        dq, dk, dp1, *rest = _b_attn_bwd(qk, p1, dov, lse_u, delta_u, dp1, g, f"l1_attnbwd{g}",
                                         comm=_rs_chips(parts2) if g == 0 else None)
        if g == 0:
            r2_2 = rest
        dqs.append(dq)
        dks.append(dk)
    dp1, dgains = _b_qk_bwd(dqs, dks, p1, tables, b_gains, dp1, "l1_qkbwd")
    grads1 = [_dw_in(h1, dp1, "l1_dwin"), _dw_out(y1, dx2a, "l1_dwout")]
    dx1, dx1a, dg1, *r1_1 = _dh_norm_bwd(dp1, wb_in, x1, ng(1), dx2, "l1_dh", comm=_rs_sibling(grads1))
    parts1 = _reduce_scatter_adds(coords, grads1, r1_1, "l1_")

    dp0, dws0, dbs0, dvg0, *r2_1 = _a_bwd(dx1a, wa_out0, p0, a_vg[0:1], a_w_s[0], bst[0], "l0_bwd", comm=_rs_chips(parts1))
    grads0 = [_dw_in(h0, dp0, "l0_dwin"), _dw_out(y0, dx1a, "l0_dwout")]
    r1_0 = _run_comm(_rs_sibling(grads0), "l0_rs_sibling")
    parts0 = _reduce_scatter_adds(coords, grads0, r1_0, "l0_")
    dx0, _, dg0, *r2_0 = _dh_norm_bwd(dp0, wa_in0, xs, ng(0), dx1, "l0_dh", comm=_rs_chips(parts0))

    small = dict(norm=jnp.concatenate([dg0, dg1, dg2, dg3], axis=0), a_ws=jnp.stack([dws0, dws1]),
                 a_bs=jnp.stack([dbs0.T, dbs1.T]), b_gains=dgains, a_vg=jnp.concatenate([dvg0, dvg1], axis=0), c_sc=dsc)

    order = ["norm", "a_ws", "a_bs", "b_gains", "a_vg", "c_sc"]
    rows = [small[k].reshape(-1, 128) for k in order]
    roff = np.cumsum([0] + [r.shape[0] for r in rows])
    tot = _all_reduce_small(jnp.concatenate(rows, axis=0))
    sm = {k: tot[int(roff[i]):int(roff[i + 1])].reshape(small[k].shape) for i, k in enumerate(order)}
    vl = a_v_gain.shape[1]
    g_small = dict(
        norm_gain=sm["norm"], a_w_s=sm["a_ws"], a_b_s=sm["a_bs"],
        b_q_gain=sm["b_gains"][None, 0:3], b_k_gain=sm["b_gains"][None, 3:6],
        a_v_gain=lax.dynamic_slice_in_dim(sm["a_vg"], dev * vl, vl, axis=1),
        c_scale=lax.dynamic_slice_in_dim(sm["c_sc"], dev * vl, vl, axis=1),
    )

    shares = dict(
        a_w_in=[(grads0[0], r1_0[0], r2_0[0]), (grads3[0], r1_3[0], r2_3[0])],
        a_w_out=[(grads0[1], r1_0[1], r2_0[1]), (grads3[1], r1_3[1], r2_3[1])],
        b_w_in=[(grads1[0], r1_1[0], r2_1[0])], b_w_out=[(grads1[1], r1_1[1], r2_1[1])],
        c_w_in=[(grads2[0], r1_2[0], r2_2[0])], c_w_out=[(grads2[1], r1_2[1], r2_2[1])],
        c_w_grp=[(grads2[2], r1_2[2], r2_2[2])])

    params = dict(a_w_in=a_w_in, a_w_out=a_w_out, b_w_in=b_w_in, b_w_out=b_w_out, c_w_in=c_w_in, c_w_grp=c_w_grp, c_w_out=c_w_out,
                  norm_gain=norm_gain, a_v_gain=a_v_gain, a_w_s=a_w_s, a_b_s=a_b_s, b_q_gain=b_q_gain, b_k_gain=b_k_gain, c_scale=c_scale)
    moms = dict(a_w_in=(m_a_w_in, v_a_w_in), a_w_out=(m_a_w_out, v_a_w_out), b_w_in=(m_b_w_in, v_b_w_in), b_w_out=(m_b_w_out, v_b_w_out),
                c_w_in=(m_c_w_in, v_c_w_in), c_w_grp=(m_c_w_grp, v_c_w_grp), c_w_out=(m_c_w_out, v_c_w_out),
                norm_gain=(m_norm_gain, v_norm_gain), a_v_gain=(m_a_v_gain, v_a_v_gain), a_w_s=(m_a_w_s, v_a_w_s),
                a_b_s=(m_a_b_s, v_a_b_s), b_q_gain=(m_b_q_gain, v_b_q_gain), b_k_gain=(m_b_k_gain, v_b_k_gain),
                c_scale=(m_c_scale, v_c_scale))
    grad, delta, new_m, new_v = {}, {}, {}, {}
    for pname, layers in shares.items():
        w, (m, v) = params[pname], moms[pname]
        C = w.shape[-1]
        per_layer = [_adamw_sharded(coords, w[l].reshape(-1, C), m[l].reshape(-1, C), v[l].reshape(-1, C), g, r1, r2,
                                    f"adamw_{pname}{l}") for l, (g, r1, r2) in enumerate(layers)]
        grad[pname], delta[pname], new_m[pname], new_v[pname] = [
            jnp.stack([o.reshape(w.shape[1:]) for o in outs]) for outs in zip(*per_layer)]
    for pname, g in g_small.items():
        w = params[pname]
        C = w.shape[-1]
        outs = _adamw_small(w.reshape(-1, C), g.reshape(-1, C), moms[pname][0].reshape(-1, C), moms[pname][1].reshape(-1, C),
                            f"adamw_{pname}")
        grad[pname] = g.reshape(w.shape)
        delta[pname], new_m[pname], new_v[pname] = [o.reshape(w.shape) for o in outs]

    wnames = ["norm_gain", "a_w_in", "a_v_gain", "a_w_s", "a_b_s", "a_w_out", "b_w_in", "b_q_gain", "b_k_gain", "b_w_out",
              "c_w_in", "c_w_grp", "c_scale", "c_w_out"]
    return (loss, dx0[None], *[grad[n] for n in wnames], *[delta[n] for n in wnames],
            *[new_m[n] for n in wnames], *[new_v[n] for n in wnames])
```

```python
import functools

import numpy as np
import jax
import jax.numpy as jnp
from jax import lax
from jax.experimental import pallas as pl
from jax.experimental.pallas import tpu as pltpu

F32 = jnp.float32
MXU = jnp.bfloat16
ACT = jnp.bfloat16
WIRE = jnp.bfloat16

EPS = 1e-6
CHUNK = 128
A_GROUPS = 8
HEAD_DIM = 128
B_HEADS = 8
B_DILATIONS = (1, 4, 16)
ROPE_DIM = 32
ROPE_THETA = 500000.0
POOL_SIZES = (2, 4, 8, 16)
POOL_HALO = 16
N_DEV = 8
NEG = -1e30

ADAM_LR, ADAM_B1, ADAM_B2, ADAM_EPS, ADAM_WD, ADAM_STEP = 0.001, 0.9, 0.999, 1e-08, 0.01, 10

VMEM_LIMIT = 56 * 1024 * 1024
MESH = pl.DeviceIdType.MESH


def _cp(*sem):
    return pltpu.CompilerParams(dimension_semantics=sem, vmem_limit_bytes=VMEM_LIMIT)


def _sigmoid(z):
    return 1.0 / (1.0 + jnp.exp(-z))


def _dot(a, b):
    return jnp.dot(a.astype(MXU), b.astype(MXU), preferred_element_type=F32)


def _dot_nt(a, b):
    return lax.dot_general(a.astype(MXU), b.astype(MXU), (((1,), (1,)), ((), ())), preferred_element_type=F32)


def _dot_tn(a, b):
    return lax.dot_general(a.astype(MXU), b.astype(MXU), (((0,), (0,)), ((), ())), preferred_element_type=F32)


def _chunk_slot(d):
    return (d % 2) * 4 + d // 2


class _Comm:
    def __init__(self, inputs, out_shapes, n_remote, make, aliases=None, n_local=1):
        self.inputs = list(inputs)
        self.out_shapes = list(out_shapes)
        self.n_remote = n_remote
        self.n_local = n_local
        self.make = make
        self.aliases = dict(aliases or {})

    def sems(self):
        return [pltpu.SemaphoreType.DMA((self.n_remote,)), pltpu.SemaphoreType.DMA((self.n_remote,)),
                pltpu.SemaphoreType.DMA((self.n_local,))]


_HBM = pl.BlockSpec(memory_space=pl.ANY)


def _launch(body, *, name, grid, in_specs, out_specs, out_shape, args, sem, scratch=(), aliases=None, comm=None):
    in_specs, out_specs, out_shape, scratch = list(in_specs), list(out_specs), list(out_shape), list(scratch)
    aliases = dict(aliases or {})
    if comm is None:
        return pl.pallas_call(body, name=name, grid=grid, in_specs=in_specs, out_specs=out_specs, out_shape=out_shape,
                              scratch_shapes=scratch, input_output_aliases=aliases, compiler_params=_cp(*sem))(*args)
    n_in, n_out, n_sc = len(in_specs), len(out_specs), len(scratch)
    nci, nco = len(comm.inputs), len(comm.out_shapes)

    def hosted(*refs):
        b_in, c_in = refs[:n_in], refs[n_in:n_in + nci]
        o0 = n_in + nci
        b_out, c_out = refs[o0:o0 + n_out], refs[o0 + n_out:o0 + n_out + nco]
        s0 = o0 + n_out + nco
        b_sc, sems = refs[s0:s0 + n_sc], refs[s0 + n_sc:]
        ids = [pl.program_id(a) for a in range(len(grid))]
        first = functools.reduce(jnp.logical_and, [i == 0 for i in ids])
        last = functools.reduce(jnp.logical_and, [i == g - 1 for i, g in zip(ids, grid)])

        @pl.when(first)
        def _():
            for cp in comm.make(c_in, c_out, *sems):
                cp.start()

        body(*b_in, *b_out, *b_sc)

        @pl.when(last)
        def _():
            for cp in comm.make(c_in, c_out, *sems):
                cp.wait()

    for ci, co in comm.aliases.items():
        aliases[n_in + ci] = n_out + co
    return pl.pallas_call(
        hosted, name=name, grid=grid, in_specs=in_specs + [_HBM] * nci, out_specs=out_specs + [_HBM] * nco,
        out_shape=out_shape + comm.out_shapes, scratch_shapes=scratch + comm.sems(),
        input_output_aliases=aliases, compiler_params=_cp(*["arbitrary"] * len(grid)))(*args, *comm.inputs)


def _run_comm(comm, name):
    nci, nco = len(comm.inputs), len(comm.out_shapes)

    def body(*refs):
        cps = comm.make(refs[:nci], refs[nci:nci + nco], *refs[nci + nco:])
        for cp in cps:
            cp.start()
        for cp in cps:
            cp.wait()

    return pl.pallas_call(
        body, name=name, in_specs=[_HBM] * nci, out_specs=[_HBM] * nco, out_shape=comm.out_shapes,
        scratch_shapes=comm.sems(), input_output_aliases=dict(comm.aliases))(*comm.inputs)


def _norm_proj(x, gain, w_dm, name, comm=None):
    M, Dm = x.shape
    nd, _, nl = w_dm.shape
    tm = min(M, 2048)

    def body(x_ref, g_ref, w_ref, h_ref, p_ref):
        @pl.when(pl.program_id(1) == 0)
        def _():
            xv = x_ref[...]
            r = lax.rsqrt(jnp.mean(xv * xv, axis=-1, keepdims=True) + EPS)
            h_ref[...] = (xv * r * g_ref[...]).astype(h_ref.dtype)

        p_ref[...] = _dot(h_ref[...], w_ref[...]).astype(p_ref.dtype)

    return _launch(
        body, name=name, grid=(M // tm, nd),
        in_specs=[pl.BlockSpec((tm, Dm), lambda i, j: (i, 0)),
                  pl.BlockSpec((1, Dm), lambda i, j: (0, 0)),
                  pl.BlockSpec((None, Dm, nl), lambda i, j: (j, 0, 0))],
        out_specs=[pl.BlockSpec((tm, Dm), lambda i, j: (i, 0)),
                   pl.BlockSpec((tm, nl), lambda i, j: (i, j))],
        out_shape=[jax.ShapeDtypeStruct((M, Dm), ACT), jax.ShapeDtypeStruct((M, nd * nl), ACT)],
        args=(x, gain, w_dm), sem=("parallel", "arbitrary"), comm=comm)


def _out_proj(x, y, w, name):
    M, Dm = x.shape
    K = y.shape[1]
    tm = min(M, 1024)

    def body(x_ref, y_ref, w_ref, o_ref):
        o_ref[...] = x_ref[...] + _dot(y_ref[...], w_ref[...])

    return pl.pallas_call(
        body, name=name, grid=(M // tm,),
        in_specs=[pl.BlockSpec((tm, Dm), lambda i: (i, 0)),
                  pl.BlockSpec((tm, K), lambda i: (i, 0)),
                  pl.BlockSpec((K, Dm), lambda i: (0, 0))],
        out_specs=pl.BlockSpec((tm, Dm), lambda i: (i, 0)),
        out_shape=jax.ShapeDtypeStruct((M, Dm), F32),
        compiler_params=_cp("parallel"),
    )(x, y, w)


def _out_proj_loss(x, y, w, target, name):
    M, Dm = x.shape
    K = y.shape[1]
    tm = min(M, 512)

    def body(x_ref, y_ref, w_ref, t_ref, dx_ref, dxa_ref, l_ref):
        @pl.when(pl.program_id(0) == 0)
        def _():
            l_ref[...] = jnp.zeros_like(l_ref)

        err = x_ref[...] + _dot(y_ref[...], w_ref[...]) - t_ref[...]
        dx = err * (1.0 / Dm)
        dx_ref[...] = dx
        dxa_ref[...] = dx.astype(dxa_ref.dtype)
        l_ref[...] += jnp.sum(err * err) * (0.5 / Dm)

    spec = pl.BlockSpec((tm, Dm), lambda i: (i, 0))
    dx, dxa, l = pl.pallas_call(
        body, name=name, grid=(M // tm,),
        in_specs=[spec, pl.BlockSpec((tm, K), lambda i: (i, 0)), pl.BlockSpec((K, Dm), lambda i: (0, 0)), spec],
        out_specs=[spec, spec, pl.BlockSpec((8, 128), lambda i: (0, 0))],
        out_shape=[jax.ShapeDtypeStruct((M, Dm), F32), jax.ShapeDtypeStruct((M, Dm), ACT),
                   jax.ShapeDtypeStruct((8, 128), F32)],
        compiler_params=_cp("arbitrary"),
    )(x, y, w, target)
    return l[0, 0], dx, dxa


def _dw_in(h, dproj, name, comm=None):
    M, Dm = h.shape
    nl = dproj.shape[1] // N_DEV
    tt = min(M, 2048)

    def body(a_ref, b_ref, o_ref):
        @pl.when(pl.program_id(1) == 0)
        def _():
            o_ref[...] = jnp.zeros_like(o_ref)

        o_ref[...] += _dot_tn(a_ref[...], b_ref[...])

    outs = _launch(
        body, name=name, grid=(N_DEV, M // tt),
        in_specs=[pl.BlockSpec((tt, Dm), lambda j, t: (t, 0)), pl.BlockSpec((tt, nl), lambda j, t: (t, j))],
        out_specs=[pl.BlockSpec((None, Dm, nl), lambda j, t: (_chunk_slot(j), 0, 0))],
        out_shape=[jax.ShapeDtypeStruct((N_DEV, Dm, nl), F32)],
        args=(h, dproj), sem=("parallel", "arbitrary"), comm=comm)
    return outs[0] if comm is None else outs


def _dw_out(y, dout, name, comm=None):
    M, K = y.shape
    Dm = dout.shape[1]
    kl = K // N_DEV
    tt = min(M, 512)

    def body(a_ref, b_ref, o_ref):
        @pl.when(pl.program_id(0) == 0)
        def _():
            o_ref[...] = jnp.zeros_like(o_ref)

        b = b_ref[...]
        for j in range(N_DEV):
            o_ref[_chunk_slot(j)] += _dot_tn(a_ref[:, j * kl:(j + 1) * kl], b)

    outs = _launch(
        body, name=name, grid=(M // tt,),
        in_specs=[pl.BlockSpec((tt, K), lambda t: (t, 0)), pl.BlockSpec((tt, Dm), lambda t: (t, 0))],
        out_specs=[pl.BlockSpec((N_DEV, kl, Dm), lambda t: (0, 0, 0))],
        out_shape=[jax.ShapeDtypeStruct((N_DEV, kl, Dm), F32)],
        args=(y, dout), sem=("arbitrary",), comm=comm)
    return outs[0] if comm is None else outs


def _dh_norm_bwd(dproj, w_dm, x, gain, dres, name, comm=None):
    M, Dm = x.shape
    nd, _, nl = w_dm.shape
    tm = min(M, 1024)
    rows_bytes = tm * Dm * (4 + 2 * 4 + 2 * 4 + 2 * 4 + 2 * 2)
    block_bytes = 2 * (tm * nl + Dm * nl) * 2
    pair = 2 if rows_bytes + 2 * block_bytes <= VMEM_LIMIT - 8 * 1024 * 1024 else 1
    nj = nd // pair

    def body(dp_ref, w_ref, x_ref, g_ref, dr_ref, dx_ref, dxa_ref, dg_ref, acc_ref):
        i, j = pl.program_id(0), pl.program_id(1)

        @pl.when(j == 0)
        def _():
            acc_ref[...] = jnp.zeros_like(acc_ref)

        acc_ref[...] += functools.reduce(
            lambda a, b: a + b, [_dot_nt(dp_ref[:, d * nl:(d + 1) * nl], w_ref[d]) for d in range(pair)])

        @pl.when(j == nj - 1)
        def _():
            @pl.when(i == 0)
            def _():
                dg_ref[...] = jnp.zeros_like(dg_ref)

            dh = acc_ref[...]
            xv = x_ref[...]
            r = lax.rsqrt(jnp.mean(xv * xv, axis=-1, keepdims=True) + EPS)
            xn = xv * r
            dg_ref[...] += jnp.sum(dh * xn, axis=0, keepdims=True)
            dxn = dh * g_ref[...]
            dx = dr_ref[...] + r * (dxn - xn * jnp.mean(dxn * xn, axis=-1, keepdims=True))
            dx_ref[...] = dx
            dxa_ref[...] = dx.astype(dxa_ref.dtype)

    row = pl.BlockSpec((tm, Dm), lambda i, j: (i, 0))
    return _launch(
        body, name=name, grid=(M // tm, nj),
        in_specs=[pl.BlockSpec((tm, pair * nl), lambda i, j: (i, j)),
                  pl.BlockSpec((pair, Dm, nl), lambda i, j: (j, 0, 0)),
                  row, pl.BlockSpec((1, Dm), lambda i, j: (0, 0)), row],
        out_specs=[row, row, pl.BlockSpec((1, Dm), lambda i, j: (0, 0))],
        out_shape=[jax.ShapeDtypeStruct((M, Dm), F32), jax.ShapeDtypeStruct((M, Dm), ACT),
                   jax.ShapeDtypeStruct((1, Dm), F32)],
        scratch=[pltpu.VMEM((tm, Dm), F32)],
        args=(dproj, w_dm, x, gain, dres), sem=("arbitrary", "arbitrary"), comm=comm)


def _tril_mask():
    return lax.broadcasted_iota(jnp.int32, (CHUNK, CHUNK), 0) >= lax.broadcasted_iota(jnp.int32, (CHUNK, CHUNK), 1)


def _a_mid(proj, v_gain, w_s, b_st, name, comm=None):
    M = proj.shape[0]
    W = proj.shape[1] // 3
    gd = W // A_GROUPS
    tm = min(M, 256)

    def body(p_ref, vg_ref, ws_ref, bs_ref, y_ref):
        pv = p_ref[:, W:2 * W].astype(F32)
        r = lax.rsqrt(jnp.mean(pv * pv, axis=-1, keepdims=True) + EPS)
        v = (pv * r * vg_ref[...]).astype(MXU)
        tri = _tril_mask()
        for g in range(A_GROUPS):
            wg = jnp.where(tri, ws_ref[g], 0.0).astype(MXU)
            bcol = bs_ref[:, g:g + 1]
            for c in range(tm // CHUNK):
                rows, cols = slice(c * CHUNK, (c + 1) * CHUNK), slice(g * gd, (g + 1) * gd)
                mixed = jnp.dot(wg, v[rows, cols], preferred_element_type=F32) + bcol
                u = p_ref[rows, g * gd:(g + 1) * gd].astype(F32)
                z = p_ref[rows, 2 * W + g * gd:2 * W + (g + 1) * gd].astype(F32)
                y_ref[rows, cols] = (u * mixed * (z * _sigmoid(z))).astype(y_ref.dtype)

    return _launch(
        body, name=name, grid=(M // tm,),
        in_specs=[pl.BlockSpec((tm, 3 * W), lambda i: (i, 0)),
                  pl.BlockSpec((1, W), lambda i: (0, 0)),
                  pl.BlockSpec((A_GROUPS, CHUNK, CHUNK), lambda i: (0, 0, 0)),
                  pl.BlockSpec((CHUNK, A_GROUPS), lambda i: (0, 0))],
        out_specs=[pl.BlockSpec((tm, W), lambda i: (i, 0))],
        out_shape=[jax.ShapeDtypeStruct((M, W), ACT)],
        args=(proj, v_gain, w_s, b_st), sem=("parallel",), comm=comm)


def _a_bwd(dout, w_out, proj, v_gain, w_s, b_st, name, comm=None):
    M = proj.shape[0]
    W = proj.shape[1] // 3
    Dm = dout.shape[1]
    gd = W // A_GROUPS
    tm = min(M, 256)
    nt = M // tm

    def body(do_ref, wo_ref, p_ref, vg_ref, ws_ref, bs_ref, dp_ref, dws_ref, dbs_ref, dvg_ref, dv_s):
        i = pl.program_id(0)

        @pl.when(i == 0)
        def _():
            dws_ref[...] = jnp.zeros_like(dws_ref)
            dbs_ref[...] = jnp.zeros_like(dbs_ref)
            dvg_ref[...] = jnp.zeros_like(dvg_ref)

        dy = _dot_nt(do_ref[...], wo_ref[...])
        pv = p_ref[:, W:2 * W].astype(F32)
        r = lax.rsqrt(jnp.mean(pv * pv, axis=-1, keepdims=True) + EPS)
        pvn = pv * r
        vg = vg_ref[...]
        v = (pvn * vg).astype(MXU)
        tri = _tril_mask()
        for g in range(A_GROUPS):
            wf = jnp.where(tri, ws_ref[g], 0.0)
            wg = wf.astype(MXU)
            wgt = wf.T.astype(MXU)
            bcol = bs_ref[:, g:g + 1]
            for c in range(tm // CHUNK):
                rows, cols = slice(c * CHUNK, (c + 1) * CHUNK), slice(g * gd, (g + 1) * gd)
                vb = v[rows, cols]
                mixed = jnp.dot(wg, vb, preferred_element_type=F32) + bcol
                u = p_ref[rows, g * gd:(g + 1) * gd].astype(F32)
                z = p_ref[rows, 2 * W + g * gd:2 * W + (g + 1) * gd].astype(F32)
                sig = _sigmoid(z)
                sz = z * sig
                dyb = dy[rows, cols]
                dp_ref[rows, g * gd:(g + 1) * gd] = (dyb * mixed * sz).astype(dp_ref.dtype)
                dp_ref[rows, 2 * W + g * gd:2 * W + (g + 1) * gd] = (
                    dyb * u * mixed * (sig * (1.0 + z * (1.0 - sig)))).astype(dp_ref.dtype)
                dmix = dyb * u * sz
                dws_ref[g] += _dot_nt(dmix, vb)
                dbs_ref[:, g:g + 1] += jnp.sum(dmix, axis=1, keepdims=True)
                dv_s[rows, cols] = jnp.dot(wgt, dmix.astype(MXU), preferred_element_type=F32)
        dv = dv_s[...]
        dvg_ref[...] += jnp.sum(dv * pvn, axis=0, keepdims=True)
        dpvn = dv * vg
        dp_ref[:, W:2 * W] = (r * (dpvn - pvn * jnp.mean(dpvn * pvn, axis=-1, keepdims=True))).astype(dp_ref.dtype)

        @pl.when(i == nt - 1)
        def _():
            for g in range(A_GROUPS):
                dws_ref[g] = jnp.where(tri, dws_ref[g], 0.0)

    return _launch(
        body, name=name, grid=(nt,),
        in_specs=[pl.BlockSpec((tm, Dm), lambda i: (i, 0)),
                  pl.BlockSpec((W, Dm), lambda i: (0, 0)),
                  pl.BlockSpec((tm, 3 * W), lambda i: (i, 0)),
                  pl.BlockSpec((1, W), lambda i: (0, 0)),
                  pl.BlockSpec((A_GROUPS, CHUNK, CHUNK), lambda i: (0, 0, 0)),
                  pl.BlockSpec((CHUNK, A_GROUPS), lambda i: (0, 0))],
        out_specs=[pl.BlockSpec((tm, 3 * W), lambda i: (i, 0)),
                   pl.BlockSpec((A_GROUPS, CHUNK, CHUNK), lambda i: (0, 0, 0)),
                   pl.BlockSpec((CHUNK, A_GROUPS), lambda i: (0, 0)),
                   pl.BlockSpec((1, W), lambda i: (0, 0))],
        out_shape=[jax.ShapeDtypeStruct((M, 3 * W), ACT),
                   jax.ShapeDtypeStruct((A_GROUPS, CHUNK, CHUNK), F32),
                   jax.ShapeDtypeStruct((CHUNK, A_GROUPS), F32),
                   jax.ShapeDtypeStruct((1, W), F32)],
        scratch=[pltpu.VMEM((tm, W), F32)],
        args=(dout, w_out, proj, v_gain, w_s, b_st), sem=("arbitrary",), comm=comm)


def _pool_diff(xg, tail, i, tm, w):
    t = lax.broadcasted_iota(jnp.int32, (tm, tm + POOL_HALO), 0)
    s = lax.broadcasted_iota(jnp.int32, (tm, tm + POOL_HALO), 1)
    off = t - (s - POOL_HALO)
    band = jnp.where((off >= 0) & (off < w), 1.0, 0.0).astype(MXU)
    tail = jnp.where(i > 0, tail, jnp.zeros_like(tail))
    ext = jnp.concatenate([tail, xg], axis=0)
    ssum = jnp.dot(band, ext.astype(MXU), preferred_element_type=F32)
    tglob = i * tm + lax.broadcasted_iota(jnp.int32, (tm, 1), 0)
    cnt = jnp.minimum(tglob + 1, w).astype(F32)
    return ssum / cnt - xg.astype(F32)


def _c_mid(proj, w_grp, scale, name):
    M = proj.shape[0]
    W = proj.shape[1] // 2
    ng = len(POOL_SIZES)
    cg = W // ng
    tm = min(M, 256)
    hb = tm // POOL_HALO

    def body(xc_ref, tail_ref, z_ref, wg_ref, sc_ref, y_ref):
        i = pl.program_id(0)
        for g, w in enumerate(POOL_SIZES):
            cols = slice(g * cg, (g + 1) * cg)
            d = _pool_diff(xc_ref[:, cols], tail_ref[:, cols], i, tm, w)
            mixed = _dot(d, wg_ref[g]) * sc_ref[:, cols]
            z = z_ref[:, cols].astype(F32)
            y_ref[:, cols] = (mixed * (z * _sigmoid(z))).astype(y_ref.dtype)

    return pl.pallas_call(
        body, name=name, grid=(M // tm,),
        in_specs=[pl.BlockSpec((tm, W), lambda i: (i, 0)),
                  pl.BlockSpec((POOL_HALO, W), lambda i: (jnp.maximum(i * hb - 1, 0), 0)),
                  pl.BlockSpec((tm, W), lambda i: (i, 1)),
                  pl.BlockSpec((ng, cg, cg), lambda i: (0, 0, 0)),
                  pl.BlockSpec((1, W), lambda i: (0, 0))],
        out_specs=pl.BlockSpec((tm, W), lambda i: (i, 0)),
        out_shape=jax.ShapeDtypeStruct((M, W), ACT),
        compiler_params=_cp("parallel"),
    )(proj, proj, proj, w_grp, scale)


def _c_bwd1(dout, w_out, proj, w_grp, scale, name, comm=None):
    M = proj.shape[0]
    W = proj.shape[1] // 2
    Dm = dout.shape[1]
    ng = len(POOL_SIZES)
    cg = W // ng
    rl = cg // N_DEV
    tm = min(M, 256)
    hb = tm // POOL_HALO
    nt = M // tm

    def body(do_ref, wo_ref, xc_ref, tail_ref, z_ref, wg_ref, sc_ref, dd_ref, dz_ref, dwg_ref, dsc_ref, acc_ref):
        i = pl.program_id(0)

        @pl.when(i == 0)
        def _():
            acc_ref[...] = jnp.zeros_like(acc_ref)
            dsc_ref[...] = jnp.zeros_like(dsc_ref)

        dy = _dot_nt(do_ref[...], wo_ref[...])
        for g, w in enumerate(POOL_SIZES):
            cols = slice(g * cg, (g + 1) * cg)
            d = _pool_diff(xc_ref[:, cols], tail_ref[:, cols], i, tm, w)
            mr = _dot(d, wg_ref[g])
            sc = sc_ref[:, cols]
            z = z_ref[:, cols].astype(F32)
            sig = _sigmoid(z)
            dyg = dy[:, cols]
            dmixed = dyg * (z * sig)
            dz_ref[:, cols] = (dyg * (mr * sc) * (sig * (1.0 + z * (1.0 - sig)))).astype(dz_ref.dtype)
            dsc_ref[:, cols] += jnp.sum(dmixed * mr, axis=0, keepdims=True)
            dmr = (dmixed * sc).astype(MXU)
            acc_ref[g] += _dot_tn(d, dmr)
            dd_ref[:, cols] = _dot_nt(dmr, wg_ref[g]).astype(dd_ref.dtype)

        @pl.when(i == nt - 1)
        def _():
            for dev in range(N_DEV):
                for g in range(ng):
                    dwg_ref[_chunk_slot(dev), g] = acc_ref[g, dev * rl:(dev + 1) * rl, :]

    return _launch(
        body, name=name, grid=(nt,),
        in_specs=[pl.BlockSpec((tm, Dm), lambda i: (i, 0)),
                  pl.BlockSpec((W, Dm), lambda i: (0, 0)),
                  pl.BlockSpec((tm, W), lambda i: (i, 0)),
                  pl.BlockSpec((POOL_HALO, W), lambda i: (jnp.maximum(i * hb - 1, 0), 0)),
                  pl.BlockSpec((tm, W), lambda i: (i, 1)),
                  pl.BlockSpec((ng, cg, cg), lambda i: (0, 0, 0)),
                  pl.BlockSpec((1, W), lambda i: (0, 0))],
        out_specs=[pl.BlockSpec((tm, W), lambda i: (i, 0)),
                   pl.BlockSpec((tm, W), lambda i: (i, 0)),
                   pl.BlockSpec((N_DEV, ng, rl, cg), lambda i: (0, 0, 0, 0)),
                   pl.BlockSpec((1, W), lambda i: (0, 0))],
        out_shape=[jax.ShapeDtypeStruct((M, W), ACT), jax.ShapeDtypeStruct((M, W), ACT),
                   jax.ShapeDtypeStruct((N_DEV, ng, rl, cg), F32), jax.ShapeDtypeStruct((1, W), F32)],
        scratch=[pltpu.VMEM((ng, cg, cg), F32)],
        args=(dout, w_out, proj, proj, proj, w_grp, scale), sem=("arbitrary",), comm=comm)


def _c_bwd2(dd, dz, name):
    M, W = dd.shape
    ng = len(POOL_SIZES)
    cg = W // ng
    tm = min(M, 256)
    hb = tm // POOL_HALO
    nt = M // tm

    def body(dd_ref, head_ref, dz_ref, dp_ref):
        i = pl.program_id(0)
        s = lax.broadcasted_iota(jnp.int32, (tm, tm + POOL_HALO), 0)
        t = lax.broadcasted_iota(jnp.int32, (tm, tm + POOL_HALO), 1)
        off = t - s
        tglob = i * tm + lax.broadcasted_iota(jnp.int32, (tm + POOL_HALO, 1), 0)
        for g, w in enumerate(POOL_SIZES):
            cols = slice(g * cg, (g + 1) * cg)
            ddg = dd_ref[:, cols].astype(F32)
            head = head_ref[:, cols].astype(F32)
            head = jnp.where(i < nt - 1, head, jnp.zeros_like(head))
            cnt = jnp.minimum(tglob + 1, w).astype(F32)
            ext = (jnp.concatenate([ddg, head], axis=0) / cnt).astype(MXU)
            band = jnp.where((off >= 0) & (off < w), 1.0, 0.0).astype(MXU)
            dp_ref[:, cols] = (jnp.dot(band, ext, preferred_element_type=F32) - ddg).astype(dp_ref.dtype)
        dp_ref[:, W:] = dz_ref[...]

    return pl.pallas_call(
        body, name=name, grid=(nt,),
        in_specs=[pl.BlockSpec((tm, W), lambda i: (i, 0)),
                  pl.BlockSpec((POOL_HALO, W), lambda i: (jnp.minimum((i + 1) * hb, M // POOL_HALO - 1), 0)),
                  pl.BlockSpec((tm, W), lambda i: (i, 0))],
        out_specs=pl.BlockSpec((tm, 2 * W), lambda i: (i, 0)),
        out_shape=jax.ShapeDtypeStruct((M, 2 * W), ACT),
        compiler_params=_cp("parallel"),
    )(dd, dd, dz)


def _rope_tables(S):
    half = ROPE_DIM // 2
    inv_freq = jnp.power(jnp.float32(ROPE_THETA), -jnp.arange(half, dtype=F32) / half)
    ang = jnp.arange(S, dtype=F32)[:, None] * inv_freq[None, :]
    cos, sin = jnp.cos(ang), jnp.sin(ang)
    rest = HEAD_DIM - ROPE_DIM
    cf = jnp.concatenate([cos, cos, jnp.ones((S, rest), F32)], axis=1)
    sf = jnp.concatenate([-sin, sin, jnp.zeros((S, rest), F32)], axis=1)
    return cf, sf


def _swap_matrix():
    half = ROPE_DIM // 2
    a = lax.broadcasted_iota(jnp.int32, (HEAD_DIM, HEAD_DIM), 0)
    e = lax.broadcasted_iota(jnp.int32, (HEAD_DIM, HEAD_DIM), 1)
    hit = ((e < half) & (a == e + half)) | ((e >= half) & (e < 2 * half) & (a == e - half))
    return jnp.where(hit, 1.0, 0.0).astype(MXU)


def _b_qk_fwd(proj, tables, gains, name, comm=None):
    M = proj.shape[0]
    nsl = 2 * len(B_DILATIONS) * B_HEADS
    Wqk = nsl * HEAD_DIM
    tm = min(M, 256)

    def body(p_ref, cf_ref, sf_ref, g_ref, o_ref):
        cf, sf = cf_ref[...], sf_ref[...]
        swap = _swap_matrix()
        for j in range(nsl):
            cols = slice(j * HEAD_DIM, (j + 1) * HEAD_DIM)
            xv = p_ref[:, cols].astype(F32)
            r = lax.rsqrt(jnp.mean(xv * xv, axis=-1, keepdims=True) + EPS)
            xg = xv * g_ref[j // B_HEADS:j // B_HEADS + 1, :]
            hi = xg.astype(MXU)
            lo = (xg - hi.astype(F32)).astype(MXU)
            sw = jnp.dot(hi, swap, preferred_element_type=F32) + jnp.dot(lo, swap, preferred_element_type=F32)
            o_ref[:, cols] = (r * (xg * cf + sw * sf)).astype(o_ref.dtype)

    tspec = pl.BlockSpec((tm, HEAD_DIM), lambda i: (i, 0))
    return _launch(
        body, name=name, grid=(M // tm,),
        in_specs=[pl.BlockSpec((tm, Wqk), lambda i: (i, 0)), tspec, tspec,
                  pl.BlockSpec((8, HEAD_DIM), lambda i: (0, 0))],
        out_specs=[pl.BlockSpec((tm, Wqk), lambda i: (i, 0))],
        out_shape=[jax.ShapeDtypeStruct((M, Wqk), ACT)],
        args=(proj, *tables, gains), sem=("parallel",), comm=comm)


def _b_qk_bwd(dqs, dks, proj, tables, gains, dproj, name):
    M = proj.shape[0]
    ngr = len(B_DILATIONS)
    nsl = 2 * ngr * B_HEADS
    Wqk = nsl * HEAD_DIM
    Wg = B_HEADS * HEAD_DIM
    tm = min(M, 256)

    def body(*refs):
        d_refs = refs[:2 * ngr]
        p_ref, cf_ref, sf_ref, g_ref = refs[2 * ngr:2 * ngr + 4]
        dp_ref, dg_ref = refs[-2], refs[-1]

        @pl.when(pl.program_id(0) == 0)
        def _():
            dg_ref[...] = jnp.zeros_like(dg_ref)

        cf, sf = cf_ref[...], sf_ref[...]
        swap = _swap_matrix()
        for j in range(nsl):
            t, hh = j // B_HEADS, j % B_HEADS
            cols = slice(j * HEAD_DIM, (j + 1) * HEAD_DIM)
            dy = d_refs[t][:, hh * HEAD_DIM:(hh + 1) * HEAD_DIM].astype(F32)
            dxn = dy * cf + jnp.dot((dy * sf).astype(MXU), swap, preferred_element_type=F32)
            xv = p_ref[:, cols].astype(F32)
            r = lax.rsqrt(jnp.mean(xv * xv, axis=-1, keepdims=True) + EPS)
            xh = xv * r
            dg_ref[t:t + 1, :] += jnp.sum(dxn * xh, axis=0, keepdims=True)
            dxh = dxn * g_ref[t:t + 1, :]
            dp_ref[:, cols] = (r * (dxh - xh * jnp.mean(dxh * xh, axis=-1, keepdims=True))).astype(dp_ref.dtype)

    tspec = pl.BlockSpec((tm, HEAD_DIM), lambda i: (i, 0))
    dspec = pl.BlockSpec((tm, Wg), lambda i: (i, 0))
    n_in = 2 * ngr + 5
    return pl.pallas_call(
        body, name=name, grid=(M // tm,),
        in_specs=[dspec] * (2 * ngr) + [pl.BlockSpec((tm, Wqk), lambda i: (i, 0)), tspec, tspec,
                                        pl.BlockSpec((8, HEAD_DIM), lambda i: (0, 0)),
                                        pl.BlockSpec(memory_space=pl.ANY)],
        out_specs=[pl.BlockSpec((tm, Wqk), lambda i: (i, 0)), pl.BlockSpec((8, HEAD_DIM), lambda i: (0, 0))],
        out_shape=[jax.ShapeDtypeStruct(dproj.shape, dproj.dtype), jax.ShapeDtypeStruct((8, HEAD_DIM), F32)],
        input_output_aliases={n_in - 1: 0},
        compiler_params=_cp("arbitrary"),
    )(*dqs, *dks, proj, *tables, gains, dproj)


def _attn_tile(D, M):
    return max(HEAD_DIM * D, min(M, 2048))


class _TokenRows:
    GROUP = 16

    def __init__(self, D):
        self.D = D
        self.pitch = 24 if D == 16 else self.GROUP

    def rows(self, ntok):
        return ntok // self.GROUP * self.pitch

    def every_dth(self, tok0, n):
        start = tok0 // self.GROUP * self.pitch + tok0 % self.GROUP
        stride = self.D * self.pitch // self.GROUP
        return pl.ds(start, n) if stride == 1 else pl.ds(start, n, stride=stride)

    def put(self, dst, tok0, src_ref, ntok):
        if self.pitch == self.GROUP:
            dst[tok0:tok0 + ntok, :] = src_ref[...].astype(F32)
            return

        def group(i, carry):
            row = pl.multiple_of((tok0 // self.GROUP + i) * self.pitch, 8)
            dst[pl.ds(row, self.GROUP), :] = src_ref[pl.ds(pl.multiple_of(i * self.GROUP, self.GROUP), self.GROUP), :].astype(F32)
            return carry

        lax.fori_loop(0, ntok // self.GROUP, group, 0, unroll=8)

    def get(self, dst_ref, src, ntok):
        if self.pitch == self.GROUP:
            dst_ref[...] = src[0:ntok, :].astype(dst_ref.dtype)
            return

        def group(i, carry):
            row = pl.multiple_of(i * self.pitch, 8)
            dst_ref[pl.ds(pl.multiple_of(i * self.GROUP, self.GROUP), self.GROUP), :] = src[pl.ds(row, self.GROUP), :].astype(dst_ref.dtype)
            return carry

        lax.fori_loop(0, ntok // self.GROUP, group, 0, unroll=8)


def _attn_mask(base):
    qi = lax.broadcasted_iota(jnp.int32, (CHUNK, 2 * CHUNK), 0)
    ki = lax.broadcasted_iota(jnp.int32, (CHUNK, 2 * CHUNK), 1)
    return (ki >= qi) & (ki <= qi + CHUNK) & (ki >= CHUNK - base)


def _b_attn_fwd(qk, proj, g, name):
    M = qk.shape[0]
    D = B_DILATIONS[g]
    ngr = len(B_DILATIONS)
    T = _attn_tile(D, M)
    P = HEAD_DIM * D
    nsb = T // P
    Wg = B_HEADS * HEAD_DIM
    scale = np.float32(1.0 / np.sqrt(HEAD_DIM))

    lay = _TokenRows(D)
    RP, RT = lay.rows(P), lay.rows(T)

    def body(q_ref, k_ref, v_ref, o_ref, l_ref, qs, ks, vs, os_):
        n = pl.program_id(1)

        @pl.when(n == 0)
        def _():
            ks[0:RP, :] = jnp.zeros((RP, HEAD_DIM), F32)
            vs[0:RP, :] = jnp.zeros((RP, HEAD_DIM), F32)

        lay.put(qs, 0, q_ref, T)
        lay.put(ks, P, k_ref, T)
        lay.put(vs, P, v_ref, T)

        for b in range(nsb):
            mask = _attn_mask(n * (T // D) + b * CHUNK)
            for r in range(D):
                start = b * P + r
                q = qs[lay.every_dth(start, CHUNK), :]
                k = ks[lay.every_dth(start, 2 * CHUNK), :]
                v = vs[lay.every_dth(start, 2 * CHUNK), :]
                s = jnp.where(mask, _dot_nt(q, k) * scale, NEG)
                m = jnp.max(s, axis=-1, keepdims=True)
                p = jnp.exp(s - m)
                l = jnp.sum(p, axis=-1, keepdims=True)
                o = _dot(p, v) / l
                os_[lay.every_dth(start, CHUNK), :] = o
                l_ref[:, b * D + r:b * D + r + 1] = m + jnp.log(l)

        lay.get(o_ref, os_, T)
        ks[0:RP, :] = ks[RT:RT + RP, :]
        vs[0:RP, :] = vs[RT:RT + RP, :]

    blk = (T, HEAD_DIM)
    U = nsb * D
    return pl.pallas_call(
        body, name=name, grid=(B_HEADS, M // T),
        in_specs=[pl.BlockSpec(blk, lambda h, n: (n, g * B_HEADS + h)),
                  pl.BlockSpec(blk, lambda h, n: (n, (ngr + g) * B_HEADS + h)),
                  pl.BlockSpec(blk, lambda h, n: (n, (2 * ngr + g) * B_HEADS + h))],
        out_specs=[pl.BlockSpec(blk, lambda h, n: (n, h)), pl.BlockSpec((None, CHUNK, U), lambda h, n: (h, n, 0))],
        out_shape=[jax.ShapeDtypeStruct((M, Wg), ACT), jax.ShapeDtypeStruct((B_HEADS, (M // T) * CHUNK, U), F32)],
        scratch_shapes=[pltpu.VMEM((RT, HEAD_DIM), F32), pltpu.VMEM((RP + RT, HEAD_DIM), F32),
                        pltpu.VMEM((RP + RT, HEAD_DIM), F32), pltpu.VMEM((RT, HEAD_DIM), F32)],
        compiler_params=_cp("parallel", "arbitrary"),
    )(qk, qk, proj)


def _units_to_tokens(a, D, T):
    H = a.shape[0]
    nsb = T // (HEAD_DIM * D)
    return a.reshape(H, -1, CHUNK, nsb, D).transpose(1, 3, 2, 4, 0).reshape(-1, H)


def _tokens_to_units(a, D, T):
    M, H = a.shape
    nsb = T // (HEAD_DIM * D)
    return a.reshape(M // T, nsb, CHUNK, D, H).transpose(4, 0, 2, 1, 3).reshape(H, (M // T) * CHUNK, nsb * D)


def _b_combine(os_, ls, proj, name):
    M, Wg = os_[0].shape
    ngr = len(B_DILATIONS)
    tm = min(M, 512)

    def body(*refs):
        o_refs, l_refs, z_ref = refs[:ngr], refs[ngr:2 * ngr], refs[2 * ngr]
        y_ref, o_ref, lse_ref = refs[2 * ngr + 1:]
        for h in range(B_HEADS):
            cols = slice(h * HEAD_DIM, (h + 1) * HEAD_DIM)
            ls_ = [r[:, h:h + 1] for r in l_refs]
            m = functools.reduce(jnp.maximum, ls_)
            es = [jnp.exp(l - m) for l in ls_]
            tot = functools.reduce(lambda a, b: a + b, es)
            o = functools.reduce(lambda a, b: a + b, [(e / tot) * r[:, cols].astype(F32) for e, r in zip(es, o_refs)])
            z = z_ref[:, cols].astype(F32)
            y_ref[:, cols] = (o * (z * _sigmoid(z))).astype(y_ref.dtype)
            o_ref[:, cols] = o.astype(o_ref.dtype)
            lse_ref[:, h:h + 1] = m + jnp.log(tot)

    spec = pl.BlockSpec((tm, Wg), lambda i: (i, 0))
    hspec = pl.BlockSpec((tm, B_HEADS), lambda i: (i, 0))
    return pl.pallas_call(
        body, name=name, grid=(M // tm,),
        in_specs=[spec] * ngr + [hspec] * ngr + [pl.BlockSpec((tm, Wg), lambda i: (i, 3 * ngr))],
        out_specs=[spec, spec, hspec],
        out_shape=[jax.ShapeDtypeStruct((M, Wg), ACT), jax.ShapeDtypeStruct((M, Wg), ACT),
                   jax.ShapeDtypeStruct((M, B_HEADS), F32)],
        compiler_params=_cp("parallel"),
    )(*os_, *ls, proj)


def _b_bwd_pre(dout, w_out, o, proj, name):
    M, Wg = o.shape
    Dm = dout.shape[1]
    ngr = len(B_DILATIONS)
    tm = min(M, 512)

    def body(do_ref, wo_ref, o_ref, z_ref, dov_ref, dl_ref, dp_ref):
        dy = _dot_nt(do_ref[...], wo_ref[...])
        z = z_ref[...].astype(F32)
        sig = _sigmoid(z)
        ov = o_ref[...].astype(F32)
        dp_ref[...] = (dy * ov * (sig * (1.0 + z * (1.0 - sig)))).astype(dp_ref.dtype)
        dov = dy * (z * sig)
        dov_ref[...] = dov.astype(dov_ref.dtype)
        prod = dov * ov
        for h in range(B_HEADS):
            dl_ref[:, h:h + 1] = jnp.sum(prod[:, h * HEAD_DIM:(h + 1) * HEAD_DIM], axis=-1, keepdims=True)

    spec = pl.BlockSpec((tm, Wg), lambda i: (i, 0))
    zspec = pl.BlockSpec((tm, Wg), lambda i: (i, 3 * ngr))
    return pl.pallas_call(
        body, name=name, grid=(M // tm,),
        in_specs=[pl.BlockSpec((tm, Dm), lambda i: (i, 0)), pl.BlockSpec((Wg, Dm), lambda i: (0, 0)), spec, zspec],
        out_specs=[spec, pl.BlockSpec((tm, B_HEADS), lambda i: (i, 0)), zspec],
        out_shape=[jax.ShapeDtypeStruct((M, Wg), ACT), jax.ShapeDtypeStruct((M, B_HEADS), F32),
                   jax.ShapeDtypeStruct(proj.shape, ACT)],
        compiler_params=_cp("parallel"),
    )(dout, w_out, o, proj)


def _b_attn_bwd(qk, proj, dov, lse, delta, dproj, g, name, comm=None):
    M = qk.shape[0]
    D = B_DILATIONS[g]
    ngr = len(B_DILATIONS)
    T = _attn_tile(D, M)
    P = HEAD_DIM * D
    nsb = T // P
    nt = M // T
    Wg = B_HEADS * HEAD_DIM
    scale = np.float32(1.0 / np.sqrt(HEAD_DIM))
    shift = T - P
    lay = _TokenRows(D)
    RP, RT = lay.rows(P), lay.rows(T)

    def body(q_ref, k_ref, v_ref, do_ref, l_ref, dl_ref, dp_any, dq_ref, dk_ref, dv_ref,
             qs, dos, ks, vs, dqs, dks, dvs):
        n = pl.program_id(1)

        @pl.when(n == 0)
        def _():
            ks[0:RP, :] = jnp.zeros((RP, HEAD_DIM), F32)
            vs[0:RP, :] = jnp.zeros((RP, HEAD_DIM), F32)
            dks[...] = jnp.zeros((2 * RT, HEAD_DIM), F32)
            dvs[...] = jnp.zeros((2 * RT, HEAD_DIM), F32)

        @pl.when(n < nt)
        def _():
            lay.put(qs, 0, q_ref, T)
            lay.put(dos, 0, do_ref, T)
            lay.put(ks, P, k_ref, T)
            lay.put(vs, P, v_ref, T)

            masks = [_attn_mask(n * (T // D) + b * CHUNK) for b in range(nsb)]
            for r in range(D):
                carry_dv = carry_dk = None
                for b in range(nsb):
                    start = b * P + r
                    qsl = lay.every_dth(start, CHUNK)
                    ksl = lay.every_dth(start, 2 * CHUNK)
                    lo = lay.every_dth(start + shift, CHUNK)
                    q = qs[qsl, :]
                    do = dos[qsl, :]
                    k = ks[ksl, :]
                    v = vs[ksl, :]
                    s = _dot_nt(q, k) * scale
                    u = b * D + r
                    p = jnp.where(masks[b], jnp.exp(s - l_ref[:, u:u + 1]), 0.0)
                    dv = _dot_tn(p, do)
                    dp = _dot_nt(do, v)
                    ds = (p * (dp - dl_ref[:, u:u + 1]) * scale).astype(MXU)
                    dqs[qsl, :] = _dot(ds, k)
                    dk = _dot_tn(ds, q)
                    if b == 0:
                        dvs[lo, :] += dv[:CHUNK]
                        dks[lo, :] += dk[:CHUNK]
                    else:
                        dvs[lo, :] = carry_dv + dv[:CHUNK]
                        dks[lo, :] = carry_dk + dk[:CHUNK]
                    carry_dv, carry_dk = dv[CHUNK:], dk[CHUNK:]
                hi = lay.every_dth((nsb - 1) * P + r + shift + P, CHUNK)
                dvs[hi, :] = carry_dv
                dks[hi, :] = carry_dk

        lay.get(dq_ref, dqs, T)
        lay.get(dk_ref, dks, T)
        lay.get(dv_ref, dvs, T)
        dks[0:RT, :] = dks[RT:2 * RT, :]
        dvs[0:RT, :] = dvs[RT:2 * RT, :]
        ks[0:RP, :] = ks[RT:RT + RP, :]
        vs[0:RP, :] = vs[RT:RT + RP, :]

    blk = (T, HEAD_DIM)
    cur = lambda n: jnp.minimum(n, nt - 1)
    prv = lambda n: jnp.maximum(n - 1, 0)
    return _launch(
        body, name=name, grid=(B_HEADS, nt + 1),
        in_specs=[pl.BlockSpec(blk, lambda h, n: (cur(n), g * B_HEADS + h)),
                  pl.BlockSpec(blk, lambda h, n: (cur(n), (ngr + g) * B_HEADS + h)),
                  pl.BlockSpec(blk, lambda h, n: (cur(n), (2 * ngr + g) * B_HEADS + h)),
                  pl.BlockSpec(blk, lambda h, n: (cur(n), h)),
                  pl.BlockSpec((None, CHUNK, nsb * D), lambda h, n: (h, cur(n), 0)),
                  pl.BlockSpec((None, CHUNK, nsb * D), lambda h, n: (h, cur(n), 0)),
                  pl.BlockSpec(memory_space=pl.ANY)],
        out_specs=[pl.BlockSpec(blk, lambda h, n: (cur(n), h)),
                   pl.BlockSpec(blk, lambda h, n: (prv(n), h)),
                   pl.BlockSpec(blk, lambda h, n: (prv(n), (2 * ngr + g) * B_HEADS + h))],
        out_shape=[jax.ShapeDtypeStruct((M, Wg), ACT), jax.ShapeDtypeStruct((M, Wg), ACT),
                   jax.ShapeDtypeStruct(dproj.shape, dproj.dtype)],
        scratch=[pltpu.VMEM((RT, HEAD_DIM), F32)] * 2
        + [pltpu.VMEM((RP + RT, HEAD_DIM), F32)] * 2
        + [pltpu.VMEM((RT, HEAD_DIM), F32)]
        + [pltpu.VMEM((2 * RT, HEAD_DIM), F32)] * 2,
        aliases={6: 2},
        args=(qk, qk, proj, dov, lse, delta, dproj), sem=("parallel", "arbitrary"), comm=comm)


def _coords():
    return lax.axis_index("x"), lax.axis_index("y"), lax.axis_index("c")


def _gather_blocks(x_refs, out_refs, send_sems, recv_sems, local_sems):
    x, y, c = _coords()
    me, sibling = (x, y, c), (x, y, 1 - c)
    chips = [(1 - x, y), (x, 1 - y), (1 - x, 1 - y)]
    arrays = range(len(x_refs))

    def slot(a, px, py, pc):
        return out_refs[a].at[4 * px + 2 * py + pc]

    def copy(a, k, block, to, src=None):
        return _remote(slot(a, *block) if src is None else src, slot(a, *block), send_sems, recv_sems, 7 * a + k, to)

    mine = [pltpu.make_async_copy(x_refs[a], slot(a, *me), local_sems.at[a]) for a in arrays]
    first = [copy(a, 0, me, sibling, src=x_refs[a]) for a in arrays]
    first += [copy(a, 1 + j, me, (*chip, c), src=x_refs[a]) for j, chip in enumerate(chips) for a in arrays]
    for cp in mine + first:
        cp.start()
    passed = []
    for j, chip in enumerate(chips):
        for a in arrays:
            copy(a, 1 + j, (*chip, c), me).wait_recv()
            passed.append(copy(a, 4 + j, (*chip, c), sibling))
            passed[-1].start()
    for a in arrays:
        copy(a, 0, sibling, me).wait_recv()
        for j, chip in enumerate(chips):
            copy(a, 4 + j, (*chip, 1 - c), me).wait_recv()
    for cp in first + passed:
        cp.wait_send()
    for cp in mine:
        cp.wait()


def _all_gather_hbm(arrays, name):
    n = len(arrays)

    def body(*refs):
        _gather_blocks(refs[:n], refs[n:2 * n], *refs[2 * n:])

    return pl.pallas_call(
        body, name=name, in_specs=[_HBM] * n, out_specs=[_HBM] * n,
        out_shape=[jax.ShapeDtypeStruct((N_DEV,) + a.shape, a.dtype) for a in arrays],
        scratch_shapes=[pltpu.SemaphoreType.DMA((7 * n,)), pltpu.SemaphoreType.DMA((7 * n,)),
                        pltpu.SemaphoreType.DMA((n,))],
    )(*arrays)


def _all_reduce_small(part):
    R, C = part.shape

    def body(x_ref, tot_ref, gath, send_sems, recv_sems, local_sems):
        _gather_blocks([x_ref], [gath], send_sems, recv_sems, local_sems)
        acc = gath[0]
        for d in range(1, N_DEV):
            acc = acc + gath[d]
        tot_ref[...] = acc

    return pl.pallas_call(
        body, name="ar_small",
        in_specs=[pl.BlockSpec(memory_space=pltpu.VMEM)],
        out_specs=pl.BlockSpec(memory_space=pltpu.VMEM),
        out_shape=jax.ShapeDtypeStruct((R, C), F32),
        scratch_shapes=[pltpu.VMEM((N_DEV, R, C), F32),
                        pltpu.SemaphoreType.DMA((7,)), pltpu.SemaphoreType.DMA((7,)), pltpu.SemaphoreType.DMA((1,))],
        compiler_params=pltpu.CompilerParams(vmem_limit_bytes=VMEM_LIMIT),
    )(part)


def _sum_blocks(gath, name):
    _, R, C = gath.shape

    def body(g_ref, o_ref):
        acc = g_ref[0]
        for d in range(1, N_DEV):
            acc = acc + g_ref[d]
        o_ref[...] = acc

    return pl.pallas_call(
        body, name=name,
        in_specs=[pl.BlockSpec(memory_space=pltpu.VMEM)], out_specs=pl.BlockSpec(memory_space=pltpu.VMEM),
        out_shape=jax.ShapeDtypeStruct((R, C), F32),
        compiler_params=pltpu.CompilerParams(vmem_limit_bytes=VMEM_LIMIT),
    )(gath)


def _remote(src, dst, send_sems, recv_sems, k, peer):
    return pltpu.make_async_remote_copy(src_ref=src, dst_ref=dst, send_sem=send_sems.at[k], recv_sem=recv_sems.at[k],
                                        device_id=peer, device_id_type=MESH)


def _ag_send(arrays):
    n = len(arrays)

    def make(c_in, c_out, send_sems, recv_sems, local_sems):
        x, y, c = _coords()
        peers = [(x, y, 1 - c), (1 - x, y, c), (x, 1 - y, c), (1 - x, 1 - y, c)]
        cps = []
        for a in range(n):
            src, dst = c_in[a], c_out[a].at[4 * x + 2 * y + c]
            cps.append(pltpu.make_async_copy(src, dst, local_sems.at[a]))
            cps += [_remote(src, dst, send_sems, recv_sems, 4 * a + k, peer) for k, peer in enumerate(peers)]
        return cps

    return _Comm(arrays, [jax.ShapeDtypeStruct((N_DEV,) + a.shape, a.dtype) for a in arrays], 4 * n, make, n_local=n)


def _ag_forward(gaths):
    n = len(gaths)

    def make(c_in, c_out, send_sems, recv_sems, local_sems):
        x, y, c = _coords()
        chips = [(1 - x, y), (x, 1 - y), (1 - x, 1 - y)]
        cps = []
        for a in range(n):
            buf = c_out[a]
            cps += [_remote(buf.at[4 * px + 2 * py + c], buf.at[4 * px + 2 * py + c], send_sems, recv_sems, 3 * a + j,
                            (x, y, 1 - c)) for j, (px, py) in enumerate(chips)]
        return cps

    return _Comm(gaths, [jax.ShapeDtypeStruct(g.shape, g.dtype) for g in gaths], 3 * n, make,
                 aliases={a: a for a in range(n)})


def _rs_sibling(grads):
    n = len(grads)

    def make(c_in, c_out, send_sems, recv_sems, local_sem):
        x, y, c = _coords()
        return [_remote(c_in[a].at[pl.ds(4 * (1 - c), 4)], c_out[a], send_sems, recv_sems, a, (x, y, 1 - c))
                for a in range(n)]

    return _Comm(grads, [jax.ShapeDtypeStruct((4,) + g.shape[1:], g.dtype) for g in grads], n, make)


def _rs_chips(parts):
    n = len(parts)

    def make(c_in, c_out, send_sems, recv_sems, local_sem):
        x, y, c = _coords()
        peers = [(x, 1 - y, c), (1 - x, y, c), (1 - x, 1 - y, c)]
        return [_remote(c_in[a].at[k], c_out[a].at[k], send_sems, recv_sems, 3 * a + k, peer)
                for a in range(n) for k, peer in enumerate(peers)]

    return _Comm(parts, [jax.ShapeDtypeStruct(p.shape, p.dtype) for p in parts], 3 * n, make)


def _row_tile(rows, cols):
    tr = min(rows, 1 << int(np.log2((1 << 18) // cols)))
    assert rows % tr == 0
    return tr


def _chip_partials(coords, g, r1, name):
    _, rows, C = g.shape
    tr = _row_tile(rows, C)

    def body(co_ref, g_ref, r_ref, o_ref):
        o_ref[...] = (g_ref[...] + r_ref[...]).astype(o_ref.dtype)

    def chip(k, co):
        return jnp.bitwise_xor(2 * co[0] + co[1], k + 1)

    return pl.pallas_call(
        body, name=name,
        grid_spec=pltpu.PrefetchScalarGridSpec(
            num_scalar_prefetch=1, grid=(3, rows // tr),
            in_specs=[pl.BlockSpec((None, tr, C), lambda k, t, co: (4 * co[2] + chip(k, co), t, 0)),
                      pl.BlockSpec((None, tr, C), lambda k, t, co: (chip(k, co), t, 0))],
            out_specs=pl.BlockSpec((None, tr, C), lambda k, t, co: (k, t, 0))),
        out_shape=jax.ShapeDtypeStruct((3, rows, C), WIRE),
        compiler_params=_cp("parallel", "parallel"),
    )(coords, g, r1)


def _adam_math(w, g, m, v):
    m = ADAM_B1 * m + (1.0 - ADAM_B1) * g
    v = ADAM_B2 * v + (1.0 - ADAM_B2) * (g * g)
    m_hat = m / (1.0 - ADAM_B1 ** ADAM_STEP)
    v_hat = v / (1.0 - ADAM_B2 ** ADAM_STEP)
    delta = -ADAM_LR * (m_hat / (jnp.sqrt(v_hat) + ADAM_EPS) + ADAM_WD * w)
    return delta, m, v


def _adamw_sharded(coords, w, m, v, g, r1, r2, layer, prev, name):
    L, rows, C = w.shape
    tr = _row_tile(rows, C)
    n_prev = 0 if prev is None else len(prev)

    def body(co_ref, w_ref, m_ref, v_ref, g_ref, r1_ref, r2_ref, *rest):
        go_ref, d_ref, mo_ref, vo_ref = rest[n_prev:]
        grad = g_ref[...] + r1_ref[...]
        for k in range(3):
            grad = grad + r2_ref[k].astype(F32)
        go_ref[...] = grad
        d_ref[...], mo_ref[...], vo_ref[...] = _adam_math(w_ref[...], grad, m_ref[...], v_ref[...])

    spec = pl.BlockSpec((None, tr, C), lambda t, co: (layer, t, 0))
    return pl.pallas_call(
        body, name=name,
        grid_spec=pltpu.PrefetchScalarGridSpec(
            num_scalar_prefetch=1, grid=(rows // tr,),
            in_specs=[spec, spec, spec,
                      pl.BlockSpec((None, tr, C), lambda t, co: (4 * co[2] + 2 * co[0] + co[1], t, 0)),
                      pl.BlockSpec((None, tr, C), lambda t, co: (2 * co[0] + co[1], t, 0)),
                      pl.BlockSpec((3, tr, C), lambda t, co: (0, t, 0))] + [_HBM] * n_prev,
            out_specs=[spec] * 4),
        out_shape=[jax.ShapeDtypeStruct((L, rows, C), F32)] * 4,
        input_output_aliases={7 + k: k for k in range(n_prev)},
        compiler_params=_cp("parallel"),
    )(coords, w, m, v, g, r1, r2, *(prev or []))


def _adamw_small(w, g, m, v, name):
    def body(w_ref, g_ref, m_ref, v_ref, d_ref, mo_ref, vo_ref):
        d_ref[...], mo_ref[...], vo_ref[...] = _adam_math(w_ref[...], g_ref[...], m_ref[...], v_ref[...])

    return pl.pallas_call(
        body, name=name, out_shape=[jax.ShapeDtypeStruct(w.shape, F32)] * 3,
        in_specs=[pl.BlockSpec(memory_space=pltpu.VMEM)] * 4,
        out_specs=[pl.BlockSpec(memory_space=pltpu.VMEM)] * 3,
    )(w, g, m, v)


def _reduce_scatter_adds(coords, grads, r1s, tag):
    return [_chip_partials(coords, g, r, f"rs_add_{tag}{i}") for i, (g, r) in enumerate(zip(grads, r1s))]


def kernel(x, norm_gain, a_w_in, a_v_gain, a_w_s, a_b_s, a_w_out, b_w_in, b_q_gain, b_k_gain, b_w_out, c_w_in, c_w_grp, c_scale, c_w_out, loss_target, m_norm_gain, m_a_w_in, m_a_v_gain, m_a_w_s, m_a_b_s, m_a_w_out, m_b_w_in, m_b_q_gain, m_b_k_gain, m_b_w_out, m_c_w_in, m_c_w_grp, m_c_scale, m_c_w_out, v_norm_gain, v_a_w_in, v_a_v_gain, v_a_w_s, v_a_b_s, v_a_w_out, v_b_w_in, v_b_q_gain, v_b_k_gain, v_b_w_out, v_c_w_in, v_c_w_grp, v_c_scale, v_c_w_out):
    cx, cy, cc = _coords()
    coords = jnp.stack([cx, cy, cc]).astype(jnp.int32)
    dev = 4 * cx + 2 * cy + cc
    Dm = x.shape[2]

    xs, tgt = x[0], loss_target[0]
    tables = _rope_tables(xs.shape[0])
    ng = lambda i: norm_gain[i:i + 1]
    ngr = len(B_DILATIONS)
    bst = [a_b_s[l].T for l in range(2)]
    b_gains = jnp.concatenate([b_q_gain[0], b_k_gain[0], jnp.zeros((2, HEAD_DIM), F32)], axis=0)
    nla, nlb, nlc = a_w_in.shape[2], b_w_in.shape[2], c_w_in.shape[2]
    ngp, rlc, cgc = c_w_grp.shape[1:]
    wire = lambda w: w.astype(WIRE)

    nvg, nsc = a_v_gain.size, c_scale.size
    vec = jnp.concatenate([a_v_gain.reshape(-1), c_scale.reshape(-1), jnp.zeros((1024 - nvg - nsc,), F32)]).reshape(8, 128)
    wa_in0, wa_out0, vecs = _all_gather_hbm([wire(a_w_in[0]), wire(a_w_out[0]), vec], "ag_layer0")
    wa_out0 = wa_out0.reshape(-1, Dm)
    vecs = vecs.reshape(N_DEV, -1)
    a_vg = vecs[:, :nvg].reshape((N_DEV,) + a_v_gain.shape).transpose(1, 0, 2).reshape(a_v_gain.shape[0], -1)
    c_sc = vecs[:, nvg:nvg + nsc].reshape(1, -1)

    h0, p0, *g1 = _norm_proj(xs, ng(0), wa_in0, "l0_proj", comm=_ag_send([wire(b_w_in[0]), wire(b_w_out[0])]))
    y0, wb_in, wb_out = _a_mid(p0, a_vg[0:1], a_w_s[0], bst[0], "l0_mid", comm=_ag_forward(g1))
    x1 = _out_proj(xs, y0, wa_out0, "l0_out")
    wb_out = wb_out.reshape(-1, Dm)

    later = [wire(c_w_in[0]), wire(c_w_grp[0]), wire(c_w_out[0]), wire(a_w_in[1]), wire(a_w_out[1])]
    h1, p1, *g2 = _norm_proj(x1, ng(1), wb_in, "l1_proj", comm=_ag_send(later))
    qk, wc_in, wc_grp, wc_out, wa_in1, wa_out1 = _b_qk_fwd(p1, tables, b_gains, "l1_qk", comm=_ag_forward(g2))
    ogs, lgs = zip(*[_b_attn_fwd(qk, p1, g, f"l1_attn{g}") for g in range(ngr)])
    tiles = [_attn_tile(D, xs.shape[0]) for D in B_DILATIONS]
    lgs = [_units_to_tokens(l, D, T) for l, D, T in zip(lgs, B_DILATIONS, tiles)]
    y1, o1, lse = _b_combine(ogs, lgs, p1, "l1_comb")
    x2 = _out_proj(x1, y1, wb_out, "l1_out")
    wc_grp = wc_grp.transpose(1, 0, 2, 3).reshape(ngp, N_DEV * rlc, cgc)
    wc_out = wc_out.reshape(-1, Dm)
    wa_out1 = wa_out1.reshape(-1, Dm)

    h2, p2 = _norm_proj(x2, ng(2), wc_in, "l2_proj")
    y2 = _c_mid(p2, wc_grp, c_sc, "l2_mid")
    x3 = _out_proj(x2, y2, wc_out, "l2_out")
    h3, p3 = _norm_proj(x3, ng(3), wa_in1, "l3_proj")
    y3, = _a_mid(p3, a_vg[1:2], a_w_s[1], bst[1], "l3_mid")
    loss_local, dx4, dx4a = _out_proj_loss(x3, y3, wa_out1, tgt, "l3_out_loss")

    flat3 = lambda g: g.reshape(N_DEV, -1, g.shape[-1])
    dp3, dws1, dbs1, dvg1 = _a_bwd(dx4a, wa_out1, p3, a_vg[1:2], a_w_s[1], bst[1], "l3_bwd")
    grads3 = [_dw_in(h3, dp3, "l3_dwin"), _dw_out(y3, dx4a, "l3_dwout")]
    dx3, dx3a, dg3, *r1_3 = _dh_norm_bwd(dp3, wa_in1, x3, ng(3), dx4, "l3_dh", comm=_rs_sibling(grads3))
    parts3 = _reduce_scatter_adds(coords, grads3, r1_3, "l3_")

    dd, dz, gc_grp, dsc, *r2_3 = _c_bwd1(dx3a, wc_out, p2, wc_grp, c_sc, "l2_bwd1", comm=_rs_chips(parts3))
    dp2 = _c_bwd2(dd, dz, "l2_bwd2")
    grads2 = [_dw_in(h2, dp2, "l2_dwin"), _dw_out(y2, dx3a, "l2_dwout"), flat3(gc_grp)]
    dx2, dx2a, dg2, *r1_2 = _dh_norm_bwd(dp2, wc_in, x2, ng(2), dx3, "l2_dh", comm=_rs_sibling(grads2))
    parts2 = _reduce_scatter_adds(coords, grads2, r1_2, "l2_")

    dov, delta, dp1 = _b_bwd_pre(dx2a, wb_out, o1, p1, "l1_bwdpre")
    dqs, dks, r2_2 = [], [], None
    for g in range(ngr):
        lse_u, delta_u = [_tokens_to_units(a, B_DILATIONS[g], tiles[g]) for a in (lse, delta)]
        dq, dk, dp1, *rest = _b_attn_bwd(qk, p1, dov, lse_u, delta_u, dp1, g, f"l1_attnbwd{g}",
                                         comm=_rs_chips(parts2) if g == 0 else None)
        if g == 0:
            r2_2 = rest
        dqs.append(dq)
        dks.append(dk)
    dp1, dgains = _b_qk_bwd(dqs, dks, p1, tables, b_gains, dp1, "l1_qkbwd")
    grads1 = [_dw_in(h1, dp1, "l1_dwin"), _dw_out(y1, dx2a, "l1_dwout")]
    dx1, dx1a, dg1, *r1_1 = _dh_norm_bwd(dp1, wb_in, x1, ng(1), dx2, "l1_dh", comm=_rs_sibling(grads1))
    parts1 = _reduce_scatter_adds(coords, grads1, r1_1, "l1_")

    dp0, dws0, dbs0, dvg0, *r2_1 = _a_bwd(dx1a, wa_out0, p0, a_vg[0:1], a_w_s[0], bst[0], "l0_bwd", comm=_rs_chips(parts1))
    small = dict(norm=jnp.concatenate([dg1, dg2, dg3], axis=0), a_ws=jnp.stack([dws0, dws1]),
                 a_bs=jnp.stack([dbs0.T, dbs1.T]), b_gains=dgains, a_vg=jnp.concatenate([dvg0, dvg1], axis=0), c_sc=dsc)
    order = ["norm", "a_ws", "a_bs", "b_gains", "a_vg", "c_sc"]
    rows = [small[k].reshape(-1, 128) for k in order]
    roff = np.cumsum([0] + [r.shape[0] for r in rows])
    gw_in0, gsmall = _dw_in(h0, dp0, "l0_dwin", comm=_ag_send([jnp.concatenate(rows, axis=0)]))
    gw_out0, gsmall = _dw_out(y0, dx1a, "l0_dwout", comm=_ag_forward([gsmall]))
    grads0 = [gw_in0, gw_out0]
    r1_0 = _run_comm(_rs_sibling(grads0), "l0_rs_sibling")
    parts0 = _reduce_scatter_adds(coords, grads0, r1_0, "l0_")
    dx0, _, dg0, *r2_0 = _dh_norm_bwd(dp0, wa_in0, xs, ng(0), dx1, "l0_dh", comm=_rs_chips(parts0))

    tot = _sum_blocks(gsmall, "small_sum")
    sm = {k: tot[int(roff[i]):int(roff[i + 1])].reshape(small[k].shape) for i, k in enumerate(order)}
    late = _all_reduce_small(jnp.concatenate([dg0.reshape(-1, 128), jnp.full((8, 128), loss_local, F32)], axis=0))
    sm["norm"] = jnp.concatenate([late[0:8].reshape(1, -1), sm["norm"]], axis=0)
    loss = late[8, 0]
    vl = a_v_gain.shape[1]
    g_small = dict(
        norm_gain=sm["norm"], a_w_s=sm["a_ws"], a_b_s=sm["a_bs"],
        b_q_gain=sm["b_gains"][None, 0:3], b_k_gain=sm["b_gains"][None, 3:6],
        a_v_gain=lax.dynamic_slice_in_dim(sm["a_vg"], dev * vl, vl, axis=1),
        c_scale=lax.dynamic_slice_in_dim(sm["c_sc"], dev * vl, vl, axis=1),
    )

    shares = dict(
        a_w_in=[(grads0[0], r1_0[0], r2_0[0]), (grads3[0], r1_3[0], r2_3[0])],
        a_w_out=[(grads0[1], r1_0[1], r2_0[1]), (grads3[1], r1_3[1], r2_3[1])],
        b_w_in=[(grads1[0], r1_1[0], r2_1[0])], b_w_out=[(grads1[1], r1_1[1], r2_1[1])],
        c_w_in=[(grads2[0], r1_2[0], r2_2[0])], c_w_out=[(grads2[1], r1_2[1], r2_2[1])],
        c_w_grp=[(grads2[2], r1_2[2], r2_2[2])])

    params = dict(a_w_in=a_w_in, a_w_out=a_w_out, b_w_in=b_w_in, b_w_out=b_w_out, c_w_in=c_w_in, c_w_grp=c_w_grp, c_w_out=c_w_out,
                  norm_gain=norm_gain, a_v_gain=a_v_gain, a_w_s=a_w_s, a_b_s=a_b_s, b_q_gain=b_q_gain, b_k_gain=b_k_gain, c_scale=c_scale)
    moms = dict(a_w_in=(m_a_w_in, v_a_w_in), a_w_out=(m_a_w_out, v_a_w_out), b_w_in=(m_b_w_in, v_b_w_in), b_w_out=(m_b_w_out, v_b_w_out),
                c_w_in=(m_c_w_in, v_c_w_in), c_w_grp=(m_c_w_grp, v_c_w_grp), c_w_out=(m_c_w_out, v_c_w_out),
                norm_gain=(m_norm_gain, v_norm_gain), a_v_gain=(m_a_v_gain, v_a_v_gain), a_w_s=(m_a_w_s, v_a_w_s),
                a_b_s=(m_a_b_s, v_a_b_s), b_q_gain=(m_b_q_gain, v_b_q_gain), b_k_gain=(m_b_k_gain, v_b_k_gain),
                c_scale=(m_c_scale, v_c_scale))
    grad, delta, new_m, new_v = {}, {}, {}, {}
    for pname, layers in shares.items():
        w, (m, v) = params[pname], moms[pname]
        as3 = lambda a: a.reshape(a.shape[0], -1, a.shape[-1])
        outs = None
        for l, (g, r1, r2) in enumerate(layers):
            outs = _adamw_sharded(coords, as3(w), as3(m), as3(v), g, r1, r2, l, outs, f"adamw_{pname}{l}")
        grad[pname], delta[pname], new_m[pname], new_v[pname] = [o.reshape(w.shape) for o in outs]
    for pname, g in g_small.items():
        w = params[pname]
        C = w.shape[-1]
        outs = _adamw_small(w.reshape(-1, C), g.reshape(-1, C), moms[pname][0].reshape(-1, C), moms[pname][1].reshape(-1, C),
                            f"adamw_{pname}")
        grad[pname] = g.reshape(w.shape)
        delta[pname], new_m[pname], new_v[pname] = [o.reshape(w.shape) for o in outs]

    wnames = ["norm_gain", "a_w_in", "a_v_gain", "a_w_s", "a_b_s", "a_w_out", "b_w_in", "b_q_gain", "b_k_gain", "b_w_out",
              "c_w_in", "c_w_grp", "c_scale", "c_w_out"]
    return (loss, dx0[None], *[grad[n] for n in wnames], *[delta[n] for n in wnames],
            *[new_m[n] for n in wnames], *[new_v[n] for n in wnames])
```

```python
import functools

import numpy as np
import jax
import jax.numpy as jnp
from jax import lax
from jax.experimental import pallas as pl
from jax.experimental.pallas import tpu as pltpu

F32 = jnp.float32
MXU = jnp.bfloat16
ACT = jnp.bfloat16
WIRE = jnp.bfloat16

EPS = 1e-6
CHUNK = 128
A_GROUPS = 8
HEAD_DIM = 128
B_HEADS = 8
B_DILATIONS = (1, 4, 16)
ROPE_DIM = 32
ROPE_THETA = 500000.0
POOL_SIZES = (2, 4, 8, 16)
POOL_HALO = 16
N_DEV = 8
NEG = -1e30

ADAM_LR, ADAM_B1, ADAM_B2, ADAM_EPS, ADAM_WD, ADAM_STEP = 0.001, 0.9, 0.999, 1e-08, 0.01, 10

VMEM_LIMIT = 56 * 1024 * 1024
MESH = pl.DeviceIdType.MESH


def _cp(*sem):
    return pltpu.CompilerParams(dimension_semantics=sem, vmem_limit_bytes=VMEM_LIMIT)


def _sigmoid(z):
    return 1.0 / (1.0 + jnp.exp(-z))


def _dot(a, b):
    return jnp.dot(a.astype(MXU), b.astype(MXU), preferred_element_type=F32)


def _dot_nt(a, b):
    return lax.dot_general(a.astype(MXU), b.astype(MXU), (((1,), (1,)), ((), ())), preferred_element_type=F32)


def _dot_tn(a, b):
    return lax.dot_general(a.astype(MXU), b.astype(MXU), (((0,), (0,)), ((), ())), preferred_element_type=F32)


def _chunk_slot(d):
    return (d % 2) * 4 + d // 2


class _Comm:
    def __init__(self, inputs, out_shapes, n_remote, make, aliases=None, n_local=1):
        self.inputs = list(inputs)
        self.out_shapes = list(out_shapes)
        self.n_remote = n_remote
        self.n_local = n_local
        self.make = make
        self.aliases = dict(aliases or {})

    def sems(self):
        return [pltpu.SemaphoreType.DMA((self.n_remote,)), pltpu.SemaphoreType.DMA((self.n_remote,)),
                pltpu.SemaphoreType.DMA((self.n_local,))]


_HBM = pl.BlockSpec(memory_space=pl.ANY)


def _launch(body, *, name, grid, in_specs, out_specs, out_shape, args, sem, scratch=(), aliases=None, comm=None):
    in_specs, out_specs, out_shape, scratch = list(in_specs), list(out_specs), list(out_shape), list(scratch)
    aliases = dict(aliases or {})
    if comm is None:
        return pl.pallas_call(body, name=name, grid=grid, in_specs=in_specs, out_specs=out_specs, out_shape=out_shape,
                              scratch_shapes=scratch, input_output_aliases=aliases, compiler_params=_cp(*sem))(*args)
    n_in, n_out, n_sc = len(in_specs), len(out_specs), len(scratch)
    nci, nco = len(comm.inputs), len(comm.out_shapes)

    def hosted(*refs):
        b_in, c_in = refs[:n_in], refs[n_in:n_in + nci]
        o0 = n_in + nci
        b_out, c_out = refs[o0:o0 + n_out], refs[o0 + n_out:o0 + n_out + nco]
        s0 = o0 + n_out + nco
        b_sc, sems = refs[s0:s0 + n_sc], refs[s0 + n_sc:]
        ids = [pl.program_id(a) for a in range(len(grid))]
        first = functools.reduce(jnp.logical_and, [i == 0 for i in ids])
        last = functools.reduce(jnp.logical_and, [i == g - 1 for i, g in zip(ids, grid)])

        @pl.when(first)
        def _():
            for cp in comm.make(c_in, c_out, *sems):
                cp.start()

        body(*b_in, *b_out, *b_sc)

        @pl.when(last)
        def _():
            for cp in comm.make(c_in, c_out, *sems):
                cp.wait()

    for ci, co in comm.aliases.items():
        aliases[n_in + ci] = n_out + co
    return pl.pallas_call(
        hosted, name=name, grid=grid, in_specs=in_specs + [_HBM] * nci, out_specs=out_specs + [_HBM] * nco,
        out_shape=out_shape + comm.out_shapes, scratch_shapes=scratch + comm.sems(),
        input_output_aliases=aliases, compiler_params=_cp(*["arbitrary"] * len(grid)))(*args, *comm.inputs)


def _run_comm(comm, name):
    nci, nco = len(comm.inputs), len(comm.out_shapes)

    def body(*refs):
        cps = comm.make(refs[:nci], refs[nci:nci + nco], *refs[nci + nco:])
        for cp in cps:
            cp.start()
        for cp in cps:
            cp.wait()

    return pl.pallas_call(
        body, name=name, in_specs=[_HBM] * nci, out_specs=[_HBM] * nco, out_shape=comm.out_shapes,
        scratch_shapes=comm.sems(), input_output_aliases=dict(comm.aliases))(*comm.inputs)


def _norm_proj(x, gain, w_dm, name, comm=None):
    M, Dm = x.shape
    nd, _, nl = w_dm.shape
    tm = min(M, 2048)

    def body(x_ref, g_ref, w_ref, h_ref, p_ref):
        @pl.when(pl.program_id(1) == 0)
        def _():
            xv = x_ref[...]
            r = lax.rsqrt(jnp.mean(xv * xv, axis=-1, keepdims=True) + EPS)
            h_ref[...] = (xv * r * g_ref[...]).astype(h_ref.dtype)

        p_ref[...] = _dot(h_ref[...], w_ref[...]).astype(p_ref.dtype)

    return _launch(
        body, name=name, grid=(M // tm, nd),
        in_specs=[pl.BlockSpec((tm, Dm), lambda i, j: (i, 0)),
                  pl.BlockSpec((1, Dm), lambda i, j: (0, 0)),
                  pl.BlockSpec((None, Dm, nl), lambda i, j: (j, 0, 0))],
        out_specs=[pl.BlockSpec((tm, Dm), lambda i, j: (i, 0)),
                   pl.BlockSpec((tm, nl), lambda i, j: (i, j))],
        out_shape=[jax.ShapeDtypeStruct((M, Dm), ACT), jax.ShapeDtypeStruct((M, nd * nl), ACT)],
        args=(x, gain, w_dm), sem=("parallel", "arbitrary"), comm=comm)


def _out_proj(x, y, w, name):
    M, Dm = x.shape
    K = y.shape[1]
    tm = min(M, 1024)

    def body(x_ref, y_ref, w_ref, o_ref):
        o_ref[...] = x_ref[...] + _dot(y_ref[...], w_ref[...])

    return pl.pallas_call(
        body, name=name, grid=(M // tm,),
        in_specs=[pl.BlockSpec((tm, Dm), lambda i: (i, 0)),
                  pl.BlockSpec((tm, K), lambda i: (i, 0)),
                  pl.BlockSpec((K, Dm), lambda i: (0, 0))],
        out_specs=pl.BlockSpec((tm, Dm), lambda i: (i, 0)),
        out_shape=jax.ShapeDtypeStruct((M, Dm), F32),
        compiler_params=_cp("parallel"),
    )(x, y, w)


def _out_proj_loss(x, y, w, target, name):
    M, Dm = x.shape
    K = y.shape[1]
    tm = min(M, 512)

    def body(x_ref, y_ref, w_ref, t_ref, dx_ref, dxa_ref, l_ref):
        @pl.when(pl.program_id(0) == 0)
        def _():
            l_ref[...] = jnp.zeros_like(l_ref)

        err = x_ref[...] + _dot(y_ref[...], w_ref[...]) - t_ref[...]
        dx = err * (1.0 / Dm)
        dx_ref[...] = dx
        dxa_ref[...] = dx.astype(dxa_ref.dtype)
        l_ref[...] += jnp.sum(err * err) * (0.5 / Dm)

    spec = pl.BlockSpec((tm, Dm), lambda i: (i, 0))
    dx, dxa, l = pl.pallas_call(
        body, name=name, grid=(M // tm,),
        in_specs=[spec, pl.BlockSpec((tm, K), lambda i: (i, 0)), pl.BlockSpec((K, Dm), lambda i: (0, 0)), spec],
        out_specs=[spec, spec, pl.BlockSpec((8, 128), lambda i: (0, 0))],
        out_shape=[jax.ShapeDtypeStruct((M, Dm), F32), jax.ShapeDtypeStruct((M, Dm), ACT),
                   jax.ShapeDtypeStruct((8, 128), F32)],
        compiler_params=_cp("arbitrary"),
    )(x, y, w, target)
    return l[0, 0], dx, dxa


def _dw_in(h, dproj, name, comm=None):
    M, Dm = h.shape
    nl = dproj.shape[1] // N_DEV
    tt = min(M, 2048)

    def body(a_ref, b_ref, o_ref):
        @pl.when(pl.program_id(1) == 0)
        def _():
            o_ref[...] = jnp.zeros_like(o_ref)

        o_ref[...] += _dot_tn(a_ref[...], b_ref[...])

    outs = _launch(
        body, name=name, grid=(N_DEV, M // tt),
        in_specs=[pl.BlockSpec((tt, Dm), lambda j, t: (t, 0)), pl.BlockSpec((tt, nl), lambda j, t: (t, j))],
        out_specs=[pl.BlockSpec((None, Dm, nl), lambda j, t: (_chunk_slot(j), 0, 0))],
        out_shape=[jax.ShapeDtypeStruct((N_DEV, Dm, nl), F32)],
        args=(h, dproj), sem=("parallel", "arbitrary"), comm=comm)
    return outs[0] if comm is None else outs


def _dw_out(y, dout, name, comm=None):
    M, K = y.shape
    Dm = dout.shape[1]
    kl = K // N_DEV
    tt = min(M, 512)

    def body(a_ref, b_ref, o_ref):
        @pl.when(pl.program_id(0) == 0)
        def _():
            o_ref[...] = jnp.zeros_like(o_ref)

        b = b_ref[...]
        for j in range(N_DEV):
            o_ref[_chunk_slot(j)] += _dot_tn(a_ref[:, j * kl:(j + 1) * kl], b)

    outs = _launch(
        body, name=name, grid=(M // tt,),
        in_specs=[pl.BlockSpec((tt, K), lambda t: (t, 0)), pl.BlockSpec((tt, Dm), lambda t: (t, 0))],
        out_specs=[pl.BlockSpec((N_DEV, kl, Dm), lambda t: (0, 0, 0))],
        out_shape=[jax.ShapeDtypeStruct((N_DEV, kl, Dm), F32)],
        args=(y, dout), sem=("arbitrary",), comm=comm)
    return outs[0] if comm is None else outs


def _dh_norm_bwd(dproj, w_dm, x, gain, dres, name, comm=None):
    M, Dm = x.shape
    nd, _, nl = w_dm.shape
    tm = min(M, 1024)
    rows_bytes = tm * Dm * (4 + 2 * 4 + 2 * 4 + 2 * 4 + 2 * 2)
    block_bytes = 2 * (tm * nl + Dm * nl) * 2
    pair = 2 if rows_bytes + 2 * block_bytes <= VMEM_LIMIT - 8 * 1024 * 1024 else 1
    nj = nd // pair

    def body(dp_ref, w_ref, x_ref, g_ref, dr_ref, dx_ref, dxa_ref, dg_ref, acc_ref):
        i, j = pl.program_id(0), pl.program_id(1)

        @pl.when(j == 0)
        def _():
            acc_ref[...] = jnp.zeros_like(acc_ref)

        acc_ref[...] += functools.reduce(
            lambda a, b: a + b, [_dot_nt(dp_ref[:, d * nl:(d + 1) * nl], w_ref[d]) for d in range(pair)])

        @pl.when(j == nj - 1)
        def _():
            @pl.when(i == 0)
            def _():
                dg_ref[...] = jnp.zeros_like(dg_ref)

            dh = acc_ref[...]
            xv = x_ref[...]
            r = lax.rsqrt(jnp.mean(xv * xv, axis=-1, keepdims=True) + EPS)
            xn = xv * r
            dg_ref[...] += jnp.sum(dh * xn, axis=0, keepdims=True)
            dxn = dh * g_ref[...]
            dx = dr_ref[...] + r * (dxn - xn * jnp.mean(dxn * xn, axis=-1, keepdims=True))
            dx_ref[...] = dx
            dxa_ref[...] = dx.astype(dxa_ref.dtype)

    row = pl.BlockSpec((tm, Dm), lambda i, j: (i, 0))
    return _launch(
        body, name=name, grid=(M // tm, nj),
        in_specs=[pl.BlockSpec((tm, pair * nl), lambda i, j: (i, j)),
                  pl.BlockSpec((pair, Dm, nl), lambda i, j: (j, 0, 0)),
                  row, pl.BlockSpec((1, Dm), lambda i, j: (0, 0)), row],
        out_specs=[row, row, pl.BlockSpec((1, Dm), lambda i, j: (0, 0))],
        out_shape=[jax.ShapeDtypeStruct((M, Dm), F32), jax.ShapeDtypeStruct((M, Dm), ACT),
                   jax.ShapeDtypeStruct((1, Dm), F32)],
        scratch=[pltpu.VMEM((tm, Dm), F32)],
        args=(dproj, w_dm, x, gain, dres), sem=("arbitrary", "arbitrary"), comm=comm)


def _tril_mask():
    return lax.broadcasted_iota(jnp.int32, (CHUNK, CHUNK), 0) >= lax.broadcasted_iota(jnp.int32, (CHUNK, CHUNK), 1)


def _a_mid(proj, v_gain, w_s, b_st, name, comm=None):
    M = proj.shape[0]
    W = proj.shape[1] // 3
    gd = W // A_GROUPS
    tm = min(M, 256)

    def body(p_ref, vg_ref, ws_ref, bs_ref, y_ref):
        pv = p_ref[:, W:2 * W].astype(F32)
        r = lax.rsqrt(jnp.mean(pv * pv, axis=-1, keepdims=True) + EPS)
        v = (pv * r * vg_ref[...]).astype(MXU)
        tri = _tril_mask()
        for g in range(A_GROUPS):
            wg = jnp.where(tri, ws_ref[g], 0.0).astype(MXU)
            bcol = bs_ref[:, g:g + 1]
            for c in range(tm // CHUNK):
                rows, cols = slice(c * CHUNK, (c + 1) * CHUNK), slice(g * gd, (g + 1) * gd)
                mixed = jnp.dot(wg, v[rows, cols], preferred_element_type=F32) + bcol
                u = p_ref[rows, g * gd:(g + 1) * gd].astype(F32)
                z = p_ref[rows, 2 * W + g * gd:2 * W + (g + 1) * gd].astype(F32)
                y_ref[rows, cols] = (u * mixed * (z * _sigmoid(z))).astype(y_ref.dtype)

    return _launch(
        body, name=name, grid=(M // tm,),
        in_specs=[pl.BlockSpec((tm, 3 * W), lambda i: (i, 0)),
                  pl.BlockSpec((1, W), lambda i: (0, 0)),
                  pl.BlockSpec((A_GROUPS, CHUNK, CHUNK), lambda i: (0, 0, 0)),
                  pl.BlockSpec((CHUNK, A_GROUPS), lambda i: (0, 0))],
        out_specs=[pl.BlockSpec((tm, W), lambda i: (i, 0))],
        out_shape=[jax.ShapeDtypeStruct((M, W), ACT)],
        args=(proj, v_gain, w_s, b_st), sem=("parallel",), comm=comm)


def _a_bwd(dout, w_out, proj, v_gain, w_s, b_st, name, comm=None):
    M = proj.shape[0]
    W = proj.shape[1] // 3
    Dm = dout.shape[1]
    gd = W // A_GROUPS
    tm = min(M, 256)
    nt = M // tm

    def body(do_ref, wo_ref, p_ref, vg_ref, ws_ref, bs_ref, dp_ref, dws_ref, dbs_ref, dvg_ref, dv_s):
        i = pl.program_id(0)

        @pl.when(i == 0)
        def _():
            dws_ref[...] = jnp.zeros_like(dws_ref)
            dbs_ref[...] = jnp.zeros_like(dbs_ref)
            dvg_ref[...] = jnp.zeros_like(dvg_ref)

        dy = _dot_nt(do_ref[...], wo_ref[...])
        pv = p_ref[:, W:2 * W].astype(F32)
        r = lax.rsqrt(jnp.mean(pv * pv, axis=-1, keepdims=True) + EPS)
        pvn = pv * r
        vg = vg_ref[...]
        v = (pvn * vg).astype(MXU)
        tri = _tril_mask()
        for g in range(A_GROUPS):
            wf = jnp.where(tri, ws_ref[g], 0.0)
            wg = wf.astype(MXU)
            wgt = wf.T.astype(MXU)
            bcol = bs_ref[:, g:g + 1]
            for c in range(tm // CHUNK):
                rows, cols = slice(c * CHUNK, (c + 1) * CHUNK), slice(g * gd, (g + 1) * gd)
                vb = v[rows, cols]
                mixed = jnp.dot(wg, vb, preferred_element_type=F32) + bcol
                u = p_ref[rows, g * gd:(g + 1) * gd].astype(F32)
                z = p_ref[rows, 2 * W + g * gd:2 * W + (g + 1) * gd].astype(F32)
                sig = _sigmoid(z)
                sz = z * sig
                dyb = dy[rows, cols]
                dp_ref[rows, g * gd:(g + 1) * gd] = (dyb * mixed * sz).astype(dp_ref.dtype)
                dp_ref[rows, 2 * W + g * gd:2 * W + (g + 1) * gd] = (
                    dyb * u * mixed * (sig * (1.0 + z * (1.0 - sig)))).astype(dp_ref.dtype)
                dmix = dyb * u * sz
                dws_ref[g] += _dot_nt(dmix, vb)
                dbs_ref[:, g:g + 1] += jnp.sum(dmix, axis=1, keepdims=True)
                dv_s[rows, cols] = jnp.dot(wgt, dmix.astype(MXU), preferred_element_type=F32)
        dv = dv_s[...]
        dvg_ref[...] += jnp.sum(dv * pvn, axis=0, keepdims=True)
        dpvn = dv * vg
        dp_ref[:, W:2 * W] = (r * (dpvn - pvn * jnp.mean(dpvn * pvn, axis=-1, keepdims=True))).astype(dp_ref.dtype)

        @pl.when(i == nt - 1)
        def _():
            for g in range(A_GROUPS):
                dws_ref[g] = jnp.where(tri, dws_ref[g], 0.0)

    return _launch(
        body, name=name, grid=(nt,),
        in_specs=[pl.BlockSpec((tm, Dm), lambda i: (i, 0)),
                  pl.BlockSpec((W, Dm), lambda i: (0, 0)),
                  pl.BlockSpec((tm, 3 * W), lambda i: (i, 0)),
                  pl.BlockSpec((1, W), lambda i: (0, 0)),
                  pl.BlockSpec((A_GROUPS, CHUNK, CHUNK), lambda i: (0, 0, 0)),
                  pl.BlockSpec((CHUNK, A_GROUPS), lambda i: (0, 0))],
        out_specs=[pl.BlockSpec((tm, 3 * W), lambda i: (i, 0)),
                   pl.BlockSpec((A_GROUPS, CHUNK, CHUNK), lambda i: (0, 0, 0)),
                   pl.BlockSpec((CHUNK, A_GROUPS), lambda i: (0, 0)),
                   pl.BlockSpec((1, W), lambda i: (0, 0))],
        out_shape=[jax.ShapeDtypeStruct((M, 3 * W), ACT),
                   jax.ShapeDtypeStruct((A_GROUPS, CHUNK, CHUNK), F32),
                   jax.ShapeDtypeStruct((CHUNK, A_GROUPS), F32),
                   jax.ShapeDtypeStruct((1, W), F32)],
        scratch=[pltpu.VMEM((tm, W), F32)],
        args=(dout, w_out, proj, v_gain, w_s, b_st), sem=("arbitrary",), comm=comm)


def _pool_diff(xg, tail, i, tm, w):
    t = lax.broadcasted_iota(jnp.int32, (tm, tm + POOL_HALO), 0)
    s = lax.broadcasted_iota(jnp.int32, (tm, tm + POOL_HALO), 1)
    off = t - (s - POOL_HALO)
    band = jnp.where((off >= 0) & (off < w), 1.0, 0.0).astype(MXU)
    tail = jnp.where(i > 0, tail, jnp.zeros_like(tail))
    ext = jnp.concatenate([tail, xg], axis=0)
    ssum = jnp.dot(band, ext.astype(MXU), preferred_element_type=F32)
    tglob = i * tm + lax.broadcasted_iota(jnp.int32, (tm, 1), 0)
    cnt = jnp.minimum(tglob + 1, w).astype(F32)
    return ssum / cnt - xg.astype(F32)


def _c_mid(proj, w_grp, scale, name):
    M = proj.shape[0]
    W = proj.shape[1] // 2
    ng = len(POOL_SIZES)
    cg = W // ng
    tm = min(M, 256)
    hb = tm // POOL_HALO

    def body(xc_ref, tail_ref, z_ref, wg_ref, sc_ref, y_ref):
        i = pl.program_id(0)
        for g, w in enumerate(POOL_SIZES):
            cols = slice(g * cg, (g + 1) * cg)
            d = _pool_diff(xc_ref[:, cols], tail_ref[:, cols], i, tm, w)
            mixed = _dot(d, wg_ref[g]) * sc_ref[:, cols]
            z = z_ref[:, cols].astype(F32)
            y_ref[:, cols] = (mixed * (z * _sigmoid(z))).astype(y_ref.dtype)

    return pl.pallas_call(
        body, name=name, grid=(M // tm,),
        in_specs=[pl.BlockSpec((tm, W), lambda i: (i, 0)),
                  pl.BlockSpec((POOL_HALO, W), lambda i: (jnp.maximum(i * hb - 1, 0), 0)),
                  pl.BlockSpec((tm, W), lambda i: (i, 1)),
                  pl.BlockSpec((ng, cg, cg), lambda i: (0, 0, 0)),
                  pl.BlockSpec((1, W), lambda i: (0, 0))],
        out_specs=pl.BlockSpec((tm, W), lambda i: (i, 0)),
        out_shape=jax.ShapeDtypeStruct((M, W), ACT),
        compiler_params=_cp("parallel"),
    )(proj, proj, proj, w_grp, scale)


def _c_bwd1(dout, w_out, proj, w_grp, scale, name, comm=None):
    M = proj.shape[0]
    W = proj.shape[1] // 2
    Dm = dout.shape[1]
    ng = len(POOL_SIZES)
    cg = W // ng
    rl = cg // N_DEV
    tm = min(M, 256)
    hb = tm // POOL_HALO
    nt = M // tm

    def body(do_ref, wo_ref, xc_ref, tail_ref, z_ref, wg_ref, sc_ref, dd_ref, dz_ref, dwg_ref, dsc_ref, acc_ref):
        i = pl.program_id(0)

        @pl.when(i == 0)
        def _():
            acc_ref[...] = jnp.zeros_like(acc_ref)
            dsc_ref[...] = jnp.zeros_like(dsc_ref)

        dy = _dot_nt(do_ref[...], wo_ref[...])
        for g, w in enumerate(POOL_SIZES):
            cols = slice(g * cg, (g + 1) * cg)
            d = _pool_diff(xc_ref[:, cols], tail_ref[:, cols], i, tm, w)
            mr = _dot(d, wg_ref[g])
            sc = sc_ref[:, cols]
            z = z_ref[:, cols].astype(F32)
            sig = _sigmoid(z)
            dyg = dy[:, cols]
            dmixed = dyg * (z * sig)
            dz_ref[:, cols] = (dyg * (mr * sc) * (sig * (1.0 + z * (1.0 - sig)))).astype(dz_ref.dtype)
            dsc_ref[:, cols] += jnp.sum(dmixed * mr, axis=0, keepdims=True)
            dmr = (dmixed * sc).astype(MXU)
            acc_ref[g] += _dot_tn(d, dmr)
            dd_ref[:, cols] = _dot_nt(dmr, wg_ref[g]).astype(dd_ref.dtype)

        @pl.when(i == nt - 1)
        def _():
            for dev in range(N_DEV):
                for g in range(ng):
                    dwg_ref[_chunk_slot(dev), g] = acc_ref[g, dev * rl:(dev + 1) * rl, :]

    return _launch(
        body, name=name, grid=(nt,),
        in_specs=[pl.BlockSpec((tm, Dm), lambda i: (i, 0)),
                  pl.BlockSpec((W, Dm), lambda i: (0, 0)),
                  pl.BlockSpec((tm, W), lambda i: (i, 0)),
                  pl.BlockSpec((POOL_HALO, W), lambda i: (jnp.maximum(i * hb - 1, 0), 0)),
                  pl.BlockSpec((tm, W), lambda i: (i, 1)),
                  pl.BlockSpec((ng, cg, cg), lambda i: (0, 0, 0)),
                  pl.BlockSpec((1, W), lambda i: (0, 0))],
        out_specs=[pl.BlockSpec((tm, W), lambda i: (i, 0)),
                   pl.BlockSpec((tm, W), lambda i: (i, 0)),
                   pl.BlockSpec((N_DEV, ng, rl, cg), lambda i: (0, 0, 0, 0)),
                   pl.BlockSpec((1, W), lambda i: (0, 0))],
        out_shape=[jax.ShapeDtypeStruct((M, W), ACT), jax.ShapeDtypeStruct((M, W), ACT),
                   jax.ShapeDtypeStruct((N_DEV, ng, rl, cg), F32), jax.ShapeDtypeStruct((1, W), F32)],
        scratch=[pltpu.VMEM((ng, cg, cg), F32)],
        args=(dout, w_out, proj, proj, proj, w_grp, scale), sem=("arbitrary",), comm=comm)


def _c_bwd2(dd, dz, name):
    M, W = dd.shape
    ng = len(POOL_SIZES)
    cg = W // ng
    tm = min(M, 256)
    hb = tm // POOL_HALO
    nt = M // tm

    def body(dd_ref, head_ref, dz_ref, dp_ref):
        i = pl.program_id(0)
        s = lax.broadcasted_iota(jnp.int32, (tm, tm + POOL_HALO), 0)
        t = lax.broadcasted_iota(jnp.int32, (tm, tm + POOL_HALO), 1)
        off = t - s
        tglob = i * tm + lax.broadcasted_iota(jnp.int32, (tm + POOL_HALO, 1), 0)
        for g, w in enumerate(POOL_SIZES):
            cols = slice(g * cg, (g + 1) * cg)
            ddg = dd_ref[:, cols].astype(F32)
            head = head_ref[:, cols].astype(F32)
            head = jnp.where(i < nt - 1, head, jnp.zeros_like(head))
            cnt = jnp.minimum(tglob + 1, w).astype(F32)
            ext = (jnp.concatenate([ddg, head], axis=0) / cnt).astype(MXU)
            band = jnp.where((off >= 0) & (off < w), 1.0, 0.0).astype(MXU)
            dp_ref[:, cols] = (jnp.dot(band, ext, preferred_element_type=F32) - ddg).astype(dp_ref.dtype)
        dp_ref[:, W:] = dz_ref[...]

    return pl.pallas_call(
        body, name=name, grid=(nt,),
        in_specs=[pl.BlockSpec((tm, W), lambda i: (i, 0)),
                  pl.BlockSpec((POOL_HALO, W), lambda i: (jnp.minimum((i + 1) * hb, M // POOL_HALO - 1), 0)),
                  pl.BlockSpec((tm, W), lambda i: (i, 0))],
        out_specs=pl.BlockSpec((tm, 2 * W), lambda i: (i, 0)),
        out_shape=jax.ShapeDtypeStruct((M, 2 * W), ACT),
        compiler_params=_cp("parallel"),
    )(dd, dd, dz)


def _rope_tables(S):
    half = ROPE_DIM // 2
    inv_freq = jnp.power(jnp.float32(ROPE_THETA), -jnp.arange(half, dtype=F32) / half)
    ang = jnp.arange(S, dtype=F32)[:, None] * inv_freq[None, :]
    cos, sin = jnp.cos(ang), jnp.sin(ang)
    rest = HEAD_DIM - ROPE_DIM
    cf = jnp.concatenate([cos, cos, jnp.ones((S, rest), F32)], axis=1)
    sf = jnp.concatenate([-sin, sin, jnp.zeros((S, rest), F32)], axis=1)
    return cf, sf


def _swap_matrix():
    half = ROPE_DIM // 2
    a = lax.broadcasted_iota(jnp.int32, (HEAD_DIM, HEAD_DIM), 0)
    e = lax.broadcasted_iota(jnp.int32, (HEAD_DIM, HEAD_DIM), 1)
    hit = ((e < half) & (a == e + half)) | ((e >= half) & (e < 2 * half) & (a == e - half))
    return jnp.where(hit, 1.0, 0.0).astype(MXU)


def _b_qk_fwd(proj, tables, gains, name, comm=None):
    M = proj.shape[0]
    nsl = 2 * len(B_DILATIONS) * B_HEADS
    Wqk = nsl * HEAD_DIM
    tm = min(M, 256)

    def body(p_ref, cf_ref, sf_ref, g_ref, o_ref):
        cf, sf = cf_ref[...], sf_ref[...]
        swap = _swap_matrix()
        for j in range(nsl):
            cols = slice(j * HEAD_DIM, (j + 1) * HEAD_DIM)
            xv = p_ref[:, cols].astype(F32)
            r = lax.rsqrt(jnp.mean(xv * xv, axis=-1, keepdims=True) + EPS)
            xg = xv * g_ref[j // B_HEADS:j // B_HEADS + 1, :]
            hi = xg.astype(MXU)
            lo = (xg - hi.astype(F32)).astype(MXU)
            sw = jnp.dot(hi, swap, preferred_element_type=F32) + jnp.dot(lo, swap, preferred_element_type=F32)
            o_ref[:, cols] = (r * (xg * cf + sw * sf)).astype(o_ref.dtype)

    tspec = pl.BlockSpec((tm, HEAD_DIM), lambda i: (i, 0))
    return _launch(
        body, name=name, grid=(M // tm,),
        in_specs=[pl.BlockSpec((tm, Wqk), lambda i: (i, 0)), tspec, tspec,
                  pl.BlockSpec((8, HEAD_DIM), lambda i: (0, 0))],
        out_specs=[pl.BlockSpec((tm, Wqk), lambda i: (i, 0))],
        out_shape=[jax.ShapeDtypeStruct((M, Wqk), ACT)],
        args=(proj, *tables, gains), sem=("parallel",), comm=comm)


def _b_qk_bwd(dqs, dks, proj, tables, gains, dproj, name):
    M = proj.shape[0]
    ngr = len(B_DILATIONS)
    nsl = 2 * ngr * B_HEADS
    Wqk = nsl * HEAD_DIM
    Wg = B_HEADS * HEAD_DIM
    tm = min(M, 512)

    def body(*refs):
        d_refs = refs[:2 * ngr]
        p_ref, cf_ref, sf_ref, g_ref = refs[2 * ngr:2 * ngr + 4]
        dp_ref, dg_ref = refs[-2], refs[-1]

        @pl.when(pl.program_id(0) == 0)
        def _():
            dg_ref[...] = jnp.zeros_like(dg_ref)

        cf, sf = cf_ref[...], sf_ref[...]
        swap = _swap_matrix()
        for j in range(nsl):
            t, hh = j // B_HEADS, j % B_HEADS
            cols = slice(j * HEAD_DIM, (j + 1) * HEAD_DIM)
            dy = d_refs[t][:, hh * HEAD_DIM:(hh + 1) * HEAD_DIM].astype(F32)
            dxn = dy * cf + jnp.dot((dy * sf).astype(MXU), swap, preferred_element_type=F32)
            xv = p_ref[:, cols].astype(F32)
            r = lax.rsqrt(jnp.mean(xv * xv, axis=-1, keepdims=True) + EPS)
            xh = xv * r
            dg_ref[t:t + 1, :] += jnp.sum(dxn * xh, axis=0, keepdims=True)
            dxh = dxn * g_ref[t:t + 1, :]
            dp_ref[:, cols] = (r * (dxh - xh * jnp.mean(dxh * xh, axis=-1, keepdims=True))).astype(dp_ref.dtype)

    tspec = pl.BlockSpec((tm, HEAD_DIM), lambda i: (i, 0))
    dspec = pl.BlockSpec((tm, Wg), lambda i: (i, 0))
    n_in = 2 * ngr + 5
    return pl.pallas_call(
        body, name=name, grid=(M // tm,),
        in_specs=[dspec] * (2 * ngr) + [pl.BlockSpec((tm, Wqk), lambda i: (i, 0)), tspec, tspec,
                                        pl.BlockSpec((8, HEAD_DIM), lambda i: (0, 0)),
                                        pl.BlockSpec(memory_space=pl.ANY)],
        out_specs=[pl.BlockSpec((tm, Wqk), lambda i: (i, 0)), pl.BlockSpec((8, HEAD_DIM), lambda i: (0, 0))],
        out_shape=[jax.ShapeDtypeStruct(dproj.shape, dproj.dtype), jax.ShapeDtypeStruct((8, HEAD_DIM), F32)],
        input_output_aliases={n_in - 1: 0},
        compiler_params=_cp("arbitrary"),
    )(*dqs, *dks, proj, *tables, gains, dproj)


def _attn_tile(D, M):
    return max(HEAD_DIM * D, min(M, 2048))


class _TokenRows:
    GROUP = 16

    def __init__(self, D):
        self.D = D
        self.pitch = 24 if D == 16 else self.GROUP
        self.operand_dtype = F32 if D > 1 else ACT

    def rows(self, ntok):
        return ntok // self.GROUP * self.pitch

    def every_dth(self, tok0, n):
        start = tok0 // self.GROUP * self.pitch + tok0 % self.GROUP
        stride = self.D * self.pitch // self.GROUP
        return pl.ds(start, n) if stride == 1 else pl.ds(start, n, stride=stride)

    def put(self, dst, tok0, src_ref, ntok):
        if self.pitch == self.GROUP:
            dst[tok0:tok0 + ntok, :] = src_ref[...].astype(dst.dtype)
            return

        def group(i, carry):
            row = pl.multiple_of((tok0 // self.GROUP + i) * self.pitch, 8)
            dst[pl.ds(row, self.GROUP), :] = src_ref[pl.ds(pl.multiple_of(i * self.GROUP, self.GROUP), self.GROUP), :].astype(F32)
            return carry

        lax.fori_loop(0, ntok // self.GROUP, group, 0, unroll=8)

    def get(self, dst_ref, src, ntok):
        if self.pitch == self.GROUP:
            dst_ref[...] = src[0:ntok, :].astype(dst_ref.dtype)
            return

        def group(i, carry):
            row = pl.multiple_of(i * self.pitch, 8)
            dst_ref[pl.ds(pl.multiple_of(i * self.GROUP, self.GROUP), self.GROUP), :] = src[pl.ds(row, self.GROUP), :].astype(dst_ref.dtype)
            return carry

        lax.fori_loop(0, ntok // self.GROUP, group, 0, unroll=8)


def _attn_mask(base):
    qi = lax.broadcasted_iota(jnp.int32, (CHUNK, 2 * CHUNK), 0)
    ki = lax.broadcasted_iota(jnp.int32, (CHUNK, 2 * CHUNK), 1)
    return (ki >= qi) & (ki <= qi + CHUNK) & (ki >= CHUNK - base)


def _b_attn_fwd(qk, proj, g, name):
    M = qk.shape[0]
    D = B_DILATIONS[g]
    ngr = len(B_DILATIONS)
    T = _attn_tile(D, M)
    P = HEAD_DIM * D
    nsb = T // P
    Wg = B_HEADS * HEAD_DIM
    scale = np.float32(1.0 / np.sqrt(HEAD_DIM))

    lay = _TokenRows(D)
    RP, RT = lay.rows(P), lay.rows(T)

    def body(q_ref, k_ref, v_ref, o_ref, l_ref, qs, ks, vs, os_):
        n = pl.program_id(1)

        @pl.when(n == 0)
        def _():
            ks[0:RP, :] = jnp.zeros((RP, HEAD_DIM), ks.dtype)
            vs[0:RP, :] = jnp.zeros((RP, HEAD_DIM), vs.dtype)

        lay.put(qs, 0, q_ref, T)
        lay.put(ks, P, k_ref, T)
        lay.put(vs, P, v_ref, T)

        for b in range(nsb):
            mask = _attn_mask(n * (T // D) + b * CHUNK)
            for r in range(D):
                start = b * P + r
                q = qs[lay.every_dth(start, CHUNK), :]
                k = ks[lay.every_dth(start, 2 * CHUNK), :]
                v = vs[lay.every_dth(start, 2 * CHUNK), :]
                s = jnp.where(mask, _dot_nt(q, k) * scale, NEG)
                m = jnp.max(s, axis=-1, keepdims=True)
                p = jnp.exp(s - m)
                l = jnp.sum(p, axis=-1, keepdims=True)
                o = _dot(p, v) / l
                os_[lay.every_dth(start, CHUNK), :] = o
                l_ref[:, b * D + r:b * D + r + 1] = m + jnp.log(l)

        lay.get(o_ref, os_, T)
        ks[0:RP, :] = ks[RT:RT + RP, :]
        vs[0:RP, :] = vs[RT:RT + RP, :]

    blk = (T, HEAD_DIM)
    U = nsb * D
    return pl.pallas_call(
        body, name=name, grid=(B_HEADS, M // T),
        in_specs=[pl.BlockSpec(blk, lambda h, n: (n, g * B_HEADS + h)),
                  pl.BlockSpec(blk, lambda h, n: (n, (ngr + g) * B_HEADS + h)),
                  pl.BlockSpec(blk, lambda h, n: (n, (2 * ngr + g) * B_HEADS + h))],
        out_specs=[pl.BlockSpec(blk, lambda h, n: (n, h)), pl.BlockSpec((None, CHUNK, U), lambda h, n: (h, n, 0))],
        out_shape=[jax.ShapeDtypeStruct((M, Wg), ACT), jax.ShapeDtypeStruct((B_HEADS, (M // T) * CHUNK, U), F32)],
        scratch_shapes=[pltpu.VMEM((RT, HEAD_DIM), lay.operand_dtype), pltpu.VMEM((RP + RT, HEAD_DIM), lay.operand_dtype),
                        pltpu.VMEM((RP + RT, HEAD_DIM), lay.operand_dtype), pltpu.VMEM((RT, HEAD_DIM), F32)],
        compiler_params=_cp("parallel", "arbitrary"),
    )(qk, qk, proj)


def _units_to_tokens(a, D, T):
    H = a.shape[0]
    nsb = T // (HEAD_DIM * D)
    return a.reshape(H, -1, CHUNK, nsb, D).transpose(1, 3, 2, 4, 0).reshape(-1, H)


def _tokens_to_units(a, D, T):
    M, H = a.shape
    nsb = T // (HEAD_DIM * D)
    return a.reshape(M // T, nsb, CHUNK, D, H).transpose(4, 0, 2, 1, 3).reshape(H, (M // T) * CHUNK, nsb * D)


def _b_combine(os_, ls, proj, name):
    M, Wg = os_[0].shape
    ngr = len(B_DILATIONS)
    tm = min(M, 512)

    def body(*refs):
        o_refs, l_refs, z_ref = refs[:ngr], refs[ngr:2 * ngr], refs[2 * ngr]
        y_ref, o_ref, lse_ref = refs[2 * ngr + 1:]
        for h in range(B_HEADS):
            cols = slice(h * HEAD_DIM, (h + 1) * HEAD_DIM)
            ls_ = [r[:, h:h + 1] for r in l_refs]
            m = functools.reduce(jnp.maximum, ls_)
            es = [jnp.exp(l - m) for l in ls_]
            tot = functools.reduce(lambda a, b: a + b, es)
            o = functools.reduce(lambda a, b: a + b, [(e / tot) * r[:, cols].astype(F32) for e, r in zip(es, o_refs)])
            z = z_ref[:, cols].astype(F32)
            y_ref[:, cols] = (o * (z * _sigmoid(z))).astype(y_ref.dtype)
            o_ref[:, cols] = o.astype(o_ref.dtype)
            lse_ref[:, h:h + 1] = m + jnp.log(tot)

    spec = pl.BlockSpec((tm, Wg), lambda i: (i, 0))
    hspec = pl.BlockSpec((tm, B_HEADS), lambda i: (i, 0))
    return pl.pallas_call(
        body, name=name, grid=(M // tm,),
        in_specs=[spec] * ngr + [hspec] * ngr + [pl.BlockSpec((tm, Wg), lambda i: (i, 3 * ngr))],
        out_specs=[spec, spec, hspec],
        out_shape=[jax.ShapeDtypeStruct((M, Wg), ACT), jax.ShapeDtypeStruct((M, Wg), ACT),
                   jax.ShapeDtypeStruct((M, B_HEADS), F32)],
        compiler_params=_cp("parallel"),
    )(*os_, *ls, proj)


def _b_bwd_pre(dout, w_out, o, proj, name):
    M, Wg = o.shape
    Dm = dout.shape[1]
    ngr = len(B_DILATIONS)
    tm = min(M, 512)

    def body(do_ref, wo_ref, o_ref, z_ref, dov_ref, dl_ref, dp_ref):
        dy = _dot_nt(do_ref[...], wo_ref[...])
        z = z_ref[...].astype(F32)
        sig = _sigmoid(z)
        ov = o_ref[...].astype(F32)
        dp_ref[...] = (dy * ov * (sig * (1.0 + z * (1.0 - sig)))).astype(dp_ref.dtype)
        dov = dy * (z * sig)
        dov_ref[...] = dov.astype(dov_ref.dtype)
        prod = dov * ov
        for h in range(B_HEADS):
            dl_ref[:, h:h + 1] = jnp.sum(prod[:, h * HEAD_DIM:(h + 1) * HEAD_DIM], axis=-1, keepdims=True)

    spec = pl.BlockSpec((tm, Wg), lambda i: (i, 0))
    zspec = pl.BlockSpec((tm, Wg), lambda i: (i, 3 * ngr))
    return pl.pallas_call(
        body, name=name, grid=(M // tm,),
        in_specs=[pl.BlockSpec((tm, Dm), lambda i: (i, 0)), pl.BlockSpec((Wg, Dm), lambda i: (0, 0)), spec, zspec],
        out_specs=[spec, pl.BlockSpec((tm, B_HEADS), lambda i: (i, 0)), zspec],
        out_shape=[jax.ShapeDtypeStruct((M, Wg), ACT), jax.ShapeDtypeStruct((M, B_HEADS), F32),
                   jax.ShapeDtypeStruct(proj.shape, ACT)],
        compiler_params=_cp("parallel"),
    )(dout, w_out, o, proj)


def _b_attn_bwd(qk, proj, dov, lse, delta, dproj, g, name, comm=None):
    M = qk.shape[0]
    D = B_DILATIONS[g]
    ngr = len(B_DILATIONS)
    T = _attn_tile(D, M)
    P = HEAD_DIM * D
    nsb = T // P
    nt = M // T
    Wg = B_HEADS * HEAD_DIM
    scale = np.float32(1.0 / np.sqrt(HEAD_DIM))
    shift = T - P
    lay = _TokenRows(D)
    RP, RT = lay.rows(P), lay.rows(T)

    def body(q_ref, k_ref, v_ref, do_ref, l_ref, dl_ref, dp_any, dq_ref, dk_ref, dv_ref,
             qs, dos, ks, vs, dqs, dks, dvs):
        n = pl.program_id(1)

        @pl.when(n == 0)
        def _():
            ks[0:RP, :] = jnp.zeros((RP, HEAD_DIM), ks.dtype)
            vs[0:RP, :] = jnp.zeros((RP, HEAD_DIM), vs.dtype)
            dks[...] = jnp.zeros((2 * RT, HEAD_DIM), F32)
            dvs[...] = jnp.zeros((2 * RT, HEAD_DIM), F32)

        @pl.when(n < nt)
        def _():
            lay.put(qs, 0, q_ref, T)
            lay.put(dos, 0, do_ref, T)
            lay.put(ks, P, k_ref, T)
            lay.put(vs, P, v_ref, T)

            masks = [_attn_mask(n * (T // D) + b * CHUNK) for b in range(nsb)]
            for r in range(D):
                carry_dv = carry_dk = None
                for b in range(nsb):
                    start = b * P + r
                    qsl = lay.every_dth(start, CHUNK)
                    ksl = lay.every_dth(start, 2 * CHUNK)
                    lo = lay.every_dth(start + shift, CHUNK)
                    q = qs[qsl, :]
                    do = dos[qsl, :]
                    k = ks[ksl, :]
                    v = vs[ksl, :]
                    s = _dot_nt(q, k) * scale
                    u = b * D + r
                    p = jnp.where(masks[b], jnp.exp(s - l_ref[:, u:u + 1]), 0.0)
                    dv = _dot_tn(p, do)
                    dp = _dot_nt(do, v)
                    ds = (p * (dp - dl_ref[:, u:u + 1]) * scale).astype(MXU)
                    dqs[qsl, :] = _dot(ds, k)
                    dk = _dot_tn(ds, q)
                    if b == 0:
                        dvs[lo, :] += dv[:CHUNK]
                        dks[lo, :] += dk[:CHUNK]
                    else:
                        dvs[lo, :] = carry_dv + dv[:CHUNK]
                        dks[lo, :] = carry_dk + dk[:CHUNK]
                    carry_dv, carry_dk = dv[CHUNK:], dk[CHUNK:]
                hi = lay.every_dth((nsb - 1) * P + r + shift + P, CHUNK)
                dvs[hi, :] = carry_dv
                dks[hi, :] = carry_dk

        lay.get(dq_ref, dqs, T)
        lay.get(dk_ref, dks, T)
        lay.get(dv_ref, dvs, T)
        dks[0:RT, :] = dks[RT:2 * RT, :]
        dvs[0:RT, :] = dvs[RT:2 * RT, :]
        ks[0:RP, :] = ks[RT:RT + RP, :]
        vs[0:RP, :] = vs[RT:RT + RP, :]

    blk = (T, HEAD_DIM)
    cur = lambda n: jnp.minimum(n, nt - 1)
    prv = lambda n: jnp.maximum(n - 1, 0)
    return _launch(
        body, name=name, grid=(B_HEADS, nt + 1),
        in_specs=[pl.BlockSpec(blk, lambda h, n: (cur(n), g * B_HEADS + h)),
                  pl.BlockSpec(blk, lambda h, n: (cur(n), (ngr + g) * B_HEADS + h)),
                  pl.BlockSpec(blk, lambda h, n: (cur(n), (2 * ngr + g) * B_HEADS + h)),
                  pl.BlockSpec(blk, lambda h, n: (cur(n), h)),
                  pl.BlockSpec((None, CHUNK, nsb * D), lambda h, n: (h, cur(n), 0)),
                  pl.BlockSpec((None, CHUNK, nsb * D), lambda h, n: (h, cur(n), 0)),
                  pl.BlockSpec(memory_space=pl.ANY)],
        out_specs=[pl.BlockSpec(blk, lambda h, n: (cur(n), h)),
                   pl.BlockSpec(blk, lambda h, n: (prv(n), h)),
                   pl.BlockSpec(blk, lambda h, n: (prv(n), (2 * ngr + g) * B_HEADS + h))],
        out_shape=[jax.ShapeDtypeStruct((M, Wg), ACT), jax.ShapeDtypeStruct((M, Wg), ACT),
                   jax.ShapeDtypeStruct(dproj.shape, dproj.dtype)],
        scratch=[pltpu.VMEM((RT, HEAD_DIM), lay.operand_dtype)] * 2
        + [pltpu.VMEM((RP + RT, HEAD_DIM), lay.operand_dtype)] * 2
        + [pltpu.VMEM((RT, HEAD_DIM), F32)]
        + [pltpu.VMEM((2 * RT, HEAD_DIM), F32)] * 2,
        aliases={6: 2},
        args=(qk, qk, proj, dov, lse, delta, dproj), sem=("parallel", "arbitrary"), comm=comm)


def _coords():
    return lax.axis_index("x"), lax.axis_index("y"), lax.axis_index("c")


def _gather_blocks(x_refs, out_refs, send_sems, recv_sems, local_sems):
    x, y, c = _coords()
    me, sibling = (x, y, c), (x, y, 1 - c)
    chips = [(1 - x, y), (x, 1 - y), (1 - x, 1 - y)]
    arrays = range(len(x_refs))

    def slot(a, px, py, pc):
        return out_refs[a].at[4 * px + 2 * py + pc]

    def copy(a, k, block, to, src=None):
        return _remote(slot(a, *block) if src is None else src, slot(a, *block), send_sems, recv_sems, 7 * a + k, to)

    mine = [pltpu.make_async_copy(x_refs[a], slot(a, *me), local_sems.at[a]) for a in arrays]
    first = [copy(a, 0, me, sibling, src=x_refs[a]) for a in arrays]
    first += [copy(a, 1 + j, me, (*chip, c), src=x_refs[a]) for j, chip in enumerate(chips) for a in arrays]
    for cp in mine + first:
        cp.start()
    passed = []
    for j, chip in enumerate(chips):
        for a in arrays:
            copy(a, 1 + j, (*chip, c), me).wait_recv()
            passed.append(copy(a, 4 + j, (*chip, c), sibling))
            passed[-1].start()
    for a in arrays:
        copy(a, 0, sibling, me).wait_recv()
        for j, chip in enumerate(chips):
            copy(a, 4 + j, (*chip, 1 - c), me).wait_recv()
    for cp in first + passed:
        cp.wait_send()
    for cp in mine:
        cp.wait()


def _all_gather_hbm(arrays, name):
    n = len(arrays)

    def body(*refs):
        _gather_blocks(refs[:n], refs[n:2 * n], *refs[2 * n:])

    return pl.pallas_call(
        body, name=name, in_specs=[_HBM] * n, out_specs=[_HBM] * n,
        out_shape=[jax.ShapeDtypeStruct((N_DEV,) + a.shape, a.dtype) for a in arrays],
        scratch_shapes=[pltpu.SemaphoreType.DMA((7 * n,)), pltpu.SemaphoreType.DMA((7 * n,)),
                        pltpu.SemaphoreType.DMA((n,))],
    )(*arrays)


def _all_reduce_small(part):
    R, C = part.shape

    def body(x_ref, tot_ref, gath, send_sems, recv_sems, local_sems):
        _gather_blocks([x_ref], [gath], send_sems, recv_sems, local_sems)
        acc = gath[0]
        for d in range(1, N_DEV):
            acc = acc + gath[d]
        tot_ref[...] = acc

    return pl.pallas_call(
        body, name="ar_small",
        in_specs=[pl.BlockSpec(memory_space=pltpu.VMEM)],
        out_specs=pl.BlockSpec(memory_space=pltpu.VMEM),
        out_shape=jax.ShapeDtypeStruct((R, C), F32),
        scratch_shapes=[pltpu.VMEM((N_DEV, R, C), F32),
                        pltpu.SemaphoreType.DMA((7,)), pltpu.SemaphoreType.DMA((7,)), pltpu.SemaphoreType.DMA((1,))],
        compiler_params=pltpu.CompilerParams(vmem_limit_bytes=VMEM_LIMIT),
    )(part)


def _sum_blocks(gath, name):
    _, R, C = gath.shape

    def body(g_ref, o_ref):
        acc = g_ref[0]
        for d in range(1, N_DEV):
            acc = acc + g_ref[d]
        o_ref[...] = acc

    return pl.pallas_call(
        body, name=name,
        in_specs=[pl.BlockSpec(memory_space=pltpu.VMEM)], out_specs=pl.BlockSpec(memory_space=pltpu.VMEM),
        out_shape=jax.ShapeDtypeStruct((R, C), F32),
        compiler_params=pltpu.CompilerParams(vmem_limit_bytes=VMEM_LIMIT),
    )(gath)


def _remote(src, dst, send_sems, recv_sems, k, peer):
    return pltpu.make_async_remote_copy(src_ref=src, dst_ref=dst, send_sem=send_sems.at[k], recv_sem=recv_sems.at[k],
                                        device_id=peer, device_id_type=MESH)


def _ag_send(arrays):
    n = len(arrays)

    def make(c_in, c_out, send_sems, recv_sems, local_sems):
        x, y, c = _coords()
        peers = [(x, y, 1 - c), (1 - x, y, c), (x, 1 - y, c), (1 - x, 1 - y, c)]
        cps = []
        for a in range(n):
            src, dst = c_in[a], c_out[a].at[4 * x + 2 * y + c]
            cps.append(pltpu.make_async_copy(src, dst, local_sems.at[a]))
            cps += [_remote(src, dst, send_sems, recv_sems, 4 * a + k, peer) for k, peer in enumerate(peers)]
        return cps

    return _Comm(arrays, [jax.ShapeDtypeStruct((N_DEV,) + a.shape, a.dtype) for a in arrays], 4 * n, make, n_local=n)


def _ag_forward(gaths):
    n = len(gaths)

    def make(c_in, c_out, send_sems, recv_sems, local_sems):
        x, y, c = _coords()
        chips = [(1 - x, y), (x, 1 - y), (1 - x, 1 - y)]
        cps = []
        for a in range(n):
            buf = c_out[a]
            cps += [_remote(buf.at[4 * px + 2 * py + c], buf.at[4 * px + 2 * py + c], send_sems, recv_sems, 3 * a + j,
                            (x, y, 1 - c)) for j, (px, py) in enumerate(chips)]
        return cps

    return _Comm(gaths, [jax.ShapeDtypeStruct(g.shape, g.dtype) for g in gaths], 3 * n, make,
                 aliases={a: a for a in range(n)})


def _rs_sibling(grads):
    n = len(grads)

    def make(c_in, c_out, send_sems, recv_sems, local_sem):
        x, y, c = _coords()
        return [_remote(c_in[a].at[pl.ds(4 * (1 - c), 4)], c_out[a], send_sems, recv_sems, a, (x, y, 1 - c))
                for a in range(n)]

    return _Comm(grads, [jax.ShapeDtypeStruct((4,) + g.shape[1:], g.dtype) for g in grads], n, make)


def _rs_chips(parts):
    n = len(parts)

    def make(c_in, c_out, send_sems, recv_sems, local_sem):
        x, y, c = _coords()
        peers = [(x, 1 - y, c), (1 - x, y, c), (1 - x, 1 - y, c)]
        return [_remote(c_in[a].at[k], c_out[a].at[k], send_sems, recv_sems, 3 * a + k, peer)
                for a in range(n) for k, peer in enumerate(peers)]

    return _Comm(parts, [jax.ShapeDtypeStruct(p.shape, p.dtype) for p in parts], 3 * n, make)


def _row_tile(rows, cols):
    tr = min(rows, 1 << int(np.log2((1 << 18) // cols)))
    assert rows % tr == 0
    return tr


def _chip_partials(coords, g, r1, name):
    _, rows, C = g.shape
    tr = _row_tile(rows, C)

    def body(co_ref, g_ref, r_ref, o_ref):
        o_ref[...] = (g_ref[...] + r_ref[...]).astype(o_ref.dtype)

    def chip(k, co):
        return jnp.bitwise_xor(2 * co[0] + co[1], k + 1)

    return pl.pallas_call(
        body, name=name,
        grid_spec=pltpu.PrefetchScalarGridSpec(
            num_scalar_prefetch=1, grid=(3, rows // tr),
            in_specs=[pl.BlockSpec((None, tr, C), lambda k, t, co: (4 * co[2] + chip(k, co), t, 0)),
                      pl.BlockSpec((None, tr, C), lambda k, t, co: (chip(k, co), t, 0))],
            out_specs=pl.BlockSpec((None, tr, C), lambda k, t, co: (k, t, 0))),
        out_shape=jax.ShapeDtypeStruct((3, rows, C), WIRE),
        compiler_params=_cp("parallel", "parallel"),
    )(coords, g, r1)


def _adam_math(w, g, m, v):
    m = ADAM_B1 * m + (1.0 - ADAM_B1) * g
    v = ADAM_B2 * v + (1.0 - ADAM_B2) * (g * g)
    m_hat = m / (1.0 - ADAM_B1 ** ADAM_STEP)
    v_hat = v / (1.0 - ADAM_B2 ** ADAM_STEP)
    delta = -ADAM_LR * (m_hat / (jnp.sqrt(v_hat) + ADAM_EPS) + ADAM_WD * w)
    return delta, m, v


def _adamw_sharded(coords, w, m, v, g, r1, r2, layer, prev, name):
    L, rows, C = w.shape
    tr = _row_tile(rows, C)
    n_prev = 0 if prev is None else len(prev)

    def body(co_ref, w_ref, m_ref, v_ref, g_ref, r1_ref, r2_ref, *rest):
        go_ref, d_ref, mo_ref, vo_ref = rest[n_prev:]
        grad = g_ref[...] + r1_ref[...]
        for k in range(3):
            grad = grad + r2_ref[k].astype(F32)
        go_ref[...] = grad
        d_ref[...], mo_ref[...], vo_ref[...] = _adam_math(w_ref[...], grad, m_ref[...], v_ref[...])

    spec = pl.BlockSpec((None, tr, C), lambda t, co: (layer, t, 0))
    return pl.pallas_call(
        body, name=name,
        grid_spec=pltpu.PrefetchScalarGridSpec(
            num_scalar_prefetch=1, grid=(rows // tr,),
            in_specs=[spec, spec, spec,
                      pl.BlockSpec((None, tr, C), lambda t, co: (4 * co[2] + 2 * co[0] + co[1], t, 0)),
                      pl.BlockSpec((None, tr, C), lambda t, co: (2 * co[0] + co[1], t, 0)),
                      pl.BlockSpec((3, tr, C), lambda t, co: (0, t, 0))] + [_HBM] * n_prev,
            out_specs=[spec] * 4),
        out_shape=[jax.ShapeDtypeStruct((L, rows, C), F32)] * 4,
        input_output_aliases={7 + k: k for k in range(n_prev)},
        compiler_params=_cp("parallel"),
    )(coords, w, m, v, g, r1, r2, *(prev or []))


def _adamw_small(w, g, m, v, name):
    def body(w_ref, g_ref, m_ref, v_ref, d_ref, mo_ref, vo_ref):
        d_ref[...], mo_ref[...], vo_ref[...] = _adam_math(w_ref[...], g_ref[...], m_ref[...], v_ref[...])

    return pl.pallas_call(
        body, name=name, out_shape=[jax.ShapeDtypeStruct(w.shape, F32)] * 3,
        in_specs=[pl.BlockSpec(memory_space=pltpu.VMEM)] * 4,
        out_specs=[pl.BlockSpec(memory_space=pltpu.VMEM)] * 3,
    )(w, g, m, v)


def _reduce_scatter_adds(coords, grads, r1s, tag):
    return [_chip_partials(coords, g, r, f"rs_add_{tag}{i}") for i, (g, r) in enumerate(zip(grads, r1s))]


def kernel(x, norm_gain, a_w_in, a_v_gain, a_w_s, a_b_s, a_w_out, b_w_in, b_q_gain, b_k_gain, b_w_out, c_w_in, c_w_grp, c_scale, c_w_out, loss_target, m_norm_gain, m_a_w_in, m_a_v_gain, m_a_w_s, m_a_b_s, m_a_w_out, m_b_w_in, m_b_q_gain, m_b_k_gain, m_b_w_out, m_c_w_in, m_c_w_grp, m_c_scale, m_c_w_out, v_norm_gain, v_a_w_in, v_a_v_gain, v_a_w_s, v_a_b_s, v_a_w_out, v_b_w_in, v_b_q_gain, v_b_k_gain, v_b_w_out, v_c_w_in, v_c_w_grp, v_c_scale, v_c_w_out):
    cx, cy, cc = _coords()
    coords = jnp.stack([cx, cy, cc]).astype(jnp.int32)
    dev = 4 * cx + 2 * cy + cc
    Dm = x.shape[2]

    xs, tgt = x[0], loss_target[0]
    tables = _rope_tables(xs.shape[0])
    ng = lambda i: norm_gain[i:i + 1]
    ngr = len(B_DILATIONS)
    bst = [a_b_s[l].T for l in range(2)]
    b_gains = jnp.concatenate([b_q_gain[0], b_k_gain[0], jnp.zeros((2, HEAD_DIM), F32)], axis=0)
    nla, nlb, nlc = a_w_in.shape[2], b_w_in.shape[2], c_w_in.shape[2]
    ngp, rlc, cgc = c_w_grp.shape[1:]
    wire = lambda w: w.astype(WIRE)

    nvg, nsc = a_v_gain.size, c_scale.size
    vec = jnp.concatenate([a_v_gain.reshape(-1), c_scale.reshape(-1), jnp.zeros((1024 - nvg - nsc,), F32)]).reshape(8, 128)
    wa_in0, wa_out0, vecs = _all_gather_hbm([wire(a_w_in[0]), wire(a_w_out[0]), vec], "ag_layer0")
    wa_out0 = wa_out0.reshape(-1, Dm)
    vecs = vecs.reshape(N_DEV, -1)
    a_vg = vecs[:, :nvg].reshape((N_DEV,) + a_v_gain.shape).transpose(1, 0, 2).reshape(a_v_gain.shape[0], -1)
    c_sc = vecs[:, nvg:nvg + nsc].reshape(1, -1)

    h0, p0, *g1 = _norm_proj(xs, ng(0), wa_in0, "l0_proj", comm=_ag_send([wire(b_w_in[0]), wire(b_w_out[0])]))
    y0, wb_in, wb_out = _a_mid(p0, a_vg[0:1], a_w_s[0], bst[0], "l0_mid", comm=_ag_forward(g1))
    x1 = _out_proj(xs, y0, wa_out0, "l0_out")
    wb_out = wb_out.reshape(-1, Dm)

    later = [wire(c_w_in[0]), wire(c_w_grp[0]), wire(c_w_out[0]), wire(a_w_in[1]), wire(a_w_out[1])]
    h1, p1, *g2 = _norm_proj(x1, ng(1), wb_in, "l1_proj", comm=_ag_send(later))
    qk, wc_in, wc_grp, wc_out, wa_in1, wa_out1 = _b_qk_fwd(p1, tables, b_gains, "l1_qk", comm=_ag_forward(g2))
    ogs, lgs = zip(*[_b_attn_fwd(qk, p1, g, f"l1_attn{g}") for g in range(ngr)])
    tiles = [_attn_tile(D, xs.shape[0]) for D in B_DILATIONS]
    lgs = [_units_to_tokens(l, D, T) for l, D, T in zip(lgs, B_DILATIONS, tiles)]
    y1, o1, lse = _b_combine(ogs, lgs, p1, "l1_comb")
    x2 = _out_proj(x1, y1, wb_out, "l1_out")
    wc_grp = wc_grp.transpose(1, 0, 2, 3).reshape(ngp, N_DEV * rlc, cgc)
    wc_out = wc_out.reshape(-1, Dm)
    wa_out1 = wa_out1.reshape(-1, Dm)

    h2, p2 = _norm_proj(x2, ng(2), wc_in, "l2_proj")
    y2 = _c_mid(p2, wc_grp, c_sc, "l2_mid")
    x3 = _out_proj(x2, y2, wc_out, "l2_out")
    h3, p3 = _norm_proj(x3, ng(3), wa_in1, "l3_proj")
    y3, = _a_mid(p3, a_vg[1:2], a_w_s[1], bst[1], "l3_mid")
    loss_local, dx4, dx4a = _out_proj_loss(x3, y3, wa_out1, tgt, "l3_out_loss")

    flat3 = lambda g: g.reshape(N_DEV, -1, g.shape[-1])
    dp3, dws1, dbs1, dvg1 = _a_bwd(dx4a, wa_out1, p3, a_vg[1:2], a_w_s[1], bst[1], "l3_bwd")
    grads3 = [_dw_in(h3, dp3, "l3_dwin"), _dw_out(y3, dx4a, "l3_dwout")]
    dx3, dx3a, dg3, *r1_3 = _dh_norm_bwd(dp3, wa_in1, x3, ng(3), dx4, "l3_dh", comm=_rs_sibling(grads3))
    parts3 = _reduce_scatter_adds(coords, grads3, r1_3, "l3_")

    dd, dz, gc_grp, dsc, *r2_3 = _c_bwd1(dx3a, wc_out, p2, wc_grp, c_sc, "l2_bwd1", comm=_rs_chips(parts3))
    dp2 = _c_bwd2(dd, dz, "l2_bwd2")
    grads2 = [_dw_in(h2, dp2, "l2_dwin"), _dw_out(y2, dx3a, "l2_dwout"), flat3(gc_grp)]
    dx2, dx2a, dg2, *r1_2 = _dh_norm_bwd(dp2, wc_in, x2, ng(2), dx3, "l2_dh", comm=_rs_sibling(grads2))
    parts2 = _reduce_scatter_adds(coords, grads2, r1_2, "l2_")

    dov, delta, dp1 = _b_bwd_pre(dx2a, wb_out, o1, p1, "l1_bwdpre")
    dqs, dks, r2_2 = [], [], None
    for g in range(ngr):
        lse_u, delta_u = [_tokens_to_units(a, B_DILATIONS[g], tiles[g]) for a in (lse, delta)]
        dq, dk, dp1, *rest = _b_attn_bwd(qk, p1, dov, lse_u, delta_u, dp1, g, f"l1_attnbwd{g}",
                                         comm=_rs_chips(parts2) if g == 0 else None)
        if g == 0:
            r2_2 = rest
        dqs.append(dq)
        dks.append(dk)
    dp1, dgains = _b_qk_bwd(dqs, dks, p1, tables, b_gains, dp1, "l1_qkbwd")
    grads1 = [_dw_in(h1, dp1, "l1_dwin"), _dw_out(y1, dx2a, "l1_dwout")]
    dx1, dx1a, dg1, *r1_1 = _dh_norm_bwd(dp1, wb_in, x1, ng(1), dx2, "l1_dh", comm=_rs_sibling(grads1))
    parts1 = _reduce_scatter_adds(coords, grads1, r1_1, "l1_")

    dp0, dws0, dbs0, dvg0, *r2_1 = _a_bwd(dx1a, wa_out0, p0, a_vg[0:1], a_w_s[0], bst[0], "l0_bwd", comm=_rs_chips(parts1))
    small = dict(norm=jnp.concatenate([dg1, dg2, dg3], axis=0), a_ws=jnp.stack([dws0, dws1]),
                 a_bs=jnp.stack([dbs0.T, dbs1.T]), b_gains=dgains, a_vg=jnp.concatenate([dvg0, dvg1], axis=0), c_sc=dsc)
    order = ["norm", "a_ws", "a_bs", "b_gains", "a_vg", "c_sc"]
    rows = [small[k].reshape(-1, 128) for k in order]
    roff = np.cumsum([0] + [r.shape[0] for r in rows])
    gw_in0, gsmall = _dw_in(h0, dp0, "l0_dwin", comm=_ag_send([jnp.concatenate(rows, axis=0)]))
    gw_out0, gsmall = _dw_out(y0, dx1a, "l0_dwout", comm=_ag_forward([gsmall]))
    grads0 = [gw_in0, gw_out0]
    r1_0 = _run_comm(_rs_sibling(grads0), "l0_rs_sibling")
    parts0 = _reduce_scatter_adds(coords, grads0, r1_0, "l0_")
    dx0, _, dg0, *r2_0 = _dh_norm_bwd(dp0, wa_in0, xs, ng(0), dx1, "l0_dh", comm=_rs_chips(parts0))

    tot = _sum_blocks(gsmall, "small_sum")
    sm = {k: tot[int(roff[i]):int(roff[i + 1])].reshape(small[k].shape) for i, k in enumerate(order)}
    late = _all_reduce_small(jnp.concatenate([dg0.reshape(-1, 128), jnp.full((8, 128), loss_local, F32)], axis=0))
    sm["norm"] = jnp.concatenate([late[0:8].reshape(1, -1), sm["norm"]], axis=0)
    loss = late[8, 0]
    vl = a_v_gain.shape[1]
    g_small = dict(
        norm_gain=sm["norm"], a_w_s=sm["a_ws"], a_b_s=sm["a_bs"],
        b_q_gain=sm["b_gains"][None, 0:3], b_k_gain=sm["b_gains"][None, 3:6],
        a_v_gain=lax.dynamic_slice_in_dim(sm["a_vg"], dev * vl, vl, axis=1),
        c_scale=lax.dynamic_slice_in_dim(sm["c_sc"], dev * vl, vl, axis=1),
    )

    shares = dict(
        a_w_in=[(grads0[0], r1_0[0], r2_0[0]), (grads3[0], r1_3[0], r2_3[0])],
        a_w_out=[(grads0[1], r1_0[1], r2_0[1]), (grads3[1], r1_3[1], r2_3[1])],
        b_w_in=[(grads1[0], r1_1[0], r2_1[0])], b_w_out=[(grads1[1], r1_1[1], r2_1[1])],
        c_w_in=[(grads2[0], r1_2[0], r2_2[0])], c_w_out=[(grads2[1], r1_2[1], r2_2[1])],
        c_w_grp=[(grads2[2], r1_2[2], r2_2[2])])

    params = dict(a_w_in=a_w_in, a_w_out=a_w_out, b_w_in=b_w_in, b_w_out=b_w_out, c_w_in=c_w_in, c_w_grp=c_w_grp, c_w_out=c_w_out,
                  norm_gain=norm_gain, a_v_gain=a_v_gain, a_w_s=a_w_s, a_b_s=a_b_s, b_q_gain=b_q_gain, b_k_gain=b_k_gain, c_scale=c_scale)
    moms = dict(a_w_in=(m_a_w_in, v_a_w_in), a_w_out=(m_a_w_out, v_a_w_out), b_w_in=(m_b_w_in, v_b_w_in), b_w_out=(m_b_w_out, v_b_w_out),
                c_w_in=(m_c_w_in, v_c_w_in), c_w_grp=(m_c_w_grp, v_c_w_grp), c_w_out=(m_c_w_out, v_c_w_out),
                norm_gain=(m_norm_gain, v_norm_gain), a_v_gain=(m_a_v_gain, v_a_v_gain), a_w_s=(m_a_w_s, v_a_w_s),
                a_b_s=(m_a_b_s, v_a_b_s), b_q_gain=(m_b_q_gain, v_b_q_gain), b_k_gain=(m_b_k_gain, v_b_k_gain),
                c_scale=(m_c_scale, v_c_scale))
    grad, delta, new_m, new_v = {}, {}, {}, {}
    for pname, layers in shares.items():
        w, (m, v) = params[pname], moms[pname]
        as3 = lambda a: a.reshape(a.shape[0], -1, a.shape[-1])
        outs = None
        for l, (g, r1, r2) in enumerate(layers):
            outs = _adamw_sharded(coords, as3(w), as3(m), as3(v), g, r1, r2, l, outs, f"adamw_{pname}{l}")
        grad[pname], delta[pname], new_m[pname], new_v[pname] = [o.reshape(w.shape) for o in outs]
    for pname, g in g_small.items():
        w = params[pname]
        C = w.shape[-1]
        outs = _adamw_small(w.reshape(-1, C), g.reshape(-1, C), moms[pname][0].reshape(-1, C), moms[pname][1].reshape(-1, C),
                            f"adamw_{pname}")
        grad[pname] = g.reshape(w.shape)
        delta[pname], new_m[pname], new_v[pname] = [o.reshape(w.shape) for o in outs]

    wnames = ["norm_gain", "a_w_in", "a_v_gain", "a_w_s", "a_b_s", "a_w_out", "b_w_in", "b_q_gain", "b_k_gain", "b_w_out",
              "c_w_in", "c_w_grp", "c_scale", "c_w_out"]
    return (loss, dx0[None], *[grad[n] for n in wnames], *[delta[n] for n in wnames],
            *[new_m[n] for n in wnames], *[new_v[n] for n in wnames])
```

```python
import functools

import numpy as np
import jax
import jax.numpy as jnp
from jax import lax
from jax.experimental import pallas as pl
from jax.experimental.pallas import tpu as pltpu

F32 = jnp.float32
MXU = jnp.bfloat16
ACT = jnp.bfloat16
WIRE = jnp.bfloat16

EPS = 1e-6
CHUNK = 128
A_GROUPS = 8
HEAD_DIM = 128
B_HEADS = 8
B_DILATIONS = (1, 4, 16)
ROPE_DIM = 32
ROPE_THETA = 500000.0
POOL_SIZES = (2, 4, 8, 16)
POOL_HALO = 16
N_DEV = 8
NEG = -1e30

ADAM_LR, ADAM_B1, ADAM_B2, ADAM_EPS, ADAM_WD, ADAM_STEP = 0.001, 0.9, 0.999, 1e-08, 0.01, 10

VMEM_LIMIT = 62 * 1024 * 1024
MESH = pl.DeviceIdType.MESH


def _cp(*sem):
    return pltpu.CompilerParams(dimension_semantics=sem, vmem_limit_bytes=VMEM_LIMIT)


def _sigmoid(z):
    return 1.0 / (1.0 + jnp.exp(-z))


def _dot(a, b):
    return jnp.dot(a.astype(MXU), b.astype(MXU), preferred_element_type=F32)


def _dot_nt(a, b):
    return lax.dot_general(a.astype(MXU), b.astype(MXU), (((1,), (1,)), ((), ())), preferred_element_type=F32)


def _dot_tn(a, b):
    return lax.dot_general(a.astype(MXU), b.astype(MXU), (((0,), (0,)), ((), ())), preferred_element_type=F32)


def _chunk_slot(d):
    return (d % 2) * 4 + d // 2


class _Comm:
    def __init__(self, inputs, out_shapes, n_remote, make, aliases=None, n_local=1):
        self.inputs = list(inputs)
        self.out_shapes = list(out_shapes)
        self.n_remote = n_remote
        self.n_local = n_local
        self.make = make
        self.aliases = dict(aliases or {})

    def sems(self):
        return [pltpu.SemaphoreType.DMA((self.n_remote,)), pltpu.SemaphoreType.DMA((self.n_remote,)),
                pltpu.SemaphoreType.DMA((self.n_local,))]


_HBM = pl.BlockSpec(memory_space=pl.ANY)


def _launch(body, *, name, grid, in_specs, out_specs, out_shape, args, sem, scratch=(), aliases=None, comm=None):
    in_specs, out_specs, out_shape, scratch = list(in_specs), list(out_specs), list(out_shape), list(scratch)
    aliases = dict(aliases or {})
    if comm is None:
        return pl.pallas_call(body, name=name, grid=grid, in_specs=in_specs, out_specs=out_specs, out_shape=out_shape,
                              scratch_shapes=scratch, input_output_aliases=aliases, compiler_params=_cp(*sem))(*args)
    n_in, n_out, n_sc = len(in_specs), len(out_specs), len(scratch)
    nci, nco = len(comm.inputs), len(comm.out_shapes)

    def hosted(*refs):
        b_in, c_in = refs[:n_in], refs[n_in:n_in + nci]
        o0 = n_in + nci
        b_out, c_out = refs[o0:o0 + n_out], refs[o0 + n_out:o0 + n_out + nco]
        s0 = o0 + n_out + nco
        b_sc, sems = refs[s0:s0 + n_sc], refs[s0 + n_sc:]
        ids = [pl.program_id(a) for a in range(len(grid))]
        first = functools.reduce(jnp.logical_and, [i == 0 for i in ids])
        last = functools.reduce(jnp.logical_and, [i == g - 1 for i, g in zip(ids, grid)])

        @pl.when(first)
        def _():
            for cp in comm.make(c_in, c_out, *sems):
                cp.start()

        body(*b_in, *b_out, *b_sc)

        @pl.when(last)
        def _():
            for cp in comm.make(c_in, c_out, *sems):
                cp.wait()

    for ci, co in comm.aliases.items():
        aliases[n_in + ci] = n_out + co
    return pl.pallas_call(
        hosted, name=name, grid=grid, in_specs=in_specs + [_HBM] * nci, out_specs=out_specs + [_HBM] * nco,
        out_shape=out_shape + comm.out_shapes, scratch_shapes=scratch + comm.sems(),
        input_output_aliases=aliases, compiler_params=_cp(*["arbitrary"] * len(grid)))(*args, *comm.inputs)


def _run_comm(comm, name):
    nci, nco = len(comm.inputs), len(comm.out_shapes)

    def body(*refs):
        cps = comm.make(refs[:nci], refs[nci:nci + nco], *refs[nci + nco:])
        for cp in cps:
            cp.start()
        for cp in cps:
            cp.wait()

    return pl.pallas_call(
        body, name=name, in_specs=[_HBM] * nci, out_specs=[_HBM] * nco, out_shape=comm.out_shapes,
        scratch_shapes=comm.sems(), input_output_aliases=dict(comm.aliases))(*comm.inputs)


def _norm_proj(x, gain, w_dm, name, comm=None):
    M, Dm = x.shape
    nd, _, nl = w_dm.shape
    tm = min(M, 2048)

    def body(x_ref, g_ref, w_ref, h_ref, p_ref):
        @pl.when(pl.program_id(1) == 0)
        def _():
            xv = x_ref[...]
            r = lax.rsqrt(jnp.mean(xv * xv, axis=-1, keepdims=True) + EPS)
            h_ref[...] = (xv * r * g_ref[...]).astype(h_ref.dtype)

        p_ref[...] = _dot(h_ref[...], w_ref[...]).astype(p_ref.dtype)

    return _launch(
        body, name=name, grid=(M // tm, nd),
        in_specs=[pl.BlockSpec((tm, Dm), lambda i, j: (i, 0)),
                  pl.BlockSpec((1, Dm), lambda i, j: (0, 0)),
                  pl.BlockSpec((None, Dm, nl), lambda i, j: (j, 0, 0))],
        out_specs=[pl.BlockSpec((tm, Dm), lambda i, j: (i, 0)),
                   pl.BlockSpec((tm, nl), lambda i, j: (i, j))],
        out_shape=[jax.ShapeDtypeStruct((M, Dm), ACT), jax.ShapeDtypeStruct((M, nd * nl), ACT)],
        args=(x, gain, w_dm), sem=("parallel", "arbitrary"), comm=comm)


def _out_proj(x, y, w, name):
    M, Dm = x.shape
    K = y.shape[1]
    tm = min(M, 1024)

    def body(x_ref, y_ref, w_ref, o_ref):
        o_ref[...] = x_ref[...] + _dot(y_ref[...], w_ref[...])

    return pl.pallas_call(
        body, name=name, grid=(M // tm,),
        in_specs=[pl.BlockSpec((tm, Dm), lambda i: (i, 0)),
                  pl.BlockSpec((tm, K), lambda i: (i, 0)),
                  pl.BlockSpec((K, Dm), lambda i: (0, 0))],
        out_specs=pl.BlockSpec((tm, Dm), lambda i: (i, 0)),
        out_shape=jax.ShapeDtypeStruct((M, Dm), F32),
        compiler_params=_cp("parallel"),
    )(x, y, w)


def _out_proj_loss(x, y, w, target, name):
    M, Dm = x.shape
    K = y.shape[1]
    tm = min(M, 512)

    def body(x_ref, y_ref, w_ref, t_ref, dx_ref, dxa_ref, l_ref):
        @pl.when(pl.program_id(0) == 0)
        def _():
            l_ref[...] = jnp.zeros_like(l_ref)

        err = x_ref[...] + _dot(y_ref[...], w_ref[...]) - t_ref[...]
        dx = err * (1.0 / Dm)
        dx_ref[...] = dx
        dxa_ref[...] = dx.astype(dxa_ref.dtype)
        l_ref[...] += jnp.sum(err * err) * (0.5 / Dm)

    spec = pl.BlockSpec((tm, Dm), lambda i: (i, 0))
    dx, dxa, l = pl.pallas_call(
        body, name=name, grid=(M // tm,),
        in_specs=[spec, pl.BlockSpec((tm, K), lambda i: (i, 0)), pl.BlockSpec((K, Dm), lambda i: (0, 0)), spec],
        out_specs=[spec, spec, pl.BlockSpec((8, 128), lambda i: (0, 0))],
        out_shape=[jax.ShapeDtypeStruct((M, Dm), F32), jax.ShapeDtypeStruct((M, Dm), ACT),
                   jax.ShapeDtypeStruct((8, 128), F32)],
        compiler_params=_cp("arbitrary"),
    )(x, y, w, target)
    return l[0, 0], dx, dxa


def _dw_in(h, dproj, name, comm=None):
    M, Dm = h.shape
    nl = dproj.shape[1] // N_DEV
    tt = min(M, 2048)

    def body(a_ref, b_ref, o_ref):
        @pl.when(pl.program_id(1) == 0)
        def _():
            o_ref[...] = jnp.zeros_like(o_ref)

        o_ref[...] += _dot_tn(a_ref[...], b_ref[...])

    outs = _launch(
        body, name=name, grid=(N_DEV, M // tt),
        in_specs=[pl.BlockSpec((tt, Dm), lambda j, t: (t, 0)), pl.BlockSpec((tt, nl), lambda j, t: (t, j))],
        out_specs=[pl.BlockSpec((None, Dm, nl), lambda j, t: (_chunk_slot(j), 0, 0))],
        out_shape=[jax.ShapeDtypeStruct((N_DEV, Dm, nl), F32)],
        args=(h, dproj), sem=("parallel", "arbitrary"), comm=comm)
    return outs[0] if comm is None else outs


def _dw_out(y, dout, name, comm=None):
    M, K = y.shape
    Dm = dout.shape[1]
    kl = K // N_DEV
    tt = min(M, 512)

    def body(a_ref, b_ref, o_ref):
        @pl.when(pl.program_id(0) == 0)
        def _():
            o_ref[...] = jnp.zeros_like(o_ref)

        b = b_ref[...]
        for j in range(N_DEV):
            o_ref[_chunk_slot(j)] += _dot_tn(a_ref[:, j * kl:(j + 1) * kl], b)

    outs = _launch(
        body, name=name, grid=(M // tt,),
        in_specs=[pl.BlockSpec((tt, K), lambda t: (t, 0)), pl.BlockSpec((tt, Dm), lambda t: (t, 0))],
        out_specs=[pl.BlockSpec((N_DEV, kl, Dm), lambda t: (0, 0, 0))],
        out_shape=[jax.ShapeDtypeStruct((N_DEV, kl, Dm), F32)],
        args=(y, dout), sem=("arbitrary",), comm=comm)
    return outs[0] if comm is None else outs


def _dh_norm_bwd(dproj, w_dm, x, gain, dres, name, comm=None):
    M, Dm = x.shape
    nd, _, nl = w_dm.shape
    tm = min(M, 1024)
    rows_bytes = tm * Dm * (4 + 2 * 4 + 2 * 4 + 2 * 4 + 2 * 2)
    block_bytes = 2 * (tm * nl + Dm * nl) * 2
    pair = 2 if rows_bytes + 2 * block_bytes <= VMEM_LIMIT - 8 * 1024 * 1024 else 1
    nj = nd // pair

    def body(dp_ref, w_ref, x_ref, g_ref, dr_ref, dx_ref, dxa_ref, dg_ref, acc_ref):
        i, j = pl.program_id(0), pl.program_id(1)

        @pl.when(j == 0)
        def _():
            acc_ref[...] = jnp.zeros_like(acc_ref)

        acc_ref[...] += functools.reduce(
            lambda a, b: a + b, [_dot_nt(dp_ref[:, d * nl:(d + 1) * nl], w_ref[d]) for d in range(pair)])

        @pl.when(j == nj - 1)
        def _():
            @pl.when(i == 0)
            def _():
                dg_ref[...] = jnp.zeros_like(dg_ref)

            dh = acc_ref[...]
            xv = x_ref[...]
            r = lax.rsqrt(jnp.mean(xv * xv, axis=-1, keepdims=True) + EPS)
            xn = xv * r
            dg_ref[...] += jnp.sum(dh * xn, axis=0, keepdims=True)
            dxn = dh * g_ref[...]
            dx = dr_ref[...] + r * (dxn - xn * jnp.mean(dxn * xn, axis=-1, keepdims=True))
            dx_ref[...] = dx
            dxa_ref[...] = dx.astype(dxa_ref.dtype)

    row = pl.BlockSpec((tm, Dm), lambda i, j: (i, 0))
    return _launch(
        body, name=name, grid=(M // tm, nj),
        in_specs=[pl.BlockSpec((tm, pair * nl), lambda i, j: (i, j)),
                  pl.BlockSpec((pair, Dm, nl), lambda i, j: (j, 0, 0)),
                  row, pl.BlockSpec((1, Dm), lambda i, j: (0, 0)), row],
        out_specs=[row, row, pl.BlockSpec((1, Dm), lambda i, j: (0, 0))],
        out_shape=[jax.ShapeDtypeStruct((M, Dm), F32), jax.ShapeDtypeStruct((M, Dm), ACT),
                   jax.ShapeDtypeStruct((1, Dm), F32)],
        scratch=[pltpu.VMEM((tm, Dm), F32)],
        args=(dproj, w_dm, x, gain, dres), sem=("arbitrary", "arbitrary"), comm=comm)


def _tril_mask():
    return lax.broadcasted_iota(jnp.int32, (CHUNK, CHUNK), 0) >= lax.broadcasted_iota(jnp.int32, (CHUNK, CHUNK), 1)


def _a_mid(proj, v_gain, w_s, b_st, name, comm=None):
    M = proj.shape[0]
    W = proj.shape[1] // 3
    gd = W // A_GROUPS
    tm = min(M, 256)

    def body(p_ref, vg_ref, ws_ref, bs_ref, y_ref):
        pv = p_ref[:, W:2 * W].astype(F32)
        r = lax.rsqrt(jnp.mean(pv * pv, axis=-1, keepdims=True) + EPS)
        v = (pv * r * vg_ref[...]).astype(MXU)
        tri = _tril_mask()
        for g in range(A_GROUPS):
            wg = jnp.where(tri, ws_ref[g], 0.0).astype(MXU)
            bcol = bs_ref[:, g:g + 1]
            for c in range(tm // CHUNK):
                rows, cols = slice(c * CHUNK, (c + 1) * CHUNK), slice(g * gd, (g + 1) * gd)
                mixed = jnp.dot(wg, v[rows, cols], preferred_element_type=F32) + bcol
                u = p_ref[rows, g * gd:(g + 1) * gd].astype(F32)
                z = p_ref[rows, 2 * W + g * gd:2 * W + (g + 1) * gd].astype(F32)
                y_ref[rows, cols] = (u * mixed * (z * _sigmoid(z))).astype(y_ref.dtype)

    return _launch(
        body, name=name, grid=(M // tm,),
        in_specs=[pl.BlockSpec((tm, 3 * W), lambda i: (i, 0)),
                  pl.BlockSpec((1, W), lambda i: (0, 0)),
                  pl.BlockSpec((A_GROUPS, CHUNK, CHUNK), lambda i: (0, 0, 0)),
                  pl.BlockSpec((CHUNK, A_GROUPS), lambda i: (0, 0))],
        out_specs=[pl.BlockSpec((tm, W), lambda i: (i, 0))],
        out_shape=[jax.ShapeDtypeStruct((M, W), ACT)],
        args=(proj, v_gain, w_s, b_st), sem=("parallel",), comm=comm)


def _a_bwd(dout, w_out, proj, v_gain, w_s, b_st, name, comm=None):
    M = proj.shape[0]
    W = proj.shape[1] // 3
    Dm = dout.shape[1]
    gd = W // A_GROUPS
    tm = min(M, 512)
    nt = M // tm

    def body(do_ref, wo_ref, p_ref, vg_ref, ws_ref, bs_ref, dp_ref, dws_ref, dbs_ref, dvg_ref, dv_s):
        i = pl.program_id(0)

        @pl.when(i == 0)
        def _():
            dws_ref[...] = jnp.zeros_like(dws_ref)
            dbs_ref[...] = jnp.zeros_like(dbs_ref)
            dvg_ref[...] = jnp.zeros_like(dvg_ref)

        dy = _dot_nt(do_ref[...], wo_ref[...])
        pv = p_ref[:, W:2 * W].astype(F32)
        r = lax.rsqrt(jnp.mean(pv * pv, axis=-1, keepdims=True) + EPS)
        pvn = pv * r
        vg = vg_ref[...]
        v = (pvn * vg).astype(MXU)
        tri = _tril_mask()
        for g in range(A_GROUPS):
            wf = jnp.where(tri, ws_ref[g], 0.0)
            wg = wf.astype(MXU)
            wgt = wf.T.astype(MXU)
            bcol = bs_ref[:, g:g + 1]
            for c in range(tm // CHUNK):
                rows, cols = slice(c * CHUNK, (c + 1) * CHUNK), slice(g * gd, (g + 1) * gd)
                vb = v[rows, cols]
                mixed = jnp.dot(wg, vb, preferred_element_type=F32) + bcol
                u = p_ref[rows, g * gd:(g + 1) * gd].astype(F32)
                z = p_ref[rows, 2 * W + g * gd:2 * W + (g + 1) * gd].astype(F32)
                sig = _sigmoid(z)
                sz = z * sig
                dyb = dy[rows, cols]
                dp_ref[rows, g * gd:(g + 1) * gd] = (dyb * mixed * sz).astype(dp_ref.dtype)
                dp_ref[rows, 2 * W + g * gd:2 * W + (g + 1) * gd] = (
                    dyb * u * mixed * (sig * (1.0 + z * (1.0 - sig)))).astype(dp_ref.dtype)
                dmix = dyb * u * sz
                dws_ref[g] += _dot_nt(dmix, vb)
                dbs_ref[:, g:g + 1] += jnp.sum(dmix, axis=1, keepdims=True)
                dv_s[rows, cols] = jnp.dot(wgt, dmix.astype(MXU), preferred_element_type=F32)
        dv = dv_s[...]
        dvg_ref[...] += jnp.sum(dv * pvn, axis=0, keepdims=True)
        dpvn = dv * vg
        dp_ref[:, W:2 * W] = (r * (dpvn - pvn * jnp.mean(dpvn * pvn, axis=-1, keepdims=True))).astype(dp_ref.dtype)

        @pl.when(i == nt - 1)
        def _():
            for g in range(A_GROUPS):
                dws_ref[g] = jnp.where(tri, dws_ref[g], 0.0)

    return _launch(
        body, name=name, grid=(nt,),
        in_specs=[pl.BlockSpec((tm, Dm), lambda i: (i, 0)),
                  pl.BlockSpec((W, Dm), lambda i: (0, 0)),
                  pl.BlockSpec((tm, 3 * W), lambda i: (i, 0)),
                  pl.BlockSpec((1, W), lambda i: (0, 0)),
                  pl.BlockSpec((A_GROUPS, CHUNK, CHUNK), lambda i: (0, 0, 0)),
                  pl.BlockSpec((CHUNK, A_GROUPS), lambda i: (0, 0))],
        out_specs=[pl.BlockSpec((tm, 3 * W), lambda i: (i, 0)),
                   pl.BlockSpec((A_GROUPS, CHUNK, CHUNK), lambda i: (0, 0, 0)),
                   pl.BlockSpec((CHUNK, A_GROUPS), lambda i: (0, 0)),
                   pl.BlockSpec((1, W), lambda i: (0, 0))],
        out_shape=[jax.ShapeDtypeStruct((M, 3 * W), ACT),
                   jax.ShapeDtypeStruct((A_GROUPS, CHUNK, CHUNK), F32),
                   jax.ShapeDtypeStruct((CHUNK, A_GROUPS), F32),
                   jax.ShapeDtypeStruct((1, W), F32)],
        scratch=[pltpu.VMEM((tm, W), F32)],
        args=(dout, w_out, proj, v_gain, w_s, b_st), sem=("arbitrary",), comm=comm)


def _pool_diff(xg, tail, i, tm, w):
    t = lax.broadcasted_iota(jnp.int32, (tm, tm + POOL_HALO), 0)
    s = lax.broadcasted_iota(jnp.int32, (tm, tm + POOL_HALO), 1)
    off = t - (s - POOL_HALO)
    band = jnp.where((off >= 0) & (off < w), 1.0, 0.0).astype(MXU)
    tail = jnp.where(i > 0, tail, jnp.zeros_like(tail))
    ext = jnp.concatenate([tail, xg], axis=0)
    ssum = jnp.dot(band, ext.astype(MXU), preferred_element_type=F32)
    tglob = i * tm + lax.broadcasted_iota(jnp.int32, (tm, 1), 0)
    cnt = jnp.minimum(tglob + 1, w).astype(F32)
    return ssum / cnt - xg.astype(F32)


def _c_mid(proj, w_grp, scale, name):
    M = proj.shape[0]
    W = proj.shape[1] // 2
    ng = len(POOL_SIZES)
    cg = W // ng
    tm = min(M, 256)
    hb = tm // POOL_HALO

    def body(xc_ref, tail_ref, z_ref, wg_ref, sc_ref, y_ref):
        i = pl.program_id(0)
        for g, w in enumerate(POOL_SIZES):
            cols = slice(g * cg, (g + 1) * cg)
            d = _pool_diff(xc_ref[:, cols], tail_ref[:, cols], i, tm, w)
            mixed = _dot(d, wg_ref[g]) * sc_ref[:, cols]
            z = z_ref[:, cols].astype(F32)
            y_ref[:, cols] = (mixed * (z * _sigmoid(z))).astype(y_ref.dtype)

    return pl.pallas_call(
        body, name=name, grid=(M // tm,),
        in_specs=[pl.BlockSpec((tm, W), lambda i: (i, 0)),
                  pl.BlockSpec((POOL_HALO, W), lambda i: (jnp.maximum(i * hb - 1, 0), 0)),
                  pl.BlockSpec((tm, W), lambda i: (i, 1)),
                  pl.BlockSpec((ng, cg, cg), lambda i: (0, 0, 0)),
                  pl.BlockSpec((1, W), lambda i: (0, 0))],
        out_specs=pl.BlockSpec((tm, W), lambda i: (i, 0)),
        out_shape=jax.ShapeDtypeStruct((M, W), ACT),
        compiler_params=_cp("parallel"),
    )(proj, proj, proj, w_grp, scale)


def _c_bwd1(dout, w_out, proj, w_grp, scale, name, comm=None):
    M = proj.shape[0]
    W = proj.shape[1] // 2
    Dm = dout.shape[1]
    ng = len(POOL_SIZES)
    cg = W // ng
    rl = cg // N_DEV
    tm = min(M, 256)
    hb = tm // POOL_HALO
    nt = M // tm

    def body(do_ref, wo_ref, xc_ref, tail_ref, z_ref, wg_ref, sc_ref, dd_ref, dz_ref, dwg_ref, dsc_ref, acc_ref):
        i = pl.program_id(0)

        @pl.when(i == 0)
        def _():
            acc_ref[...] = jnp.zeros_like(acc_ref)
            dsc_ref[...] = jnp.zeros_like(dsc_ref)

        dy = _dot_nt(do_ref[...], wo_ref[...])
        for g, w in enumerate(POOL_SIZES):
            cols = slice(g * cg, (g + 1) * cg)
            d = _pool_diff(xc_ref[:, cols], tail_ref[:, cols], i, tm, w)
            mr = _dot(d, wg_ref[g])
            sc = sc_ref[:, cols]
            z = z_ref[:, cols].astype(F32)
            sig = _sigmoid(z)
            dyg = dy[:, cols]
            dmixed = dyg * (z * sig)
            dz_ref[:, cols] = (dyg * (mr * sc) * (sig * (1.0 + z * (1.0 - sig)))).astype(dz_ref.dtype)
            dsc_ref[:, cols] += jnp.sum(dmixed * mr, axis=0, keepdims=True)
            dmr = (dmixed * sc).astype(MXU)
            acc_ref[g] += _dot_tn(d, dmr)
            dd_ref[:, cols] = _dot_nt(dmr, wg_ref[g]).astype(dd_ref.dtype)

        @pl.when(i == nt - 1)
        def _():
            for dev in range(N_DEV):
                for g in range(ng):
                    dwg_ref[_chunk_slot(dev), g] = acc_ref[g, dev * rl:(dev + 1) * rl, :]

    return _launch(
        body, name=name, grid=(nt,),
        in_specs=[pl.BlockSpec((tm, Dm), lambda i: (i, 0)),
                  pl.BlockSpec((W, Dm), lambda i: (0, 0)),
                  pl.BlockSpec((tm, W), lambda i: (i, 0)),
                  pl.BlockSpec((POOL_HALO, W), lambda i: (jnp.maximum(i * hb - 1, 0), 0)),
                  pl.BlockSpec((tm, W), lambda i: (i, 1)),
                  pl.BlockSpec((ng, cg, cg), lambda i: (0, 0, 0)),
                  pl.BlockSpec((1, W), lambda i: (0, 0))],
        out_specs=[pl.BlockSpec((tm, W), lambda i: (i, 0)),
                   pl.BlockSpec((tm, W), lambda i: (i, 0)),
                   pl.BlockSpec((N_DEV, ng, rl, cg), lambda i: (0, 0, 0, 0)),
                   pl.BlockSpec((1, W), lambda i: (0, 0))],
        out_shape=[jax.ShapeDtypeStruct((M, W), ACT), jax.ShapeDtypeStruct((M, W), ACT),
                   jax.ShapeDtypeStruct((N_DEV, ng, rl, cg), F32), jax.ShapeDtypeStruct((1, W), F32)],
        scratch=[pltpu.VMEM((ng, cg, cg), F32)],
        args=(dout, w_out, proj, proj, proj, w_grp, scale), sem=("arbitrary",), comm=comm)


def _c_bwd2(dd, dz, name):
    M, W = dd.shape
    ng = len(POOL_SIZES)
    cg = W // ng
    tm = min(M, 256)
    hb = tm // POOL_HALO
    nt = M // tm

    def body(dd_ref, head_ref, dz_ref, dp_ref):
        i = pl.program_id(0)
        s = lax.broadcasted_iota(jnp.int32, (tm, tm + POOL_HALO), 0)
        t = lax.broadcasted_iota(jnp.int32, (tm, tm + POOL_HALO), 1)
        off = t - s
        tglob = i * tm + lax.broadcasted_iota(jnp.int32, (tm + POOL_HALO, 1), 0)
        for g, w in enumerate(POOL_SIZES):
            cols = slice(g * cg, (g + 1) * cg)
            ddg = dd_ref[:, cols].astype(F32)
            head = head_ref[:, cols].astype(F32)
            head = jnp.where(i < nt - 1, head, jnp.zeros_like(head))
            cnt = jnp.minimum(tglob + 1, w).astype(F32)
            ext = (jnp.concatenate([ddg, head], axis=0) / cnt).astype(MXU)
            band = jnp.where((off >= 0) & (off < w), 1.0, 0.0).astype(MXU)
            dp_ref[:, cols] = (jnp.dot(band, ext, preferred_element_type=F32) - ddg).astype(dp_ref.dtype)
        dp_ref[:, W:] = dz_ref[...]

    return pl.pallas_call(
        body, name=name, grid=(nt,),
        in_specs=[pl.BlockSpec((tm, W), lambda i: (i, 0)),
                  pl.BlockSpec((POOL_HALO, W), lambda i: (jnp.minimum((i + 1) * hb, M // POOL_HALO - 1), 0)),
                  pl.BlockSpec((tm, W), lambda i: (i, 0))],
        out_specs=pl.BlockSpec((tm, 2 * W), lambda i: (i, 0)),
        out_shape=jax.ShapeDtypeStruct((M, 2 * W), ACT),
        compiler_params=_cp("parallel"),
    )(dd, dd, dz)


def _rope_tables(S):
    half = ROPE_DIM // 2
    inv_freq = jnp.power(jnp.float32(ROPE_THETA), -jnp.arange(half, dtype=F32) / half)
    ang = jnp.arange(S, dtype=F32)[:, None] * inv_freq[None, :]
    cos, sin = jnp.cos(ang), jnp.sin(ang)
    rest = HEAD_DIM - ROPE_DIM
    cf = jnp.concatenate([cos, cos, jnp.ones((S, rest), F32)], axis=1)
    sf = jnp.concatenate([-sin, sin, jnp.zeros((S, rest), F32)], axis=1)
    return cf, sf


def _swap_matrix():
    half = ROPE_DIM // 2
    a = lax.broadcasted_iota(jnp.int32, (HEAD_DIM, HEAD_DIM), 0)
    e = lax.broadcasted_iota(jnp.int32, (HEAD_DIM, HEAD_DIM), 1)
    hit = ((e < half) & (a == e + half)) | ((e >= half) & (e < 2 * half) & (a == e - half))
    return jnp.where(hit, 1.0, 0.0).astype(MXU)


def _b_qk_fwd(proj, tables, gains, name, comm=None):
    M = proj.shape[0]
    nsl = 2 * len(B_DILATIONS) * B_HEADS
    Wqk = nsl * HEAD_DIM
    tm = min(M, 256)

    def body(p_ref, cf_ref, sf_ref, g_ref, o_ref):
        cf, sf = cf_ref[...], sf_ref[...]
        swap = _swap_matrix()
        for j in range(nsl):
            cols = slice(j * HEAD_DIM, (j + 1) * HEAD_DIM)
            xv = p_ref[:, cols].astype(F32)
            r = lax.rsqrt(jnp.mean(xv * xv, axis=-1, keepdims=True) + EPS)
            xg = xv * g_ref[j // B_HEADS:j // B_HEADS + 1, :]
            hi = xg.astype(MXU)
            lo = (xg - hi.astype(F32)).astype(MXU)
            sw = jnp.dot(hi, swap, preferred_element_type=F32) + jnp.dot(lo, swap, preferred_element_type=F32)
            o_ref[:, cols] = (r * (xg * cf + sw * sf)).astype(o_ref.dtype)

    tspec = pl.BlockSpec((tm, HEAD_DIM), lambda i: (i, 0))
    return _launch(
        body, name=name, grid=(M // tm,),
        in_specs=[pl.BlockSpec((tm, Wqk), lambda i: (i, 0)), tspec, tspec,
                  pl.BlockSpec((8, HEAD_DIM), lambda i: (0, 0))],
        out_specs=[pl.BlockSpec((tm, Wqk), lambda i: (i, 0))],
        out_shape=[jax.ShapeDtypeStruct((M, Wqk), ACT)],
        args=(proj, *tables, gains), sem=("parallel",), comm=comm)


def _b_qk_bwd(dqs, dks, proj, tables, gains, dproj, name):
    M = proj.shape[0]
    ngr = len(B_DILATIONS)
    nsl = 2 * ngr * B_HEADS
    Wqk = nsl * HEAD_DIM
    Wg = B_HEADS * HEAD_DIM
    tm = min(M, 512)

    def body(*refs):
        d_refs = refs[:2 * ngr]
        p_ref, cf_ref, sf_ref, g_ref = refs[2 * ngr:2 * ngr + 4]
        dp_ref, dg_ref = refs[-2], refs[-1]

        @pl.when(pl.program_id(0) == 0)
        def _():
            dg_ref[...] = jnp.zeros_like(dg_ref)

        cf, sf = cf_ref[...], sf_ref[...]
        swap = _swap_matrix()
        for j in range(nsl):
            t, hh = j // B_HEADS, j % B_HEADS
            cols = slice(j * HEAD_DIM, (j + 1) * HEAD_DIM)
            dy = d_refs[t][:, hh * HEAD_DIM:(hh + 1) * HEAD_DIM].astype(F32)
            dxn = dy * cf + jnp.dot((dy * sf).astype(MXU), swap, preferred_element_type=F32)
            xv = p_ref[:, cols].astype(F32)
            r = lax.rsqrt(jnp.mean(xv * xv, axis=-1, keepdims=True) + EPS)
            xh = xv * r
            dg_ref[t:t + 1, :] += jnp.sum(dxn * xh, axis=0, keepdims=True)
            dxh = dxn * g_ref[t:t + 1, :]
            dp_ref[:, cols] = (r * (dxh - xh * jnp.mean(dxh * xh, axis=-1, keepdims=True))).astype(dp_ref.dtype)

    tspec = pl.BlockSpec((tm, HEAD_DIM), lambda i: (i, 0))
    dspec = pl.BlockSpec((tm, Wg), lambda i: (i, 0))
    n_in = 2 * ngr + 5
    return pl.pallas_call(
        body, name=name, grid=(M // tm,),
        in_specs=[dspec] * (2 * ngr) + [pl.BlockSpec((tm, Wqk), lambda i: (i, 0)), tspec, tspec,
                                        pl.BlockSpec((8, HEAD_DIM), lambda i: (0, 0)),
                                        pl.BlockSpec(memory_space=pl.ANY)],
        out_specs=[pl.BlockSpec((tm, Wqk), lambda i: (i, 0)), pl.BlockSpec((8, HEAD_DIM), lambda i: (0, 0))],
        out_shape=[jax.ShapeDtypeStruct(dproj.shape, dproj.dtype), jax.ShapeDtypeStruct((8, HEAD_DIM), F32)],
        input_output_aliases={n_in - 1: 0},
        compiler_params=_cp("arbitrary"),
    )(*dqs, *dks, proj, *tables, gains, dproj)


def _attn_tile(D, M):
    return max(HEAD_DIM * D, min(M, 2048))


class _TokenRows:
    GROUP = 16

    def __init__(self, D):
        self.D = D
        self.pitch = 24 if D == 16 else self.GROUP
        self.operand_dtype = F32 if D > 1 else ACT

    def rows(self, ntok):
        return ntok // self.GROUP * self.pitch

    def every_dth(self, tok0, n):
        start = tok0 // self.GROUP * self.pitch + tok0 % self.GROUP
        stride = self.D * self.pitch // self.GROUP
        return pl.ds(start, n) if stride == 1 else pl.ds(start, n, stride=stride)

    def put(self, dst, tok0, src_ref, ntok):
        if self.pitch == self.GROUP:
            dst[tok0:tok0 + ntok, :] = src_ref[...].astype(dst.dtype)
            return

        def group(i, carry):
            row = pl.multiple_of((tok0 // self.GROUP + i) * self.pitch, 8)
            dst[pl.ds(row, self.GROUP), :] = src_ref[pl.ds(pl.multiple_of(i * self.GROUP, self.GROUP), self.GROUP), :].astype(F32)
            return carry

        lax.fori_loop(0, ntok // self.GROUP, group, 0, unroll=8)

    def get(self, dst_ref, src, ntok):
        if self.pitch == self.GROUP:
            dst_ref[...] = src[0:ntok, :].astype(dst_ref.dtype)
            return

        def group(i, carry):
            row = pl.multiple_of(i * self.pitch, 8)
            dst_ref[pl.ds(pl.multiple_of(i * self.GROUP, self.GROUP), self.GROUP), :] = src[pl.ds(row, self.GROUP), :].astype(dst_ref.dtype)
            return carry

        lax.fori_loop(0, ntok // self.GROUP, group, 0, unroll=8)


def _attn_mask(base):
    qi = lax.broadcasted_iota(jnp.int32, (CHUNK, 2 * CHUNK), 0)
    ki = lax.broadcasted_iota(jnp.int32, (CHUNK, 2 * CHUNK), 1)
    return (ki >= qi) & (ki <= qi + CHUNK) & (ki >= CHUNK - base)


def _b_attn_fwd(qk, proj, g, name):
    M = qk.shape[0]
    D = B_DILATIONS[g]
    ngr = len(B_DILATIONS)
    T = _attn_tile(D, M)
    P = HEAD_DIM * D
    nsb = T // P
    Wg = B_HEADS * HEAD_DIM
    scale = np.float32(1.0 / np.sqrt(HEAD_DIM))

    lay = _TokenRows(D)
    RP, RT = lay.rows(P), lay.rows(T)

    def body(q_ref, k_ref, v_ref, o_ref, l_ref, qs, ks, vs, os_):
        n = pl.program_id(1)

        @pl.when(n == 0)
        def _():
            ks[0:RP, :] = jnp.zeros((RP, HEAD_DIM), ks.dtype)
            vs[0:RP, :] = jnp.zeros((RP, HEAD_DIM), vs.dtype)

        lay.put(qs, 0, q_ref, T)
        lay.put(ks, P, k_ref, T)
        lay.put(vs, P, v_ref, T)

        for b in range(nsb):
            mask = _attn_mask(n * (T // D) + b * CHUNK)
            for r in range(D):
                start = b * P + r
                q = qs[lay.every_dth(start, CHUNK), :]
                k = ks[lay.every_dth(start, 2 * CHUNK), :]
                v = vs[lay.every_dth(start, 2 * CHUNK), :]
                s = jnp.where(mask, _dot_nt(q, k) * scale, NEG)
                m = jnp.max(s, axis=-1, keepdims=True)
                p = jnp.exp(s - m)
                l = jnp.sum(p, axis=-1, keepdims=True)
                o = _dot(p, v) / l
                os_[lay.every_dth(start, CHUNK), :] = o
                l_ref[:, b * D + r:b * D + r + 1] = m + jnp.log(l)

        lay.get(o_ref, os_, T)
        ks[0:RP, :] = ks[RT:RT + RP, :]
        vs[0:RP, :] = vs[RT:RT + RP, :]

    blk = (T, HEAD_DIM)
    U = nsb * D
    return pl.pallas_call(
        body, name=name, grid=(B_HEADS, M // T),
        in_specs=[pl.BlockSpec(blk, lambda h, n: (n, g * B_HEADS + h)),
                  pl.BlockSpec(blk, lambda h, n: (n, (ngr + g) * B_HEADS + h)),
                  pl.BlockSpec(blk, lambda h, n: (n, (2 * ngr + g) * B_HEADS + h))],
        out_specs=[pl.BlockSpec(blk, lambda h, n: (n, h)), pl.BlockSpec((None, CHUNK, U), lambda h, n: (h, n, 0))],
        out_shape=[jax.ShapeDtypeStruct((M, Wg), ACT), jax.ShapeDtypeStruct((B_HEADS, (M // T) * CHUNK, U), F32)],
        scratch_shapes=[pltpu.VMEM((RT, HEAD_DIM), lay.operand_dtype), pltpu.VMEM((RP + RT, HEAD_DIM), lay.operand_dtype),
                        pltpu.VMEM((RP + RT, HEAD_DIM), lay.operand_dtype), pltpu.VMEM((RT, HEAD_DIM), F32)],
        compiler_params=_cp("parallel", "arbitrary"),
    )(qk, qk, proj)


def _units_to_tokens(a, D, T):
    H = a.shape[0]
    nsb = T // (HEAD_DIM * D)
    return a.reshape(H, -1, CHUNK, nsb, D).transpose(1, 3, 2, 4, 0).reshape(-1, H)


def _tokens_to_units(a, D, T):
    M, H = a.shape
    nsb = T // (HEAD_DIM * D)
    return a.reshape(M // T, nsb, CHUNK, D, H).transpose(4, 0, 2, 1, 3).reshape(H, (M // T) * CHUNK, nsb * D)


def _b_combine(os_, ls, proj, name):
    M, Wg = os_[0].shape
    ngr = len(B_DILATIONS)
    tm = min(M, 512)

    def body(*refs):
        o_refs, l_refs, z_ref = refs[:ngr], refs[ngr:2 * ngr], refs[2 * ngr]
        y_ref, o_ref, lse_ref = refs[2 * ngr + 1:]
        for h in range(B_HEADS):
            cols = slice(h * HEAD_DIM, (h + 1) * HEAD_DIM)
            ls_ = [r[:, h:h + 1] for r in l_refs]
            m = functools.reduce(jnp.maximum, ls_)
            es = [jnp.exp(l - m) for l in ls_]
            tot = functools.reduce(lambda a, b: a + b, es)
            o = functools.reduce(lambda a, b: a + b, [(e / tot) * r[:, cols].astype(F32) for e, r in zip(es, o_refs)])
            z = z_ref[:, cols].astype(F32)
            y_ref[:, cols] = (o * (z * _sigmoid(z))).astype(y_ref.dtype)
            o_ref[:, cols] = o.astype(o_ref.dtype)
            lse_ref[:, h:h + 1] = m + jnp.log(tot)

    spec = pl.BlockSpec((tm, Wg), lambda i: (i, 0))
    hspec = pl.BlockSpec((tm, B_HEADS), lambda i: (i, 0))
    return pl.pallas_call(
        body, name=name, grid=(M // tm,),
        in_specs=[spec] * ngr + [hspec] * ngr + [pl.BlockSpec((tm, Wg), lambda i: (i, 3 * ngr))],
        out_specs=[spec, spec, hspec],
        out_shape=[jax.ShapeDtypeStruct((M, Wg), ACT), jax.ShapeDtypeStruct((M, Wg), ACT),
                   jax.ShapeDtypeStruct((M, B_HEADS), F32)],
        compiler_params=_cp("parallel"),
    )(*os_, *ls, proj)


def _b_bwd_pre(dout, w_out, o, proj, name):
    M, Wg = o.shape
    Dm = dout.shape[1]
    ngr = len(B_DILATIONS)
    tm = min(M, 512)

    def body(do_ref, wo_ref, o_ref, z_ref, dov_ref, dl_ref, dp_ref):
        dy = _dot_nt(do_ref[...], wo_ref[...])
        z = z_ref[...].astype(F32)
        sig = _sigmoid(z)
        ov = o_ref[...].astype(F32)
        dp_ref[...] = (dy * ov * (sig * (1.0 + z * (1.0 - sig)))).astype(dp_ref.dtype)
        dov = dy * (z * sig)
        dov_ref[...] = dov.astype(dov_ref.dtype)
        prod = dov * ov
        for h in range(B_HEADS):
            dl_ref[:, h:h + 1] = jnp.sum(prod[:, h * HEAD_DIM:(h + 1) * HEAD_DIM], axis=-1, keepdims=True)

    spec = pl.BlockSpec((tm, Wg), lambda i: (i, 0))
    zspec = pl.BlockSpec((tm, Wg), lambda i: (i, 3 * ngr))
    return pl.pallas_call(
        body, name=name, grid=(M // tm,),
        in_specs=[pl.BlockSpec((tm, Dm), lambda i: (i, 0)), pl.BlockSpec((Wg, Dm), lambda i: (0, 0)), spec, zspec],
        out_specs=[spec, pl.BlockSpec((tm, B_HEADS), lambda i: (i, 0)), zspec],
        out_shape=[jax.ShapeDtypeStruct((M, Wg), ACT), jax.ShapeDtypeStruct((M, B_HEADS), F32),
                   jax.ShapeDtypeStruct(proj.shape, ACT)],
        compiler_params=_cp("parallel"),
    )(dout, w_out, o, proj)


def _b_attn_bwd(qk, proj, dov, lse, delta, dproj, g, name, comm=None):
    M = qk.shape[0]
    D = B_DILATIONS[g]
    ngr = len(B_DILATIONS)
    T = _attn_tile(D, M)
    P = HEAD_DIM * D
    nsb = T // P
    nt = M // T
    Wg = B_HEADS * HEAD_DIM
    scale = np.float32(1.0 / np.sqrt(HEAD_DIM))
    shift = T - P
    lay = _TokenRows(D)
    RP, RT = lay.rows(P), lay.rows(T)

    def body(q_ref, k_ref, v_ref, do_ref, l_ref, dl_ref, dp_any, dq_ref, dk_ref, dv_ref,
             qs, dos, ks, vs, dqs, dks, dvs):
        n = pl.program_id(1)

        @pl.when(n == 0)
        def _():
            ks[0:RP, :] = jnp.zeros((RP, HEAD_DIM), ks.dtype)
            vs[0:RP, :] = jnp.zeros((RP, HEAD_DIM), vs.dtype)
            dks[...] = jnp.zeros((2 * RT, HEAD_DIM), F32)
            dvs[...] = jnp.zeros((2 * RT, HEAD_DIM), F32)

        @pl.when(n < nt)
        def _():
            lay.put(qs, 0, q_ref, T)
            lay.put(dos, 0, do_ref, T)
            lay.put(ks, P, k_ref, T)
            lay.put(vs, P, v_ref, T)

            masks = [_attn_mask(n * (T // D) + b * CHUNK) for b in range(nsb)]
            for r in range(D):
                carry_dv = carry_dk = None
                for b in range(nsb):
                    start = b * P + r
                    qsl = lay.every_dth(start, CHUNK)
                    ksl = lay.every_dth(start, 2 * CHUNK)
                    lo = lay.every_dth(start + shift, CHUNK)
                    q = qs[qsl, :]
                    do = dos[qsl, :]
                    k = ks[ksl, :]
                    v = vs[ksl, :]
                    s = _dot_nt(q, k) * scale
                    u = b * D + r
                    p = jnp.where(masks[b], jnp.exp(s - l_ref[:, u:u + 1]), 0.0)
                    dv = _dot_tn(p, do)
                    dp = _dot_nt(do, v)
                    ds = (p * (dp - dl_ref[:, u:u + 1]) * scale).astype(MXU)
                    dqs[qsl, :] = _dot(ds, k)
                    dk = _dot_tn(ds, q)
                    if b == 0:
                        dvs[lo, :] += dv[:CHUNK]
                        dks[lo, :] += dk[:CHUNK]
                    else:
                        dvs[lo, :] = carry_dv + dv[:CHUNK]
                        dks[lo, :] = carry_dk + dk[:CHUNK]
                    carry_dv, carry_dk = dv[CHUNK:], dk[CHUNK:]
                hi = lay.every_dth((nsb - 1) * P + r + shift + P, CHUNK)
                dvs[hi, :] = carry_dv
                dks[hi, :] = carry_dk

        lay.get(dq_ref, dqs, T)
        lay.get(dk_ref, dks, T)
        lay.get(dv_ref, dvs, T)
        dks[0:RT, :] = dks[RT:2 * RT, :]
        dvs[0:RT, :] = dvs[RT:2 * RT, :]
        ks[0:RP, :] = ks[RT:RT + RP, :]
        vs[0:RP, :] = vs[RT:RT + RP, :]

    blk = (T, HEAD_DIM)
    cur = lambda n: jnp.minimum(n, nt - 1)
    prv = lambda n: jnp.maximum(n - 1, 0)
    return _launch(
        body, name=name, grid=(B_HEADS, nt + 1),
        in_specs=[pl.BlockSpec(blk, lambda h, n: (cur(n), g * B_HEADS + h)),
                  pl.BlockSpec(blk, lambda h, n: (cur(n), (ngr + g) * B_HEADS + h)),
                  pl.BlockSpec(blk, lambda h, n: (cur(n), (2 * ngr + g) * B_HEADS + h)),
                  pl.BlockSpec(blk, lambda h, n: (cur(n), h)),
                  pl.BlockSpec((None, CHUNK, nsb * D), lambda h, n: (h, cur(n), 0)),
                  pl.BlockSpec((None, CHUNK, nsb * D), lambda h, n: (h, cur(n), 0)),
                  pl.BlockSpec(memory_space=pl.ANY)],
        out_specs=[pl.BlockSpec(blk, lambda h, n: (cur(n), h)),
                   pl.BlockSpec(blk, lambda h, n: (prv(n), h)),
                   pl.BlockSpec(blk, lambda h, n: (prv(n), (2 * ngr + g) * B_HEADS + h))],
        out_shape=[jax.ShapeDtypeStruct((M, Wg), ACT), jax.ShapeDtypeStruct((M, Wg), ACT),
                   jax.ShapeDtypeStruct(dproj.shape, dproj.dtype)],
        scratch=[pltpu.VMEM((RT, HEAD_DIM), lay.operand_dtype)] * 2
        + [pltpu.VMEM((RP + RT, HEAD_DIM), lay.operand_dtype)] * 2
        + [pltpu.VMEM((RT, HEAD_DIM), F32)]
        + [pltpu.VMEM((2 * RT, HEAD_DIM), F32)] * 2,
        aliases={6: 2},
        args=(qk, qk, proj, dov, lse, delta, dproj), sem=("parallel", "arbitrary"), comm=comm)


def _coords():
    return lax.axis_index("x"), lax.axis_index("y"), lax.axis_index("c")


def _gather_blocks(x_refs, out_refs, send_sems, recv_sems, local_sems):
    x, y, c = _coords()
    me, sibling = (x, y, c), (x, y, 1 - c)
    chips = [(1 - x, y), (x, 1 - y), (1 - x, 1 - y)]
    arrays = range(len(x_refs))

    def slot(a, px, py, pc):
        return out_refs[a].at[4 * px + 2 * py + pc]

    def copy(a, k, block, to, src=None):
        return _remote(slot(a, *block) if src is None else src, slot(a, *block), send_sems, recv_sems, 7 * a + k, to)

    mine = [pltpu.make_async_copy(x_refs[a], slot(a, *me), local_sems.at[a]) for a in arrays]
    first = [copy(a, 0, me, sibling, src=x_refs[a]) for a in arrays]
    first += [copy(a, 1 + j, me, (*chip, c), src=x_refs[a]) for j, chip in enumerate(chips) for a in arrays]
    for cp in mine + first:
        cp.start()
    passed = []
    for j, chip in enumerate(chips):
        for a in arrays:
            copy(a, 1 + j, (*chip, c), me).wait_recv()
            passed.append(copy(a, 4 + j, (*chip, c), sibling))
            passed[-1].start()
    for a in arrays:
        copy(a, 0, sibling, me).wait_recv()
        for j, chip in enumerate(chips):
            copy(a, 4 + j, (*chip, 1 - c), me).wait_recv()
    for cp in first + passed:
        cp.wait_send()
    for cp in mine:
        cp.wait()


def _all_gather_hbm(arrays, name):
    n = len(arrays)

    def body(*refs):
        _gather_blocks(refs[:n], refs[n:2 * n], *refs[2 * n:])

    return pl.pallas_call(
        body, name=name, in_specs=[_HBM] * n, out_specs=[_HBM] * n,
        out_shape=[jax.ShapeDtypeStruct((N_DEV,) + a.shape, a.dtype) for a in arrays],
        scratch_shapes=[pltpu.SemaphoreType.DMA((7 * n,)), pltpu.SemaphoreType.DMA((7 * n,)),
                        pltpu.SemaphoreType.DMA((n,))],
    )(*arrays)


def _all_reduce_small(part):
    R, C = part.shape

    def body(x_ref, tot_ref, gath, send_sems, recv_sems, local_sems):
        _gather_blocks([x_ref], [gath], send_sems, recv_sems, local_sems)
        acc = gath[0]
        for d in range(1, N_DEV):
            acc = acc + gath[d]
        tot_ref[...] = acc

    return pl.pallas_call(
        body, name="ar_small",
        in_specs=[pl.BlockSpec(memory_space=pltpu.VMEM)],
        out_specs=pl.BlockSpec(memory_space=pltpu.VMEM),
        out_shape=jax.ShapeDtypeStruct((R, C), F32),
        scratch_shapes=[pltpu.VMEM((N_DEV, R, C), F32),
                        pltpu.SemaphoreType.DMA((7,)), pltpu.SemaphoreType.DMA((7,)), pltpu.SemaphoreType.DMA((1,))],
        compiler_params=pltpu.CompilerParams(vmem_limit_bytes=VMEM_LIMIT),
    )(part)


def _sum_blocks(gath, name):
    _, R, C = gath.shape

    def body(g_ref, o_ref):
        acc = g_ref[0]
        for d in range(1, N_DEV):
            acc = acc + g_ref[d]
        o_ref[...] = acc

    return pl.pallas_call(
        body, name=name,
        in_specs=[pl.BlockSpec(memory_space=pltpu.VMEM)], out_specs=pl.BlockSpec(memory_space=pltpu.VMEM),
        out_shape=jax.ShapeDtypeStruct((R, C), F32),
        compiler_params=pltpu.CompilerParams(vmem_limit_bytes=VMEM_LIMIT),
    )(gath)


def _remote(src, dst, send_sems, recv_sems, k, peer):
    return pltpu.make_async_remote_copy(src_ref=src, dst_ref=dst, send_sem=send_sems.at[k], recv_sem=recv_sems.at[k],
                                        device_id=peer, device_id_type=MESH)


def _ag_send(arrays):
    n = len(arrays)

    def make(c_in, c_out, send_sems, recv_sems, local_sems):
        x, y, c = _coords()
        peers = [(x, y, 1 - c), (1 - x, y, c), (x, 1 - y, c), (1 - x, 1 - y, c)]
        cps = []
        for a in range(n):
            src, dst = c_in[a], c_out[a].at[4 * x + 2 * y + c]
            cps.append(pltpu.make_async_copy(src, dst, local_sems.at[a]))
            cps += [_remote(src, dst, send_sems, recv_sems, 4 * a + k, peer) for k, peer in enumerate(peers)]
        return cps

    return _Comm(arrays, [jax.ShapeDtypeStruct((N_DEV,) + a.shape, a.dtype) for a in arrays], 4 * n, make, n_local=n)


def _ag_forward(gaths):
    n = len(gaths)

    def make(c_in, c_out, send_sems, recv_sems, local_sems):
        x, y, c = _coords()
        chips = [(1 - x, y), (x, 1 - y), (1 - x, 1 - y)]
        cps = []
        for a in range(n):
            buf = c_out[a]
            cps += [_remote(buf.at[4 * px + 2 * py + c], buf.at[4 * px + 2 * py + c], send_sems, recv_sems, 3 * a + j,
                            (x, y, 1 - c)) for j, (px, py) in enumerate(chips)]
        return cps

    return _Comm(gaths, [jax.ShapeDtypeStruct(g.shape, g.dtype) for g in gaths], 3 * n, make,
                 aliases={a: a for a in range(n)})


def _rs_sibling(grads):
    n = len(grads)

    def make(c_in, c_out, send_sems, recv_sems, local_sem):
        x, y, c = _coords()
        return [_remote(c_in[a].at[pl.ds(4 * (1 - c), 4)], c_out[a], send_sems, recv_sems, a, (x, y, 1 - c))
                for a in range(n)]

    return _Comm(grads, [jax.ShapeDtypeStruct((4,) + g.shape[1:], g.dtype) for g in grads], n, make)


def _rs_chips(parts):
    n = len(parts)

    def make(c_in, c_out, send_sems, recv_sems, local_sem):
        x, y, c = _coords()
        peers = [(x, 1 - y, c), (1 - x, y, c), (1 - x, 1 - y, c)]
        return [_remote(c_in[a].at[k], c_out[a].at[k], send_sems, recv_sems, 3 * a + k, peer)
                for a in range(n) for k, peer in enumerate(peers)]

    return _Comm(parts, [jax.ShapeDtypeStruct(p.shape, p.dtype) for p in parts], 3 * n, make)


def _row_tile(rows, cols):
    tr = min(rows, 1 << int(np.log2((1 << 18) // cols)))
    assert rows % tr == 0
    return tr


def _chip_partials(coords, g, r1, name):
    _, rows, C = g.shape
    tr = _row_tile(rows, C)

    def body(co_ref, g_ref, r_ref, o_ref):
        o_ref[...] = (g_ref[...] + r_ref[...]).astype(o_ref.dtype)

    def chip(k, co):
        return jnp.bitwise_xor(2 * co[0] + co[1], k + 1)

    return pl.pallas_call(
        body, name=name,
        grid_spec=pltpu.PrefetchScalarGridSpec(
            num_scalar_prefetch=1, grid=(3, rows // tr),
            in_specs=[pl.BlockSpec((None, tr, C), lambda k, t, co: (4 * co[2] + chip(k, co), t, 0)),
                      pl.BlockSpec((None, tr, C), lambda k, t, co: (chip(k, co), t, 0))],
            out_specs=pl.BlockSpec((None, tr, C), lambda k, t, co: (k, t, 0))),
        out_shape=jax.ShapeDtypeStruct((3, rows, C), WIRE),
        compiler_params=_cp("parallel", "parallel"),
    )(coords, g, r1)


def _adam_math(w, g, m, v):
    m = ADAM_B1 * m + (1.0 - ADAM_B1) * g
    v = ADAM_B2 * v + (1.0 - ADAM_B2) * (g * g)
    m_hat = m / (1.0 - ADAM_B1 ** ADAM_STEP)
    v_hat = v / (1.0 - ADAM_B2 ** ADAM_STEP)
    delta = -ADAM_LR * (m_hat / (jnp.sqrt(v_hat) + ADAM_EPS) + ADAM_WD * w)
    return delta, m, v


def _adamw_sharded(coords, w, m, v, g, r1, r2, layer, prev, name):
    L, rows, C = w.shape
    tr = _row_tile(rows, C)
    n_prev = 0 if prev is None else len(prev)

    def body(co_ref, w_ref, m_ref, v_ref, g_ref, r1_ref, r2_ref, *rest):
        go_ref, d_ref, mo_ref, vo_ref = rest[n_prev:]
        grad = g_ref[...] + r1_ref[...]
        for k in range(3):
            grad = grad + r2_ref[k].astype(F32)
        go_ref[...] = grad
        d_ref[...], mo_ref[...], vo_ref[...] = _adam_math(w_ref[...], grad, m_ref[...], v_ref[...])

    spec = pl.BlockSpec((None, tr, C), lambda t, co: (layer, t, 0))
    return pl.pallas_call(
        body, name=name,
        grid_spec=pltpu.PrefetchScalarGridSpec(
            num_scalar_prefetch=1, grid=(rows // tr,),
            in_specs=[spec, spec, spec,
                      pl.BlockSpec((None, tr, C), lambda t, co: (4 * co[2] + 2 * co[0] + co[1], t, 0)),
                      pl.BlockSpec((None, tr, C), lambda t, co: (2 * co[0] + co[1], t, 0)),
                      pl.BlockSpec((3, tr, C), lambda t, co: (0, t, 0))] + [_HBM] * n_prev,
            out_specs=[spec] * 4),
        out_shape=[jax.ShapeDtypeStruct((L, rows, C), F32)] * 4,
        input_output_aliases={7 + k: k for k in range(n_prev)},
        compiler_params=_cp("parallel"),
    )(coords, w, m, v, g, r1, r2, *(prev or []))


def _adamw_small(w, g, m, v, name):
    def body(w_ref, g_ref, m_ref, v_ref, d_ref, mo_ref, vo_ref):
        d_ref[...], mo_ref[...], vo_ref[...] = _adam_math(w_ref[...], g_ref[...], m_ref[...], v_ref[...])

    return pl.pallas_call(
        body, name=name, out_shape=[jax.ShapeDtypeStruct(w.shape, F32)] * 3,
        in_specs=[pl.BlockSpec(memory_space=pltpu.VMEM)] * 4,
        out_specs=[pl.BlockSpec(memory_space=pltpu.VMEM)] * 3,
    )(w, g, m, v)


def _reduce_scatter_adds(coords, grads, r1s, tag):
    return [_chip_partials(coords, g, r, f"rs_add_{tag}{i}") for i, (g, r) in enumerate(zip(grads, r1s))]


def kernel(x, norm_gain, a_w_in, a_v_gain, a_w_s, a_b_s, a_w_out, b_w_in, b_q_gain, b_k_gain, b_w_out, c_w_in, c_w_grp, c_scale, c_w_out, loss_target, m_norm_gain, m_a_w_in, m_a_v_gain, m_a_w_s, m_a_b_s, m_a_w_out, m_b_w_in, m_b_q_gain, m_b_k_gain, m_b_w_out, m_c_w_in, m_c_w_grp, m_c_scale, m_c_w_out, v_norm_gain, v_a_w_in, v_a_v_gain, v_a_w_s, v_a_b_s, v_a_w_out, v_b_w_in, v_b_q_gain, v_b_k_gain, v_b_w_out, v_c_w_in, v_c_w_grp, v_c_scale, v_c_w_out):
    cx, cy, cc = _coords()
    coords = jnp.stack([cx, cy, cc]).astype(jnp.int32)
    dev = 4 * cx + 2 * cy + cc
    Dm = x.shape[2]

    xs, tgt = x[0], loss_target[0]
    tables = _rope_tables(xs.shape[0])
    ng = lambda i: norm_gain[i:i + 1]
    ngr = len(B_DILATIONS)
    bst = [a_b_s[l].T for l in range(2)]
    b_gains = jnp.concatenate([b_q_gain[0], b_k_gain[0], jnp.zeros((2, HEAD_DIM), F32)], axis=0)
    nla, nlb, nlc = a_w_in.shape[2], b_w_in.shape[2], c_w_in.shape[2]
    ngp, rlc, cgc = c_w_grp.shape[1:]
    wire = lambda w: w.astype(WIRE)

    nvg, nsc = a_v_gain.size, c_scale.size
    vec = jnp.concatenate([a_v_gain.reshape(-1), c_scale.reshape(-1), jnp.zeros((1024 - nvg - nsc,), F32)]).reshape(8, 128)
    wa_in0, wa_out0, vecs = _all_gather_hbm([wire(a_w_in[0]), wire(a_w_out[0]), vec], "ag_layer0")
    wa_out0 = wa_out0.reshape(-1, Dm)
    vecs = vecs.reshape(N_DEV, -1)
    a_vg = vecs[:, :nvg].reshape((N_DEV,) + a_v_gain.shape).transpose(1, 0, 2).reshape(a_v_gain.shape[0], -1)
    c_sc = vecs[:, nvg:nvg + nsc].reshape(1, -1)

    h0, p0, *g1 = _norm_proj(xs, ng(0), wa_in0, "l0_proj", comm=_ag_send([wire(b_w_in[0]), wire(b_w_out[0])]))
    y0, wb_in, wb_out = _a_mid(p0, a_vg[0:1], a_w_s[0], bst[0], "l0_mid", comm=_ag_forward(g1))
    x1 = _out_proj(xs, y0, wa_out0, "l0_out")
    wb_out = wb_out.reshape(-1, Dm)

    later = [wire(c_w_in[0]), wire(c_w_grp[0]), wire(c_w_out[0]), wire(a_w_in[1]), wire(a_w_out[1])]
    h1, p1, *g2 = _norm_proj(x1, ng(1), wb_in, "l1_proj", comm=_ag_send(later))
    qk, wc_in, wc_grp, wc_out, wa_in1, wa_out1 = _b_qk_fwd(p1, tables, b_gains, "l1_qk", comm=_ag_forward(g2))
    ogs, lgs = zip(*[_b_attn_fwd(qk, p1, g, f"l1_attn{g}") for g in range(ngr)])
    tiles = [_attn_tile(D, xs.shape[0]) for D in B_DILATIONS]
    lgs = [_units_to_tokens(l, D, T) for l, D, T in zip(lgs, B_DILATIONS, tiles)]
    y1, o1, lse = _b_combine(ogs, lgs, p1, "l1_comb")
    x2 = _out_proj(x1, y1, wb_out, "l1_out")
    wc_grp = wc_grp.transpose(1, 0, 2, 3).reshape(ngp, N_DEV * rlc, cgc)
    wc_out = wc_out.reshape(-1, Dm)
    wa_out1 = wa_out1.reshape(-1, Dm)

    h2, p2 = _norm_proj(x2, ng(2), wc_in, "l2_proj")
    y2 = _c_mid(p2, wc_grp, c_sc, "l2_mid")
    x3 = _out_proj(x2, y2, wc_out, "l2_out")
    h3, p3 = _norm_proj(x3, ng(3), wa_in1, "l3_proj")
    y3, = _a_mid(p3, a_vg[1:2], a_w_s[1], bst[1], "l3_mid")
    loss_local, dx4, dx4a = _out_proj_loss(x3, y3, wa_out1, tgt, "l3_out_loss")

    flat3 = lambda g: g.reshape(N_DEV, -1, g.shape[-1])
    dp3, dws1, dbs1, dvg1 = _a_bwd(dx4a, wa_out1, p3, a_vg[1:2], a_w_s[1], bst[1], "l3_bwd")
    grads3 = [_dw_in(h3, dp3, "l3_dwin"), _dw_out(y3, dx4a, "l3_dwout")]
    dx3, dx3a, dg3, *r1_3 = _dh_norm_bwd(dp3, wa_in1, x3, ng(3), dx4, "l3_dh", comm=_rs_sibling(grads3))
    parts3 = _reduce_scatter_adds(coords, grads3, r1_3, "l3_")

    dd, dz, gc_grp, dsc, *r2_3 = _c_bwd1(dx3a, wc_out, p2, wc_grp, c_sc, "l2_bwd1", comm=_rs_chips(parts3))
    dp2 = _c_bwd2(dd, dz, "l2_bwd2")
    grads2 = [_dw_in(h2, dp2, "l2_dwin"), _dw_out(y2, dx3a, "l2_dwout"), flat3(gc_grp)]
    dx2, dx2a, dg2, *r1_2 = _dh_norm_bwd(dp2, wc_in, x2, ng(2), dx3, "l2_dh", comm=_rs_sibling(grads2))
    parts2 = _reduce_scatter_adds(coords, grads2, r1_2, "l2_")

    dov, delta, dp1 = _b_bwd_pre(dx2a, wb_out, o1, p1, "l1_bwdpre")
    dqs, dks, r2_2 = [], [], None
    for g in range(ngr):
        lse_u, delta_u = [_tokens_to_units(a, B_DILATIONS[g], tiles[g]) for a in (lse, delta)]
        dq, dk, dp1, *rest = _b_attn_bwd(qk, p1, dov, lse_u, delta_u, dp1, g, f"l1_attnbwd{g}",
                                         comm=_rs_chips(parts2) if g == 0 else None)
        if g == 0:
            r2_2 = rest
        dqs.append(dq)
        dks.append(dk)
    dp1, dgains = _b_qk_bwd(dqs, dks, p1, tables, b_gains, dp1, "l1_qkbwd")
    grads1 = [_dw_in(h1, dp1, "l1_dwin"), _dw_out(y1, dx2a, "l1_dwout")]
    dx1, dx1a, dg1, *r1_1 = _dh_norm_bwd(dp1, wb_in, x1, ng(1), dx2, "l1_dh", comm=_rs_sibling(grads1))
    parts1 = _reduce_scatter_adds(coords, grads1, r1_1, "l1_")

    dp0, dws0, dbs0, dvg0, *r2_1 = _a_bwd(dx1a, wa_out0, p0, a_vg[0:1], a_w_s[0], bst[0], "l0_bwd", comm=_rs_chips(parts1))
    small = dict(norm=jnp.concatenate([dg1, dg2, dg3], axis=0), a_ws=jnp.stack([dws0, dws1]),
                 a_bs=jnp.stack([dbs0.T, dbs1.T]), b_gains=dgains, a_vg=jnp.concatenate([dvg0, dvg1], axis=0), c_sc=dsc)
    order = ["norm", "a_ws", "a_bs", "b_gains", "a_vg", "c_sc"]
    rows = [small[k].reshape(-1, 128) for k in order]
    roff = np.cumsum([0] + [r.shape[0] for r in rows])
    gw_in0, gsmall = _dw_in(h0, dp0, "l0_dwin", comm=_ag_send([jnp.concatenate(rows, axis=0)]))
    gw_out0, gsmall = _dw_out(y0, dx1a, "l0_dwout", comm=_ag_forward([gsmall]))
    grads0 = [gw_in0, gw_out0]
    r1_0 = _run_comm(_rs_sibling(grads0), "l0_rs_sibling")
    parts0 = _reduce_scatter_adds(coords, grads0, r1_0, "l0_")
    dx0, _, dg0, *r2_0 = _dh_norm_bwd(dp0, wa_in0, xs, ng(0), dx1, "l0_dh", comm=_rs_chips(parts0))

    tot = _sum_blocks(gsmall, "small_sum")
    sm = {k: tot[int(roff[i]):int(roff[i + 1])].reshape(small[k].shape) for i, k in enumerate(order)}
    late = _all_reduce_small(jnp.concatenate([dg0.reshape(-1, 128), jnp.full((8, 128), loss_local, F32)], axis=0))
    sm["norm"] = jnp.concatenate([late[0:8].reshape(1, -1), sm["norm"]], axis=0)
    loss = late[8, 0]
    vl = a_v_gain.shape[1]
    g_small = dict(
        norm_gain=sm["norm"], a_w_s=sm["a_ws"], a_b_s=sm["a_bs"],
        b_q_gain=sm["b_gains"][None, 0:3], b_k_gain=sm["b_gains"][None, 3:6],
        a_v_gain=lax.dynamic_slice_in_dim(sm["a_vg"], dev * vl, vl, axis=1),
        c_scale=lax.dynamic_slice_in_dim(sm["c_sc"], dev * vl, vl, axis=1),
    )

    shares = dict(
        a_w_in=[(grads0[0], r1_0[0], r2_0[0]), (grads3[0], r1_3[0], r2_3[0])],
        a_w_out=[(grads0[1], r1_0[1], r2_0[1]), (grads3[1], r1_3[1], r2_3[1])],
        b_w_in=[(grads1[0], r1_1[0], r2_1[0])], b_w_out=[(grads1[1], r1_1[1], r2_1[1])],
        c_w_in=[(grads2[0], r1_2[0], r2_2[0])], c_w_out=[(grads2[1], r1_2[1], r2_2[1])],
        c_w_grp=[(grads2[2], r1_2[2], r2_2[2])])

    params = dict(a_w_in=a_w_in, a_w_out=a_w_out, b_w_in=b_w_in, b_w_out=b_w_out, c_w_in=c_w_in, c_w_grp=c_w_grp, c_w_out=c_w_out,
                  norm_gain=norm_gain, a_v_gain=a_v_gain, a_w_s=a_w_s, a_b_s=a_b_s, b_q_gain=b_q_gain, b_k_gain=b_k_gain, c_scale=c_scale)
    moms = dict(a_w_in=(m_a_w_in, v_a_w_in), a_w_out=(m_a_w_out, v_a_w_out), b_w_in=(m_b_w_in, v_b_w_in), b_w_out=(m_b_w_out, v_b_w_out),
                c_w_in=(m_c_w_in, v_c_w_in), c_w_grp=(m_c_w_grp, v_c_w_grp), c_w_out=(m_c_w_out, v_c_w_out),
                norm_gain=(m_norm_gain, v_norm_gain), a_v_gain=(m_a_v_gain, v_a_v_gain), a_w_s=(m_a_w_s, v_a_w_s),
                a_b_s=(m_a_b_s, v_a_b_s), b_q_gain=(m_b_q_gain, v_b_q_gain), b_k_gain=(m_b_k_gain, v_b_k_gain),
                c_scale=(m_c_scale, v_c_scale))
    grad, delta, new_m, new_v = {}, {}, {}, {}
    for pname, layers in shares.items():
        w, (m, v) = params[pname], moms[pname]
        as3 = lambda a: a.reshape(a.shape[0], -1, a.shape[-1])
        outs = None
        for l, (g, r1, r2) in enumerate(layers):
            outs = _adamw_sharded(coords, as3(w), as3(m), as3(v), g, r1, r2, l, outs, f"adamw_{pname}{l}")
        grad[pname], delta[pname], new_m[pname], new_v[pname] = [o.reshape(w.shape) for o in outs]
    for pname, g in g_small.items():
        w = params[pname]
        C = w.shape[-1]
        outs = _adamw_small(w.reshape(-1, C), g.reshape(-1, C), moms[pname][0].reshape(-1, C), moms[pname][1].reshape(-1, C),
                            f"adamw_{pname}")
        grad[pname] = g.reshape(w.shape)
        delta[pname], new_m[pname], new_v[pname] = [o.reshape(w.shape) for o in outs]

    wnames = ["norm_gain", "a_w_in", "a_v_gain", "a_w_s", "a_b_s", "a_w_out", "b_w_in", "b_q_gain", "b_k_gain", "b_w_out",
              "c_w_in", "c_w_grp", "c_scale", "c_w_out"]
    return (loss, dx0[None], *[grad[n] for n in wnames], *[delta[n] for n in wnames],
            *[new_m[n] for n in wnames], *[new_v[n] for n in wnames])
```

```python
import functools

import numpy as np
import jax
import jax.numpy as jnp
from jax import lax
from jax.experimental import pallas as pl
from jax.experimental.pallas import tpu as pltpu

F32 = jnp.float32
MXU = jnp.bfloat16
ACT = jnp.bfloat16
WIRE = jnp.bfloat16

EPS = 1e-6
CHUNK = 128
A_GROUPS = 8
HEAD_DIM = 128
B_HEADS = 8
B_DILATIONS = (1, 4, 16)
ROPE_DIM = 32
ROPE_THETA = 500000.0
POOL_SIZES = (2, 4, 8, 16)
POOL_HALO = 16
N_DEV = 8
NEG = -1e30

ADAM_LR, ADAM_B1, ADAM_B2, ADAM_EPS, ADAM_WD, ADAM_STEP = 0.001, 0.9, 0.999, 1e-08, 0.01, 10

VMEM_LIMIT = 62 * 1024 * 1024
MESH = pl.DeviceIdType.MESH


def _cp(*sem):
    return pltpu.CompilerParams(dimension_semantics=sem, vmem_limit_bytes=VMEM_LIMIT)


def _sigmoid(z):
    return 1.0 / (1.0 + jnp.exp(-z))


def _dot(a, b):
    return jnp.dot(a.astype(MXU), b.astype(MXU), preferred_element_type=F32)


def _dot_nt(a, b):
    return lax.dot_general(a.astype(MXU), b.astype(MXU), (((1,), (1,)), ((), ())), preferred_element_type=F32)


def _dot_tn(a, b):
    return lax.dot_general(a.astype(MXU), b.astype(MXU), (((0,), (0,)), ((), ())), preferred_element_type=F32)


def _chunk_slot(d):
    return (d % 2) * 4 + d // 2


class _Comm:
    def __init__(self, inputs, out_shapes, n_remote, make, aliases=None, n_local=1):
        self.inputs = list(inputs)
        self.out_shapes = list(out_shapes)
        self.n_remote = n_remote
        self.n_local = n_local
        self.make = make
        self.aliases = dict(aliases or {})

    def sems(self):
        return [pltpu.SemaphoreType.DMA((self.n_remote,)), pltpu.SemaphoreType.DMA((self.n_remote,)),
                pltpu.SemaphoreType.DMA((self.n_local,))]


_HBM = pl.BlockSpec(memory_space=pl.ANY)


def _launch(body, *, name, grid, in_specs, out_specs, out_shape, args, sem, scratch=(), aliases=None, comm=None):
    in_specs, out_specs, out_shape, scratch = list(in_specs), list(out_specs), list(out_shape), list(scratch)
    aliases = dict(aliases or {})
    if comm is None:
        return pl.pallas_call(body, name=name, grid=grid, in_specs=in_specs, out_specs=out_specs, out_shape=out_shape,
                              scratch_shapes=scratch, input_output_aliases=aliases, compiler_params=_cp(*sem))(*args)
    n_in, n_out, n_sc = len(in_specs), len(out_specs), len(scratch)
    nci, nco = len(comm.inputs), len(comm.out_shapes)

    def hosted(*refs):
        b_in, c_in = refs[:n_in], refs[n_in:n_in + nci]
        o0 = n_in + nci
        b_out, c_out = refs[o0:o0 + n_out], refs[o0 + n_out:o0 + n_out + nco]
        s0 = o0 + n_out + nco
        b_sc, sems = refs[s0:s0 + n_sc], refs[s0 + n_sc:]
        ids = [pl.program_id(a) for a in range(len(grid))]
        first = functools.reduce(jnp.logical_and, [i == 0 for i in ids])
        last = functools.reduce(jnp.logical_and, [i == g - 1 for i, g in zip(ids, grid)])

        @pl.when(first)
        def _():
            for cp in comm.make(c_in, c_out, *sems):
                cp.start()

        body(*b_in, *b_out, *b_sc)

        @pl.when(last)
        def _():
            for cp in comm.make(c_in, c_out, *sems):
                cp.wait()

    for ci, co in comm.aliases.items():
        aliases[n_in + ci] = n_out + co
    return pl.pallas_call(
        hosted, name=name, grid=grid, in_specs=in_specs + [_HBM] * nci, out_specs=out_specs + [_HBM] * nco,
        out_shape=out_shape + comm.out_shapes, scratch_shapes=scratch + comm.sems(),
        input_output_aliases=aliases, compiler_params=_cp(*["arbitrary"] * len(grid)))(*args, *comm.inputs)


def _run_comm(comm, name):
    nci, nco = len(comm.inputs), len(comm.out_shapes)

    def body(*refs):
        cps = comm.make(refs[:nci], refs[nci:nci + nco], *refs[nci + nco:])
        for cp in cps:
            cp.start()
        for cp in cps:
            cp.wait()

    return pl.pallas_call(
        body, name=name, in_specs=[_HBM] * nci, out_specs=[_HBM] * nco, out_shape=comm.out_shapes,
        scratch_shapes=comm.sems(), input_output_aliases=dict(comm.aliases))(*comm.inputs)


def _norm_proj(x, gain, w_dm, name, comm=None):
    M, Dm = x.shape
    nd, _, nl = w_dm.shape
    tm = min(M, 2048)

    def body(x_ref, g_ref, w_ref, h_ref, p_ref):
        @pl.when(pl.program_id(1) == 0)
        def _():
            xv = x_ref[...]
            r = lax.rsqrt(jnp.mean(xv * xv, axis=-1, keepdims=True) + EPS)
            h_ref[...] = (xv * r * g_ref[...]).astype(h_ref.dtype)

        p_ref[...] = _dot(h_ref[...], w_ref[...]).astype(p_ref.dtype)

    return _launch(
        body, name=name, grid=(M // tm, nd),
        in_specs=[pl.BlockSpec((tm, Dm), lambda i, j: (i, 0)),
                  pl.BlockSpec((1, Dm), lambda i, j: (0, 0)),
                  pl.BlockSpec((None, Dm, nl), lambda i, j: (j, 0, 0))],
        out_specs=[pl.BlockSpec((tm, Dm), lambda i, j: (i, 0)),
                   pl.BlockSpec((tm, nl), lambda i, j: (i, j))],
        out_shape=[jax.ShapeDtypeStruct((M, Dm), ACT), jax.ShapeDtypeStruct((M, nd * nl), ACT)],
        args=(x, gain, w_dm), sem=("parallel", "arbitrary"), comm=comm)


def _out_proj(x, y, w, name):
    M, Dm = x.shape
    K = y.shape[1]
    tm = min(M, 1024)

    def body(x_ref, y_ref, w_ref, o_ref):
        o_ref[...] = x_ref[...] + _dot(y_ref[...], w_ref[...])

    return pl.pallas_call(
        body, name=name, grid=(M // tm,),
        in_specs=[pl.BlockSpec((tm, Dm), lambda i: (i, 0)),
                  pl.BlockSpec((tm, K), lambda i: (i, 0)),
                  pl.BlockSpec((K, Dm), lambda i: (0, 0))],
        out_specs=pl.BlockSpec((tm, Dm), lambda i: (i, 0)),
        out_shape=jax.ShapeDtypeStruct((M, Dm), F32),
        compiler_params=_cp("parallel"),
    )(x, y, w)


def _out_proj_loss(x, y, w, target, name):
    M, Dm = x.shape
    K = y.shape[1]
    tm = min(M, 512)

    def body(x_ref, y_ref, w_ref, t_ref, dx_ref, dxa_ref, l_ref):
        @pl.when(pl.program_id(0) == 0)
        def _():
            l_ref[...] = jnp.zeros_like(l_ref)

        err = x_ref[...] + _dot(y_ref[...], w_ref[...]) - t_ref[...]
        dx = err * (1.0 / Dm)
        dx_ref[...] = dx
        dxa_ref[...] = dx.astype(dxa_ref.dtype)
        l_ref[...] += jnp.sum(err * err) * (0.5 / Dm)

    spec = pl.BlockSpec((tm, Dm), lambda i: (i, 0))
    dx, dxa, l = pl.pallas_call(
        body, name=name, grid=(M // tm,),
        in_specs=[spec, pl.BlockSpec((tm, K), lambda i: (i, 0)), pl.BlockSpec((K, Dm), lambda i: (0, 0)), spec],
        out_specs=[spec, spec, pl.BlockSpec((8, 128), lambda i: (0, 0))],
        out_shape=[jax.ShapeDtypeStruct((M, Dm), F32), jax.ShapeDtypeStruct((M, Dm), ACT),
                   jax.ShapeDtypeStruct((8, 128), F32)],
        compiler_params=_cp("arbitrary"),
    )(x, y, w, target)
    return l[0, 0], dx, dxa


def _dw_in(h, dproj, name, comm=None):
    M, Dm = h.shape
    nl = dproj.shape[1] // N_DEV
    tt = min(M, 2048)

    def body(a_ref, b_ref, o_ref):
        @pl.when(pl.program_id(1) == 0)
        def _():
            o_ref[...] = jnp.zeros_like(o_ref)

        o_ref[...] += _dot_tn(a_ref[...], b_ref[...])

    outs = _launch(
        body, name=name, grid=(N_DEV, M // tt),
        in_specs=[pl.BlockSpec((tt, Dm), lambda j, t: (t, 0)), pl.BlockSpec((tt, nl), lambda j, t: (t, j))],
        out_specs=[pl.BlockSpec((None, Dm, nl), lambda j, t: (_chunk_slot(j), 0, 0))],
        out_shape=[jax.ShapeDtypeStruct((N_DEV, Dm, nl), F32)],
        args=(h, dproj), sem=("parallel", "arbitrary"), comm=comm)
    return outs[0] if comm is None else outs


def _dw_out(y, dout, name, comm=None):
    M, K = y.shape
    Dm = dout.shape[1]
    kl = K // N_DEV
    tt = min(M, 1024)

    def body(a_ref, b_ref, o_ref):
        @pl.when(pl.program_id(0) == 0)
        def _():
            o_ref[...] = jnp.zeros_like(o_ref)

        b = b_ref[...]
        for j in range(N_DEV):
            o_ref[_chunk_slot(j)] += _dot_tn(a_ref[:, j * kl:(j + 1) * kl], b)

    outs = _launch(
        body, name=name, grid=(M // tt,),
        in_specs=[pl.BlockSpec((tt, K), lambda t: (t, 0)), pl.BlockSpec((tt, Dm), lambda t: (t, 0))],
        out_specs=[pl.BlockSpec((N_DEV, kl, Dm), lambda t: (0, 0, 0))],
        out_shape=[jax.ShapeDtypeStruct((N_DEV, kl, Dm), F32)],
        args=(y, dout), sem=("arbitrary",), comm=comm)
    return outs[0] if comm is None else outs


def _dh_norm_bwd(dproj, w_dm, x, gain, dres, name, comm=None):
    M, Dm = x.shape
    nd, _, nl = w_dm.shape
    tm = min(M, 1024)
    rows_bytes = tm * Dm * (4 + 2 * 4 + 2 * 4 + 2 * 4 + 2 * 2)
    block_bytes = 2 * (tm * nl + Dm * nl) * 2
    pair = 2 if rows_bytes + 2 * block_bytes <= VMEM_LIMIT - 8 * 1024 * 1024 else 1
    nj = nd // pair

    def body(dp_ref, w_ref, x_ref, g_ref, dr_ref, dx_ref, dxa_ref, dg_ref, acc_ref):
        i, j = pl.program_id(0), pl.program_id(1)

        @pl.when(j == 0)
        def _():
            acc_ref[...] = jnp.zeros_like(acc_ref)

        acc_ref[...] += functools.reduce(
            lambda a, b: a + b, [_dot_nt(dp_ref[:, d * nl:(d + 1) * nl], w_ref[d]) for d in range(pair)])

        @pl.when(j == nj - 1)
        def _():
            @pl.when(i == 0)
            def _():
                dg_ref[...] = jnp.zeros_like(dg_ref)

            dh = acc_ref[...]
            xv = x_ref[...]
            r = lax.rsqrt(jnp.mean(xv * xv, axis=-1, keepdims=True) + EPS)
            xn = xv * r
            dg_ref[...] += jnp.sum(dh * xn, axis=0, keepdims=True)
            dxn = dh * g_ref[...]
            dx = dr_ref[...] + r * (dxn - xn * jnp.mean(dxn * xn, axis=-1, keepdims=True))
            dx_ref[...] = dx
            dxa_ref[...] = dx.astype(dxa_ref.dtype)

    row = pl.BlockSpec((tm, Dm), lambda i, j: (i, 0))
    return _launch(
        body, name=name, grid=(M // tm, nj),
        in_specs=[pl.BlockSpec((tm, pair * nl), lambda i, j: (i, j)),
                  pl.BlockSpec((pair, Dm, nl), lambda i, j: (j, 0, 0)),
                  row, pl.BlockSpec((1, Dm), lambda i, j: (0, 0)), row],
        out_specs=[row, row, pl.BlockSpec((1, Dm), lambda i, j: (0, 0))],
        out_shape=[jax.ShapeDtypeStruct((M, Dm), F32), jax.ShapeDtypeStruct((M, Dm), ACT),
                   jax.ShapeDtypeStruct((1, Dm), F32)],
        scratch=[pltpu.VMEM((tm, Dm), F32)],
        args=(dproj, w_dm, x, gain, dres), sem=("arbitrary", "arbitrary"), comm=comm)


def _tril_mask():
    return lax.broadcasted_iota(jnp.int32, (CHUNK, CHUNK), 0) >= lax.broadcasted_iota(jnp.int32, (CHUNK, CHUNK), 1)


def _a_mid(proj, v_gain, w_s, b_st, name, comm=None):
    M = proj.shape[0]
    W = proj.shape[1] // 3
    gd = W // A_GROUPS
    tm = min(M, 256)

    def body(p_ref, vg_ref, ws_ref, bs_ref, y_ref):
        pv = p_ref[:, W:2 * W].astype(F32)
        r = lax.rsqrt(jnp.mean(pv * pv, axis=-1, keepdims=True) + EPS)
        v = (pv * r * vg_ref[...]).astype(MXU)
        tri = _tril_mask()
        for g in range(A_GROUPS):
            wg = jnp.where(tri, ws_ref[g], 0.0).astype(MXU)
            bcol = bs_ref[:, g:g + 1]
            for c in range(tm // CHUNK):
                rows, cols = slice(c * CHUNK, (c + 1) * CHUNK), slice(g * gd, (g + 1) * gd)
                mixed = jnp.dot(wg, v[rows, cols], preferred_element_type=F32) + bcol
                u = p_ref[rows, g * gd:(g + 1) * gd].astype(F32)
                z = p_ref[rows, 2 * W + g * gd:2 * W + (g + 1) * gd].astype(F32)
                y_ref[rows, cols] = (u * mixed * (z * _sigmoid(z))).astype(y_ref.dtype)

    return _launch(
        body, name=name, grid=(M // tm,),
        in_specs=[pl.BlockSpec((tm, 3 * W), lambda i: (i, 0)),
                  pl.BlockSpec((1, W), lambda i: (0, 0)),
                  pl.BlockSpec((A_GROUPS, CHUNK, CHUNK), lambda i: (0, 0, 0)),
                  pl.BlockSpec((CHUNK, A_GROUPS), lambda i: (0, 0))],
        out_specs=[pl.BlockSpec((tm, W), lambda i: (i, 0))],
        out_shape=[jax.ShapeDtypeStruct((M, W), ACT)],
        args=(proj, v_gain, w_s, b_st), sem=("parallel",), comm=comm)


def _a_bwd(dout, w_out, proj, v_gain, w_s, b_st, name, comm=None):
    M = proj.shape[0]
    W = proj.shape[1] // 3
    Dm = dout.shape[1]
    gd = W // A_GROUPS
    tm = min(M, 512)
    nt = M // tm

    def body(do_ref, wo_ref, p_ref, vg_ref, ws_ref, bs_ref, dp_ref, dws_ref, dbs_ref, dvg_ref, dv_s):
        i = pl.program_id(0)

        @pl.when(i == 0)
        def _():
            dws_ref[...] = jnp.zeros_like(dws_ref)
            dbs_ref[...] = jnp.zeros_like(dbs_ref)
            dvg_ref[...] = jnp.zeros_like(dvg_ref)

        dy = _dot_nt(do_ref[...], wo_ref[...])
        pv = p_ref[:, W:2 * W].astype(F32)
        r = lax.rsqrt(jnp.mean(pv * pv, axis=-1, keepdims=True) + EPS)
        pvn = pv * r
        vg = vg_ref[...]
        v = (pvn * vg).astype(MXU)
        tri = _tril_mask()
        for g in range(A_GROUPS):
            wf = jnp.where(tri, ws_ref[g], 0.0)
            wg = wf.astype(MXU)
            wgt = wf.T.astype(MXU)
            bcol = bs_ref[:, g:g + 1]
            for c in range(tm // CHUNK):
                rows, cols = slice(c * CHUNK, (c + 1) * CHUNK), slice(g * gd, (g + 1) * gd)
                vb = v[rows, cols]
                mixed = jnp.dot(wg, vb, preferred_element_type=F32) + bcol
                u = p_ref[rows, g * gd:(g + 1) * gd].astype(F32)
                z = p_ref[rows, 2 * W + g * gd:2 * W + (g + 1) * gd].astype(F32)
                sig = _sigmoid(z)
                sz = z * sig
                dyb = dy[rows, cols]
                dp_ref[rows, g * gd:(g + 1) * gd] = (dyb * mixed * sz).astype(dp_ref.dtype)
                dp_ref[rows, 2 * W + g * gd:2 * W + (g + 1) * gd] = (
                    dyb * u * mixed * (sig * (1.0 + z * (1.0 - sig)))).astype(dp_ref.dtype)
                dmix = dyb * u * sz
                dws_ref[g] += _dot_nt(dmix, vb)
                dbs_ref[:, g:g + 1] += jnp.sum(dmix, axis=1, keepdims=True)
                dv_s[rows, cols] = jnp.dot(wgt, dmix.astype(MXU), preferred_element_type=F32)
        dv = dv_s[...]
        dvg_ref[...] += jnp.sum(dv * pvn, axis=0, keepdims=True)
        dpvn = dv * vg
        dp_ref[:, W:2 * W] = (r * (dpvn - pvn * jnp.mean(dpvn * pvn, axis=-1, keepdims=True))).astype(dp_ref.dtype)

        @pl.when(i == nt - 1)
        def _():
            for g in range(A_GROUPS):
                dws_ref[g] = jnp.where(tri, dws_ref[g], 0.0)

    return _launch(
        body, name=name, grid=(nt,),
        in_specs=[pl.BlockSpec((tm, Dm), lambda i: (i, 0)),
                  pl.BlockSpec((W, Dm), lambda i: (0, 0)),
                  pl.BlockSpec((tm, 3 * W), lambda i: (i, 0)),
                  pl.BlockSpec((1, W), lambda i: (0, 0)),
                  pl.BlockSpec((A_GROUPS, CHUNK, CHUNK), lambda i: (0, 0, 0)),
                  pl.BlockSpec((CHUNK, A_GROUPS), lambda i: (0, 0))],
        out_specs=[pl.BlockSpec((tm, 3 * W), lambda i: (i, 0)),
                   pl.BlockSpec((A_GROUPS, CHUNK, CHUNK), lambda i: (0, 0, 0)),
                   pl.BlockSpec((CHUNK, A_GROUPS), lambda i: (0, 0)),
                   pl.BlockSpec((1, W), lambda i: (0, 0))],
        out_shape=[jax.ShapeDtypeStruct((M, 3 * W), ACT),
                   jax.ShapeDtypeStruct((A_GROUPS, CHUNK, CHUNK), F32),
                   jax.ShapeDtypeStruct((CHUNK, A_GROUPS), F32),
                   jax.ShapeDtypeStruct((1, W), F32)],
        scratch=[pltpu.VMEM((tm, W), F32)],
        args=(dout, w_out, proj, v_gain, w_s, b_st), sem=("arbitrary",), comm=comm)


def _pool_diff(xg, tail, i, tm, w):
    t = lax.broadcasted_iota(jnp.int32, (tm, tm + POOL_HALO), 0)
    s = lax.broadcasted_iota(jnp.int32, (tm, tm + POOL_HALO), 1)
    off = t - (s - POOL_HALO)
    band = jnp.where((off >= 0) & (off < w), 1.0, 0.0).astype(MXU)
    tail = jnp.where(i > 0, tail, jnp.zeros_like(tail))
    ext = jnp.concatenate([tail, xg], axis=0)
    ssum = jnp.dot(band, ext.astype(MXU), preferred_element_type=F32)
    tglob = i * tm + lax.broadcasted_iota(jnp.int32, (tm, 1), 0)
    cnt = jnp.minimum(tglob + 1, w).astype(F32)
    return ssum / cnt - xg.astype(F32)


def _c_mid(proj, w_grp, scale, name):
    M = proj.shape[0]
    W = proj.shape[1] // 2
    ng = len(POOL_SIZES)
    cg = W // ng
    tm = min(M, 256)
    hb = tm // POOL_HALO

    def body(xc_ref, tail_ref, z_ref, wg_ref, sc_ref, y_ref):
        i = pl.program_id(0)
        for g, w in enumerate(POOL_SIZES):
            cols = slice(g * cg, (g + 1) * cg)
            d = _pool_diff(xc_ref[:, cols], tail_ref[:, cols], i, tm, w)
            mixed = _dot(d, wg_ref[g]) * sc_ref[:, cols]
            z = z_ref[:, cols].astype(F32)
            y_ref[:, cols] = (mixed * (z * _sigmoid(z))).astype(y_ref.dtype)

    return pl.pallas_call(
        body, name=name, grid=(M // tm,),
        in_specs=[pl.BlockSpec((tm, W), lambda i: (i, 0)),
                  pl.BlockSpec((POOL_HALO, W), lambda i: (jnp.maximum(i * hb - 1, 0), 0)),
                  pl.BlockSpec((tm, W), lambda i: (i, 1)),
                  pl.BlockSpec((ng, cg, cg), lambda i: (0, 0, 0)),
                  pl.BlockSpec((1, W), lambda i: (0, 0))],
        out_specs=pl.BlockSpec((tm, W), lambda i: (i, 0)),
        out_shape=jax.ShapeDtypeStruct((M, W), ACT),
        compiler_params=_cp("parallel"),
    )(proj, proj, proj, w_grp, scale)


def _c_bwd1(dout, w_out, proj, w_grp, scale, name, comm=None):
    M = proj.shape[0]
    W = proj.shape[1] // 2
    Dm = dout.shape[1]
    ng = len(POOL_SIZES)
    cg = W // ng
    rl = cg // N_DEV
    tm = min(M, 256)
    hb = tm // POOL_HALO
    nt = M // tm

    def body(do_ref, wo_ref, xc_ref, tail_ref, z_ref, wg_ref, sc_ref, dd_ref, dz_ref, dwg_ref, dsc_ref, acc_ref):
        i = pl.program_id(0)

        @pl.when(i == 0)
        def _():
            acc_ref[...] = jnp.zeros_like(acc_ref)
            dsc_ref[...] = jnp.zeros_like(dsc_ref)

        dy = _dot_nt(do_ref[...], wo_ref[...])
        for g, w in enumerate(POOL_SIZES):
            cols = slice(g * cg, (g + 1) * cg)
            d = _pool_diff(xc_ref[:, cols], tail_ref[:, cols], i, tm, w)
            mr = _dot(d, wg_ref[g])
            sc = sc_ref[:, cols]
            z = z_ref[:, cols].astype(F32)
            sig = _sigmoid(z)
            dyg = dy[:, cols]
            dmixed = dyg * (z * sig)
            dz_ref[:, cols] = (dyg * (mr * sc) * (sig * (1.0 + z * (1.0 - sig)))).astype(dz_ref.dtype)
            dsc_ref[:, cols] += jnp.sum(dmixed * mr, axis=0, keepdims=True)
            dmr = (dmixed * sc).astype(MXU)
            acc_ref[g] += _dot_tn(d, dmr)
            dd_ref[:, cols] = _dot_nt(dmr, wg_ref[g]).astype(dd_ref.dtype)

        @pl.when(i == nt - 1)
        def _():
            for dev in range(N_DEV):
                for g in range(ng):
                    dwg_ref[_chunk_slot(dev), g] = acc_ref[g, dev * rl:(dev + 1) * rl, :]

    return _launch(
        body, name=name, grid=(nt,),
        in_specs=[pl.BlockSpec((tm, Dm), lambda i: (i, 0)),
                  pl.BlockSpec((W, Dm), lambda i: (0, 0)),
                  pl.BlockSpec((tm, W), lambda i: (i, 0)),
                  pl.BlockSpec((POOL_HALO, W), lambda i: (jnp.maximum(i * hb - 1, 0), 0)),
                  pl.BlockSpec((tm, W), lambda i: (i, 1)),
                  pl.BlockSpec((ng, cg, cg), lambda i: (0, 0, 0)),
                  pl.BlockSpec((1, W), lambda i: (0, 0))],
        out_specs=[pl.BlockSpec((tm, W), lambda i: (i, 0)),
                   pl.BlockSpec((tm, W), lambda i: (i, 0)),
                   pl.BlockSpec((N_DEV, ng, rl, cg), lambda i: (0, 0, 0, 0)),
                   pl.BlockSpec((1, W), lambda i: (0, 0))],
        out_shape=[jax.ShapeDtypeStruct((M, W), ACT), jax.ShapeDtypeStruct((M, W), ACT),
                   jax.ShapeDtypeStruct((N_DEV, ng, rl, cg), F32), jax.ShapeDtypeStruct((1, W), F32)],
        scratch=[pltpu.VMEM((ng, cg, cg), F32)],
        args=(dout, w_out, proj, proj, proj, w_grp, scale), sem=("arbitrary",), comm=comm)


def _c_bwd2(dd, dz, name):
    M, W = dd.shape
    ng = len(POOL_SIZES)
    cg = W // ng
    tm = min(M, 256)
    hb = tm // POOL_HALO
    nt = M // tm

    def body(dd_ref, head_ref, dz_ref, dp_ref):
        i = pl.program_id(0)
        s = lax.broadcasted_iota(jnp.int32, (tm, tm + POOL_HALO), 0)
        t = lax.broadcasted_iota(jnp.int32, (tm, tm + POOL_HALO), 1)
        off = t - s
        tglob = i * tm + lax.broadcasted_iota(jnp.int32, (tm + POOL_HALO, 1), 0)
        for g, w in enumerate(POOL_SIZES):
            cols = slice(g * cg, (g + 1) * cg)
            ddg = dd_ref[:, cols].astype(F32)
            head = head_ref[:, cols].astype(F32)
            head = jnp.where(i < nt - 1, head, jnp.zeros_like(head))
            cnt = jnp.minimum(tglob + 1, w).astype(F32)
            ext = (jnp.concatenate([ddg, head], axis=0) / cnt).astype(MXU)
            band = jnp.where((off >= 0) & (off < w), 1.0, 0.0).astype(MXU)
            dp_ref[:, cols] = (jnp.dot(band, ext, preferred_element_type=F32) - ddg).astype(dp_ref.dtype)
        dp_ref[:, W:] = dz_ref[...]

    return pl.pallas_call(
        body, name=name, grid=(nt,),
        in_specs=[pl.BlockSpec((tm, W), lambda i: (i, 0)),
                  pl.BlockSpec((POOL_HALO, W), lambda i: (jnp.minimum((i + 1) * hb, M // POOL_HALO - 1), 0)),
                  pl.BlockSpec((tm, W), lambda i: (i, 0))],
        out_specs=pl.BlockSpec((tm, 2 * W), lambda i: (i, 0)),
        out_shape=jax.ShapeDtypeStruct((M, 2 * W), ACT),
        compiler_params=_cp("parallel"),
    )(dd, dd, dz)


def _rope_tables(S):
    half = ROPE_DIM // 2
    inv_freq = jnp.power(jnp.float32(ROPE_THETA), -jnp.arange(half, dtype=F32) / half)
    ang = jnp.arange(S, dtype=F32)[:, None] * inv_freq[None, :]
    cos, sin = jnp.cos(ang), jnp.sin(ang)
    rest = HEAD_DIM - ROPE_DIM
    cf = jnp.concatenate([cos, cos, jnp.ones((S, rest), F32)], axis=1)
    sf = jnp.concatenate([-sin, sin, jnp.zeros((S, rest), F32)], axis=1)
    return cf, sf


def _swap_matrix():
    half = ROPE_DIM // 2
    a = lax.broadcasted_iota(jnp.int32, (HEAD_DIM, HEAD_DIM), 0)
    e = lax.broadcasted_iota(jnp.int32, (HEAD_DIM, HEAD_DIM), 1)
    hit = ((e < half) & (a == e + half)) | ((e >= half) & (e < 2 * half) & (a == e - half))
    return jnp.where(hit, 1.0, 0.0).astype(MXU)


def _b_qk_fwd(proj, tables, gains, name, comm=None):
    M = proj.shape[0]
    nsl = 2 * len(B_DILATIONS) * B_HEADS
    Wqk = nsl * HEAD_DIM
    tm = min(M, 256)

    def body(p_ref, cf_ref, sf_ref, g_ref, o_ref):
        cf, sf = cf_ref[...], sf_ref[...]
        swap = _swap_matrix()
        for j in range(nsl):
            cols = slice(j * HEAD_DIM, (j + 1) * HEAD_DIM)
            xv = p_ref[:, cols].astype(F32)
            r = lax.rsqrt(jnp.mean(xv * xv, axis=-1, keepdims=True) + EPS)
            xg = xv * g_ref[j // B_HEADS:j // B_HEADS + 1, :]
            hi = xg.astype(MXU)
            lo = (xg - hi.astype(F32)).astype(MXU)
            sw = jnp.dot(hi, swap, preferred_element_type=F32) + jnp.dot(lo, swap, preferred_element_type=F32)
            o_ref[:, cols] = (r * (xg * cf + sw * sf)).astype(o_ref.dtype)

    tspec = pl.BlockSpec((tm, HEAD_DIM), lambda i: (i, 0))
    return _launch(
        body, name=name, grid=(M // tm,),
        in_specs=[pl.BlockSpec((tm, Wqk), lambda i: (i, 0)), tspec, tspec,
                  pl.BlockSpec((8, HEAD_DIM), lambda i: (0, 0))],
        out_specs=[pl.BlockSpec((tm, Wqk), lambda i: (i, 0))],
        out_shape=[jax.ShapeDtypeStruct((M, Wqk), ACT)],
        args=(proj, *tables, gains), sem=("parallel",), comm=comm)


def _b_qk_bwd(dqs, dks, proj, tables, gains, dproj, name):
    M = proj.shape[0]
    ngr = len(B_DILATIONS)
    nsl = 2 * ngr * B_HEADS
    Wqk = nsl * HEAD_DIM
    Wg = B_HEADS * HEAD_DIM
    tm = min(M, 512)

    def body(*refs):
        d_refs = refs[:2 * ngr]
        p_ref, cf_ref, sf_ref, g_ref = refs[2 * ngr:2 * ngr + 4]
        dp_ref, dg_ref = refs[-2], refs[-1]

        @pl.when(pl.program_id(0) == 0)
        def _():
            dg_ref[...] = jnp.zeros_like(dg_ref)

        cf, sf = cf_ref[...], sf_ref[...]
        swap = _swap_matrix()
        for j in range(nsl):
            t, hh = j // B_HEADS, j % B_HEADS
            cols = slice(j * HEAD_DIM, (j + 1) * HEAD_DIM)
            dy = d_refs[t][:, hh * HEAD_DIM:(hh + 1) * HEAD_DIM].astype(F32)
            dxn = dy * cf + jnp.dot((dy * sf).astype(MXU), swap, preferred_element_type=F32)
            xv = p_ref[:, cols].astype(F32)
            r = lax.rsqrt(jnp.mean(xv * xv, axis=-1, keepdims=True) + EPS)
            xh = xv * r
            dg_ref[t:t + 1, :] += jnp.sum(dxn * xh, axis=0, keepdims=True)
            dxh = dxn * g_ref[t:t + 1, :]
            dp_ref[:, cols] = (r * (dxh - xh * jnp.mean(dxh * xh, axis=-1, keepdims=True))).astype(dp_ref.dtype)

    tspec = pl.BlockSpec((tm, HEAD_DIM), lambda i: (i, 0))
    dspec = pl.BlockSpec((tm, Wg), lambda i: (i, 0))
    n_in = 2 * ngr + 5
    return pl.pallas_call(
        body, name=name, grid=(M // tm,),
        in_specs=[dspec] * (2 * ngr) + [pl.BlockSpec((tm, Wqk), lambda i: (i, 0)), tspec, tspec,
                                        pl.BlockSpec((8, HEAD_DIM), lambda i: (0, 0)),
                                        pl.BlockSpec(memory_space=pl.ANY)],
        out_specs=[pl.BlockSpec((tm, Wqk), lambda i: (i, 0)), pl.BlockSpec((8, HEAD_DIM), lambda i: (0, 0))],
        out_shape=[jax.ShapeDtypeStruct(dproj.shape, dproj.dtype), jax.ShapeDtypeStruct((8, HEAD_DIM), F32)],
        input_output_aliases={n_in - 1: 0},
        compiler_params=_cp("arbitrary"),
    )(*dqs, *dks, proj, *tables, gains, dproj)


def _attn_tile(D, M):
    return max(HEAD_DIM * D, min(M, 2048))


class _TokenRows:
    GROUP = 16

    def __init__(self, D):
        self.D = D
        self.pitch = 24 if D == 16 else self.GROUP
        self.operand_dtype = F32 if D > 1 else ACT

    def rows(self, ntok):
        return ntok // self.GROUP * self.pitch

    def every_dth(self, tok0, n):
        start = tok0 // self.GROUP * self.pitch + tok0 % self.GROUP
        stride = self.D * self.pitch // self.GROUP
        return pl.ds(start, n) if stride == 1 else pl.ds(start, n, stride=stride)

    def put(self, dst, tok0, src_ref, ntok):
        if self.pitch == self.GROUP:
            dst[tok0:tok0 + ntok, :] = src_ref[...].astype(dst.dtype)
            return

        def group(i, carry):
            row = pl.multiple_of((tok0 // self.GROUP + i) * self.pitch, 8)
            dst[pl.ds(row, self.GROUP), :] = src_ref[pl.ds(pl.multiple_of(i * self.GROUP, self.GROUP), self.GROUP), :].astype(F32)
            return carry

        lax.fori_loop(0, ntok // self.GROUP, group, 0, unroll=8)

    def get(self, dst_ref, src, ntok):
        if self.pitch == self.GROUP:
            dst_ref[...] = src[0:ntok, :].astype(dst_ref.dtype)
            return

        def group(i, carry):
            row = pl.multiple_of(i * self.pitch, 8)
            dst_ref[pl.ds(pl.multiple_of(i * self.GROUP, self.GROUP), self.GROUP), :] = src[pl.ds(row, self.GROUP), :].astype(dst_ref.dtype)
            return carry

        lax.fori_loop(0, ntok // self.GROUP, group, 0, unroll=8)


def _attn_mask(base):
    qi = lax.broadcasted_iota(jnp.int32, (CHUNK, 2 * CHUNK), 0)
    ki = lax.broadcasted_iota(jnp.int32, (CHUNK, 2 * CHUNK), 1)
    return (ki >= qi) & (ki <= qi + CHUNK) & (ki >= CHUNK - base)


def _b_attn_fwd(qk, proj, g, name):
    M = qk.shape[0]
    D = B_DILATIONS[g]
    ngr = len(B_DILATIONS)
    T = _attn_tile(D, M)
    P = HEAD_DIM * D
    nsb = T // P
    Wg = B_HEADS * HEAD_DIM
    scale = np.float32(1.0 / np.sqrt(HEAD_DIM))

    lay = _TokenRows(D)
    RP, RT = lay.rows(P), lay.rows(T)

    def body(q_ref, k_ref, v_ref, o_ref, l_ref, qs, ks, vs, os_):
        n = pl.program_id(1)

        @pl.when(n == 0)
        def _():
            ks[0:RP, :] = jnp.zeros((RP, HEAD_DIM), ks.dtype)
            vs[0:RP, :] = jnp.zeros((RP, HEAD_DIM), vs.dtype)

        lay.put(qs, 0, q_ref, T)
        lay.put(ks, P, k_ref, T)
        lay.put(vs, P, v_ref, T)

        for b in range(nsb):
            mask = _attn_mask(n * (T // D) + b * CHUNK)
            for r in range(D):
                start = b * P + r
                q = qs[lay.every_dth(start, CHUNK), :]
                k = ks[lay.every_dth(start, 2 * CHUNK), :]
                v = vs[lay.every_dth(start, 2 * CHUNK), :]
                s = jnp.where(mask, _dot_nt(q, k) * scale, NEG)
                m = jnp.max(s, axis=-1, keepdims=True)
                p = jnp.exp(s - m)
                l = jnp.sum(p, axis=-1, keepdims=True)
                o = _dot(p, v) / l
                os_[lay.every_dth(start, CHUNK), :] = o
                l_ref[:, b * D + r:b * D + r + 1] = m + jnp.log(l)

        lay.get(o_ref, os_, T)
        ks[0:RP, :] = ks[RT:RT + RP, :]
        vs[0:RP, :] = vs[RT:RT + RP, :]

    blk = (T, HEAD_DIM)
    U = nsb * D
    return pl.pallas_call(
        body, name=name, grid=(B_HEADS, M // T),
        in_specs=[pl.BlockSpec(blk, lambda h, n: (n, g * B_HEADS + h)),
                  pl.BlockSpec(blk, lambda h, n: (n, (ngr + g) * B_HEADS + h)),
                  pl.BlockSpec(blk, lambda h, n: (n, (2 * ngr + g) * B_HEADS + h))],
        out_specs=[pl.BlockSpec(blk, lambda h, n: (n, h)), pl.BlockSpec((None, CHUNK, U), lambda h, n: (h, n, 0))],
        out_shape=[jax.ShapeDtypeStruct((M, Wg), ACT), jax.ShapeDtypeStruct((B_HEADS, (M // T) * CHUNK, U), F32)],
        scratch_shapes=[pltpu.VMEM((RT, HEAD_DIM), lay.operand_dtype), pltpu.VMEM((RP + RT, HEAD_DIM), lay.operand_dtype),
                        pltpu.VMEM((RP + RT, HEAD_DIM), lay.operand_dtype), pltpu.VMEM((RT, HEAD_DIM), F32)],
        compiler_params=_cp("parallel", "arbitrary"),
    )(qk, qk, proj)


def _units_to_tokens(a, D, T):
    H = a.shape[0]
    nsb = T // (HEAD_DIM * D)
    return a.reshape(H, -1, CHUNK, nsb, D).transpose(1, 3, 2, 4, 0).reshape(-1, H)


def _tokens_to_units(a, D, T):
    M, H = a.shape
    nsb = T // (HEAD_DIM * D)
    return a.reshape(M // T, nsb, CHUNK, D, H).transpose(4, 0, 2, 1, 3).reshape(H, (M // T) * CHUNK, nsb * D)


def _b_combine(os_, ls, proj, name):
    M, Wg = os_[0].shape
    ngr = len(B_DILATIONS)
    tm = min(M, 512)

    def body(*refs):
        o_refs, l_refs, z_ref = refs[:ngr], refs[ngr:2 * ngr], refs[2 * ngr]
        y_ref, o_ref, lse_ref = refs[2 * ngr + 1:]
        for h in range(B_HEADS):
            cols = slice(h * HEAD_DIM, (h + 1) * HEAD_DIM)
            ls_ = [r[:, h:h + 1] for r in l_refs]
            m = functools.reduce(jnp.maximum, ls_)
            es = [jnp.exp(l - m) for l in ls_]
            tot = functools.reduce(lambda a, b: a + b, es)
            o = functools.reduce(lambda a, b: a + b, [(e / tot) * r[:, cols].astype(F32) for e, r in zip(es, o_refs)])
            z = z_ref[:, cols].astype(F32)
            y_ref[:, cols] = (o * (z * _sigmoid(z))).astype(y_ref.dtype)
            o_ref[:, cols] = o.astype(o_ref.dtype)
            lse_ref[:, h:h + 1] = m + jnp.log(tot)

    spec = pl.BlockSpec((tm, Wg), lambda i: (i, 0))
    hspec = pl.BlockSpec((tm, B_HEADS), lambda i: (i, 0))
    return pl.pallas_call(
        body, name=name, grid=(M // tm,),
        in_specs=[spec] * ngr + [hspec] * ngr + [pl.BlockSpec((tm, Wg), lambda i: (i, 3 * ngr))],
        out_specs=[spec, spec, hspec],
        out_shape=[jax.ShapeDtypeStruct((M, Wg), ACT), jax.ShapeDtypeStruct((M, Wg), ACT),
                   jax.ShapeDtypeStruct((M, B_HEADS), F32)],
        compiler_params=_cp("parallel"),
    )(*os_, *ls, proj)


def _b_bwd_pre(dout, w_out, o, proj, name):
    M, Wg = o.shape
    Dm = dout.shape[1]
    ngr = len(B_DILATIONS)
    tm = min(M, 512)

    def body(do_ref, wo_ref, o_ref, z_ref, dov_ref, dl_ref, dp_ref):
        dy = _dot_nt(do_ref[...], wo_ref[...])
        z = z_ref[...].astype(F32)
        sig = _sigmoid(z)
        ov = o_ref[...].astype(F32)
        dp_ref[...] = (dy * ov * (sig * (1.0 + z * (1.0 - sig)))).astype(dp_ref.dtype)
        dov = dy * (z * sig)
        dov_ref[...] = dov.astype(dov_ref.dtype)
        prod = dov * ov
        for h in range(B_HEADS):
            dl_ref[:, h:h + 1] = jnp.sum(prod[:, h * HEAD_DIM:(h + 1) * HEAD_DIM], axis=-1, keepdims=True)

    spec = pl.BlockSpec((tm, Wg), lambda i: (i, 0))
    zspec = pl.BlockSpec((tm, Wg), lambda i: (i, 3 * ngr))
    return pl.pallas_call(
        body, name=name, grid=(M // tm,),
        in_specs=[pl.BlockSpec((tm, Dm), lambda i: (i, 0)), pl.BlockSpec((Wg, Dm), lambda i: (0, 0)), spec, zspec],
        out_specs=[spec, pl.BlockSpec((tm, B_HEADS), lambda i: (i, 0)), zspec],
        out_shape=[jax.ShapeDtypeStruct((M, Wg), ACT), jax.ShapeDtypeStruct((M, B_HEADS), F32),
                   jax.ShapeDtypeStruct(proj.shape, ACT)],
        compiler_params=_cp("parallel"),
    )(dout, w_out, o, proj)


def _b_attn_bwd(qk, proj, dov, lse, delta, dproj, g, name, comm=None):
    M = qk.shape[0]
    D = B_DILATIONS[g]
    ngr = len(B_DILATIONS)
    T = _attn_tile(D, M)
    P = HEAD_DIM * D
    nsb = T // P
    nt = M // T
    Wg = B_HEADS * HEAD_DIM
    scale = np.float32(1.0 / np.sqrt(HEAD_DIM))
    shift = T - P
    lay = _TokenRows(D)
    RP, RT = lay.rows(P), lay.rows(T)

    def body(q_ref, k_ref, v_ref, do_ref, l_ref, dl_ref, dp_any, dq_ref, dk_ref, dv_ref,
             qs, dos, ks, vs, dqs, dks, dvs):
        n = pl.program_id(1)

        @pl.when(n == 0)
        def _():
            ks[0:RP, :] = jnp.zeros((RP, HEAD_DIM), ks.dtype)
            vs[0:RP, :] = jnp.zeros((RP, HEAD_DIM), vs.dtype)
            dks[...] = jnp.zeros((2 * RT, HEAD_DIM), F32)
            dvs[...] = jnp.zeros((2 * RT, HEAD_DIM), F32)

        @pl.when(n < nt)
        def _():
            lay.put(qs, 0, q_ref, T)
            lay.put(dos, 0, do_ref, T)
            lay.put(ks, P, k_ref, T)
            lay.put(vs, P, v_ref, T)

            masks = [_attn_mask(n * (T // D) + b * CHUNK) for b in range(nsb)]
            for r in range(D):
                carry_dv = carry_dk = None
                for b in range(nsb):
                    start = b * P + r
                    qsl = lay.every_dth(start, CHUNK)
                    ksl = lay.every_dth(start, 2 * CHUNK)
                    lo = lay.every_dth(start + shift, CHUNK)
                    q = qs[qsl, :]
                    do = dos[qsl, :]
                    k = ks[ksl, :]
                    v = vs[ksl, :]
                    s = _dot_nt(q, k) * scale
                    u = b * D + r
                    p = jnp.where(masks[b], jnp.exp(s - l_ref[:, u:u + 1]), 0.0)
                    dv = _dot_tn(p, do)
                    dp = _dot_nt(do, v)
                    ds = (p * (dp - dl_ref[:, u:u + 1]) * scale).astype(MXU)
                    dqs[qsl, :] = _dot(ds, k)
                    dk = _dot_tn(ds, q)
                    if b == 0:
                        dvs[lo, :] += dv[:CHUNK]
                        dks[lo, :] += dk[:CHUNK]
                    else:
                        dvs[lo, :] = carry_dv + dv[:CHUNK]
                        dks[lo, :] = carry_dk + dk[:CHUNK]
                    carry_dv, carry_dk = dv[CHUNK:], dk[CHUNK:]
                hi = lay.every_dth((nsb - 1) * P + r + shift + P, CHUNK)
                dvs[hi, :] = carry_dv
                dks[hi, :] = carry_dk

        lay.get(dq_ref, dqs, T)
        lay.get(dk_ref, dks, T)
        lay.get(dv_ref, dvs, T)
        dks[0:RT, :] = dks[RT:2 * RT, :]
        dvs[0:RT, :] = dvs[RT:2 * RT, :]
        ks[0:RP, :] = ks[RT:RT + RP, :]
        vs[0:RP, :] = vs[RT:RT + RP, :]

    blk = (T, HEAD_DIM)
    cur = lambda n: jnp.minimum(n, nt - 1)
    prv = lambda n: jnp.maximum(n - 1, 0)
    return _launch(
        body, name=name, grid=(B_HEADS, nt + 1),
        in_specs=[pl.BlockSpec(blk, lambda h, n: (cur(n), g * B_HEADS + h)),
                  pl.BlockSpec(blk, lambda h, n: (cur(n), (ngr + g) * B_HEADS + h)),
                  pl.BlockSpec(blk, lambda h, n: (cur(n), (2 * ngr + g) * B_HEADS + h)),
                  pl.BlockSpec(blk, lambda h, n: (cur(n), h)),
                  pl.BlockSpec((None, CHUNK, nsb * D), lambda h, n: (h, cur(n), 0)),
                  pl.BlockSpec((None, CHUNK, nsb * D), lambda h, n: (h, cur(n), 0)),
                  pl.BlockSpec(memory_space=pl.ANY)],
        out_specs=[pl.BlockSpec(blk, lambda h, n: (cur(n), h)),
                   pl.BlockSpec(blk, lambda h, n: (prv(n), h)),
                   pl.BlockSpec(blk, lambda h, n: (prv(n), (2 * ngr + g) * B_HEADS + h))],
        out_shape=[jax.ShapeDtypeStruct((M, Wg), ACT), jax.ShapeDtypeStruct((M, Wg), ACT),
                   jax.ShapeDtypeStruct(dproj.shape, dproj.dtype)],
        scratch=[pltpu.VMEM((RT, HEAD_DIM), lay.operand_dtype)] * 2
        + [pltpu.VMEM((RP + RT, HEAD_DIM), lay.operand_dtype)] * 2
        + [pltpu.VMEM((RT, HEAD_DIM), F32)]
        + [pltpu.VMEM((2 * RT, HEAD_DIM), F32)] * 2,
        aliases={6: 2},
        args=(qk, qk, proj, dov, lse, delta, dproj), sem=("parallel", "arbitrary"), comm=comm)


def _coords():
    return lax.axis_index("x"), lax.axis_index("y"), lax.axis_index("c")


def _gather_blocks(x_refs, out_refs, send_sems, recv_sems, local_sems):
    x, y, c = _coords()
    me, sibling = (x, y, c), (x, y, 1 - c)
    chips = [(1 - x, y), (x, 1 - y), (1 - x, 1 - y)]
    arrays = range(len(x_refs))

    def slot(a, px, py, pc):
        return out_refs[a].at[4 * px + 2 * py + pc]

    def copy(a, k, block, to, src=None):
        return _remote(slot(a, *block) if src is None else src, slot(a, *block), send_sems, recv_sems, 7 * a + k, to)

    mine = [pltpu.make_async_copy(x_refs[a], slot(a, *me), local_sems.at[a]) for a in arrays]
    first = [copy(a, 0, me, sibling, src=x_refs[a]) for a in arrays]
    first += [copy(a, 1 + j, me, (*chip, c), src=x_refs[a]) for j, chip in enumerate(chips) for a in arrays]
    for cp in mine + first:
        cp.start()
    passed = []
    for j, chip in enumerate(chips):
        for a in arrays:
            copy(a, 1 + j, (*chip, c), me).wait_recv()
            passed.append(copy(a, 4 + j, (*chip, c), sibling))
            passed[-1].start()
    for a in arrays:
        copy(a, 0, sibling, me).wait_recv()
        for j, chip in enumerate(chips):
            copy(a, 4 + j, (*chip, 1 - c), me).wait_recv()
    for cp in first + passed:
        cp.wait_send()
    for cp in mine:
        cp.wait()


def _all_gather_hbm(arrays, name):
    n = len(arrays)

    def body(*refs):
        _gather_blocks(refs[:n], refs[n:2 * n], *refs[2 * n:])

    return pl.pallas_call(
        body, name=name, in_specs=[_HBM] * n, out_specs=[_HBM] * n,
        out_shape=[jax.ShapeDtypeStruct((N_DEV,) + a.shape, a.dtype) for a in arrays],
        scratch_shapes=[pltpu.SemaphoreType.DMA((7 * n,)), pltpu.SemaphoreType.DMA((7 * n,)),
                        pltpu.SemaphoreType.DMA((n,))],
    )(*arrays)


def _all_reduce_small(part):
    R, C = part.shape

    def body(x_ref, tot_ref, gath, send_sems, recv_sems, local_sems):
        _gather_blocks([x_ref], [gath], send_sems, recv_sems, local_sems)
        acc = gath[0]
        for d in range(1, N_DEV):
            acc = acc + gath[d]
        tot_ref[...] = acc

    return pl.pallas_call(
        body, name="ar_small",
        in_specs=[pl.BlockSpec(memory_space=pltpu.VMEM)],
        out_specs=pl.BlockSpec(memory_space=pltpu.VMEM),
        out_shape=jax.ShapeDtypeStruct((R, C), F32),
        scratch_shapes=[pltpu.VMEM((N_DEV, R, C), F32),
                        pltpu.SemaphoreType.DMA((7,)), pltpu.SemaphoreType.DMA((7,)), pltpu.SemaphoreType.DMA((1,))],
        compiler_params=pltpu.CompilerParams(vmem_limit_bytes=VMEM_LIMIT),
    )(part)


def _sum_blocks(gath, name):
    _, R, C = gath.shape

    def body(g_ref, o_ref):
        acc = g_ref[0]
        for d in range(1, N_DEV):
            acc = acc + g_ref[d]
        o_ref[...] = acc

    return pl.pallas_call(
        body, name=name,
        in_specs=[pl.BlockSpec(memory_space=pltpu.VMEM)], out_specs=pl.BlockSpec(memory_space=pltpu.VMEM),
        out_shape=jax.ShapeDtypeStruct((R, C), F32),
        compiler_params=pltpu.CompilerParams(vmem_limit_bytes=VMEM_LIMIT),
    )(gath)


def _remote(src, dst, send_sems, recv_sems, k, peer):
    return pltpu.make_async_remote_copy(src_ref=src, dst_ref=dst, send_sem=send_sems.at[k], recv_sem=recv_sems.at[k],
                                        device_id=peer, device_id_type=MESH)


def _ag_send(arrays):
    n = len(arrays)

    def make(c_in, c_out, send_sems, recv_sems, local_sems):
        x, y, c = _coords()
        peers = [(x, y, 1 - c), (1 - x, y, c), (x, 1 - y, c), (1 - x, 1 - y, c)]
        cps = []
        for a in range(n):
            src, dst = c_in[a], c_out[a].at[4 * x + 2 * y + c]
            cps.append(pltpu.make_async_copy(src, dst, local_sems.at[a]))
            cps += [_remote(src, dst, send_sems, recv_sems, 4 * a + k, peer) for k, peer in enumerate(peers)]
        return cps

    return _Comm(arrays, [jax.ShapeDtypeStruct((N_DEV,) + a.shape, a.dtype) for a in arrays], 4 * n, make, n_local=n)


def _ag_forward(gaths):
    n = len(gaths)

    def make(c_in, c_out, send_sems, recv_sems, local_sems):
        x, y, c = _coords()
        chips = [(1 - x, y), (x, 1 - y), (1 - x, 1 - y)]
        cps = []
        for a in range(n):
            buf = c_out[a]
            cps += [_remote(buf.at[4 * px + 2 * py + c], buf.at[4 * px + 2 * py + c], send_sems, recv_sems, 3 * a + j,
                            (x, y, 1 - c)) for j, (px, py) in enumerate(chips)]
        return cps

    return _Comm(gaths, [jax.ShapeDtypeStruct(g.shape, g.dtype) for g in gaths], 3 * n, make,
                 aliases={a: a for a in range(n)})


def _rs_sibling(grads):
    n = len(grads)

    def make(c_in, c_out, send_sems, recv_sems, local_sem):
        x, y, c = _coords()
        return [_remote(c_in[a].at[pl.ds(4 * (1 - c), 4)], c_out[a], send_sems, recv_sems, a, (x, y, 1 - c))
                for a in range(n)]

    return _Comm(grads, [jax.ShapeDtypeStruct((4,) + g.shape[1:], g.dtype) for g in grads], n, make)


def _rs_chips(parts):
    n = len(parts)

    def make(c_in, c_out, send_sems, recv_sems, local_sem):
        x, y, c = _coords()
        peers = [(x, 1 - y, c), (1 - x, y, c), (1 - x, 1 - y, c)]
        return [_remote(c_in[a].at[k], c_out[a].at[k], send_sems, recv_sems, 3 * a + k, peer)
                for a in range(n) for k, peer in enumerate(peers)]

    return _Comm(parts, [jax.ShapeDtypeStruct(p.shape, p.dtype) for p in parts], 3 * n, make)


def _row_tile(rows, cols):
    tr = min(rows, 1 << int(np.log2((1 << 18) // cols)))
    assert rows % tr == 0
    return tr


def _chip_partials(coords, g, r1, name):
    _, rows, C = g.shape
    tr = _row_tile(rows, C)

    def body(co_ref, g_ref, r_ref, o_ref):
        o_ref[...] = (g_ref[...] + r_ref[...]).astype(o_ref.dtype)

    def chip(k, co):
        return jnp.bitwise_xor(2 * co[0] + co[1], k + 1)

    return pl.pallas_call(
        body, name=name,
        grid_spec=pltpu.PrefetchScalarGridSpec(
            num_scalar_prefetch=1, grid=(3, rows // tr),
            in_specs=[pl.BlockSpec((None, tr, C), lambda k, t, co: (4 * co[2] + chip(k, co), t, 0)),
                      pl.BlockSpec((None, tr, C), lambda k, t, co: (chip(k, co), t, 0))],
            out_specs=pl.BlockSpec((None, tr, C), lambda k, t, co: (k, t, 0))),
        out_shape=jax.ShapeDtypeStruct((3, rows, C), WIRE),
        compiler_params=_cp("parallel", "parallel"),
    )(coords, g, r1)


def _adam_math(w, g, m, v):
    m = ADAM_B1 * m + (1.0 - ADAM_B1) * g
    v = ADAM_B2 * v + (1.0 - ADAM_B2) * (g * g)
    m_hat = m / (1.0 - ADAM_B1 ** ADAM_STEP)
    v_hat = v / (1.0 - ADAM_B2 ** ADAM_STEP)
    delta = -ADAM_LR * (m_hat / (jnp.sqrt(v_hat) + ADAM_EPS) + ADAM_WD * w)
    return delta, m, v


def _adamw_sharded(coords, w, m, v, g, r1, r2, layer, prev, name):
    L, rows, C = w.shape
    tr = _row_tile(rows, C)
    n_prev = 0 if prev is None else len(prev)

    def body(co_ref, w_ref, m_ref, v_ref, g_ref, r1_ref, r2_ref, *rest):
        go_ref, d_ref, mo_ref, vo_ref = rest[n_prev:]
        grad = g_ref[...] + r1_ref[...]
        for k in range(3):
            grad = grad + r2_ref[k].astype(F32)
        go_ref[...] = grad
        d_ref[...], mo_ref[...], vo_ref[...] = _adam_math(w_ref[...], grad, m_ref[...], v_ref[...])

    spec = pl.BlockSpec((None, tr, C), lambda t, co: (layer, t, 0))
    return pl.pallas_call(
        body, name=name,
        grid_spec=pltpu.PrefetchScalarGridSpec(
            num_scalar_prefetch=1, grid=(rows // tr,),
            in_specs=[spec, spec, spec,
                      pl.BlockSpec((None, tr, C), lambda t, co: (4 * co[2] + 2 * co[0] + co[1], t, 0)),
                      pl.BlockSpec((None, tr, C), lambda t, co: (2 * co[0] + co[1], t, 0)),
                      pl.BlockSpec((3, tr, C), lambda t, co: (0, t, 0))] + [_HBM] * n_prev,
            out_specs=[spec] * 4),
        out_shape=[jax.ShapeDtypeStruct((L, rows, C), F32)] * 4,
        input_output_aliases={7 + k: k for k in range(n_prev)},
        compiler_params=_cp("parallel"),
    )(coords, w, m, v, g, r1, r2, *(prev or []))


def _adamw_small(w, g, m, v, name):
    def body(w_ref, g_ref, m_ref, v_ref, d_ref, mo_ref, vo_ref):
        d_ref[...], mo_ref[...], vo_ref[...] = _adam_math(w_ref[...], g_ref[...], m_ref[...], v_ref[...])

    return pl.pallas_call(
        body, name=name, out_shape=[jax.ShapeDtypeStruct(w.shape, F32)] * 3,
        in_specs=[pl.BlockSpec(memory_space=pltpu.VMEM)] * 4,
        out_specs=[pl.BlockSpec(memory_space=pltpu.VMEM)] * 3,
    )(w, g, m, v)


def _reduce_scatter_adds(coords, grads, r1s, tag):
    return [_chip_partials(coords, g, r, f"rs_add_{tag}{i}") for i, (g, r) in enumerate(zip(grads, r1s))]


def kernel(x, norm_gain, a_w_in, a_v_gain, a_w_s, a_b_s, a_w_out, b_w_in, b_q_gain, b_k_gain, b_w_out, c_w_in, c_w_grp, c_scale, c_w_out, loss_target, m_norm_gain, m_a_w_in, m_a_v_gain, m_a_w_s, m_a_b_s, m_a_w_out, m_b_w_in, m_b_q_gain, m_b_k_gain, m_b_w_out, m_c_w_in, m_c_w_grp, m_c_scale, m_c_w_out, v_norm_gain, v_a_w_in, v_a_v_gain, v_a_w_s, v_a_b_s, v_a_w_out, v_b_w_in, v_b_q_gain, v_b_k_gain, v_b_w_out, v_c_w_in, v_c_w_grp, v_c_scale, v_c_w_out):
    cx, cy, cc = _coords()
    coords = jnp.stack([cx, cy, cc]).astype(jnp.int32)
    dev = 4 * cx + 2 * cy + cc
    Dm = x.shape[2]

    xs, tgt = x[0], loss_target[0]
    tables = _rope_tables(xs.shape[0])
    ng = lambda i: norm_gain[i:i + 1]
    ngr = len(B_DILATIONS)
    bst = [a_b_s[l].T for l in range(2)]
    b_gains = jnp.concatenate([b_q_gain[0], b_k_gain[0], jnp.zeros((2, HEAD_DIM), F32)], axis=0)
    ngp, rlc, cgc = c_w_grp.shape[1:]
    wire = lambda w: w.astype(WIRE)

    nvg, nsc = a_v_gain.size, c_scale.size
    vec = jnp.concatenate([a_v_gain.reshape(-1), c_scale.reshape(-1), jnp.zeros((1024 - nvg - nsc,), F32)]).reshape(8, 128)
    wa_in0, wa_out0, vecs = _all_gather_hbm([wire(a_w_in[0]), wire(a_w_out[0]), vec], "ag_layer0")
    wa_out0 = wa_out0.reshape(-1, Dm)
    vecs = vecs.reshape(N_DEV, -1)
    a_vg = vecs[:, :nvg].reshape((N_DEV,) + a_v_gain.shape).transpose(1, 0, 2).reshape(a_v_gain.shape[0], -1)
    c_sc = vecs[:, nvg:nvg + nsc].reshape(1, -1)

    h0, p0, *g1 = _norm_proj(xs, ng(0), wa_in0, "l0_proj", comm=_ag_send([wire(b_w_in[0]), wire(b_w_out[0])]))
    y0, wb_in, wb_out = _a_mid(p0, a_vg[0:1], a_w_s[0], bst[0], "l0_mid", comm=_ag_forward(g1))
    x1 = _out_proj(xs, y0, wa_out0, "l0_out")
    wb_out = wb_out.reshape(-1, Dm)

    later = [wire(c_w_in[0]), wire(c_w_grp[0]), wire(c_w_out[0]), wire(a_w_in[1]), wire(a_w_out[1])]
    h1, p1, *g2 = _norm_proj(x1, ng(1), wb_in, "l1_proj", comm=_ag_send(later))
    qk, wc_in, wc_grp, wc_out, wa_in1, wa_out1 = _b_qk_fwd(p1, tables, b_gains, "l1_qk", comm=_ag_forward(g2))
    ogs, lgs = zip(*[_b_attn_fwd(qk, p1, g, f"l1_attn{g}") for g in range(ngr)])
    tiles = [_attn_tile(D, xs.shape[0]) for D in B_DILATIONS]
    lgs = [_units_to_tokens(l, D, T) for l, D, T in zip(lgs, B_DILATIONS, tiles)]
    y1, o1, lse = _b_combine(ogs, lgs, p1, "l1_comb")
    x2 = _out_proj(x1, y1, wb_out, "l1_out")
    wc_grp = wc_grp.transpose(1, 0, 2, 3).reshape(ngp, N_DEV * rlc, cgc)
    wc_out = wc_out.reshape(-1, Dm)
    wa_out1 = wa_out1.reshape(-1, Dm)

    h2, p2 = _norm_proj(x2, ng(2), wc_in, "l2_proj")
    y2 = _c_mid(p2, wc_grp, c_sc, "l2_mid")
    x3 = _out_proj(x2, y2, wc_out, "l2_out")
    h3, p3 = _norm_proj(x3, ng(3), wa_in1, "l3_proj")
    y3, = _a_mid(p3, a_vg[1:2], a_w_s[1], bst[1], "l3_mid")
    loss_local, dx4, dx4a = _out_proj_loss(x3, y3, wa_out1, tgt, "l3_out_loss")

    flat3 = lambda g: g.reshape(N_DEV, -1, g.shape[-1])
    dp3, dws1, dbs1, dvg1 = _a_bwd(dx4a, wa_out1, p3, a_vg[1:2], a_w_s[1], bst[1], "l3_bwd")
    grads3 = [_dw_in(h3, dp3, "l3_dwin"), _dw_out(y3, dx4a, "l3_dwout")]
    dx3, dx3a, dg3, *r1_3 = _dh_norm_bwd(dp3, wa_in1, x3, ng(3), dx4, "l3_dh", comm=_rs_sibling(grads3))
    parts3 = _reduce_scatter_adds(coords, grads3, r1_3, "l3_")

    dd, dz, gc_grp, dsc, *r2_3 = _c_bwd1(dx3a, wc_out, p2, wc_grp, c_sc, "l2_bwd1", comm=_rs_chips(parts3))
    dp2 = _c_bwd2(dd, dz, "l2_bwd2")
    grads2 = [_dw_in(h2, dp2, "l2_dwin"), _dw_out(y2, dx3a, "l2_dwout"), flat3(gc_grp)]
    dx2, dx2a, dg2, *r1_2 = _dh_norm_bwd(dp2, wc_in, x2, ng(2), dx3, "l2_dh", comm=_rs_sibling(grads2))
    parts2 = _reduce_scatter_adds(coords, grads2, r1_2, "l2_")

    dov, delta, dp1 = _b_bwd_pre(dx2a, wb_out, o1, p1, "l1_bwdpre")
    dqs, dks, r2_2 = [], [], None
    for g in range(ngr):
        lse_u, delta_u = [_tokens_to_units(a, B_DILATIONS[g], tiles[g]) for a in (lse, delta)]
        dq, dk, dp1, *rest = _b_attn_bwd(qk, p1, dov, lse_u, delta_u, dp1, g, f"l1_attnbwd{g}",
                                         comm=_rs_chips(parts2) if g == 0 else None)
        if g == 0:
            r2_2 = rest
        dqs.append(dq)
        dks.append(dk)
    dp1, dgains = _b_qk_bwd(dqs, dks, p1, tables, b_gains, dp1, "l1_qkbwd")
    grads1 = [_dw_in(h1, dp1, "l1_dwin"), _dw_out(y1, dx2a, "l1_dwout")]
    dx1, dx1a, dg1, *r1_1 = _dh_norm_bwd(dp1, wb_in, x1, ng(1), dx2, "l1_dh", comm=_rs_sibling(grads1))
    parts1 = _reduce_scatter_adds(coords, grads1, r1_1, "l1_")

    dp0, dws0, dbs0, dvg0, *r2_1 = _a_bwd(dx1a, wa_out0, p0, a_vg[0:1], a_w_s[0], bst[0], "l0_bwd", comm=_rs_chips(parts1))
    small = dict(norm=jnp.concatenate([dg1, dg2, dg3], axis=0), a_ws=jnp.stack([dws0, dws1]),
                 a_bs=jnp.stack([dbs0.T, dbs1.T]), b_gains=dgains, a_vg=jnp.concatenate([dvg0, dvg1], axis=0), c_sc=dsc)
    order = ["norm", "a_ws", "a_bs", "b_gains", "a_vg", "c_sc"]
    rows = [small[k].reshape(-1, 128) for k in order]
    roff = np.cumsum([0] + [r.shape[0] for r in rows])
    gw_in0, gsmall = _dw_in(h0, dp0, "l0_dwin", comm=_ag_send([jnp.concatenate(rows, axis=0)]))
    gw_out0, gsmall = _dw_out(y0, dx1a, "l0_dwout", comm=_ag_forward([gsmall]))
    grads0 = [gw_in0, gw_out0]
    r1_0 = _run_comm(_rs_sibling(grads0), "l0_rs_sibling")
    parts0 = _reduce_scatter_adds(coords, grads0, r1_0, "l0_")
    dx0, _, dg0, *r2_0 = _dh_norm_bwd(dp0, wa_in0, xs, ng(0), dx1, "l0_dh", comm=_rs_chips(parts0))

    tot = _sum_blocks(gsmall, "small_sum")
    sm = {k: tot[int(roff[i]):int(roff[i + 1])].reshape(small[k].shape) for i, k in enumerate(order)}
    late = _all_reduce_small(jnp.concatenate([dg0.reshape(-1, 128), jnp.full((8, 128), loss_local, F32)], axis=0))
    sm["norm"] = jnp.concatenate([late[0:8].reshape(1, -1), sm["norm"]], axis=0)
    loss = late[8, 0]
    vl = a_v_gain.shape[1]
    g_small = dict(
        norm_gain=sm["norm"], a_w_s=sm["a_ws"], a_b_s=sm["a_bs"],
        b_q_gain=sm["b_gains"][None, 0:3], b_k_gain=sm["b_gains"][None, 3:6],
        a_v_gain=lax.dynamic_slice_in_dim(sm["a_vg"], dev * vl, vl, axis=1),
        c_scale=lax.dynamic_slice_in_dim(sm["c_sc"], dev * vl, vl, axis=1),
    )

    shares = dict(
        a_w_in=[(grads0[0], r1_0[0], r2_0[0]), (grads3[0], r1_3[0], r2_3[0])],
        a_w_out=[(grads0[1], r1_0[1], r2_0[1]), (grads3[1], r1_3[1], r2_3[1])],
        b_w_in=[(grads1[0], r1_1[0], r2_1[0])], b_w_out=[(grads1[1], r1_1[1], r2_1[1])],
        c_w_in=[(grads2[0], r1_2[0], r2_2[0])], c_w_out=[(grads2[1], r1_2[1], r2_2[1])],
        c_w_grp=[(grads2[2], r1_2[2], r2_2[2])])

    params = dict(a_w_in=a_w_in, a_w_out=a_w_out, b_w_in=b_w_in, b_w_out=b_w_out, c_w_in=c_w_in, c_w_grp=c_w_grp, c_w_out=c_w_out,
                  norm_gain=norm_gain, a_v_gain=a_v_gain, a_w_s=a_w_s, a_b_s=a_b_s, b_q_gain=b_q_gain, b_k_gain=b_k_gain, c_scale=c_scale)
    moms = dict(a_w_in=(m_a_w_in, v_a_w_in), a_w_out=(m_a_w_out, v_a_w_out), b_w_in=(m_b_w_in, v_b_w_in), b_w_out=(m_b_w_out, v_b_w_out),
                c_w_in=(m_c_w_in, v_c_w_in), c_w_grp=(m_c_w_grp, v_c_w_grp), c_w_out=(m_c_w_out, v_c_w_out),
                norm_gain=(m_norm_gain, v_norm_gain), a_v_gain=(m_a_v_gain, v_a_v_gain), a_w_s=(m_a_w_s, v_a_w_s),
                a_b_s=(m_a_b_s, v_a_b_s), b_q_gain=(m_b_q_gain, v_b_q_gain), b_k_gain=(m_b_k_gain, v_b_k_gain),
                c_scale=(m_c_scale, v_c_scale))
    grad, delta, new_m, new_v = {}, {}, {}, {}
    for pname, layers in shares.items():
        w, (m, v) = params[pname], moms[pname]
        as3 = lambda a: a.reshape(a.shape[0], -1, a.shape[-1])
        outs = None
        for l, (g, r1, r2) in enumerate(layers):
            outs = _adamw_sharded(coords, as3(w), as3(m), as3(v), g, r1, r2, l, outs, f"adamw_{pname}{l}")
        grad[pname], delta[pname], new_m[pname], new_v[pname] = [o.reshape(w.shape) for o in outs]
    for pname, g in g_small.items():
        w = params[pname]
        C = w.shape[-1]
        outs = _adamw_small(w.reshape(-1, C), g.reshape(-1, C), moms[pname][0].reshape(-1, C), moms[pname][1].reshape(-1, C),
                            f"adamw_{pname}")
        grad[pname] = g.reshape(w.shape)
        delta[pname], new_m[pname], new_v[pname] = [o.reshape(w.shape) for o in outs]

    wnames = ["norm_gain", "a_w_in", "a_v_gain", "a_w_s", "a_b_s", "a_w_out", "b_w_in", "b_q_gain", "b_k_gain", "b_w_out",
              "c_w_in", "c_w_grp", "c_scale", "c_w_out"]
    return (loss, dx0[None], *[grad[n] for n in wnames], *[delta[n] for n in wnames],
            *[new_m[n] for n in wnames], *[new_v[n] for n in wnames])
```

```python
import functools

import numpy as np
import jax
import jax.numpy as jnp
from jax import lax
from jax.experimental import pallas as pl
from jax.experimental.pallas import tpu as pltpu

F32 = jnp.float32
MXU = jnp.bfloat16
ACT = jnp.bfloat16
WIRE = jnp.bfloat16

EPS = 1e-6
CHUNK = 128
A_GROUPS = 8
HEAD_DIM = 128
B_HEADS = 8
B_DILATIONS = (1, 4, 16)
ROPE_DIM = 32
ROPE_THETA = 500000.0
POOL_SIZES = (2, 4, 8, 16)
POOL_HALO = 16
N_DEV = 8
NEG = -1e30

ADAM_LR, ADAM_B1, ADAM_B2, ADAM_EPS, ADAM_WD, ADAM_STEP = 0.001, 0.9, 0.999, 1e-08, 0.01, 10

VMEM_LIMIT = 62 * 1024 * 1024
MESH = pl.DeviceIdType.MESH


def _cp(*sem):
    return pltpu.CompilerParams(dimension_semantics=sem, vmem_limit_bytes=VMEM_LIMIT)


def _sigmoid(z):
    return 1.0 / (1.0 + jnp.exp(-z))


def _dot(a, b):
    return jnp.dot(a.astype(MXU), b.astype(MXU), preferred_element_type=F32)


def _dot_nt(a, b):
    return lax.dot_general(a.astype(MXU), b.astype(MXU), (((1,), (1,)), ((), ())), preferred_element_type=F32)


def _dot_tn(a, b):
    return lax.dot_general(a.astype(MXU), b.astype(MXU), (((0,), (0,)), ((), ())), preferred_element_type=F32)


def _chunk_slot(d):
    return (d % 2) * 4 + d // 2


class _Comm:
    def __init__(self, inputs, out_shapes, n_remote, make, aliases=None, n_local=1):
        self.inputs = list(inputs)
        self.out_shapes = list(out_shapes)
        self.n_remote = n_remote
        self.n_local = n_local
        self.make = make
        self.aliases = dict(aliases or {})

    def sems(self):
        return [pltpu.SemaphoreType.DMA((self.n_remote,)), pltpu.SemaphoreType.DMA((self.n_remote,)),
                pltpu.SemaphoreType.DMA((self.n_local,))]


_HBM = pl.BlockSpec(memory_space=pl.ANY)


def _launch(body, *, name, grid, in_specs, out_specs, out_shape, args, sem, scratch=(), aliases=None, comm=None):
    in_specs, out_specs, out_shape, scratch = list(in_specs), list(out_specs), list(out_shape), list(scratch)
    aliases = dict(aliases or {})
    if comm is None:
        return pl.pallas_call(body, name=name, grid=grid, in_specs=in_specs, out_specs=out_specs, out_shape=out_shape,
                              scratch_shapes=scratch, input_output_aliases=aliases, compiler_params=_cp(*sem))(*args)
    n_in, n_out, n_sc = len(in_specs), len(out_specs), len(scratch)
    nci, nco = len(comm.inputs), len(comm.out_shapes)

    def hosted(*refs):
        b_in, c_in = refs[:n_in], refs[n_in:n_in + nci]
        o0 = n_in + nci
        b_out, c_out = refs[o0:o0 + n_out], refs[o0 + n_out:o0 + n_out + nco]
        s0 = o0 + n_out + nco
        b_sc, sems = refs[s0:s0 + n_sc], refs[s0 + n_sc:]
        ids = [pl.program_id(a) for a in range(len(grid))]
        first = functools.reduce(jnp.logical_and, [i == 0 for i in ids])
        last = functools.reduce(jnp.logical_and, [i == g - 1 for i, g in zip(ids, grid)])

        @pl.when(first)
        def _():
            for cp in comm.make(c_in, c_out, *sems):
                cp.start()

        body(*b_in, *b_out, *b_sc)

        @pl.when(last)
        def _():
            for cp in comm.make(c_in, c_out, *sems):
                cp.wait()

    for ci, co in comm.aliases.items():
        aliases[n_in + ci] = n_out + co
    return pl.pallas_call(
        hosted, name=name, grid=grid, in_specs=in_specs + [_HBM] * nci, out_specs=out_specs + [_HBM] * nco,
        out_shape=out_shape + comm.out_shapes, scratch_shapes=scratch + comm.sems(),
        input_output_aliases=aliases, compiler_params=_cp(*["arbitrary"] * len(grid)))(*args, *comm.inputs)


def _run_comm(comm, name):
    nci, nco = len(comm.inputs), len(comm.out_shapes)

    def body(*refs):
        cps = comm.make(refs[:nci], refs[nci:nci + nco], *refs[nci + nco:])
        for cp in cps:
            cp.start()
        for cp in cps:
            cp.wait()

    return pl.pallas_call(
        body, name=name, in_specs=[_HBM] * nci, out_specs=[_HBM] * nco, out_shape=comm.out_shapes,
        scratch_shapes=comm.sems(), input_output_aliases=dict(comm.aliases))(*comm.inputs)


def _norm_proj(x, gain, w_dm, name, comm=None):
    M, Dm = x.shape
    nd, _, nl = w_dm.shape
    tm = min(M, 2048)

    def body(x_ref, g_ref, w_ref, h_ref, p_ref):
        @pl.when(pl.program_id(1) == 0)
        def _():
            xv = x_ref[...]
            r = lax.rsqrt(jnp.mean(xv * xv, axis=-1, keepdims=True) + EPS)
            h_ref[...] = (xv * r * g_ref[...]).astype(h_ref.dtype)

        p_ref[...] = _dot(h_ref[...], w_ref[...]).astype(p_ref.dtype)

    return _launch(
        body, name=name, grid=(M // tm, nd),
        in_specs=[pl.BlockSpec((tm, Dm), lambda i, j: (i, 0)),
                  pl.BlockSpec((1, Dm), lambda i, j: (0, 0)),
                  pl.BlockSpec((None, Dm, nl), lambda i, j: (j, 0, 0))],
        out_specs=[pl.BlockSpec((tm, Dm), lambda i, j: (i, 0)),
                   pl.BlockSpec((tm, nl), lambda i, j: (i, j))],
        out_shape=[jax.ShapeDtypeStruct((M, Dm), ACT), jax.ShapeDtypeStruct((M, nd * nl), ACT)],
        args=(x, gain, w_dm), sem=("parallel", "arbitrary"), comm=comm)


def _out_proj(x, y, w, name):
    M, Dm = x.shape
    K = y.shape[1]
    tm = min(M, 1024)

    def body(x_ref, y_ref, w_ref, o_ref):
        o_ref[...] = x_ref[...] + _dot(y_ref[...], w_ref[...])

    return pl.pallas_call(
        body, name=name, grid=(M // tm,),
        in_specs=[pl.BlockSpec((tm, Dm), lambda i: (i, 0)),
                  pl.BlockSpec((tm, K), lambda i: (i, 0)),
                  pl.BlockSpec((K, Dm), lambda i: (0, 0))],
        out_specs=pl.BlockSpec((tm, Dm), lambda i: (i, 0)),
        out_shape=jax.ShapeDtypeStruct((M, Dm), F32),
        compiler_params=_cp("parallel"),
    )(x, y, w)


def _out_proj_loss(x, y, w, target, name):
    M, Dm = x.shape
    K = y.shape[1]
    tm = min(M, 512)

    def body(x_ref, y_ref, w_ref, t_ref, dx_ref, dxa_ref, l_ref):
        @pl.when(pl.program_id(0) == 0)
        def _():
            l_ref[...] = jnp.zeros_like(l_ref)

        err = x_ref[...] + _dot(y_ref[...], w_ref[...]) - t_ref[...]
        dx = err * (1.0 / Dm)
        dx_ref[...] = dx
        dxa_ref[...] = dx.astype(dxa_ref.dtype)
        l_ref[...] += jnp.sum(err * err) * (0.5 / Dm)

    spec = pl.BlockSpec((tm, Dm), lambda i: (i, 0))
    dx, dxa, l = pl.pallas_call(
        body, name=name, grid=(M // tm,),
        in_specs=[spec, pl.BlockSpec((tm, K), lambda i: (i, 0)), pl.BlockSpec((K, Dm), lambda i: (0, 0)), spec],
        out_specs=[spec, spec, pl.BlockSpec((8, 128), lambda i: (0, 0))],
        out_shape=[jax.ShapeDtypeStruct((M, Dm), F32), jax.ShapeDtypeStruct((M, Dm), ACT),
                   jax.ShapeDtypeStruct((8, 128), F32)],
        compiler_params=_cp("arbitrary"),
    )(x, y, w, target)
    return l[0, 0], dx, dxa


def _dw_in(h, dproj, name, comm=None):
    M, Dm = h.shape
    nl = dproj.shape[1] // N_DEV
    tt = min(M, 4096)

    def body(a_ref, b_ref, o_ref):
        @pl.when(pl.program_id(1) == 0)
        def _():
            o_ref[...] = jnp.zeros_like(o_ref)

        o_ref[...] += _dot_tn(a_ref[...], b_ref[...])

    outs = _launch(
        body, name=name, grid=(N_DEV, M // tt),
        in_specs=[pl.BlockSpec((tt, Dm), lambda j, t: (t, 0)), pl.BlockSpec((tt, nl), lambda j, t: (t, j))],
        out_specs=[pl.BlockSpec((None, Dm, nl), lambda j, t: (_chunk_slot(j), 0, 0))],
        out_shape=[jax.ShapeDtypeStruct((N_DEV, Dm, nl), F32)],
        args=(h, dproj), sem=("parallel", "arbitrary"), comm=comm)
    return outs[0] if comm is None else outs


def _dw_out(y, dout, name, comm=None):
    M, K = y.shape
    Dm = dout.shape[1]
    kl = K // N_DEV
    tt = min(M, 1024)

    def body(a_ref, b_ref, o_ref):
        @pl.when(pl.program_id(0) == 0)
        def _():
            o_ref[...] = jnp.zeros_like(o_ref)

        b = b_ref[...]
        for j in range(N_DEV):
            o_ref[_chunk_slot(j)] += _dot_tn(a_ref[:, j * kl:(j + 1) * kl], b)

    outs = _launch(
        body, name=name, grid=(M // tt,),
        in_specs=[pl.BlockSpec((tt, K), lambda t: (t, 0)), pl.BlockSpec((tt, Dm), lambda t: (t, 0))],
        out_specs=[pl.BlockSpec((N_DEV, kl, Dm), lambda t: (0, 0, 0))],
        out_shape=[jax.ShapeDtypeStruct((N_DEV, kl, Dm), F32)],
        args=(y, dout), sem=("arbitrary",), comm=comm)
    return outs[0] if comm is None else outs


def _dh_norm_bwd(dproj, w_dm, x, gain, dres, name, comm=None):
    M, Dm = x.shape
    nd, _, nl = w_dm.shape
    tm = min(M, 1024)
    rows_bytes = tm * Dm * (4 + 2 * 4 + 2 * 4 + 2 * 4 + 2 * 2)
    block_bytes = 2 * (tm * nl + Dm * nl) * 2
    pair = 2 if rows_bytes + 2 * block_bytes <= VMEM_LIMIT - 8 * 1024 * 1024 else 1
    nj = nd // pair

    def body(dp_ref, w_ref, x_ref, g_ref, dr_ref, dx_ref, dxa_ref, dg_ref, acc_ref):
        i, j = pl.program_id(0), pl.program_id(1)

        @pl.when(j == 0)
        def _():
            acc_ref[...] = jnp.zeros_like(acc_ref)

        acc_ref[...] += functools.reduce(
            lambda a, b: a + b, [_dot_nt(dp_ref[:, d * nl:(d + 1) * nl], w_ref[d]) for d in range(pair)])

        @pl.when(j == nj - 1)
        def _():
            @pl.when(i == 0)
            def _():
                dg_ref[...] = jnp.zeros_like(dg_ref)

            dh = acc_ref[...]
            xv = x_ref[...]
            r = lax.rsqrt(jnp.mean(xv * xv, axis=-1, keepdims=True) + EPS)
            xn = xv * r
            dg_ref[...] += jnp.sum(dh * xn, axis=0, keepdims=True)
            dxn = dh * g_ref[...]
            dx = dr_ref[...] + r * (dxn - xn * jnp.mean(dxn * xn, axis=-1, keepdims=True))
            dx_ref[...] = dx
            dxa_ref[...] = dx.astype(dxa_ref.dtype)

    row = pl.BlockSpec((tm, Dm), lambda i, j: (i, 0))
    return _launch(
        body, name=name, grid=(M // tm, nj),
        in_specs=[pl.BlockSpec((tm, pair * nl), lambda i, j: (i, j)),
                  pl.BlockSpec((pair, Dm, nl), lambda i, j: (j, 0, 0)),
                  row, pl.BlockSpec((1, Dm), lambda i, j: (0, 0)), row],
        out_specs=[row, row, pl.BlockSpec((1, Dm), lambda i, j: (0, 0))],
        out_shape=[jax.ShapeDtypeStruct((M, Dm), F32), jax.ShapeDtypeStruct((M, Dm), ACT),
                   jax.ShapeDtypeStruct((1, Dm), F32)],
        scratch=[pltpu.VMEM((tm, Dm), F32)],
        args=(dproj, w_dm, x, gain, dres), sem=("arbitrary", "arbitrary"), comm=comm)


def _tril_mask():
    return lax.broadcasted_iota(jnp.int32, (CHUNK, CHUNK), 0) >= lax.broadcasted_iota(jnp.int32, (CHUNK, CHUNK), 1)


def _a_mid(proj, v_gain, w_s, b_st, name, comm=None):
    M = proj.shape[0]
    W = proj.shape[1] // 3
    gd = W // A_GROUPS
    tm = min(M, 256)

    def body(p_ref, vg_ref, ws_ref, bs_ref, y_ref):
        pv = p_ref[:, W:2 * W].astype(F32)
        r = lax.rsqrt(jnp.mean(pv * pv, axis=-1, keepdims=True) + EPS)
        v = (pv * r * vg_ref[...]).astype(MXU)
        tri = _tril_mask()
        for g in range(A_GROUPS):
            wg = jnp.where(tri, ws_ref[g], 0.0).astype(MXU)
            bcol = bs_ref[:, g:g + 1]
            for c in range(tm // CHUNK):
                rows, cols = slice(c * CHUNK, (c + 1) * CHUNK), slice(g * gd, (g + 1) * gd)
                mixed = jnp.dot(wg, v[rows, cols], preferred_element_type=F32) + bcol
                u = p_ref[rows, g * gd:(g + 1) * gd].astype(F32)
                z = p_ref[rows, 2 * W + g * gd:2 * W + (g + 1) * gd].astype(F32)
                y_ref[rows, cols] = (u * mixed * (z * _sigmoid(z))).astype(y_ref.dtype)

    return _launch(
        body, name=name, grid=(M // tm,),
        in_specs=[pl.BlockSpec((tm, 3 * W), lambda i: (i, 0)),
                  pl.BlockSpec((1, W), lambda i: (0, 0)),
                  pl.BlockSpec((A_GROUPS, CHUNK, CHUNK), lambda i: (0, 0, 0)),
                  pl.BlockSpec((CHUNK, A_GROUPS), lambda i: (0, 0))],
        out_specs=[pl.BlockSpec((tm, W), lambda i: (i, 0))],
        out_shape=[jax.ShapeDtypeStruct((M, W), ACT)],
        args=(proj, v_gain, w_s, b_st), sem=("parallel",), comm=comm)


def _a_bwd(dout, w_out, proj, v_gain, w_s, b_st, name, comm=None):
    M = proj.shape[0]
    W = proj.shape[1] // 3
    Dm = dout.shape[1]
    gd = W // A_GROUPS
    tm = min(M, 512)
    nt = M // tm

    def body(do_ref, wo_ref, p_ref, vg_ref, ws_ref, bs_ref, dp_ref, dws_ref, dbs_ref, dvg_ref, dv_s):
        i = pl.program_id(0)

        @pl.when(i == 0)
        def _():
            dws_ref[...] = jnp.zeros_like(dws_ref)
            dbs_ref[...] = jnp.zeros_like(dbs_ref)
            dvg_ref[...] = jnp.zeros_like(dvg_ref)

        dy = _dot_nt(do_ref[...], wo_ref[...])
        pv = p_ref[:, W:2 * W].astype(F32)
        r = lax.rsqrt(jnp.mean(pv * pv, axis=-1, keepdims=True) + EPS)
        pvn = pv * r
        vg = vg_ref[...]
        v = (pvn * vg).astype(MXU)
        tri = _tril_mask()
        for g in range(A_GROUPS):
            wf = jnp.where(tri, ws_ref[g], 0.0)
            wg = wf.astype(MXU)
            wgt = wf.T.astype(MXU)
            bcol = bs_ref[:, g:g + 1]
            for c in range(tm // CHUNK):
                rows, cols = slice(c * CHUNK, (c + 1) * CHUNK), slice(g * gd, (g + 1) * gd)
                vb = v[rows, cols]
                mixed = jnp.dot(wg, vb, preferred_element_type=F32) + bcol
                u = p_ref[rows, g * gd:(g + 1) * gd].astype(F32)
                z = p_ref[rows, 2 * W + g * gd:2 * W + (g + 1) * gd].astype(F32)
                sig = _sigmoid(z)
                sz = z * sig
                dyb = dy[rows, cols]
                dp_ref[rows, g * gd:(g + 1) * gd] = (dyb * mixed * sz).astype(dp_ref.dtype)
                dp_ref[rows, 2 * W + g * gd:2 * W + (g + 1) * gd] = (
                    dyb * u * mixed * (sig * (1.0 + z * (1.0 - sig)))).astype(dp_ref.dtype)
                dmix = dyb * u * sz
                dws_ref[g] += _dot_nt(dmix, vb)
                dbs_ref[:, g:g + 1] += jnp.sum(dmix, axis=1, keepdims=True)
                dv_s[rows, cols] = jnp.dot(wgt, dmix.astype(MXU), preferred_element_type=F32)
        dv = dv_s[...]
        dvg_ref[...] += jnp.sum(dv * pvn, axis=0, keepdims=True)
        dpvn = dv * vg
        dp_ref[:, W:2 * W] = (r * (dpvn - pvn * jnp.mean(dpvn * pvn, axis=-1, keepdims=True))).astype(dp_ref.dtype)

        @pl.when(i == nt - 1)
        def _():
            for g in range(A_GROUPS):
                dws_ref[g] = jnp.where(tri, dws_ref[g], 0.0)

    return _launch(
        body, name=name, grid=(nt,),
        in_specs=[pl.BlockSpec((tm, Dm), lambda i: (i, 0)),
                  pl.BlockSpec((W, Dm), lambda i: (0, 0)),
                  pl.BlockSpec((tm, 3 * W), lambda i: (i, 0)),
                  pl.BlockSpec((1, W), lambda i: (0, 0)),
                  pl.BlockSpec((A_GROUPS, CHUNK, CHUNK), lambda i: (0, 0, 0)),
                  pl.BlockSpec((CHUNK, A_GROUPS), lambda i: (0, 0))],
        out_specs=[pl.BlockSpec((tm, 3 * W), lambda i: (i, 0)),
                   pl.BlockSpec((A_GROUPS, CHUNK, CHUNK), lambda i: (0, 0, 0)),
                   pl.BlockSpec((CHUNK, A_GROUPS), lambda i: (0, 0)),
                   pl.BlockSpec((1, W), lambda i: (0, 0))],
        out_shape=[jax.ShapeDtypeStruct((M, 3 * W), ACT),
                   jax.ShapeDtypeStruct((A_GROUPS, CHUNK, CHUNK), F32),
                   jax.ShapeDtypeStruct((CHUNK, A_GROUPS), F32),
                   jax.ShapeDtypeStruct((1, W), F32)],
        scratch=[pltpu.VMEM((tm, W), F32)],
        args=(dout, w_out, proj, v_gain, w_s, b_st), sem=("arbitrary",), comm=comm)


def _pool_diff(xg, tail, i, tm, w):
    t = lax.broadcasted_iota(jnp.int32, (tm, tm + POOL_HALO), 0)
    s = lax.broadcasted_iota(jnp.int32, (tm, tm + POOL_HALO), 1)
    off = t - (s - POOL_HALO)
    band = jnp.where((off >= 0) & (off < w), 1.0, 0.0).astype(MXU)
    tail = jnp.where(i > 0, tail, jnp.zeros_like(tail))
    ext = jnp.concatenate([tail, xg], axis=0)
    ssum = jnp.dot(band, ext.astype(MXU), preferred_element_type=F32)
    tglob = i * tm + lax.broadcasted_iota(jnp.int32, (tm, 1), 0)
    cnt = jnp.minimum(tglob + 1, w).astype(F32)
    return ssum / cnt - xg.astype(F32)


def _c_mid(proj, w_grp, scale, name):
    M = proj.shape[0]
    W = proj.shape[1] // 2
    ng = len(POOL_SIZES)
    cg = W // ng
    tm = min(M, 256)
    hb = tm // POOL_HALO

    def body(xc_ref, tail_ref, z_ref, wg_ref, sc_ref, y_ref):
        i = pl.program_id(0)
        for g, w in enumerate(POOL_SIZES):
            cols = slice(g * cg, (g + 1) * cg)
            d = _pool_diff(xc_ref[:, cols], tail_ref[:, cols], i, tm, w)
            mixed = _dot(d, wg_ref[g]) * sc_ref[:, cols]
            z = z_ref[:, cols].astype(F32)
            y_ref[:, cols] = (mixed * (z * _sigmoid(z))).astype(y_ref.dtype)

    return pl.pallas_call(
        body, name=name, grid=(M // tm,),
        in_specs=[pl.BlockSpec((tm, W), lambda i: (i, 0)),
                  pl.BlockSpec((POOL_HALO, W), lambda i: (jnp.maximum(i * hb - 1, 0), 0)),
                  pl.BlockSpec((tm, W), lambda i: (i, 1)),
                  pl.BlockSpec((ng, cg, cg), lambda i: (0, 0, 0)),
                  pl.BlockSpec((1, W), lambda i: (0, 0))],
        out_specs=pl.BlockSpec((tm, W), lambda i: (i, 0)),
        out_shape=jax.ShapeDtypeStruct((M, W), ACT),
        compiler_params=_cp("parallel"),
    )(proj, proj, proj, w_grp, scale)


def _c_bwd1(dout, w_out, proj, w_grp, scale, name, comm=None):
    M = proj.shape[0]
    W = proj.shape[1] // 2
    Dm = dout.shape[1]
    ng = len(POOL_SIZES)
    cg = W // ng
    rl = cg // N_DEV
    tm = min(M, 256)
    hb = tm // POOL_HALO
    nt = M // tm

    def body(do_ref, wo_ref, xc_ref, tail_ref, z_ref, wg_ref, sc_ref, dd_ref, dz_ref, dwg_ref, dsc_ref, acc_ref):
        i = pl.program_id(0)

        @pl.when(i == 0)
        def _():
            acc_ref[...] = jnp.zeros_like(acc_ref)
            dsc_ref[...] = jnp.zeros_like(dsc_ref)

        dy = _dot_nt(do_ref[...], wo_ref[...])
        for g, w in enumerate(POOL_SIZES):
            cols = slice(g * cg, (g + 1) * cg)
            d = _pool_diff(xc_ref[:, cols], tail_ref[:, cols], i, tm, w)
            mr = _dot(d, wg_ref[g])
            sc = sc_ref[:, cols]
            z = z_ref[:, cols].astype(F32)
            sig = _sigmoid(z)
            dyg = dy[:, cols]
            dmixed = dyg * (z * sig)
            dz_ref[:, cols] = (dyg * (mr * sc) * (sig * (1.0 + z * (1.0 - sig)))).astype(dz_ref.dtype)
            dsc_ref[:, cols] += jnp.sum(dmixed * mr, axis=0, keepdims=True)
            dmr = (dmixed * sc).astype(MXU)
            acc_ref[g] += _dot_tn(d, dmr)
            dd_ref[:, cols] = _dot_nt(dmr, wg_ref[g]).astype(dd_ref.dtype)

        @pl.when(i == nt - 1)
        def _():
            for dev in range(N_DEV):
                for g in range(ng):
                    dwg_ref[_chunk_slot(dev), g] = acc_ref[g, dev * rl:(dev + 1) * rl, :]

    return _launch(
        body, name=name, grid=(nt,),
        in_specs=[pl.BlockSpec((tm, Dm), lambda i: (i, 0)),
                  pl.BlockSpec((W, Dm), lambda i: (0, 0)),
                  pl.BlockSpec((tm, W), lambda i: (i, 0)),
                  pl.BlockSpec((POOL_HALO, W), lambda i: (jnp.maximum(i * hb - 1, 0), 0)),
                  pl.BlockSpec((tm, W), lambda i: (i, 1)),
                  pl.BlockSpec((ng, cg, cg), lambda i: (0, 0, 0)),
                  pl.BlockSpec((1, W), lambda i: (0, 0))],
        out_specs=[pl.BlockSpec((tm, W), lambda i: (i, 0)),
                   pl.BlockSpec((tm, W), lambda i: (i, 0)),
                   pl.BlockSpec((N_DEV, ng, rl, cg), lambda i: (0, 0, 0, 0)),
                   pl.BlockSpec((1, W), lambda i: (0, 0))],
        out_shape=[jax.ShapeDtypeStruct((M, W), ACT), jax.ShapeDtypeStruct((M, W), ACT),
                   jax.ShapeDtypeStruct((N_DEV, ng, rl, cg), F32), jax.ShapeDtypeStruct((1, W), F32)],
        scratch=[pltpu.VMEM((ng, cg, cg), F32)],
        args=(dout, w_out, proj, proj, proj, w_grp, scale), sem=("arbitrary",), comm=comm)


def _c_bwd2(dd, dz, name):
    M, W = dd.shape
    ng = len(POOL_SIZES)
    cg = W // ng
    tm = min(M, 256)
    hb = tm // POOL_HALO
    nt = M // tm

    def body(dd_ref, head_ref, dz_ref, dp_ref):
        i = pl.program_id(0)
        s = lax.broadcasted_iota(jnp.int32, (tm, tm + POOL_HALO), 0)
        t = lax.broadcasted_iota(jnp.int32, (tm, tm + POOL_HALO), 1)
        off = t - s
        tglob = i * tm + lax.broadcasted_iota(jnp.int32, (tm + POOL_HALO, 1), 0)
        for g, w in enumerate(POOL_SIZES):
            cols = slice(g * cg, (g + 1) * cg)
            ddg = dd_ref[:, cols].astype(F32)
            head = head_ref[:, cols].astype(F32)
            head = jnp.where(i < nt - 1, head, jnp.zeros_like(head))
            cnt = jnp.minimum(tglob + 1, w).astype(F32)
            ext = (jnp.concatenate([ddg, head], axis=0) / cnt).astype(MXU)
            band = jnp.where((off >= 0) & (off < w), 1.0, 0.0).astype(MXU)
            dp_ref[:, cols] = (jnp.dot(band, ext, preferred_element_type=F32) - ddg).astype(dp_ref.dtype)
        dp_ref[:, W:] = dz_ref[...]

    return pl.pallas_call(
        body, name=name, grid=(nt,),
        in_specs=[pl.BlockSpec((tm, W), lambda i: (i, 0)),
                  pl.BlockSpec((POOL_HALO, W), lambda i: (jnp.minimum((i + 1) * hb, M // POOL_HALO - 1), 0)),
                  pl.BlockSpec((tm, W), lambda i: (i, 0))],
        out_specs=pl.BlockSpec((tm, 2 * W), lambda i: (i, 0)),
        out_shape=jax.ShapeDtypeStruct((M, 2 * W), ACT),
        compiler_params=_cp("parallel"),
    )(dd, dd, dz)


def _rope_tables(S):
    half = ROPE_DIM // 2
    inv_freq = jnp.power(jnp.float32(ROPE_THETA), -jnp.arange(half, dtype=F32) / half)
    ang = jnp.arange(S, dtype=F32)[:, None] * inv_freq[None, :]
    cos, sin = jnp.cos(ang), jnp.sin(ang)
    rest = HEAD_DIM - ROPE_DIM
    cf = jnp.concatenate([cos, cos, jnp.ones((S, rest), F32)], axis=1)
    sf = jnp.concatenate([-sin, sin, jnp.zeros((S, rest), F32)], axis=1)
    return cf, sf


def _swap_matrix():
    half = ROPE_DIM // 2
    a = lax.broadcasted_iota(jnp.int32, (HEAD_DIM, HEAD_DIM), 0)
    e = lax.broadcasted_iota(jnp.int32, (HEAD_DIM, HEAD_DIM), 1)
    hit = ((e < half) & (a == e + half)) | ((e >= half) & (e < 2 * half) & (a == e - half))
    return jnp.where(hit, 1.0, 0.0).astype(MXU)


def _b_qk_fwd(proj, tables, gains, name, comm=None):
    M = proj.shape[0]
    nsl = 2 * len(B_DILATIONS) * B_HEADS
    Wqk = nsl * HEAD_DIM
    tm = min(M, 256)

    def body(p_ref, cf_ref, sf_ref, g_ref, o_ref):
        cf, sf = cf_ref[...], sf_ref[...]
        swap = _swap_matrix()
        for j in range(nsl):
            cols = slice(j * HEAD_DIM, (j + 1) * HEAD_DIM)
            xv = p_ref[:, cols].astype(F32)
            r = lax.rsqrt(jnp.mean(xv * xv, axis=-1, keepdims=True) + EPS)
            xg = xv * g_ref[j // B_HEADS:j // B_HEADS + 1, :]
            hi = xg.astype(MXU)
            lo = (xg - hi.astype(F32)).astype(MXU)
            sw = jnp.dot(hi, swap, preferred_element_type=F32) + jnp.dot(lo, swap, preferred_element_type=F32)
            o_ref[:, cols] = (r * (xg * cf + sw * sf)).astype(o_ref.dtype)

    tspec = pl.BlockSpec((tm, HEAD_DIM), lambda i: (i, 0))
    return _launch(
        body, name=name, grid=(M // tm,),
        in_specs=[pl.BlockSpec((tm, Wqk), lambda i: (i, 0)), tspec, tspec,
                  pl.BlockSpec((8, HEAD_DIM), lambda i: (0, 0))],
        out_specs=[pl.BlockSpec((tm, Wqk), lambda i: (i, 0))],
        out_shape=[jax.ShapeDtypeStruct((M, Wqk), ACT)],
        args=(proj, *tables, gains), sem=("parallel",), comm=comm)


def _b_qk_bwd(dqs, dks, proj, tables, gains, dproj, name):
    M = proj.shape[0]
    ngr = len(B_DILATIONS)
    nsl = 2 * ngr * B_HEADS
    Wqk = nsl * HEAD_DIM
    Wg = B_HEADS * HEAD_DIM
    tm = min(M, 512)

    def body(*refs):
        d_refs = refs[:2 * ngr]
        p_ref, cf_ref, sf_ref, g_ref = refs[2 * ngr:2 * ngr + 4]
        dp_ref, dg_ref = refs[-2], refs[-1]

        @pl.when(pl.program_id(0) == 0)
        def _():
            dg_ref[...] = jnp.zeros_like(dg_ref)

        cf, sf = cf_ref[...], sf_ref[...]
        swap = _swap_matrix()
        for j in range(nsl):
            t, hh = j // B_HEADS, j % B_HEADS
            cols = slice(j * HEAD_DIM, (j + 1) * HEAD_DIM)
            dy = d_refs[t][:, hh * HEAD_DIM:(hh + 1) * HEAD_DIM].astype(F32)
            dxn = dy * cf + jnp.dot((dy * sf).astype(MXU), swap, preferred_element_type=F32)
            xv = p_ref[:, cols].astype(F32)
            r = lax.rsqrt(jnp.mean(xv * xv, axis=-1, keepdims=True) + EPS)
            xh = xv * r
            dg_ref[t:t + 1, :] += jnp.sum(dxn * xh, axis=0, keepdims=True)
            dxh = dxn * g_ref[t:t + 1, :]
            dp_ref[:, cols] = (r * (dxh - xh * jnp.mean(dxh * xh, axis=-1, keepdims=True))).astype(dp_ref.dtype)

    tspec = pl.BlockSpec((tm, HEAD_DIM), lambda i: (i, 0))
    dspec = pl.BlockSpec((tm, Wg), lambda i: (i, 0))
    n_in = 2 * ngr + 5
    return pl.pallas_call(
        body, name=name, grid=(M // tm,),
        in_specs=[dspec] * (2 * ngr) + [pl.BlockSpec((tm, Wqk), lambda i: (i, 0)), tspec, tspec,
                                        pl.BlockSpec((8, HEAD_DIM), lambda i: (0, 0)),
                                        pl.BlockSpec(memory_space=pl.ANY)],
        out_specs=[pl.BlockSpec((tm, Wqk), lambda i: (i, 0)), pl.BlockSpec((8, HEAD_DIM), lambda i: (0, 0))],
        out_shape=[jax.ShapeDtypeStruct(dproj.shape, dproj.dtype), jax.ShapeDtypeStruct((8, HEAD_DIM), F32)],
        input_output_aliases={n_in - 1: 0},
        compiler_params=_cp("arbitrary"),
    )(*dqs, *dks, proj, *tables, gains, dproj)


def _attn_tile(D, M):
    return max(HEAD_DIM * D, min(M, 2048))


class _TokenRows:
    GROUP = 16

    def __init__(self, D):
        self.D = D
        self.pitch = 24 if D == 16 else self.GROUP
        self.operand_dtype = F32 if D > 1 else ACT

    def rows(self, ntok):
        return ntok // self.GROUP * self.pitch

    def every_dth(self, tok0, n):
        start = tok0 // self.GROUP * self.pitch + tok0 % self.GROUP
        stride = self.D * self.pitch // self.GROUP
        return pl.ds(start, n) if stride == 1 else pl.ds(start, n, stride=stride)

    def put(self, dst, tok0, src_ref, ntok):
        if self.pitch == self.GROUP:
            dst[tok0:tok0 + ntok, :] = src_ref[...].astype(dst.dtype)
            return

        def group(i, carry):
            row = pl.multiple_of((tok0 // self.GROUP + i) * self.pitch, 8)
            dst[pl.ds(row, self.GROUP), :] = src_ref[pl.ds(pl.multiple_of(i * self.GROUP, self.GROUP), self.GROUP), :].astype(F32)
            return carry

        lax.fori_loop(0, ntok // self.GROUP, group, 0, unroll=8)

    def get(self, dst_ref, src, ntok):
        if self.pitch == self.GROUP:
            dst_ref[...] = src[0:ntok, :].astype(dst_ref.dtype)
            return

        def group(i, carry):
            row = pl.multiple_of(i * self.pitch, 8)
            dst_ref[pl.ds(pl.multiple_of(i * self.GROUP, self.GROUP), self.GROUP), :] = src[pl.ds(row, self.GROUP), :].astype(dst_ref.dtype)
            return carry

        lax.fori_loop(0, ntok // self.GROUP, group, 0, unroll=8)


def _attn_mask(base):
    qi = lax.broadcasted_iota(jnp.int32, (CHUNK, 2 * CHUNK), 0)
    ki = lax.broadcasted_iota(jnp.int32, (CHUNK, 2 * CHUNK), 1)
    return (ki >= qi) & (ki <= qi + CHUNK) & (ki >= CHUNK - base)


def _b_attn_fwd(qk, proj, g, name):
    M = qk.shape[0]
    D = B_DILATIONS[g]
    ngr = len(B_DILATIONS)
    T = _attn_tile(D, M)
    P = HEAD_DIM * D
    nsb = T // P
    Wg = B_HEADS * HEAD_DIM
    scale = np.float32(1.0 / np.sqrt(HEAD_DIM))

    lay = _TokenRows(D)
    RP, RT = lay.rows(P), lay.rows(T)

    def body(q_ref, k_ref, v_ref, o_ref, l_ref, qs, ks, vs, os_):
        n = pl.program_id(1)

        @pl.when(n == 0)
        def _():
            ks[0:RP, :] = jnp.zeros((RP, HEAD_DIM), ks.dtype)
            vs[0:RP, :] = jnp.zeros((RP, HEAD_DIM), vs.dtype)

        lay.put(qs, 0, q_ref, T)
        lay.put(ks, P, k_ref, T)
        lay.put(vs, P, v_ref, T)

        for b in range(nsb):
            mask = _attn_mask(n * (T // D) + b * CHUNK)
            for r in range(D):
                start = b * P + r
                q = qs[lay.every_dth(start, CHUNK), :]
                k = ks[lay.every_dth(start, 2 * CHUNK), :]
                v = vs[lay.every_dth(start, 2 * CHUNK), :]
                s = jnp.where(mask, _dot_nt(q, k) * scale, NEG)
                m = jnp.max(s, axis=-1, keepdims=True)
                p = jnp.exp(s - m)
                l = jnp.sum(p, axis=-1, keepdims=True)
                o = _dot(p, v) / l
                os_[lay.every_dth(start, CHUNK), :] = o
                l_ref[:, b * D + r:b * D + r + 1] = m + jnp.log(l)

        lay.get(o_ref, os_, T)
        ks[0:RP, :] = ks[RT:RT + RP, :]
        vs[0:RP, :] = vs[RT:RT + RP, :]

    blk = (T, HEAD_DIM)
    U = nsb * D
    return pl.pallas_call(
        body, name=name, grid=(B_HEADS, M // T),
        in_specs=[pl.BlockSpec(blk, lambda h, n: (n, g * B_HEADS + h)),
                  pl.BlockSpec(blk, lambda h, n: (n, (ngr + g) * B_HEADS + h)),
                  pl.BlockSpec(blk, lambda h, n: (n, (2 * ngr + g) * B_HEADS + h))],
        out_specs=[pl.BlockSpec(blk, lambda h, n: (n, h)), pl.BlockSpec((None, CHUNK, U), lambda h, n: (h, n, 0))],
        out_shape=[jax.ShapeDtypeStruct((M, Wg), ACT), jax.ShapeDtypeStruct((B_HEADS, (M // T) * CHUNK, U), F32)],
        scratch_shapes=[pltpu.VMEM((RT, HEAD_DIM), lay.operand_dtype), pltpu.VMEM((RP + RT, HEAD_DIM), lay.operand_dtype),
                        pltpu.VMEM((RP + RT, HEAD_DIM), lay.operand_dtype), pltpu.VMEM((RT, HEAD_DIM), F32)],
        compiler_params=_cp("parallel", "arbitrary"),
    )(qk, qk, proj)


def _units_to_tokens(a, D, T):
    H = a.shape[0]
    nsb = T // (HEAD_DIM * D)
    return a.reshape(H, -1, CHUNK, nsb, D).transpose(1, 3, 2, 4, 0).reshape(-1, H)


def _tokens_to_units(a, D, T):
    M, H = a.shape
    nsb = T // (HEAD_DIM * D)
    return a.reshape(M // T, nsb, CHUNK, D, H).transpose(4, 0, 2, 1, 3).reshape(H, (M // T) * CHUNK, nsb * D)


def _b_combine(os_, ls, proj, name):
    M, Wg = os_[0].shape
    ngr = len(B_DILATIONS)
    tm = min(M, 512)

    def body(*refs):
        o_refs, l_refs, z_ref = refs[:ngr], refs[ngr:2 * ngr], refs[2 * ngr]
        y_ref, o_ref, lse_ref = refs[2 * ngr + 1:]
        for h in range(B_HEADS):
            cols = slice(h * HEAD_DIM, (h + 1) * HEAD_DIM)
            ls_ = [r[:, h:h + 1] for r in l_refs]
            m = functools.reduce(jnp.maximum, ls_)
            es = [jnp.exp(l - m) for l in ls_]
            tot = functools.reduce(lambda a, b: a + b, es)
            o = functools.reduce(lambda a, b: a + b, [(e / tot) * r[:, cols].astype(F32) for e, r in zip(es, o_refs)])
            z = z_ref[:, cols].astype(F32)
            y_ref[:, cols] = (o * (z * _sigmoid(z))).astype(y_ref.dtype)
            o_ref[:, cols] = o.astype(o_ref.dtype)
            lse_ref[:, h:h + 1] = m + jnp.log(tot)

    spec = pl.BlockSpec((tm, Wg), lambda i: (i, 0))
    hspec = pl.BlockSpec((tm, B_HEADS), lambda i: (i, 0))
    return pl.pallas_call(
        body, name=name, grid=(M // tm,),
        in_specs=[spec] * ngr + [hspec] * ngr + [pl.BlockSpec((tm, Wg), lambda i: (i, 3 * ngr))],
        out_specs=[spec, spec, hspec],
        out_shape=[jax.ShapeDtypeStruct((M, Wg), ACT), jax.ShapeDtypeStruct((M, Wg), ACT),
                   jax.ShapeDtypeStruct((M, B_HEADS), F32)],
        compiler_params=_cp("parallel"),
    )(*os_, *ls, proj)


def _b_bwd_pre(dout, w_out, o, proj, name):
    M, Wg = o.shape
    Dm = dout.shape[1]
    ngr = len(B_DILATIONS)
    tm = min(M, 512)

    def body(do_ref, wo_ref, o_ref, z_ref, dov_ref, dl_ref, dp_ref):
        dy = _dot_nt(do_ref[...], wo_ref[...])
        z = z_ref[...].astype(F32)
        sig = _sigmoid(z)
        ov = o_ref[...].astype(F32)
        dp_ref[...] = (dy * ov * (sig * (1.0 + z * (1.0 - sig)))).astype(dp_ref.dtype)
        dov = dy * (z * sig)
        dov_ref[...] = dov.astype(dov_ref.dtype)
        prod = dov * ov
        for h in range(B_HEADS):
            dl_ref[:, h:h + 1] = jnp.sum(prod[:, h * HEAD_DIM:(h + 1) * HEAD_DIM], axis=-1, keepdims=True)

    spec = pl.BlockSpec((tm, Wg), lambda i: (i, 0))
    zspec = pl.BlockSpec((tm, Wg), lambda i: (i, 3 * ngr))
    return pl.pallas_call(
        body, name=name, grid=(M // tm,),
        in_specs=[pl.BlockSpec((tm, Dm), lambda i: (i, 0)), pl.BlockSpec((Wg, Dm), lambda i: (0, 0)), spec, zspec],
        out_specs=[spec, pl.BlockSpec((tm, B_HEADS), lambda i: (i, 0)), zspec],
        out_shape=[jax.ShapeDtypeStruct((M, Wg), ACT), jax.ShapeDtypeStruct((M, B_HEADS), F32),
                   jax.ShapeDtypeStruct(proj.shape, ACT)],
        compiler_params=_cp("parallel"),
    )(dout, w_out, o, proj)


def _b_attn_bwd(qk, proj, dov, lse, delta, dproj, g, name, comm=None):
    M = qk.shape[0]
    D = B_DILATIONS[g]
    ngr = len(B_DILATIONS)
    T = _attn_tile(D, M)
    P = HEAD_DIM * D
    nsb = T // P
    nt = M // T
    Wg = B_HEADS * HEAD_DIM
    scale = np.float32(1.0 / np.sqrt(HEAD_DIM))
    shift = T - P
    lay = _TokenRows(D)
    RP, RT = lay.rows(P), lay.rows(T)

    def body(q_ref, k_ref, v_ref, do_ref, l_ref, dl_ref, dp_any, dq_ref, dk_ref, dv_ref,
             qs, dos, ks, vs, dqs, dks, dvs):
        n = pl.program_id(1)

        @pl.when(n == 0)
        def _():
            ks[0:RP, :] = jnp.zeros((RP, HEAD_DIM), ks.dtype)
            vs[0:RP, :] = jnp.zeros((RP, HEAD_DIM), vs.dtype)
            dks[...] = jnp.zeros((2 * RT, HEAD_DIM), F32)
            dvs[...] = jnp.zeros((2 * RT, HEAD_DIM), F32)

        @pl.when(n < nt)
        def _():
            lay.put(qs, 0, q_ref, T)
            lay.put(dos, 0, do_ref, T)
            lay.put(ks, P, k_ref, T)
            lay.put(vs, P, v_ref, T)

            masks = [_attn_mask(n * (T // D) + b * CHUNK) for b in range(nsb)]
            for r in range(D):
                carry_dv = carry_dk = None
                for b in range(nsb):
                    start = b * P + r
                    qsl = lay.every_dth(start, CHUNK)
                    ksl = lay.every_dth(start, 2 * CHUNK)
                    lo = lay.every_dth(start + shift, CHUNK)
                    q = qs[qsl, :]
                    do = dos[qsl, :]
                    k = ks[ksl, :]
                    v = vs[ksl, :]
                    s = _dot_nt(q, k) * scale
                    u = b * D + r
                    p = jnp.where(masks[b], jnp.exp(s - l_ref[:, u:u + 1]), 0.0)
                    dv = _dot_tn(p, do)
                    dp = _dot_nt(do, v)
                    ds = (p * (dp - dl_ref[:, u:u + 1]) * scale).astype(MXU)
                    dqs[qsl, :] = _dot(ds, k)
                    dk = _dot_tn(ds, q)
                    if b == 0:
                        dvs[lo, :] += dv[:CHUNK]
                        dks[lo, :] += dk[:CHUNK]
                    else:
                        dvs[lo, :] = carry_dv + dv[:CHUNK]
                        dks[lo, :] = carry_dk + dk[:CHUNK]
                    carry_dv, carry_dk = dv[CHUNK:], dk[CHUNK:]
                hi = lay.every_dth((nsb - 1) * P + r + shift + P, CHUNK)
                dvs[hi, :] = carry_dv
                dks[hi, :] = carry_dk

        lay.get(dq_ref, dqs, T)
        lay.get(dk_ref, dks, T)
        lay.get(dv_ref, dvs, T)
        dks[0:RT, :] = dks[RT:2 * RT, :]
        dvs[0:RT, :] = dvs[RT:2 * RT, :]
        ks[0:RP, :] = ks[RT:RT + RP, :]
        vs[0:RP, :] = vs[RT:RT + RP, :]

    blk = (T, HEAD_DIM)
    cur = lambda n: jnp.minimum(n, nt - 1)
    prv = lambda n: jnp.maximum(n - 1, 0)
    return _launch(
        body, name=name, grid=(B_HEADS, nt + 1),
        in_specs=[pl.BlockSpec(blk, lambda h, n: (cur(n), g * B_HEADS + h)),
                  pl.BlockSpec(blk, lambda h, n: (cur(n), (ngr + g) * B_HEADS + h)),
                  pl.BlockSpec(blk, lambda h, n: (cur(n), (2 * ngr + g) * B_HEADS + h)),
                  pl.BlockSpec(blk, lambda h, n: (cur(n), h)),
                  pl.BlockSpec((None, CHUNK, nsb * D), lambda h, n: (h, cur(n), 0)),
                  pl.BlockSpec((None, CHUNK, nsb * D), lambda h, n: (h, cur(n), 0)),
                  pl.BlockSpec(memory_space=pl.ANY)],
        out_specs=[pl.BlockSpec(blk, lambda h, n: (cur(n), h)),
                   pl.BlockSpec(blk, lambda h, n: (prv(n), h)),
                   pl.BlockSpec(blk, lambda h, n: (prv(n), (2 * ngr + g) * B_HEADS + h))],
        out_shape=[jax.ShapeDtypeStruct((M, Wg), ACT), jax.ShapeDtypeStruct((M, Wg), ACT),
                   jax.ShapeDtypeStruct(dproj.shape, dproj.dtype)],
        scratch=[pltpu.VMEM((RT, HEAD_DIM), lay.operand_dtype)] * 2
        + [pltpu.VMEM((RP + RT, HEAD_DIM), lay.operand_dtype)] * 2
        + [pltpu.VMEM((RT, HEAD_DIM), F32)]
        + [pltpu.VMEM((2 * RT, HEAD_DIM), F32)] * 2,
        aliases={6: 2},
        args=(qk, qk, proj, dov, lse, delta, dproj), sem=("parallel", "arbitrary"), comm=comm)


def _coords():
    return lax.axis_index("x"), lax.axis_index("y"), lax.axis_index("c")


def _gather_blocks(x_refs, out_refs, send_sems, recv_sems, local_sems):
    x, y, c = _coords()
    me, sibling = (x, y, c), (x, y, 1 - c)
    chips = [(1 - x, y), (x, 1 - y), (1 - x, 1 - y)]
    arrays = range(len(x_refs))

    def slot(a, px, py, pc):
        return out_refs[a].at[4 * px + 2 * py + pc]

    def copy(a, k, block, to, src=None):
        return _remote(slot(a, *block) if src is None else src, slot(a, *block), send_sems, recv_sems, 7 * a + k, to)

    mine = [pltpu.make_async_copy(x_refs[a], slot(a, *me), local_sems.at[a]) for a in arrays]
    first = [copy(a, 0, me, sibling, src=x_refs[a]) for a in arrays]
    first += [copy(a, 1 + j, me, (*chip, c), src=x_refs[a]) for j, chip in enumerate(chips) for a in arrays]
    for cp in mine + first:
        cp.start()
    passed = []
    for j, chip in enumerate(chips):
        for a in arrays:
            copy(a, 1 + j, (*chip, c), me).wait_recv()
            passed.append(copy(a, 4 + j, (*chip, c), sibling))
            passed[-1].start()
    for a in arrays:
        copy(a, 0, sibling, me).wait_recv()
        for j, chip in enumerate(chips):
            copy(a, 4 + j, (*chip, 1 - c), me).wait_recv()
    for cp in first + passed:
        cp.wait_send()
    for cp in mine:
        cp.wait()


def _all_gather_hbm(arrays, name):
    n = len(arrays)

    def body(*refs):
        _gather_blocks(refs[:n], refs[n:2 * n], *refs[2 * n:])

    return pl.pallas_call(
        body, name=name, in_specs=[_HBM] * n, out_specs=[_HBM] * n,
        out_shape=[jax.ShapeDtypeStruct((N_DEV,) + a.shape, a.dtype) for a in arrays],
        scratch_shapes=[pltpu.SemaphoreType.DMA((7 * n,)), pltpu.SemaphoreType.DMA((7 * n,)),
                        pltpu.SemaphoreType.DMA((n,))],
    )(*arrays)


def _all_reduce_small(part):
    R, C = part.shape

    def body(x_ref, tot_ref, gath, send_sems, recv_sems, local_sems):
        _gather_blocks([x_ref], [gath], send_sems, recv_sems, local_sems)
        acc = gath[0]
        for d in range(1, N_DEV):
            acc = acc + gath[d]
        tot_ref[...] = acc

    return pl.pallas_call(
        body, name="ar_small",
        in_specs=[pl.BlockSpec(memory_space=pltpu.VMEM)],
        out_specs=pl.BlockSpec(memory_space=pltpu.VMEM),
        out_shape=jax.ShapeDtypeStruct((R, C), F32),
        scratch_shapes=[pltpu.VMEM((N_DEV, R, C), F32),
                        pltpu.SemaphoreType.DMA((7,)), pltpu.SemaphoreType.DMA((7,)), pltpu.SemaphoreType.DMA((1,))],
        compiler_params=pltpu.CompilerParams(vmem_limit_bytes=VMEM_LIMIT),
    )(part)


def _sum_blocks(gath, name):
    _, R, C = gath.shape

    def body(g_ref, o_ref):
        acc = g_ref[0]
        for d in range(1, N_DEV):
            acc = acc + g_ref[d]
        o_ref[...] = acc

    return pl.pallas_call(
        body, name=name,
        in_specs=[pl.BlockSpec(memory_space=pltpu.VMEM)], out_specs=pl.BlockSpec(memory_space=pltpu.VMEM),
        out_shape=jax.ShapeDtypeStruct((R, C), F32),
        compiler_params=pltpu.CompilerParams(vmem_limit_bytes=VMEM_LIMIT),
    )(gath)


def _remote(src, dst, send_sems, recv_sems, k, peer):
    return pltpu.make_async_remote_copy(src_ref=src, dst_ref=dst, send_sem=send_sems.at[k], recv_sem=recv_sems.at[k],
                                        device_id=peer, device_id_type=MESH)


def _ag_send(arrays):
    n = len(arrays)

    def make(c_in, c_out, send_sems, recv_sems, local_sems):
        x, y, c = _coords()
        peers = [(x, y, 1 - c), (1 - x, y, c), (x, 1 - y, c), (1 - x, 1 - y, c)]
        cps = []
        for a in range(n):
            src, dst = c_in[a], c_out[a].at[4 * x + 2 * y + c]
            cps.append(pltpu.make_async_copy(src, dst, local_sems.at[a]))
            cps += [_remote(src, dst, send_sems, recv_sems, 4 * a + k, peer) for k, peer in enumerate(peers)]
        return cps

    return _Comm(arrays, [jax.ShapeDtypeStruct((N_DEV,) + a.shape, a.dtype) for a in arrays], 4 * n, make, n_local=n)


def _ag_forward(gaths):
    n = len(gaths)

    def make(c_in, c_out, send_sems, recv_sems, local_sems):
        x, y, c = _coords()
        chips = [(1 - x, y), (x, 1 - y), (1 - x, 1 - y)]
        cps = []
        for a in range(n):
            buf = c_out[a]
            cps += [_remote(buf.at[4 * px + 2 * py + c], buf.at[4 * px + 2 * py + c], send_sems, recv_sems, 3 * a + j,
                            (x, y, 1 - c)) for j, (px, py) in enumerate(chips)]
        return cps

    return _Comm(gaths, [jax.ShapeDtypeStruct(g.shape, g.dtype) for g in gaths], 3 * n, make,
                 aliases={a: a for a in range(n)})


def _rs_sibling(grads):
    n = len(grads)

    def make(c_in, c_out, send_sems, recv_sems, local_sem):
        x, y, c = _coords()
        return [_remote(c_in[a].at[pl.ds(4 * (1 - c), 4)], c_out[a], send_sems, recv_sems, a, (x, y, 1 - c))
                for a in range(n)]

    return _Comm(grads, [jax.ShapeDtypeStruct((4,) + g.shape[1:], g.dtype) for g in grads], n, make)


def _rs_chips(parts):
    n = len(parts)

    def make(c_in, c_out, send_sems, recv_sems, local_sem):
        x, y, c = _coords()
        peers = [(x, 1 - y, c), (1 - x, y, c), (1 - x, 1 - y, c)]
        return [_remote(c_in[a].at[k], c_out[a].at[k], send_sems, recv_sems, 3 * a + k, peer)
                for a in range(n) for k, peer in enumerate(peers)]

    return _Comm(parts, [jax.ShapeDtypeStruct(p.shape, p.dtype) for p in parts], 3 * n, make)


def _row_tile(rows, cols):
    tr = min(rows, 1 << int(np.log2((1 << 18) // cols)))
    assert rows % tr == 0
    return tr


def _chip_partials(coords, g, r1, name):
    _, rows, C = g.shape
    tr = _row_tile(rows, C)

    def body(co_ref, g_ref, r_ref, o_ref):
        o_ref[...] = (g_ref[...] + r_ref[...]).astype(o_ref.dtype)

    def chip(k, co):
        return jnp.bitwise_xor(2 * co[0] + co[1], k + 1)

    return pl.pallas_call(
        body, name=name,
        grid_spec=pltpu.PrefetchScalarGridSpec(
            num_scalar_prefetch=1, grid=(3, rows // tr),
            in_specs=[pl.BlockSpec((None, tr, C), lambda k, t, co: (4 * co[2] + chip(k, co), t, 0)),
                      pl.BlockSpec((None, tr, C), lambda k, t, co: (chip(k, co), t, 0))],
            out_specs=pl.BlockSpec((None, tr, C), lambda k, t, co: (k, t, 0))),
        out_shape=jax.ShapeDtypeStruct((3, rows, C), WIRE),
        compiler_params=_cp("parallel", "parallel"),
    )(coords, g, r1)


def _adam_math(w, g, m, v):
    m = ADAM_B1 * m + (1.0 - ADAM_B1) * g
    v = ADAM_B2 * v + (1.0 - ADAM_B2) * (g * g)
    m_hat = m / (1.0 - ADAM_B1 ** ADAM_STEP)
    v_hat = v / (1.0 - ADAM_B2 ** ADAM_STEP)
    delta = -ADAM_LR * (m_hat / (jnp.sqrt(v_hat) + ADAM_EPS) + ADAM_WD * w)
    return delta, m, v


def _adamw_sharded(coords, w, m, v, g, r1, r2, layer, prev, name):
    L, rows, C = w.shape
    tr = _row_tile(rows, C)
    n_prev = 0 if prev is None else len(prev)

    def body(co_ref, w_ref, m_ref, v_ref, g_ref, r1_ref, r2_ref, *rest):
        go_ref, d_ref, mo_ref, vo_ref = rest[n_prev:]
        grad = g_ref[...] + r1_ref[...]
        for k in range(3):
            grad = grad + r2_ref[k].astype(F32)
        go_ref[...] = grad
        d_ref[...], mo_ref[...], vo_ref[...] = _adam_math(w_ref[...], grad, m_ref[...], v_ref[...])

    spec = pl.BlockSpec((None, tr, C), lambda t, co: (layer, t, 0))
    return pl.pallas_call(
        body, name=name,
        grid_spec=pltpu.PrefetchScalarGridSpec(
            num_scalar_prefetch=1, grid=(rows // tr,),
            in_specs=[spec, spec, spec,
                      pl.BlockSpec((None, tr, C), lambda t, co: (4 * co[2] + 2 * co[0] + co[1], t, 0)),
                      pl.BlockSpec((None, tr, C), lambda t, co: (2 * co[0] + co[1], t, 0)),
                      pl.BlockSpec((3, tr, C), lambda t, co: (0, t, 0))] + [_HBM] * n_prev,
            out_specs=[spec] * 4),
        out_shape=[jax.ShapeDtypeStruct((L, rows, C), F32)] * 4,
        input_output_aliases={7 + k: k for k in range(n_prev)},
        compiler_params=_cp("parallel"),
    )(coords, w, m, v, g, r1, r2, *(prev or []))


def _adamw_small(w, g, m, v, name):
    def body(w_ref, g_ref, m_ref, v_ref, d_ref, mo_ref, vo_ref):
        d_ref[...], mo_ref[...], vo_ref[...] = _adam_math(w_ref[...], g_ref[...], m_ref[...], v_ref[...])

    return pl.pallas_call(
        body, name=name, out_shape=[jax.ShapeDtypeStruct(w.shape, F32)] * 3,
        in_specs=[pl.BlockSpec(memory_space=pltpu.VMEM)] * 4,
        out_specs=[pl.BlockSpec(memory_space=pltpu.VMEM)] * 3,
    )(w, g, m, v)


def _reduce_scatter_adds(coords, grads, r1s, tag):
    return [_chip_partials(coords, g, r, f"rs_add_{tag}{i}") for i, (g, r) in enumerate(zip(grads, r1s))]


def kernel(x, norm_gain, a_w_in, a_v_gain, a_w_s, a_b_s, a_w_out, b_w_in, b_q_gain, b_k_gain, b_w_out, c_w_in, c_w_grp, c_scale, c_w_out, loss_target, m_norm_gain, m_a_w_in, m_a_v_gain, m_a_w_s, m_a_b_s, m_a_w_out, m_b_w_in, m_b_q_gain, m_b_k_gain, m_b_w_out, m_c_w_in, m_c_w_grp, m_c_scale, m_c_w_out, v_norm_gain, v_a_w_in, v_a_v_gain, v_a_w_s, v_a_b_s, v_a_w_out, v_b_w_in, v_b_q_gain, v_b_k_gain, v_b_w_out, v_c_w_in, v_c_w_grp, v_c_scale, v_c_w_out):
    cx, cy, cc = _coords()
    coords = jnp.stack([cx, cy, cc]).astype(jnp.int32)
    dev = 4 * cx + 2 * cy + cc
    Dm = x.shape[2]

    xs, tgt = x[0], loss_target[0]
    tables = _rope_tables(xs.shape[0])
    ng = lambda i: norm_gain[i:i + 1]
    ngr = len(B_DILATIONS)
    bst = [a_b_s[l].T for l in range(2)]
    b_gains = jnp.concatenate([b_q_gain[0], b_k_gain[0], jnp.zeros((2, HEAD_DIM), F32)], axis=0)
    ngp, rlc, cgc = c_w_grp.shape[1:]
    wire = lambda w: w.astype(WIRE)

    nvg, nsc = a_v_gain.size, c_scale.size
    vec = jnp.concatenate([a_v_gain.reshape(-1), c_scale.reshape(-1), jnp.zeros((1024 - nvg - nsc,), F32)]).reshape(8, 128)
    wa_in0, wa_out0, vecs = _all_gather_hbm([wire(a_w_in[0]), wire(a_w_out[0]), vec], "ag_layer0")
    wa_out0 = wa_out0.reshape(-1, Dm)
    vecs = vecs.reshape(N_DEV, -1)
    a_vg = vecs[:, :nvg].reshape((N_DEV,) + a_v_gain.shape).transpose(1, 0, 2).reshape(a_v_gain.shape[0], -1)
    c_sc = vecs[:, nvg:nvg + nsc].reshape(1, -1)

    h0, p0, *g1 = _norm_proj(xs, ng(0), wa_in0, "l0_proj", comm=_ag_send([wire(b_w_in[0]), wire(b_w_out[0])]))
    y0, wb_in, wb_out = _a_mid(p0, a_vg[0:1], a_w_s[0], bst[0], "l0_mid", comm=_ag_forward(g1))
    x1 = _out_proj(xs, y0, wa_out0, "l0_out")
    wb_out = wb_out.reshape(-1, Dm)

    later = [wire(c_w_in[0]), wire(c_w_grp[0]), wire(c_w_out[0]), wire(a_w_in[1]), wire(a_w_out[1])]
    h1, p1, *g2 = _norm_proj(x1, ng(1), wb_in, "l1_proj", comm=_ag_send(later))
    qk, wc_in, wc_grp, wc_out, wa_in1, wa_out1 = _b_qk_fwd(p1, tables, b_gains, "l1_qk", comm=_ag_forward(g2))
    ogs, lgs = zip(*[_b_attn_fwd(qk, p1, g, f"l1_attn{g}") for g in range(ngr)])
    tiles = [_attn_tile(D, xs.shape[0]) for D in B_DILATIONS]
    lgs = [_units_to_tokens(l, D, T) for l, D, T in zip(lgs, B_DILATIONS, tiles)]
    y1, o1, lse = _b_combine(ogs, lgs, p1, "l1_comb")
    x2 = _out_proj(x1, y1, wb_out, "l1_out")
    wc_grp = wc_grp.transpose(1, 0, 2, 3).reshape(ngp, N_DEV * rlc, cgc)
    wc_out = wc_out.reshape(-1, Dm)
    wa_out1 = wa_out1.reshape(-1, Dm)

    h2, p2 = _norm_proj(x2, ng(2), wc_in, "l2_proj")
    y2 = _c_mid(p2, wc_grp, c_sc, "l2_mid")
    x3 = _out_proj(x2, y2, wc_out, "l2_out")
    h3, p3 = _norm_proj(x3, ng(3), wa_in1, "l3_proj")
    y3, = _a_mid(p3, a_vg[1:2], a_w_s[1], bst[1], "l3_mid")
    loss_local, dx4, dx4a = _out_proj_loss(x3, y3, wa_out1, tgt, "l3_out_loss")

    flat3 = lambda g: g.reshape(N_DEV, -1, g.shape[-1])
    dp3, dws1, dbs1, dvg1 = _a_bwd(dx4a, wa_out1, p3, a_vg[1:2], a_w_s[1], bst[1], "l3_bwd")
    grads3 = [_dw_in(h3, dp3, "l3_dwin"), _dw_out(y3, dx4a, "l3_dwout")]
    dx3, dx3a, dg3, *r1_3 = _dh_norm_bwd(dp3, wa_in1, x3, ng(3), dx4, "l3_dh", comm=_rs_sibling(grads3))
    parts3 = _reduce_scatter_adds(coords, grads3, r1_3, "l3_")

    dd, dz, gc_grp, dsc, *r2_3 = _c_bwd1(dx3a, wc_out, p2, wc_grp, c_sc, "l2_bwd1", comm=_rs_chips(parts3))
    dp2 = _c_bwd2(dd, dz, "l2_bwd2")
    grads2 = [_dw_in(h2, dp2, "l2_dwin"), _dw_out(y2, dx3a, "l2_dwout"), flat3(gc_grp)]
    dx2, dx2a, dg2, *r1_2 = _dh_norm_bwd(dp2, wc_in, x2, ng(2), dx3, "l2_dh", comm=_rs_sibling(grads2))
    parts2 = _reduce_scatter_adds(coords, grads2, r1_2, "l2_")

    dov, delta, dp1 = _b_bwd_pre(dx2a, wb_out, o1, p1, "l1_bwdpre")
    dqs, dks, r2_2 = [], [], None
    for g in range(ngr):
        lse_u, delta_u = [_tokens_to_units(a, B_DILATIONS[g], tiles[g]) for a in (lse, delta)]
        dq, dk, dp1, *rest = _b_attn_bwd(qk, p1, dov, lse_u, delta_u, dp1, g, f"l1_attnbwd{g}",
                                         comm=_rs_chips(parts2) if g == 0 else None)
        if g == 0:
            r2_2 = rest
        dqs.append(dq)
        dks.append(dk)
    dp1, dgains = _b_qk_bwd(dqs, dks, p1, tables, b_gains, dp1, "l1_qkbwd")
    grads1 = [_dw_in(h1, dp1, "l1_dwin"), _dw_out(y1, dx2a, "l1_dwout")]
    dx1, dx1a, dg1, *r1_1 = _dh_norm_bwd(dp1, wb_in, x1, ng(1), dx2, "l1_dh", comm=_rs_sibling(grads1))
    parts1 = _reduce_scatter_adds(coords, grads1, r1_1, "l1_")

    dp0, dws0, dbs0, dvg0, *r2_1 = _a_bwd(dx1a, wa_out0, p0, a_vg[0:1], a_w_s[0], bst[0], "l0_bwd", comm=_rs_chips(parts1))
    small = dict(norm=jnp.concatenate([dg1, dg2, dg3], axis=0), a_ws=jnp.stack([dws0, dws1]),
                 a_bs=jnp.stack([dbs0.T, dbs1.T]), b_gains=dgains, a_vg=jnp.concatenate([dvg0, dvg1], axis=0), c_sc=dsc)
    order = ["norm", "a_ws", "a_bs", "b_gains", "a_vg", "c_sc"]
    rows = [small[k].reshape(-1, 128) for k in order]
    roff = np.cumsum([0] + [r.shape[0] for r in rows])
    gw_in0, gsmall = _dw_in(h0, dp0, "l0_dwin", comm=_ag_send([jnp.concatenate(rows, axis=0)]))
    gw_out0, gsmall = _dw_out(y0, dx1a, "l0_dwout", comm=_ag_forward([gsmall]))
    grads0 = [gw_in0, gw_out0]
    r1_0 = _run_comm(_rs_sibling(grads0), "l0_rs_sibling")
    parts0 = _reduce_scatter_adds(coords, grads0, r1_0, "l0_")
    dx0, _, dg0, *r2_0 = _dh_norm_bwd(dp0, wa_in0, xs, ng(0), dx1, "l0_dh", comm=_rs_chips(parts0))

    tot = _sum_blocks(gsmall, "small_sum")
    sm = {k: tot[int(roff[i]):int(roff[i + 1])].reshape(small[k].shape) for i, k in enumerate(order)}
    late = _all_reduce_small(jnp.concatenate([dg0.reshape(-1, 128), jnp.full((8, 128), loss_local, F32)], axis=0))
    sm["norm"] = jnp.concatenate([late[0:8].reshape(1, -1), sm["norm"]], axis=0)
    loss = late[8, 0]
    vl = a_v_gain.shape[1]
    g_small = dict(
        norm_gain=sm["norm"], a_w_s=sm["a_ws"], a_b_s=sm["a_bs"],
        b_q_gain=sm["b_gains"][None, 0:3], b_k_gain=sm["b_gains"][None, 3:6],
        a_v_gain=lax.dynamic_slice_in_dim(sm["a_vg"], dev * vl, vl, axis=1),
        c_scale=lax.dynamic_slice_in_dim(sm["c_sc"], dev * vl, vl, axis=1),
    )

    shares = dict(
        a_w_in=[(grads0[0], r1_0[0], r2_0[0]), (grads3[0], r1_3[0], r2_3[0])],
        a_w_out=[(grads0[1], r1_0[1], r2_0[1]), (grads3[1], r1_3[1], r2_3[1])],
        b_w_in=[(grads1[0], r1_1[0], r2_1[0])], b_w_out=[(grads1[1], r1_1[1], r2_1[1])],
        c_w_in=[(grads2[0], r1_2[0], r2_2[0])], c_w_out=[(grads2[1], r1_2[1], r2_2[1])],
        c_w_grp=[(grads2[2], r1_2[2], r2_2[2])])

    params = dict(a_w_in=a_w_in, a_w_out=a_w_out, b_w_in=b_w_in, b_w_out=b_w_out, c_w_in=c_w_in, c_w_grp=c_w_grp, c_w_out=c_w_out,
                  norm_gain=norm_gain, a_v_gain=a_v_gain, a_w_s=a_w_s, a_b_s=a_b_s, b_q_gain=b_q_gain, b_k_gain=b_k_gain, c_scale=c_scale)
    moms = dict(a_w_in=(m_a_w_in, v_a_w_in), a_w_out=(m_a_w_out, v_a_w_out), b_w_in=(m_b_w_in, v_b_w_in), b_w_out=(m_b_w_out, v_b_w_out),
                c_w_in=(m_c_w_in, v_c_w_in), c_w_grp=(m_c_w_grp, v_c_w_grp), c_w_out=(m_c_w_out, v_c_w_out),
                norm_gain=(m_norm_gain, v_norm_gain), a_v_gain=(m_a_v_gain, v_a_v_gain), a_w_s=(m_a_w_s, v_a_w_s),
                a_b_s=(m_a_b_s, v_a_b_s), b_q_gain=(m_b_q_gain, v_b_q_gain), b_k_gain=(m_b_k_gain, v_b_k_gain),
                c_scale=(m_c_scale, v_c_scale))
    grad, delta, new_m, new_v = {}, {}, {}, {}
    for pname, layers in shares.items():
        w, (m, v) = params[pname], moms[pname]
        as3 = lambda a: a.reshape(a.shape[0], -1, a.shape[-1])
        outs = None
        for l, (g, r1, r2) in enumerate(layers):
            outs = _adamw_sharded(coords, as3(w), as3(m), as3(v), g, r1, r2, l, outs, f"adamw_{pname}{l}")
        grad[pname], delta[pname], new_m[pname], new_v[pname] = [o.reshape(w.shape) for o in outs]
    for pname, g in g_small.items():
        w = params[pname]
        C = w.shape[-1]
        outs = _adamw_small(w.reshape(-1, C), g.reshape(-1, C), moms[pname][0].reshape(-1, C), moms[pname][1].reshape(-1, C),
                            f"adamw_{pname}")
        grad[pname] = g.reshape(w.shape)
        delta[pname], new_m[pname], new_v[pname] = [o.reshape(w.shape) for o in outs]

    wnames = ["norm_gain", "a_w_in", "a_v_gain", "a_w_s", "a_b_s", "a_w_out", "b_w_in", "b_q_gain", "b_k_gain", "b_w_out",
              "c_w_in", "c_w_grp", "c_scale", "c_w_out"]
    return (loss, dx0[None], *[grad[n] for n in wnames], *[delta[n] for n in wnames],
            *[new_m[n] for n in wnames], *[new_v[n] for n in wnames])
```

```python
import functools

import numpy as np
import jax
import jax.numpy as jnp
from jax import lax
from jax.experimental import pallas as pl
from jax.experimental.pallas import tpu as pltpu

F32 = jnp.float32
MXU = jnp.bfloat16
ACT = jnp.bfloat16
WIRE = jnp.bfloat16

EPS = 1e-6
CHUNK = 128
A_GROUPS = 8
HEAD_DIM = 128
B_HEADS = 8
B_DILATIONS = (1, 4, 16)
ROPE_DIM = 32
ROPE_THETA = 500000.0
POOL_SIZES = (2, 4, 8, 16)
POOL_HALO = 16
N_DEV = 8
NEG = -1e30

ADAM_LR, ADAM_B1, ADAM_B2, ADAM_EPS, ADAM_WD, ADAM_STEP = 0.001, 0.9, 0.999, 1e-08, 0.01, 10

VMEM_LIMIT = 62 * 1024 * 1024
MESH = pl.DeviceIdType.MESH


def _cp(*sem):
    return pltpu.CompilerParams(dimension_semantics=sem, vmem_limit_bytes=VMEM_LIMIT)


def _sigmoid(z):
    return 1.0 / (1.0 + jnp.exp(-z))


def _dot(a, b):
    return jnp.dot(a.astype(MXU), b.astype(MXU), preferred_element_type=F32)


def _dot_nt(a, b):
    return lax.dot_general(a.astype(MXU), b.astype(MXU), (((1,), (1,)), ((), ())), preferred_element_type=F32)


def _dot_tn(a, b):
    return lax.dot_general(a.astype(MXU), b.astype(MXU), (((0,), (0,)), ((), ())), preferred_element_type=F32)


def _chunk_slot(d):
    return (d % 2) * 4 + d // 2


class _Comm:
    def __init__(self, inputs, out_shapes, n_remote, make, aliases=None, n_local=1):
        self.inputs = list(inputs)
        self.out_shapes = list(out_shapes)
        self.n_remote = n_remote
        self.n_local = n_local
        self.make = make
        self.aliases = dict(aliases or {})

    def sems(self):
        return [pltpu.SemaphoreType.DMA((self.n_remote,)), pltpu.SemaphoreType.DMA((self.n_remote,)),
                pltpu.SemaphoreType.DMA((self.n_local,))]


_HBM = pl.BlockSpec(memory_space=pl.ANY)


def _launch(body, *, name, grid, in_specs, out_specs, out_shape, args, sem, scratch=(), aliases=None, comm=None):
    in_specs, out_specs, out_shape, scratch = list(in_specs), list(out_specs), list(out_shape), list(scratch)
    aliases = dict(aliases or {})
    if comm is None:
        return pl.pallas_call(body, name=name, grid=grid, in_specs=in_specs, out_specs=out_specs, out_shape=out_shape,
                              scratch_shapes=scratch, input_output_aliases=aliases, compiler_params=_cp(*sem))(*args)
    n_in, n_out, n_sc = len(in_specs), len(out_specs), len(scratch)
    nci, nco = len(comm.inputs), len(comm.out_shapes)

    def hosted(*refs):
        b_in, c_in = refs[:n_in], refs[n_in:n_in + nci]
        o0 = n_in + nci
        b_out, c_out = refs[o0:o0 + n_out], refs[o0 + n_out:o0 + n_out + nco]
        s0 = o0 + n_out + nco
        b_sc, sems = refs[s0:s0 + n_sc], refs[s0 + n_sc:]
        ids = [pl.program_id(a) for a in range(len(grid))]
        first = functools.reduce(jnp.logical_and, [i == 0 for i in ids])
        last = functools.reduce(jnp.logical_and, [i == g - 1 for i, g in zip(ids, grid)])

        @pl.when(first)
        def _():
            for cp in comm.make(c_in, c_out, *sems):
                cp.start()

        body(*b_in, *b_out, *b_sc)

        @pl.when(last)
        def _():
            for cp in comm.make(c_in, c_out, *sems):
                cp.wait()

    for ci, co in comm.aliases.items():
        aliases[n_in + ci] = n_out + co
    return pl.pallas_call(
        hosted, name=name, grid=grid, in_specs=in_specs + [_HBM] * nci, out_specs=out_specs + [_HBM] * nco,
        out_shape=out_shape + comm.out_shapes, scratch_shapes=scratch + comm.sems(),
        input_output_aliases=aliases, compiler_params=_cp(*["arbitrary"] * len(grid)))(*args, *comm.inputs)


def _run_comm(comm, name):
    nci, nco = len(comm.inputs), len(comm.out_shapes)

    def body(*refs):
        cps = comm.make(refs[:nci], refs[nci:nci + nco], *refs[nci + nco:])
        for cp in cps:
            cp.start()
        for cp in cps:
            cp.wait()

    return pl.pallas_call(
        body, name=name, in_specs=[_HBM] * nci, out_specs=[_HBM] * nco, out_shape=comm.out_shapes,
        scratch_shapes=comm.sems(), input_output_aliases=dict(comm.aliases))(*comm.inputs)


def _norm_proj(x, gain, w_dm, name, comm=None):
    M, Dm = x.shape
    nd, _, nl = w_dm.shape
    tm = min(M, 2048)

    def body(x_ref, g_ref, w_ref, h_ref, p_ref):
        @pl.when(pl.program_id(1) == 0)
        def _():
            xv = x_ref[...]
            r = lax.rsqrt(jnp.mean(xv * xv, axis=-1, keepdims=True) + EPS)
            h_ref[...] = (xv * r * g_ref[...]).astype(h_ref.dtype)

        p_ref[...] = _dot(h_ref[...], w_ref[...]).astype(p_ref.dtype)

    return _launch(
        body, name=name, grid=(M // tm, nd),
        in_specs=[pl.BlockSpec((tm, Dm), lambda i, j: (i, 0)),
                  pl.BlockSpec((1, Dm), lambda i, j: (0, 0)),
                  pl.BlockSpec((None, Dm, nl), lambda i, j: (j, 0, 0))],
        out_specs=[pl.BlockSpec((tm, Dm), lambda i, j: (i, 0)),
                   pl.BlockSpec((tm, nl), lambda i, j: (i, j))],
        out_shape=[jax.ShapeDtypeStruct((M, Dm), ACT), jax.ShapeDtypeStruct((M, nd * nl), ACT)],
        args=(x, gain, w_dm), sem=("parallel", "arbitrary"), comm=comm)


def _out_proj(x, y, w, name):
    M, Dm = x.shape
    K = y.shape[1]
    tm = min(M, 1024)

    def body(x_ref, y_ref, w_ref, o_ref):
        o_ref[...] = x_ref[...] + _dot(y_ref[...], w_ref[...])

    return pl.pallas_call(
        body, name=name, grid=(M // tm,),
        in_specs=[pl.BlockSpec((tm, Dm), lambda i: (i, 0)),
                  pl.BlockSpec((tm, K), lambda i: (i, 0)),
                  pl.BlockSpec((K, Dm), lambda i: (0, 0))],
        out_specs=pl.BlockSpec((tm, Dm), lambda i: (i, 0)),
        out_shape=jax.ShapeDtypeStruct((M, Dm), F32),
        compiler_params=_cp("parallel"),
    )(x, y, w)


def _out_proj_loss(x, y, w, target, name):
    M, Dm = x.shape
    K = y.shape[1]
    tm = min(M, 512)

    def body(x_ref, y_ref, w_ref, t_ref, dx_ref, dxa_ref, l_ref):
        @pl.when(pl.program_id(0) == 0)
        def _():
            l_ref[...] = jnp.zeros_like(l_ref)

        err = x_ref[...] + _dot(y_ref[...], w_ref[...]) - t_ref[...]
        dx = err * (1.0 / Dm)
        dx_ref[...] = dx
        dxa_ref[...] = dx.astype(dxa_ref.dtype)
        l_ref[...] += jnp.sum(err * err) * (0.5 / Dm)

    spec = pl.BlockSpec((tm, Dm), lambda i: (i, 0))
    dx, dxa, l = pl.pallas_call(
        body, name=name, grid=(M // tm,),
        in_specs=[spec, pl.BlockSpec((tm, K), lambda i: (i, 0)), pl.BlockSpec((K, Dm), lambda i: (0, 0)), spec],
        out_specs=[spec, spec, pl.BlockSpec((8, 128), lambda i: (0, 0))],
        out_shape=[jax.ShapeDtypeStruct((M, Dm), F32), jax.ShapeDtypeStruct((M, Dm), ACT),
                   jax.ShapeDtypeStruct((8, 128), F32)],
        compiler_params=_cp("arbitrary"),
    )(x, y, w, target)
    return l[0, 0], dx, dxa


def _dw_in(h, dproj, name, comm=None):
    M, Dm = h.shape
    nl = dproj.shape[1] // N_DEV
    tt = min(M, 4096)

    def body(a_ref, b_ref, o_ref):
        @pl.when(pl.program_id(1) == 0)
        def _():
            o_ref[...] = jnp.zeros_like(o_ref)

        o_ref[...] += _dot_tn(a_ref[...], b_ref[...])

    outs = _launch(
        body, name=name, grid=(N_DEV, M // tt),
        in_specs=[pl.BlockSpec((tt, Dm), lambda j, t: (t, 0)), pl.BlockSpec((tt, nl), lambda j, t: (t, j))],
        out_specs=[pl.BlockSpec((None, Dm, nl), lambda j, t: (_chunk_slot(j), 0, 0))],
        out_shape=[jax.ShapeDtypeStruct((N_DEV, Dm, nl), F32)],
        args=(h, dproj), sem=("parallel", "arbitrary"), comm=comm)
    return outs[0] if comm is None else outs


def _dw_out(y, dout, name, comm=None):
    M, K = y.shape
    Dm = dout.shape[1]
    kl = K // N_DEV
    tt = min(M, 2048)

    def body(a_ref, b_ref, o_ref):
        @pl.when(pl.program_id(0) == 0)
        def _():
            o_ref[...] = jnp.zeros_like(o_ref)

        b = b_ref[...]
        for j in range(N_DEV):
            o_ref[_chunk_slot(j)] += _dot_tn(a_ref[:, j * kl:(j + 1) * kl], b)

    outs = _launch(
        body, name=name, grid=(M // tt,),
        in_specs=[pl.BlockSpec((tt, K), lambda t: (t, 0)), pl.BlockSpec((tt, Dm), lambda t: (t, 0))],
        out_specs=[pl.BlockSpec((N_DEV, kl, Dm), lambda t: (0, 0, 0))],
        out_shape=[jax.ShapeDtypeStruct((N_DEV, kl, Dm), F32)],
        args=(y, dout), sem=("arbitrary",), comm=comm)
    return outs[0] if comm is None else outs


def _dh_norm_bwd(dproj, w_dm, x, gain, dres, name, comm=None):
    M, Dm = x.shape
    nd, _, nl = w_dm.shape
    tm = min(M, 1024)
    rows_bytes = tm * Dm * (4 + 2 * 4 + 2 * 4 + 2 * 4 + 2 * 2)
    block_bytes = 2 * (tm * nl + Dm * nl) * 2
    pair = 2 if rows_bytes + 2 * block_bytes <= VMEM_LIMIT - 8 * 1024 * 1024 else 1
    nj = nd // pair

    def body(dp_ref, w_ref, x_ref, g_ref, dr_ref, dx_ref, dxa_ref, dg_ref, acc_ref):
        i, j = pl.program_id(0), pl.program_id(1)

        @pl.when(j == 0)
        def _():
            acc_ref[...] = jnp.zeros_like(acc_ref)

        acc_ref[...] += functools.reduce(
            lambda a, b: a + b, [_dot_nt(dp_ref[:, d * nl:(d + 1) * nl], w_ref[d]) for d in range(pair)])

        @pl.when(j == nj - 1)
        def _():
            @pl.when(i == 0)
            def _():
                dg_ref[...] = jnp.zeros_like(dg_ref)

            dh = acc_ref[...]
            xv = x_ref[...]
            r = lax.rsqrt(jnp.mean(xv * xv, axis=-1, keepdims=True) + EPS)
            xn = xv * r
            dg_ref[...] += jnp.sum(dh * xn, axis=0, keepdims=True)
            dxn = dh * g_ref[...]
            dx = dr_ref[...] + r * (dxn - xn * jnp.mean(dxn * xn, axis=-1, keepdims=True))
            dx_ref[...] = dx
            dxa_ref[...] = dx.astype(dxa_ref.dtype)

    row = pl.BlockSpec((tm, Dm), lambda i, j: (i, 0))
    return _launch(
        body, name=name, grid=(M // tm, nj),
        in_specs=[pl.BlockSpec((tm, pair * nl), lambda i, j: (i, j)),
                  pl.BlockSpec((pair, Dm, nl), lambda i, j: (j, 0, 0)),
                  row, pl.BlockSpec((1, Dm), lambda i, j: (0, 0)), row],
        out_specs=[row, row, pl.BlockSpec((1, Dm), lambda i, j: (0, 0))],
        out_shape=[jax.ShapeDtypeStruct((M, Dm), F32), jax.ShapeDtypeStruct((M, Dm), ACT),
                   jax.ShapeDtypeStruct((1, Dm), F32)],
        scratch=[pltpu.VMEM((tm, Dm), F32)],
        args=(dproj, w_dm, x, gain, dres), sem=("arbitrary", "arbitrary"), comm=comm)


def _tril_mask():
    return lax.broadcasted_iota(jnp.int32, (CHUNK, CHUNK), 0) >= lax.broadcasted_iota(jnp.int32, (CHUNK, CHUNK), 1)


def _a_mid(proj, v_gain, w_s, b_st, name, comm=None):
    M = proj.shape[0]
    W = proj.shape[1] // 3
    gd = W // A_GROUPS
    tm = min(M, 256)

    def body(p_ref, vg_ref, ws_ref, bs_ref, y_ref):
        pv = p_ref[:, W:2 * W].astype(F32)
        r = lax.rsqrt(jnp.mean(pv * pv, axis=-1, keepdims=True) + EPS)
        v = (pv * r * vg_ref[...]).astype(MXU)
        tri = _tril_mask()
        for g in range(A_GROUPS):
            wg = jnp.where(tri, ws_ref[g], 0.0).astype(MXU)
            bcol = bs_ref[:, g:g + 1]
            for c in range(tm // CHUNK):
                rows, cols = slice(c * CHUNK, (c + 1) * CHUNK), slice(g * gd, (g + 1) * gd)
                mixed = jnp.dot(wg, v[rows, cols], preferred_element_type=F32) + bcol
                u = p_ref[rows, g * gd:(g + 1) * gd].astype(F32)
                z = p_ref[rows, 2 * W + g * gd:2 * W + (g + 1) * gd].astype(F32)
                y_ref[rows, cols] = (u * mixed * (z * _sigmoid(z))).astype(y_ref.dtype)

    return _launch(
        body, name=name, grid=(M // tm,),
        in_specs=[pl.BlockSpec((tm, 3 * W), lambda i: (i, 0)),
                  pl.BlockSpec((1, W), lambda i: (0, 0)),
                  pl.BlockSpec((A_GROUPS, CHUNK, CHUNK), lambda i: (0, 0, 0)),
                  pl.BlockSpec((CHUNK, A_GROUPS), lambda i: (0, 0))],
        out_specs=[pl.BlockSpec((tm, W), lambda i: (i, 0))],
        out_shape=[jax.ShapeDtypeStruct((M, W), ACT)],
        args=(proj, v_gain, w_s, b_st), sem=("parallel",), comm=comm)


def _a_bwd(dout, w_out, proj, v_gain, w_s, b_st, name, comm=None):
    M = proj.shape[0]
    W = proj.shape[1] // 3
    Dm = dout.shape[1]
    gd = W // A_GROUPS
    tm = min(M, 512)
    nt = M // tm

    def body(do_ref, wo_ref, p_ref, vg_ref, ws_ref, bs_ref, dp_ref, dws_ref, dbs_ref, dvg_ref, dv_s):
        i = pl.program_id(0)

        @pl.when(i == 0)
        def _():
            dws_ref[...] = jnp.zeros_like(dws_ref)
            dbs_ref[...] = jnp.zeros_like(dbs_ref)
            dvg_ref[...] = jnp.zeros_like(dvg_ref)

        dy = _dot_nt(do_ref[...], wo_ref[...])
        pv = p_ref[:, W:2 * W].astype(F32)
        r = lax.rsqrt(jnp.mean(pv * pv, axis=-1, keepdims=True) + EPS)
        pvn = pv * r
        vg = vg_ref[...]
        v = (pvn * vg).astype(MXU)
        tri = _tril_mask()
        for g in range(A_GROUPS):
            wf = jnp.where(tri, ws_ref[g], 0.0)
            wg = wf.astype(MXU)
            wgt = wf.T.astype(MXU)
            bcol = bs_ref[:, g:g + 1]
            for c in range(tm // CHUNK):
                rows, cols = slice(c * CHUNK, (c + 1) * CHUNK), slice(g * gd, (g + 1) * gd)
                vb = v[rows, cols]
                mixed = jnp.dot(wg, vb, preferred_element_type=F32) + bcol
                u = p_ref[rows, g * gd:(g + 1) * gd].astype(F32)
                z = p_ref[rows, 2 * W + g * gd:2 * W + (g + 1) * gd].astype(F32)
                sig = _sigmoid(z)
                sz = z * sig
                dyb = dy[rows, cols]
                dp_ref[rows, g * gd:(g + 1) * gd] = (dyb * mixed * sz).astype(dp_ref.dtype)
                dp_ref[rows, 2 * W + g * gd:2 * W + (g + 1) * gd] = (
                    dyb * u * mixed * (sig * (1.0 + z * (1.0 - sig)))).astype(dp_ref.dtype)
                dmix = dyb * u * sz
                dws_ref[g] += _dot_nt(dmix, vb)
                dbs_ref[:, g:g + 1] += jnp.sum(dmix, axis=1, keepdims=True)
                dv_s[rows, cols] = jnp.dot(wgt, dmix.astype(MXU), preferred_element_type=F32)
        dv = dv_s[...]
        dvg_ref[...] += jnp.sum(dv * pvn, axis=0, keepdims=True)
        dpvn = dv * vg
        dp_ref[:, W:2 * W] = (r * (dpvn - pvn * jnp.mean(dpvn * pvn, axis=-1, keepdims=True))).astype(dp_ref.dtype)

        @pl.when(i == nt - 1)
        def _():
            for g in range(A_GROUPS):
                dws_ref[g] = jnp.where(tri, dws_ref[g], 0.0)

    return _launch(
        body, name=name, grid=(nt,),
        in_specs=[pl.BlockSpec((tm, Dm), lambda i: (i, 0)),
                  pl.BlockSpec((W, Dm), lambda i: (0, 0)),
                  pl.BlockSpec((tm, 3 * W), lambda i: (i, 0)),
                  pl.BlockSpec((1, W), lambda i: (0, 0)),
                  pl.BlockSpec((A_GROUPS, CHUNK, CHUNK), lambda i: (0, 0, 0)),
                  pl.BlockSpec((CHUNK, A_GROUPS), lambda i: (0, 0))],
        out_specs=[pl.BlockSpec((tm, 3 * W), lambda i: (i, 0)),
                   pl.BlockSpec((A_GROUPS, CHUNK, CHUNK), lambda i: (0, 0, 0)),
                   pl.BlockSpec((CHUNK, A_GROUPS), lambda i: (0, 0)),
                   pl.BlockSpec((1, W), lambda i: (0, 0))],
        out_shape=[jax.ShapeDtypeStruct((M, 3 * W), ACT),
                   jax.ShapeDtypeStruct((A_GROUPS, CHUNK, CHUNK), F32),
                   jax.ShapeDtypeStruct((CHUNK, A_GROUPS), F32),
                   jax.ShapeDtypeStruct((1, W), F32)],
        scratch=[pltpu.VMEM((tm, W), F32)],
        args=(dout, w_out, proj, v_gain, w_s, b_st), sem=("arbitrary",), comm=comm)


def _pool_diff(xg, tail, i, tm, w):
    t = lax.broadcasted_iota(jnp.int32, (tm, tm + POOL_HALO), 0)
    s = lax.broadcasted_iota(jnp.int32, (tm, tm + POOL_HALO), 1)
    off = t - (s - POOL_HALO)
    band = jnp.where((off >= 0) & (off < w), 1.0, 0.0).astype(MXU)
    tail = jnp.where(i > 0, tail, jnp.zeros_like(tail))
    ext = jnp.concatenate([tail, xg], axis=0)
    ssum = jnp.dot(band, ext.astype(MXU), preferred_element_type=F32)
    tglob = i * tm + lax.broadcasted_iota(jnp.int32, (tm, 1), 0)
    cnt = jnp.minimum(tglob + 1, w).astype(F32)
    return ssum / cnt - xg.astype(F32)


def _c_mid(proj, w_grp, scale, name):
    M = proj.shape[0]
    W = proj.shape[1] // 2
    ng = len(POOL_SIZES)
    cg = W // ng
    tm = min(M, 256)
    hb = tm // POOL_HALO

    def body(xc_ref, tail_ref, z_ref, wg_ref, sc_ref, y_ref):
        i = pl.program_id(0)
        for g, w in enumerate(POOL_SIZES):
            cols = slice(g * cg, (g + 1) * cg)
            d = _pool_diff(xc_ref[:, cols], tail_ref[:, cols], i, tm, w)
            mixed = _dot(d, wg_ref[g]) * sc_ref[:, cols]
            z = z_ref[:, cols].astype(F32)
            y_ref[:, cols] = (mixed * (z * _sigmoid(z))).astype(y_ref.dtype)

    return pl.pallas_call(
        body, name=name, grid=(M // tm,),
        in_specs=[pl.BlockSpec((tm, W), lambda i: (i, 0)),
                  pl.BlockSpec((POOL_HALO, W), lambda i: (jnp.maximum(i * hb - 1, 0), 0)),
                  pl.BlockSpec((tm, W), lambda i: (i, 1)),
                  pl.BlockSpec((ng, cg, cg), lambda i: (0, 0, 0)),
                  pl.BlockSpec((1, W), lambda i: (0, 0))],
        out_specs=pl.BlockSpec((tm, W), lambda i: (i, 0)),
        out_shape=jax.ShapeDtypeStruct((M, W), ACT),
        compiler_params=_cp("parallel"),
    )(proj, proj, proj, w_grp, scale)


def _c_bwd1(dout, w_out, proj, w_grp, scale, name, comm=None):
    M = proj.shape[0]
    W = proj.shape[1] // 2
    Dm = dout.shape[1]
    ng = len(POOL_SIZES)
    cg = W // ng
    rl = cg // N_DEV
    tm = min(M, 256)
    hb = tm // POOL_HALO
    nt = M // tm

    def body(do_ref, wo_ref, xc_ref, tail_ref, z_ref, wg_ref, sc_ref, dd_ref, dz_ref, dwg_ref, dsc_ref, acc_ref):
        i = pl.program_id(0)

        @pl.when(i == 0)
        def _():
            acc_ref[...] = jnp.zeros_like(acc_ref)
            dsc_ref[...] = jnp.zeros_like(dsc_ref)

        dy = _dot_nt(do_ref[...], wo_ref[...])
        for g, w in enumerate(POOL_SIZES):
            cols = slice(g * cg, (g + 1) * cg)
            d = _pool_diff(xc_ref[:, cols], tail_ref[:, cols], i, tm, w)
            mr = _dot(d, wg_ref[g])
            sc = sc_ref[:, cols]
            z = z_ref[:, cols].astype(F32)
            sig = _sigmoid(z)
            dyg = dy[:, cols]
            dmixed = dyg * (z * sig)
            dz_ref[:, cols] = (dyg * (mr * sc) * (sig * (1.0 + z * (1.0 - sig)))).astype(dz_ref.dtype)
            dsc_ref[:, cols] += jnp.sum(dmixed * mr, axis=0, keepdims=True)
            dmr = (dmixed * sc).astype(MXU)
            acc_ref[g] += _dot_tn(d, dmr)
            dd_ref[:, cols] = _dot_nt(dmr, wg_ref[g]).astype(dd_ref.dtype)

        @pl.when(i == nt - 1)
        def _():
            for dev in range(N_DEV):
                for g in range(ng):
                    dwg_ref[_chunk_slot(dev), g] = acc_ref[g, dev * rl:(dev + 1) * rl, :]

    return _launch(
        body, name=name, grid=(nt,),
        in_specs=[pl.BlockSpec((tm, Dm), lambda i: (i, 0)),
                  pl.BlockSpec((W, Dm), lambda i: (0, 0)),
                  pl.BlockSpec((tm, W), lambda i: (i, 0)),
                  pl.BlockSpec((POOL_HALO, W), lambda i: (jnp.maximum(i * hb - 1, 0), 0)),
                  pl.BlockSpec((tm, W), lambda i: (i, 1)),
                  pl.BlockSpec((ng, cg, cg), lambda i: (0, 0, 0)),
                  pl.BlockSpec((1, W), lambda i: (0, 0))],
        out_specs=[pl.BlockSpec((tm, W), lambda i: (i, 0)),
                   pl.BlockSpec((tm, W), lambda i: (i, 0)),
                   pl.BlockSpec((N_DEV, ng, rl, cg), lambda i: (0, 0, 0, 0)),
                   pl.BlockSpec((1, W), lambda i: (0, 0))],
        out_shape=[jax.ShapeDtypeStruct((M, W), ACT), jax.ShapeDtypeStruct((M, W), ACT),
                   jax.ShapeDtypeStruct((N_DEV, ng, rl, cg), F32), jax.ShapeDtypeStruct((1, W), F32)],
        scratch=[pltpu.VMEM((ng, cg, cg), F32)],
        args=(dout, w_out, proj, proj, proj, w_grp, scale), sem=("arbitrary",), comm=comm)


def _c_bwd2(dd, dz, name):
    M, W = dd.shape
    ng = len(POOL_SIZES)
    cg = W // ng
    tm = min(M, 256)
    hb = tm // POOL_HALO
    nt = M // tm

    def body(dd_ref, head_ref, dz_ref, dp_ref):
        i = pl.program_id(0)
        s = lax.broadcasted_iota(jnp.int32, (tm, tm + POOL_HALO), 0)
        t = lax.broadcasted_iota(jnp.int32, (tm, tm + POOL_HALO), 1)
        off = t - s
        tglob = i * tm + lax.broadcasted_iota(jnp.int32, (tm + POOL_HALO, 1), 0)
        for g, w in enumerate(POOL_SIZES):
            cols = slice(g * cg, (g + 1) * cg)
            ddg = dd_ref[:, cols].astype(F32)
            head = head_ref[:, cols].astype(F32)
            head = jnp.where(i < nt - 1, head, jnp.zeros_like(head))
            cnt = jnp.minimum(tglob + 1, w).astype(F32)
            ext = (jnp.concatenate([ddg, head], axis=0) / cnt).astype(MXU)
            band = jnp.where((off >= 0) & (off < w), 1.0, 0.0).astype(MXU)
            dp_ref[:, cols] = (jnp.dot(band, ext, preferred_element_type=F32) - ddg).astype(dp_ref.dtype)
        dp_ref[:, W:] = dz_ref[...]

    return pl.pallas_call(
        body, name=name, grid=(nt,),
        in_specs=[pl.BlockSpec((tm, W), lambda i: (i, 0)),
                  pl.BlockSpec((POOL_HALO, W), lambda i: (jnp.minimum((i + 1) * hb, M // POOL_HALO - 1), 0)),
                  pl.BlockSpec((tm, W), lambda i: (i, 0))],
        out_specs=pl.BlockSpec((tm, 2 * W), lambda i: (i, 0)),
        out_shape=jax.ShapeDtypeStruct((M, 2 * W), ACT),
        compiler_params=_cp("parallel"),
    )(dd, dd, dz)


def _rope_tables(S):
    half = ROPE_DIM // 2
    inv_freq = jnp.power(jnp.float32(ROPE_THETA), -jnp.arange(half, dtype=F32) / half)
    ang = jnp.arange(S, dtype=F32)[:, None] * inv_freq[None, :]
    cos, sin = jnp.cos(ang), jnp.sin(ang)
    rest = HEAD_DIM - ROPE_DIM
    cf = jnp.concatenate([cos, cos, jnp.ones((S, rest), F32)], axis=1)
    sf = jnp.concatenate([-sin, sin, jnp.zeros((S, rest), F32)], axis=1)
    return cf, sf


def _swap_matrix():
    half = ROPE_DIM // 2
    a = lax.broadcasted_iota(jnp.int32, (HEAD_DIM, HEAD_DIM), 0)
    e = lax.broadcasted_iota(jnp.int32, (HEAD_DIM, HEAD_DIM), 1)
    hit = ((e < half) & (a == e + half)) | ((e >= half) & (e < 2 * half) & (a == e - half))
    return jnp.where(hit, 1.0, 0.0).astype(MXU)


def _b_qk_fwd(proj, tables, gains, name, comm=None):
    M = proj.shape[0]
    nsl = 2 * len(B_DILATIONS) * B_HEADS
    Wqk = nsl * HEAD_DIM
    tm = min(M, 256)

    def body(p_ref, cf_ref, sf_ref, g_ref, o_ref):
        cf, sf = cf_ref[...], sf_ref[...]
        swap = _swap_matrix()
        for j in range(nsl):
            cols = slice(j * HEAD_DIM, (j + 1) * HEAD_DIM)
            xv = p_ref[:, cols].astype(F32)
            r = lax.rsqrt(jnp.mean(xv * xv, axis=-1, keepdims=True) + EPS)
            xg = xv * g_ref[j // B_HEADS:j // B_HEADS + 1, :]
            hi = xg.astype(MXU)
            lo = (xg - hi.astype(F32)).astype(MXU)
            sw = jnp.dot(hi, swap, preferred_element_type=F32) + jnp.dot(lo, swap, preferred_element_type=F32)
            o_ref[:, cols] = (r * (xg * cf + sw * sf)).astype(o_ref.dtype)

    tspec = pl.BlockSpec((tm, HEAD_DIM), lambda i: (i, 0))
    return _launch(
        body, name=name, grid=(M // tm,),
        in_specs=[pl.BlockSpec((tm, Wqk), lambda i: (i, 0)), tspec, tspec,
                  pl.BlockSpec((8, HEAD_DIM), lambda i: (0, 0))],
        out_specs=[pl.BlockSpec((tm, Wqk), lambda i: (i, 0))],
        out_shape=[jax.ShapeDtypeStruct((M, Wqk), ACT)],
        args=(proj, *tables, gains), sem=("parallel",), comm=comm)


def _b_qk_bwd(dqs, dks, proj, tables, gains, dproj, name):
    M = proj.shape[0]
    ngr = len(B_DILATIONS)
    nsl = 2 * ngr * B_HEADS
    Wqk = nsl * HEAD_DIM
    Wg = B_HEADS * HEAD_DIM
    tm = min(M, 512)

    def body(*refs):
        d_refs = refs[:2 * ngr]
        p_ref, cf_ref, sf_ref, g_ref = refs[2 * ngr:2 * ngr + 4]
        dp_ref, dg_ref = refs[-2], refs[-1]

        @pl.when(pl.program_id(0) == 0)
        def _():
            dg_ref[...] = jnp.zeros_like(dg_ref)

        cf, sf = cf_ref[...], sf_ref[...]
        swap = _swap_matrix()
        for j in range(nsl):
            t, hh = j // B_HEADS, j % B_HEADS
            cols = slice(j * HEAD_DIM, (j + 1) * HEAD_DIM)
            dy = d_refs[t][:, hh * HEAD_DIM:(hh + 1) * HEAD_DIM].astype(F32)
            dxn = dy * cf + jnp.dot((dy * sf).astype(MXU), swap, preferred_element_type=F32)
            xv = p_ref[:, cols].astype(F32)
            r = lax.rsqrt(jnp.mean(xv * xv, axis=-1, keepdims=True) + EPS)
            xh = xv * r
            dg_ref[t:t + 1, :] += jnp.sum(dxn * xh, axis=0, keepdims=True)
            dxh = dxn * g_ref[t:t + 1, :]
            dp_ref[:, cols] = (r * (dxh - xh * jnp.mean(dxh * xh, axis=-1, keepdims=True))).astype(dp_ref.dtype)

    tspec = pl.BlockSpec((tm, HEAD_DIM), lambda i: (i, 0))
    dspec = pl.BlockSpec((tm, Wg), lambda i: (i, 0))
    n_in = 2 * ngr + 5
    return pl.pallas_call(
        body, name=name, grid=(M // tm,),
        in_specs=[dspec] * (2 * ngr) + [pl.BlockSpec((tm, Wqk), lambda i: (i, 0)), tspec, tspec,
                                        pl.BlockSpec((8, HEAD_DIM), lambda i: (0, 0)),
                                        pl.BlockSpec(memory_space=pl.ANY)],
        out_specs=[pl.BlockSpec((tm, Wqk), lambda i: (i, 0)), pl.BlockSpec((8, HEAD_DIM), lambda i: (0, 0))],
        out_shape=[jax.ShapeDtypeStruct(dproj.shape, dproj.dtype), jax.ShapeDtypeStruct((8, HEAD_DIM), F32)],
        input_output_aliases={n_in - 1: 0},
        compiler_params=_cp("arbitrary"),
    )(*dqs, *dks, proj, *tables, gains, dproj)


def _attn_tile(D, M):
    return max(HEAD_DIM * D, min(M, 2048))


class _TokenRows:
    GROUP = 16

    def __init__(self, D):
        self.D = D
        self.pitch = 24 if D == 16 else self.GROUP
        self.operand_dtype = F32 if D > 1 else ACT

    def rows(self, ntok):
        return ntok // self.GROUP * self.pitch

    def every_dth(self, tok0, n):
        start = tok0 // self.GROUP * self.pitch + tok0 % self.GROUP
        stride = self.D * self.pitch // self.GROUP
        return pl.ds(start, n) if stride == 1 else pl.ds(start, n, stride=stride)

    def put(self, dst, tok0, src_ref, ntok):
        if self.pitch == self.GROUP:
            dst[tok0:tok0 + ntok, :] = src_ref[...].astype(dst.dtype)
            return

        def group(i, carry):
            row = pl.multiple_of((tok0 // self.GROUP + i) * self.pitch, 8)
            dst[pl.ds(row, self.GROUP), :] = src_ref[pl.ds(pl.multiple_of(i * self.GROUP, self.GROUP), self.GROUP), :].astype(F32)
            return carry

        lax.fori_loop(0, ntok // self.GROUP, group, 0, unroll=8)

    def get(self, dst_ref, src, ntok):
        if self.pitch == self.GROUP:
            dst_ref[...] = src[0:ntok, :].astype(dst_ref.dtype)
            return

        def group(i, carry):
            row = pl.multiple_of(i * self.pitch, 8)
            dst_ref[pl.ds(pl.multiple_of(i * self.GROUP, self.GROUP), self.GROUP), :] = src[pl.ds(row, self.GROUP), :].astype(dst_ref.dtype)
            return carry

        lax.fori_loop(0, ntok // self.GROUP, group, 0, unroll=8)


def _attn_mask(base):
    qi = lax.broadcasted_iota(jnp.int32, (CHUNK, 2 * CHUNK), 0)
    ki = lax.broadcasted_iota(jnp.int32, (CHUNK, 2 * CHUNK), 1)
    return (ki >= qi) & (ki <= qi + CHUNK) & (ki >= CHUNK - base)


def _b_attn_fwd(qk, proj, g, name):
    M = qk.shape[0]
    D = B_DILATIONS[g]
    ngr = len(B_DILATIONS)
    T = _attn_tile(D, M)
    P = HEAD_DIM * D
    nsb = T // P
    Wg = B_HEADS * HEAD_DIM
    scale = np.float32(1.0 / np.sqrt(HEAD_DIM))

    lay = _TokenRows(D)
    RP, RT = lay.rows(P), lay.rows(T)

    def body(q_ref, k_ref, v_ref, o_ref, l_ref, qs, ks, vs, os_):
        n = pl.program_id(1)

        @pl.when(n == 0)
        def _():
            ks[0:RP, :] = jnp.zeros((RP, HEAD_DIM), ks.dtype)
            vs[0:RP, :] = jnp.zeros((RP, HEAD_DIM), vs.dtype)

        lay.put(qs, 0, q_ref, T)
        lay.put(ks, P, k_ref, T)
        lay.put(vs, P, v_ref, T)

        for b in range(nsb):
            mask = _attn_mask(n * (T // D) + b * CHUNK)
            for r in range(D):
                start = b * P + r
                q = qs[lay.every_dth(start, CHUNK), :]
                k = ks[lay.every_dth(start, 2 * CHUNK), :]
                v = vs[lay.every_dth(start, 2 * CHUNK), :]
                s = jnp.where(mask, _dot_nt(q, k) * scale, NEG)
                m = jnp.max(s, axis=-1, keepdims=True)
                p = jnp.exp(s - m)
                l = jnp.sum(p, axis=-1, keepdims=True)
                o = _dot(p, v) / l
                os_[lay.every_dth(start, CHUNK), :] = o
                l_ref[:, b * D + r:b * D + r + 1] = m + jnp.log(l)

        lay.get(o_ref, os_, T)
        ks[0:RP, :] = ks[RT:RT + RP, :]
        vs[0:RP, :] = vs[RT:RT + RP, :]

    blk = (T, HEAD_DIM)
    U = nsb * D
    return pl.pallas_call(
        body, name=name, grid=(B_HEADS, M // T),
        in_specs=[pl.BlockSpec(blk, lambda h, n: (n, g * B_HEADS + h)),
                  pl.BlockSpec(blk, lambda h, n: (n, (ngr + g) * B_HEADS + h)),
                  pl.BlockSpec(blk, lambda h, n: (n, (2 * ngr + g) * B_HEADS + h))],
        out_specs=[pl.BlockSpec(blk, lambda h, n: (n, h)), pl.BlockSpec((None, CHUNK, U), lambda h, n: (h, n, 0))],
        out_shape=[jax.ShapeDtypeStruct((M, Wg), ACT), jax.ShapeDtypeStruct((B_HEADS, (M // T) * CHUNK, U), F32)],
        scratch_shapes=[pltpu.VMEM((RT, HEAD_DIM), lay.operand_dtype), pltpu.VMEM((RP + RT, HEAD_DIM), lay.operand_dtype),
                        pltpu.VMEM((RP + RT, HEAD_DIM), lay.operand_dtype), pltpu.VMEM((RT, HEAD_DIM), F32)],
        compiler_params=_cp("parallel", "arbitrary"),
    )(qk, qk, proj)


def _units_to_tokens(a, D, T):
    H = a.shape[0]
    nsb = T // (HEAD_DIM * D)
    return a.reshape(H, -1, CHUNK, nsb, D).transpose(1, 3, 2, 4, 0).reshape(-1, H)


def _tokens_to_units(a, D, T):
    M, H = a.shape
    nsb = T // (HEAD_DIM * D)
    return a.reshape(M // T, nsb, CHUNK, D, H).transpose(4, 0, 2, 1, 3).reshape(H, (M // T) * CHUNK, nsb * D)


def _b_combine(os_, ls, proj, name):
    M, Wg = os_[0].shape
    ngr = len(B_DILATIONS)
    tm = min(M, 512)

    def body(*refs):
        o_refs, l_refs, z_ref = refs[:ngr], refs[ngr:2 * ngr], refs[2 * ngr]
        y_ref, o_ref, lse_ref = refs[2 * ngr + 1:]
        for h in range(B_HEADS):
            cols = slice(h * HEAD_DIM, (h + 1) * HEAD_DIM)
            ls_ = [r[:, h:h + 1] for r in l_refs]
            m = functools.reduce(jnp.maximum, ls_)
            es = [jnp.exp(l - m) for l in ls_]
            tot = functools.reduce(lambda a, b: a + b, es)
            o = functools.reduce(lambda a, b: a + b, [(e / tot) * r[:, cols].astype(F32) for e, r in zip(es, o_refs)])
            z = z_ref[:, cols].astype(F32)
            y_ref[:, cols] = (o * (z * _sigmoid(z))).astype(y_ref.dtype)
            o_ref[:, cols] = o.astype(o_ref.dtype)
            lse_ref[:, h:h + 1] = m + jnp.log(tot)

    spec = pl.BlockSpec((tm, Wg), lambda i: (i, 0))
    hspec = pl.BlockSpec((tm, B_HEADS), lambda i: (i, 0))
    return pl.pallas_call(
        body, name=name, grid=(M // tm,),
        in_specs=[spec] * ngr + [hspec] * ngr + [pl.BlockSpec((tm, Wg), lambda i: (i, 3 * ngr))],
        out_specs=[spec, spec, hspec],
        out_shape=[jax.ShapeDtypeStruct((M, Wg), ACT), jax.ShapeDtypeStruct((M, Wg), ACT),
                   jax.ShapeDtypeStruct((M, B_HEADS), F32)],
        compiler_params=_cp("parallel"),
    )(*os_, *ls, proj)


def _b_bwd_pre(dout, w_out, o, proj, name):
    M, Wg = o.shape
    Dm = dout.shape[1]
    ngr = len(B_DILATIONS)
    tm = min(M, 512)

    def body(do_ref, wo_ref, o_ref, z_ref, dov_ref, dl_ref, dp_ref):
        dy = _dot_nt(do_ref[...], wo_ref[...])
        z = z_ref[...].astype(F32)
        sig = _sigmoid(z)
        ov = o_ref[...].astype(F32)
        dp_ref[...] = (dy * ov * (sig * (1.0 + z * (1.0 - sig)))).astype(dp_ref.dtype)
        dov = dy * (z * sig)
        dov_ref[...] = dov.astype(dov_ref.dtype)
        prod = dov * ov
        for h in range(B_HEADS):
            dl_ref[:, h:h + 1] = jnp.sum(prod[:, h * HEAD_DIM:(h + 1) * HEAD_DIM], axis=-1, keepdims=True)

    spec = pl.BlockSpec((tm, Wg), lambda i: (i, 0))
    zspec = pl.BlockSpec((tm, Wg), lambda i: (i, 3 * ngr))
    return pl.pallas_call(
        body, name=name, grid=(M // tm,),
        in_specs=[pl.BlockSpec((tm, Dm), lambda i: (i, 0)), pl.BlockSpec((Wg, Dm), lambda i: (0, 0)), spec, zspec],
        out_specs=[spec, pl.BlockSpec((tm, B_HEADS), lambda i: (i, 0)), zspec],
        out_shape=[jax.ShapeDtypeStruct((M, Wg), ACT), jax.ShapeDtypeStruct((M, B_HEADS), F32),
                   jax.ShapeDtypeStruct(proj.shape, ACT)],
        compiler_params=_cp("parallel"),
    )(dout, w_out, o, proj)


def _b_attn_bwd(qk, proj, dov, lse, delta, dproj, g, name, comm=None):
    M = qk.shape[0]
    D = B_DILATIONS[g]
    ngr = len(B_DILATIONS)
    T = _attn_tile(D, M)
    P = HEAD_DIM * D
    nsb = T // P
    nt = M // T
    Wg = B_HEADS * HEAD_DIM
    scale = np.float32(1.0 / np.sqrt(HEAD_DIM))
    shift = T - P
    lay = _TokenRows(D)
    RP, RT = lay.rows(P), lay.rows(T)

    def body(q_ref, k_ref, v_ref, do_ref, l_ref, dl_ref, dp_any, dq_ref, dk_ref, dv_ref,
             qs, dos, ks, vs, dqs, dks, dvs):
        n = pl.program_id(1)

        @pl.when(n == 0)
        def _():
            ks[0:RP, :] = jnp.zeros((RP, HEAD_DIM), ks.dtype)
            vs[0:RP, :] = jnp.zeros((RP, HEAD_DIM), vs.dtype)
            dks[...] = jnp.zeros((2 * RT, HEAD_DIM), F32)
            dvs[...] = jnp.zeros((2 * RT, HEAD_DIM), F32)

        @pl.when(n < nt)
        def _():
            lay.put(qs, 0, q_ref, T)
            lay.put(dos, 0, do_ref, T)
            lay.put(ks, P, k_ref, T)
            lay.put(vs, P, v_ref, T)

            masks = [_attn_mask(n * (T // D) + b * CHUNK) for b in range(nsb)]
            for r in range(D):
                carry_dv = carry_dk = None
                for b in range(nsb):
                    start = b * P + r
                    qsl = lay.every_dth(start, CHUNK)
                    ksl = lay.every_dth(start, 2 * CHUNK)
                    lo = lay.every_dth(start + shift, CHUNK)
                    q = qs[qsl, :]
                    do = dos[qsl, :]
                    k = ks[ksl, :]
                    v = vs[ksl, :]
                    s = _dot_nt(q, k) * scale
                    u = b * D + r
                    p = jnp.where(masks[b], jnp.exp(s - l_ref[:, u:u + 1]), 0.0)
                    dv = _dot_tn(p, do)
                    dp = _dot_nt(do, v)
                    ds = (p * (dp - dl_ref[:, u:u + 1]) * scale).astype(MXU)
                    dqs[qsl, :] = _dot(ds, k)
                    dk = _dot_tn(ds, q)
                    if b == 0:
                        dvs[lo, :] += dv[:CHUNK]
                        dks[lo, :] += dk[:CHUNK]
                    else:
                        dvs[lo, :] = carry_dv + dv[:CHUNK]
                        dks[lo, :] = carry_dk + dk[:CHUNK]
                    carry_dv, carry_dk = dv[CHUNK:], dk[CHUNK:]
                hi = lay.every_dth((nsb - 1) * P + r + shift + P, CHUNK)
                dvs[hi, :] = carry_dv
                dks[hi, :] = carry_dk

        lay.get(dq_ref, dqs, T)
        lay.get(dk_ref, dks, T)
        lay.get(dv_ref, dvs, T)
        dks[0:RT, :] = dks[RT:2 * RT, :]
        dvs[0:RT, :] = dvs[RT:2 * RT, :]
        ks[0:RP, :] = ks[RT:RT + RP, :]
        vs[0:RP, :] = vs[RT:RT + RP, :]

    blk = (T, HEAD_DIM)
    cur = lambda n: jnp.minimum(n, nt - 1)
    prv = lambda n: jnp.maximum(n - 1, 0)
    return _launch(
        body, name=name, grid=(B_HEADS, nt + 1),
        in_specs=[pl.BlockSpec(blk, lambda h, n: (cur(n), g * B_HEADS + h)),
                  pl.BlockSpec(blk, lambda h, n: (cur(n), (ngr + g) * B_HEADS + h)),
                  pl.BlockSpec(blk, lambda h, n: (cur(n), (2 * ngr + g) * B_HEADS + h)),
                  pl.BlockSpec(blk, lambda h, n: (cur(n), h)),
                  pl.BlockSpec((None, CHUNK, nsb * D), lambda h, n: (h, cur(n), 0)),
                  pl.BlockSpec((None, CHUNK, nsb * D), lambda h, n: (h, cur(n), 0)),
                  pl.BlockSpec(memory_space=pl.ANY)],
        out_specs=[pl.BlockSpec(blk, lambda h, n: (cur(n), h)),
                   pl.BlockSpec(blk, lambda h, n: (prv(n), h)),
                   pl.BlockSpec(blk, lambda h, n: (prv(n), (2 * ngr + g) * B_HEADS + h))],
        out_shape=[jax.ShapeDtypeStruct((M, Wg), ACT), jax.ShapeDtypeStruct((M, Wg), ACT),
                   jax.ShapeDtypeStruct(dproj.shape, dproj.dtype)],
        scratch=[pltpu.VMEM((RT, HEAD_DIM), lay.operand_dtype)] * 2
        + [pltpu.VMEM((RP + RT, HEAD_DIM), lay.operand_dtype)] * 2
        + [pltpu.VMEM((RT, HEAD_DIM), F32)]
        + [pltpu.VMEM((2 * RT, HEAD_DIM), F32)] * 2,
        aliases={6: 2},
        args=(qk, qk, proj, dov, lse, delta, dproj), sem=("parallel", "arbitrary"), comm=comm)


def _coords():
    return lax.axis_index("x"), lax.axis_index("y"), lax.axis_index("c")


def _gather_blocks(x_refs, out_refs, send_sems, recv_sems, local_sems):
    x, y, c = _coords()
    me, sibling = (x, y, c), (x, y, 1 - c)
    chips = [(1 - x, y), (x, 1 - y), (1 - x, 1 - y)]
    arrays = range(len(x_refs))

    def slot(a, px, py, pc):
        return out_refs[a].at[4 * px + 2 * py + pc]

    def copy(a, k, block, to, src=None):
        return _remote(slot(a, *block) if src is None else src, slot(a, *block), send_sems, recv_sems, 7 * a + k, to)

    mine = [pltpu.make_async_copy(x_refs[a], slot(a, *me), local_sems.at[a]) for a in arrays]
    first = [copy(a, 0, me, sibling, src=x_refs[a]) for a in arrays]
    first += [copy(a, 1 + j, me, (*chip, c), src=x_refs[a]) for j, chip in enumerate(chips) for a in arrays]
    for cp in mine + first:
        cp.start()
    passed = []
    for j, chip in enumerate(chips):
        for a in arrays:
            copy(a, 1 + j, (*chip, c), me).wait_recv()
            passed.append(copy(a, 4 + j, (*chip, c), sibling))
            passed[-1].start()
    for a in arrays:
        copy(a, 0, sibling, me).wait_recv()
        for j, chip in enumerate(chips):
            copy(a, 4 + j, (*chip, 1 - c), me).wait_recv()
    for cp in first + passed:
        cp.wait_send()
    for cp in mine:
        cp.wait()


def _all_gather_hbm(arrays, name):
    n = len(arrays)

    def body(*refs):
        _gather_blocks(refs[:n], refs[n:2 * n], *refs[2 * n:])

    return pl.pallas_call(
        body, name=name, in_specs=[_HBM] * n, out_specs=[_HBM] * n,
        out_shape=[jax.ShapeDtypeStruct((N_DEV,) + a.shape, a.dtype) for a in arrays],
        scratch_shapes=[pltpu.SemaphoreType.DMA((7 * n,)), pltpu.SemaphoreType.DMA((7 * n,)),
                        pltpu.SemaphoreType.DMA((n,))],
    )(*arrays)


def _all_reduce_small(part):
    R, C = part.shape

    def body(x_ref, tot_ref, gath, send_sems, recv_sems, local_sems):
        _gather_blocks([x_ref], [gath], send_sems, recv_sems, local_sems)
        acc = gath[0]
        for d in range(1, N_DEV):
            acc = acc + gath[d]
        tot_ref[...] = acc

    return pl.pallas_call(
        body, name="ar_small",
        in_specs=[pl.BlockSpec(memory_space=pltpu.VMEM)],
        out_specs=pl.BlockSpec(memory_space=pltpu.VMEM),
        out_shape=jax.ShapeDtypeStruct((R, C), F32),
        scratch_shapes=[pltpu.VMEM((N_DEV, R, C), F32),
                        pltpu.SemaphoreType.DMA((7,)), pltpu.SemaphoreType.DMA((7,)), pltpu.SemaphoreType.DMA((1,))],
        compiler_params=pltpu.CompilerParams(vmem_limit_bytes=VMEM_LIMIT),
    )(part)


def _sum_blocks(gath, name):
    _, R, C = gath.shape

    def body(g_ref, o_ref):
        acc = g_ref[0]
        for d in range(1, N_DEV):
            acc = acc + g_ref[d]
        o_ref[...] = acc

    return pl.pallas_call(
        body, name=name,
        in_specs=[pl.BlockSpec(memory_space=pltpu.VMEM)], out_specs=pl.BlockSpec(memory_space=pltpu.VMEM),
        out_shape=jax.ShapeDtypeStruct((R, C), F32),
        compiler_params=pltpu.CompilerParams(vmem_limit_bytes=VMEM_LIMIT),
    )(gath)


def _remote(src, dst, send_sems, recv_sems, k, peer):
    return pltpu.make_async_remote_copy(src_ref=src, dst_ref=dst, send_sem=send_sems.at[k], recv_sem=recv_sems.at[k],
                                        device_id=peer, device_id_type=MESH)


def _ag_send(arrays):
    n = len(arrays)

    def make(c_in, c_out, send_sems, recv_sems, local_sems):
        x, y, c = _coords()
        peers = [(x, y, 1 - c), (1 - x, y, c), (x, 1 - y, c), (1 - x, 1 - y, c)]
        cps = []
        for a in range(n):
            src, dst = c_in[a], c_out[a].at[4 * x + 2 * y + c]
            cps.append(pltpu.make_async_copy(src, dst, local_sems.at[a]))
            cps += [_remote(src, dst, send_sems, recv_sems, 4 * a + k, peer) for k, peer in enumerate(peers)]
        return cps

    return _Comm(arrays, [jax.ShapeDtypeStruct((N_DEV,) + a.shape, a.dtype) for a in arrays], 4 * n, make, n_local=n)


def _ag_forward(gaths):
    n = len(gaths)

    def make(c_in, c_out, send_sems, recv_sems, local_sems):
        x, y, c = _coords()
        chips = [(1 - x, y), (x, 1 - y), (1 - x, 1 - y)]
        cps = []
        for a in range(n):
            buf = c_out[a]
            cps += [_remote(buf.at[4 * px + 2 * py + c], buf.at[4 * px + 2 * py + c], send_sems, recv_sems, 3 * a + j,
                            (x, y, 1 - c)) for j, (px, py) in enumerate(chips)]
        return cps

    return _Comm(gaths, [jax.ShapeDtypeStruct(g.shape, g.dtype) for g in gaths], 3 * n, make,
                 aliases={a: a for a in range(n)})


def _rs_sibling(grads):
    n = len(grads)

    def make(c_in, c_out, send_sems, recv_sems, local_sem):
        x, y, c = _coords()
        return [_remote(c_in[a].at[pl.ds(4 * (1 - c), 4)], c_out[a], send_sems, recv_sems, a, (x, y, 1 - c))
                for a in range(n)]

    return _Comm(grads, [jax.ShapeDtypeStruct((4,) + g.shape[1:], g.dtype) for g in grads], n, make)


def _rs_chips(parts):
    n = len(parts)

    def make(c_in, c_out, send_sems, recv_sems, local_sem):
        x, y, c = _coords()
        peers = [(x, 1 - y, c), (1 - x, y, c), (1 - x, 1 - y, c)]
        return [_remote(c_in[a].at[k], c_out[a].at[k], send_sems, recv_sems, 3 * a + k, peer)
                for a in range(n) for k, peer in enumerate(peers)]

    return _Comm(parts, [jax.ShapeDtypeStruct(p.shape, p.dtype) for p in parts], 3 * n, make)


def _row_tile(rows, cols):
    tr = min(rows, 1 << int(np.log2((1 << 18) // cols)))
    assert rows % tr == 0
    return tr


def _chip_partials(coords, g, r1, name):
    _, rows, C = g.shape
    tr = _row_tile(rows, C)

    def body(co_ref, g_ref, r_ref, o_ref):
        o_ref[...] = (g_ref[...] + r_ref[...]).astype(o_ref.dtype)

    def chip(k, co):
        return jnp.bitwise_xor(2 * co[0] + co[1], k + 1)

    return pl.pallas_call(
        body, name=name,
        grid_spec=pltpu.PrefetchScalarGridSpec(
            num_scalar_prefetch=1, grid=(3, rows // tr),
            in_specs=[pl.BlockSpec((None, tr, C), lambda k, t, co: (4 * co[2] + chip(k, co), t, 0)),
                      pl.BlockSpec((None, tr, C), lambda k, t, co: (chip(k, co), t, 0))],
            out_specs=pl.BlockSpec((None, tr, C), lambda k, t, co: (k, t, 0))),
        out_shape=jax.ShapeDtypeStruct((3, rows, C), WIRE),
        compiler_params=_cp("parallel", "parallel"),
    )(coords, g, r1)


def _adam_math(w, g, m, v):
    m = ADAM_B1 * m + (1.0 - ADAM_B1) * g
    v = ADAM_B2 * v + (1.0 - ADAM_B2) * (g * g)
    m_hat = m / (1.0 - ADAM_B1 ** ADAM_STEP)
    v_hat = v / (1.0 - ADAM_B2 ** ADAM_STEP)
    delta = -ADAM_LR * (m_hat / (jnp.sqrt(v_hat) + ADAM_EPS) + ADAM_WD * w)
    return delta, m, v


def _adamw_sharded(coords, w, m, v, g, r1, r2, layer, prev, name):
    L, rows, C = w.shape
    tr = _row_tile(rows, C)
    n_prev = 0 if prev is None else len(prev)

    def body(co_ref, w_ref, m_ref, v_ref, g_ref, r1_ref, r2_ref, *rest):
        go_ref, d_ref, mo_ref, vo_ref = rest[n_prev:]
        grad = g_ref[...] + r1_ref[...]
        for k in range(3):
            grad = grad + r2_ref[k].astype(F32)
        go_ref[...] = grad
        d_ref[...], mo_ref[...], vo_ref[...] = _adam_math(w_ref[...], grad, m_ref[...], v_ref[...])

    spec = pl.BlockSpec((None, tr, C), lambda t, co: (layer, t, 0))
    return pl.pallas_call(
        body, name=name,
        grid_spec=pltpu.PrefetchScalarGridSpec(
            num_scalar_prefetch=1, grid=(rows // tr,),
            in_specs=[spec, spec, spec,
                      pl.BlockSpec((None, tr, C), lambda t, co: (4 * co[2] + 2 * co[0] + co[1], t, 0)),
                      pl.BlockSpec((None, tr, C), lambda t, co: (2 * co[0] + co[1], t, 0)),
                      pl.BlockSpec((3, tr, C), lambda t, co: (0, t, 0))] + [_HBM] * n_prev,
            out_specs=[spec] * 4),
        out_shape=[jax.ShapeDtypeStruct((L, rows, C), F32)] * 4,
        input_output_aliases={7 + k: k for k in range(n_prev)},
        compiler_params=_cp("parallel"),
    )(coords, w, m, v, g, r1, r2, *(prev or []))


def _adamw_small(w, g, m, v, name):
    def body(w_ref, g_ref, m_ref, v_ref, d_ref, mo_ref, vo_ref):
        d_ref[...], mo_ref[...], vo_ref[...] = _adam_math(w_ref[...], g_ref[...], m_ref[...], v_ref[...])

    return pl.pallas_call(
        body, name=name, out_shape=[jax.ShapeDtypeStruct(w.shape, F32)] * 3,
        in_specs=[pl.BlockSpec(memory_space=pltpu.VMEM)] * 4,
        out_specs=[pl.BlockSpec(memory_space=pltpu.VMEM)] * 3,
    )(w, g, m, v)


def _reduce_scatter_adds(coords, grads, r1s, tag):
    return [_chip_partials(coords, g, r, f"rs_add_{tag}{i}") for i, (g, r) in enumerate(zip(grads, r1s))]


def kernel(x, norm_gain, a_w_in, a_v_gain, a_w_s, a_b_s, a_w_out, b_w_in, b_q_gain, b_k_gain, b_w_out, c_w_in, c_w_grp, c_scale, c_w_out, loss_target, m_norm_gain, m_a_w_in, m_a_v_gain, m_a_w_s, m_a_b_s, m_a_w_out, m_b_w_in, m_b_q_gain, m_b_k_gain, m_b_w_out, m_c_w_in, m_c_w_grp, m_c_scale, m_c_w_out, v_norm_gain, v_a_w_in, v_a_v_gain, v_a_w_s, v_a_b_s, v_a_w_out, v_b_w_in, v_b_q_gain, v_b_k_gain, v_b_w_out, v_c_w_in, v_c_w_grp, v_c_scale, v_c_w_out):
    cx, cy, cc = _coords()
    coords = jnp.stack([cx, cy, cc]).astype(jnp.int32)
    dev = 4 * cx + 2 * cy + cc
    Dm = x.shape[2]

    xs, tgt = x[0], loss_target[0]
    tables = _rope_tables(xs.shape[0])
    ng = lambda i: norm_gain[i:i + 1]
    ngr = len(B_DILATIONS)
    bst = [a_b_s[l].T for l in range(2)]
    b_gains = jnp.concatenate([b_q_gain[0], b_k_gain[0], jnp.zeros((2, HEAD_DIM), F32)], axis=0)
    ngp, rlc, cgc = c_w_grp.shape[1:]
    wire = lambda w: w.astype(WIRE)

    nvg, nsc = a_v_gain.size, c_scale.size
    vec = jnp.concatenate([a_v_gain.reshape(-1), c_scale.reshape(-1), jnp.zeros((1024 - nvg - nsc,), F32)]).reshape(8, 128)
    wa_in0, wa_out0, vecs = _all_gather_hbm([wire(a_w_in[0]), wire(a_w_out[0]), vec], "ag_layer0")
    wa_out0 = wa_out0.reshape(-1, Dm)
    vecs = vecs.reshape(N_DEV, -1)
    a_vg = vecs[:, :nvg].reshape((N_DEV,) + a_v_gain.shape).transpose(1, 0, 2).reshape(a_v_gain.shape[0], -1)
    c_sc = vecs[:, nvg:nvg + nsc].reshape(1, -1)

    h0, p0, *g1 = _norm_proj(xs, ng(0), wa_in0, "l0_proj", comm=_ag_send([wire(b_w_in[0]), wire(b_w_out[0])]))
    y0, wb_in, wb_out = _a_mid(p0, a_vg[0:1], a_w_s[0], bst[0], "l0_mid", comm=_ag_forward(g1))
    x1 = _out_proj(xs, y0, wa_out0, "l0_out")
    wb_out = wb_out.reshape(-1, Dm)

    later = [wire(c_w_in[0]), wire(c_w_grp[0]), wire(c_w_out[0]), wire(a_w_in[1]), wire(a_w_out[1])]
    h1, p1, *g2 = _norm_proj(x1, ng(1), wb_in, "l1_proj", comm=_ag_send(later))
    qk, wc_in, wc_grp, wc_out, wa_in1, wa_out1 = _b_qk_fwd(p1, tables, b_gains, "l1_qk", comm=_ag_forward(g2))
    ogs, lgs = zip(*[_b_attn_fwd(qk, p1, g, f"l1_attn{g}") for g in range(ngr)])
    tiles = [_attn_tile(D, xs.shape[0]) for D in B_DILATIONS]
    lgs = [_units_to_tokens(l, D, T) for l, D, T in zip(lgs, B_DILATIONS, tiles)]
    y1, o1, lse = _b_combine(ogs, lgs, p1, "l1_comb")
    x2 = _out_proj(x1, y1, wb_out, "l1_out")
    wc_grp = wc_grp.transpose(1, 0, 2, 3).reshape(ngp, N_DEV * rlc, cgc)
    wc_out = wc_out.reshape(-1, Dm)
    wa_out1 = wa_out1.reshape(-1, Dm)

    h2, p2 = _norm_proj(x2, ng(2), wc_in, "l2_proj")
    y2 = _c_mid(p2, wc_grp, c_sc, "l2_mid")
    x3 = _out_proj(x2, y2, wc_out, "l2_out")
    h3, p3 = _norm_proj(x3, ng(3), wa_in1, "l3_proj")
    y3, = _a_mid(p3, a_vg[1:2], a_w_s[1], bst[1], "l3_mid")
    loss_local, dx4, dx4a = _out_proj_loss(x3, y3, wa_out1, tgt, "l3_out_loss")

    flat3 = lambda g: g.reshape(N_DEV, -1, g.shape[-1])
    dp3, dws1, dbs1, dvg1 = _a_bwd(dx4a, wa_out1, p3, a_vg[1:2], a_w_s[1], bst[1], "l3_bwd")
    grads3 = [_dw_in(h3, dp3, "l3_dwin"), _dw_out(y3, dx4a, "l3_dwout")]
    dx3, dx3a, dg3, *r1_3 = _dh_norm_bwd(dp3, wa_in1, x3, ng(3), dx4, "l3_dh", comm=_rs_sibling(grads3))
    parts3 = _reduce_scatter_adds(coords, grads3, r1_3, "l3_")

    dd, dz, gc_grp, dsc, *r2_3 = _c_bwd1(dx3a, wc_out, p2, wc_grp, c_sc, "l2_bwd1", comm=_rs_chips(parts3))
    dp2 = _c_bwd2(dd, dz, "l2_bwd2")
    grads2 = [_dw_in(h2, dp2, "l2_dwin"), _dw_out(y2, dx3a, "l2_dwout"), flat3(gc_grp)]
    dx2, dx2a, dg2, *r1_2 = _dh_norm_bwd(dp2, wc_in, x2, ng(2), dx3, "l2_dh", comm=_rs_sibling(grads2))
    parts2 = _reduce_scatter_adds(coords, grads2, r1_2, "l2_")

    dov, delta, dp1 = _b_bwd_pre(dx2a, wb_out, o1, p1, "l1_bwdpre")
    dqs, dks, r2_2 = [], [], None
    for g in range(ngr):
        lse_u, delta_u = [_tokens_to_units(a, B_DILATIONS[g], tiles[g]) for a in (lse, delta)]
        dq, dk, dp1, *rest = _b_attn_bwd(qk, p1, dov, lse_u, delta_u, dp1, g, f"l1_attnbwd{g}",
                                         comm=_rs_chips(parts2) if g == 0 else None)
        if g == 0:
            r2_2 = rest
        dqs.append(dq)
        dks.append(dk)
    dp1, dgains = _b_qk_bwd(dqs, dks, p1, tables, b_gains, dp1, "l1_qkbwd")
    grads1 = [_dw_in(h1, dp1, "l1_dwin"), _dw_out(y1, dx2a, "l1_dwout")]
    dx1, dx1a, dg1, *r1_1 = _dh_norm_bwd(dp1, wb_in, x1, ng(1), dx2, "l1_dh", comm=_rs_sibling(grads1))
    parts1 = _reduce_scatter_adds(coords, grads1, r1_1, "l1_")

    dp0, dws0, dbs0, dvg0, *r2_1 = _a_bwd(dx1a, wa_out0, p0, a_vg[0:1], a_w_s[0], bst[0], "l0_bwd", comm=_rs_chips(parts1))
    small = dict(norm=jnp.concatenate([dg1, dg2, dg3], axis=0), a_ws=jnp.stack([dws0, dws1]),
                 a_bs=jnp.stack([dbs0.T, dbs1.T]), b_gains=dgains, a_vg=jnp.concatenate([dvg0, dvg1], axis=0), c_sc=dsc)
    order = ["norm", "a_ws", "a_bs", "b_gains", "a_vg", "c_sc"]
    rows = [small[k].reshape(-1, 128) for k in order]
    roff = np.cumsum([0] + [r.shape[0] for r in rows])
    gw_in0, gsmall = _dw_in(h0, dp0, "l0_dwin", comm=_ag_send([jnp.concatenate(rows, axis=0)]))
    gw_out0, gsmall = _dw_out(y0, dx1a, "l0_dwout", comm=_ag_forward([gsmall]))
    grads0 = [gw_in0, gw_out0]
    r1_0 = _run_comm(_rs_sibling(grads0), "l0_rs_sibling")
    parts0 = _reduce_scatter_adds(coords, grads0, r1_0, "l0_")
    dx0, _, dg0, *r2_0 = _dh_norm_bwd(dp0, wa_in0, xs, ng(0), dx1, "l0_dh", comm=_rs_chips(parts0))

    tot = _sum_blocks(gsmall, "small_sum")
    sm = {k: tot[int(roff[i]):int(roff[i + 1])].reshape(small[k].shape) for i, k in enumerate(order)}
    late = _all_reduce_small(jnp.concatenate([dg0.reshape(-1, 128), jnp.full((8, 128), loss_local, F32)], axis=0))
    sm["norm"] = jnp.concatenate([late[0:8].reshape(1, -1), sm["norm"]], axis=0)
    loss = late[8, 0]
    vl = a_v_gain.shape[1]
    g_small = dict(
        norm_gain=sm["norm"], a_w_s=sm["a_ws"], a_b_s=sm["a_bs"],
        b_q_gain=sm["b_gains"][None, 0:3], b_k_gain=sm["b_gains"][None, 3:6],
        a_v_gain=lax.dynamic_slice_in_dim(sm["a_vg"], dev * vl, vl, axis=1),
        c_scale=lax.dynamic_slice_in_dim(sm["c_sc"], dev * vl, vl, axis=1),
    )

    shares = dict(
        a_w_in=[(grads0[0], r1_0[0], r2_0[0]), (grads3[0], r1_3[0], r2_3[0])],
        a_w_out=[(grads0[1], r1_0[1], r2_0[1]), (grads3[1], r1_3[1], r2_3[1])],
        b_w_in=[(grads1[0], r1_1[0], r2_1[0])], b_w_out=[(grads1[1], r1_1[1], r2_1[1])],
        c_w_in=[(grads2[0], r1_2[0], r2_2[0])], c_w_out=[(grads2[1], r1_2[1], r2_2[1])],
        c_w_grp=[(grads2[2], r1_2[2], r2_2[2])])

    params = dict(a_w_in=a_w_in, a_w_out=a_w_out, b_w_in=b_w_in, b_w_out=b_w_out, c_w_in=c_w_in, c_w_grp=c_w_grp, c_w_out=c_w_out,
                  norm_gain=norm_gain, a_v_gain=a_v_gain, a_w_s=a_w_s, a_b_s=a_b_s, b_q_gain=b_q_gain, b_k_gain=b_k_gain, c_scale=c_scale)
    moms = dict(a_w_in=(m_a_w_in, v_a_w_in), a_w_out=(m_a_w_out, v_a_w_out), b_w_in=(m_b_w_in, v_b_w_in), b_w_out=(m_b_w_out, v_b_w_out),
                c_w_in=(m_c_w_in, v_c_w_in), c_w_grp=(m_c_w_grp, v_c_w_grp), c_w_out=(m_c_w_out, v_c_w_out),
                norm_gain=(m_norm_gain, v_norm_gain), a_v_gain=(m_a_v_gain, v_a_v_gain), a_w_s=(m_a_w_s, v_a_w_s),
                a_b_s=(m_a_b_s, v_a_b_s), b_q_gain=(m_b_q_gain, v_b_q_gain), b_k_gain=(m_b_k_gain, v_b_k_gain),
                c_scale=(m_c_scale, v_c_scale))
    grad, delta, new_m, new_v = {}, {}, {}, {}
    for pname, layers in shares.items():
        w, (m, v) = params[pname], moms[pname]
        as3 = lambda a: a.reshape(a.shape[0], -1, a.shape[-1])
        outs = None
        for l, (g, r1, r2) in enumerate(layers):
            outs = _adamw_sharded(coords, as3(w), as3(m), as3(v), g, r1, r2, l, outs, f"adamw_{pname}{l}")
        grad[pname], delta[pname], new_m[pname], new_v[pname] = [o.reshape(w.shape) for o in outs]
    for pname, g in g_small.items():
        w = params[pname]
        C = w.shape[-1]
        outs = _adamw_small(w.reshape(-1, C), g.reshape(-1, C), moms[pname][0].reshape(-1, C), moms[pname][1].reshape(-1, C),
                            f"adamw_{pname}")
        grad[pname] = g.reshape(w.shape)
        delta[pname], new_m[pname], new_v[pname] = [o.reshape(w.shape) for o in outs]

    wnames = ["norm_gain", "a_w_in", "a_v_gain", "a_w_s", "a_b_s", "a_w_out", "b_w_in", "b_q_gain", "b_k_gain", "b_w_out",
              "c_w_in", "c_w_grp", "c_scale", "c_w_out"]
    return (loss, dx0[None], *[grad[n] for n in wnames], *[delta[n] for n in wnames],
            *[new_m[n] for n in wnames], *[new_v[n] for n in wnames])
```
